```python
import jax, jax.numpy as jnp
from jax import lax
import numpy as np

D_MODEL = 2048
BATCH = 8
SEQ = 4096
DEPTH = 1

CHUNK = 64
RET_WIDTH = D_MODEL // 2
ATT_WIDTH = D_MODEL - RET_WIDTH
RET_HEADS = 8
ATT_HEADS = 8
RET_HEAD_DIM = RET_WIDTH // RET_HEADS
ATT_HEAD_DIM = ATT_WIDTH // ATT_HEADS
MIX_WIDTH = RET_WIDTH + ATT_WIDTH
IN_WIDTH = 4 * RET_WIDTH + 3 * ATT_WIDTH
LEFT_CHUNKS = 8
BAND = (LEFT_CHUNKS + 1) * CHUNK
REL_CLIP = 128
REL_SIZE = (CHUNK - 1) + REL_CLIP + 1
D_FF = 4 * D_MODEL
ROPE_BASE = 10000.0
EPS = 1e-6
GN_EPS = 1e-5

kernel_name = "hymba_retention_chunkattn_sqrelu_block"


def rms_norm(x, g):
    xf = x.astype(jnp.float32)
    y = xf * lax.rsqrt(jnp.mean(xf * xf, axis=-1, keepdims=True) + EPS)
    return (y * g.astype(jnp.float32)).astype(x.dtype)


def rotary(x, pos):
    half = x.shape[-1] // 2
    inv_freq = ROPE_BASE ** (-jnp.arange(half, dtype=jnp.float32) / half)
    ang = pos[:, None] * inv_freq[None, :]
    cos = jnp.cos(ang)[None, :, None, :]
    sin = jnp.sin(ang)[None, :, None, :]
    xf = x.astype(jnp.float32)
    x1, x2 = xf[..., :half], xf[..., half:]
    out = jnp.concatenate([x1 * cos - x2 * sin, x1 * sin + x2 * cos], axis=-1)
    return out.astype(x.dtype)


def retention(q, k, v):
    b, s, h, d = q.shape
    nc = s // CHUNK
    f32 = jnp.float32
    q = q.astype(f32).reshape(b, nc, CHUNK, h, d)
    k = k.astype(f32).reshape(b, nc, CHUNK, h, d) * (d ** -0.5)
    v = v.astype(f32).reshape(b, nc, CHUNK, h, d)
    log_g = jnp.log1p(-jnp.exp2(-(5.0 + jnp.arange(h, dtype=f32))))
    pos = jnp.arange(CHUNK, dtype=f32)
    intra_decay = jnp.exp(log_g[:, None, None] * jnp.abs(pos[:, None] - pos[None, :]))
    scores = jnp.einsum('bnihd,bnjhd->bnhij', q, k) * intra_decay
    intra = jnp.einsum('bnhij,bnjhe->bnihe', scores, v)
    k_dec = jnp.exp(log_g[None, :] * (CHUNK - 1.0 - pos)[:, None])
    contrib = jnp.einsum('bnjhd,bnjhe->nbhde', k * k_dec[:, :, None], v)
    chunk_dec = jnp.exp(log_g * CHUNK)[:, None, None]

    def step(state, c):
        return chunk_dec * state + c, state

    _, prev = lax.scan(step, jnp.zeros_like(contrib[0]), contrib)
    q_dec = jnp.exp(log_g[None, :] * (pos + 1.0)[:, None])
    cross = jnp.einsum('bnihd,nbhde->bnihe', q * q_dec[:, :, None], prev)
    return (intra + cross).reshape(b, s, h, d)


def head_group_norm(y, g):
    b, s, h, d = y.shape
    mu = jnp.mean(y, axis=-1, keepdims=True)
    var = jnp.mean(jnp.square(y - mu), axis=-1, keepdims=True)
    yn = (y - mu) * lax.rsqrt(var + GN_EPS) * g.astype(jnp.float32).reshape(h, d)
    return yn.reshape(b, s, h * d)


def chunk_attention(q, k, v, rel_bias):
    b, s, h, d = q.shape
    nc = s // CHUNK
    q = q.reshape(b, nc, CHUNK, h, d)
    pad = ((0, 0), (LEFT_CHUNKS, 0), (0, 0), (0, 0), (0, 0))
    kp = jnp.pad(k.reshape(b, nc, CHUNK, h, d), pad)
    vp = jnp.pad(v.reshape(b, nc, CHUNK, h, d), pad)
    band_idx = jnp.arange(nc)[:, None] + jnp.arange(LEFT_CHUNKS + 1)[None, :]
    kb = jnp.take(kp, band_idx, axis=1).reshape(b, nc, BAND, h, d)
    vb = jnp.take(vp, band_idx, axis=1).reshape(b, nc, BAND, h, d)
    scores = jnp.einsum('bnihd,bnjhd->bnhij', q, kb,
                        preferred_element_type=jnp.float32) * (d ** -0.5)
    qi = jnp.arange(CHUNK)[:, None]
    kj = jnp.arange(BAND)[None, :]
    rel = jnp.clip(qi + LEFT_CHUNKS * CHUNK - kj, -(CHUNK - 1), REL_CLIP) + (CHUNK - 1)
    bias = rel_bias.astype(jnp.float32)[:, rel]
    valid = (jnp.arange(nc)[:, None] - LEFT_CHUNKS + kj // CHUNK) >= 0
    scores = jnp.where(valid[None, :, None, None, :], scores + bias[None, None], -1e30)
    probs = jax.nn.softmax(scores, axis=-1).astype(v.dtype)
    out = jnp.einsum('bnhij,bnjhd->bnihd', probs, vb)
    return out.reshape(b, s, h * d)


def _fwd_setup_inputs(seed: int = 0) -> dict:
    key = jax.random.key(seed)
    ks = jax.random.split(key, 12)
    f32 = jnp.float32
    nrm = lambda k, shape, scale: jax.random.normal(k, shape, f32) * scale
    return {
        "x": jax.random.normal(ks[0], (BATCH, SEQ, D_MODEL), f32),
        "norm1_g": 1.0 + nrm(ks[1], (DEPTH, D_MODEL), 0.05),
        "w_in": nrm(ks[2], (DEPTH, D_MODEL, IN_WIDTH), D_MODEL ** -0.5),
        "ret_norm_g": 1.0 + nrm(ks[3], (DEPTH, RET_WIDTH), 0.05),
        "q_norm_g": 1.0 + nrm(ks[4], (DEPTH, ATT_HEAD_DIM), 0.05),
        "k_norm_g": 1.0 + nrm(ks[5], (DEPTH, ATT_HEAD_DIM), 0.05),
        "rel_bias": nrm(ks[6], (DEPTH, ATT_HEADS, REL_SIZE), 0.2),
        "w_out": nrm(ks[7], (DEPTH, MIX_WIDTH, D_MODEL), MIX_WIDTH ** -0.5),
        "norm2_g": 1.0 + nrm(ks[8], (DEPTH, D_MODEL), 0.05),
        "w_ff1": nrm(ks[9], (DEPTH, D_MODEL, D_FF), D_MODEL ** -0.5),
        "w_ff2": nrm(ks[10], (DEPTH, D_FF, D_MODEL), D_FF ** -0.5),
    }


def _fwd_reference(x, norm1_g, w_in, ret_norm_g, q_norm_g, k_norm_g, rel_bias,
              w_out, norm2_g, w_ff1, w_ff2):
    b, s, _ = x.shape
    pos = jnp.arange(s, dtype=jnp.float32)
    split_at = [RET_WIDTH, 2 * RET_WIDTH, 3 * RET_WIDTH, 4 * RET_WIDTH,
                4 * RET_WIDTH + ATT_WIDTH, 4 * RET_WIDTH + 2 * ATT_WIDTH]
    for layer in range(DEPTH):
        h = rms_norm(x, norm1_g[layer])
        proj = h @ w_in[layer]
        rq, rk, rv, rg, aq, ak, av = jnp.split(proj, split_at, axis=-1)
        rq = rotary(rq.reshape(b, s, RET_HEADS, RET_HEAD_DIM), pos)
        rk = rotary(rk.reshape(b, s, RET_HEADS, RET_HEAD_DIM), pos)
        rv = rv.reshape(b, s, RET_HEADS, RET_HEAD_DIM)
        ret = head_group_norm(retention(rq, rk, rv), ret_norm_g[layer])
        ret = (jax.nn.silu(rg.astype(jnp.float32)) * ret).astype(x.dtype)
        aq = rms_norm(aq.reshape(b, s, ATT_HEADS, ATT_HEAD_DIM), q_norm_g[layer])
        ak = rms_norm(ak.reshape(b, s, ATT_HEADS, ATT_HEAD_DIM), k_norm_g[layer])
        av = av.reshape(b, s, ATT_HEADS, ATT_HEAD_DIM)
        att = chunk_attention(aq, ak, av, rel_bias[layer])
        x = x + jnp.concatenate([ret, att], axis=-1) @ w_out[layer]
        h = rms_norm(x, norm2_g[layer])
        x = x + jnp.square(jax.nn.relu(h @ w_ff1[layer])) @ w_ff2[layer]
    return x


import jax as _jax
import jax.numpy as _jnp

TWIN_FORMAT = 'train_step'
FWD_PARAMS = ['x', 'norm1_g', 'w_in', 'ret_norm_g', 'q_norm_g', 'k_norm_g', 'rel_bias', 'w_out', 'norm2_g', 'w_ff1', 'w_ff2']
TWIN_WEIGHTS = ['norm1_g', 'w_in', 'ret_norm_g', 'q_norm_g', 'k_norm_g', 'rel_bias', 'w_out', 'norm2_g', 'w_ff1', 'w_ff2']
TWIN_DIFF_INPUT = 'x'
TWIN_INPUTS = ['x', 'norm1_g', 'w_in', 'ret_norm_g', 'q_norm_g', 'k_norm_g', 'rel_bias', 'w_out', 'norm2_g', 'w_ff1', 'w_ff2', 'loss_target', 'm_norm1_g', 'm_w_in', 'm_ret_norm_g', 'm_q_norm_g', 'm_k_norm_g', 'm_rel_bias', 'm_w_out', 'm_norm2_g', 'm_w_ff1', 'm_w_ff2', 'v_norm1_g', 'v_w_in', 'v_ret_norm_g', 'v_q_norm_g', 'v_k_norm_g', 'v_rel_bias', 'v_w_out', 'v_norm2_g', 'v_w_ff1', 'v_w_ff2']
TWIN_OUTPUTS = ['loss', 'grad_x', 'grad_norm1_g', 'grad_w_in', 'grad_ret_norm_g', 'grad_q_norm_g', 'grad_k_norm_g', 'grad_rel_bias', 'grad_w_out', 'grad_norm2_g', 'grad_w_ff1', 'grad_w_ff2', 'delta_norm1_g', 'delta_w_in', 'delta_ret_norm_g', 'delta_q_norm_g', 'delta_k_norm_g', 'delta_rel_bias', 'delta_w_out', 'delta_norm2_g', 'delta_w_ff1', 'delta_w_ff2', 'new_m_norm1_g', 'new_m_w_in', 'new_m_ret_norm_g', 'new_m_q_norm_g', 'new_m_k_norm_g', 'new_m_rel_bias', 'new_m_w_out', 'new_m_norm2_g', 'new_m_w_ff1', 'new_m_w_ff2', 'new_v_norm1_g', 'new_v_w_in', 'new_v_ret_norm_g', 'new_v_q_norm_g', 'new_v_k_norm_g', 'new_v_rel_bias', 'new_v_w_out', 'new_v_norm2_g', 'new_v_w_ff1', 'new_v_w_ff2']
TWIN_LEAF_KINDS = {'loss': 'loss', 'grad_x': 'grad_x', 'grad_norm1_g': 'grad_w', 'grad_w_in': 'grad_w', 'grad_ret_norm_g': 'grad_w', 'grad_q_norm_g': 'grad_w', 'grad_k_norm_g': 'grad_w', 'grad_rel_bias': 'grad_w', 'grad_w_out': 'grad_w', 'grad_norm2_g': 'grad_w', 'grad_w_ff1': 'grad_w', 'grad_w_ff2': 'grad_w', 'delta_norm1_g': 'delta_w', 'delta_w_in': 'delta_w', 'delta_ret_norm_g': 'delta_w', 'delta_q_norm_g': 'delta_w', 'delta_k_norm_g': 'delta_w', 'delta_rel_bias': 'delta_w', 'delta_w_out': 'delta_w', 'delta_norm2_g': 'delta_w', 'delta_w_ff1': 'delta_w', 'delta_w_ff2': 'delta_w', 'new_m_norm1_g': 'new_m', 'new_m_w_in': 'new_m', 'new_m_ret_norm_g': 'new_m', 'new_m_q_norm_g': 'new_m', 'new_m_k_norm_g': 'new_m', 'new_m_rel_bias': 'new_m', 'new_m_w_out': 'new_m', 'new_m_norm2_g': 'new_m', 'new_m_w_ff1': 'new_m', 'new_m_w_ff2': 'new_m', 'new_v_norm1_g': 'new_v', 'new_v_w_in': 'new_v', 'new_v_ret_norm_g': 'new_v', 'new_v_q_norm_g': 'new_v', 'new_v_k_norm_g': 'new_v', 'new_v_rel_bias': 'new_v', 'new_v_w_out': 'new_v', 'new_v_norm2_g': 'new_v', 'new_v_w_ff1': 'new_v', 'new_v_w_ff2': 'new_v'}


def _forward(args):
    return _fwd_reference(*[args[k] for k in FWD_PARAMS])


def _output_shape():
    def fwd():
        inp = _fwd_setup_inputs(0)
        return _fwd_reference(*[inp[k] for k in FWD_PARAMS])
    out = _jax.eval_shape(fwd)
    return out.shape, out.dtype

N_MICROBATCH = 1
ADAM_LR = 0.001
ADAM_B1 = 0.9
ADAM_B2 = 0.999
ADAM_EPS = 1e-08
ADAM_WD = 0.01
ADAM_STEP = 10
PER_EXAMPLE_BATCH_AXIS = {'x': 0, 'loss_target': 0}
SHARED_INPUTS = []
_WEIGHT_DTYPES = {'norm1_g': _jnp.float32, 'w_in': _jnp.float32, 'ret_norm_g': _jnp.float32, 'q_norm_g': _jnp.float32, 'k_norm_g': _jnp.float32, 'rel_bias': _jnp.float32, 'w_out': _jnp.float32, 'norm2_g': _jnp.float32, 'w_ff1': _jnp.float32, 'w_ff2': _jnp.float32}
MOMENT_SCALE = {'norm1_g': 3.221367e+00, 'w_in': 1.877025e-01, 'ret_norm_g': 5.655490e+00, 'q_norm_g': 5.713531e-01, 'k_norm_g': 5.626298e-01, 'rel_bias': 4.028855e-02, 'w_out': 2.127831e-01, 'norm2_g': 4.833435e+01, 'w_ff1': 3.136770e-01, 'w_ff2': 4.032066e+00}


def _to_microbatches(a, axis):
    t = _jnp.moveaxis(a, axis, 0)
    t = t.reshape((N_MICROBATCH, t.shape[0] // N_MICROBATCH) + t.shape[1:])
    return _jnp.moveaxis(t, 1, axis + 1)


def setup_inputs(seed: int = 0) -> dict:
    inp = _fwd_setup_inputs(seed)
    key = _jax.random.fold_in(_jax.random.key(seed), 7919)
    shape, _ = _output_shape()
    out = dict(inp)
    out["loss_target"] = _jax.random.normal(_jax.random.fold_in(key, 0), shape, _jnp.float32)
    for i, name in enumerate(TWIN_WEIGHTS):
        w = inp[name].astype(_jnp.float32)
        if MOMENT_SCALE is None:
            s = _jnp.sqrt(_jnp.mean(_jnp.square(w)) + 1e-30)
        else:
            s = MOMENT_SCALE[name]
        km, kv = _jax.random.split(_jax.random.fold_in(key, i + 1))
        out[name] = w
        out["m_" + name] = s * _jax.random.normal(km, w.shape, _jnp.float32)
        out["v_" + name] = (s * s) * _jax.random.uniform(kv, w.shape, _jnp.float32, 0.5, 1.5)
    if N_MICROBATCH > 1:
        for name, axis in PER_EXAMPLE_BATCH_AXIS.items():
            out[name] = _to_microbatches(out[name], axis)
    return {'x': out['x'], 'norm1_g': out['norm1_g'], 'w_in': out['w_in'], 'ret_norm_g': out['ret_norm_g'], 'q_norm_g': out['q_norm_g'], 'k_norm_g': out['k_norm_g'], 'rel_bias': out['rel_bias'], 'w_out': out['w_out'], 'norm2_g': out['norm2_g'], 'w_ff1': out['w_ff1'], 'w_ff2': out['w_ff2'], 'loss_target': out['loss_target'], 'm_norm1_g': out['m_norm1_g'], 'm_w_in': out['m_w_in'], 'm_ret_norm_g': out['m_ret_norm_g'], 'm_q_norm_g': out['m_q_norm_g'], 'm_k_norm_g': out['m_k_norm_g'], 'm_rel_bias': out['m_rel_bias'], 'm_w_out': out['m_w_out'], 'm_norm2_g': out['m_norm2_g'], 'm_w_ff1': out['m_w_ff1'], 'm_w_ff2': out['m_w_ff2'], 'v_norm1_g': out['v_norm1_g'], 'v_w_in': out['v_w_in'], 'v_ret_norm_g': out['v_ret_norm_g'], 'v_q_norm_g': out['v_q_norm_g'], 'v_k_norm_g': out['v_k_norm_g'], 'v_rel_bias': out['v_rel_bias'], 'v_w_out': out['v_w_out'], 'v_norm2_g': out['v_norm2_g'], 'v_w_ff1': out['v_w_ff1'], 'v_w_ff2': out['v_w_ff2']}


def _loss(weights, diff, rest, loss_target):
    with _jax.named_scope("forward"):
        args = {**rest, TWIN_DIFF_INPUT: diff, **{k: w.astype(_WEIGHT_DTYPES[k]) for k, w in weights.items()}}
        y = _forward(args)
    with _jax.named_scope("loss_head"):
        err = _jnp.square(y.astype(_jnp.float32) - loss_target)
        return 0.5 * _jnp.sum(_jnp.mean(err, axis=-1)) if err.ndim else 0.5 * err


def _adamw(w, g, m, v):
    m = ADAM_B1 * m + (1.0 - ADAM_B1) * g
    v = ADAM_B2 * v + (1.0 - ADAM_B2) * _jnp.square(g)
    m_hat = m / (1.0 - ADAM_B1 ** ADAM_STEP)
    v_hat = v / (1.0 - ADAM_B2 ** ADAM_STEP)
    delta = -ADAM_LR * (m_hat / (_jnp.sqrt(v_hat) + ADAM_EPS) + ADAM_WD * w)
    return delta, m, v


def reference(x, norm1_g, w_in, ret_norm_g, q_norm_g, k_norm_g, rel_bias, w_out, norm2_g, w_ff1, w_ff2, loss_target, m_norm1_g, m_w_in, m_ret_norm_g, m_q_norm_g, m_k_norm_g, m_rel_bias, m_w_out, m_norm2_g, m_w_ff1, m_w_ff2, v_norm1_g, v_w_in, v_ret_norm_g, v_q_norm_g, v_k_norm_g, v_rel_bias, v_w_out, v_norm2_g, v_w_ff1, v_w_ff2):
    given = dict(x=x, norm1_g=norm1_g, w_in=w_in, ret_norm_g=ret_norm_g, q_norm_g=q_norm_g, k_norm_g=k_norm_g, rel_bias=rel_bias, w_out=w_out, norm2_g=norm2_g, w_ff1=w_ff1, w_ff2=w_ff2, loss_target=loss_target, m_norm1_g=m_norm1_g, m_w_in=m_w_in, m_ret_norm_g=m_ret_norm_g, m_q_norm_g=m_q_norm_g, m_k_norm_g=m_k_norm_g, m_rel_bias=m_rel_bias, m_w_out=m_w_out, m_norm2_g=m_norm2_g, m_w_ff1=m_w_ff1, m_w_ff2=m_w_ff2, v_norm1_g=v_norm1_g, v_w_in=v_w_in, v_ret_norm_g=v_ret_norm_g, v_q_norm_g=v_q_norm_g, v_k_norm_g=v_k_norm_g, v_rel_bias=v_rel_bias, v_w_out=v_w_out, v_norm2_g=v_norm2_g, v_w_ff1=v_w_ff1, v_w_ff2=v_w_ff2)
    weights = {n: given[n] for n in TWIN_WEIGHTS}
    shared = {n: given[n] for n in SHARED_INPUTS}
    per_example = {n: given[n] for n in ['x']}
    grad_fn = _jax.value_and_grad(_loss, argnums=(0, 1))

    def one_microbatch(ex, loss_target):
        ex = dict(ex)
        diff = ex.pop(TWIN_DIFF_INPUT)
        return grad_fn(weights, diff, {**shared, **ex}, loss_target)

    if N_MICROBATCH == 1:
        loss, (grad_w, grad_x) = one_microbatch(per_example, given["loss_target"])
    else:
        def body(carry, xs):
            loss_sum, grad_sum = carry
            l_k, (gw_k, gx_k) = one_microbatch(xs[0], xs[1])
            with _jax.named_scope("update"):
                return (loss_sum + l_k, _jax.tree.map(_jnp.add, grad_sum, gw_k)), gx_k

        init = (_jnp.zeros((), _jnp.float32), _jax.tree.map(_jnp.zeros_like, weights))
        (loss, grad_w), grad_x = _jax.lax.scan(body, init, (per_example, given["loss_target"]))
    with _jax.named_scope("update"):
        delta_w, new_m, new_v = {}, {}, {}
        for n in TWIN_WEIGHTS:
            delta_w[n], new_m[n], new_v[n] = _adamw(weights[n], grad_w[n], given["m_" + n], given["v_" + n])
    return (loss, grad_x, *[grad_w[n] for n in TWIN_WEIGHTS], *[delta_w[n] for n in TWIN_WEIGHTS],
            *[new_m[n] for n in TWIN_WEIGHTS], *[new_v[n] for n in TWIN_WEIGHTS])
```

```python
import functools

import jax
import jax.numpy as jnp
from jax import lax
from jax.experimental import pallas as pl
from jax.experimental.pallas import tpu as pltpu

F32 = jnp.float32
BF16 = jnp.bfloat16
MXU_DTYPE = jnp.bfloat16

CHUNK = 64
HEADS = 8
HEAD_DIM = 128
LEFT_CHUNKS = 8
BAND = (LEFT_CHUNKS + 1) * CHUNK
REL_CLIP = 128
REL_SIZE = (CHUNK - 1) + REL_CLIP + 1
ROPE_BASE = 10000.0
EPS = 1e-6
GN_EPS = 1e-5
ADAM_LR, ADAM_B1, ADAM_B2, ADAM_EPS, ADAM_WD, ADAM_STEP = 0.001, 0.9, 0.999, 1e-08, 0.01, 10
N_CHIPS = 4
VMEM_LIMIT = 56 * 1024 * 1024
MESH = pl.DeviceIdType.MESH
ANY = pl.BlockSpec(memory_space=pl.ANY)

NN = (((1,), (0,)), ((), ()))
NT = (((1,), (1,)), ((), ()))
TN = (((0,), (0,)), ((), ()))


def _pallas(body, **kw):
    return pl.pallas_call(body, **kw)


def _params(*sem):
    return pltpu.CompilerParams(dimension_semantics=sem, vmem_limit_bytes=VMEM_LIMIT)


def _dot(a, b, dims):
    return lax.dot_general(a.astype(MXU_DTYPE), b.astype(MXU_DTYPE), dims, preferred_element_type=F32)


def _mm(name, a, b, dims, grid, a_spec, b_spec, outs, o_specs, acc_shape, extras=(), extra_specs=(), epi=None):
    nk = grid[2]
    n_ex, n_out = len(extras), len(outs)

    def body(*refs):
        a_ref, b_ref = refs[0], refs[1]
        ex_refs = refs[2:2 + n_ex]
        o_refs = refs[2 + n_ex:2 + n_ex + n_out]
        acc_ref = refs[-1]
        k = pl.program_id(2)

        @pl.when(k == 0)
        def _():
            acc_ref[...] = jnp.zeros_like(acc_ref)

        acc_ref[...] += _dot(a_ref[...], b_ref[...], dims)

        @pl.when(k == nk - 1)
        def _():
            acc = acc_ref[...]
            vals = epi(acc, *[r[...] for r in ex_refs]) if epi is not None else (acc,)
            for r, v in zip(o_refs, vals):
                r[...] = v.astype(r.dtype)

    return _pallas(
        body, name=name, grid=grid, in_specs=[a_spec, b_spec, *extra_specs], out_specs=list(o_specs),
        out_shape=list(outs), scratch_shapes=[pltpu.VMEM(acc_shape, F32)],
        compiler_params=_params("parallel", "parallel", "arbitrary"),
    )(a, b, *extras)


def _sds(shape, dtype):
    return jax.ShapeDtypeStruct(shape, dtype)


def _cast_bf16(w, name):
    r, c = w.shape
    tr = min(r, 256)

    def body(w_ref, o_ref):
        o_ref[...] = w_ref[...].astype(BF16)

    return _pallas(body, name=name, grid=(r // tr,), in_specs=[pl.BlockSpec((tr, c), lambda i: (i, 0))],
                   out_specs=pl.BlockSpec((tr, c), lambda i: (i, 0)), out_shape=_sds((r, c), BF16),
                   compiler_params=_params("parallel"))(w)


def _rmsnorm_fwd(x, g, name):
    s, d = x.shape
    tr = 256

    def body(x_ref, g_ref, o_ref):
        xv = x_ref[...]
        y = xv * lax.rsqrt(jnp.mean(xv * xv, axis=-1, keepdims=True) + EPS)
        o_ref[...] = (y * g_ref[...]).astype(o_ref.dtype)

    return _pallas(body, name=name, grid=(s // tr,),
                   in_specs=[pl.BlockSpec((tr, d), lambda i: (i, 0)), pl.BlockSpec((1, d), lambda i: (0, 0))],
                   out_specs=pl.BlockSpec((tr, d), lambda i: (i, 0)), out_shape=_sds((s, d), BF16),
                   compiler_params=_params("parallel"))(x, g)


def _rmsnorm_bwd(x, g, dh, res, name):
    s, d = x.shape
    tr = 256

    def body(x_ref, g_ref, dh_ref, res_ref, dx_ref, dxb_ref, dg_ref):
        i = pl.program_id(0)
        xv = x_ref[...]
        rstd = lax.rsqrt(jnp.mean(xv * xv, axis=-1, keepdims=True) + EPS)
        xh = xv * rstd
        dhv = dh_ref[...]

        @pl.when(i == 0)
        def _():
            dg_ref[...] = jnp.zeros_like(dg_ref)

        dg_ref[...] += jnp.sum(dhv * xh, axis=0, keepdims=True)
        dxh = dhv * g_ref[...]
        dx = res_ref[...] + rstd * (dxh - xh * jnp.mean(dxh * xh, axis=-1, keepdims=True))
        dx_ref[...] = dx
        dxb_ref[...] = dx.astype(BF16)

    row = pl.BlockSpec((tr, d), lambda i: (i, 0))
    vec = pl.BlockSpec((1, d), lambda i: (0, 0))
    return _pallas(body, name=name, grid=(s // tr,), in_specs=[row, vec, row, row], out_specs=[row, row, vec],
                   out_shape=[_sds((s, d), F32), _sds((s, d), BF16), _sds((1, d), F32)],
                   compiler_params=_params("arbitrary"))(x, g, dh, res)


def _adamw_math(w, g, m, v):
    m = ADAM_B1 * m + (1.0 - ADAM_B1) * g
    v = ADAM_B2 * v + (1.0 - ADAM_B2) * (g * g)
    m_hat = m / (1.0 - ADAM_B1 ** ADAM_STEP)
    v_hat = v / (1.0 - ADAM_B2 ** ADAM_STEP)
    delta = -ADAM_LR * (m_hat / (jnp.sqrt(v_hat) + ADAM_EPS) + ADAM_WD * w)
    return delta, m, v


def _adamw(w, g, m, v, name):
    r, c = w.shape
    tr = 128

    def body(w_ref, g_ref, m_ref, v_ref, d_ref, nm_ref, nv_ref):
        d_ref[...], nm_ref[...], nv_ref[...] = _adamw_math(w_ref[...], g_ref[...], m_ref[...], v_ref[...])

    blk = pl.BlockSpec((tr, c), lambda i: (i, 0))
    return _pallas(body, name=name, grid=(r // tr,), in_specs=[blk] * 4, out_specs=[blk] * 3,
                   out_shape=[_sds((r, c), F32)] * 3, compiler_params=_params("parallel"))(w, g, m, v)


def _tables(s):
    half = HEAD_DIM // 2
    pos = jnp.arange(s, dtype=F32)
    inv_freq = ROPE_BASE ** (-jnp.arange(half, dtype=F32) / half)
    ang = pos[:, None] * inv_freq[None, :]
    cos, sin = jnp.cos(ang), jnp.sin(ang)
    cos_f = jnp.concatenate([cos, cos], axis=-1)
    sin_f = jnp.concatenate([-sin, sin], axis=-1)
    log_g = jnp.log1p(-jnp.exp2(-(5.0 + jnp.arange(HEADS, dtype=F32))))
    p = jnp.arange(CHUNK, dtype=F32)
    decay = jnp.exp(log_g[:, None, None] * jnp.abs(p[:, None] - p[None, :]))
    k_dec = jnp.exp(log_g[None, :] * (CHUNK - 1.0 - p)[:, None])
    q_dec = jnp.exp(log_g[None, :] * (p + 1.0)[:, None])
    c_dec = jnp.exp(log_g * CHUNK)
    k_dec = jnp.broadcast_to(k_dec.T[:, :, None], (HEADS, CHUNK, HEAD_DIM))
    q_dec = jnp.broadcast_to(q_dec.T[:, :, None], (HEADS, CHUNK, HEAD_DIM))
    c_dec = jnp.broadcast_to(c_dec[:, None, None], (HEADS, 1, HEAD_DIM))
    return cos_f, sin_f, decay, k_dec, q_dec, c_dec


def _rot(x, cos_f, sin_f):
    return x * cos_f + pltpu.roll(x, HEAD_DIM // 2, 1) * sin_f


def _rot_bwd(d, cos_f, sin_f):
    return d * cos_f + pltpu.roll(d * sin_f, HEAD_DIM // 2, 1)


RET_BLOCK_CHUNKS = 8
RET_ROWS = RET_BLOCK_CHUNKS * CHUNK
K_SCALE = HEAD_DIM ** -0.5


def _retention_fwd(proj, gn_g, tables):
    s = proj.shape[0]
    nb = s // RET_ROWS
    nc = s // CHUNK
    cos_f, sin_f, decay, k_dec, q_dec, c_dec = tables

    def body(q_ref, k_ref, v_ref, g_ref, cos_ref, sin_ref, dec_ref, kd_ref, qd_ref, cd_ref, gn_ref,
             ret_ref, y_ref, prev_ref, state_ref):
        @pl.when(pl.program_id(1) == 0)
        def _():
            state_ref[...] = jnp.zeros_like(state_ref)

        cosv, sinv = cos_ref[...], sin_ref[...]
        q = _rot(q_ref[...], cosv, sinv)
        k = _rot(k_ref[...], cosv, sinv) * K_SCALE
        v = v_ref[...]
        rg = g_ref[...]
        dec, kd, qd, cd, gn = dec_ref[...], kd_ref[...], qd_ref[...], cd_ref[...], gn_ref[...]
        state = state_ref[...]
        for c in range(RET_BLOCK_CHUNKS):
            rows = slice(c * CHUNK, (c + 1) * CHUNK)
            qc, kc, vc = q[rows], k[rows], v[rows]
            sc = _dot(qc, kc, NT) * dec
            intra = _dot(sc, vc, NN)
            prev_ref[c] = state.astype(prev_ref.dtype)
            cross = _dot(qc * qd, state, NN)
            contrib = _dot(kc * kd, vc, TN)
            state = cd * state + contrib
            y = intra + cross
            y_ref[rows, :] = y
            mu = jnp.mean(y, axis=-1, keepdims=True)
            yc = y - mu
            var = jnp.mean(yc * yc, axis=-1, keepdims=True)
            yn = yc * lax.rsqrt(var + GN_EPS) * gn
            rgc = rg[rows]
            ret_ref[rows, :] = (rgc * jax.nn.sigmoid(rgc) * yn).astype(ret_ref.dtype)
        state_ref[...] = state

    def col(off):
        return pl.BlockSpec((RET_ROWS, HEAD_DIM), lambda h, i: (i, off + h))

    pos = pl.BlockSpec((RET_ROWS, HEAD_DIM), lambda h, i: (i, 0))
    per_head = lambda shape: pl.BlockSpec((None, *shape), lambda h, i: (h, 0, 0))
    return _pallas(
        body, name="retention_fwd", grid=(HEADS, nb),
        in_specs=[col(0), col(HEADS), col(2 * HEADS), col(3 * HEADS), pos, pos,
                  per_head((CHUNK, CHUNK)), per_head((CHUNK, HEAD_DIM)), per_head((CHUNK, HEAD_DIM)),
                  per_head((1, HEAD_DIM)), pl.BlockSpec((1, HEAD_DIM), lambda h, i: (0, h))],
        out_specs=[col(0), col(0),
                   pl.BlockSpec((None, RET_BLOCK_CHUNKS, HEAD_DIM, HEAD_DIM), lambda h, i: (h, i, 0, 0))],
        out_shape=[_sds((s, HEADS * HEAD_DIM), BF16), _sds((s, HEADS * HEAD_DIM), F32),
                   _sds((HEADS, nc, HEAD_DIM, HEAD_DIM), MXU_DTYPE)],
        scratch_shapes=[pltpu.VMEM((HEAD_DIM, HEAD_DIM), F32)],
        compiler_params=_params("parallel", "arbitrary"),
    )(proj, proj, proj, proj, cos_f, sin_f, decay, k_dec, q_dec, c_dec, gn_g)


def _retention_bwd(proj, gn_g, tables, y, prev, dmix):
    s = proj.shape[0]
    nb = s // RET_ROWS
    cos_f, sin_f, decay, k_dec, q_dec, c_dec = tables

    def body(q_ref, k_ref, v_ref, g_ref, cos_ref, sin_ref, dec_ref, kd_ref, qd_ref, cd_ref, gn_ref,
             y_ref, prev_ref, dret_ref, dq_ref, dk_ref, dv_ref, dg_ref, dgn_ref, gstate_ref):
        @pl.when(pl.program_id(1) == 0)
        def _():
            gstate_ref[...] = jnp.zeros_like(gstate_ref)
            dgn_ref[...] = jnp.zeros_like(dgn_ref)

        cosv, sinv = cos_ref[...], sin_ref[...]
        q = _rot(q_ref[...], cosv, sinv)
        k = _rot(k_ref[...], cosv, sinv) * K_SCALE
        v = v_ref[...]
        dec, kd, qd, cd, gn = dec_ref[...], kd_ref[...], qd_ref[...], cd_ref[...], gn_ref[...]
        rg = g_ref[...]
        yv = y_ref[...]
        dret = dret_ref[...]
        sig = jax.nn.sigmoid(rg)
        gate = rg * sig
        mu = jnp.mean(yv, axis=-1, keepdims=True)
        yc = yv - mu
        rstd = lax.rsqrt(jnp.mean(yc * yc, axis=-1, keepdims=True) + GN_EPS)
        z = yc * rstd
        dyn = dret * gate
        dg_ref[...] = (dret * (z * gn) * (sig * (1.0 + rg * (1.0 - sig)))).astype(dg_ref.dtype)
        dgn_ref[...] += jnp.sum(dyn * z, axis=0, keepdims=True)
        dz = dyn * gn
        dy = rstd * (dz - jnp.mean(dz, axis=-1, keepdims=True) - z * jnp.mean(dz * z, axis=-1, keepdims=True))
        gst = gstate_ref[...]
        for c in reversed(range(RET_BLOCK_CHUNKS)):
            rows = slice(c * CHUNK, (c + 1) * CHUNK)
            qc, kc, vc, dyc = q[rows], k[rows], v[rows], dy[rows]
            sc = _dot(qc, kc, NT) * dec
            dp = _dot(dyc, vc, NT)
            dvc = _dot(sc, dyc, TN)
            ds = dp * dec
            dqc = _dot(ds, kc, NN)
            dkc = _dot(ds, qc, TN)
            prevc = prev_ref[c]
            dqc += _dot(dyc, prevc, NT) * qd
            dprev = _dot(qc * qd, dyc, TN)
            dkc += _dot(vc, gst, NT) * kd
            dvc += _dot(kc * kd, gst, NN)
            gst = dprev + cd * gst
            dq_ref[rows, :] = _rot_bwd(dqc, cosv[rows], sinv[rows]).astype(dq_ref.dtype)
            dk_ref[rows, :] = _rot_bwd(dkc * K_SCALE, cosv[rows], sinv[rows]).astype(dk_ref.dtype)
            dv_ref[rows, :] = dvc.astype(dv_ref.dtype)
        gstate_ref[...] = gst

    rev = lambda i: nb - 1 - i

    def col(off):
        return pl.BlockSpec((RET_ROWS, HEAD_DIM), lambda h, i: (rev(i), off + h))

    pos = pl.BlockSpec((RET_ROWS, HEAD_DIM), lambda h, i: (rev(i), 0))
    per_head = lambda shape: pl.BlockSpec((None, *shape), lambda h, i: (h, 0, 0))
    outb = _sds((s, HEADS * HEAD_DIM), BF16)
    return _pallas(
        body, name="retention_bwd", grid=(HEADS, nb),
        in_specs=[col(0), col(HEADS), col(2 * HEADS), col(3 * HEADS), pos, pos,
                  per_head((CHUNK, CHUNK)), per_head((CHUNK, HEAD_DIM)), per_head((CHUNK, HEAD_DIM)),
                  per_head((1, HEAD_DIM)), pl.BlockSpec((1, HEAD_DIM), lambda h, i: (0, h)),
                  col(0), pl.BlockSpec((None, RET_BLOCK_CHUNKS, HEAD_DIM, HEAD_DIM), lambda h, i: (h, rev(i), 0, 0)),
                  col(0)],
        out_specs=[col(0), col(0), col(0), col(0), per_head((1, HEAD_DIM))],
        out_shape=[outb, outb, outb, outb, _sds((HEADS, 1, HEAD_DIM), F32)],
        scratch_shapes=[pltpu.VMEM((HEAD_DIM, HEAD_DIM), F32)],
        compiler_params=_params("parallel", "arbitrary"),
    )(proj, proj, proj, proj, cos_f, sin_f, decay, k_dec, q_dec, c_dec, gn_g, y, prev, dmix)


ATT_COL0 = 4 * HEADS
PAD_ROWS = LEFT_CHUNKS * CHUNK
NORM_ROWS = 512


def _qk_norm(x, g):
    return x * lax.rsqrt(jnp.mean(x * x, axis=-1, keepdims=True) + EPS) * g


def _band_probs(qb, kb, bias, n):
    sc = _dot(qb, kb, NT) * K_SCALE + bias
    band_chunk = lax.broadcasted_iota(jnp.int32, (CHUNK, BAND), 1) // CHUNK
    sc = jnp.where(n - LEFT_CHUNKS + band_chunk >= 0, sc, -1e30)
    e = jnp.exp(sc - jnp.max(sc, axis=-1, keepdims=True))
    return e / jnp.sum(e, axis=-1, keepdims=True)


def _attention_fwd(proj, gq, gk, bias):
    s = proj.shape[0]
    nc = s // CHUNK

    def body(q_ref, k_ref, v_ref, gq_ref, gk_ref, bias_ref, o_ref, kp_ref, vp_ref):
        kp_ref[0:PAD_ROWS, :] = jnp.zeros((PAD_ROWS, HEAD_DIM), kp_ref.dtype)
        vp_ref[0:PAD_ROWS, :] = jnp.zeros((PAD_ROWS, HEAD_DIM), vp_ref.dtype)
        gqv, gkv = gq_ref[...], gk_ref[...]

        def fill(b, carry):
            r0 = pl.multiple_of(b * NORM_ROWS, NORM_ROWS)
            kp_ref[pl.ds(PAD_ROWS + r0, NORM_ROWS), :] = _qk_norm(k_ref[pl.ds(r0, NORM_ROWS), :], gkv).astype(kp_ref.dtype)
            vp_ref[pl.ds(PAD_ROWS + r0, NORM_ROWS), :] = v_ref[pl.ds(r0, NORM_ROWS), :].astype(vp_ref.dtype)
            return carry

        lax.fori_loop(0, s // NORM_ROWS, fill, 0)
        biasv = bias_ref[...]

        def chunk(n, carry):
            r0 = pl.multiple_of(n * CHUNK, CHUNK)
            qn = _qk_norm(q_ref[pl.ds(r0, CHUNK), :], gqv)
            p = _band_probs(qn, kp_ref[pl.ds(r0, BAND), :], biasv, n)
            o_ref[pl.ds(r0, CHUNK), :] = _dot(p, vp_ref[pl.ds(r0, BAND), :], NN).astype(o_ref.dtype)
            return carry

        lax.fori_loop(0, nc, chunk, 0)

    def col(off):
        return pl.BlockSpec((s, HEAD_DIM), lambda h: (0, off + h))

    vec = pl.BlockSpec((1, HEAD_DIM), lambda h: (0, 0))
    return _pallas(
        body, name="attention_fwd", grid=(HEADS,),
        in_specs=[col(ATT_COL0), col(ATT_COL0 + HEADS), col(ATT_COL0 + 2 * HEADS), vec, vec,
                  pl.BlockSpec((None, CHUNK, BAND), lambda h: (h, 0, 0))],
        out_specs=col(0), out_shape=_sds((s, HEADS * HEAD_DIM), BF16),
        scratch_shapes=[pltpu.VMEM((s + PAD_ROWS, HEAD_DIM), MXU_DTYPE), pltpu.VMEM((s + PAD_ROWS, HEAD_DIM), MXU_DTYPE)],
        compiler_params=_params("parallel"),
    )(proj, proj, proj, gq, gk, bias)


def _attention_bwd(proj, gq, gk, bias, dmix):
    s = proj.shape[0]
    nc = s // CHUNK

    def body(q_ref, k_ref, v_ref, gq_ref, gk_ref, bias_ref, do_ref,
             dq_ref, dk_ref, dv_ref, dgq_ref, dgk_ref, dbias_ref, kp_ref, vp_ref, dkp_ref, dvp_ref, dqn_ref):
        kp_ref[0:PAD_ROWS, :] = jnp.zeros((PAD_ROWS, HEAD_DIM), kp_ref.dtype)
        vp_ref[0:PAD_ROWS, :] = jnp.zeros((PAD_ROWS, HEAD_DIM), vp_ref.dtype)
        dkp_ref[...] = jnp.zeros_like(dkp_ref)
        dvp_ref[...] = jnp.zeros_like(dvp_ref)
        dbias_ref[...] = jnp.zeros_like(dbias_ref)
        gqv, gkv = gq_ref[...], gk_ref[...]

        def fill(b, carry):
            r0 = pl.multiple_of(b * NORM_ROWS, NORM_ROWS)
            kp_ref[pl.ds(PAD_ROWS + r0, NORM_ROWS), :] = _qk_norm(k_ref[pl.ds(r0, NORM_ROWS), :], gkv).astype(kp_ref.dtype)
            vp_ref[pl.ds(PAD_ROWS + r0, NORM_ROWS), :] = v_ref[pl.ds(r0, NORM_ROWS), :].astype(vp_ref.dtype)
            return carry

        lax.fori_loop(0, s // NORM_ROWS, fill, 0)
        biasv = bias_ref[...]

        def chunk(n, carry):
            r0 = pl.multiple_of(n * CHUNK, CHUNK)
            qn = _qk_norm(q_ref[pl.ds(r0, CHUNK), :], gqv)
            kb = kp_ref[pl.ds(r0, BAND), :]
            vb = vp_ref[pl.ds(r0, BAND), :]
            p = _band_probs(qn, kb, biasv, n)
            do = do_ref[pl.ds(r0, CHUNK), :]
            dvp_ref[pl.ds(r0, BAND), :] += _dot(p, do, TN)
            dp = _dot(do, vb, NT)
            ds = p * (dp - jnp.sum(dp * p, axis=-1, keepdims=True))
            dbias_ref[...] += ds
            dss = ds * K_SCALE
            dqn_ref[pl.ds(r0, CHUNK), :] = _dot(dss, kb, NN)
            dkp_ref[pl.ds(r0, BAND), :] += _dot(dss, qn, TN)
            return carry

        lax.fori_loop(0, nc, chunk, 0)

        @pl.when(pl.program_id(0) == 0)
        def _():
            dgq_ref[...] = jnp.zeros_like(dgq_ref)
            dgk_ref[...] = jnp.zeros_like(dgk_ref)

        def norm_bwd(x, g, dn):
            rstd = lax.rsqrt(jnp.mean(x * x, axis=-1, keepdims=True) + EPS)
            xh = x * rstd
            dxh = dn * g
            return rstd * (dxh - xh * jnp.mean(dxh * xh, axis=-1, keepdims=True)), jnp.sum(dn * xh, axis=0, keepdims=True)

        def finish(b, carry):
            r0 = pl.multiple_of(b * NORM_ROWS, NORM_ROWS)
            rows = pl.ds(r0, NORM_ROWS)
            dq, dgq = norm_bwd(q_ref[rows, :], gqv, dqn_ref[rows, :])
            dk, dgk = norm_bwd(k_ref[rows, :], gkv, dkp_ref[pl.ds(PAD_ROWS + r0, NORM_ROWS), :])
            dq_ref[rows, :] = dq.astype(dq_ref.dtype)
            dk_ref[rows, :] = dk.astype(dk_ref.dtype)
            dv_ref[rows, :] = dvp_ref[pl.ds(PAD_ROWS + r0, NORM_ROWS), :].astype(dv_ref.dtype)
            dgq_ref[...] += dgq
            dgk_ref[...] += dgk
            return carry

        lax.fori_loop(0, s // NORM_ROWS, finish, 0)

    def col(off):
        return pl.BlockSpec((s, HEAD_DIM), lambda h: (0, off + h))

    vec = pl.BlockSpec((1, HEAD_DIM), lambda h: (0, 0))
    hbias = pl.BlockSpec((None, CHUNK, BAND), lambda h: (h, 0, 0))
    outb = _sds((s, HEADS * HEAD_DIM), BF16)
    return _pallas(
        body, name="attention_bwd", grid=(HEADS,),
        in_specs=[col(ATT_COL0), col(ATT_COL0 + HEADS), col(ATT_COL0 + 2 * HEADS), vec, vec, hbias, col(HEADS)],
        out_specs=[col(0), col(0), col(0), vec, vec, hbias],
        out_shape=[outb, outb, outb, _sds((1, HEAD_DIM), F32), _sds((1, HEAD_DIM), F32),
                   _sds((HEADS, CHUNK, BAND), F32)],
        scratch_shapes=[pltpu.VMEM((s + PAD_ROWS, HEAD_DIM), MXU_DTYPE), pltpu.VMEM((s + PAD_ROWS, HEAD_DIM), MXU_DTYPE),
                        pltpu.VMEM((s + PAD_ROWS, HEAD_DIM), F32), pltpu.VMEM((s + PAD_ROWS, HEAD_DIM), F32),
                        pltpu.VMEM((s, HEAD_DIM), F32)],
        compiler_params=_params("arbitrary"),
    )(proj, proj, proj, gq, gk, bias, dmix)


def _rel_distance():
    qi = lax.broadcasted_iota(jnp.int32, (CHUNK, BAND), 0)
    kj = lax.broadcasted_iota(jnp.int32, (CHUNK, BAND), 1)
    return jnp.clip(qi + LEFT_CHUNKS * CHUNK - kj, -(CHUNK - 1), REL_CLIP) + (CHUNK - 1)


def _rel_bias_expand(rel_bias):
    def body(rb_ref, o_ref):
        h = pl.program_id(0)
        rel = _rel_distance()
        o_ref[...] = lax.fori_loop(0, REL_SIZE, lambda r, acc: jnp.where(rel == r, rb_ref[h, r], acc),
                                   jnp.zeros((CHUNK, BAND), F32))

    return _pallas(body, name="rel_bias_expand", grid=(HEADS,), in_specs=[pl.BlockSpec(memory_space=pltpu.SMEM)],
                   out_specs=pl.BlockSpec((None, CHUNK, BAND), lambda h: (h, 0, 0)),
                   out_shape=_sds((HEADS, CHUNK, BAND), F32), compiler_params=_params("parallel"))(rel_bias)


def _rel_bias_fold(dbias):
    def body(a_ref, o_ref):
        a = a_ref[...]
        rel = _rel_distance()
        lane = lax.broadcasted_iota(jnp.int32, (1, REL_SIZE), 1)

        def step(r, acc):
            tot = jnp.sum(jnp.where(rel == r, a, 0.0), axis=1, keepdims=True)
            tot = jnp.sum(tot, axis=0, keepdims=True)
            return acc + jnp.where(lane == r, tot, 0.0)

        o_ref[...] = lax.fori_loop(0, REL_SIZE, step, jnp.zeros((1, REL_SIZE), F32))

    return _pallas(body, name="rel_bias_fold", grid=(HEADS,),
                   in_specs=[pl.BlockSpec((None, CHUNK, BAND), lambda h: (h, 0, 0))],
                   out_specs=pl.BlockSpec((None, 1, REL_SIZE), lambda h: (h, 0, 0)),
                   out_shape=_sds((HEADS, 1, REL_SIZE), F32), compiler_params=_params("parallel"))(dbias)


def _place():
    return lax.axis_index("x"), lax.axis_index("y"), lax.axis_index("c")


def _other_chips(x, y):
    return [(1 - x, y), (x, 1 - y), (1 - x, 1 - y)]


def _all_gather_weights(shards):
    n = len(shards)

    def body(*refs):
        src = refs[:n]
        out = refs[n:2 * n]
        send_sems, recv_sems, local_sems = refs[2 * n:]
        x, y, c = _place()
        mine = 2 * x + y
        chips = _other_chips(x, y)
        local = [pltpu.make_async_copy(src[w], out[w].at[mine], local_sems.at[w]) for w in range(n)]
        for cp in local:
            cp.start()

        def half(w, ref, cc):
            hr = shards[w].shape[0] // 2
            return ref.at[pl.ds(cc * hr, hr), :]

        def ici(w, j, chip_from, to):
            k = 2 * chip_from[0] + chip_from[1]
            return pltpu.make_async_remote_copy(
                src_ref=half(w, src[w], c), dst_ref=half(w, out[w].at[k], c),
                send_sem=send_sems.at[w, j], recv_sem=recv_sems.at[w, j], device_id=to, device_id_type=MESH)

        def d2d(w, j, chip_from, cc, to):
            k = 2 * chip_from[0] + chip_from[1]
            return pltpu.make_async_remote_copy(
                src_ref=half(w, out[w].at[k], cc), dst_ref=half(w, out[w].at[k], cc),
                send_sem=send_sems.at[w, 3 + j], recv_sem=recv_sems.at[w, 3 + j], device_id=to, device_id_type=MESH)

        first = [[ici(w, j, (x, y), (*chip, c)) for j, chip in enumerate(chips)] for w in range(n)]
        for w in range(n):
            for cp in first[w]:
                cp.start()
        passed = [[d2d(w, j, chip, c, (x, y, 1 - c)) for j, chip in enumerate(chips)] for w in range(n)]
        for w in range(n):
            for j, chip in enumerate(chips):
                ici(w, j, chip, (x, y, c)).wait_recv()
                passed[w][j].start()
        for w in range(n):
            for j, chip in enumerate(chips):
                d2d(w, j, chip, 1 - c, (x, y, c)).wait_recv()
        for w in range(n):
            for cp in first[w] + passed[w]:
                cp.wait_send()
            local[w].wait()

    return _pallas(
        body, name="all_gather_weights", in_specs=[ANY] * n, out_specs=[ANY] * n,
        out_shape=[_sds((N_CHIPS, *s.shape), s.dtype) for s in shards],
        scratch_shapes=[pltpu.SemaphoreType.DMA((n, 6)), pltpu.SemaphoreType.DMA((n, 6)), pltpu.SemaphoreType.DMA((n,))],
    )(*shards)


def _swap_halves(grads):
    n = len(grads)

    def body(*refs):
        src = refs[:n]
        out = refs[n:2 * n]
        send_sems, recv_sems = refs[2 * n:]
        x, y, c = _place()
        copies = []
        for w in range(n):
            hr = grads[w].shape[1] // 2
            copies.append(pltpu.make_async_remote_copy(
                src_ref=src[w].at[:, pl.ds((1 - c) * hr, hr), :], dst_ref=out[w],
                send_sem=send_sems.at[w], recv_sem=recv_sems.at[w], device_id=(x, y, 1 - c), device_id_type=MESH))
        for cp in copies:
            cp.start()
        for cp in copies:
            cp.wait()

    return _pallas(
        body, name="grad_swap_halves", in_specs=[ANY] * n, out_specs=[ANY] * n,
        out_shape=[_sds((N_CHIPS, g.shape[1] // 2, g.shape[2]), g.dtype) for g in grads],
        scratch_shapes=[pltpu.SemaphoreType.DMA((n,)), pltpu.SemaphoreType.DMA((n,))],
    )(*grads)


def _add_half(g, got, c_arr, name):
    nk, r, cols = g.shape
    hr = r // 2
    tr = min(hr, 256)
    nb = hr // tr

    def body(c_ref, g_ref, got_ref, o_ref):
        o_ref[...] = (g_ref[...].astype(F32) + got_ref[...].astype(F32)).astype(o_ref.dtype)

    grid_spec = pltpu.PrefetchScalarGridSpec(
        num_scalar_prefetch=1, grid=(nk, nb),
        in_specs=[pl.BlockSpec((None, tr, cols), lambda k, i, c_ref: (k, c_ref[0] * nb + i, 0)),
                  pl.BlockSpec((None, tr, cols), lambda k, i, c_ref: (k, i, 0))],
        out_specs=pl.BlockSpec((None, tr, cols), lambda k, i, c_ref: (k, i, 0)))
    return _pallas(body, name=name, grid_spec=grid_spec, out_shape=_sds((nk, hr, cols), g.dtype),
                   compiler_params=_params("parallel", "parallel"))(c_arr, g, got)


def _send_partials(parts):
    n = len(parts)

    def body(*refs):
        src = refs[:n]
        out = refs[n:2 * n]
        send_sems, recv_sems = refs[2 * n:]
        x, y, c = _place()
        copies = []
        for w in range(n):
            for j, chip in enumerate(_other_chips(x, y)):
                copies.append(pltpu.make_async_remote_copy(
                    src_ref=src[w].at[2 * chip[0] + chip[1]], dst_ref=out[w].at[j],
                    send_sem=send_sems.at[w, j], recv_sem=recv_sems.at[w, j], device_id=(*chip, c), device_id_type=MESH))
        for cp in copies:
            cp.start()
        for cp in copies:
            cp.wait()

    return _pallas(
        body, name="grad_send_partials", in_specs=[ANY] * n, out_specs=[ANY] * n,
        out_shape=[_sds((N_CHIPS - 1, *p.shape[1:]), p.dtype) for p in parts],
        scratch_shapes=[pltpu.SemaphoreType.DMA((n, 3)), pltpu.SemaphoreType.DMA((n, 3))],
    )(*parts)


def _sum_partials(part, got, k_arr, name):
    _, hr, cols = part.shape
    tr = min(hr, 256)

    def body(k_ref, p_ref, g0_ref, g1_ref, g2_ref, o_ref):
        o_ref[...] = ((p_ref[...].astype(F32) + g0_ref[...].astype(F32)) + g1_ref[...].astype(F32)) + g2_ref[...].astype(F32)

    slot = lambda j: pl.BlockSpec((None, tr, cols), lambda i, k_ref: (j, i, 0))
    grid_spec = pltpu.PrefetchScalarGridSpec(
        num_scalar_prefetch=1, grid=(hr // tr,),
        in_specs=[pl.BlockSpec((None, tr, cols), lambda i, k_ref: (k_ref[0], i, 0)), slot(0), slot(1), slot(2)],
        out_specs=pl.BlockSpec((tr, cols), lambda i, k_ref: (i, 0)))
    return _pallas(body, name=name, grid_spec=grid_spec, out_shape=_sds((hr, cols), F32),
                   compiler_params=_params("parallel"))(k_arr, part, got, got, got)


def _share_halves(halves):
    n = len(halves)

    def body(*refs):
        src = refs[:n]
        out = refs[n:2 * n]
        send_sems, recv_sems, local_sems = refs[2 * n:]
        x, y, c = _place()
        copies, local = [], []
        for w in range(n):
            hr = halves[w].shape[0]
            mine = out[w].at[pl.ds(c * hr, hr), :]
            local.append(pltpu.make_async_copy(src[w], mine, local_sems.at[w]))
            copies.append(pltpu.make_async_remote_copy(
                src_ref=src[w], dst_ref=mine, send_sem=send_sems.at[w], recv_sem=recv_sems.at[w],
                device_id=(x, y, 1 - c), device_id_type=MESH))
        for cp in local + copies:
            cp.start()
        for cp in copies + local:
            cp.wait()

    return _pallas(
        body, name="grad_share_halves", in_specs=[ANY] * n, out_specs=[ANY] * n,
        out_shape=[_sds((2 * h.shape[0], h.shape[1]), h.dtype) for h in halves],
        scratch_shapes=[pltpu.SemaphoreType.DMA((n,)), pltpu.SemaphoreType.DMA((n,)), pltpu.SemaphoreType.DMA((n,))],
    )(*halves)


def _small_allreduce_adamw(g_part, w, m, v):
    rows = g_part.shape[0]

    def body(g_ref, w_ref, m_ref, v_ref, go_ref, d_ref, nm_ref, nv_ref, all_ref, send_sems, recv_sems):
        x, y, c = _place()
        me = 4 * x + 2 * y + c
        all_ref[me] = g_ref[...]
        copies = []
        for r in range(1, 8):
            dx, dy, dc = (r >> 2) & 1, (r >> 1) & 1, r & 1
            peer = (1 - x if dx else x, 1 - y if dy else y, 1 - c if dc else c)
            copies.append(pltpu.make_async_remote_copy(
                src_ref=g_ref, dst_ref=all_ref.at[me], send_sem=send_sems.at[r - 1], recv_sem=recv_sems.at[r - 1],
                device_id=peer, device_id_type=MESH))
        for cp in copies:
            cp.start()
        for cp in copies:
            cp.wait()
        tot = all_ref[0]
        for d in range(1, 8):
            tot = tot + all_ref[d]
        go_ref[...] = tot
        d_ref[...], nm_ref[...], nv_ref[...] = _adamw_math(w_ref[...], tot, m_ref[...], v_ref[...])

    vm = pl.BlockSpec(memory_space=pltpu.VMEM)
    return _pallas(
        body, name="small_allreduce_adamw", in_specs=[vm] * 4, out_specs=[vm] * 4,
        out_shape=[_sds((rows, 128), F32)] * 4,
        scratch_shapes=[pltpu.VMEM((8, rows, 128), F32), pltpu.SemaphoreType.DMA((7,)), pltpu.SemaphoreType.DMA((7,))],
    )(g_part, w, m, v)


SMALL_SIZES = (2048, 1024, 128, 128, HEADS * REL_SIZE, 2048)
SMALL_PART_ROWS = tuple(-(-size // 1024) * 8 for size in SMALL_SIZES)
SMALL_ROWS = sum(SMALL_PART_ROWS)


def _pack_small(parts):
    rows = []
    for p, size, nr in zip(parts, SMALL_SIZES, SMALL_PART_ROWS):
        rows.append(jnp.pad(p.reshape(-1), (0, nr * 128 - size)).reshape(nr, 128))
    return jnp.concatenate(rows, axis=0)


def _unpack_small(slab, shapes):
    out, off = [], 0
    for size, nr, shape in zip(SMALL_SIZES, SMALL_PART_ROWS, shapes):
        out.append(slab[off:off + nr].reshape(-1)[:size].reshape(shape))
        off += nr
    return out


def kernel(x, norm1_g, w_in, ret_norm_g, q_norm_g, k_norm_g, rel_bias, w_out, norm2_g, w_ff1, w_ff2, loss_target, m_norm1_g, m_w_in, m_ret_norm_g, m_q_norm_g, m_k_norm_g, m_rel_bias, m_w_out, m_norm2_g, m_w_ff1, m_w_ff2, v_norm1_g, v_w_in, v_ret_norm_g, v_q_norm_g, v_k_norm_g, v_rel_bias, v_w_out, v_norm2_g, v_w_ff1, v_w_ff2):
    xs = x[0]
    tgt = loss_target[0]
    s, d = xs.shape
    d_in = N_CHIPS * w_in.shape[2]
    d_ff = N_CHIPS * w_ff1.shape[2]
    in_sh, ff_sh = w_in.shape[2], w_ff1.shape[2]
    tm = min(s, 1024)
    gi = s // tm
    c_arr = lax.axis_index("c").astype(jnp.int32).reshape(1)
    k_arr = (2 * lax.axis_index("x") + lax.axis_index("y")).astype(jnp.int32).reshape(1)
    tables = _tables(s)
    bias = _rel_bias_expand(rel_bias[0])

    shards = [_cast_bf16(w_in[0], "cast_w_in"), _cast_bf16(w_out[0], "cast_w_out"),
              _cast_bf16(w_ff1[0], "cast_w_ff1"), _cast_bf16(w_ff2[0], "cast_w_ff2")]
    wg_in, wg_out, wg_ff1, wg_ff2 = _all_gather_weights(shards)
    wg_out = wg_out.reshape(d, d)
    wg_ff2 = wg_ff2.reshape(d_ff, d)

    h1 = _rmsnorm_fwd(xs, norm1_g, "rmsnorm1")
    tn_in = in_sh // 2
    tk = 512
    (proj,) = _mm("proj", h1, wg_in, NN, (gi, 2 * N_CHIPS, d // tk),
                  pl.BlockSpec((tm, tk), lambda i, j, k: (i, k)),
                  pl.BlockSpec((None, tk, tn_in), lambda i, j, k: (j // 2, k, j % 2)),
                  [_sds((s, d_in), F32)], [pl.BlockSpec((tm, tn_in), lambda i, j, k: (i, j))], (tm, tn_in))
    ret, y_ret, prev = _retention_fwd(proj, ret_norm_g, tables)
    att = _attention_fwd(proj, q_norm_g, k_norm_g, bias)
    mix = jnp.concatenate([ret, att], axis=-1)
    tn = 1024
    tile = pl.BlockSpec((tm, tn), lambda i, j, k: (i, j))
    (x1,) = _mm("out_proj", mix, wg_out, NN, (gi, d // tn, d // tk),
                pl.BlockSpec((tm, tk), lambda i, j, k: (i, k)), pl.BlockSpec((tk, tn), lambda i, j, k: (k, j)),
                [_sds((s, d), F32)], [tile], (tm, tn), extras=(xs,), extra_specs=(tile,),
                epi=lambda acc, r: (r + acc,))
    h2 = _rmsnorm_fwd(x1, norm2_g, "rmsnorm2")
    tn_ff = min(ff_sh, 1024)
    per = ff_sh // tn_ff

    def relu2(acc):
        r = jnp.maximum(acc, 0.0)
        return acc, r * r

    u, act = _mm("ff1", h2, wg_ff1, NN, (gi, N_CHIPS * per, d // tk),
                 pl.BlockSpec((tm, tk), lambda i, j, k: (i, k)),
                 pl.BlockSpec((None, tk, tn_ff), lambda i, j, k: (j // per, k, j % per)),
                 [_sds((s, d_ff), F32), _sds((s, d_ff), BF16)],
                 [pl.BlockSpec((tm, tn_ff), lambda i, j, k: (i, j))] * 2, (tm, tn_ff), epi=relu2)

    def loss_epi(acc, res, t):
        diff = (res + acc) - t
        dy = diff / d
        return dy, dy, jnp.sum(diff * diff, axis=0, keepdims=True)

    dy, dyb, loss_cols = _mm(
        "ff2_loss", act, wg_ff2, NN, (gi, d // tn, d_ff // tk),
        pl.BlockSpec((tm, tk), lambda i, j, k: (i, k)), pl.BlockSpec((tk, tn), lambda i, j, k: (k, j)),
        [_sds((s, d), F32), _sds((s, d), BF16), _sds((gi, 1, d), F32)],
        [tile, tile, pl.BlockSpec((None, 1, tn), lambda i, j, k: (i, 0, j))], (tm, tn),
        extras=(x1, tgt), extra_specs=(tile, tile), epi=loss_epi)
    loss = lax.psum(0.5 * jnp.sum(loss_cols) / d, ("x", "y", "c"))

    (du,) = _mm("d_act", dyb, wg_ff2, NT, (gi, d_ff // tn, d // tk),
                pl.BlockSpec((tm, tk), lambda i, j, k: (i, k)), pl.BlockSpec((tn, tk), lambda i, j, k: (j, k)),
                [_sds((s, d_ff), BF16)], [tile], (tm, tn), extras=(u,), extra_specs=(tile,),
                epi=lambda acc, uu: (acc * (2.0 * jnp.maximum(uu, 0.0)),))
    ts = min(s, 512)
    wtile = pl.BlockSpec((tn, tn), lambda i, j, k: (i, j))
    (g_ff2,) = _mm("dw_ff2", act, dyb, TN, (d_ff // tn, d // tn, s // ts),
                   pl.BlockSpec((ts, tn), lambda i, j, k: (k, i)), pl.BlockSpec((ts, tn), lambda i, j, k: (k, j)),
                   [_sds((d_ff, d), BF16)], [wtile], (tn, tn))
    (g_ff1,) = _mm("dw_ff1", h2, du, TN, (d // tn, N_CHIPS * per, s // ts),
                   pl.BlockSpec((ts, tn), lambda i, j, k: (k, i)), pl.BlockSpec((ts, tn_ff), lambda i, j, k: (k, j)),
                   [_sds((N_CHIPS, d, ff_sh), BF16)],
                   [pl.BlockSpec((None, tn, tn_ff), lambda i, j, k: (j // per, i, j % per))], (tn, tn_ff))
    kper = ff_sh // tk
    (dh2,) = _mm("d_h2", du, wg_ff1, NT, (gi, d // tn, d_ff // tk),
                 pl.BlockSpec((tm, tk), lambda i, j, k: (i, k)),
                 pl.BlockSpec((None, tn, tk), lambda i, j, k: (k // kper, j, k % kper)),
                 [_sds((s, d), F32)], [tile], (tm, tn))
    dx1, dx1b, g_norm2 = _rmsnorm_bwd(x1, norm2_g, dh2, dy, "rmsnorm2_bwd")

    (dmix,) = _mm("d_mix", dx1b, wg_out, NT, (gi, d // tn, d // tk),
                  pl.BlockSpec((tm, tk), lambda i, j, k: (i, k)), pl.BlockSpec((tn, tk), lambda i, j, k: (j, k)),
                  [_sds((s, d), F32)], [tile], (tm, tn))
    (g_out,) = _mm("dw_out", mix, dx1b, TN, (d // tn, d // tn, s // ts),
                   pl.BlockSpec((ts, tn), lambda i, j, k: (k, i)), pl.BlockSpec((ts, tn), lambda i, j, k: (k, j)),
                   [_sds((d, d), BF16)], [wtile], (tn, tn))
    d_rq, d_rk, d_rv, d_rg, g_gn = _retention_bwd(proj, ret_norm_g, tables, y_ret, prev, dmix)
    d_aq, d_ak, d_av, g_gq, g_gk, dbias = _attention_bwd(proj, q_norm_g, k_norm_g, bias, dmix)
    g_rel = _rel_bias_fold(dbias)
    dproj = jnp.concatenate([d_rq, d_rk, d_rv, d_rg, d_aq, d_ak, d_av], axis=-1)
    (g_in,) = _mm("dw_in", h1, dproj, TN, (d // tn, 2 * N_CHIPS, s // ts),
                  pl.BlockSpec((ts, tn), lambda i, j, k: (k, i)), pl.BlockSpec((ts, tn_in), lambda i, j, k: (k, j)),
                  [_sds((N_CHIPS, d, in_sh), BF16)],
                  [pl.BlockSpec((None, tn, tn_in), lambda i, j, k: (j // 2, i, j % 2))], (tn, tn_in))
    (dh1,) = _mm("d_h1", dproj, wg_in, NT, (gi, d // tn, 2 * N_CHIPS),
                 pl.BlockSpec((tm, tn_in), lambda i, j, k: (i, k)),
                 pl.BlockSpec((None, tn, tn_in), lambda i, j, k: (k // 2, j, k % 2)),
                 [_sds((s, d), F32)], [tile], (tm, tn))
    grad_x, _, g_norm1 = _rmsnorm_bwd(xs, norm1_g, dh1, dx1, "rmsnorm1_bwd")

    full = [g_in, g_out.reshape(N_CHIPS, d // N_CHIPS, d), g_ff1, g_ff2.reshape(N_CHIPS, d_ff // N_CHIPS, d)]
    names = ["w_in", "w_out", "w_ff1", "w_ff2"]
    got = _swap_halves(full)
    parts = [_add_half(g, r, c_arr, "chip_partial_" + nm) for g, r, nm in zip(full, got, names)]
    got2 = _send_partials(parts)
    halves = [_sum_partials(p, r, k_arr, "sum_partials_" + nm) for p, r, nm in zip(parts, got2, names)]
    g_big = _share_halves(halves)
    big = []
    for g, w, m, v, nm in zip(g_big, (w_in, w_out, w_ff1, w_ff2), (m_w_in, m_w_out, m_w_ff1, m_w_ff2),
                              (v_w_in, v_w_out, v_w_ff1, v_w_ff2), names):
        delta, new_m, new_v = _adamw(w[0], g, m[0], v[0], "adamw_" + nm)
        big.append((g[None], delta[None], new_m[None], new_v[None]))

    small_w = (norm1_g, ret_norm_g, q_norm_g, k_norm_g, rel_bias, norm2_g)
    small_m = (m_norm1_g, m_ret_norm_g, m_q_norm_g, m_k_norm_g, m_rel_bias, m_norm2_g)
    small_v = (v_norm1_g, v_ret_norm_g, v_q_norm_g, v_k_norm_g, v_rel_bias, v_norm2_g)
    shapes = [p.shape for p in small_w]
    g_small = _pack_small([g_norm1, g_gn, g_gq, g_gk, g_rel, g_norm2])
    sg, sd, sm, sv = (_unpack_small(a, shapes) for a in _small_allreduce_adamw(
        g_small, _pack_small(small_w), _pack_small(small_m), _pack_small(small_v)))

    def ordered(kind):
        sm_ = (sg, sd, sm, sv)[kind]
        return (sm_[0], big[0][kind], sm_[1], sm_[2], sm_[3], sm_[4], big[1][kind], sm_[5], big[2][kind], big[3][kind])

    return (loss, grad_x[None], *ordered(0), *ordered(1), *ordered(2), *ordered(3))
```

```python
import functools

import jax
import jax.numpy as jnp
from jax import lax
from jax.experimental import pallas as pl
from jax.experimental.pallas import tpu as pltpu

F32 = jnp.float32
BF16 = jnp.bfloat16
MXU_DTYPE = jnp.bfloat16

CHUNK = 64
HEADS = 8
HEAD_DIM = 128
LEFT_CHUNKS = 8
BAND = (LEFT_CHUNKS + 1) * CHUNK
REL_CLIP = 128
REL_SIZE = (CHUNK - 1) + REL_CLIP + 1
ROPE_BASE = 10000.0
EPS = 1e-6
GN_EPS = 1e-5
ADAM_LR, ADAM_B1, ADAM_B2, ADAM_EPS, ADAM_WD, ADAM_STEP = 0.001, 0.9, 0.999, 1e-08, 0.01, 10
N_CHIPS = 4
VMEM_LIMIT = 56 * 1024 * 1024
MESH = pl.DeviceIdType.MESH
ANY = pl.BlockSpec(memory_space=pl.ANY)

NN = (((1,), (0,)), ((), ()))
NT = (((1,), (1,)), ((), ()))
TN = (((0,), (0,)), ((), ()))


def _pallas(body, **kw):
    return pl.pallas_call(body, **kw)


def _params(*sem):
    return pltpu.CompilerParams(dimension_semantics=sem, vmem_limit_bytes=VMEM_LIMIT)


def _dot(a, b, dims):
    return lax.dot_general(a.astype(MXU_DTYPE), b.astype(MXU_DTYPE), dims, preferred_element_type=F32)


def _mm(name, a, b, dims, grid, a_spec, b_spec, outs, o_specs, acc_shape, extras=(), extra_specs=(), epi=None):
    nk = grid[2]
    n_ex, n_out = len(extras), len(outs)

    def body(*refs):
        a_ref, b_ref = refs[0], refs[1]
        ex_refs = refs[2:2 + n_ex]
        o_refs = refs[2 + n_ex:2 + n_ex + n_out]
        acc_ref = refs[-1]
        k = pl.program_id(2)

        @pl.when(k == 0)
        def _():
            acc_ref[...] = jnp.zeros_like(acc_ref)

        acc_ref[...] += _dot(a_ref[...], b_ref[...], dims)

        @pl.when(k == nk - 1)
        def _():
            acc = acc_ref[...]
            vals = epi(acc, *[r[...] for r in ex_refs]) if epi is not None else (acc,)
            for r, v in zip(o_refs, vals):
                r[...] = v.astype(r.dtype)

    return _pallas(
        body, name=name, grid=grid, in_specs=[a_spec, b_spec, *extra_specs], out_specs=list(o_specs),
        out_shape=list(outs), scratch_shapes=[pltpu.VMEM(acc_shape, F32)],
        compiler_params=_params("parallel", "parallel", "arbitrary"),
    )(a, b, *extras)


def _sds(shape, dtype):
    return jax.ShapeDtypeStruct(shape, dtype)


def _cast_bf16(w, k_arr, name):
    r, c = w.shape
    tr = min(r, 256)

    def body(k_ref, w_ref, o_ref):
        o_ref[...] = w_ref[...].astype(BF16)

    grid_spec = pltpu.PrefetchScalarGridSpec(
        num_scalar_prefetch=1, grid=(r // tr,), in_specs=[pl.BlockSpec((tr, c), lambda i, k_ref: (i, 0))],
        out_specs=pl.BlockSpec((None, tr, c), lambda i, k_ref: (k_ref[0], i, 0)))
    return _pallas(body, name=name, grid_spec=grid_spec, out_shape=_sds((N_CHIPS, r, c), BF16),
                   compiler_params=_params("parallel"))(k_arr, w)


def _rmsnorm_fwd(x, g, name):
    s, d = x.shape
    tr = 256

    def body(x_ref, g_ref, o_ref):
        xv = x_ref[...]
        y = xv * lax.rsqrt(jnp.mean(xv * xv, axis=-1, keepdims=True) + EPS)
        o_ref[...] = (y * g_ref[...]).astype(o_ref.dtype)

    return _pallas(body, name=name, grid=(s // tr,),
                   in_specs=[pl.BlockSpec((tr, d), lambda i: (i, 0)), pl.BlockSpec((1, d), lambda i: (0, 0))],
                   out_specs=pl.BlockSpec((tr, d), lambda i: (i, 0)), out_shape=_sds((s, d), BF16),
                   compiler_params=_params("parallel"))(x, g)


def _rmsnorm_bwd(x, g, dh, res, name):
    s, d = x.shape
    tr = 256

    def body(x_ref, g_ref, dh_ref, res_ref, dx_ref, dxb_ref, dg_ref):
        i = pl.program_id(0)
        xv = x_ref[...]
        rstd = lax.rsqrt(jnp.mean(xv * xv, axis=-1, keepdims=True) + EPS)
        xh = xv * rstd
        dhv = dh_ref[...]

        @pl.when(i == 0)
        def _():
            dg_ref[...] = jnp.zeros_like(dg_ref)

        dg_ref[...] += jnp.sum(dhv * xh, axis=0, keepdims=True)
        dxh = dhv * g_ref[...]
        dx = res_ref[...] + rstd * (dxh - xh * jnp.mean(dxh * xh, axis=-1, keepdims=True))
        dx_ref[...] = dx
        dxb_ref[...] = dx.astype(BF16)

    row = pl.BlockSpec((tr, d), lambda i: (i, 0))
    vec = pl.BlockSpec((1, d), lambda i: (0, 0))
    return _pallas(body, name=name, grid=(s // tr,), in_specs=[row, vec, row, row], out_specs=[row, row, vec],
                   out_shape=[_sds((s, d), F32), _sds((s, d), BF16), _sds((1, d), F32)],
                   compiler_params=_params("arbitrary"))(x, g, dh, res)


def _adamw_math(w, g, m, v):
    m = ADAM_B1 * m + (1.0 - ADAM_B1) * g
    v = ADAM_B2 * v + (1.0 - ADAM_B2) * (g * g)
    m_hat = m / (1.0 - ADAM_B1 ** ADAM_STEP)
    v_hat = v / (1.0 - ADAM_B2 ** ADAM_STEP)
    delta = -ADAM_LR * (m_hat / (jnp.sqrt(v_hat) + ADAM_EPS) + ADAM_WD * w)
    return delta, m, v


def _adamw(w, g, m, v, name):
    r, c = w.shape
    tr = 128

    def body(w_ref, g_ref, m_ref, v_ref, d_ref, nm_ref, nv_ref):
        d_ref[...], nm_ref[...], nv_ref[...] = _adamw_math(w_ref[...], g_ref[...], m_ref[...], v_ref[...])

    blk = pl.BlockSpec((tr, c), lambda i: (i, 0))
    return _pallas(body, name=name, grid=(r // tr,), in_specs=[blk] * 4, out_specs=[blk] * 3,
                   out_shape=[_sds((r, c), F32)] * 3, compiler_params=_params("parallel"))(w, g, m, v)


def _tables(s):
    half = HEAD_DIM // 2
    pos = jnp.arange(s, dtype=F32)
    inv_freq = ROPE_BASE ** (-jnp.arange(half, dtype=F32) / half)
    ang = pos[:, None] * inv_freq[None, :]
    cos, sin = jnp.cos(ang), jnp.sin(ang)
    cos_f = jnp.concatenate([cos, cos], axis=-1)
    sin_f = jnp.concatenate([-sin, sin], axis=-1)
    log_g = jnp.log1p(-jnp.exp2(-(5.0 + jnp.arange(HEADS, dtype=F32))))
    p = jnp.arange(CHUNK, dtype=F32)
    decay = jnp.exp(log_g[:, None, None] * jnp.abs(p[:, None] - p[None, :]))
    k_dec = jnp.exp(log_g[None, :] * (CHUNK - 1.0 - p)[:, None])
    q_dec = jnp.exp(log_g[None, :] * (p + 1.0)[:, None])
    c_dec = jnp.exp(log_g * CHUNK)
    k_dec = jnp.broadcast_to(k_dec.T[:, :, None], (HEADS, CHUNK, HEAD_DIM))
    q_dec = jnp.broadcast_to(q_dec.T[:, :, None], (HEADS, CHUNK, HEAD_DIM))
    c_dec = jnp.broadcast_to(c_dec[:, None, None], (HEADS, 1, HEAD_DIM))
    return cos_f, sin_f, decay, k_dec, q_dec, c_dec


def _rot(x, cos_f, sin_f):
    return x * cos_f + pltpu.roll(x, HEAD_DIM // 2, 1) * sin_f


def _rot_bwd(d, cos_f, sin_f):
    return d * cos_f + pltpu.roll(d * sin_f, HEAD_DIM // 2, 1)


RET_BLOCK_CHUNKS = 8
RET_ROWS = RET_BLOCK_CHUNKS * CHUNK
K_SCALE = HEAD_DIM ** -0.5


def _retention_fwd(proj, gn_g, tables):
    s = proj.shape[0]
    nb = s // RET_ROWS
    nc = s // CHUNK
    cos_f, sin_f, decay, k_dec, q_dec, c_dec = tables

    def body(q_ref, k_ref, v_ref, g_ref, cos_ref, sin_ref, dec_ref, kd_ref, qd_ref, cd_ref, gn_ref,
             ret_ref, y_ref, prev_ref, state_ref):
        @pl.when(pl.program_id(1) == 0)
        def _():
            state_ref[...] = jnp.zeros_like(state_ref)

        cosv, sinv = cos_ref[...], sin_ref[...]
        q = _rot(q_ref[...], cosv, sinv)
        k = _rot(k_ref[...], cosv, sinv) * K_SCALE
        v = v_ref[...]
        rg = g_ref[...]
        dec, kd, qd, cd, gn = dec_ref[...], kd_ref[...], qd_ref[...], cd_ref[...], gn_ref[...]
        state = state_ref[...]
        for c in range(RET_BLOCK_CHUNKS):
            rows = slice(c * CHUNK, (c + 1) * CHUNK)
            qc, kc, vc = q[rows], k[rows], v[rows]
            sc = _dot(qc, kc, NT) * dec
            intra = _dot(sc, vc, NN)
            prev_ref[c] = state.astype(prev_ref.dtype)
            cross = _dot(qc * qd, state, NN)
            contrib = _dot(kc * kd, vc, TN)
            state = cd * state + contrib
            y = intra + cross
            y_ref[rows, :] = y
            mu = jnp.mean(y, axis=-1, keepdims=True)
            yc = y - mu
            var = jnp.mean(yc * yc, axis=-1, keepdims=True)
            yn = yc * lax.rsqrt(var + GN_EPS) * gn
            rgc = rg[rows]
            ret_ref[rows, :] = (rgc * jax.nn.sigmoid(rgc) * yn).astype(ret_ref.dtype)
        state_ref[...] = state

    def col(off):
        return pl.BlockSpec((RET_ROWS, HEAD_DIM), lambda h, i: (i, off + h))

    pos = pl.BlockSpec((RET_ROWS, HEAD_DIM), lambda h, i: (i, 0))
    per_head = lambda shape: pl.BlockSpec((None, *shape), lambda h, i: (h, 0, 0))
    return _pallas(
        body, name="retention_fwd", grid=(HEADS, nb),
        in_specs=[col(0), col(HEADS), col(2 * HEADS), col(3 * HEADS), pos, pos,
                  per_head((CHUNK, CHUNK)), per_head((CHUNK, HEAD_DIM)), per_head((CHUNK, HEAD_DIM)),
                  per_head((1, HEAD_DIM)), pl.BlockSpec((1, HEAD_DIM), lambda h, i: (0, h))],
        out_specs=[col(0), col(0),
                   pl.BlockSpec((None, RET_BLOCK_CHUNKS, HEAD_DIM, HEAD_DIM), lambda h, i: (h, i, 0, 0))],
        out_shape=[_sds((s, HEADS * HEAD_DIM), BF16), _sds((s, HEADS * HEAD_DIM), F32),
                   _sds((HEADS, nc, HEAD_DIM, HEAD_DIM), MXU_DTYPE)],
        scratch_shapes=[pltpu.VMEM((HEAD_DIM, HEAD_DIM), F32)],
        compiler_params=_params("parallel", "arbitrary"),
    )(proj, proj, proj, proj, cos_f, sin_f, decay, k_dec, q_dec, c_dec, gn_g)


def _retention_bwd(proj, gn_g, tables, y, prev, dmix):
    s = proj.shape[0]
    nb = s // RET_ROWS
    cos_f, sin_f, decay, k_dec, q_dec, c_dec = tables

    def body(q_ref, k_ref, v_ref, g_ref, cos_ref, sin_ref, dec_ref, kd_ref, qd_ref, cd_ref, gn_ref,
             y_ref, prev_ref, dret_ref, dq_ref, dk_ref, dv_ref, dg_ref, dgn_ref, gstate_ref):
        @pl.when(pl.program_id(1) == 0)
        def _():
            gstate_ref[...] = jnp.zeros_like(gstate_ref)
            dgn_ref[...] = jnp.zeros_like(dgn_ref)

        cosv, sinv = cos_ref[...], sin_ref[...]
        q = _rot(q_ref[...], cosv, sinv)
        k = _rot(k_ref[...], cosv, sinv) * K_SCALE
        v = v_ref[...]
        dec, kd, qd, cd, gn = dec_ref[...], kd_ref[...], qd_ref[...], cd_ref[...], gn_ref[...]
        rg = g_ref[...]
        yv = y_ref[...]
        dret = dret_ref[...]
        sig = jax.nn.sigmoid(rg)
        gate = rg * sig
        mu = jnp.mean(yv, axis=-1, keepdims=True)
        yc = yv - mu
        rstd = lax.rsqrt(jnp.mean(yc * yc, axis=-1, keepdims=True) + GN_EPS)
        z = yc * rstd
        dyn = dret * gate
        dg_ref[...] = (dret * (z * gn) * (sig * (1.0 + rg * (1.0 - sig)))).astype(dg_ref.dtype)
        dgn_ref[...] += jnp.sum(dyn * z, axis=0, keepdims=True)
        dz = dyn * gn
        dy = rstd * (dz - jnp.mean(dz, axis=-1, keepdims=True) - z * jnp.mean(dz * z, axis=-1, keepdims=True))
        gst = gstate_ref[...]
        for c in reversed(range(RET_BLOCK_CHUNKS)):
            rows = slice(c * CHUNK, (c + 1) * CHUNK)
            qc, kc, vc, dyc = q[rows], k[rows], v[rows], dy[rows]
            sc = _dot(qc, kc, NT) * dec
            dp = _dot(dyc, vc, NT)
            dvc = _dot(sc, dyc, TN)
            ds = dp * dec
            dqc = _dot(ds, kc, NN)
            dkc = _dot(ds, qc, TN)
            prevc = prev_ref[c]
            dqc += _dot(dyc, prevc, NT) * qd
            dprev = _dot(qc * qd, dyc, TN)
            dkc += _dot(vc, gst, NT) * kd
            dvc += _dot(kc * kd, gst, NN)
            gst = dprev + cd * gst
            dq_ref[rows, :] = _rot_bwd(dqc, cosv[rows], sinv[rows]).astype(dq_ref.dtype)
            dk_ref[rows, :] = _rot_bwd(dkc * K_SCALE, cosv[rows], sinv[rows]).astype(dk_ref.dtype)
            dv_ref[rows, :] = dvc.astype(dv_ref.dtype)
        gstate_ref[...] = gst

    rev = lambda i: nb - 1 - i

    def col(off):
        return pl.BlockSpec((RET_ROWS, HEAD_DIM), lambda h, i: (rev(i), off + h))

    pos = pl.BlockSpec((RET_ROWS, HEAD_DIM), lambda h, i: (rev(i), 0))
    per_head = lambda shape: pl.BlockSpec((None, *shape), lambda h, i: (h, 0, 0))
    outb = _sds((s, HEADS * HEAD_DIM), BF16)
    return _pallas(
        body, name="retention_bwd", grid=(HEADS, nb),
        in_specs=[col(0), col(HEADS), col(2 * HEADS), col(3 * HEADS), pos, pos,
                  per_head((CHUNK, CHUNK)), per_head((CHUNK, HEAD_DIM)), per_head((CHUNK, HEAD_DIM)),
                  per_head((1, HEAD_DIM)), pl.BlockSpec((1, HEAD_DIM), lambda h, i: (0, h)),
                  col(0), pl.BlockSpec((None, RET_BLOCK_CHUNKS, HEAD_DIM, HEAD_DIM), lambda h, i: (h, rev(i), 0, 0)),
                  col(0)],
        out_specs=[col(0), col(0), col(0), col(0), per_head((1, HEAD_DIM))],
        out_shape=[outb, outb, outb, outb, _sds((HEADS, 1, HEAD_DIM), F32)],
        scratch_shapes=[pltpu.VMEM((HEAD_DIM, HEAD_DIM), F32)],
        compiler_params=_params("parallel", "arbitrary"),
    )(proj, proj, proj, proj, cos_f, sin_f, decay, k_dec, q_dec, c_dec, gn_g, y, prev, dmix)


ATT_COL0 = 4 * HEADS
PAD_ROWS = LEFT_CHUNKS * CHUNK
NORM_ROWS = 512


def _qk_norm(x, g):
    return x * lax.rsqrt(jnp.mean(x * x, axis=-1, keepdims=True) + EPS) * g


def _band_probs(qb, kb, bias, n):
    sc = _dot(qb, kb, NT) * K_SCALE + bias
    band_chunk = lax.broadcasted_iota(jnp.int32, (CHUNK, BAND), 1) // CHUNK
    sc = jnp.where(n - LEFT_CHUNKS + band_chunk >= 0, sc, -1e30)
    e = jnp.exp(sc - jnp.max(sc, axis=-1, keepdims=True))
    return e / jnp.sum(e, axis=-1, keepdims=True)


def _attention_fwd(proj, gq, gk, bias):
    s = proj.shape[0]
    nc = s // CHUNK

    def body(q_ref, k_ref, v_ref, gq_ref, gk_ref, bias_ref, o_ref, kp_ref, vp_ref):
        kp_ref[0:PAD_ROWS, :] = jnp.zeros((PAD_ROWS, HEAD_DIM), kp_ref.dtype)
        vp_ref[0:PAD_ROWS, :] = jnp.zeros((PAD_ROWS, HEAD_DIM), vp_ref.dtype)
        gqv, gkv = gq_ref[...], gk_ref[...]

        def fill(b, carry):
            r0 = pl.multiple_of(b * NORM_ROWS, NORM_ROWS)
            kp_ref[pl.ds(PAD_ROWS + r0, NORM_ROWS), :] = _qk_norm(k_ref[pl.ds(r0, NORM_ROWS), :], gkv).astype(kp_ref.dtype)
            vp_ref[pl.ds(PAD_ROWS + r0, NORM_ROWS), :] = v_ref[pl.ds(r0, NORM_ROWS), :].astype(vp_ref.dtype)
            return carry

        lax.fori_loop(0, s // NORM_ROWS, fill, 0)
        biasv = bias_ref[...]

        def chunk(n, carry):
            r0 = pl.multiple_of(n * CHUNK, CHUNK)
            qn = _qk_norm(q_ref[pl.ds(r0, CHUNK), :], gqv)
            p = _band_probs(qn, kp_ref[pl.ds(r0, BAND), :], biasv, n)
            o_ref[pl.ds(r0, CHUNK), :] = _dot(p, vp_ref[pl.ds(r0, BAND), :], NN).astype(o_ref.dtype)
            return carry

        lax.fori_loop(0, nc, chunk, 0)

    def col(off):
        return pl.BlockSpec((s, HEAD_DIM), lambda h: (0, off + h))

    vec = pl.BlockSpec((1, HEAD_DIM), lambda h: (0, 0))
    return _pallas(
        body, name="attention_fwd", grid=(HEADS,),
        in_specs=[col(ATT_COL0), col(ATT_COL0 + HEADS), col(ATT_COL0 + 2 * HEADS), vec, vec,
                  pl.BlockSpec((None, CHUNK, BAND), lambda h: (h, 0, 0))],
        out_specs=col(0), out_shape=_sds((s, HEADS * HEAD_DIM), BF16),
        scratch_shapes=[pltpu.VMEM((s + PAD_ROWS, HEAD_DIM), MXU_DTYPE), pltpu.VMEM((s + PAD_ROWS, HEAD_DIM), MXU_DTYPE)],
        compiler_params=_params("parallel"),
    )(proj, proj, proj, gq, gk, bias)


def _attention_bwd(proj, gq, gk, bias, dmix):
    s = proj.shape[0]
    nc = s // CHUNK

    def body(q_ref, k_ref, v_ref, gq_ref, gk_ref, bias_ref, do_ref,
             dq_ref, dk_ref, dv_ref, dgq_ref, dgk_ref, dbias_ref, kp_ref, vp_ref, dkp_ref, dvp_ref, dqn_ref):
        kp_ref[0:PAD_ROWS, :] = jnp.zeros((PAD_ROWS, HEAD_DIM), kp_ref.dtype)
        vp_ref[0:PAD_ROWS, :] = jnp.zeros((PAD_ROWS, HEAD_DIM), vp_ref.dtype)
        dkp_ref[...] = jnp.zeros_like(dkp_ref)
        dvp_ref[...] = jnp.zeros_like(dvp_ref)
        dbias_ref[...] = jnp.zeros_like(dbias_ref)
        gqv, gkv = gq_ref[...], gk_ref[...]

        def fill(b, carry):
            r0 = pl.multiple_of(b * NORM_ROWS, NORM_ROWS)
            kp_ref[pl.ds(PAD_ROWS + r0, NORM_ROWS), :] = _qk_norm(k_ref[pl.ds(r0, NORM_ROWS), :], gkv).astype(kp_ref.dtype)
            vp_ref[pl.ds(PAD_ROWS + r0, NORM_ROWS), :] = v_ref[pl.ds(r0, NORM_ROWS), :].astype(vp_ref.dtype)
            return carry

        lax.fori_loop(0, s // NORM_ROWS, fill, 0)
        biasv = bias_ref[...]

        def chunk(n, carry):
            r0 = pl.multiple_of(n * CHUNK, CHUNK)
            qn = _qk_norm(q_ref[pl.ds(r0, CHUNK), :], gqv)
            kb = kp_ref[pl.ds(r0, BAND), :]
            vb = vp_ref[pl.ds(r0, BAND), :]
            p = _band_probs(qn, kb, biasv, n)
            do = do_ref[pl.ds(r0, CHUNK), :]
            dvp_ref[pl.ds(r0, BAND), :] += _dot(p, do, TN)
            dp = _dot(do, vb, NT)
            ds = p * (dp - jnp.sum(dp * p, axis=-1, keepdims=True))
            dbias_ref[...] += ds
            dss = ds * K_SCALE
            dqn_ref[pl.ds(r0, CHUNK), :] = _dot(dss, kb, NN)
            dkp_ref[pl.ds(r0, BAND), :] += _dot(dss, qn, TN)
            return carry

        lax.fori_loop(0, nc, chunk, 0)

        @pl.when(pl.program_id(0) == 0)
        def _():
            dgq_ref[...] = jnp.zeros_like(dgq_ref)
            dgk_ref[...] = jnp.zeros_like(dgk_ref)

        def norm_bwd(x, g, dn):
            rstd = lax.rsqrt(jnp.mean(x * x, axis=-1, keepdims=True) + EPS)
            xh = x * rstd
            dxh = dn * g
            return rstd * (dxh - xh * jnp.mean(dxh * xh, axis=-1, keepdims=True)), jnp.sum(dn * xh, axis=0, keepdims=True)

        def finish(b, carry):
            r0 = pl.multiple_of(b * NORM_ROWS, NORM_ROWS)
            rows = pl.ds(r0, NORM_ROWS)
            dq, dgq = norm_bwd(q_ref[rows, :], gqv, dqn_ref[rows, :])
            dk, dgk = norm_bwd(k_ref[rows, :], gkv, dkp_ref[pl.ds(PAD_ROWS + r0, NORM_ROWS), :])
            dq_ref[rows, :] = dq.astype(dq_ref.dtype)
            dk_ref[rows, :] = dk.astype(dk_ref.dtype)
            dv_ref[rows, :] = dvp_ref[pl.ds(PAD_ROWS + r0, NORM_ROWS), :].astype(dv_ref.dtype)
            dgq_ref[...] += dgq
            dgk_ref[...] += dgk
            return carry

        lax.fori_loop(0, s // NORM_ROWS, finish, 0)

    def col(off):
        return pl.BlockSpec((s, HEAD_DIM), lambda h: (0, off + h))

    vec = pl.BlockSpec((1, HEAD_DIM), lambda h: (0, 0))
    hbias = pl.BlockSpec((None, CHUNK, BAND), lambda h: (h, 0, 0))
    outb = _sds((s, HEADS * HEAD_DIM), BF16)
    return _pallas(
        body, name="attention_bwd", grid=(HEADS,),
        in_specs=[col(ATT_COL0), col(ATT_COL0 + HEADS), col(ATT_COL0 + 2 * HEADS), vec, vec, hbias, col(HEADS)],
        out_specs=[col(0), col(0), col(0), vec, vec, hbias],
        out_shape=[outb, outb, outb, _sds((1, HEAD_DIM), F32), _sds((1, HEAD_DIM), F32),
                   _sds((HEADS, CHUNK, BAND), F32)],
        scratch_shapes=[pltpu.VMEM((s + PAD_ROWS, HEAD_DIM), MXU_DTYPE), pltpu.VMEM((s + PAD_ROWS, HEAD_DIM), MXU_DTYPE),
                        pltpu.VMEM((s + PAD_ROWS, HEAD_DIM), F32), pltpu.VMEM((s + PAD_ROWS, HEAD_DIM), F32),
                        pltpu.VMEM((s, HEAD_DIM), F32)],
        compiler_params=_params("arbitrary"),
    )(proj, proj, proj, gq, gk, bias, dmix)


def _rel_distance():
    qi = lax.broadcasted_iota(jnp.int32, (CHUNK, BAND), 0)
    kj = lax.broadcasted_iota(jnp.int32, (CHUNK, BAND), 1)
    return jnp.clip(qi + LEFT_CHUNKS * CHUNK - kj, -(CHUNK - 1), REL_CLIP) + (CHUNK - 1)


def _rel_bias_expand(rel_bias):
    def body(rb_ref, o_ref):
        h = pl.program_id(0)
        rel = _rel_distance()
        o_ref[...] = lax.fori_loop(0, REL_SIZE, lambda r, acc: jnp.where(rel == r, rb_ref[h, r], acc),
                                   jnp.zeros((CHUNK, BAND), F32))

    return _pallas(body, name="rel_bias_expand", grid=(HEADS,), in_specs=[pl.BlockSpec(memory_space=pltpu.SMEM)],
                   out_specs=pl.BlockSpec((None, CHUNK, BAND), lambda h: (h, 0, 0)),
                   out_shape=_sds((HEADS, CHUNK, BAND), F32), compiler_params=_params("parallel"))(rel_bias)


def _rel_bias_fold(dbias):
    def body(a_ref, o_ref):
        a = a_ref[...]
        rel = _rel_distance()
        lane = lax.broadcasted_iota(jnp.int32, (1, REL_SIZE), 1)

        def step(r, acc):
            tot = jnp.sum(jnp.where(rel == r, a, 0.0), axis=0, keepdims=True)
            tot = jnp.sum(tot, axis=1, keepdims=True)
            return acc + jnp.where(lane == r, tot, 0.0)

        o_ref[...] = lax.fori_loop(0, REL_SIZE, step, jnp.zeros((1, REL_SIZE), F32))

    return _pallas(body, name="rel_bias_fold", grid=(HEADS,),
                   in_specs=[pl.BlockSpec((None, CHUNK, BAND), lambda h: (h, 0, 0))],
                   out_specs=pl.BlockSpec((None, 1, REL_SIZE), lambda h: (h, 0, 0)),
                   out_shape=_sds((HEADS, 1, REL_SIZE), F32), compiler_params=_params("parallel"))(dbias)


def _place():
    return lax.axis_index("x"), lax.axis_index("y"), lax.axis_index("c")


def _other_chips(x, y):
    return [(1 - x, y), (x, 1 - y), (1 - x, 1 - y)]


def _all_gather_weights(blocks):
    n = len(blocks)

    def body(*refs):
        out = refs[n:2 * n]
        send_sems, recv_sems = refs[2 * n:]
        x, y, c = _place()
        chips = _other_chips(x, y)

        def half(w, chip, cc):
            hr = blocks[w].shape[1] // 2
            return out[w].at[2 * chip[0] + chip[1], pl.ds(cc * hr, hr), :]

        def copy(w, j, chip_from, cc, to):
            return pltpu.make_async_remote_copy(
                src_ref=half(w, chip_from, cc), dst_ref=half(w, chip_from, cc),
                send_sem=send_sems.at[w, j], recv_sem=recv_sems.at[w, j], device_id=to, device_id_type=MESH)

        first = [[copy(w, j, (x, y), c, (*chip, c)) for j, chip in enumerate(chips)] for w in range(n)]
        for w in range(n):
            for cp in first[w]:
                cp.start()
        passed = [[copy(w, 3 + j, chip, c, (x, y, 1 - c)) for j, chip in enumerate(chips)] for w in range(n)]
        for w in range(n):
            for j, chip in enumerate(chips):
                copy(w, j, chip, c, (x, y, c)).wait_recv()
                passed[w][j].start()
        for w in range(n):
            for j, chip in enumerate(chips):
                copy(w, 3 + j, chip, 1 - c, (x, y, c)).wait_recv()
        for w in range(n):
            for cp in first[w] + passed[w]:
                cp.wait_send()

    return _pallas(
        body, name="all_gather_weights", in_specs=[ANY] * n, out_specs=[ANY] * n,
        out_shape=[_sds(b.shape, b.dtype) for b in blocks], input_output_aliases={w: w for w in range(n)},
        scratch_shapes=[pltpu.SemaphoreType.DMA((n, 6)), pltpu.SemaphoreType.DMA((n, 6))],
    )(*blocks)


def _swap_halves(grads):
    n = len(grads)

    def body(*refs):
        src = refs[:n]
        out = refs[n:2 * n]
        send_sems, recv_sems = refs[2 * n:]
        x, y, c = _place()
        copies = []
        for w in range(n):
            hr = grads[w].shape[1] // 2
            copies.append(pltpu.make_async_remote_copy(
                src_ref=src[w].at[:, pl.ds((1 - c) * hr, hr), :], dst_ref=out[w],
                send_sem=send_sems.at[w], recv_sem=recv_sems.at[w], device_id=(x, y, 1 - c), device_id_type=MESH))
        for cp in copies:
            cp.start()
        for cp in copies:
            cp.wait()

    return _pallas(
        body, name="grad_swap_halves", in_specs=[ANY] * n, out_specs=[ANY] * n,
        out_shape=[_sds((N_CHIPS, g.shape[1] // 2, g.shape[2]), g.dtype) for g in grads],
        scratch_shapes=[pltpu.SemaphoreType.DMA((n,)), pltpu.SemaphoreType.DMA((n,))],
    )(*grads)


def _add_half(g, got, c_arr, name):
    nk, r, cols = g.shape
    hr = r // 2
    tr = min(hr, 256)
    nb = hr // tr

    def body(c_ref, g_ref, got_ref, o_ref):
        o_ref[...] = (g_ref[...].astype(F32) + got_ref[...].astype(F32)).astype(o_ref.dtype)

    grid_spec = pltpu.PrefetchScalarGridSpec(
        num_scalar_prefetch=1, grid=(nk, nb),
        in_specs=[pl.BlockSpec((None, tr, cols), lambda k, i, c_ref: (k, c_ref[0] * nb + i, 0)),
                  pl.BlockSpec((None, tr, cols), lambda k, i, c_ref: (k, i, 0))],
        out_specs=pl.BlockSpec((None, tr, cols), lambda k, i, c_ref: (k, i, 0)))
    return _pallas(body, name=name, grid_spec=grid_spec, out_shape=_sds((nk, hr, cols), g.dtype),
                   compiler_params=_params("parallel", "parallel"))(c_arr, g, got)


def _send_partials(parts):
    n = len(parts)

    def body(*refs):
        src = refs[:n]
        out = refs[n:2 * n]
        send_sems, recv_sems = refs[2 * n:]
        x, y, c = _place()
        copies = []
        for w in range(n):
            for j, chip in enumerate(_other_chips(x, y)):
                copies.append(pltpu.make_async_remote_copy(
                    src_ref=src[w].at[2 * chip[0] + chip[1]], dst_ref=out[w].at[j],
                    send_sem=send_sems.at[w, j], recv_sem=recv_sems.at[w, j], device_id=(*chip, c), device_id_type=MESH))
        for cp in copies:
            cp.start()
        for cp in copies:
            cp.wait()

    return _pallas(
        body, name="grad_send_partials", in_specs=[ANY] * n, out_specs=[ANY] * n,
        out_shape=[_sds((N_CHIPS - 1, *p.shape[1:]), p.dtype) for p in parts],
        scratch_shapes=[pltpu.SemaphoreType.DMA((n, 3)), pltpu.SemaphoreType.DMA((n, 3))],
    )(*parts)


def _sum_partials(part, got, kc_arr, name):
    _, hr, cols = part.shape
    tr = min(hr, 256)
    nb = hr // tr

    def body(kc_ref, p_ref, g0_ref, g1_ref, g2_ref, o_ref):
        o_ref[...] = ((p_ref[...].astype(F32) + g0_ref[...].astype(F32)) + g1_ref[...].astype(F32)) + g2_ref[...].astype(F32)

    slot = lambda j: pl.BlockSpec((None, tr, cols), lambda i, kc_ref: (j, i, 0))
    grid_spec = pltpu.PrefetchScalarGridSpec(
        num_scalar_prefetch=1, grid=(nb,),
        in_specs=[pl.BlockSpec((None, tr, cols), lambda i, kc_ref: (kc_ref[0], i, 0)), slot(0), slot(1), slot(2)],
        out_specs=pl.BlockSpec((tr, cols), lambda i, kc_ref: (kc_ref[1] * nb + i, 0)))
    return _pallas(body, name=name, grid_spec=grid_spec, out_shape=_sds((2 * hr, cols), F32),
                   compiler_params=_params("parallel"))(kc_arr, part, got, got, got)


def _share_halves(grads):
    n = len(grads)

    def body(*refs):
        out = refs[n:2 * n]
        send_sems, recv_sems = refs[2 * n:]
        x, y, c = _place()
        copies = []
        for w in range(n):
            hr = grads[w].shape[0] // 2
            mine = out[w].at[pl.ds(c * hr, hr), :]
            copies.append(pltpu.make_async_remote_copy(
                src_ref=mine, dst_ref=mine, send_sem=send_sems.at[w], recv_sem=recv_sems.at[w],
                device_id=(x, y, 1 - c), device_id_type=MESH))
        for cp in copies:
            cp.start()
        for cp in copies:
            cp.wait()

    return _pallas(
        body, name="grad_share_halves", in_specs=[ANY] * n, out_specs=[ANY] * n,
        out_shape=[_sds(g.shape, g.dtype) for g in grads], input_output_aliases={w: w for w in range(n)},
        scratch_shapes=[pltpu.SemaphoreType.DMA((n,)), pltpu.SemaphoreType.DMA((n,))],
    )(*grads)


def _small_allreduce_adamw(g_part, w, m, v):
    rows = g_part.shape[0]

    def body(g_ref, w_ref, m_ref, v_ref, go_ref, d_ref, nm_ref, nv_ref, all_ref, send_sems, recv_sems):
        x, y, c = _place()
        me = 4 * x + 2 * y + c
        all_ref[me] = g_ref[...]
        copies = []
        for r in range(1, 8):
            dx, dy, dc = (r >> 2) & 1, (r >> 1) & 1, r & 1
            peer = (1 - x if dx else x, 1 - y if dy else y, 1 - c if dc else c)
            copies.append(pltpu.make_async_remote_copy(
                src_ref=g_ref, dst_ref=all_ref.at[me], send_sem=send_sems.at[r - 1], recv_sem=recv_sems.at[r - 1],
                device_id=peer, device_id_type=MESH))
        for cp in copies:
            cp.start()
        for cp in copies:
            cp.wait()
        tot = all_ref[0]
        for d in range(1, 8):
            tot = tot + all_ref[d]
        go_ref[...] = tot
        d_ref[...], nm_ref[...], nv_ref[...] = _adamw_math(w_ref[...], tot, m_ref[...], v_ref[...])

    vm = pl.BlockSpec(memory_space=pltpu.VMEM)
    return _pallas(
        body, name="small_allreduce_adamw", in_specs=[vm] * 4, out_specs=[vm] * 4,
        out_shape=[_sds((rows, 128), F32)] * 4,
        scratch_shapes=[pltpu.VMEM((8, rows, 128), F32), pltpu.SemaphoreType.DMA((7,)), pltpu.SemaphoreType.DMA((7,))],
    )(g_part, w, m, v)


SMALL_SIZES = (2048, 1024, 128, 128, HEADS * REL_SIZE, 2048)
SMALL_PART_ROWS = tuple(-(-size // 1024) * 8 for size in SMALL_SIZES)
SMALL_ROWS = sum(SMALL_PART_ROWS)


def _pack_small(parts):
    rows = []
    for p, size, nr in zip(parts, SMALL_SIZES, SMALL_PART_ROWS):
        rows.append(jnp.pad(p.reshape(-1), (0, nr * 128 - size)).reshape(nr, 128))
    return jnp.concatenate(rows, axis=0)


def _unpack_small(slab, shapes):
    out, off = [], 0
    for size, nr, shape in zip(SMALL_SIZES, SMALL_PART_ROWS, shapes):
        out.append(slab[off:off + nr].reshape(-1)[:size].reshape(shape))
        off += nr
    return out


def kernel(x, norm1_g, w_in, ret_norm_g, q_norm_g, k_norm_g, rel_bias, w_out, norm2_g, w_ff1, w_ff2, loss_target, m_norm1_g, m_w_in, m_ret_norm_g, m_q_norm_g, m_k_norm_g, m_rel_bias, m_w_out, m_norm2_g, m_w_ff1, m_w_ff2, v_norm1_g, v_w_in, v_ret_norm_g, v_q_norm_g, v_k_norm_g, v_rel_bias, v_w_out, v_norm2_g, v_w_ff1, v_w_ff2):
    xs = x[0]
    tgt = loss_target[0]
    s, d = xs.shape
    d_in = N_CHIPS * w_in.shape[2]
    d_ff = N_CHIPS * w_ff1.shape[2]
    in_sh, ff_sh = w_in.shape[2], w_ff1.shape[2]
    tm = min(s, 1024)
    gi = s // tm
    c_arr = lax.axis_index("c").astype(jnp.int32).reshape(1)
    k_arr = (2 * lax.axis_index("x") + lax.axis_index("y")).astype(jnp.int32).reshape(1)
    tables = _tables(s)
    bias = _rel_bias_expand(rel_bias[0])

    blocks = [_cast_bf16(w_in[0], k_arr, "cast_w_in"), _cast_bf16(w_out[0], k_arr, "cast_w_out"),
              _cast_bf16(w_ff1[0], k_arr, "cast_w_ff1"), _cast_bf16(w_ff2[0], k_arr, "cast_w_ff2")]
    wg_in, wg_out, wg_ff1, wg_ff2 = _all_gather_weights(blocks)
    wg_out = wg_out.reshape(d, d)
    wg_ff2 = wg_ff2.reshape(d_ff, d)

    h1 = _rmsnorm_fwd(xs, norm1_g, "rmsnorm1")
    tn_in = in_sh // 2
    tk = 512
    (proj,) = _mm("proj", h1, wg_in, NN, (gi, 2 * N_CHIPS, d // tk),
                  pl.BlockSpec((tm, tk), lambda i, j, k: (i, k)),
                  pl.BlockSpec((None, tk, tn_in), lambda i, j, k: (j // 2, k, j % 2)),
                  [_sds((s, d_in), F32)], [pl.BlockSpec((tm, tn_in), lambda i, j, k: (i, j))], (tm, tn_in))
    ret, y_ret, prev = _retention_fwd(proj, ret_norm_g, tables)
    att = _attention_fwd(proj, q_norm_g, k_norm_g, bias)
    mix = jnp.concatenate([ret, att], axis=-1)
    tn = 1024
    tile = pl.BlockSpec((tm, tn), lambda i, j, k: (i, j))
    (x1,) = _mm("out_proj", mix, wg_out, NN, (gi, d // tn, d // tk),
                pl.BlockSpec((tm, tk), lambda i, j, k: (i, k)), pl.BlockSpec((tk, tn), lambda i, j, k: (k, j)),
                [_sds((s, d), F32)], [tile], (tm, tn), extras=(xs,), extra_specs=(tile,),
                epi=lambda acc, r: (r + acc,))
    h2 = _rmsnorm_fwd(x1, norm2_g, "rmsnorm2")
    tn_ff = min(ff_sh, 1024)
    per = ff_sh // tn_ff

    def relu2(acc):
        r = jnp.maximum(acc, 0.0)
        return acc, r * r

    u, act = _mm("ff1", h2, wg_ff1, NN, (gi, N_CHIPS * per, d // tk),
                 pl.BlockSpec((tm, tk), lambda i, j, k: (i, k)),
                 pl.BlockSpec((None, tk, tn_ff), lambda i, j, k: (j // per, k, j % per)),
                 [_sds((s, d_ff), F32), _sds((s, d_ff), BF16)],
                 [pl.BlockSpec((tm, tn_ff), lambda i, j, k: (i, j))] * 2, (tm, tn_ff), epi=relu2)

    def loss_epi(acc, res, t):
        diff = (res + acc) - t
        dy = diff / d
        return dy, dy, jnp.sum(diff * diff, axis=0, keepdims=True)

    dy, dyb, loss_cols = _mm(
        "ff2_loss", act, wg_ff2, NN, (gi, d // tn, d_ff // tk),
        pl.BlockSpec((tm, tk), lambda i, j, k: (i, k)), pl.BlockSpec((tk, tn), lambda i, j, k: (k, j)),
        [_sds((s, d), F32), _sds((s, d), BF16), _sds((gi, 1, d), F32)],
        [tile, tile, pl.BlockSpec((None, 1, tn), lambda i, j, k: (i, 0, j))], (tm, tn),
        extras=(x1, tgt), extra_specs=(tile, tile), epi=loss_epi)
    loss = lax.psum(0.5 * jnp.sum(loss_cols) / d, ("x", "y", "c"))

    (du,) = _mm("d_act", dyb, wg_ff2, NT, (gi, d_ff // tn, d // tk),
                pl.BlockSpec((tm, tk), lambda i, j, k: (i, k)), pl.BlockSpec((tn, tk), lambda i, j, k: (j, k)),
                [_sds((s, d_ff), BF16)], [tile], (tm, tn), extras=(u,), extra_specs=(tile,),
                epi=lambda acc, uu: (acc * (2.0 * jnp.maximum(uu, 0.0)),))
    ts = min(s, 512)
    wtile = pl.BlockSpec((tn, tn), lambda i, j, k: (i, j))
    (g_ff2,) = _mm("dw_ff2", act, dyb, TN, (d_ff // tn, d // tn, s // ts),
                   pl.BlockSpec((ts, tn), lambda i, j, k: (k, i)), pl.BlockSpec((ts, tn), lambda i, j, k: (k, j)),
                   [_sds((d_ff, d), BF16)], [wtile], (tn, tn))
    (g_ff1,) = _mm("dw_ff1", h2, du, TN, (d // tn, N_CHIPS * per, s // ts),
                   pl.BlockSpec((ts, tn), lambda i, j, k: (k, i)), pl.BlockSpec((ts, tn_ff), lambda i, j, k: (k, j)),
                   [_sds((N_CHIPS, d, ff_sh), BF16)],
                   [pl.BlockSpec((None, tn, tn_ff), lambda i, j, k: (j // per, i, j % per))], (tn, tn_ff))
    kper = ff_sh // tk
    (dh2,) = _mm("d_h2", du, wg_ff1, NT, (gi, d // tn, d_ff // tk),
                 pl.BlockSpec((tm, tk), lambda i, j, k: (i, k)),
                 pl.BlockSpec((None, tn, tk), lambda i, j, k: (k // kper, j, k % kper)),
                 [_sds((s, d), F32)], [tile], (tm, tn))
    dx1, dx1b, g_norm2 = _rmsnorm_bwd(x1, norm2_g, dh2, dy, "rmsnorm2_bwd")

    (dmix,) = _mm("d_mix", dx1b, wg_out, NT, (gi, d // tn, d // tk),
                  pl.BlockSpec((tm, tk), lambda i, j, k: (i, k)), pl.BlockSpec((tn, tk), lambda i, j, k: (j, k)),
                  [_sds((s, d), F32)], [tile], (tm, tn))
    (g_out,) = _mm("dw_out", mix, dx1b, TN, (d // tn, d // tn, s // ts),
                   pl.BlockSpec((ts, tn), lambda i, j, k: (k, i)), pl.BlockSpec((ts, tn), lambda i, j, k: (k, j)),
                   [_sds((d, d), BF16)], [wtile], (tn, tn))
    d_rq, d_rk, d_rv, d_rg, g_gn = _retention_bwd(proj, ret_norm_g, tables, y_ret, prev, dmix)
    d_aq, d_ak, d_av, g_gq, g_gk, dbias = _attention_bwd(proj, q_norm_g, k_norm_g, bias, dmix)
    g_rel = _rel_bias_fold(dbias)
    dproj = jnp.concatenate([d_rq, d_rk, d_rv, d_rg, d_aq, d_ak, d_av], axis=-1)
    (g_in,) = _mm("dw_in", h1, dproj, TN, (d // tn, 2 * N_CHIPS, s // ts),
                  pl.BlockSpec((ts, tn), lambda i, j, k: (k, i)), pl.BlockSpec((ts, tn_in), lambda i, j, k: (k, j)),
                  [_sds((N_CHIPS, d, in_sh), BF16)],
                  [pl.BlockSpec((None, tn, tn_in), lambda i, j, k: (j // 2, i, j % 2))], (tn, tn_in))
    (dh1,) = _mm("d_h1", dproj, wg_in, NT, (gi, d // tn, 2 * N_CHIPS),
                 pl.BlockSpec((tm, tn_in), lambda i, j, k: (i, k)),
                 pl.BlockSpec((None, tn, tn_in), lambda i, j, k: (k // 2, j, k % 2)),
                 [_sds((s, d), F32)], [tile], (tm, tn))
    grad_x, _, g_norm1 = _rmsnorm_bwd(xs, norm1_g, dh1, dx1, "rmsnorm1_bwd")

    full = [g_in, g_out.reshape(N_CHIPS, d // N_CHIPS, d), g_ff1, g_ff2.reshape(N_CHIPS, d_ff // N_CHIPS, d)]
    names = ["w_in", "w_out", "w_ff1", "w_ff2"]
    got = _swap_halves(full)
    parts = [_add_half(g, r, c_arr, "chip_partial_" + nm) for g, r, nm in zip(full, got, names)]
    got2 = _send_partials(parts)
    kc_arr = jnp.concatenate([k_arr, c_arr])
    halves = [_sum_partials(p, r, kc_arr, "sum_partials_" + nm) for p, r, nm in zip(parts, got2, names)]
    g_big = _share_halves(halves)
    big = []
    for g, w, m, v, nm in zip(g_big, (w_in, w_out, w_ff1, w_ff2), (m_w_in, m_w_out, m_w_ff1, m_w_ff2),
                              (v_w_in, v_w_out, v_w_ff1, v_w_ff2), names):
        delta, new_m, new_v = _adamw(w[0], g, m[0], v[0], "adamw_" + nm)
        big.append((g[None], delta[None], new_m[None], new_v[None]))

    small_w = (norm1_g, ret_norm_g, q_norm_g, k_norm_g, rel_bias, norm2_g)
    small_m = (m_norm1_g, m_ret_norm_g, m_q_norm_g, m_k_norm_g, m_rel_bias, m_norm2_g)
    small_v = (v_norm1_g, v_ret_norm_g, v_q_norm_g, v_k_norm_g, v_rel_bias, v_norm2_g)
    shapes = [p.shape for p in small_w]
    g_small = _pack_small([g_norm1, g_gn, g_gq, g_gk, g_rel, g_norm2])
    sg, sd, sm, sv = (_unpack_small(a, shapes) for a in _small_allreduce_adamw(
        g_small, _pack_small(small_w), _pack_small(small_m), _pack_small(small_v)))

    def ordered(kind):
        sm_ = (sg, sd, sm, sv)[kind]
        return (sm_[0], big[0][kind], sm_[1], sm_[2], sm_[3], sm_[4], big[1][kind], sm_[5], big[2][kind], big[3][kind])

    return (loss, grad_x[None], *ordered(0), *ordered(1), *ordered(2), *ordered(3))
```

```python
import functools

import jax
import jax.numpy as jnp
from jax import lax
from jax.experimental import pallas as pl
from jax.experimental.pallas import tpu as pltpu

F32 = jnp.float32
BF16 = jnp.bfloat16
MXU_DTYPE = jnp.bfloat16

CHUNK = 64
HEADS = 8
HEAD_DIM = 128
LEFT_CHUNKS = 8
BAND = (LEFT_CHUNKS + 1) * CHUNK
REL_CLIP = 128
REL_SIZE = (CHUNK - 1) + REL_CLIP + 1
ROPE_BASE = 10000.0
EPS = 1e-6
GN_EPS = 1e-5
ADAM_LR, ADAM_B1, ADAM_B2, ADAM_EPS, ADAM_WD, ADAM_STEP = 0.001, 0.9, 0.999, 1e-08, 0.01, 10
N_CHIPS = 4
VMEM_LIMIT = 56 * 1024 * 1024
MESH = pl.DeviceIdType.MESH
ANY = pl.BlockSpec(memory_space=pl.ANY)

NN = (((1,), (0,)), ((), ()))
NT = (((1,), (1,)), ((), ()))
TN = (((0,), (0,)), ((), ()))


def _pallas(body, **kw):
    return pl.pallas_call(body, **kw)


def _params(*sem):
    return pltpu.CompilerParams(dimension_semantics=sem, vmem_limit_bytes=VMEM_LIMIT)


def _dot(a, b, dims):
    return lax.dot_general(a.astype(MXU_DTYPE), b.astype(MXU_DTYPE), dims, preferred_element_type=F32)


RIDER_MID = 0.8


def _mm(name, a, b, dims, grid, a_spec, b_spec, outs, o_specs, acc_shape, extras=(), extra_specs=(), epi=None,
        riders=()):
    ni, nj, nk = grid
    n_ex, n_out = len(extras), len(outs)
    n_in = 2 + n_ex
    rs = _Riders(riders, n_in, n_out)
    n_rin, n_rout = len(rs.arrays), len(rs.out_shapes)
    steps = ni * nj * nk

    def body(*refs):
        a_ref, b_ref = refs[0], refs[1]
        ex_refs = refs[2:n_in]
        o_refs = refs[n_in + n_rin:n_in + n_rin + n_out]
        acc_ref = refs[n_in + n_rin + n_out + n_rout]
        k = pl.program_id(2)
        if riders:
            bound = rs.bind(refs[n_in:n_in + n_rin], refs[n_in + n_rin + n_out:n_in + n_rin + n_out + n_rout],
                            refs[n_in + n_rin + n_out + n_rout + 1:])
            step = (pl.program_id(0) * nj + pl.program_id(1)) * nk + k
            pl.when(step == 0)(lambda: rs.run("start", bound))
            pl.when(step == int(steps * RIDER_MID))(lambda: rs.run("mid", bound))

        @pl.when(k == 0)
        def _():
            acc_ref[...] = jnp.zeros_like(acc_ref)

        acc_ref[...] += _dot(a_ref[...], b_ref[...], dims)

        @pl.when(k == nk - 1)
        def _():
            acc = acc_ref[...]
            vals = epi(acc, *[r[...] for r in ex_refs]) if epi is not None else (acc,)
            for r, v in zip(o_refs, vals):
                r[...] = v.astype(r.dtype)

        if riders:
            pl.when(step == steps - 1)(lambda: rs.run("end", bound))

    res = _pallas(
        body, name=name, grid=grid, in_specs=[a_spec, b_spec, *extra_specs, *rs.in_specs],
        out_specs=[*o_specs, *rs.out_specs], out_shape=[*outs, *rs.out_shapes], input_output_aliases=rs.aliases,
        scratch_shapes=[pltpu.VMEM(acc_shape, F32), *rs.scratch],
        compiler_params=_params(*(("arbitrary",) * 3 if riders else ("parallel", "parallel", "arbitrary"))),
    )(a, b, *extras, *rs.arrays)
    return (res[:n_out], rs.split(res[n_out:])) if riders else res


def _sds(shape, dtype):
    return jax.ShapeDtypeStruct(shape, dtype)


def _cast_bf16(w, k_arr, name):
    r, c = w.shape
    tr = min(r, 256)

    def body(k_ref, w_ref, o_ref):
        o_ref[...] = w_ref[...].astype(BF16)

    grid_spec = pltpu.PrefetchScalarGridSpec(
        num_scalar_prefetch=1, grid=(r // tr,), in_specs=[pl.BlockSpec((tr, c), lambda i, k_ref: (i, 0))],
        out_specs=pl.BlockSpec((None, tr, c), lambda i, k_ref: (k_ref[0], i, 0)))
    return _pallas(body, name=name, grid_spec=grid_spec, out_shape=_sds((N_CHIPS, r, c), BF16),
                   compiler_params=_params("parallel"))(k_arr, w)


def _rmsnorm_fwd(x, g, name):
    s, d = x.shape
    tr = 256

    def body(x_ref, g_ref, o_ref):
        xv = x_ref[...]
        y = xv * lax.rsqrt(jnp.mean(xv * xv, axis=-1, keepdims=True) + EPS)
        o_ref[...] = (y * g_ref[...]).astype(o_ref.dtype)

    return _pallas(body, name=name, grid=(s // tr,),
                   in_specs=[pl.BlockSpec((tr, d), lambda i: (i, 0)), pl.BlockSpec((1, d), lambda i: (0, 0))],
                   out_specs=pl.BlockSpec((tr, d), lambda i: (i, 0)), out_shape=_sds((s, d), BF16),
                   compiler_params=_params("parallel"))(x, g)


def _rmsnorm_bwd(x, g, dh, res, name):
    s, d = x.shape
    tr = 256

    def body(x_ref, g_ref, dh_ref, res_ref, dx_ref, dxb_ref, dg_ref):
        i = pl.program_id(0)
        xv = x_ref[...]
        rstd = lax.rsqrt(jnp.mean(xv * xv, axis=-1, keepdims=True) + EPS)
        xh = xv * rstd
        dhv = dh_ref[...]

        @pl.when(i == 0)
        def _():
            dg_ref[...] = jnp.zeros_like(dg_ref)

        dg_ref[...] += jnp.sum(dhv * xh, axis=0, keepdims=True)
        dxh = dhv * g_ref[...]
        dx = res_ref[...] + rstd * (dxh - xh * jnp.mean(dxh * xh, axis=-1, keepdims=True))
        dx_ref[...] = dx
        dxb_ref[...] = dx.astype(BF16)

    row = pl.BlockSpec((tr, d), lambda i: (i, 0))
    vec = pl.BlockSpec((1, d), lambda i: (0, 0))
    return _pallas(body, name=name, grid=(s // tr,), in_specs=[row, vec, row, row], out_specs=[row, row, vec],
                   out_shape=[_sds((s, d), F32), _sds((s, d), BF16), _sds((1, d), F32)],
                   compiler_params=_params("arbitrary"))(x, g, dh, res)


def _adamw_math(w, g, m, v):
    m = ADAM_B1 * m + (1.0 - ADAM_B1) * g
    v = ADAM_B2 * v + (1.0 - ADAM_B2) * (g * g)
    m_hat = m / (1.0 - ADAM_B1 ** ADAM_STEP)
    v_hat = v / (1.0 - ADAM_B2 ** ADAM_STEP)
    delta = -ADAM_LR * (m_hat / (jnp.sqrt(v_hat) + ADAM_EPS) + ADAM_WD * w)
    return delta, m, v


def _adamw(w, g, m, v, name):
    r, c = w.shape
    tr = 128

    def body(w_ref, g_ref, m_ref, v_ref, d_ref, nm_ref, nv_ref):
        d_ref[...], nm_ref[...], nv_ref[...] = _adamw_math(w_ref[...], g_ref[...], m_ref[...], v_ref[...])

    blk = pl.BlockSpec((tr, c), lambda i: (i, 0))
    return _pallas(body, name=name, grid=(r // tr,), in_specs=[blk] * 4, out_specs=[blk] * 3,
                   out_shape=[_sds((r, c), F32)] * 3, compiler_params=_params("parallel"))(w, g, m, v)


def _tables(s):
    half = HEAD_DIM // 2
    pos = jnp.arange(s, dtype=F32)
    inv_freq = ROPE_BASE ** (-jnp.arange(half, dtype=F32) / half)
    ang = pos[:, None] * inv_freq[None, :]
    cos, sin = jnp.cos(ang), jnp.sin(ang)
    cos_f = jnp.concatenate([cos, cos], axis=-1)
    sin_f = jnp.concatenate([-sin, sin], axis=-1)
    log_g = jnp.log1p(-jnp.exp2(-(5.0 + jnp.arange(HEADS, dtype=F32))))
    p = jnp.arange(CHUNK, dtype=F32)
    decay = jnp.exp(log_g[:, None, None] * jnp.abs(p[:, None] - p[None, :]))
    k_dec = jnp.exp(log_g[None, :] * (CHUNK - 1.0 - p)[:, None])
    q_dec = jnp.exp(log_g[None, :] * (p + 1.0)[:, None])
    c_dec = jnp.exp(log_g * CHUNK)
    k_dec = jnp.broadcast_to(k_dec.T[:, :, None], (HEADS, CHUNK, HEAD_DIM))
    q_dec = jnp.broadcast_to(q_dec.T[:, :, None], (HEADS, CHUNK, HEAD_DIM))
    c_dec = jnp.broadcast_to(c_dec[:, None, None], (HEADS, 1, HEAD_DIM))
    return cos_f, sin_f, decay, k_dec, q_dec, c_dec


def _rot(x, cos_f, sin_f):
    return x * cos_f + pltpu.roll(x, HEAD_DIM // 2, 1) * sin_f


def _rot_bwd(d, cos_f, sin_f):
    return d * cos_f + pltpu.roll(d * sin_f, HEAD_DIM // 2, 1)


RET_BLOCK_CHUNKS = 8
RET_ROWS = RET_BLOCK_CHUNKS * CHUNK
K_SCALE = HEAD_DIM ** -0.5


def _retention_fwd(proj, gn_g, tables):
    s = proj.shape[0]
    nb = s // RET_ROWS
    nc = s // CHUNK
    cos_f, sin_f, decay, k_dec, q_dec, c_dec = tables

    def body(q_ref, k_ref, v_ref, g_ref, cos_ref, sin_ref, dec_ref, kd_ref, qd_ref, cd_ref, gn_ref,
             ret_ref, y_ref, prev_ref, state_ref):
        @pl.when(pl.program_id(1) == 0)
        def _():
            state_ref[...] = jnp.zeros_like(state_ref)

        cosv, sinv = cos_ref[...], sin_ref[...]
        q = _rot(q_ref[...], cosv, sinv)
        k = _rot(k_ref[...], cosv, sinv) * K_SCALE
        v = v_ref[...]
        rg = g_ref[...]
        dec, kd, qd, cd, gn = dec_ref[...], kd_ref[...], qd_ref[...], cd_ref[...], gn_ref[...]
        state = state_ref[...]
        for c in range(RET_BLOCK_CHUNKS):
            rows = slice(c * CHUNK, (c + 1) * CHUNK)
            qc, kc, vc = q[rows], k[rows], v[rows]
            sc = _dot(qc, kc, NT) * dec
            intra = _dot(sc, vc, NN)
            prev_ref[c] = state.astype(prev_ref.dtype)
            cross = _dot(qc * qd, state, NN)
            contrib = _dot(kc * kd, vc, TN)
            state = cd * state + contrib
            y = intra + cross
            y_ref[rows, :] = y
            mu = jnp.mean(y, axis=-1, keepdims=True)
            yc = y - mu
            var = jnp.mean(yc * yc, axis=-1, keepdims=True)
            yn = yc * lax.rsqrt(var + GN_EPS) * gn
            rgc = rg[rows]
            ret_ref[rows, :] = (rgc * jax.nn.sigmoid(rgc) * yn).astype(ret_ref.dtype)
        state_ref[...] = state

    def col(off):
        return pl.BlockSpec((RET_ROWS, HEAD_DIM), lambda h, i: (i, off + h))

    pos = pl.BlockSpec((RET_ROWS, HEAD_DIM), lambda h, i: (i, 0))
    per_head = lambda shape: pl.BlockSpec((None, *shape), lambda h, i: (h, 0, 0))
    return _pallas(
        body, name="retention_fwd", grid=(HEADS, nb),
        in_specs=[col(0), col(HEADS), col(2 * HEADS), col(3 * HEADS), pos, pos,
                  per_head((CHUNK, CHUNK)), per_head((CHUNK, HEAD_DIM)), per_head((CHUNK, HEAD_DIM)),
                  per_head((1, HEAD_DIM)), pl.BlockSpec((1, HEAD_DIM), lambda h, i: (0, h))],
        out_specs=[col(0), col(0),
                   pl.BlockSpec((None, RET_BLOCK_CHUNKS, HEAD_DIM, HEAD_DIM), lambda h, i: (h, i, 0, 0))],
        out_shape=[_sds((s, HEADS * HEAD_DIM), BF16), _sds((s, HEADS * HEAD_DIM), F32),
                   _sds((HEADS, nc, HEAD_DIM, HEAD_DIM), MXU_DTYPE)],
        scratch_shapes=[pltpu.VMEM((HEAD_DIM, HEAD_DIM), F32)],
        compiler_params=_params("parallel", "arbitrary"),
    )(proj, proj, proj, proj, cos_f, sin_f, decay, k_dec, q_dec, c_dec, gn_g)


def _retention_bwd(proj, gn_g, tables, y, prev, dmix):
    s = proj.shape[0]
    nb = s // RET_ROWS
    cos_f, sin_f, decay, k_dec, q_dec, c_dec = tables

    def body(q_ref, k_ref, v_ref, g_ref, cos_ref, sin_ref, dec_ref, kd_ref, qd_ref, cd_ref, gn_ref,
             y_ref, prev_ref, dret_ref, dq_ref, dk_ref, dv_ref, dg_ref, dgn_ref, gstate_ref):
        @pl.when(pl.program_id(1) == 0)
        def _():
            gstate_ref[...] = jnp.zeros_like(gstate_ref)
            dgn_ref[...] = jnp.zeros_like(dgn_ref)

        cosv, sinv = cos_ref[...], sin_ref[...]
        q = _rot(q_ref[...], cosv, sinv)
        k = _rot(k_ref[...], cosv, sinv) * K_SCALE
        v = v_ref[...]
        dec, kd, qd, cd, gn = dec_ref[...], kd_ref[...], qd_ref[...], cd_ref[...], gn_ref[...]
        rg = g_ref[...]
        yv = y_ref[...]
        dret = dret_ref[...]
        sig = jax.nn.sigmoid(rg)
        gate = rg * sig
        mu = jnp.mean(yv, axis=-1, keepdims=True)
        yc = yv - mu
        rstd = lax.rsqrt(jnp.mean(yc * yc, axis=-1, keepdims=True) + GN_EPS)
        z = yc * rstd
        dyn = dret * gate
        dg_ref[...] = (dret * (z * gn) * (sig * (1.0 + rg * (1.0 - sig)))).astype(dg_ref.dtype)
        dgn_ref[...] += jnp.sum(dyn * z, axis=0, keepdims=True)
        dz = dyn * gn
        dy = rstd * (dz - jnp.mean(dz, axis=-1, keepdims=True) - z * jnp.mean(dz * z, axis=-1, keepdims=True))
        gst = gstate_ref[...]
        for c in reversed(range(RET_BLOCK_CHUNKS)):
            rows = slice(c * CHUNK, (c + 1) * CHUNK)
            qc, kc, vc, dyc = q[rows], k[rows], v[rows], dy[rows]
            sc = _dot(qc, kc, NT) * dec
            dp = _dot(dyc, vc, NT)
            dvc = _dot(sc, dyc, TN)
            ds = dp * dec
            dqc = _dot(ds, kc, NN)
            dkc = _dot(ds, qc, TN)
            prevc = prev_ref[c]
            dqc += _dot(dyc, prevc, NT) * qd
            dprev = _dot(qc * qd, dyc, TN)
            dkc += _dot(vc, gst, NT) * kd
            dvc += _dot(kc * kd, gst, NN)
            gst = dprev + cd * gst
            dq_ref[rows, :] = _rot_bwd(dqc, cosv[rows], sinv[rows]).astype(dq_ref.dtype)
            dk_ref[rows, :] = _rot_bwd(dkc * K_SCALE, cosv[rows], sinv[rows]).astype(dk_ref.dtype)
            dv_ref[rows, :] = dvc.astype(dv_ref.dtype)
        gstate_ref[...] = gst

    rev = lambda i: nb - 1 - i

    def col(off):
        return pl.BlockSpec((RET_ROWS, HEAD_DIM), lambda h, i: (rev(i), off + h))

    pos = pl.BlockSpec((RET_ROWS, HEAD_DIM), lambda h, i: (rev(i), 0))
    per_head = lambda shape: pl.BlockSpec((None, *shape), lambda h, i: (h, 0, 0))
    outb = _sds((s, HEADS * HEAD_DIM), BF16)
    return _pallas(
        body, name="retention_bwd", grid=(HEADS, nb),
        in_specs=[col(0), col(HEADS), col(2 * HEADS), col(3 * HEADS), pos, pos,
                  per_head((CHUNK, CHUNK)), per_head((CHUNK, HEAD_DIM)), per_head((CHUNK, HEAD_DIM)),
                  per_head((1, HEAD_DIM)), pl.BlockSpec((1, HEAD_DIM), lambda h, i: (0, h)),
                  col(0), pl.BlockSpec((None, RET_BLOCK_CHUNKS, HEAD_DIM, HEAD_DIM), lambda h, i: (h, rev(i), 0, 0)),
                  col(0)],
        out_specs=[col(0), col(0), col(0), col(0), per_head((1, HEAD_DIM))],
        out_shape=[outb, outb, outb, outb, _sds((HEADS, 1, HEAD_DIM), F32)],
        scratch_shapes=[pltpu.VMEM((HEAD_DIM, HEAD_DIM), F32)],
        compiler_params=_params("parallel", "arbitrary"),
    )(proj, proj, proj, proj, cos_f, sin_f, decay, k_dec, q_dec, c_dec, gn_g, y, prev, dmix)


ATT_COL0 = 4 * HEADS
PAD_ROWS = LEFT_CHUNKS * CHUNK
NORM_ROWS = 512


def _qk_norm(x, g):
    return x * lax.rsqrt(jnp.mean(x * x, axis=-1, keepdims=True) + EPS) * g


def _band_probs(qb, kb, bias, n):
    sc = _dot(qb, kb, NT) * K_SCALE + bias
    band_chunk = lax.broadcasted_iota(jnp.int32, (CHUNK, BAND), 1) // CHUNK
    sc = jnp.where(n - LEFT_CHUNKS + band_chunk >= 0, sc, -1e30)
    e = jnp.exp(sc - jnp.max(sc, axis=-1, keepdims=True))
    return e / jnp.sum(e, axis=-1, keepdims=True)


def _with_riders(core, n_in, n_out, n_scratch, rs, steps):
    n_rin, n_rout = len(rs.arrays), len(rs.out_shapes)

    def body(*refs):
        outs_at = n_in + n_rin
        scratch_at = outs_at + n_out + n_rout
        bound = rs.bind(refs[n_in:outs_at], refs[outs_at + n_out:scratch_at], refs[scratch_at + n_scratch:])
        step = pl.program_id(0)
        pl.when(step == 0)(lambda: rs.run("start", bound))
        pl.when(step == int(steps * RIDER_MID))(lambda: rs.run("mid", bound))
        core(*refs[:n_in], *refs[outs_at:outs_at + n_out], *refs[scratch_at:scratch_at + n_scratch])
        pl.when(step == steps - 1)(lambda: rs.run("end", bound))

    return body


def _attention_fwd(proj, gq, gk, bias, riders=()):
    s = proj.shape[0]
    nc = s // CHUNK
    rs = _Riders(riders, 6, 1)

    def body(q_ref, k_ref, v_ref, gq_ref, gk_ref, bias_ref, o_ref, kp_ref, vp_ref):
        kp_ref[0:PAD_ROWS, :] = jnp.zeros((PAD_ROWS, HEAD_DIM), kp_ref.dtype)
        vp_ref[0:PAD_ROWS, :] = jnp.zeros((PAD_ROWS, HEAD_DIM), vp_ref.dtype)
        gqv, gkv = gq_ref[...], gk_ref[...]

        def fill(b, carry):
            r0 = pl.multiple_of(b * NORM_ROWS, NORM_ROWS)
            kp_ref[pl.ds(PAD_ROWS + r0, NORM_ROWS), :] = _qk_norm(k_ref[pl.ds(r0, NORM_ROWS), :], gkv).astype(kp_ref.dtype)
            vp_ref[pl.ds(PAD_ROWS + r0, NORM_ROWS), :] = v_ref[pl.ds(r0, NORM_ROWS), :].astype(vp_ref.dtype)
            return carry

        lax.fori_loop(0, s // NORM_ROWS, fill, 0)
        biasv = bias_ref[...]

        def chunk(n, carry):
            r0 = pl.multiple_of(n * CHUNK, CHUNK)
            qn = _qk_norm(q_ref[pl.ds(r0, CHUNK), :], gqv)
            p = _band_probs(qn, kp_ref[pl.ds(r0, BAND), :], biasv, n)
            o_ref[pl.ds(r0, CHUNK), :] = _dot(p, vp_ref[pl.ds(r0, BAND), :], NN).astype(o_ref.dtype)
            return carry

        lax.fori_loop(0, nc, chunk, 0)

    def col(off):
        return pl.BlockSpec((s, HEAD_DIM), lambda h: (0, off + h))

    vec = pl.BlockSpec((1, HEAD_DIM), lambda h: (0, 0))
    res = _pallas(
        _with_riders(body, 6, 1, 2, rs, HEADS), name="attention_fwd", grid=(HEADS,),
        in_specs=[col(ATT_COL0), col(ATT_COL0 + HEADS), col(ATT_COL0 + 2 * HEADS), vec, vec,
                  pl.BlockSpec((None, CHUNK, BAND), lambda h: (h, 0, 0)), *rs.in_specs],
        out_specs=[col(0), *rs.out_specs], out_shape=[_sds((s, HEADS * HEAD_DIM), BF16), *rs.out_shapes],
        input_output_aliases=rs.aliases,
        scratch_shapes=[pltpu.VMEM((s + PAD_ROWS, HEAD_DIM), MXU_DTYPE), pltpu.VMEM((s + PAD_ROWS, HEAD_DIM), MXU_DTYPE),
                        *rs.scratch],
        compiler_params=_params("arbitrary"),
    )(proj, proj, proj, gq, gk, bias, *rs.arrays)
    return res[0], rs.split(res[1:])


def _attention_bwd(proj, gq, gk, bias, dmix, riders=()):
    s = proj.shape[0]
    nc = s // CHUNK
    rs = _Riders(riders, 7, 6)

    def body(q_ref, k_ref, v_ref, gq_ref, gk_ref, bias_ref, do_ref,
             dq_ref, dk_ref, dv_ref, dgq_ref, dgk_ref, dbias_ref, kp_ref, vp_ref, dkp_ref, dvp_ref, dqn_ref):
        kp_ref[0:PAD_ROWS, :] = jnp.zeros((PAD_ROWS, HEAD_DIM), kp_ref.dtype)
        vp_ref[0:PAD_ROWS, :] = jnp.zeros((PAD_ROWS, HEAD_DIM), vp_ref.dtype)
        dkp_ref[...] = jnp.zeros_like(dkp_ref)
        dvp_ref[...] = jnp.zeros_like(dvp_ref)
        dbias_ref[...] = jnp.zeros_like(dbias_ref)
        gqv, gkv = gq_ref[...], gk_ref[...]

        def fill(b, carry):
            r0 = pl.multiple_of(b * NORM_ROWS, NORM_ROWS)
            kp_ref[pl.ds(PAD_ROWS + r0, NORM_ROWS), :] = _qk_norm(k_ref[pl.ds(r0, NORM_ROWS), :], gkv).astype(kp_ref.dtype)
            vp_ref[pl.ds(PAD_ROWS + r0, NORM_ROWS), :] = v_ref[pl.ds(r0, NORM_ROWS), :].astype(vp_ref.dtype)
            return carry

        lax.fori_loop(0, s // NORM_ROWS, fill, 0)
        biasv = bias_ref[...]

        def chunk(n, carry):
            r0 = pl.multiple_of(n * CHUNK, CHUNK)
            qn = _qk_norm(q_ref[pl.ds(r0, CHUNK), :], gqv)
            kb = kp_ref[pl.ds(r0, BAND), :]
            vb = vp_ref[pl.ds(r0, BAND), :]
            p = _band_probs(qn, kb, biasv, n)
            do = do_ref[pl.ds(r0, CHUNK), :]
            dvp_ref[pl.ds(r0, BAND), :] += _dot(p, do, TN)
            dp = _dot(do, vb, NT)
            ds = p * (dp - jnp.sum(dp * p, axis=-1, keepdims=True))
            dbias_ref[...] += ds
            dss = ds * K_SCALE
            dqn_ref[pl.ds(r0, CHUNK), :] = _dot(dss, kb, NN)
            dkp_ref[pl.ds(r0, BAND), :] += _dot(dss, qn, TN)
            return carry

        lax.fori_loop(0, nc, chunk, 0)

        @pl.when(pl.program_id(0) == 0)
        def _():
            dgq_ref[...] = jnp.zeros_like(dgq_ref)
            dgk_ref[...] = jnp.zeros_like(dgk_ref)

        def norm_bwd(x, g, dn):
            rstd = lax.rsqrt(jnp.mean(x * x, axis=-1, keepdims=True) + EPS)
            xh = x * rstd
            dxh = dn * g
            return rstd * (dxh - xh * jnp.mean(dxh * xh, axis=-1, keepdims=True)), jnp.sum(dn * xh, axis=0, keepdims=True)

        def finish(b, carry):
            r0 = pl.multiple_of(b * NORM_ROWS, NORM_ROWS)
            rows = pl.ds(r0, NORM_ROWS)
            dq, dgq = norm_bwd(q_ref[rows, :], gqv, dqn_ref[rows, :])
            dk, dgk = norm_bwd(k_ref[rows, :], gkv, dkp_ref[pl.ds(PAD_ROWS + r0, NORM_ROWS), :])
            dq_ref[rows, :] = dq.astype(dq_ref.dtype)
            dk_ref[rows, :] = dk.astype(dk_ref.dtype)
            dv_ref[rows, :] = dvp_ref[pl.ds(PAD_ROWS + r0, NORM_ROWS), :].astype(dv_ref.dtype)
            dgq_ref[...] += dgq
            dgk_ref[...] += dgk
            return carry

        lax.fori_loop(0, s // NORM_ROWS, finish, 0)

    def col(off):
        return pl.BlockSpec((s, HEAD_DIM), lambda h: (0, off + h))

    vec = pl.BlockSpec((1, HEAD_DIM), lambda h: (0, 0))
    hbias = pl.BlockSpec((None, CHUNK, BAND), lambda h: (h, 0, 0))
    outb = _sds((s, HEADS * HEAD_DIM), BF16)
    res = _pallas(
        _with_riders(body, 7, 6, 5, rs, HEADS), name="attention_bwd", grid=(HEADS,),
        in_specs=[col(ATT_COL0), col(ATT_COL0 + HEADS), col(ATT_COL0 + 2 * HEADS), vec, vec, hbias, col(HEADS),
                  *rs.in_specs],
        out_specs=[col(0), col(0), col(0), vec, vec, hbias, *rs.out_specs],
        out_shape=[outb, outb, outb, _sds((1, HEAD_DIM), F32), _sds((1, HEAD_DIM), F32),
                   _sds((HEADS, CHUNK, BAND), F32), *rs.out_shapes],
        input_output_aliases=rs.aliases,
        scratch_shapes=[pltpu.VMEM((s + PAD_ROWS, HEAD_DIM), MXU_DTYPE), pltpu.VMEM((s + PAD_ROWS, HEAD_DIM), MXU_DTYPE),
                        pltpu.VMEM((s + PAD_ROWS, HEAD_DIM), F32), pltpu.VMEM((s + PAD_ROWS, HEAD_DIM), F32),
                        pltpu.VMEM((s, HEAD_DIM), F32), *rs.scratch],
        compiler_params=_params("arbitrary"),
    )(proj, proj, proj, gq, gk, bias, dmix, *rs.arrays)
    return res[:6], rs.split(res[6:])


def _rel_distance():
    qi = lax.broadcasted_iota(jnp.int32, (CHUNK, BAND), 0)
    kj = lax.broadcasted_iota(jnp.int32, (CHUNK, BAND), 1)
    return jnp.clip(qi + LEFT_CHUNKS * CHUNK - kj, -(CHUNK - 1), REL_CLIP) + (CHUNK - 1)


def _rel_bias_expand(rel_bias):
    def body(rb_ref, o_ref):
        h = pl.program_id(0)
        rel = _rel_distance()
        o_ref[...] = lax.fori_loop(0, REL_SIZE, lambda r, acc: jnp.where(rel == r, rb_ref[h, r], acc),
                                   jnp.zeros((CHUNK, BAND), F32))

    return _pallas(body, name="rel_bias_expand", grid=(HEADS,), in_specs=[pl.BlockSpec(memory_space=pltpu.SMEM)],
                   out_specs=pl.BlockSpec((None, CHUNK, BAND), lambda h: (h, 0, 0)),
                   out_shape=_sds((HEADS, CHUNK, BAND), F32), compiler_params=_params("parallel"))(rel_bias)


def _rel_bias_fold(dbias):
    def body(a_ref, o_ref):
        a = a_ref[...]
        rel = _rel_distance()
        lane = lax.broadcasted_iota(jnp.int32, (1, REL_SIZE), 1)

        def step(r, acc):
            tot = jnp.sum(jnp.where(rel == r, a, 0.0), axis=0, keepdims=True)
            tot = jnp.sum(tot, axis=1, keepdims=True)
            return acc + jnp.where(lane == r, tot, 0.0)

        o_ref[...] = lax.fori_loop(0, REL_SIZE, step, jnp.zeros((1, REL_SIZE), F32), unroll=8)

    return _pallas(body, name="rel_bias_fold", grid=(HEADS,),
                   in_specs=[pl.BlockSpec((None, CHUNK, BAND), lambda h: (h, 0, 0))],
                   out_specs=pl.BlockSpec((None, 1, REL_SIZE), lambda h: (h, 0, 0)),
                   out_shape=_sds((HEADS, 1, REL_SIZE), F32), compiler_params=_params("parallel"))(dbias)


def _place():
    return lax.axis_index("x"), lax.axis_index("y"), lax.axis_index("c")


def _other_chips(x, y):
    return [(1 - x, y), (x, 1 - y), (1 - x, 1 - y)]


class _Rider:
    reads, ins, new, n_sems = (), (), (), 1

    def start(self, reads, ins, new, send, recv):
        pass

    def mid(self, reads, ins, new, send, recv):
        pass

    def end(self, reads, ins, new, send, recv):
        pass


class _Riders:
    def __init__(self, riders, n_host_in, n_host_out):
        self.riders = list(riders)
        self.arrays, self.out_shapes, self.aliases, self.scratch = [], [], {}, []
        for r in self.riders:
            for t, a in enumerate(r.ins):
                self.aliases[n_host_in + len(self.arrays) + len(r.reads) + t] = n_host_out + len(self.out_shapes) + t
            self.arrays += [*r.reads, *r.ins]
            self.out_shapes += [_sds(a.shape, a.dtype) for a in r.ins] + list(r.new)
            self.scratch += [pltpu.SemaphoreType.DMA((r.n_sems,)), pltpu.SemaphoreType.DMA((r.n_sems,))]
        self.in_specs = [ANY] * len(self.arrays)
        self.out_specs = [ANY] * len(self.out_shapes)

    def bind(self, in_refs, out_refs, scratch_refs):
        bound, i, o = [], 0, 0
        for t, r in enumerate(self.riders):
            reads = in_refs[i:i + len(r.reads)]
            i += len(r.reads) + len(r.ins)
            ins = out_refs[o:o + len(r.ins)]
            new = out_refs[o + len(r.ins):o + len(r.ins) + len(r.new)]
            o += len(r.ins) + len(r.new)
            bound.append((reads, ins, new, scratch_refs[2 * t], scratch_refs[2 * t + 1]))
        return bound

    def run(self, phase, bound):
        for r, b in zip(self.riders, bound):
            getattr(r, phase)(*b)

    def split(self, outs):
        res, o = [], 0
        for r in self.riders:
            n = len(r.ins) + len(r.new)
            res.append(list(outs[o:o + n]))
            o += n
        return res


def _run_riders(name, riders):
    rs = _Riders(riders, 0, 0)
    n_in, n_out = len(rs.arrays), len(rs.out_shapes)

    def body(*refs):
        bound = rs.bind(refs[:n_in], refs[n_in:n_in + n_out], refs[n_in + n_out:])
        rs.run("start", bound)
        rs.run("mid", bound)
        rs.run("end", bound)

    outs = _pallas(body, name=name, in_specs=rs.in_specs, out_specs=rs.out_specs, out_shape=rs.out_shapes,
                   input_output_aliases=rs.aliases, scratch_shapes=rs.scratch)(*rs.arrays)
    return rs.split(outs)


class _GatherRider(_Rider):
    def __init__(self, blocks):
        self.ins = tuple(blocks)
        self.n_sems = 6 * len(blocks)

    def _copy(self, out, send, recv, w, j, chip_from, cc, to):
        hr = self.ins[w].shape[1] // 2
        half = out[w].at[2 * chip_from[0] + chip_from[1], pl.ds(cc * hr, hr), :]
        return pltpu.make_async_remote_copy(src_ref=half, dst_ref=half, send_sem=send.at[6 * w + j],
                                            recv_sem=recv.at[6 * w + j], device_id=to, device_id_type=MESH)

    def start(self, reads, out, new, send, recv):
        x, y, c = _place()
        for w in range(len(self.ins)):
            for j, chip in enumerate(_other_chips(x, y)):
                self._copy(out, send, recv, w, j, (x, y), c, (*chip, c)).start()

    def mid(self, reads, out, new, send, recv):
        x, y, c = _place()
        for w in range(len(self.ins)):
            for j, chip in enumerate(_other_chips(x, y)):
                self._copy(out, send, recv, w, j, chip, c, (x, y, c)).wait_recv()
                self._copy(out, send, recv, w, 3 + j, chip, c, (x, y, 1 - c)).start()

    def end(self, reads, out, new, send, recv):
        x, y, c = _place()
        for w in range(len(self.ins)):
            for j, chip in enumerate(_other_chips(x, y)):
                self._copy(out, send, recv, w, 3 + j, chip, 1 - c, (x, y, c)).wait_recv()
        for w in range(len(self.ins)):
            for j, chip in enumerate(_other_chips(x, y)):
                self._copy(out, send, recv, w, j, (x, y), c, (*chip, c)).wait_send()
                self._copy(out, send, recv, w, 3 + j, chip, c, (x, y, 1 - c)).wait_send()


class _SwapRider(_Rider):
    def __init__(self, grads):
        self.reads = tuple(grads)
        self.new = tuple(_sds((N_CHIPS, g.shape[1] // 2, g.shape[2]), g.dtype) for g in grads)
        self.n_sems = len(grads)

    def _copies(self, src, new, send, recv):
        x, y, c = _place()
        copies = []
        for w in range(len(self.reads)):
            hr = self.reads[w].shape[1] // 2
            copies.append(pltpu.make_async_remote_copy(
                src_ref=src[w].at[:, pl.ds((1 - c) * hr, hr), :], dst_ref=new[w],
                send_sem=send.at[w], recv_sem=recv.at[w], device_id=(x, y, 1 - c), device_id_type=MESH))
        return copies

    def start(self, src, ins, new, send, recv):
        for cp in self._copies(src, new, send, recv):
            cp.start()

    def end(self, src, ins, new, send, recv):
        for cp in self._copies(src, new, send, recv):
            cp.wait()


def _add_half(g, got, c_arr, name):
    nk, r, cols = g.shape
    hr = r // 2
    tr = min(hr, 256)
    nb = hr // tr

    def body(c_ref, g_ref, got_ref, o_ref):
        o_ref[...] = (g_ref[...].astype(F32) + got_ref[...].astype(F32)).astype(o_ref.dtype)

    grid_spec = pltpu.PrefetchScalarGridSpec(
        num_scalar_prefetch=1, grid=(nk, nb),
        in_specs=[pl.BlockSpec((None, tr, cols), lambda k, i, c_ref: (k, c_ref[0] * nb + i, 0)),
                  pl.BlockSpec((None, tr, cols), lambda k, i, c_ref: (k, i, 0))],
        out_specs=pl.BlockSpec((None, tr, cols), lambda k, i, c_ref: (k, i, 0)))
    return _pallas(body, name=name, grid_spec=grid_spec, out_shape=_sds((nk, hr, cols), g.dtype),
                   compiler_params=_params("parallel", "parallel"))(c_arr, g, got)


class _SendPartialsRider(_Rider):
    def __init__(self, parts):
        self.reads = tuple(parts)
        self.new = tuple(_sds((N_CHIPS - 1, *p.shape[1:]), p.dtype) for p in parts)
        self.n_sems = 3 * len(parts)

    def _copies(self, src, new, send, recv):
        x, y, c = _place()
        copies = []
        for w in range(len(self.reads)):
            for j, chip in enumerate(_other_chips(x, y)):
                copies.append(pltpu.make_async_remote_copy(
                    src_ref=src[w].at[2 * chip[0] + chip[1]], dst_ref=new[w].at[j],
                    send_sem=send.at[3 * w + j], recv_sem=recv.at[3 * w + j], device_id=(*chip, c), device_id_type=MESH))
        return copies

    def start(self, src, ins, new, send, recv):
        for cp in self._copies(src, new, send, recv):
            cp.start()

    def end(self, src, ins, new, send, recv):
        for cp in self._copies(src, new, send, recv):
            cp.wait()


def _sum_partials(part, got, kc_arr, name):
    _, hr, cols = part.shape
    tr = min(hr, 256)
    nb = hr // tr

    def body(kc_ref, p_ref, g0_ref, g1_ref, g2_ref, o_ref):
        o_ref[...] = ((p_ref[...].astype(F32) + g0_ref[...].astype(F32)) + g1_ref[...].astype(F32)) + g2_ref[...].astype(F32)

    slot = lambda j: pl.BlockSpec((None, tr, cols), lambda i, kc_ref: (j, i, 0))
    grid_spec = pltpu.PrefetchScalarGridSpec(
        num_scalar_prefetch=1, grid=(nb,),
        in_specs=[pl.BlockSpec((None, tr, cols), lambda i, kc_ref: (kc_ref[0], i, 0)), slot(0), slot(1), slot(2)],
        out_specs=pl.BlockSpec((tr, cols), lambda i, kc_ref: (kc_ref[1] * nb + i, 0)))
    return _pallas(body, name=name, grid_spec=grid_spec, out_shape=_sds((2 * hr, cols), F32),
                   compiler_params=_params("parallel"))(kc_arr, part, got, got, got)


class _ShareRider(_Rider):
    def __init__(self, grads):
        self.ins = tuple(grads)
        self.n_sems = len(grads)

    def _copies(self, out, send, recv):
        x, y, c = _place()
        copies = []
        for w in range(len(self.ins)):
            hr = self.ins[w].shape[0] // 2
            mine = out[w].at[pl.ds(c * hr, hr), :]
            copies.append(pltpu.make_async_remote_copy(
                src_ref=mine, dst_ref=mine, send_sem=send.at[w], recv_sem=recv.at[w],
                device_id=(x, y, 1 - c), device_id_type=MESH))
        return copies

    def start(self, reads, out, new, send, recv):
        for cp in self._copies(out, send, recv):
            cp.start()

    def end(self, reads, out, new, send, recv):
        for cp in self._copies(out, send, recv):
            cp.wait()


def _small_allreduce_adamw(g_part, w, m, v):
    rows = g_part.shape[0]

    def body(g_ref, w_ref, m_ref, v_ref, go_ref, d_ref, nm_ref, nv_ref, all_ref, send_sems, recv_sems):
        x, y, c = _place()
        me = 4 * x + 2 * y + c
        all_ref[me] = g_ref[...]
        copies = []
        for r in range(1, 8):
            dx, dy, dc = (r >> 2) & 1, (r >> 1) & 1, r & 1
            peer = (1 - x if dx else x, 1 - y if dy else y, 1 - c if dc else c)
            copies.append(pltpu.make_async_remote_copy(
                src_ref=g_ref, dst_ref=all_ref.at[me], send_sem=send_sems.at[r - 1], recv_sem=recv_sems.at[r - 1],
                device_id=peer, device_id_type=MESH))
        for cp in copies:
            cp.start()
        for cp in copies:
            cp.wait()
        tot = all_ref[0]
        for d in range(1, 8):
            tot = tot + all_ref[d]
        go_ref[...] = tot
        d_ref[...], nm_ref[...], nv_ref[...] = _adamw_math(w_ref[...], tot, m_ref[...], v_ref[...])

    vm = pl.BlockSpec(memory_space=pltpu.VMEM)
    return _pallas(
        body, name="small_allreduce_adamw", in_specs=[vm] * 4, out_specs=[vm] * 4,
        out_shape=[_sds((rows, 128), F32)] * 4,
        scratch_shapes=[pltpu.VMEM((8, rows, 128), F32), pltpu.SemaphoreType.DMA((7,)), pltpu.SemaphoreType.DMA((7,))],
    )(g_part, w, m, v)


SMALL_SIZES = (2048, 1024, 128, 128, HEADS * REL_SIZE, 2048)
SMALL_PART_ROWS = tuple(-(-size // 1024) * 8 for size in SMALL_SIZES)
SMALL_ROWS = sum(SMALL_PART_ROWS)


def _pack_small(parts):
    rows = []
    for p, size, nr in zip(parts, SMALL_SIZES, SMALL_PART_ROWS):
        rows.append(jnp.pad(p.reshape(-1), (0, nr * 128 - size)).reshape(nr, 128))
    return jnp.concatenate(rows, axis=0)


def _unpack_small(slab, shapes):
    out, off = [], 0
    for size, nr, shape in zip(SMALL_SIZES, SMALL_PART_ROWS, shapes):
        out.append(slab[off:off + nr].reshape(-1)[:size].reshape(shape))
        off += nr
    return out


def kernel(x, norm1_g, w_in, ret_norm_g, q_norm_g, k_norm_g, rel_bias, w_out, norm2_g, w_ff1, w_ff2, loss_target, m_norm1_g, m_w_in, m_ret_norm_g, m_q_norm_g, m_k_norm_g, m_rel_bias, m_w_out, m_norm2_g, m_w_ff1, m_w_ff2, v_norm1_g, v_w_in, v_ret_norm_g, v_q_norm_g, v_k_norm_g, v_rel_bias, v_w_out, v_norm2_g, v_w_ff1, v_w_ff2):
    xs = x[0]
    tgt = loss_target[0]
    s, d = xs.shape
    d_in = N_CHIPS * w_in.shape[2]
    d_ff = N_CHIPS * w_ff1.shape[2]
    in_sh, ff_sh = w_in.shape[2], w_ff1.shape[2]
    tm = min(s, 1024)
    gi = s // tm
    c_arr = lax.axis_index("c").astype(jnp.int32).reshape(1)
    k_arr = (2 * lax.axis_index("x") + lax.axis_index("y")).astype(jnp.int32).reshape(1)
    tables = _tables(s)
    bias = _rel_bias_expand(rel_bias[0])

    blk_in, blk_out, blk_ff1, blk_ff2 = (
        _cast_bf16(w_in[0], k_arr, "cast_w_in"), _cast_bf16(w_out[0], k_arr, "cast_w_out"),
        _cast_bf16(w_ff1[0], k_arr, "cast_w_ff1"), _cast_bf16(w_ff2[0], k_arr, "cast_w_ff2"))
    ((wg_in,),) = _run_riders("all_gather_w_in", [_GatherRider([blk_in])])

    h1 = _rmsnorm_fwd(xs, norm1_g, "rmsnorm1")
    tn_in = in_sh // 2
    tk = 512
    (proj,), ((wg_ff1,),) = _mm(
        "proj", h1, wg_in, NN, (gi, 2 * N_CHIPS, d // tk),
        pl.BlockSpec((tm, tk), lambda i, j, k: (i, k)),
        pl.BlockSpec((None, tk, tn_in), lambda i, j, k: (j // 2, k, j % 2)),
        [_sds((s, d_in), F32)], [pl.BlockSpec((tm, tn_in), lambda i, j, k: (i, j))], (tm, tn_in),
        riders=[_GatherRider([blk_ff1])])
    ret, y_ret, prev = _retention_fwd(proj, ret_norm_g, tables)
    att, ((wg_out, wg_ff2),) = _attention_fwd(proj, q_norm_g, k_norm_g, bias, riders=[_GatherRider([blk_out, blk_ff2])])
    wg_out = wg_out.reshape(d, d)
    wg_ff2 = wg_ff2.reshape(d_ff, d)
    mix = jnp.concatenate([ret, att], axis=-1)
    tn = 1024
    tile = pl.BlockSpec((tm, tn), lambda i, j, k: (i, j))
    (x1,) = _mm("out_proj", mix, wg_out, NN, (gi, d // tn, d // tk),
                pl.BlockSpec((tm, tk), lambda i, j, k: (i, k)), pl.BlockSpec((tk, tn), lambda i, j, k: (k, j)),
                [_sds((s, d), F32)], [tile], (tm, tn), extras=(xs,), extra_specs=(tile,),
                epi=lambda acc, r: (r + acc,))
    h2 = _rmsnorm_fwd(x1, norm2_g, "rmsnorm2")
    tn_ff = min(ff_sh, 1024)
    per = ff_sh // tn_ff

    def relu2(acc):
        r = jnp.maximum(acc, 0.0)
        return acc, r * r

    u, act = _mm("ff1", h2, wg_ff1, NN, (gi, N_CHIPS * per, d // tk),
                 pl.BlockSpec((tm, tk), lambda i, j, k: (i, k)),
                 pl.BlockSpec((None, tk, tn_ff), lambda i, j, k: (j // per, k, j % per)),
                 [_sds((s, d_ff), F32), _sds((s, d_ff), BF16)],
                 [pl.BlockSpec((tm, tn_ff), lambda i, j, k: (i, j))] * 2, (tm, tn_ff), epi=relu2)

    def loss_epi(acc, res, t):
        diff = (res + acc) - t
        dy = diff / d
        return dy, dy, jnp.sum(diff * diff, axis=0, keepdims=True)

    dy, dyb, loss_cols = _mm(
        "ff2_loss", act, wg_ff2, NN, (gi, d // tn, d_ff // tk),
        pl.BlockSpec((tm, tk), lambda i, j, k: (i, k)), pl.BlockSpec((tk, tn), lambda i, j, k: (k, j)),
        [_sds((s, d), F32), _sds((s, d), BF16), _sds((gi, 1, d), F32)],
        [tile, tile, pl.BlockSpec((None, 1, tn), lambda i, j, k: (i, 0, j))], (tm, tn),
        extras=(x1, tgt), extra_specs=(tile, tile), epi=loss_epi)
    loss = lax.psum(0.5 * jnp.sum(loss_cols) / d, ("x", "y", "c"))

    (du,) = _mm("d_act", dyb, wg_ff2, NT, (gi, d_ff // tn, d // tk),
                pl.BlockSpec((tm, tk), lambda i, j, k: (i, k)), pl.BlockSpec((tn, tk), lambda i, j, k: (j, k)),
                [_sds((s, d_ff), BF16)], [tile], (tm, tn), extras=(u,), extra_specs=(tile,),
                epi=lambda acc, uu: (acc * (2.0 * jnp.maximum(uu, 0.0)),))
    ts = min(s, 512)
    wtile = pl.BlockSpec((tn, tn), lambda i, j, k: (i, j))
    (g_ff2,) = _mm("dw_ff2", act, dyb, TN, (d_ff // tn, d // tn, s // ts),
                   pl.BlockSpec((ts, tn), lambda i, j, k: (k, i)), pl.BlockSpec((ts, tn), lambda i, j, k: (k, j)),
                   [_sds((d_ff, d), BF16)], [wtile], (tn, tn))
    g_ff2 = g_ff2.reshape(N_CHIPS, d_ff // N_CHIPS, d)
    (g_ff1,), ((got_ff2,),) = _mm(
        "dw_ff1", h2, du, TN, (d // tn, N_CHIPS * per, s // ts),
        pl.BlockSpec((ts, tn), lambda i, j, k: (k, i)), pl.BlockSpec((ts, tn_ff), lambda i, j, k: (k, j)),
        [_sds((N_CHIPS, d, ff_sh), BF16)],
        [pl.BlockSpec((None, tn, tn_ff), lambda i, j, k: (j // per, i, j % per))], (tn, tn_ff),
        riders=[_SwapRider([g_ff2])])
    p_ff2 = _add_half(g_ff2, got_ff2, c_arr, "chip_partial_w_ff2")
    kper = ff_sh // tk
    (dh2,), ((got2_ff2,), (got_ff1,)) = _mm(
        "d_h2", du, wg_ff1, NT, (gi, d // tn, d_ff // tk),
        pl.BlockSpec((tm, tk), lambda i, j, k: (i, k)),
        pl.BlockSpec((None, tn, tk), lambda i, j, k: (k // kper, j, k % kper)),
        [_sds((s, d), F32)], [tile], (tm, tn), riders=[_SendPartialsRider([p_ff2]), _SwapRider([g_ff1])])
    p_ff1 = _add_half(g_ff1, got_ff1, c_arr, "chip_partial_w_ff1")
    dx1, dx1b, g_norm2 = _rmsnorm_bwd(x1, norm2_g, dh2, dy, "rmsnorm2_bwd")

    (dmix,) = _mm("d_mix", dx1b, wg_out, NT, (gi, d // tn, d // tk),
                  pl.BlockSpec((tm, tk), lambda i, j, k: (i, k)), pl.BlockSpec((tn, tk), lambda i, j, k: (j, k)),
                  [_sds((s, d), F32)], [tile], (tm, tn))
    (g_out,) = _mm("dw_out", mix, dx1b, TN, (d // tn, d // tn, s // ts),
                   pl.BlockSpec((ts, tn), lambda i, j, k: (k, i)), pl.BlockSpec((ts, tn), lambda i, j, k: (k, j)),
                   [_sds((d, d), BF16)], [wtile], (tn, tn))
    g_out = g_out.reshape(N_CHIPS, d // N_CHIPS, d)
    ((got_out,),) = _run_riders("grad_swap_w_out", [_SwapRider([g_out])])
    p_out = _add_half(g_out, got_out, c_arr, "chip_partial_w_out")
    d_rq, d_rk, d_rv, d_rg, g_gn = _retention_bwd(proj, ret_norm_g, tables, y_ret, prev, dmix)
    (d_aq, d_ak, d_av, g_gq, g_gk, dbias), ((got2_ff1, got2_out),) = _attention_bwd(
        proj, q_norm_g, k_norm_g, bias, dmix, riders=[_SendPartialsRider([p_ff1, p_out])])
    g_rel = _rel_bias_fold(dbias)
    dproj = jnp.concatenate([d_rq, d_rk, d_rv, d_rg, d_aq, d_ak, d_av], axis=-1)
    (g_in,) = _mm("dw_in", h1, dproj, TN, (d // tn, 2 * N_CHIPS, s // ts),
                  pl.BlockSpec((ts, tn), lambda i, j, k: (k, i)), pl.BlockSpec((ts, tn_in), lambda i, j, k: (k, j)),
                  [_sds((N_CHIPS, d, in_sh), BF16)],
                  [pl.BlockSpec((None, tn, tn_in), lambda i, j, k: (j // 2, i, j % 2))], (tn, tn_in))
    ((got_in,),) = _run_riders("grad_swap_w_in", [_SwapRider([g_in])])
    p_in = _add_half(g_in, got_in, c_arr, "chip_partial_w_in")
    (dh1,), ((got2_in,),) = _mm(
        "d_h1", dproj, wg_in, NT, (gi, d // tn, 2 * N_CHIPS),
        pl.BlockSpec((tm, tn_in), lambda i, j, k: (i, k)),
        pl.BlockSpec((None, tn, tn_in), lambda i, j, k: (k // 2, j, k % 2)),
        [_sds((s, d), F32)], [tile], (tm, tn), riders=[_SendPartialsRider([p_in])])
    grad_x, _, g_norm1 = _rmsnorm_bwd(xs, norm1_g, dh1, dx1, "rmsnorm1_bwd")

    names = ["w_in", "w_out", "w_ff1", "w_ff2"]
    kc_arr = jnp.concatenate([k_arr, c_arr])
    halves = [_sum_partials(p, r, kc_arr, "sum_partials_" + nm)
              for p, r, nm in zip((p_in, p_out, p_ff1, p_ff2), (got2_in, got2_out, got2_ff1, got2_ff2), names)]
    (g_big,) = _run_riders("grad_share_halves", [_ShareRider(halves)])
    big = []
    for g, w, m, v, nm in zip(g_big, (w_in, w_out, w_ff1, w_ff2), (m_w_in, m_w_out, m_w_ff1, m_w_ff2),
                              (v_w_in, v_w_out, v_w_ff1, v_w_ff2), names):
        delta, new_m, new_v = _adamw(w[0], g, m[0], v[0], "adamw_" + nm)
        big.append((g[None], delta[None], new_m[None], new_v[None]))

    small_w = (norm1_g, ret_norm_g, q_norm_g, k_norm_g, rel_bias, norm2_g)
    small_m = (m_norm1_g, m_ret_norm_g, m_q_norm_g, m_k_norm_g, m_rel_bias, m_norm2_g)
    small_v = (v_norm1_g, v_ret_norm_g, v_q_norm_g, v_k_norm_g, v_rel_bias, v_norm2_g)
    shapes = [p.shape for p in small_w]
    g_small = _pack_small([g_norm1, g_gn, g_gq, g_gk, g_rel, g_norm2])
    sg, sd, sm, sv = (_unpack_small(a, shapes) for a in _small_allreduce_adamw(
        g_small, _pack_small(small_w), _pack_small(small_m), _pack_small(small_v)))

    def ordered(kind):
        sm_ = (sg, sd, sm, sv)[kind]
        return (sm_[0], big[0][kind], sm_[1], sm_[2], sm_[3], sm_[4], big[1][kind], sm_[5], big[2][kind], big[3][kind])

    return (loss, grad_x[None], *ordered(0), *ordered(1), *ordered(2), *ordered(3))
```

```python
import functools

import jax
import jax.numpy as jnp
from jax import lax
from jax.experimental import pallas as pl
from jax.experimental.pallas import tpu as pltpu

F32 = jnp.float32
BF16 = jnp.bfloat16
MXU_DTYPE = jnp.bfloat16

CHUNK = 64
HEADS = 8
HEAD_DIM = 128
LEFT_CHUNKS = 8
BAND = (LEFT_CHUNKS + 1) * CHUNK
REL_CLIP = 128
REL_SIZE = (CHUNK - 1) + REL_CLIP + 1
ROPE_BASE = 10000.0
EPS = 1e-6
GN_EPS = 1e-5
ADAM_LR, ADAM_B1, ADAM_B2, ADAM_EPS, ADAM_WD, ADAM_STEP = 0.001, 0.9, 0.999, 1e-08, 0.01, 10
N_CHIPS = 4
VMEM_LIMIT = 56 * 1024 * 1024
MESH = pl.DeviceIdType.MESH
ANY = pl.BlockSpec(memory_space=pl.ANY)

NN = (((1,), (0,)), ((), ()))
NT = (((1,), (1,)), ((), ()))
TN = (((0,), (0,)), ((), ()))


def _pallas(body, **kw):
    return pl.pallas_call(body, **kw)


def _params(*sem):
    return pltpu.CompilerParams(dimension_semantics=sem, vmem_limit_bytes=VMEM_LIMIT)


def _dot(a, b, dims):
    return lax.dot_general(a.astype(MXU_DTYPE), b.astype(MXU_DTYPE), dims, preferred_element_type=F32)


RIDER_MID = 0.8


def _mm(name, a, b, dims, grid, a_spec, b_spec, outs, o_specs, acc_shape, extras=(), extra_specs=(), epi=None,
        riders=()):
    ni, nj, nk = grid
    n_ex, n_out = len(extras), len(outs)
    n_in = 2 + n_ex
    rs = _Riders(riders, n_in, n_out)
    n_rin, n_rout = len(rs.arrays), len(rs.out_shapes)
    steps = ni * nj * nk

    def body(*refs):
        a_ref, b_ref = refs[0], refs[1]
        ex_refs = refs[2:n_in]
        o_refs = refs[n_in + n_rin:n_in + n_rin + n_out]
        acc_ref = refs[n_in + n_rin + n_out + n_rout]
        k = pl.program_id(2)
        if riders:
            bound = rs.bind(refs[n_in:n_in + n_rin], refs[n_in + n_rin + n_out:n_in + n_rin + n_out + n_rout],
                            refs[n_in + n_rin + n_out + n_rout + 1:])
            step = (pl.program_id(0) * nj + pl.program_id(1)) * nk + k
            pl.when(step == 0)(lambda: rs.run("start", bound))
            pl.when(step == int(steps * RIDER_MID))(lambda: rs.run("mid", bound))

        def finish(acc):
            vals = epi(acc, *[r[...] for r in ex_refs]) if epi is not None else (acc,)
            for r, v in zip(o_refs, vals):
                r[...] = v.astype(r.dtype)

        if nk == 1:
            finish(_dot(a_ref[...], b_ref[...], dims))
        else:
            @pl.when(k == 0)
            def _():
                acc_ref[...] = jnp.zeros_like(acc_ref)

            acc_ref[...] += _dot(a_ref[...], b_ref[...], dims)
            pl.when(k == nk - 1)(lambda: finish(acc_ref[...]))

        if riders:
            pl.when(step == steps - 1)(lambda: rs.run("end", bound))

    res = _pallas(
        body, name=name, grid=grid, in_specs=[a_spec, b_spec, *extra_specs, *rs.in_specs],
        out_specs=[*o_specs, *rs.out_specs], out_shape=[*outs, *rs.out_shapes], input_output_aliases=rs.aliases,
        scratch_shapes=[pltpu.VMEM(acc_shape if nk > 1 else (8, 128), F32), *rs.scratch],
        compiler_params=_params(*(("arbitrary",) * 3 if riders else ("parallel", "parallel", "arbitrary"))),
    )(a, b, *extras, *rs.arrays)
    return (res[:n_out], rs.split(res[n_out:])) if riders else res


def _sds(shape, dtype):
    return jax.ShapeDtypeStruct(shape, dtype)


def _cast_bf16(w, k_arr, name):
    r, c = w.shape
    tr = min(r, 256)

    def body(k_ref, w_ref, o_ref):
        o_ref[...] = w_ref[...].astype(BF16)

    grid_spec = pltpu.PrefetchScalarGridSpec(
        num_scalar_prefetch=1, grid=(r // tr,), in_specs=[pl.BlockSpec((tr, c), lambda i, k_ref: (i, 0))],
        out_specs=pl.BlockSpec((None, tr, c), lambda i, k_ref: (k_ref[0], i, 0)))
    return _pallas(body, name=name, grid_spec=grid_spec, out_shape=_sds((N_CHIPS, r, c), BF16),
                   compiler_params=_params("parallel"))(k_arr, w)


def _rmsnorm_fwd(x, g, name):
    s, d = x.shape
    tr = 256

    def body(x_ref, g_ref, o_ref):
        xv = x_ref[...]
        y = xv * lax.rsqrt(jnp.mean(xv * xv, axis=-1, keepdims=True) + EPS)
        o_ref[...] = (y * g_ref[...]).astype(o_ref.dtype)

    return _pallas(body, name=name, grid=(s // tr,),
                   in_specs=[pl.BlockSpec((tr, d), lambda i: (i, 0)), pl.BlockSpec((1, d), lambda i: (0, 0))],
                   out_specs=pl.BlockSpec((tr, d), lambda i: (i, 0)), out_shape=_sds((s, d), BF16),
                   compiler_params=_params("parallel"))(x, g)


def _rmsnorm_bwd(x, g, dh, res, name):
    s, d = x.shape
    tr = 256

    def body(x_ref, g_ref, dh_ref, res_ref, dx_ref, dxb_ref, dg_ref):
        i = pl.program_id(0)
        xv = x_ref[...]
        rstd = lax.rsqrt(jnp.mean(xv * xv, axis=-1, keepdims=True) + EPS)
        xh = xv * rstd
        dhv = dh_ref[...]

        @pl.when(i == 0)
        def _():
            dg_ref[...] = jnp.zeros_like(dg_ref)

        dg_ref[...] += jnp.sum(dhv * xh, axis=0, keepdims=True)
        dxh = dhv * g_ref[...]
        dx = res_ref[...] + rstd * (dxh - xh * jnp.mean(dxh * xh, axis=-1, keepdims=True))
        dx_ref[...] = dx
        dxb_ref[...] = dx.astype(BF16)

    row = pl.BlockSpec((tr, d), lambda i: (i, 0))
    vec = pl.BlockSpec((1, d), lambda i: (0, 0))
    return _pallas(body, name=name, grid=(s // tr,), in_specs=[row, vec, row, row], out_specs=[row, row, vec],
                   out_shape=[_sds((s, d), F32), _sds((s, d), BF16), _sds((1, d), F32)],
                   compiler_params=_params("arbitrary"))(x, g, dh, res)


def _adamw_math(w, g, m, v):
    m = ADAM_B1 * m + (1.0 - ADAM_B1) * g
    v = ADAM_B2 * v + (1.0 - ADAM_B2) * (g * g)
    m_hat = m / (1.0 - ADAM_B1 ** ADAM_STEP)
    v_hat = v / (1.0 - ADAM_B2 ** ADAM_STEP)
    delta = -ADAM_LR * (m_hat / (jnp.sqrt(v_hat) + ADAM_EPS) + ADAM_WD * w)
    return delta, m, v


def _adamw(w, g, m, v, name):
    r, c = w.shape
    tr = 128

    def body(w_ref, g_ref, m_ref, v_ref, d_ref, nm_ref, nv_ref):
        d_ref[...], nm_ref[...], nv_ref[...] = _adamw_math(w_ref[...], g_ref[...], m_ref[...], v_ref[...])

    blk = pl.BlockSpec((tr, c), lambda i: (i, 0))
    return _pallas(body, name=name, grid=(r // tr,), in_specs=[blk] * 4, out_specs=[blk] * 3,
                   out_shape=[_sds((r, c), F32)] * 3, compiler_params=_params("parallel"))(w, g, m, v)


def _tables(s):
    half = HEAD_DIM // 2
    pos = jnp.arange(s, dtype=F32)
    inv_freq = ROPE_BASE ** (-jnp.arange(half, dtype=F32) / half)
    ang = pos[:, None] * inv_freq[None, :]
    cos, sin = jnp.cos(ang), jnp.sin(ang)
    cos_f = jnp.concatenate([cos, cos], axis=-1)
    sin_f = jnp.concatenate([-sin, sin], axis=-1)
    log_g = jnp.log1p(-jnp.exp2(-(5.0 + jnp.arange(HEADS, dtype=F32))))
    p = jnp.arange(CHUNK, dtype=F32)
    decay = jnp.exp(log_g[:, None, None] * jnp.abs(p[:, None] - p[None, :]))
    k_dec = jnp.exp(log_g[None, :] * (CHUNK - 1.0 - p)[:, None])
    q_dec = jnp.exp(log_g[None, :] * (p + 1.0)[:, None])
    c_dec = jnp.exp(log_g * CHUNK)
    k_dec = jnp.broadcast_to(k_dec.T[:, :, None], (HEADS, CHUNK, HEAD_DIM))
    q_dec = jnp.broadcast_to(q_dec.T[:, :, None], (HEADS, CHUNK, HEAD_DIM))
    c_dec = jnp.broadcast_to(c_dec[:, None, None], (HEADS, 1, HEAD_DIM))
    return cos_f, sin_f, decay, k_dec, q_dec, c_dec


def _rot(x, cos_f, sin_f):
    return x * cos_f + pltpu.roll(x, HEAD_DIM // 2, 1) * sin_f


def _rot_bwd(d, cos_f, sin_f):
    return d * cos_f + pltpu.roll(d * sin_f, HEAD_DIM // 2, 1)


RET_BLOCK_CHUNKS = 8
RET_ROWS = RET_BLOCK_CHUNKS * CHUNK
K_SCALE = HEAD_DIM ** -0.5


def _retention_fwd(proj, gn_g, tables):
    s = proj.shape[0]
    nb = s // RET_ROWS
    nc = s // CHUNK
    cos_f, sin_f, decay, k_dec, q_dec, c_dec = tables

    def body(q_ref, k_ref, v_ref, g_ref, cos_ref, sin_ref, dec_ref, kd_ref, qd_ref, cd_ref, gn_ref,
             ret_ref, y_ref, prev_ref, state_ref):
        @pl.when(pl.program_id(1) == 0)
        def _():
            state_ref[...] = jnp.zeros_like(state_ref)

        cosv, sinv = cos_ref[...], sin_ref[...]
        q = _rot(q_ref[...], cosv, sinv)
        k = _rot(k_ref[...], cosv, sinv) * K_SCALE
        v = v_ref[...]
        rg = g_ref[...]
        dec, kd, qd, cd, gn = dec_ref[...], kd_ref[...], qd_ref[...], cd_ref[...], gn_ref[...]
        state = state_ref[...]
        for c in range(RET_BLOCK_CHUNKS):
            rows = slice(c * CHUNK, (c + 1) * CHUNK)
            qc, kc, vc = q[rows], k[rows], v[rows]
            sc = _dot(qc, kc, NT) * dec
            intra = _dot(sc, vc, NN)
            prev_ref[c] = state.astype(prev_ref.dtype)
            cross = _dot(qc * qd, state, NN)
            contrib = _dot(kc * kd, vc, TN)
            state = cd * state + contrib
            y = intra + cross
            y_ref[rows, :] = y
            mu = jnp.mean(y, axis=-1, keepdims=True)
            yc = y - mu
            var = jnp.mean(yc * yc, axis=-1, keepdims=True)
            yn = yc * lax.rsqrt(var + GN_EPS) * gn
            rgc = rg[rows]
            ret_ref[rows, :] = (rgc * jax.nn.sigmoid(rgc) * yn).astype(ret_ref.dtype)
        state_ref[...] = state

    def col(off):
        return pl.BlockSpec((RET_ROWS, HEAD_DIM), lambda h, i: (i, off + h))

    pos = pl.BlockSpec((RET_ROWS, HEAD_DIM), lambda h, i: (i, 0))
    per_head = lambda shape: pl.BlockSpec((None, *shape), lambda h, i: (h, 0, 0))
    return _pallas(
        body, name="retention_fwd", grid=(HEADS, nb),
        in_specs=[col(0), col(HEADS), col(2 * HEADS), col(3 * HEADS), pos, pos,
                  per_head((CHUNK, CHUNK)), per_head((CHUNK, HEAD_DIM)), per_head((CHUNK, HEAD_DIM)),
                  per_head((1, HEAD_DIM)), pl.BlockSpec((1, HEAD_DIM), lambda h, i: (0, h))],
        out_specs=[col(0), col(0),
                   pl.BlockSpec((None, RET_BLOCK_CHUNKS, HEAD_DIM, HEAD_DIM), lambda h, i: (h, i, 0, 0))],
        out_shape=[_sds((s, HEADS * HEAD_DIM), BF16), _sds((s, HEADS * HEAD_DIM), F32),
                   _sds((HEADS, nc, HEAD_DIM, HEAD_DIM), MXU_DTYPE)],
        scratch_shapes=[pltpu.VMEM((HEAD_DIM, HEAD_DIM), F32)],
        compiler_params=_params("parallel", "arbitrary"),
    )(proj, proj, proj, proj, cos_f, sin_f, decay, k_dec, q_dec, c_dec, gn_g)


def _retention_bwd(proj, gn_g, tables, y, prev, dmix):
    s = proj.shape[0]
    nb = s // RET_ROWS
    cos_f, sin_f, decay, k_dec, q_dec, c_dec = tables

    def body(q_ref, k_ref, v_ref, g_ref, cos_ref, sin_ref, dec_ref, kd_ref, qd_ref, cd_ref, gn_ref,
             y_ref, prev_ref, dret_ref, dq_ref, dk_ref, dv_ref, dg_ref, dgn_ref, gstate_ref):
        @pl.when(pl.program_id(1) == 0)
        def _():
            gstate_ref[...] = jnp.zeros_like(gstate_ref)
            dgn_ref[...] = jnp.zeros_like(dgn_ref)

        cosv, sinv = cos_ref[...], sin_ref[...]
        q = _rot(q_ref[...], cosv, sinv)
        k = _rot(k_ref[...], cosv, sinv) * K_SCALE
        v = v_ref[...]
        dec, kd, qd, cd, gn = dec_ref[...], kd_ref[...], qd_ref[...], cd_ref[...], gn_ref[...]
        rg = g_ref[...]
        yv = y_ref[...]
        dret = dret_ref[...]
        sig = jax.nn.sigmoid(rg)
        gate = rg * sig
        mu = jnp.mean(yv, axis=-1, keepdims=True)
        yc = yv - mu
        rstd = lax.rsqrt(jnp.mean(yc * yc, axis=-1, keepdims=True) + GN_EPS)
        z = yc * rstd
        dyn = dret * gate
        dg_ref[...] = (dret * (z * gn) * (sig * (1.0 + rg * (1.0 - sig)))).astype(dg_ref.dtype)
        dgn_ref[...] += jnp.sum(dyn * z, axis=0, keepdims=True)
        dz = dyn * gn
        dy = rstd * (dz - jnp.mean(dz, axis=-1, keepdims=True) - z * jnp.mean(dz * z, axis=-1, keepdims=True))
        gst = gstate_ref[...]
        for c in reversed(range(RET_BLOCK_CHUNKS)):
            rows = slice(c * CHUNK, (c + 1) * CHUNK)
            qc, kc, vc, dyc = q[rows], k[rows], v[rows], dy[rows]
            sc = _dot(qc, kc, NT) * dec
            dp = _dot(dyc, vc, NT)
            dvc = _dot(sc, dyc, TN)
            ds = dp * dec
            dqc = _dot(ds, kc, NN)
            dkc = _dot(ds, qc, TN)
            prevc = prev_ref[c]
            dqc += _dot(dyc, prevc, NT) * qd
            dprev = _dot(qc * qd, dyc, TN)
            dkc += _dot(vc, gst, NT) * kd
            dvc += _dot(kc * kd, gst, NN)
            gst = dprev + cd * gst
            dq_ref[rows, :] = _rot_bwd(dqc, cosv[rows], sinv[rows]).astype(dq_ref.dtype)
            dk_ref[rows, :] = _rot_bwd(dkc * K_SCALE, cosv[rows], sinv[rows]).astype(dk_ref.dtype)
            dv_ref[rows, :] = dvc.astype(dv_ref.dtype)
        gstate_ref[...] = gst

    rev = lambda i: nb - 1 - i

    def col(off):
        return pl.BlockSpec((RET_ROWS, HEAD_DIM), lambda h, i: (rev(i), off + h))

    pos = pl.BlockSpec((RET_ROWS, HEAD_DIM), lambda h, i: (rev(i), 0))
    per_head = lambda shape: pl.BlockSpec((None, *shape), lambda h, i: (h, 0, 0))
    outb = _sds((s, HEADS * HEAD_DIM), BF16)
    return _pallas(
        body, name="retention_bwd", grid=(HEADS, nb),
        in_specs=[col(0), col(HEADS), col(2 * HEADS), col(3 * HEADS), pos, pos,
                  per_head((CHUNK, CHUNK)), per_head((CHUNK, HEAD_DIM)), per_head((CHUNK, HEAD_DIM)),
                  per_head((1, HEAD_DIM)), pl.BlockSpec((1, HEAD_DIM), lambda h, i: (0, h)),
                  col(0), pl.BlockSpec((None, RET_BLOCK_CHUNKS, HEAD_DIM, HEAD_DIM), lambda h, i: (h, rev(i), 0, 0)),
                  col(0)],
        out_specs=[col(0), col(0), col(0), col(0), per_head((1, HEAD_DIM))],
        out_shape=[outb, outb, outb, outb, _sds((HEADS, 1, HEAD_DIM), F32)],
        scratch_shapes=[pltpu.VMEM((HEAD_DIM, HEAD_DIM), F32)],
        compiler_params=_params("parallel", "arbitrary"),
    )(proj, proj, proj, proj, cos_f, sin_f, decay, k_dec, q_dec, c_dec, gn_g, y, prev, dmix)


ATT_COL0 = 4 * HEADS
PAD_ROWS = LEFT_CHUNKS * CHUNK
NORM_ROWS = 512


def _qk_norm(x, g):
    return x * lax.rsqrt(jnp.mean(x * x, axis=-1, keepdims=True) + EPS) * g


def _band_probs(qb, kb, bias, n):
    sc = _dot(qb, kb, NT) * K_SCALE + bias
    band_chunk = lax.broadcasted_iota(jnp.int32, (CHUNK, BAND), 1) // CHUNK
    sc = jnp.where(n - LEFT_CHUNKS + band_chunk >= 0, sc, -1e30)
    e = jnp.exp(sc - jnp.max(sc, axis=-1, keepdims=True))
    return e / jnp.sum(e, axis=-1, keepdims=True)


def _with_riders(core, n_in, n_out, n_scratch, rs, steps):
    n_rin, n_rout = len(rs.arrays), len(rs.out_shapes)

    def body(*refs):
        outs_at = n_in + n_rin
        scratch_at = outs_at + n_out + n_rout
        bound = rs.bind(refs[n_in:outs_at], refs[outs_at + n_out:scratch_at], refs[scratch_at + n_scratch:])
        step = pl.program_id(0)
        pl.when(step == 0)(lambda: rs.run("start", bound))
        pl.when(step == int(steps * RIDER_MID))(lambda: rs.run("mid", bound))
        core(*refs[:n_in], *refs[outs_at:outs_at + n_out], *refs[scratch_at:scratch_at + n_scratch])
        pl.when(step == steps - 1)(lambda: rs.run("end", bound))

    return body


def _attention_fwd(proj, gq, gk, bias, riders=()):
    s = proj.shape[0]
    nc = s // CHUNK
    rs = _Riders(riders, 6, 1)

    def body(q_ref, k_ref, v_ref, gq_ref, gk_ref, bias_ref, o_ref, kp_ref, vp_ref):
        kp_ref[0:PAD_ROWS, :] = jnp.zeros((PAD_ROWS, HEAD_DIM), kp_ref.dtype)
        vp_ref[0:PAD_ROWS, :] = jnp.zeros((PAD_ROWS, HEAD_DIM), vp_ref.dtype)
        gqv, gkv = gq_ref[...], gk_ref[...]

        def fill(b, carry):
            r0 = pl.multiple_of(b * NORM_ROWS, NORM_ROWS)
            kp_ref[pl.ds(PAD_ROWS + r0, NORM_ROWS), :] = _qk_norm(k_ref[pl.ds(r0, NORM_ROWS), :], gkv).astype(kp_ref.dtype)
            vp_ref[pl.ds(PAD_ROWS + r0, NORM_ROWS), :] = v_ref[pl.ds(r0, NORM_ROWS), :].astype(vp_ref.dtype)
            return carry

        lax.fori_loop(0, s // NORM_ROWS, fill, 0)
        biasv = bias_ref[...]

        def chunk(n, carry):
            r0 = pl.multiple_of(n * CHUNK, CHUNK)
            qn = _qk_norm(q_ref[pl.ds(r0, CHUNK), :], gqv)
            p = _band_probs(qn, kp_ref[pl.ds(r0, BAND), :], biasv, n)
            o_ref[pl.ds(r0, CHUNK), :] = _dot(p, vp_ref[pl.ds(r0, BAND), :], NN).astype(o_ref.dtype)
            return carry

        lax.fori_loop(0, nc, chunk, 0)

    def col(off):
        return pl.BlockSpec((s, HEAD_DIM), lambda h: (0, off + h))

    vec = pl.BlockSpec((1, HEAD_DIM), lambda h: (0, 0))
    res = _pallas(
        _with_riders(body, 6, 1, 2, rs, HEADS), name="attention_fwd", grid=(HEADS,),
        in_specs=[col(ATT_COL0), col(ATT_COL0 + HEADS), col(ATT_COL0 + 2 * HEADS), vec, vec,
                  pl.BlockSpec((None, CHUNK, BAND), lambda h: (h, 0, 0)), *rs.in_specs],
        out_specs=[col(0), *rs.out_specs], out_shape=[_sds((s, HEADS * HEAD_DIM), BF16), *rs.out_shapes],
        input_output_aliases=rs.aliases,
        scratch_shapes=[pltpu.VMEM((s + PAD_ROWS, HEAD_DIM), MXU_DTYPE), pltpu.VMEM((s + PAD_ROWS, HEAD_DIM), MXU_DTYPE),
                        *rs.scratch],
        compiler_params=_params("arbitrary"),
    )(proj, proj, proj, gq, gk, bias, *rs.arrays)
    return res[0], rs.split(res[1:])


def _attention_bwd(proj, gq, gk, bias, dmix, riders=()):
    s = proj.shape[0]
    nc = s // CHUNK
    rs = _Riders(riders, 7, 6)

    def body(q_ref, k_ref, v_ref, gq_ref, gk_ref, bias_ref, do_ref,
             dq_ref, dk_ref, dv_ref, dgq_ref, dgk_ref, dbias_ref, kp_ref, vp_ref, dkp_ref, dvp_ref, dqn_ref):
        kp_ref[0:PAD_ROWS, :] = jnp.zeros((PAD_ROWS, HEAD_DIM), kp_ref.dtype)
        vp_ref[0:PAD_ROWS, :] = jnp.zeros((PAD_ROWS, HEAD_DIM), vp_ref.dtype)
        dkp_ref[...] = jnp.zeros_like(dkp_ref)
        dvp_ref[...] = jnp.zeros_like(dvp_ref)
        dbias_ref[...] = jnp.zeros_like(dbias_ref)
        gqv, gkv = gq_ref[...], gk_ref[...]

        def fill(b, carry):
            r0 = pl.multiple_of(b * NORM_ROWS, NORM_ROWS)
            kp_ref[pl.ds(PAD_ROWS + r0, NORM_ROWS), :] = _qk_norm(k_ref[pl.ds(r0, NORM_ROWS), :], gkv).astype(kp_ref.dtype)
            vp_ref[pl.ds(PAD_ROWS + r0, NORM_ROWS), :] = v_ref[pl.ds(r0, NORM_ROWS), :].astype(vp_ref.dtype)
            return carry

        lax.fori_loop(0, s // NORM_ROWS, fill, 0)
        biasv = bias_ref[...]

        def chunk(n, carry):
            r0 = pl.multiple_of(n * CHUNK, CHUNK)
            qn = _qk_norm(q_ref[pl.ds(r0, CHUNK), :], gqv)
            kb = kp_ref[pl.ds(r0, BAND), :]
            vb = vp_ref[pl.ds(r0, BAND), :]
            p = _band_probs(qn, kb, biasv, n)
            do = do_ref[pl.ds(r0, CHUNK), :]
            dvp_ref[pl.ds(r0, BAND), :] += _dot(p, do, TN)
            dp = _dot(do, vb, NT)
            ds = p * (dp - jnp.sum(dp * p, axis=-1, keepdims=True))
            dbias_ref[...] += ds
            dss = ds * K_SCALE
            dqn_ref[pl.ds(r0, CHUNK), :] = _dot(dss, kb, NN)
            dkp_ref[pl.ds(r0, BAND), :] += _dot(dss, qn, TN)
            return carry

        lax.fori_loop(0, nc, chunk, 0)

        @pl.when(pl.program_id(0) == 0)
        def _():
            dgq_ref[...] = jnp.zeros_like(dgq_ref)
            dgk_ref[...] = jnp.zeros_like(dgk_ref)

        def norm_bwd(x, g, dn):
            rstd = lax.rsqrt(jnp.mean(x * x, axis=-1, keepdims=True) + EPS)
            xh = x * rstd
            dxh = dn * g
            return rstd * (dxh - xh * jnp.mean(dxh * xh, axis=-1, keepdims=True)), jnp.sum(dn * xh, axis=0, keepdims=True)

        def finish(b, carry):
            r0 = pl.multiple_of(b * NORM_ROWS, NORM_ROWS)
            rows = pl.ds(r0, NORM_ROWS)
            dq, dgq = norm_bwd(q_ref[rows, :], gqv, dqn_ref[rows, :])
            dk, dgk = norm_bwd(k_ref[rows, :], gkv, dkp_ref[pl.ds(PAD_ROWS + r0, NORM_ROWS), :])
            dq_ref[rows, :] = dq.astype(dq_ref.dtype)
            dk_ref[rows, :] = dk.astype(dk_ref.dtype)
            dv_ref[rows, :] = dvp_ref[pl.ds(PAD_ROWS + r0, NORM_ROWS), :].astype(dv_ref.dtype)
            dgq_ref[...] += dgq
            dgk_ref[...] += dgk
            return carry

        lax.fori_loop(0, s // NORM_ROWS, finish, 0)

    def col(off):
        return pl.BlockSpec((s, HEAD_DIM), lambda h: (0, off + h))

    vec = pl.BlockSpec((1, HEAD_DIM), lambda h: (0, 0))
    hbias = pl.BlockSpec((None, CHUNK, BAND), lambda h: (h, 0, 0))
    outb = _sds((s, HEADS * HEAD_DIM), BF16)
    res = _pallas(
        _with_riders(body, 7, 6, 5, rs, HEADS), name="attention_bwd", grid=(HEADS,),
        in_specs=[col(ATT_COL0), col(ATT_COL0 + HEADS), col(ATT_COL0 + 2 * HEADS), vec, vec, hbias, col(HEADS),
                  *rs.in_specs],
        out_specs=[col(0), col(0), col(0), vec, vec, hbias, *rs.out_specs],
        out_shape=[outb, outb, outb, _sds((1, HEAD_DIM), F32), _sds((1, HEAD_DIM), F32),
                   _sds((HEADS, CHUNK, BAND), F32), *rs.out_shapes],
        input_output_aliases=rs.aliases,
        scratch_shapes=[pltpu.VMEM((s + PAD_ROWS, HEAD_DIM), MXU_DTYPE), pltpu.VMEM((s + PAD_ROWS, HEAD_DIM), MXU_DTYPE),
                        pltpu.VMEM((s + PAD_ROWS, HEAD_DIM), F32), pltpu.VMEM((s + PAD_ROWS, HEAD_DIM), F32),
                        pltpu.VMEM((s, HEAD_DIM), F32), *rs.scratch],
        compiler_params=_params("arbitrary"),
    )(proj, proj, proj, gq, gk, bias, dmix, *rs.arrays)
    return res[:6], rs.split(res[6:])


def _rel_distance():
    qi = lax.broadcasted_iota(jnp.int32, (CHUNK, BAND), 0)
    kj = lax.broadcasted_iota(jnp.int32, (CHUNK, BAND), 1)
    return jnp.clip(qi + LEFT_CHUNKS * CHUNK - kj, -(CHUNK - 1), REL_CLIP) + (CHUNK - 1)


def _rel_bias_expand(rel_bias):
    def body(rb_ref, o_ref):
        h = pl.program_id(0)
        rel = _rel_distance()
        o_ref[...] = lax.fori_loop(0, REL_SIZE, lambda r, acc: jnp.where(rel == r, rb_ref[h, r], acc),
                                   jnp.zeros((CHUNK, BAND), F32))

    return _pallas(body, name="rel_bias_expand", grid=(HEADS,), in_specs=[pl.BlockSpec(memory_space=pltpu.SMEM)],
                   out_specs=pl.BlockSpec((None, CHUNK, BAND), lambda h: (h, 0, 0)),
                   out_shape=_sds((HEADS, CHUNK, BAND), F32), compiler_params=_params("parallel"))(rel_bias)


def _rel_bias_fold(dbias):
    def body(a_ref, o_ref):
        a = a_ref[...]
        rel = _rel_distance()
        lane = lax.broadcasted_iota(jnp.int32, (1, REL_SIZE), 1)

        def step(r, acc):
            tot = jnp.sum(jnp.where(rel == r, a, 0.0), axis=0, keepdims=True)
            tot = jnp.sum(tot, axis=1, keepdims=True)
            return acc + jnp.where(lane == r, tot, 0.0)

        o_ref[...] = lax.fori_loop(0, REL_SIZE, step, jnp.zeros((1, REL_SIZE), F32), unroll=8)

    return _pallas(body, name="rel_bias_fold", grid=(HEADS,),
                   in_specs=[pl.BlockSpec((None, CHUNK, BAND), lambda h: (h, 0, 0))],
                   out_specs=pl.BlockSpec((None, 1, REL_SIZE), lambda h: (h, 0, 0)),
                   out_shape=_sds((HEADS, 1, REL_SIZE), F32), compiler_params=_params("parallel"))(dbias)


def _place():
    return lax.axis_index("x"), lax.axis_index("y"), lax.axis_index("c")


def _other_chips(x, y):
    return [(1 - x, y), (x, 1 - y), (1 - x, 1 - y)]


class _Rider:
    reads, ins, new, n_sems = (), (), (), 1

    def start(self, reads, ins, new, send, recv):
        pass

    def mid(self, reads, ins, new, send, recv):
        pass

    def end(self, reads, ins, new, send, recv):
        pass


class _Riders:
    def __init__(self, riders, n_host_in, n_host_out):
        self.riders = list(riders)
        self.arrays, self.out_shapes, self.aliases, self.scratch = [], [], {}, []
        for r in self.riders:
            for t, a in enumerate(r.ins):
                self.aliases[n_host_in + len(self.arrays) + len(r.reads) + t] = n_host_out + len(self.out_shapes) + t
            self.arrays += [*r.reads, *r.ins]
            self.out_shapes += [_sds(a.shape, a.dtype) for a in r.ins] + list(r.new)
            self.scratch += [pltpu.SemaphoreType.DMA((r.n_sems,)), pltpu.SemaphoreType.DMA((r.n_sems,))]
        self.in_specs = [ANY] * len(self.arrays)
        self.out_specs = [ANY] * len(self.out_shapes)

    def bind(self, in_refs, out_refs, scratch_refs):
        bound, i, o = [], 0, 0
        for t, r in enumerate(self.riders):
            reads = in_refs[i:i + len(r.reads)]
            i += len(r.reads) + len(r.ins)
            ins = out_refs[o:o + len(r.ins)]
            new = out_refs[o + len(r.ins):o + len(r.ins) + len(r.new)]
            o += len(r.ins) + len(r.new)
            bound.append((reads, ins, new, scratch_refs[2 * t], scratch_refs[2 * t + 1]))
        return bound

    def run(self, phase, bound):
        for r, b in zip(self.riders, bound):
            getattr(r, phase)(*b)

    def split(self, outs):
        res, o = [], 0
        for r in self.riders:
            n = len(r.ins) + len(r.new)
            res.append(list(outs[o:o + n]))
            o += n
        return res


def _run_riders(name, riders):
    rs = _Riders(riders, 0, 0)
    n_in, n_out = len(rs.arrays), len(rs.out_shapes)

    def body(*refs):
        bound = rs.bind(refs[:n_in], refs[n_in:n_in + n_out], refs[n_in + n_out:])
        rs.run("start", bound)
        rs.run("mid", bound)
        rs.run("end", bound)

    outs = _pallas(body, name=name, in_specs=rs.in_specs, out_specs=rs.out_specs, out_shape=rs.out_shapes,
                   input_output_aliases=rs.aliases, scratch_shapes=rs.scratch)(*rs.arrays)
    return rs.split(outs)


class _GatherRider(_Rider):
    def __init__(self, blocks):
        self.ins = tuple(blocks)
        self.n_sems = 6 * len(blocks)

    def _copy(self, out, send, recv, w, j, chip_from, cc, to):
        hr = self.ins[w].shape[1] // 2
        half = out[w].at[2 * chip_from[0] + chip_from[1], pl.ds(cc * hr, hr), :]
        return pltpu.make_async_remote_copy(src_ref=half, dst_ref=half, send_sem=send.at[6 * w + j],
                                            recv_sem=recv.at[6 * w + j], device_id=to, device_id_type=MESH)

    def start(self, reads, out, new, send, recv):
        x, y, c = _place()
        for w in range(len(self.ins)):
            for j, chip in enumerate(_other_chips(x, y)):
                self._copy(out, send, recv, w, j, (x, y), c, (*chip, c)).start()

    def mid(self, reads, out, new, send, recv):
        x, y, c = _place()
        for w in range(len(self.ins)):
            for j, chip in enumerate(_other_chips(x, y)):
                self._copy(out, send, recv, w, j, chip, c, (x, y, c)).wait_recv()
                self._copy(out, send, recv, w, 3 + j, chip, c, (x, y, 1 - c)).start()

    def end(self, reads, out, new, send, recv):
        x, y, c = _place()
        for w in range(len(self.ins)):
            for j, chip in enumerate(_other_chips(x, y)):
                self._copy(out, send, recv, w, 3 + j, chip, 1 - c, (x, y, c)).wait_recv()
        for w in range(len(self.ins)):
            for j, chip in enumerate(_other_chips(x, y)):
                self._copy(out, send, recv, w, j, (x, y), c, (*chip, c)).wait_send()
                self._copy(out, send, recv, w, 3 + j, chip, c, (x, y, 1 - c)).wait_send()


class _SwapRider(_Rider):
    def __init__(self, grads):
        self.reads = tuple(grads)
        self.new = tuple(_sds((N_CHIPS, g.shape[1] // 2, g.shape[2]), g.dtype) for g in grads)
        self.n_sems = len(grads)

    def _copies(self, src, new, send, recv):
        x, y, c = _place()
        copies = []
        for w in range(len(self.reads)):
            hr = self.reads[w].shape[1] // 2
            copies.append(pltpu.make_async_remote_copy(
                src_ref=src[w].at[:, pl.ds((1 - c) * hr, hr), :], dst_ref=new[w],
                send_sem=send.at[w], recv_sem=recv.at[w], device_id=(x, y, 1 - c), device_id_type=MESH))
        return copies

    def start(self, src, ins, new, send, recv):
        for cp in self._copies(src, new, send, recv):
            cp.start()

    def end(self, src, ins, new, send, recv):
        for cp in self._copies(src, new, send, recv):
            cp.wait()


def _add_half(g, got, c_arr, name):
    nk, r, cols = g.shape
    hr = r // 2
    tr = min(hr, 256)
    nb = hr // tr

    def body(c_ref, g_ref, got_ref, o_ref):
        o_ref[...] = (g_ref[...].astype(F32) + got_ref[...].astype(F32)).astype(o_ref.dtype)

    grid_spec = pltpu.PrefetchScalarGridSpec(
        num_scalar_prefetch=1, grid=(nk, nb),
        in_specs=[pl.BlockSpec((None, tr, cols), lambda k, i, c_ref: (k, c_ref[0] * nb + i, 0)),
                  pl.BlockSpec((None, tr, cols), lambda k, i, c_ref: (k, i, 0))],
        out_specs=pl.BlockSpec((None, tr, cols), lambda k, i, c_ref: (k, i, 0)))
    return _pallas(body, name=name, grid_spec=grid_spec, out_shape=_sds((nk, hr, cols), g.dtype),
                   compiler_params=_params("parallel", "parallel"))(c_arr, g, got)


class _SendPartialsRider(_Rider):
    def __init__(self, parts):
        self.reads = tuple(parts)
        self.new = tuple(_sds((N_CHIPS - 1, *p.shape[1:]), p.dtype) for p in parts)
        self.n_sems = 3 * len(parts)

    def _copies(self, src, new, send, recv):
        x, y, c = _place()
        copies = []
        for w in range(len(self.reads)):
            for j, chip in enumerate(_other_chips(x, y)):
                copies.append(pltpu.make_async_remote_copy(
                    src_ref=src[w].at[2 * chip[0] + chip[1]], dst_ref=new[w].at[j],
                    send_sem=send.at[3 * w + j], recv_sem=recv.at[3 * w + j], device_id=(*chip, c), device_id_type=MESH))
        return copies

    def start(self, src, ins, new, send, recv):
        for cp in self._copies(src, new, send, recv):
            cp.start()

    def end(self, src, ins, new, send, recv):
        for cp in self._copies(src, new, send, recv):
            cp.wait()


def _sum_partials(part, got, kc_arr, name):
    _, hr, cols = part.shape
    tr = min(hr, 256)
    nb = hr // tr

    def body(kc_ref, p_ref, g0_ref, g1_ref, g2_ref, o_ref):
        o_ref[...] = ((p_ref[...].astype(F32) + g0_ref[...].astype(F32)) + g1_ref[...].astype(F32)) + g2_ref[...].astype(F32)

    slot = lambda j: pl.BlockSpec((None, tr, cols), lambda i, kc_ref: (j, i, 0))
    grid_spec = pltpu.PrefetchScalarGridSpec(
        num_scalar_prefetch=1, grid=(nb,),
        in_specs=[pl.BlockSpec((None, tr, cols), lambda i, kc_ref: (kc_ref[0], i, 0)), slot(0), slot(1), slot(2)],
        out_specs=pl.BlockSpec((tr, cols), lambda i, kc_ref: (kc_ref[1] * nb + i, 0)))
    return _pallas(body, name=name, grid_spec=grid_spec, out_shape=_sds((2 * hr, cols), F32),
                   compiler_params=_params("parallel"))(kc_arr, part, got, got, got)


class _ShareRider(_Rider):
    def __init__(self, grads):
        self.ins = tuple(grads)
        self.n_sems = len(grads)

    def _copies(self, out, send, recv):
        x, y, c = _place()
        copies = []
        for w in range(len(self.ins)):
            hr = self.ins[w].shape[0] // 2
            mine = out[w].at[pl.ds(c * hr, hr), :]
            copies.append(pltpu.make_async_remote_copy(
                src_ref=mine, dst_ref=mine, send_sem=send.at[w], recv_sem=recv.at[w],
                device_id=(x, y, 1 - c), device_id_type=MESH))
        return copies

    def start(self, reads, out, new, send, recv):
        for cp in self._copies(out, send, recv):
            cp.start()

    def end(self, reads, out, new, send, recv):
        for cp in self._copies(out, send, recv):
            cp.wait()


def _small_allreduce_adamw(g_part, w, m, v):
    rows = g_part.shape[0]

    def body(g_ref, w_ref, m_ref, v_ref, go_ref, d_ref, nm_ref, nv_ref, all_ref, send_sems, recv_sems):
        x, y, c = _place()
        me = 4 * x + 2 * y + c
        all_ref[me] = g_ref[...]
        copies = []
        for r in range(1, 8):
            dx, dy, dc = (r >> 2) & 1, (r >> 1) & 1, r & 1
            peer = (1 - x if dx else x, 1 - y if dy else y, 1 - c if dc else c)
            copies.append(pltpu.make_async_remote_copy(
                src_ref=g_ref, dst_ref=all_ref.at[me], send_sem=send_sems.at[r - 1], recv_sem=recv_sems.at[r - 1],
                device_id=peer, device_id_type=MESH))
        for cp in copies:
            cp.start()
        for cp in copies:
            cp.wait()
        tot = all_ref[0]
        for d in range(1, 8):
            tot = tot + all_ref[d]
        go_ref[...] = tot
        d_ref[...], nm_ref[...], nv_ref[...] = _adamw_math(w_ref[...], tot, m_ref[...], v_ref[...])

    vm = pl.BlockSpec(memory_space=pltpu.VMEM)
    return _pallas(
        body, name="small_allreduce_adamw", in_specs=[vm] * 4, out_specs=[vm] * 4,
        out_shape=[_sds((rows, 128), F32)] * 4,
        scratch_shapes=[pltpu.VMEM((8, rows, 128), F32), pltpu.SemaphoreType.DMA((7,)), pltpu.SemaphoreType.DMA((7,))],
    )(g_part, w, m, v)


SMALL_SIZES = (2048, 1024, 128, 128, HEADS * REL_SIZE, 2048)
SMALL_PART_ROWS = tuple(-(-size // 1024) * 8 for size in SMALL_SIZES)
SMALL_ROWS = sum(SMALL_PART_ROWS)


def _pack_small(parts):
    rows = []
    for p, size, nr in zip(parts, SMALL_SIZES, SMALL_PART_ROWS):
        rows.append(jnp.pad(p.reshape(-1), (0, nr * 128 - size)).reshape(nr, 128))
    return jnp.concatenate(rows, axis=0)


def _unpack_small(slab, shapes):
    out, off = [], 0
    for size, nr, shape in zip(SMALL_SIZES, SMALL_PART_ROWS, shapes):
        out.append(slab[off:off + nr].reshape(-1)[:size].reshape(shape))
        off += nr
    return out


def kernel(x, norm1_g, w_in, ret_norm_g, q_norm_g, k_norm_g, rel_bias, w_out, norm2_g, w_ff1, w_ff2, loss_target, m_norm1_g, m_w_in, m_ret_norm_g, m_q_norm_g, m_k_norm_g, m_rel_bias, m_w_out, m_norm2_g, m_w_ff1, m_w_ff2, v_norm1_g, v_w_in, v_ret_norm_g, v_q_norm_g, v_k_norm_g, v_rel_bias, v_w_out, v_norm2_g, v_w_ff1, v_w_ff2):
    xs = x[0]
    tgt = loss_target[0]
    s, d = xs.shape
    d_in = N_CHIPS * w_in.shape[2]
    d_ff = N_CHIPS * w_ff1.shape[2]
    in_sh, ff_sh = w_in.shape[2], w_ff1.shape[2]
    tm = min(s, 1024)
    gi = s // tm
    c_arr = lax.axis_index("c").astype(jnp.int32).reshape(1)
    k_arr = (2 * lax.axis_index("x") + lax.axis_index("y")).astype(jnp.int32).reshape(1)
    tables = _tables(s)
    bias = _rel_bias_expand(rel_bias[0])

    blk_in, blk_out, blk_ff1, blk_ff2 = (
        _cast_bf16(w_in[0], k_arr, "cast_w_in"), _cast_bf16(w_out[0], k_arr, "cast_w_out"),
        _cast_bf16(w_ff1[0], k_arr, "cast_w_ff1"), _cast_bf16(w_ff2[0], k_arr, "cast_w_ff2"))
    ((wg_in,),) = _run_riders("all_gather_w_in", [_GatherRider([blk_in])])

    h1 = _rmsnorm_fwd(xs, norm1_g, "rmsnorm1")
    tn_in = in_sh // 2
    tk = d
    (proj,), ((wg_ff1,),) = _mm(
        "proj", h1, wg_in, NN, (gi, 2 * N_CHIPS, d // tk),
        pl.BlockSpec((tm, tk), lambda i, j, k: (i, k)),
        pl.BlockSpec((None, tk, tn_in), lambda i, j, k: (j // 2, k, j % 2)),
        [_sds((s, d_in), F32)], [pl.BlockSpec((tm, tn_in), lambda i, j, k: (i, j))], (tm, tn_in),
        riders=[_GatherRider([blk_ff1])])
    ret, y_ret, prev = _retention_fwd(proj, ret_norm_g, tables)
    att, ((wg_out, wg_ff2),) = _attention_fwd(proj, q_norm_g, k_norm_g, bias, riders=[_GatherRider([blk_out, blk_ff2])])
    wg_out = wg_out.reshape(d, d)
    wg_ff2 = wg_ff2.reshape(d_ff, d)
    mix = jnp.concatenate([ret, att], axis=-1)
    tn = 1024
    tile = pl.BlockSpec((tm, tn), lambda i, j, k: (i, j))
    (x1,) = _mm("out_proj", mix, wg_out, NN, (gi, d // tn, d // tk),
                pl.BlockSpec((tm, tk), lambda i, j, k: (i, k)), pl.BlockSpec((tk, tn), lambda i, j, k: (k, j)),
                [_sds((s, d), F32)], [tile], (tm, tn), extras=(xs,), extra_specs=(tile,),
                epi=lambda acc, r: (r + acc,))
    h2 = _rmsnorm_fwd(x1, norm2_g, "rmsnorm2")
    tn_ff = min(ff_sh, 1024)
    per = ff_sh // tn_ff

    def relu2(acc):
        r = jnp.maximum(acc, 0.0)
        return acc, r * r

    u, act = _mm("ff1", h2, wg_ff1, NN, (gi, N_CHIPS * per, d // tk),
                 pl.BlockSpec((tm, tk), lambda i, j, k: (i, k)),
                 pl.BlockSpec((None, tk, tn_ff), lambda i, j, k: (j // per, k, j % per)),
                 [_sds((s, d_ff), F32), _sds((s, d_ff), BF16)],
                 [pl.BlockSpec((tm, tn_ff), lambda i, j, k: (i, j))] * 2, (tm, tn_ff), epi=relu2)

    def loss_epi(acc, res, t):
        diff = (res + acc) - t
        dy = diff / d
        return dy, dy, jnp.sum(diff * diff, axis=0, keepdims=True)

    tn2 = 512
    tile2 = pl.BlockSpec((tm, tn2), lambda i, j, k: (i, j))
    dy, dyb, loss_cols = _mm(
        "ff2_loss", act, wg_ff2, NN, (gi, d // tn2, d_ff // tk),
        pl.BlockSpec((tm, tk), lambda i, j, k: (i, k)), pl.BlockSpec((tk, tn2), lambda i, j, k: (k, j)),
        [_sds((s, d), F32), _sds((s, d), BF16), _sds((gi, 1, d), F32)],
        [tile2, tile2, pl.BlockSpec((None, 1, tn2), lambda i, j, k: (i, 0, j))], (tm, tn2),
        extras=(x1, tgt), extra_specs=(tile2, tile2), epi=loss_epi)
    loss = lax.psum(0.5 * jnp.sum(loss_cols) / d, ("x", "y", "c"))

    (du,) = _mm("d_act", dyb, wg_ff2, NT, (gi, d_ff // tn, d // tk),
                pl.BlockSpec((tm, tk), lambda i, j, k: (i, k)), pl.BlockSpec((tn, tk), lambda i, j, k: (j, k)),
                [_sds((s, d_ff), BF16)], [tile], (tm, tn), extras=(u,), extra_specs=(tile,),
                epi=lambda acc, uu: (acc * (2.0 * jnp.maximum(uu, 0.0)),))
    ts = min(s, 2048)
    wtile = pl.BlockSpec((tn, tn), lambda i, j, k: (i, j))
    (g_ff2,) = _mm("dw_ff2", act, dyb, TN, (d_ff // tn, d // tn, s // ts),
                   pl.BlockSpec((ts, tn), lambda i, j, k: (k, i)), pl.BlockSpec((ts, tn), lambda i, j, k: (k, j)),
                   [_sds((d_ff, d), BF16)], [wtile], (tn, tn))
    g_ff2 = g_ff2.reshape(N_CHIPS, d_ff // N_CHIPS, d)
    (g_ff1,), ((got_ff2,),) = _mm(
        "dw_ff1", h2, du, TN, (d // tn, N_CHIPS * per, s // ts),
        pl.BlockSpec((ts, tn), lambda i, j, k: (k, i)), pl.BlockSpec((ts, tn_ff), lambda i, j, k: (k, j)),
        [_sds((N_CHIPS, d, ff_sh), BF16)],
        [pl.BlockSpec((None, tn, tn_ff), lambda i, j, k: (j // per, i, j % per))], (tn, tn_ff),
        riders=[_SwapRider([g_ff2])])
    p_ff2 = _add_half(g_ff2, got_ff2, c_arr, "chip_partial_w_ff2")
    tkf = min(tk, ff_sh)
    kper = ff_sh // tkf
    (dh2,), ((got2_ff2,), (got_ff1,)) = _mm(
        "d_h2", du, wg_ff1, NT, (gi, d // tn, d_ff // tkf),
        pl.BlockSpec((tm, tkf), lambda i, j, k: (i, k)),
        pl.BlockSpec((None, tn, tkf), lambda i, j, k: (k // kper, j, k % kper)),
        [_sds((s, d), F32)], [tile], (tm, tn), riders=[_SendPartialsRider([p_ff2]), _SwapRider([g_ff1])])
    p_ff1 = _add_half(g_ff1, got_ff1, c_arr, "chip_partial_w_ff1")
    dx1, dx1b, g_norm2 = _rmsnorm_bwd(x1, norm2_g, dh2, dy, "rmsnorm2_bwd")

    (dmix,) = _mm("d_mix", dx1b, wg_out, NT, (gi, d // tn, d // tk),
                  pl.BlockSpec((tm, tk), lambda i, j, k: (i, k)), pl.BlockSpec((tn, tk), lambda i, j, k: (j, k)),
                  [_sds((s, d), F32)], [tile], (tm, tn))
    (g_out,) = _mm("dw_out", mix, dx1b, TN, (d // tn, d // tn, s // ts),
                   pl.BlockSpec((ts, tn), lambda i, j, k: (k, i)), pl.BlockSpec((ts, tn), lambda i, j, k: (k, j)),
                   [_sds((d, d), BF16)], [wtile], (tn, tn))
    g_out = g_out.reshape(N_CHIPS, d // N_CHIPS, d)
    ((got_out,),) = _run_riders("grad_swap_w_out", [_SwapRider([g_out])])
    p_out = _add_half(g_out, got_out, c_arr, "chip_partial_w_out")
    d_rq, d_rk, d_rv, d_rg, g_gn = _retention_bwd(proj, ret_norm_g, tables, y_ret, prev, dmix)
    (d_aq, d_ak, d_av, g_gq, g_gk, dbias), ((got2_ff1, got2_out),) = _attention_bwd(
        proj, q_norm_g, k_norm_g, bias, dmix, riders=[_SendPartialsRider([p_ff1, p_out])])
    g_rel = _rel_bias_fold(dbias)
    dproj = jnp.concatenate([d_rq, d_rk, d_rv, d_rg, d_aq, d_ak, d_av], axis=-1)
    (g_in,) = _mm("dw_in", h1, dproj, TN, (d // tn, 2 * N_CHIPS, s // ts),
                  pl.BlockSpec((ts, tn), lambda i, j, k: (k, i)), pl.BlockSpec((ts, tn_in), lambda i, j, k: (k, j)),
                  [_sds((N_CHIPS, d, in_sh), BF16)],
                  [pl.BlockSpec((None, tn, tn_in), lambda i, j, k: (j // 2, i, j % 2))], (tn, tn_in))
    ((got_in,),) = _run_riders("grad_swap_w_in", [_SwapRider([g_in])])
    p_in = _add_half(g_in, got_in, c_arr, "chip_partial_w_in")
    (dh1,), ((got2_in,),) = _mm(
        "d_h1", dproj, wg_in, NT, (gi, d // tn, N_CHIPS),
        pl.BlockSpec((tm, in_sh), lambda i, j, k: (i, k)),
        pl.BlockSpec((None, tn, in_sh), lambda i, j, k: (k, j, 0)),
        [_sds((s, d), F32)], [tile], (tm, tn), riders=[_SendPartialsRider([p_in])])
    grad_x, _, g_norm1 = _rmsnorm_bwd(xs, norm1_g, dh1, dx1, "rmsnorm1_bwd")

    names = ["w_in", "w_out", "w_ff1", "w_ff2"]
    kc_arr = jnp.concatenate([k_arr, c_arr])
    halves = [_sum_partials(p, r, kc_arr, "sum_partials_" + nm)
              for p, r, nm in zip((p_in, p_out, p_ff1, p_ff2), (got2_in, got2_out, got2_ff1, got2_ff2), names)]
    (g_big,) = _run_riders("grad_share_halves", [_ShareRider(halves)])
    big = []
    for g, w, m, v, nm in zip(g_big, (w_in, w_out, w_ff1, w_ff2), (m_w_in, m_w_out, m_w_ff1, m_w_ff2),
                              (v_w_in, v_w_out, v_w_ff1, v_w_ff2), names):
        delta, new_m, new_v = _adamw(w[0], g, m[0], v[0], "adamw_" + nm)
        big.append((g[None], delta[None], new_m[None], new_v[None]))

    small_w = (norm1_g, ret_norm_g, q_norm_g, k_norm_g, rel_bias, norm2_g)
    small_m = (m_norm1_g, m_ret_norm_g, m_q_norm_g, m_k_norm_g, m_rel_bias, m_norm2_g)
    small_v = (v_norm1_g, v_ret_norm_g, v_q_norm_g, v_k_norm_g, v_rel_bias, v_norm2_g)
    shapes = [p.shape for p in small_w]
    g_small = _pack_small([g_norm1, g_gn, g_gq, g_gk, g_rel, g_norm2])
    sg, sd, sm, sv = (_unpack_small(a, shapes) for a in _small_allreduce_adamw(
        g_small, _pack_small(small_w), _pack_small(small_m), _pack_small(small_v)))

    def ordered(kind):
        sm_ = (sg, sd, sm, sv)[kind]
        return (sm_[0], big[0][kind], sm_[1], sm_[2], sm_[3], sm_[4], big[1][kind], sm_[5], big[2][kind], big[3][kind])

    return (loss, grad_x[None], *ordered(0), *ordered(1), *ordered(2), *ordered(3))
```

```python
import functools

import jax
import jax.numpy as jnp
from jax import lax
from jax.experimental import pallas as pl
from jax.experimental.pallas import tpu as pltpu

F32 = jnp.float32
BF16 = jnp.bfloat16
MXU_DTYPE = jnp.bfloat16

CHUNK = 64
HEADS = 8
HEAD_DIM = 128
LEFT_CHUNKS = 8
BAND = (LEFT_CHUNKS + 1) * CHUNK
REL_CLIP = 128
REL_SIZE = (CHUNK - 1) + REL_CLIP + 1
ROPE_BASE = 10000.0
EPS = 1e-6
GN_EPS = 1e-5
ADAM_LR, ADAM_B1, ADAM_B2, ADAM_EPS, ADAM_WD, ADAM_STEP = 0.001, 0.9, 0.999, 1e-08, 0.01, 10
N_CHIPS = 4
VMEM_LIMIT = 56 * 1024 * 1024
MESH = pl.DeviceIdType.MESH
ANY = pl.BlockSpec(memory_space=pl.ANY)

NN = (((1,), (0,)), ((), ()))
NT = (((1,), (1,)), ((), ()))
TN = (((0,), (0,)), ((), ()))


def _pallas(body, **kw):
    return pl.pallas_call(body, **kw)


def _params(*sem):
    return pltpu.CompilerParams(dimension_semantics=sem, vmem_limit_bytes=VMEM_LIMIT)


def _dot(a, b, dims):
    return lax.dot_general(a.astype(MXU_DTYPE), b.astype(MXU_DTYPE), dims, preferred_element_type=F32)


RIDER_MID = 0.8


def _mm(name, a, b, dims, grid, a_spec, b_spec, outs, o_specs, acc_shape, extras=(), extra_specs=(), epi=None,
        riders=()):
    ni, nj, nk = grid
    n_ex, n_out = len(extras), len(outs)
    n_in = 2 + n_ex
    rs = _Riders(riders, n_in, n_out)
    n_rin, n_rout = len(rs.arrays), len(rs.out_shapes)
    steps = ni * nj * nk

    def body(*refs):
        a_ref, b_ref = refs[0], refs[1]
        ex_refs = refs[2:n_in]
        o_refs = refs[n_in + n_rin:n_in + n_rin + n_out]
        acc_ref = refs[n_in + n_rin + n_out + n_rout]
        k = pl.program_id(2)
        if riders:
            bound = rs.bind(refs[n_in:n_in + n_rin], refs[n_in + n_rin + n_out:n_in + n_rin + n_out + n_rout],
                            refs[n_in + n_rin + n_out + n_rout + 1:])
            step = (pl.program_id(0) * nj + pl.program_id(1)) * nk + k
            pl.when(step == 0)(lambda: rs.run("start", bound))
            pl.when(step == int(steps * RIDER_MID))(lambda: rs.run("mid", bound))

        def finish(acc):
            vals = epi(acc, *[r[...] for r in ex_refs]) if epi is not None else (acc,)
            for r, v in zip(o_refs, vals):
                r[...] = v.astype(r.dtype)

        if nk == 1:
            finish(_dot(a_ref[...], b_ref[...], dims))
        else:
            @pl.when(k == 0)
            def _():
                acc_ref[...] = jnp.zeros_like(acc_ref)

            acc_ref[...] += _dot(a_ref[...], b_ref[...], dims)
            pl.when(k == nk - 1)(lambda: finish(acc_ref[...]))

        if riders:
            pl.when(step == steps - 1)(lambda: rs.run("end", bound))

    res = _pallas(
        body, name=name, grid=grid, in_specs=[a_spec, b_spec, *extra_specs, *rs.in_specs],
        out_specs=[*o_specs, *rs.out_specs], out_shape=[*outs, *rs.out_shapes], input_output_aliases=rs.aliases,
        scratch_shapes=[pltpu.VMEM(acc_shape if nk > 1 else (8, 128), F32), *rs.scratch],
        compiler_params=_params(*(("arbitrary",) * 3 if riders else ("parallel", "parallel", "arbitrary"))),
    )(a, b, *extras, *rs.arrays)
    return (res[:n_out], rs.split(res[n_out:])) if riders else res


def _sds(shape, dtype):
    return jax.ShapeDtypeStruct(shape, dtype)


def _cast_bf16(w, k_arr, name):
    r, c = w.shape
    tr = min(r, 256)

    def body(k_ref, w_ref, o_ref):
        o_ref[...] = w_ref[...].astype(BF16)

    grid_spec = pltpu.PrefetchScalarGridSpec(
        num_scalar_prefetch=1, grid=(r // tr,), in_specs=[pl.BlockSpec((tr, c), lambda i, k_ref: (i, 0))],
        out_specs=pl.BlockSpec((None, tr, c), lambda i, k_ref: (k_ref[0], i, 0)))
    return _pallas(body, name=name, grid_spec=grid_spec, out_shape=_sds((N_CHIPS, r, c), BF16),
                   compiler_params=_params("parallel"))(k_arr, w)


def _rmsnorm_fwd(x, g, name):
    s, d = x.shape
    tr = 256

    def body(x_ref, g_ref, o_ref):
        xv = x_ref[...]
        y = xv * lax.rsqrt(jnp.mean(xv * xv, axis=-1, keepdims=True) + EPS)
        o_ref[...] = (y * g_ref[...]).astype(o_ref.dtype)

    return _pallas(body, name=name, grid=(s // tr,),
                   in_specs=[pl.BlockSpec((tr, d), lambda i: (i, 0)), pl.BlockSpec((1, d), lambda i: (0, 0))],
                   out_specs=pl.BlockSpec((tr, d), lambda i: (i, 0)), out_shape=_sds((s, d), BF16),
                   compiler_params=_params("parallel"))(x, g)


def _rmsnorm_bwd(x, g, dh, res, name):
    s, d = x.shape
    tr = 256

    def body(x_ref, g_ref, dh_ref, res_ref, dx_ref, dxb_ref, dg_ref):
        i = pl.program_id(0)
        xv = x_ref[...]
        rstd = lax.rsqrt(jnp.mean(xv * xv, axis=-1, keepdims=True) + EPS)
        xh = xv * rstd
        dhv = dh_ref[...]

        @pl.when(i == 0)
        def _():
            dg_ref[...] = jnp.zeros_like(dg_ref)

        dg_ref[...] += jnp.sum(dhv * xh, axis=0, keepdims=True)
        dxh = dhv * g_ref[...]
        dx = res_ref[...] + rstd * (dxh - xh * jnp.mean(dxh * xh, axis=-1, keepdims=True))
        dx_ref[...] = dx
        dxb_ref[...] = dx.astype(BF16)

    row = pl.BlockSpec((tr, d), lambda i: (i, 0))
    vec = pl.BlockSpec((1, d), lambda i: (0, 0))
    return _pallas(body, name=name, grid=(s // tr,), in_specs=[row, vec, row, row], out_specs=[row, row, vec],
                   out_shape=[_sds((s, d), F32), _sds((s, d), BF16), _sds((1, d), F32)],
                   compiler_params=_params("arbitrary"))(x, g, dh, res)


def _adamw_math(w, g, m, v):
    m = ADAM_B1 * m + (1.0 - ADAM_B1) * g
    v = ADAM_B2 * v + (1.0 - ADAM_B2) * (g * g)
    m_hat = m / (1.0 - ADAM_B1 ** ADAM_STEP)
    v_hat = v / (1.0 - ADAM_B2 ** ADAM_STEP)
    delta = -ADAM_LR * (m_hat / (jnp.sqrt(v_hat) + ADAM_EPS) + ADAM_WD * w)
    return delta, m, v


def _adamw(w, g, m, v, name):
    r, c = w.shape
    tr = 128

    def body(w_ref, g_ref, m_ref, v_ref, d_ref, nm_ref, nv_ref):
        d_ref[...], nm_ref[...], nv_ref[...] = _adamw_math(w_ref[...], g_ref[...], m_ref[...], v_ref[...])

    blk = pl.BlockSpec((tr, c), lambda i: (i, 0))
    return _pallas(body, name=name, grid=(r // tr,), in_specs=[blk] * 4, out_specs=[blk] * 3,
                   out_shape=[_sds((r, c), F32)] * 3, compiler_params=_params("parallel"))(w, g, m, v)


def _tables(s):
    half = HEAD_DIM // 2
    pos = jnp.arange(s, dtype=F32)
    inv_freq = ROPE_BASE ** (-jnp.arange(half, dtype=F32) / half)
    ang = pos[:, None] * inv_freq[None, :]
    cos, sin = jnp.cos(ang), jnp.sin(ang)
    cos_f = jnp.concatenate([cos, cos], axis=-1)
    sin_f = jnp.concatenate([-sin, sin], axis=-1)
    log_g = jnp.log1p(-jnp.exp2(-(5.0 + jnp.arange(HEADS, dtype=F32))))
    p = jnp.arange(CHUNK, dtype=F32)
    decay = jnp.exp(log_g[:, None, None] * jnp.abs(p[:, None] - p[None, :]))
    k_dec = jnp.exp(log_g[None, :] * (CHUNK - 1.0 - p)[:, None])
    q_dec = jnp.exp(log_g[None, :] * (p + 1.0)[:, None])
    c_dec = jnp.exp(log_g * CHUNK)
    k_dec = jnp.broadcast_to(k_dec.T[:, :, None], (HEADS, CHUNK, HEAD_DIM))
    q_dec = jnp.broadcast_to(q_dec.T[:, :, None], (HEADS, CHUNK, HEAD_DIM))
    c_dec = jnp.broadcast_to(c_dec[:, None, None], (HEADS, 1, HEAD_DIM))
    return cos_f, sin_f, decay, k_dec, q_dec, c_dec


def _rot(x, cos_f, sin_f):
    return x * cos_f + pltpu.roll(x, HEAD_DIM // 2, 1) * sin_f


def _rot_bwd(d, cos_f, sin_f):
    return d * cos_f + pltpu.roll(d * sin_f, HEAD_DIM // 2, 1)


RET_BLOCK_CHUNKS = 8
RET_ROWS = RET_BLOCK_CHUNKS * CHUNK
K_SCALE = HEAD_DIM ** -0.5


def _retention_fwd(proj, gn_g, tables):
    s = proj.shape[0]
    nb = s // RET_ROWS
    nc = s // CHUNK
    cos_f, sin_f, decay, k_dec, q_dec, c_dec = tables

    def body(q_ref, k_ref, v_ref, g_ref, cos_ref, sin_ref, dec_ref, kd_ref, qd_ref, cd_ref, gn_ref,
             ret_ref, y_ref, prev_ref, state_ref):
        @pl.when(pl.program_id(1) == 0)
        def _():
            state_ref[...] = jnp.zeros_like(state_ref)

        cosv, sinv = cos_ref[...], sin_ref[...]
        q = _rot(q_ref[...], cosv, sinv)
        k = _rot(k_ref[...], cosv, sinv) * K_SCALE
        v = v_ref[...]
        rg = g_ref[...]
        dec, kd, qd, cd, gn = dec_ref[...], kd_ref[...], qd_ref[...], cd_ref[...], gn_ref[...]
        state = state_ref[...]
        for c in range(RET_BLOCK_CHUNKS):
            rows = slice(c * CHUNK, (c + 1) * CHUNK)
            qc, kc, vc = q[rows], k[rows], v[rows]
            sc = _dot(qc, kc, NT) * dec
            intra = _dot(sc, vc, NN)
            prev_ref[c] = state.astype(prev_ref.dtype)
            cross = _dot(qc * qd, state, NN)
            contrib = _dot(kc * kd, vc, TN)
            state = cd * state + contrib
            y = intra + cross
            y_ref[rows, :] = y
            mu = jnp.mean(y, axis=-1, keepdims=True)
            yc = y - mu
            var = jnp.mean(yc * yc, axis=-1, keepdims=True)
            yn = yc * lax.rsqrt(var + GN_EPS) * gn
            rgc = rg[rows]
            ret_ref[rows, :] = (rgc * jax.nn.sigmoid(rgc) * yn).astype(ret_ref.dtype)
        state_ref[...] = state

    def col(off):
        return pl.BlockSpec((RET_ROWS, HEAD_DIM), lambda h, i: (i, off + h))

    pos = pl.BlockSpec((RET_ROWS, HEAD_DIM), lambda h, i: (i, 0))
    per_head = lambda shape: pl.BlockSpec((None, *shape), lambda h, i: (h, 0, 0))
    return _pallas(
        body, name="retention_fwd", grid=(HEADS, nb),
        in_specs=[col(0), col(HEADS), col(2 * HEADS), col(3 * HEADS), pos, pos,
                  per_head((CHUNK, CHUNK)), per_head((CHUNK, HEAD_DIM)), per_head((CHUNK, HEAD_DIM)),
                  per_head((1, HEAD_DIM)), pl.BlockSpec((1, HEAD_DIM), lambda h, i: (0, h))],
        out_specs=[col(0), col(0),
                   pl.BlockSpec((None, RET_BLOCK_CHUNKS, HEAD_DIM, HEAD_DIM), lambda h, i: (h, i, 0, 0))],
        out_shape=[_sds((s, HEADS * HEAD_DIM), BF16), _sds((s, HEADS * HEAD_DIM), F32),
                   _sds((HEADS, nc, HEAD_DIM, HEAD_DIM), MXU_DTYPE)],
        scratch_shapes=[pltpu.VMEM((HEAD_DIM, HEAD_DIM), F32)],
        compiler_params=_params("parallel", "arbitrary"),
    )(proj, proj, proj, proj, cos_f, sin_f, decay, k_dec, q_dec, c_dec, gn_g)


def _retention_bwd(proj, gn_g, tables, y, prev, dmix):
    s = proj.shape[0]
    nb = s // RET_ROWS
    cos_f, sin_f, decay, k_dec, q_dec, c_dec = tables

    def body(q_ref, k_ref, v_ref, g_ref, cos_ref, sin_ref, dec_ref, kd_ref, qd_ref, cd_ref, gn_ref,
             y_ref, prev_ref, dret_ref, dq_ref, dk_ref, dv_ref, dg_ref, dgn_ref, gstate_ref):
        @pl.when(pl.program_id(1) == 0)
        def _():
            gstate_ref[...] = jnp.zeros_like(gstate_ref)
            dgn_ref[...] = jnp.zeros_like(dgn_ref)

        cosv, sinv = cos_ref[...], sin_ref[...]
        q = _rot(q_ref[...], cosv, sinv)
        k = _rot(k_ref[...], cosv, sinv) * K_SCALE
        v = v_ref[...]
        dec, kd, qd, cd, gn = dec_ref[...], kd_ref[...], qd_ref[...], cd_ref[...], gn_ref[...]
        rg = g_ref[...]
        yv = y_ref[...]
        dret = dret_ref[...]
        sig = jax.nn.sigmoid(rg)
        gate = rg * sig
        mu = jnp.mean(yv, axis=-1, keepdims=True)
        yc = yv - mu
        rstd = lax.rsqrt(jnp.mean(yc * yc, axis=-1, keepdims=True) + GN_EPS)
        z = yc * rstd
        dyn = dret * gate
        dg_ref[...] = (dret * (z * gn) * (sig * (1.0 + rg * (1.0 - sig)))).astype(dg_ref.dtype)
        dgn_ref[...] += jnp.sum(dyn * z, axis=0, keepdims=True)
        dz = dyn * gn
        dy = rstd * (dz - jnp.mean(dz, axis=-1, keepdims=True) - z * jnp.mean(dz * z, axis=-1, keepdims=True))
        gst = gstate_ref[...]
        for c in reversed(range(RET_BLOCK_CHUNKS)):
            rows = slice(c * CHUNK, (c + 1) * CHUNK)
            qc, kc, vc, dyc = q[rows], k[rows], v[rows], dy[rows]
            sc = _dot(qc, kc, NT) * dec
            dp = _dot(dyc, vc, NT)
            dvc = _dot(sc, dyc, TN)
            ds = dp * dec
            dqc = _dot(ds, kc, NN)
            dkc = _dot(ds, qc, TN)
            prevc = prev_ref[c]
            dqc += _dot(dyc, prevc, NT) * qd
            dprev = _dot(qc * qd, dyc, TN)
            dkc += _dot(vc, gst, NT) * kd
            dvc += _dot(kc * kd, gst, NN)
            gst = dprev + cd * gst
            dq_ref[rows, :] = _rot_bwd(dqc, cosv[rows], sinv[rows]).astype(dq_ref.dtype)
            dk_ref[rows, :] = _rot_bwd(dkc * K_SCALE, cosv[rows], sinv[rows]).astype(dk_ref.dtype)
            dv_ref[rows, :] = dvc.astype(dv_ref.dtype)
        gstate_ref[...] = gst

    rev = lambda i: nb - 1 - i

    def col(off):
        return pl.BlockSpec((RET_ROWS, HEAD_DIM), lambda h, i: (rev(i), off + h))

    pos = pl.BlockSpec((RET_ROWS, HEAD_DIM), lambda h, i: (rev(i), 0))
    per_head = lambda shape: pl.BlockSpec((None, *shape), lambda h, i: (h, 0, 0))
    outb = _sds((s, HEADS * HEAD_DIM), BF16)
    return _pallas(
        body, name="retention_bwd", grid=(HEADS, nb),
        in_specs=[col(0), col(HEADS), col(2 * HEADS), col(3 * HEADS), pos, pos,
                  per_head((CHUNK, CHUNK)), per_head((CHUNK, HEAD_DIM)), per_head((CHUNK, HEAD_DIM)),
                  per_head((1, HEAD_DIM)), pl.BlockSpec((1, HEAD_DIM), lambda h, i: (0, h)),
                  col(0), pl.BlockSpec((None, RET_BLOCK_CHUNKS, HEAD_DIM, HEAD_DIM), lambda h, i: (h, rev(i), 0, 0)),
                  col(0)],
        out_specs=[col(0), col(0), col(0), col(0), per_head((1, HEAD_DIM))],
        out_shape=[outb, outb, outb, outb, _sds((HEADS, 1, HEAD_DIM), F32)],
        scratch_shapes=[pltpu.VMEM((HEAD_DIM, HEAD_DIM), F32)],
        compiler_params=_params("parallel", "arbitrary"),
    )(proj, proj, proj, proj, cos_f, sin_f, decay, k_dec, q_dec, c_dec, gn_g, y, prev, dmix)


ATT_COL0 = 4 * HEADS
PAD_ROWS = LEFT_CHUNKS * CHUNK
NORM_ROWS = 512
GROUP_CHUNKS = 4
GROUP = GROUP_CHUNKS * CHUNK
WIN = (LEFT_CHUNKS + GROUP_CHUNKS) * CHUNK
MASKED = -1e30


def _qk_norm(x, g):
    return x * lax.rsqrt(jnp.mean(x * x, axis=-1, keepdims=True) + EPS) * g


def _band_probs(qb, kb, bias, g):
    sc = _dot(qb, kb, NT) * K_SCALE + bias
    win_chunk = lax.broadcasted_iota(jnp.int32, (GROUP, WIN), 1) // CHUNK
    sc = jnp.where(g * GROUP_CHUNKS - LEFT_CHUNKS + win_chunk >= 0, sc, MASKED)
    e = jnp.exp(sc - jnp.max(sc, axis=-1, keepdims=True))
    return e / jnp.sum(e, axis=-1, keepdims=True)


def _with_riders(core, n_in, n_out, n_scratch, rs, steps):
    n_rin, n_rout = len(rs.arrays), len(rs.out_shapes)

    def body(*refs):
        outs_at = n_in + n_rin
        scratch_at = outs_at + n_out + n_rout
        bound = rs.bind(refs[n_in:outs_at], refs[outs_at + n_out:scratch_at], refs[scratch_at + n_scratch:])
        step = pl.program_id(0)
        pl.when(step == 0)(lambda: rs.run("start", bound))
        pl.when(step == int(steps * RIDER_MID))(lambda: rs.run("mid", bound))
        core(*refs[:n_in], *refs[outs_at:outs_at + n_out], *refs[scratch_at:scratch_at + n_scratch])
        pl.when(step == steps - 1)(lambda: rs.run("end", bound))

    return body


def _attention_fwd(proj, gq, gk, bias, riders=()):
    s = proj.shape[0]
    nc = s // CHUNK
    rs = _Riders(riders, 6, 1)

    def body(q_ref, k_ref, v_ref, gq_ref, gk_ref, bias_ref, o_ref, kp_ref, vp_ref):
        kp_ref[0:PAD_ROWS, :] = jnp.zeros((PAD_ROWS, HEAD_DIM), kp_ref.dtype)
        vp_ref[0:PAD_ROWS, :] = jnp.zeros((PAD_ROWS, HEAD_DIM), vp_ref.dtype)
        gqv, gkv = gq_ref[...], gk_ref[...]

        def fill(b, carry):
            r0 = pl.multiple_of(b * NORM_ROWS, NORM_ROWS)
            kp_ref[pl.ds(PAD_ROWS + r0, NORM_ROWS), :] = _qk_norm(k_ref[pl.ds(r0, NORM_ROWS), :], gkv).astype(kp_ref.dtype)
            vp_ref[pl.ds(PAD_ROWS + r0, NORM_ROWS), :] = v_ref[pl.ds(r0, NORM_ROWS), :].astype(vp_ref.dtype)
            return carry

        lax.fori_loop(0, s // NORM_ROWS, fill, 0)

        def group(g, carry):
            r0 = pl.multiple_of(g * GROUP, GROUP)
            qn = _qk_norm(q_ref[pl.ds(r0, GROUP), :], gqv)
            p = _band_probs(qn, kp_ref[pl.ds(r0, WIN), :], bias_ref[...], g)
            o_ref[pl.ds(r0, GROUP), :] = _dot(p, vp_ref[pl.ds(r0, WIN), :], NN).astype(o_ref.dtype)
            return carry

        lax.fori_loop(0, s // GROUP, group, 0)

    def col(off):
        return pl.BlockSpec((s, HEAD_DIM), lambda h: (0, off + h))

    vec = pl.BlockSpec((1, HEAD_DIM), lambda h: (0, 0))
    res = _pallas(
        _with_riders(body, 6, 1, 2, rs, HEADS), name="attention_fwd", grid=(HEADS,),
        in_specs=[col(ATT_COL0), col(ATT_COL0 + HEADS), col(ATT_COL0 + 2 * HEADS), vec, vec,
                  pl.BlockSpec((None, GROUP, WIN), lambda h: (h, 0, 0)), *rs.in_specs],
        out_specs=[col(0), *rs.out_specs], out_shape=[_sds((s, HEADS * HEAD_DIM), BF16), *rs.out_shapes],
        input_output_aliases=rs.aliases,
        scratch_shapes=[pltpu.VMEM((s + PAD_ROWS, HEAD_DIM), MXU_DTYPE), pltpu.VMEM((s + PAD_ROWS, HEAD_DIM), MXU_DTYPE),
                        *rs.scratch],
        compiler_params=_params("arbitrary"),
    )(proj, proj, proj, gq, gk, bias, *rs.arrays)
    return res[0], rs.split(res[1:])


def _attention_bwd(proj, gq, gk, bias, dmix, riders=()):
    s = proj.shape[0]
    nc = s // CHUNK
    rs = _Riders(riders, 7, 6)

    def body(q_ref, k_ref, v_ref, gq_ref, gk_ref, bias_ref, do_ref,
             dq_ref, dk_ref, dv_ref, dgq_ref, dgk_ref, dbias_ref, kp_ref, vp_ref, dkp_ref, dvp_ref, dqn_ref):
        kp_ref[0:PAD_ROWS, :] = jnp.zeros((PAD_ROWS, HEAD_DIM), kp_ref.dtype)
        vp_ref[0:PAD_ROWS, :] = jnp.zeros((PAD_ROWS, HEAD_DIM), vp_ref.dtype)
        dkp_ref[...] = jnp.zeros_like(dkp_ref)
        dvp_ref[...] = jnp.zeros_like(dvp_ref)
        dbias_ref[...] = jnp.zeros_like(dbias_ref)
        gqv, gkv = gq_ref[...], gk_ref[...]

        def fill(b, carry):
            r0 = pl.multiple_of(b * NORM_ROWS, NORM_ROWS)
            kp_ref[pl.ds(PAD_ROWS + r0, NORM_ROWS), :] = _qk_norm(k_ref[pl.ds(r0, NORM_ROWS), :], gkv).astype(kp_ref.dtype)
            vp_ref[pl.ds(PAD_ROWS + r0, NORM_ROWS), :] = v_ref[pl.ds(r0, NORM_ROWS), :].astype(vp_ref.dtype)
            return carry

        lax.fori_loop(0, s // NORM_ROWS, fill, 0)

        def group(g, carry):
            r0 = pl.multiple_of(g * GROUP, GROUP)
            qn = _qk_norm(q_ref[pl.ds(r0, GROUP), :], gqv)
            kb = kp_ref[pl.ds(r0, WIN), :]
            vb = vp_ref[pl.ds(r0, WIN), :]
            p = _band_probs(qn, kb, bias_ref[...], g)
            do = do_ref[pl.ds(r0, GROUP), :]
            dvp_ref[pl.ds(r0, WIN), :] += _dot(p, do, TN)
            dp = _dot(do, vb, NT)
            ds = p * (dp - jnp.sum(dp * p, axis=-1, keepdims=True))
            dbias_ref[...] += ds
            dss = ds * K_SCALE
            dqn_ref[pl.ds(r0, GROUP), :] = _dot(dss, kb, NN)
            dkp_ref[pl.ds(r0, WIN), :] += _dot(dss, qn, TN)
            return carry

        lax.fori_loop(0, s // GROUP, group, 0)

        @pl.when(pl.program_id(0) == 0)
        def _():
            dgq_ref[...] = jnp.zeros_like(dgq_ref)
            dgk_ref[...] = jnp.zeros_like(dgk_ref)

        def norm_bwd(x, g, dn):
            rstd = lax.rsqrt(jnp.mean(x * x, axis=-1, keepdims=True) + EPS)
            xh = x * rstd
            dxh = dn * g
            return rstd * (dxh - xh * jnp.mean(dxh * xh, axis=-1, keepdims=True)), jnp.sum(dn * xh, axis=0, keepdims=True)

        def finish(b, carry):
            r0 = pl.multiple_of(b * NORM_ROWS, NORM_ROWS)
            rows = pl.ds(r0, NORM_ROWS)
            dq, dgq = norm_bwd(q_ref[rows, :], gqv, dqn_ref[rows, :])
            dk, dgk = norm_bwd(k_ref[rows, :], gkv, dkp_ref[pl.ds(PAD_ROWS + r0, NORM_ROWS), :])
            dq_ref[rows, :] = dq.astype(dq_ref.dtype)
            dk_ref[rows, :] = dk.astype(dk_ref.dtype)
            dv_ref[rows, :] = dvp_ref[pl.ds(PAD_ROWS + r0, NORM_ROWS), :].astype(dv_ref.dtype)
            dgq_ref[...] += dgq
            dgk_ref[...] += dgk
            return carry

        lax.fori_loop(0, s // NORM_ROWS, finish, 0)

    def col(off):
        return pl.BlockSpec((s, HEAD_DIM), lambda h: (0, off + h))

    vec = pl.BlockSpec((1, HEAD_DIM), lambda h: (0, 0))
    hbias = pl.BlockSpec((None, GROUP, WIN), lambda h: (h, 0, 0))
    outb = _sds((s, HEADS * HEAD_DIM), BF16)
    res = _pallas(
        _with_riders(body, 7, 6, 5, rs, HEADS), name="attention_bwd", grid=(HEADS,),
        in_specs=[col(ATT_COL0), col(ATT_COL0 + HEADS), col(ATT_COL0 + 2 * HEADS), vec, vec, hbias, col(HEADS),
                  *rs.in_specs],
        out_specs=[col(0), col(0), col(0), vec, vec, hbias, *rs.out_specs],
        out_shape=[outb, outb, outb, _sds((1, HEAD_DIM), F32), _sds((1, HEAD_DIM), F32),
                   _sds((HEADS, GROUP, WIN), F32), *rs.out_shapes],
        input_output_aliases=rs.aliases,
        scratch_shapes=[pltpu.VMEM((s + PAD_ROWS, HEAD_DIM), MXU_DTYPE), pltpu.VMEM((s + PAD_ROWS, HEAD_DIM), MXU_DTYPE),
                        pltpu.VMEM((s + PAD_ROWS, HEAD_DIM), F32), pltpu.VMEM((s + PAD_ROWS, HEAD_DIM), F32),
                        pltpu.VMEM((s, HEAD_DIM), F32), *rs.scratch],
        compiler_params=_params("arbitrary"),
    )(proj, proj, proj, gq, gk, bias, dmix, *rs.arrays)
    return res[:6], rs.split(res[6:])


DIAG_SPLIT = (BAND + WIN - CHUNK) // 2


def _diag_bin(m):
    t = jnp.where(m < DIAG_SPLIT, m, m - WIN)
    return jnp.clip(LEFT_CHUNKS * CHUNK - t, -(CHUNK - 1), REL_CLIP) + (CHUNK - 1)


def _skew_rows(a, left):
    row = lax.broadcasted_iota(jnp.int32, (GROUP, WIN), 0)
    for b in range(GROUP.bit_length() - 1):
        step = 1 << b
        a = jnp.where(jnp.bitwise_and(row, step) != 0, pltpu.roll(a, WIN - step if left else step, 1), a)
    return a


def _rel_bias_expand(rel_bias):
    def body(rb_ref, o_ref):
        h = pl.program_id(0)
        bins = _diag_bin(lax.broadcasted_iota(jnp.int32, (8, WIN), 1))
        per_diag = lax.fori_loop(0, REL_SIZE, lambda r, acc: jnp.where(bins == r, rb_ref[h, r], acc),
                                 jnp.zeros((8, WIN), F32))
        table = _skew_rows(jnp.broadcast_to(per_diag[0:1], (GROUP, WIN)), left=False)
        row_chunk = lax.broadcasted_iota(jnp.int32, (GROUP, WIN), 0) // CHUNK
        col_chunk = lax.broadcasted_iota(jnp.int32, (GROUP, WIN), 1) // CHUNK
        in_band = jnp.logical_and(col_chunk >= row_chunk, col_chunk <= row_chunk + LEFT_CHUNKS)
        o_ref[...] = jnp.where(in_band, table, MASKED)

    return _pallas(body, name="rel_bias_expand", grid=(HEADS,), in_specs=[pl.BlockSpec(memory_space=pltpu.SMEM)],
                   out_specs=pl.BlockSpec((None, GROUP, WIN), lambda h: (h, 0, 0)),
                   out_shape=_sds((HEADS, GROUP, WIN), F32), compiler_params=_params("parallel"))(rel_bias)


def _rel_bias_fold(dbias):
    def body(a_ref, o_ref):
        diag = jnp.sum(_skew_rows(a_ref[...], left=True), axis=0, keepdims=True)
        onehot = (_diag_bin(lax.broadcasted_iota(jnp.int32, (WIN, REL_SIZE), 0))
                  == lax.broadcasted_iota(jnp.int32, (WIN, REL_SIZE), 1)).astype(MXU_DTYPE)
        rest = jnp.broadcast_to(diag, (8, WIN))
        out = jnp.zeros((8, REL_SIZE), F32)
        for _ in range(3):
            piece = rest.astype(BF16)
            out = out + _dot(piece, onehot, NN)
            rest = rest - piece.astype(F32)
        o_ref[...] = out[0:1]

    return _pallas(body, name="rel_bias_fold", grid=(HEADS,),
                   in_specs=[pl.BlockSpec((None, GROUP, WIN), lambda h: (h, 0, 0))],
                   out_specs=pl.BlockSpec((None, 1, REL_SIZE), lambda h: (h, 0, 0)),
                   out_shape=_sds((HEADS, 1, REL_SIZE), F32), compiler_params=_params("parallel"))(dbias)


def _place():
    return lax.axis_index("x"), lax.axis_index("y"), lax.axis_index("c")


def _other_chips(x, y):
    return [(1 - x, y), (x, 1 - y), (1 - x, 1 - y)]


class _Rider:
    reads, ins, new, n_sems = (), (), (), 1

    def start(self, reads, ins, new, send, recv):
        pass

    def mid(self, reads, ins, new, send, recv):
        pass

    def end(self, reads, ins, new, send, recv):
        pass


class _Riders:
    def __init__(self, riders, n_host_in, n_host_out):
        self.riders = list(riders)
        self.arrays, self.out_shapes, self.aliases, self.scratch = [], [], {}, []
        for r in self.riders:
            for t, a in enumerate(r.ins):
                self.aliases[n_host_in + len(self.arrays) + len(r.reads) + t] = n_host_out + len(self.out_shapes) + t
            self.arrays += [*r.reads, *r.ins]
            self.out_shapes += [_sds(a.shape, a.dtype) for a in r.ins] + list(r.new)
            self.scratch += [pltpu.SemaphoreType.DMA((r.n_sems,)), pltpu.SemaphoreType.DMA((r.n_sems,))]
        self.in_specs = [ANY] * len(self.arrays)
        self.out_specs = [ANY] * len(self.out_shapes)

    def bind(self, in_refs, out_refs, scratch_refs):
        bound, i, o = [], 0, 0
        for t, r in enumerate(self.riders):
            reads = in_refs[i:i + len(r.reads)]
            i += len(r.reads) + len(r.ins)
            ins = out_refs[o:o + len(r.ins)]
            new = out_refs[o + len(r.ins):o + len(r.ins) + len(r.new)]
            o += len(r.ins) + len(r.new)
            bound.append((reads, ins, new, scratch_refs[2 * t], scratch_refs[2 * t + 1]))
        return bound

    def run(self, phase, bound):
        for r, b in zip(self.riders, bound):
            getattr(r, phase)(*b)

    def split(self, outs):
        res, o = [], 0
        for r in self.riders:
            n = len(r.ins) + len(r.new)
            res.append(list(outs[o:o + n]))
            o += n
        return res


def _run_riders(name, riders):
    rs = _Riders(riders, 0, 0)
    n_in, n_out = len(rs.arrays), len(rs.out_shapes)

    def body(*refs):
        bound = rs.bind(refs[:n_in], refs[n_in:n_in + n_out], refs[n_in + n_out:])
        rs.run("start", bound)
        rs.run("mid", bound)
        rs.run("end", bound)

    outs = _pallas(body, name=name, in_specs=rs.in_specs, out_specs=rs.out_specs, out_shape=rs.out_shapes,
                   input_output_aliases=rs.aliases, scratch_shapes=rs.scratch)(*rs.arrays)
    return rs.split(outs)


class _GatherRider(_Rider):
    def __init__(self, blocks):
        self.ins = tuple(blocks)
        self.n_sems = 6 * len(blocks)

    def _copy(self, out, send, recv, w, j, chip_from, cc, to):
        hr = self.ins[w].shape[1] // 2
        half = out[w].at[2 * chip_from[0] + chip_from[1], pl.ds(cc * hr, hr), :]
        return pltpu.make_async_remote_copy(src_ref=half, dst_ref=half, send_sem=send.at[6 * w + j],
                                            recv_sem=recv.at[6 * w + j], device_id=to, device_id_type=MESH)

    def start(self, reads, out, new, send, recv):
        x, y, c = _place()
        for w in range(len(self.ins)):
            for j, chip in enumerate(_other_chips(x, y)):
                self._copy(out, send, recv, w, j, (x, y), c, (*chip, c)).start()

    def mid(self, reads, out, new, send, recv):
        x, y, c = _place()
        for w in range(len(self.ins)):
            for j, chip in enumerate(_other_chips(x, y)):
                self._copy(out, send, recv, w, j, chip, c, (x, y, c)).wait_recv()
                self._copy(out, send, recv, w, 3 + j, chip, c, (x, y, 1 - c)).start()

    def end(self, reads, out, new, send, recv):
        x, y, c = _place()
        for w in range(len(self.ins)):
            for j, chip in enumerate(_other_chips(x, y)):
                self._copy(out, send, recv, w, 3 + j, chip, 1 - c, (x, y, c)).wait_recv()
        for w in range(len(self.ins)):
            for j, chip in enumerate(_other_chips(x, y)):
                self._copy(out, send, recv, w, j, (x, y), c, (*chip, c)).wait_send()
                self._copy(out, send, recv, w, 3 + j, chip, c, (x, y, 1 - c)).wait_send()


class _SwapRider(_Rider):
    def __init__(self, grads):
        self.reads = tuple(grads)
        self.new = tuple(_sds((N_CHIPS, g.shape[1] // 2, g.shape[2]), g.dtype) for g in grads)
        self.n_sems = len(grads)

    def _copies(self, src, new, send, recv):
        x, y, c = _place()
        copies = []
        for w in range(len(self.reads)):
            hr = self.reads[w].shape[1] // 2
            copies.append(pltpu.make_async_remote_copy(
                src_ref=src[w].at[:, pl.ds((1 - c) * hr, hr), :], dst_ref=new[w],
                send_sem=send.at[w], recv_sem=recv.at[w], device_id=(x, y, 1 - c), device_id_type=MESH))
        return copies

    def start(self, src, ins, new, send, recv):
        for cp in self._copies(src, new, send, recv):
            cp.start()

    def end(self, src, ins, new, send, recv):
        for cp in self._copies(src, new, send, recv):
            cp.wait()


def _add_half(g, got, c_arr, name):
    nk, r, cols = g.shape
    hr = r // 2
    tr = min(hr, 256)
    nb = hr // tr

    def body(c_ref, g_ref, got_ref, o_ref):
        o_ref[...] = (g_ref[...].astype(F32) + got_ref[...].astype(F32)).astype(o_ref.dtype)

    grid_spec = pltpu.PrefetchScalarGridSpec(
        num_scalar_prefetch=1, grid=(nk, nb),
        in_specs=[pl.BlockSpec((None, tr, cols), lambda k, i, c_ref: (k, c_ref[0] * nb + i, 0)),
                  pl.BlockSpec((None, tr, cols), lambda k, i, c_ref: (k, i, 0))],
        out_specs=pl.BlockSpec((None, tr, cols), lambda k, i, c_ref: (k, i, 0)))
    return _pallas(body, name=name, grid_spec=grid_spec, out_shape=_sds((nk, hr, cols), g.dtype),
                   compiler_params=_params("parallel", "parallel"))(c_arr, g, got)


class _SendPartialsRider(_Rider):
    def __init__(self, parts):
        self.reads = tuple(parts)
        self.new = tuple(_sds((N_CHIPS - 1, *p.shape[1:]), p.dtype) for p in parts)
        self.n_sems = 3 * len(parts)

    def _copies(self, src, new, send, recv):
        x, y, c = _place()
        copies = []
        for w in range(len(self.reads)):
            for j, chip in enumerate(_other_chips(x, y)):
                copies.append(pltpu.make_async_remote_copy(
                    src_ref=src[w].at[2 * chip[0] + chip[1]], dst_ref=new[w].at[j],
                    send_sem=send.at[3 * w + j], recv_sem=recv.at[3 * w + j], device_id=(*chip, c), device_id_type=MESH))
        return copies

    def start(self, src, ins, new, send, recv):
        for cp in self._copies(src, new, send, recv):
            cp.start()

    def end(self, src, ins, new, send, recv):
        for cp in self._copies(src, new, send, recv):
            cp.wait()


def _sum_partials(part, got, kc_arr, name):
    _, hr, cols = part.shape
    tr = min(hr, 256)
    nb = hr // tr

    def body(kc_ref, p_ref, g0_ref, g1_ref, g2_ref, o_ref):
        o_ref[...] = ((p_ref[...].astype(F32) + g0_ref[...].astype(F32)) + g1_ref[...].astype(F32)) + g2_ref[...].astype(F32)

    slot = lambda j: pl.BlockSpec((None, tr, cols), lambda i, kc_ref: (j, i, 0))
    grid_spec = pltpu.PrefetchScalarGridSpec(
        num_scalar_prefetch=1, grid=(nb,),
        in_specs=[pl.BlockSpec((None, tr, cols), lambda i, kc_ref: (kc_ref[0], i, 0)), slot(0), slot(1), slot(2)],
        out_specs=pl.BlockSpec((tr, cols), lambda i, kc_ref: (kc_ref[1] * nb + i, 0)))
    return _pallas(body, name=name, grid_spec=grid_spec, out_shape=_sds((2 * hr, cols), F32),
                   compiler_params=_params("parallel"))(kc_arr, part, got, got, got)


class _ShareRider(_Rider):
    def __init__(self, grads):
        self.ins = tuple(grads)
        self.n_sems = len(grads)

    def _copies(self, out, send, recv):
        x, y, c = _place()
        copies = []
        for w in range(len(self.ins)):
            hr = self.ins[w].shape[0] // 2
            mine = out[w].at[pl.ds(c * hr, hr), :]
            copies.append(pltpu.make_async_remote_copy(
                src_ref=mine, dst_ref=mine, send_sem=send.at[w], recv_sem=recv.at[w],
                device_id=(x, y, 1 - c), device_id_type=MESH))
        return copies

    def start(self, reads, out, new, send, recv):
        for cp in self._copies(out, send, recv):
            cp.start()

    def end(self, reads, out, new, send, recv):
        for cp in self._copies(out, send, recv):
            cp.wait()


def _small_allreduce_adamw(g_part, w, m, v):
    rows = g_part.shape[0]

    def body(g_ref, w_ref, m_ref, v_ref, go_ref, d_ref, nm_ref, nv_ref, all_ref, send_sems, recv_sems):
        x, y, c = _place()
        me = 4 * x + 2 * y + c
        all_ref[me] = g_ref[...]
        copies = []
        for r in range(1, 8):
            dx, dy, dc = (r >> 2) & 1, (r >> 1) & 1, r & 1
            peer = (1 - x if dx else x, 1 - y if dy else y, 1 - c if dc else c)
            copies.append(pltpu.make_async_remote_copy(
                src_ref=g_ref, dst_ref=all_ref.at[me], send_sem=send_sems.at[r - 1], recv_sem=recv_sems.at[r - 1],
                device_id=peer, device_id_type=MESH))
        for cp in copies:
            cp.start()
        for cp in copies:
            cp.wait()
        tot = all_ref[0]
        for d in range(1, 8):
            tot = tot + all_ref[d]
        go_ref[...] = tot
        d_ref[...], nm_ref[...], nv_ref[...] = _adamw_math(w_ref[...], tot, m_ref[...], v_ref[...])

    vm = pl.BlockSpec(memory_space=pltpu.VMEM)
    return _pallas(
        body, name="small_allreduce_adamw", in_specs=[vm] * 4, out_specs=[vm] * 4,
        out_shape=[_sds((rows, 128), F32)] * 4,
        scratch_shapes=[pltpu.VMEM((8, rows, 128), F32), pltpu.SemaphoreType.DMA((7,)), pltpu.SemaphoreType.DMA((7,))],
    )(g_part, w, m, v)


SMALL_SIZES = (2048, 1024, 128, 128, HEADS * REL_SIZE, 2048)
SMALL_PART_ROWS = tuple(-(-size // 1024) * 8 for size in SMALL_SIZES)
SMALL_ROWS = sum(SMALL_PART_ROWS)


def _pack_small(parts):
    rows = []
    for p, size, nr in zip(parts, SMALL_SIZES, SMALL_PART_ROWS):
        rows.append(jnp.pad(p.reshape(-1), (0, nr * 128 - size)).reshape(nr, 128))
    return jnp.concatenate(rows, axis=0)


def _unpack_small(slab, shapes):
    out, off = [], 0
    for size, nr, shape in zip(SMALL_SIZES, SMALL_PART_ROWS, shapes):
        out.append(slab[off:off + nr].reshape(-1)[:size].reshape(shape))
        off += nr
    return out


def kernel(x, norm1_g, w_in, ret_norm_g, q_norm_g, k_norm_g, rel_bias, w_out, norm2_g, w_ff1, w_ff2, loss_target, m_norm1_g, m_w_in, m_ret_norm_g, m_q_norm_g, m_k_norm_g, m_rel_bias, m_w_out, m_norm2_g, m_w_ff1, m_w_ff2, v_norm1_g, v_w_in, v_ret_norm_g, v_q_norm_g, v_k_norm_g, v_rel_bias, v_w_out, v_norm2_g, v_w_ff1, v_w_ff2):
    xs = x[0]
    tgt = loss_target[0]
    s, d = xs.shape
    d_in = N_CHIPS * w_in.shape[2]
    d_ff = N_CHIPS * w_ff1.shape[2]
    in_sh, ff_sh = w_in.shape[2], w_ff1.shape[2]
    tm = min(s, 1024)
    gi = s // tm
    c_arr = lax.axis_index("c").astype(jnp.int32).reshape(1)
    k_arr = (2 * lax.axis_index("x") + lax.axis_index("y")).astype(jnp.int32).reshape(1)
    tables = _tables(s)
    bias = _rel_bias_expand(rel_bias[0])

    blk_in, blk_out, blk_ff1, blk_ff2 = (
        _cast_bf16(w_in[0], k_arr, "cast_w_in"), _cast_bf16(w_out[0], k_arr, "cast_w_out"),
        _cast_bf16(w_ff1[0], k_arr, "cast_w_ff1"), _cast_bf16(w_ff2[0], k_arr, "cast_w_ff2"))
    ((wg_in,),) = _run_riders("all_gather_w_in", [_GatherRider([blk_in])])

    h1 = _rmsnorm_fwd(xs, norm1_g, "rmsnorm1")
    tn_in = in_sh // 2
    tk = d
    (proj,), ((wg_ff1,),) = _mm(
        "proj", h1, wg_in, NN, (gi, 2 * N_CHIPS, d // tk),
        pl.BlockSpec((tm, tk), lambda i, j, k: (i, k)),
        pl.BlockSpec((None, tk, tn_in), lambda i, j, k: (j // 2, k, j % 2)),
        [_sds((s, d_in), F32)], [pl.BlockSpec((tm, tn_in), lambda i, j, k: (i, j))], (tm, tn_in),
        riders=[_GatherRider([blk_ff1])])
    ret, y_ret, prev = _retention_fwd(proj, ret_norm_g, tables)
    att, ((wg_out, wg_ff2),) = _attention_fwd(proj, q_norm_g, k_norm_g, bias, riders=[_GatherRider([blk_out, blk_ff2])])
    wg_out = wg_out.reshape(d, d)
    wg_ff2 = wg_ff2.reshape(d_ff, d)
    mix = jnp.concatenate([ret, att], axis=-1)
    tn = 1024
    tile = pl.BlockSpec((tm, tn), lambda i, j, k: (i, j))
    (x1,) = _mm("out_proj", mix, wg_out, NN, (gi, d // tn, d // tk),
                pl.BlockSpec((tm, tk), lambda i, j, k: (i, k)), pl.BlockSpec((tk, tn), lambda i, j, k: (k, j)),
                [_sds((s, d), F32)], [tile], (tm, tn), extras=(xs,), extra_specs=(tile,),
                epi=lambda acc, r: (r + acc,))
    h2 = _rmsnorm_fwd(x1, norm2_g, "rmsnorm2")
    tn_ff = min(ff_sh, 1024)
    per = ff_sh // tn_ff

    def relu2(acc):
        r = jnp.maximum(acc, 0.0)
        return acc, r * r

    u, act = _mm("ff1", h2, wg_ff1, NN, (gi, N_CHIPS * per, d // tk),
                 pl.BlockSpec((tm, tk), lambda i, j, k: (i, k)),
                 pl.BlockSpec((None, tk, tn_ff), lambda i, j, k: (j // per, k, j % per)),
                 [_sds((s, d_ff), F32), _sds((s, d_ff), BF16)],
                 [pl.BlockSpec((tm, tn_ff), lambda i, j, k: (i, j))] * 2, (tm, tn_ff), epi=relu2)

    def loss_epi(acc, res, t):
        diff = (res + acc) - t
        dy = diff / d
        return dy, dy, jnp.sum(diff * diff, axis=0, keepdims=True)

    tn2 = 512
    tile2 = pl.BlockSpec((tm, tn2), lambda i, j, k: (i, j))
    dy, dyb, loss_cols = _mm(
        "ff2_loss", act, wg_ff2, NN, (gi, d // tn2, d_ff // tk),
        pl.BlockSpec((tm, tk), lambda i, j, k: (i, k)), pl.BlockSpec((tk, tn2), lambda i, j, k: (k, j)),
        [_sds((s, d), F32), _sds((s, d), BF16), _sds((gi, 1, d), F32)],
        [tile2, tile2, pl.BlockSpec((None, 1, tn2), lambda i, j, k: (i, 0, j))], (tm, tn2),
        extras=(x1, tgt), extra_specs=(tile2, tile2), epi=loss_epi)
    loss = lax.psum(0.5 * jnp.sum(loss_cols) / d, ("x", "y", "c"))

    (du,) = _mm("d_act", dyb, wg_ff2, NT, (gi, d_ff // tn, d // tk),
                pl.BlockSpec((tm, tk), lambda i, j, k: (i, k)), pl.BlockSpec((tn, tk), lambda i, j, k: (j, k)),
                [_sds((s, d_ff), BF16)], [tile], (tm, tn), extras=(u,), extra_specs=(tile,),
                epi=lambda acc, uu: (acc * (2.0 * jnp.maximum(uu, 0.0)),))
    ts = min(s, 2048)
    wtile = pl.BlockSpec((tn, tn), lambda i, j, k: (i, j))
    (g_ff2,) = _mm("dw_ff2", act, dyb, TN, (d_ff // tn, d // tn, s // ts),
                   pl.BlockSpec((ts, tn), lambda i, j, k: (k, i)), pl.BlockSpec((ts, tn), lambda i, j, k: (k, j)),
                   [_sds((d_ff, d), BF16)], [wtile], (tn, tn))
    g_ff2 = g_ff2.reshape(N_CHIPS, d_ff // N_CHIPS, d)
    (g_ff1,), ((got_ff2,),) = _mm(
        "dw_ff1", h2, du, TN, (d // tn, N_CHIPS * per, s // ts),
        pl.BlockSpec((ts, tn), lambda i, j, k: (k, i)), pl.BlockSpec((ts, tn_ff), lambda i, j, k: (k, j)),
        [_sds((N_CHIPS, d, ff_sh), BF16)],
        [pl.BlockSpec((None, tn, tn_ff), lambda i, j, k: (j // per, i, j % per))], (tn, tn_ff),
        riders=[_SwapRider([g_ff2])])
    p_ff2 = _add_half(g_ff2, got_ff2, c_arr, "chip_partial_w_ff2")
    tkf = min(tk, ff_sh)
    kper = ff_sh // tkf
    (dh2,), ((got2_ff2,), (got_ff1,)) = _mm(
        "d_h2", du, wg_ff1, NT, (gi, d // tn, d_ff // tkf),
        pl.BlockSpec((tm, tkf), lambda i, j, k: (i, k)),
        pl.BlockSpec((None, tn, tkf), lambda i, j, k: (k // kper, j, k % kper)),
        [_sds((s, d), F32)], [tile], (tm, tn), riders=[_SendPartialsRider([p_ff2]), _SwapRider([g_ff1])])
    p_ff1 = _add_half(g_ff1, got_ff1, c_arr, "chip_partial_w_ff1")
    dx1, dx1b, g_norm2 = _rmsnorm_bwd(x1, norm2_g, dh2, dy, "rmsnorm2_bwd")

    (dmix,) = _mm("d_mix", dx1b, wg_out, NT, (gi, d // tn, d // tk),
                  pl.BlockSpec((tm, tk), lambda i, j, k: (i, k)), pl.BlockSpec((tn, tk), lambda i, j, k: (j, k)),
                  [_sds((s, d), F32)], [tile], (tm, tn))
    (g_out,) = _mm("dw_out", mix, dx1b, TN, (d // tn, d // tn, s // ts),
                   pl.BlockSpec((ts, tn), lambda i, j, k: (k, i)), pl.BlockSpec((ts, tn), lambda i, j, k: (k, j)),
                   [_sds((d, d), BF16)], [wtile], (tn, tn))
    g_out = g_out.reshape(N_CHIPS, d // N_CHIPS, d)
    ((got_out,),) = _run_riders("grad_swap_w_out", [_SwapRider([g_out])])
    p_out = _add_half(g_out, got_out, c_arr, "chip_partial_w_out")
    d_rq, d_rk, d_rv, d_rg, g_gn = _retention_bwd(proj, ret_norm_g, tables, y_ret, prev, dmix)
    (d_aq, d_ak, d_av, g_gq, g_gk, dbias), ((got2_ff1, got2_out),) = _attention_bwd(
        proj, q_norm_g, k_norm_g, bias, dmix, riders=[_SendPartialsRider([p_ff1, p_out])])
    g_rel = _rel_bias_fold(dbias)
    dproj = jnp.concatenate([d_rq, d_rk, d_rv, d_rg, d_aq, d_ak, d_av], axis=-1)
    (g_in,) = _mm("dw_in", h1, dproj, TN, (d // tn, 2 * N_CHIPS, s // ts),
                  pl.BlockSpec((ts, tn), lambda i, j, k: (k, i)), pl.BlockSpec((ts, tn_in), lambda i, j, k: (k, j)),
                  [_sds((N_CHIPS, d, in_sh), BF16)],
                  [pl.BlockSpec((None, tn, tn_in), lambda i, j, k: (j // 2, i, j % 2))], (tn, tn_in))
    ((got_in,),) = _run_riders("grad_swap_w_in", [_SwapRider([g_in])])
    p_in = _add_half(g_in, got_in, c_arr, "chip_partial_w_in")
    (dh1,), ((got2_in,),) = _mm(
        "d_h1", dproj, wg_in, NT, (gi, d // tn, N_CHIPS),
        pl.BlockSpec((tm, in_sh), lambda i, j, k: (i, k)),
        pl.BlockSpec((None, tn, in_sh), lambda i, j, k: (k, j, 0)),
        [_sds((s, d), F32)], [tile], (tm, tn), riders=[_SendPartialsRider([p_in])])
    grad_x, _, g_norm1 = _rmsnorm_bwd(xs, norm1_g, dh1, dx1, "rmsnorm1_bwd")

    names = ["w_in", "w_out", "w_ff1", "w_ff2"]
    kc_arr = jnp.concatenate([k_arr, c_arr])
    halves = [_sum_partials(p, r, kc_arr, "sum_partials_" + nm)
              for p, r, nm in zip((p_in, p_out, p_ff1, p_ff2), (got2_in, got2_out, got2_ff1, got2_ff2), names)]
    (g_big,) = _run_riders("grad_share_halves", [_ShareRider(halves)])
    big = []
    for g, w, m, v, nm in zip(g_big, (w_in, w_out, w_ff1, w_ff2), (m_w_in, m_w_out, m_w_ff1, m_w_ff2),
                              (v_w_in, v_w_out, v_w_ff1, v_w_ff2), names):
        delta, new_m, new_v = _adamw(w[0], g, m[0], v[0], "adamw_" + nm)
        big.append((g[None], delta[None], new_m[None], new_v[None]))

    small_w = (norm1_g, ret_norm_g, q_norm_g, k_norm_g, rel_bias, norm2_g)
    small_m = (m_norm1_g, m_ret_norm_g, m_q_norm_g, m_k_norm_g, m_rel_bias, m_norm2_g)
    small_v = (v_norm1_g, v_ret_norm_g, v_q_norm_g, v_k_norm_g, v_rel_bias, v_norm2_g)
    shapes = [p.shape for p in small_w]
    g_small = _pack_small([g_norm1, g_gn, g_gq, g_gk, g_rel, g_norm2])
    sg, sd, sm, sv = (_unpack_small(a, shapes) for a in _small_allreduce_adamw(
        g_small, _pack_small(small_w), _pack_small(small_m), _pack_small(small_v)))

    def ordered(kind):
        sm_ = (sg, sd, sm, sv)[kind]
        return (sm_[0], big[0][kind], sm_[1], sm_[2], sm_[3], sm_[4], big[1][kind], sm_[5], big[2][kind], big[3][kind])

    return (loss, grad_x[None], *ordered(0), *ordered(1), *ordered(2), *ordered(3))
```

```python
import functools

import jax
import jax.numpy as jnp
from jax import lax
from jax.experimental import pallas as pl
from jax.experimental.pallas import tpu as pltpu

F32 = jnp.float32
BF16 = jnp.bfloat16
MXU_DTYPE = jnp.bfloat16

CHUNK = 64
HEADS = 8
HEAD_DIM = 128
LEFT_CHUNKS = 8
BAND = (LEFT_CHUNKS + 1) * CHUNK
REL_CLIP = 128
REL_SIZE = (CHUNK - 1) + REL_CLIP + 1
ROPE_BASE = 10000.0
EPS = 1e-6
GN_EPS = 1e-5
ADAM_LR, ADAM_B1, ADAM_B2, ADAM_EPS, ADAM_WD, ADAM_STEP = 0.001, 0.9, 0.999, 1e-08, 0.01, 10
N_CHIPS = 4
VMEM_LIMIT = 56 * 1024 * 1024
MESH = pl.DeviceIdType.MESH
ANY = pl.BlockSpec(memory_space=pl.ANY)

NN = (((1,), (0,)), ((), ()))
NT = (((1,), (1,)), ((), ()))
TN = (((0,), (0,)), ((), ()))


def _pallas(body, **kw):
    return pl.pallas_call(body, **kw)


def _params(*sem):
    return pltpu.CompilerParams(dimension_semantics=sem, vmem_limit_bytes=VMEM_LIMIT)


def _dot(a, b, dims):
    return lax.dot_general(a.astype(MXU_DTYPE), b.astype(MXU_DTYPE), dims, preferred_element_type=F32)


RIDER_MID = 0.8


def _mm(name, a, b, dims, grid, a_spec, b_spec, outs, o_specs, acc_shape, extras=(), extra_specs=(), epi=None,
        riders=()):
    ni, nj, nk = grid
    n_ex, n_out = len(extras), len(outs)
    n_in = 2 + n_ex
    rs = _Riders(riders, n_in, n_out)
    n_rin, n_rout = len(rs.arrays), len(rs.out_shapes)
    steps = ni * nj * nk

    def body(*refs):
        a_ref, b_ref = refs[0], refs[1]
        ex_refs = refs[2:n_in]
        o_refs = refs[n_in + n_rin:n_in + n_rin + n_out]
        acc_ref = refs[n_in + n_rin + n_out + n_rout]
        k = pl.program_id(2)
        if riders:
            bound = rs.bind(refs[n_in:n_in + n_rin], refs[n_in + n_rin + n_out:n_in + n_rin + n_out + n_rout],
                            refs[n_in + n_rin + n_out + n_rout + 1:])
            step = (pl.program_id(0) * nj + pl.program_id(1)) * nk + k
            pl.when(step == 0)(lambda: rs.run("start", bound))
            pl.when(step == int(steps * RIDER_MID))(lambda: rs.run("mid", bound))

        def finish(acc):
            vals = epi(acc, *[r[...] for r in ex_refs]) if epi is not None else (acc,)
            for r, v in zip(o_refs, vals):
                r[...] = v.astype(r.dtype)

        if nk == 1:
            finish(_dot(a_ref[...], b_ref[...], dims))
        else:
            @pl.when(k == 0)
            def _():
                acc_ref[...] = jnp.zeros_like(acc_ref)

            acc_ref[...] += _dot(a_ref[...], b_ref[...], dims)
            pl.when(k == nk - 1)(lambda: finish(acc_ref[...]))

        if riders:
            pl.when(step == steps - 1)(lambda: rs.run("end", bound))

    res = _pallas(
        body, name=name, grid=grid, in_specs=[a_spec, b_spec, *extra_specs, *rs.in_specs],
        out_specs=[*o_specs, *rs.out_specs], out_shape=[*outs, *rs.out_shapes], input_output_aliases=rs.aliases,
        scratch_shapes=[pltpu.VMEM(acc_shape if nk > 1 else (8, 128), F32), *rs.scratch],
        compiler_params=_params(*(("arbitrary",) * 3 if riders else ("parallel", "parallel", "arbitrary"))),
    )(a, b, *extras, *rs.arrays)
    return (res[:n_out], rs.split(res[n_out:])) if riders else res


def _sds(shape, dtype):
    return jax.ShapeDtypeStruct(shape, dtype)


def _cast_bf16(w, k_arr, name):
    r, c = w.shape
    tr = min(r, 256)

    def body(k_ref, w_ref, o_ref):
        o_ref[...] = w_ref[...].astype(BF16)

    grid_spec = pltpu.PrefetchScalarGridSpec(
        num_scalar_prefetch=1, grid=(r // tr,), in_specs=[pl.BlockSpec((tr, c), lambda i, k_ref: (i, 0))],
        out_specs=pl.BlockSpec((None, tr, c), lambda i, k_ref: (k_ref[0], i, 0)))
    return _pallas(body, name=name, grid_spec=grid_spec, out_shape=_sds((N_CHIPS, r, c), BF16),
                   compiler_params=_params("parallel"))(k_arr, w)


def _rmsnorm_fwd(x, g, name):
    s, d = x.shape
    tr = 256

    def body(x_ref, g_ref, o_ref):
        xv = x_ref[...]
        y = xv * lax.rsqrt(jnp.mean(xv * xv, axis=-1, keepdims=True) + EPS)
        o_ref[...] = (y * g_ref[...]).astype(o_ref.dtype)

    return _pallas(body, name=name, grid=(s // tr,),
                   in_specs=[pl.BlockSpec((tr, d), lambda i: (i, 0)), pl.BlockSpec((1, d), lambda i: (0, 0))],
                   out_specs=pl.BlockSpec((tr, d), lambda i: (i, 0)), out_shape=_sds((s, d), BF16),
                   compiler_params=_params("parallel"))(x, g)


def _rmsnorm_bwd(x, g, dh, res, name, riders=()):
    s, d = x.shape
    tr = 256

    def body(x_ref, g_ref, dh_ref, res_ref, dx_ref, dxb_ref, dg_ref):
        i = pl.program_id(0)
        xv = x_ref[...]
        rstd = lax.rsqrt(jnp.mean(xv * xv, axis=-1, keepdims=True) + EPS)
        xh = xv * rstd
        dhv = dh_ref[...]

        @pl.when(i == 0)
        def _():
            dg_ref[...] = jnp.zeros_like(dg_ref)

        dg_ref[...] += jnp.sum(dhv * xh, axis=0, keepdims=True)
        dxh = dhv * g_ref[...]
        dx = res_ref[...] + rstd * (dxh - xh * jnp.mean(dxh * xh, axis=-1, keepdims=True))
        dx_ref[...] = dx
        dxb_ref[...] = dx.astype(BF16)

    row = pl.BlockSpec((tr, d), lambda i: (i, 0))
    vec = pl.BlockSpec((1, d), lambda i: (0, 0))
    rs = _Riders(riders, 4, 3)
    out = _pallas(_with_riders(body, 4, 3, 0, rs, (s // tr,)), name=name, grid=(s // tr,),
                  in_specs=[row, vec, row, row, *rs.in_specs], out_specs=[row, row, vec, *rs.out_specs],
                  out_shape=[_sds((s, d), F32), _sds((s, d), BF16), _sds((1, d), F32), *rs.out_shapes],
                  input_output_aliases=rs.aliases, scratch_shapes=rs.scratch,
                  compiler_params=_params("arbitrary"))(x, g, dh, res, *rs.arrays)
    return (out[:3], rs.split(out[3:])) if riders else out


def _adamw_math(w, g, m, v):
    m = ADAM_B1 * m + (1.0 - ADAM_B1) * g
    v = ADAM_B2 * v + (1.0 - ADAM_B2) * (g * g)
    m_hat = m / (1.0 - ADAM_B1 ** ADAM_STEP)
    v_hat = v / (1.0 - ADAM_B2 ** ADAM_STEP)
    delta = -ADAM_LR * (m_hat / (jnp.sqrt(v_hat) + ADAM_EPS) + ADAM_WD * w)
    return delta, m, v


def _adamw(w, g, m, v, name):
    r, c = w.shape
    tr = 128

    def body(w_ref, g_ref, m_ref, v_ref, d_ref, nm_ref, nv_ref):
        d_ref[...], nm_ref[...], nv_ref[...] = _adamw_math(w_ref[...], g_ref[...], m_ref[...], v_ref[...])

    blk = pl.BlockSpec((tr, c), lambda i: (i, 0))
    return _pallas(body, name=name, grid=(r // tr,), in_specs=[blk] * 4, out_specs=[blk] * 3,
                   out_shape=[_sds((r, c), F32)] * 3, compiler_params=_params("parallel"))(w, g, m, v)


def _tables(s):
    half = HEAD_DIM // 2
    pos = jnp.arange(s, dtype=F32)
    inv_freq = ROPE_BASE ** (-jnp.arange(half, dtype=F32) / half)
    ang = pos[:, None] * inv_freq[None, :]
    cos, sin = jnp.cos(ang), jnp.sin(ang)
    cos_f = jnp.concatenate([cos, cos], axis=-1)
    sin_f = jnp.concatenate([-sin, sin], axis=-1)
    log_g = jnp.log1p(-jnp.exp2(-(5.0 + jnp.arange(HEADS, dtype=F32))))
    p = jnp.arange(CHUNK, dtype=F32)
    decay = jnp.exp(log_g[:, None, None] * jnp.abs(p[:, None] - p[None, :]))
    k_dec = jnp.exp(log_g[None, :] * (CHUNK - 1.0 - p)[:, None])
    q_dec = jnp.exp(log_g[None, :] * (p + 1.0)[:, None])
    c_dec = jnp.exp(log_g * CHUNK)
    k_dec = jnp.broadcast_to(k_dec.T[:, :, None], (HEADS, CHUNK, HEAD_DIM))
    q_dec = jnp.broadcast_to(q_dec.T[:, :, None], (HEADS, CHUNK, HEAD_DIM))
    c_dec = jnp.broadcast_to(c_dec[:, None, None], (HEADS, 1, HEAD_DIM))
    return cos_f, sin_f, decay, k_dec, q_dec, c_dec


def _rot(x, cos_f, sin_f):
    return x * cos_f + pltpu.roll(x, HEAD_DIM // 2, 1) * sin_f


def _rot_bwd(d, cos_f, sin_f):
    return d * cos_f + pltpu.roll(d * sin_f, HEAD_DIM // 2, 1)


RET_BLOCK_CHUNKS = 8
RET_ROWS = RET_BLOCK_CHUNKS * CHUNK
K_SCALE = HEAD_DIM ** -0.5


def _retention_fwd(proj, gn_g, tables, riders=()):
    s = proj.shape[0]
    nb = s // RET_ROWS
    nc = s // CHUNK
    cos_f, sin_f, decay, k_dec, q_dec, c_dec = tables

    def body(q_ref, k_ref, v_ref, g_ref, cos_ref, sin_ref, dec_ref, kd_ref, qd_ref, cd_ref, gn_ref,
             ret_ref, y_ref, prev_ref, state_ref):
        @pl.when(pl.program_id(1) == 0)
        def _():
            state_ref[...] = jnp.zeros_like(state_ref)

        cosv, sinv = cos_ref[...], sin_ref[...]
        q = _rot(q_ref[...], cosv, sinv)
        k = _rot(k_ref[...], cosv, sinv) * K_SCALE
        v = v_ref[...]
        rg = g_ref[...]
        dec, kd, qd, cd, gn = dec_ref[...], kd_ref[...], qd_ref[...], cd_ref[...], gn_ref[...]
        state = state_ref[...]
        for c in range(RET_BLOCK_CHUNKS):
            rows = slice(c * CHUNK, (c + 1) * CHUNK)
            qc, kc, vc = q[rows], k[rows], v[rows]
            sc = _dot(qc, kc, NT) * dec
            intra = _dot(sc, vc, NN)
            prev_ref[c] = state.astype(prev_ref.dtype)
            cross = _dot(qc * qd, state, NN)
            contrib = _dot(kc * kd, vc, TN)
            state = cd * state + contrib
            y = intra + cross
            y_ref[rows, :] = y
            mu = jnp.mean(y, axis=-1, keepdims=True)
            yc = y - mu
            var = jnp.mean(yc * yc, axis=-1, keepdims=True)
            yn = yc * lax.rsqrt(var + GN_EPS) * gn
            rgc = rg[rows]
            ret_ref[rows, :] = (rgc * jax.nn.sigmoid(rgc) * yn).astype(ret_ref.dtype)
        state_ref[...] = state

    def col(off):
        return pl.BlockSpec((RET_ROWS, HEAD_DIM), lambda h, i: (i, off + h))

    pos = pl.BlockSpec((RET_ROWS, HEAD_DIM), lambda h, i: (i, 0))
    per_head = lambda shape: pl.BlockSpec((None, *shape), lambda h, i: (h, 0, 0))
    rs = _Riders(riders, 11, 3)
    res = _pallas(
        _with_riders(body, 11, 3, 1, rs, (HEADS, nb)), name="retention_fwd", grid=(HEADS, nb),
        in_specs=[col(0), col(HEADS), col(2 * HEADS), col(3 * HEADS), pos, pos,
                  per_head((CHUNK, CHUNK)), per_head((CHUNK, HEAD_DIM)), per_head((CHUNK, HEAD_DIM)),
                  per_head((1, HEAD_DIM)), pl.BlockSpec((1, HEAD_DIM), lambda h, i: (0, h)), *rs.in_specs],
        out_specs=[col(0), col(0),
                   pl.BlockSpec((None, RET_BLOCK_CHUNKS, HEAD_DIM, HEAD_DIM), lambda h, i: (h, i, 0, 0)),
                   *rs.out_specs],
        out_shape=[_sds((s, HEADS * HEAD_DIM), BF16), _sds((s, HEADS * HEAD_DIM), F32),
                   _sds((HEADS, nc, HEAD_DIM, HEAD_DIM), MXU_DTYPE), *rs.out_shapes],
        input_output_aliases=rs.aliases,
        scratch_shapes=[pltpu.VMEM((HEAD_DIM, HEAD_DIM), F32), *rs.scratch],
        compiler_params=_params("arbitrary", "arbitrary"),
    )(proj, proj, proj, proj, cos_f, sin_f, decay, k_dec, q_dec, c_dec, gn_g, *rs.arrays)
    return res[:3], rs.split(res[3:])


def _retention_bwd(proj, gn_g, tables, y, prev, dmix, riders=()):
    s = proj.shape[0]
    nb = s // RET_ROWS
    cos_f, sin_f, decay, k_dec, q_dec, c_dec = tables

    def body(q_ref, k_ref, v_ref, g_ref, cos_ref, sin_ref, dec_ref, kd_ref, qd_ref, cd_ref, gn_ref,
             y_ref, prev_ref, dret_ref, dq_ref, dk_ref, dv_ref, dg_ref, dgn_ref, gstate_ref):
        @pl.when(pl.program_id(1) == 0)
        def _():
            gstate_ref[...] = jnp.zeros_like(gstate_ref)
            dgn_ref[...] = jnp.zeros_like(dgn_ref)

        cosv, sinv = cos_ref[...], sin_ref[...]
        q = _rot(q_ref[...], cosv, sinv)
        k = _rot(k_ref[...], cosv, sinv) * K_SCALE
        v = v_ref[...]
        dec, kd, qd, cd, gn = dec_ref[...], kd_ref[...], qd_ref[...], cd_ref[...], gn_ref[...]
        rg = g_ref[...]
        yv = y_ref[...]
        dret = dret_ref[...]
        sig = jax.nn.sigmoid(rg)
        gate = rg * sig
        mu = jnp.mean(yv, axis=-1, keepdims=True)
        yc = yv - mu
        rstd = lax.rsqrt(jnp.mean(yc * yc, axis=-1, keepdims=True) + GN_EPS)
        z = yc * rstd
        dyn = dret * gate
        dg_ref[...] = (dret * (z * gn) * (sig * (1.0 + rg * (1.0 - sig)))).astype(dg_ref.dtype)
        dgn_ref[...] += jnp.sum(dyn * z, axis=0, keepdims=True)
        dz = dyn * gn
        dy = rstd * (dz - jnp.mean(dz, axis=-1, keepdims=True) - z * jnp.mean(dz * z, axis=-1, keepdims=True))
        gst = gstate_ref[...]
        for c in reversed(range(RET_BLOCK_CHUNKS)):
            rows = slice(c * CHUNK, (c + 1) * CHUNK)
            qc, kc, vc, dyc = q[rows], k[rows], v[rows], dy[rows]
            sc = _dot(qc, kc, NT) * dec
            dp = _dot(dyc, vc, NT)
            dvc = _dot(sc, dyc, TN)
            ds = dp * dec
            dqc = _dot(ds, kc, NN)
            dkc = _dot(ds, qc, TN)
            prevc = prev_ref[c]
            dqc += _dot(dyc, prevc, NT) * qd
            dprev = _dot(qc * qd, dyc, TN)
            dkc += _dot(vc, gst, NT) * kd
            dvc += _dot(kc * kd, gst, NN)
            gst = dprev + cd * gst
            dq_ref[rows, :] = _rot_bwd(dqc, cosv[rows], sinv[rows]).astype(dq_ref.dtype)
            dk_ref[rows, :] = _rot_bwd(dkc * K_SCALE, cosv[rows], sinv[rows]).astype(dk_ref.dtype)
            dv_ref[rows, :] = dvc.astype(dv_ref.dtype)
        gstate_ref[...] = gst

    rev = lambda i: nb - 1 - i

    def col(off):
        return pl.BlockSpec((RET_ROWS, HEAD_DIM), lambda h, i: (rev(i), off + h))

    pos = pl.BlockSpec((RET_ROWS, HEAD_DIM), lambda h, i: (rev(i), 0))
    per_head = lambda shape: pl.BlockSpec((None, *shape), lambda h, i: (h, 0, 0))
    outb = _sds((s, HEADS * HEAD_DIM), BF16)
    rs = _Riders(riders, 14, 5)
    res = _pallas(
        _with_riders(body, 14, 5, 1, rs, (HEADS, nb)), name="retention_bwd", grid=(HEADS, nb),
        in_specs=[col(0), col(HEADS), col(2 * HEADS), col(3 * HEADS), pos, pos,
                  per_head((CHUNK, CHUNK)), per_head((CHUNK, HEAD_DIM)), per_head((CHUNK, HEAD_DIM)),
                  per_head((1, HEAD_DIM)), pl.BlockSpec((1, HEAD_DIM), lambda h, i: (0, h)),
                  col(0), pl.BlockSpec((None, RET_BLOCK_CHUNKS, HEAD_DIM, HEAD_DIM), lambda h, i: (h, rev(i), 0, 0)),
                  col(0), *rs.in_specs],
        out_specs=[col(0), col(0), col(0), col(0), per_head((1, HEAD_DIM)), *rs.out_specs],
        out_shape=[outb, outb, outb, outb, _sds((HEADS, 1, HEAD_DIM), F32), *rs.out_shapes],
        input_output_aliases=rs.aliases,
        scratch_shapes=[pltpu.VMEM((HEAD_DIM, HEAD_DIM), F32), *rs.scratch],
        compiler_params=_params("arbitrary", "arbitrary"),
    )(proj, proj, proj, proj, cos_f, sin_f, decay, k_dec, q_dec, c_dec, gn_g, y, prev, dmix, *rs.arrays)
    return res[:5], rs.split(res[5:])


ATT_COL0 = 4 * HEADS
PAD_ROWS = LEFT_CHUNKS * CHUNK
NORM_ROWS = 512
GROUP_CHUNKS = 4
GROUP = GROUP_CHUNKS * CHUNK
WIN = (LEFT_CHUNKS + GROUP_CHUNKS) * CHUNK
MASKED = -1e30


def _qk_norm(x, g):
    return x * lax.rsqrt(jnp.mean(x * x, axis=-1, keepdims=True) + EPS) * g


def _band_probs(qb, kb, bias, g):
    sc = _dot(qb, kb, NT) * K_SCALE + bias
    win_chunk = lax.broadcasted_iota(jnp.int32, (GROUP, WIN), 1) // CHUNK
    sc = jnp.where(g * GROUP_CHUNKS - LEFT_CHUNKS + win_chunk >= 0, sc, MASKED)
    e = jnp.exp(sc - jnp.max(sc, axis=-1, keepdims=True))
    return e / jnp.sum(e, axis=-1, keepdims=True)


def _with_riders(core, n_in, n_out, n_scratch, rs, grid):
    n_rin, n_rout = len(rs.arrays), len(rs.out_shapes)
    if not rs.riders:
        return core
    steps = 1
    for n in grid:
        steps *= n

    def body(*refs):
        outs_at = n_in + n_rin
        scratch_at = outs_at + n_out + n_rout
        bound = rs.bind(refs[n_in:outs_at], refs[outs_at + n_out:scratch_at], refs[scratch_at + n_scratch:])
        step = 0
        for axis, n in enumerate(grid):
            step = step * n + pl.program_id(axis)
        pl.when(step == 0)(lambda: rs.run("start", bound))
        pl.when(step == int(steps * RIDER_MID))(lambda: rs.run("mid", bound))
        core(*refs[:n_in], *refs[outs_at:outs_at + n_out], *refs[scratch_at:scratch_at + n_scratch])
        pl.when(step == steps - 1)(lambda: rs.run("end", bound))

    return body


def _attention_fwd(proj, gq, gk, bias, riders=()):
    s = proj.shape[0]
    nc = s // CHUNK
    rs = _Riders(riders, 6, 1)

    def body(q_ref, k_ref, v_ref, gq_ref, gk_ref, bias_ref, o_ref, kp_ref, vp_ref):
        kp_ref[0:PAD_ROWS, :] = jnp.zeros((PAD_ROWS, HEAD_DIM), kp_ref.dtype)
        vp_ref[0:PAD_ROWS, :] = jnp.zeros((PAD_ROWS, HEAD_DIM), vp_ref.dtype)
        gqv, gkv = gq_ref[...], gk_ref[...]

        def fill(b, carry):
            r0 = pl.multiple_of(b * NORM_ROWS, NORM_ROWS)
            kp_ref[pl.ds(PAD_ROWS + r0, NORM_ROWS), :] = _qk_norm(k_ref[pl.ds(r0, NORM_ROWS), :], gkv).astype(kp_ref.dtype)
            vp_ref[pl.ds(PAD_ROWS + r0, NORM_ROWS), :] = v_ref[pl.ds(r0, NORM_ROWS), :].astype(vp_ref.dtype)
            return carry

        lax.fori_loop(0, s // NORM_ROWS, fill, 0)

        def group(g, carry):
            r0 = pl.multiple_of(g * GROUP, GROUP)
            qn = _qk_norm(q_ref[pl.ds(r0, GROUP), :], gqv)
            p = _band_probs(qn, kp_ref[pl.ds(r0, WIN), :], bias_ref[...], g)
            o_ref[pl.ds(r0, GROUP), :] = _dot(p, vp_ref[pl.ds(r0, WIN), :], NN).astype(o_ref.dtype)
            return carry

        lax.fori_loop(0, s // GROUP, group, 0)

    def col(off):
        return pl.BlockSpec((s, HEAD_DIM), lambda h: (0, off + h))

    vec = pl.BlockSpec((1, HEAD_DIM), lambda h: (0, 0))
    res = _pallas(
        _with_riders(body, 6, 1, 2, rs, (HEADS,)), name="attention_fwd", grid=(HEADS,),
        in_specs=[col(ATT_COL0), col(ATT_COL0 + HEADS), col(ATT_COL0 + 2 * HEADS), vec, vec,
                  pl.BlockSpec((None, GROUP, WIN), lambda h: (h, 0, 0)), *rs.in_specs],
        out_specs=[col(0), *rs.out_specs], out_shape=[_sds((s, HEADS * HEAD_DIM), BF16), *rs.out_shapes],
        input_output_aliases=rs.aliases,
        scratch_shapes=[pltpu.VMEM((s + PAD_ROWS, HEAD_DIM), MXU_DTYPE), pltpu.VMEM((s + PAD_ROWS, HEAD_DIM), MXU_DTYPE),
                        *rs.scratch],
        compiler_params=_params("arbitrary"),
    )(proj, proj, proj, gq, gk, bias, *rs.arrays)
    return res[0], rs.split(res[1:])


def _attention_bwd(proj, gq, gk, bias, dmix, riders=()):
    s = proj.shape[0]
    nc = s // CHUNK
    rs = _Riders(riders, 7, 6)

    def body(q_ref, k_ref, v_ref, gq_ref, gk_ref, bias_ref, do_ref,
             dq_ref, dk_ref, dv_ref, dgq_ref, dgk_ref, dbias_ref, kp_ref, vp_ref, dkp_ref, dvp_ref, dqn_ref):
        kp_ref[0:PAD_ROWS, :] = jnp.zeros((PAD_ROWS, HEAD_DIM), kp_ref.dtype)
        vp_ref[0:PAD_ROWS, :] = jnp.zeros((PAD_ROWS, HEAD_DIM), vp_ref.dtype)
        dkp_ref[...] = jnp.zeros_like(dkp_ref)
        dvp_ref[...] = jnp.zeros_like(dvp_ref)
        dbias_ref[...] = jnp.zeros_like(dbias_ref)
        gqv, gkv = gq_ref[...], gk_ref[...]

        def fill(b, carry):
            r0 = pl.multiple_of(b * NORM_ROWS, NORM_ROWS)
            kp_ref[pl.ds(PAD_ROWS + r0, NORM_ROWS), :] = _qk_norm(k_ref[pl.ds(r0, NORM_ROWS), :], gkv).astype(kp_ref.dtype)
            vp_ref[pl.ds(PAD_ROWS + r0, NORM_ROWS), :] = v_ref[pl.ds(r0, NORM_ROWS), :].astype(vp_ref.dtype)
            return carry

        lax.fori_loop(0, s // NORM_ROWS, fill, 0)

        def group(g, carry):
            r0 = pl.multiple_of(g * GROUP, GROUP)
            qn = _qk_norm(q_ref[pl.ds(r0, GROUP), :], gqv)
            kb = kp_ref[pl.ds(r0, WIN), :]
            vb = vp_ref[pl.ds(r0, WIN), :]
            p = _band_probs(qn, kb, bias_ref[...], g)
            do = do_ref[pl.ds(r0, GROUP), :]
            dvp_ref[pl.ds(r0, WIN), :] += _dot(p, do, TN)
            dp = _dot(do, vb, NT)
            ds = p * (dp - jnp.sum(dp * p, axis=-1, keepdims=True))
            dbias_ref[...] += ds
            dss = ds * K_SCALE
            dqn_ref[pl.ds(r0, GROUP), :] = _dot(dss, kb, NN)
            dkp_ref[pl.ds(r0, WIN), :] += _dot(dss, qn, TN)
            return carry

        lax.fori_loop(0, s // GROUP, group, 0)

        @pl.when(pl.program_id(0) == 0)
        def _():
            dgq_ref[...] = jnp.zeros_like(dgq_ref)
            dgk_ref[...] = jnp.zeros_like(dgk_ref)

        def norm_bwd(x, g, dn):
            rstd = lax.rsqrt(jnp.mean(x * x, axis=-1, keepdims=True) + EPS)
            xh = x * rstd
            dxh = dn * g
            return rstd * (dxh - xh * jnp.mean(dxh * xh, axis=-1, keepdims=True)), jnp.sum(dn * xh, axis=0, keepdims=True)

        def finish(b, carry):
            r0 = pl.multiple_of(b * NORM_ROWS, NORM_ROWS)
            rows = pl.ds(r0, NORM_ROWS)
            dq, dgq = norm_bwd(q_ref[rows, :], gqv, dqn_ref[rows, :])
            dk, dgk = norm_bwd(k_ref[rows, :], gkv, dkp_ref[pl.ds(PAD_ROWS + r0, NORM_ROWS), :])
            dq_ref[rows, :] = dq.astype(dq_ref.dtype)
            dk_ref[rows, :] = dk.astype(dk_ref.dtype)
            dv_ref[rows, :] = dvp_ref[pl.ds(PAD_ROWS + r0, NORM_ROWS), :].astype(dv_ref.dtype)
            dgq_ref[...] += dgq
            dgk_ref[...] += dgk
            return carry

        lax.fori_loop(0, s // NORM_ROWS, finish, 0)

    def col(off):
        return pl.BlockSpec((s, HEAD_DIM), lambda h: (0, off + h))

    vec = pl.BlockSpec((1, HEAD_DIM), lambda h: (0, 0))
    hbias = pl.BlockSpec((None, GROUP, WIN), lambda h: (h, 0, 0))
    outb = _sds((s, HEADS * HEAD_DIM), BF16)
    res = _pallas(
        _with_riders(body, 7, 6, 5, rs, (HEADS,)), name="attention_bwd", grid=(HEADS,),
        in_specs=[col(ATT_COL0), col(ATT_COL0 + HEADS), col(ATT_COL0 + 2 * HEADS), vec, vec, hbias, col(HEADS),
                  *rs.in_specs],
        out_specs=[col(0), col(0), col(0), vec, vec, hbias, *rs.out_specs],
        out_shape=[outb, outb, outb, _sds((1, HEAD_DIM), F32), _sds((1, HEAD_DIM), F32),
                   _sds((HEADS, GROUP, WIN), F32), *rs.out_shapes],
        input_output_aliases=rs.aliases,
        scratch_shapes=[pltpu.VMEM((s + PAD_ROWS, HEAD_DIM), MXU_DTYPE), pltpu.VMEM((s + PAD_ROWS, HEAD_DIM), MXU_DTYPE),
                        pltpu.VMEM((s + PAD_ROWS, HEAD_DIM), F32), pltpu.VMEM((s + PAD_ROWS, HEAD_DIM), F32),
                        pltpu.VMEM((s, HEAD_DIM), F32), *rs.scratch],
        compiler_params=_params("arbitrary"),
    )(proj, proj, proj, gq, gk, bias, dmix, *rs.arrays)
    return res[:6], rs.split(res[6:])


DIAG_SPLIT = (BAND + WIN - CHUNK) // 2


def _diag_bin(m):
    t = jnp.where(m < DIAG_SPLIT, m, m - WIN)
    return jnp.clip(LEFT_CHUNKS * CHUNK - t, -(CHUNK - 1), REL_CLIP) + (CHUNK - 1)


def _skew_rows(a, left):
    row = lax.broadcasted_iota(jnp.int32, (GROUP, WIN), 0)
    for b in range(GROUP.bit_length() - 1):
        step = 1 << b
        a = jnp.where(jnp.bitwise_and(row, step) != 0, pltpu.roll(a, WIN - step if left else step, 1), a)
    return a


def _rel_bias_expand(rel_bias):
    def body(rb_ref, o_ref):
        h = pl.program_id(0)
        bins = _diag_bin(lax.broadcasted_iota(jnp.int32, (8, WIN), 1))
        per_diag = lax.fori_loop(0, REL_SIZE, lambda r, acc: jnp.where(bins == r, rb_ref[h, r], acc),
                                 jnp.zeros((8, WIN), F32))
        table = _skew_rows(jnp.broadcast_to(per_diag[0:1], (GROUP, WIN)), left=False)
        row_chunk = lax.broadcasted_iota(jnp.int32, (GROUP, WIN), 0) // CHUNK
        col_chunk = lax.broadcasted_iota(jnp.int32, (GROUP, WIN), 1) // CHUNK
        in_band = jnp.logical_and(col_chunk >= row_chunk, col_chunk <= row_chunk + LEFT_CHUNKS)
        o_ref[...] = jnp.where(in_band, table, MASKED)

    return _pallas(body, name="rel_bias_expand", grid=(HEADS,), in_specs=[pl.BlockSpec(memory_space=pltpu.SMEM)],
                   out_specs=pl.BlockSpec((None, GROUP, WIN), lambda h: (h, 0, 0)),
                   out_shape=_sds((HEADS, GROUP, WIN), F32), compiler_params=_params("parallel"))(rel_bias)


def _rel_bias_fold(dbias):
    def body(a_ref, o_ref):
        diag = jnp.sum(_skew_rows(a_ref[...], left=True), axis=0, keepdims=True)
        onehot = (_diag_bin(lax.broadcasted_iota(jnp.int32, (WIN, REL_SIZE), 0))
                  == lax.broadcasted_iota(jnp.int32, (WIN, REL_SIZE), 1)).astype(MXU_DTYPE)
        rest = jnp.broadcast_to(diag, (8, WIN))
        out = jnp.zeros((8, REL_SIZE), F32)
        for _ in range(3):
            piece = rest.astype(BF16)
            out = out + _dot(piece, onehot, NN)
            rest = rest - piece.astype(F32)
        o_ref[...] = out[0:1]

    return _pallas(body, name="rel_bias_fold", grid=(HEADS,),
                   in_specs=[pl.BlockSpec((None, GROUP, WIN), lambda h: (h, 0, 0))],
                   out_specs=pl.BlockSpec((None, 1, REL_SIZE), lambda h: (h, 0, 0)),
                   out_shape=_sds((HEADS, 1, REL_SIZE), F32), compiler_params=_params("parallel"))(dbias)


def _place():
    return lax.axis_index("x"), lax.axis_index("y"), lax.axis_index("c")


def _other_chips(x, y):
    return [(1 - x, y), (x, 1 - y), (1 - x, 1 - y)]


class _Rider:
    reads, ins, new, n_sems = (), (), (), 1

    def start(self, reads, ins, new, send, recv):
        pass

    def mid(self, reads, ins, new, send, recv):
        pass

    def end(self, reads, ins, new, send, recv):
        pass


class _Riders:
    def __init__(self, riders, n_host_in, n_host_out):
        self.riders = list(riders)
        self.arrays, self.out_shapes, self.aliases, self.scratch = [], [], {}, []
        for r in self.riders:
            for t, a in enumerate(r.ins):
                self.aliases[n_host_in + len(self.arrays) + len(r.reads) + t] = n_host_out + len(self.out_shapes) + t
            self.arrays += [*r.reads, *r.ins]
            self.out_shapes += [_sds(a.shape, a.dtype) for a in r.ins] + list(r.new)
            self.scratch += [pltpu.SemaphoreType.DMA((r.n_sems,)), pltpu.SemaphoreType.DMA((r.n_sems,))]
        self.in_specs = [ANY] * len(self.arrays)
        self.out_specs = [ANY] * len(self.out_shapes)

    def bind(self, in_refs, out_refs, scratch_refs):
        bound, i, o = [], 0, 0
        for t, r in enumerate(self.riders):
            reads = in_refs[i:i + len(r.reads)]
            i += len(r.reads) + len(r.ins)
            ins = out_refs[o:o + len(r.ins)]
            new = out_refs[o + len(r.ins):o + len(r.ins) + len(r.new)]
            o += len(r.ins) + len(r.new)
            bound.append((reads, ins, new, scratch_refs[2 * t], scratch_refs[2 * t + 1]))
        return bound

    def run(self, phase, bound):
        for r, b in zip(self.riders, bound):
            getattr(r, phase)(*b)

    def split(self, outs):
        res, o = [], 0
        for r in self.riders:
            n = len(r.ins) + len(r.new)
            res.append(list(outs[o:o + n]))
            o += n
        return res


def _run_riders(name, riders):
    rs = _Riders(riders, 0, 0)
    n_in, n_out = len(rs.arrays), len(rs.out_shapes)

    def body(*refs):
        bound = rs.bind(refs[:n_in], refs[n_in:n_in + n_out], refs[n_in + n_out:])
        rs.run("start", bound)
        rs.run("mid", bound)
        rs.run("end", bound)

    outs = _pallas(body, name=name, in_specs=rs.in_specs, out_specs=rs.out_specs, out_shape=rs.out_shapes,
                   input_output_aliases=rs.aliases, scratch_shapes=rs.scratch)(*rs.arrays)
    return rs.split(outs)


class _GatherRider(_Rider):
    def __init__(self, blocks, part=(0, 1, 1)):
        self.ins = tuple(blocks)
        self.part = part
        self.n_sems = 6 * len(blocks)

    def _copy(self, out, send, recv, w, j, chip_from, cc, to):
        hr = self.ins[w].shape[1] // 2
        lo, hi, n = self.part
        half = out[w].at[2 * chip_from[0] + chip_from[1], pl.ds(cc * hr + lo * (hr // n), (hi - lo) * (hr // n)), :]
        return pltpu.make_async_remote_copy(src_ref=half, dst_ref=half, send_sem=send.at[6 * w + j],
                                            recv_sem=recv.at[6 * w + j], device_id=to, device_id_type=MESH)

    def start(self, reads, out, new, send, recv):
        x, y, c = _place()
        for w in range(len(self.ins)):
            for j, chip in enumerate(_other_chips(x, y)):
                self._copy(out, send, recv, w, j, (x, y), c, (*chip, c)).start()

    def mid(self, reads, out, new, send, recv):
        x, y, c = _place()
        for w in range(len(self.ins)):
            for j, chip in enumerate(_other_chips(x, y)):
                self._copy(out, send, recv, w, j, chip, c, (x, y, c)).wait_recv()
                self._copy(out, send, recv, w, 3 + j, chip, c, (x, y, 1 - c)).start()

    def end(self, reads, out, new, send, recv):
        x, y, c = _place()
        for w in range(len(self.ins)):
            for j, chip in enumerate(_other_chips(x, y)):
                self._copy(out, send, recv, w, 3 + j, chip, 1 - c, (x, y, c)).wait_recv()
        for w in range(len(self.ins)):
            for j, chip in enumerate(_other_chips(x, y)):
                self._copy(out, send, recv, w, j, (x, y), c, (*chip, c)).wait_send()
                self._copy(out, send, recv, w, 3 + j, chip, c, (x, y, 1 - c)).wait_send()


class _SwapRider(_Rider):
    def __init__(self, grads):
        self.reads = tuple(grads)
        self.new = tuple(_sds((N_CHIPS, g.shape[1] // 2, g.shape[2]), g.dtype) for g in grads)
        self.n_sems = len(grads)

    def _copies(self, src, new, send, recv):
        x, y, c = _place()
        copies = []
        for w in range(len(self.reads)):
            hr = self.reads[w].shape[1] // 2
            copies.append(pltpu.make_async_remote_copy(
                src_ref=src[w].at[:, pl.ds((1 - c) * hr, hr), :], dst_ref=new[w],
                send_sem=send.at[w], recv_sem=recv.at[w], device_id=(x, y, 1 - c), device_id_type=MESH))
        return copies

    def start(self, src, ins, new, send, recv):
        for cp in self._copies(src, new, send, recv):
            cp.start()

    def end(self, src, ins, new, send, recv):
        for cp in self._copies(src, new, send, recv):
            cp.wait()


def _add_half(g, got, c_arr, name):
    nk, r, cols = g.shape
    hr = r // 2
    tr = min(hr, 256)
    nb = hr // tr

    def body(c_ref, g_ref, got_ref, o_ref):
        o_ref[...] = (g_ref[...].astype(F32) + got_ref[...].astype(F32)).astype(o_ref.dtype)

    grid_spec = pltpu.PrefetchScalarGridSpec(
        num_scalar_prefetch=1, grid=(nk, nb),
        in_specs=[pl.BlockSpec((None, tr, cols), lambda k, i, c_ref: (k, c_ref[0] * nb + i, 0)),
                  pl.BlockSpec((None, tr, cols), lambda k, i, c_ref: (k, i, 0))],
        out_specs=pl.BlockSpec((None, tr, cols), lambda k, i, c_ref: (k, i, 0)))
    return _pallas(body, name=name, grid_spec=grid_spec, out_shape=_sds((nk, hr, cols), g.dtype),
                   compiler_params=_params("parallel", "parallel"))(c_arr, g, got)


class _SendPartialsRider(_Rider):
    def __init__(self, parts, got=None, part=(0, 1, 1)):
        self.reads = tuple(parts)
        if got is None:
            self.new = tuple(_sds((N_CHIPS - 1, *p.shape[1:]), p.dtype) for p in parts)
        else:
            self.ins = tuple(got)
        self.part = part
        self.n_sems = 3 * len(parts)

    def _copies(self, src, ins, new, send, recv):
        x, y, c = _place()
        land = ins if self.ins else new
        lo, hi, n = self.part
        copies = []
        for w in range(len(self.reads)):
            pr = self.reads[w].shape[1] // n
            rows = pl.ds(lo * pr, (hi - lo) * pr)
            for j, chip in enumerate(_other_chips(x, y)):
                copies.append(pltpu.make_async_remote_copy(
                    src_ref=src[w].at[2 * chip[0] + chip[1], rows, :], dst_ref=land[w].at[j, rows, :],
                    send_sem=send.at[3 * w + j], recv_sem=recv.at[3 * w + j], device_id=(*chip, c), device_id_type=MESH))
        return copies

    def start(self, src, ins, new, send, recv):
        for cp in self._copies(src, ins, new, send, recv):
            cp.start()

    def end(self, src, ins, new, send, recv):
        for cp in self._copies(src, ins, new, send, recv):
            cp.wait()


def _sum_partials(part, got, kc_arr, name):
    _, hr, cols = part.shape
    tr = min(hr, 256)
    nb = hr // tr

    def body(kc_ref, p_ref, g0_ref, g1_ref, g2_ref, o_ref):
        o_ref[...] = ((p_ref[...].astype(F32) + g0_ref[...].astype(F32)) + g1_ref[...].astype(F32)) + g2_ref[...].astype(F32)

    slot = lambda j: pl.BlockSpec((None, tr, cols), lambda i, kc_ref: (j, i, 0))
    grid_spec = pltpu.PrefetchScalarGridSpec(
        num_scalar_prefetch=1, grid=(nb,),
        in_specs=[pl.BlockSpec((None, tr, cols), lambda i, kc_ref: (kc_ref[0], i, 0)), slot(0), slot(1), slot(2)],
        out_specs=pl.BlockSpec((tr, cols), lambda i, kc_ref: (kc_ref[1] * nb + i, 0)))
    return _pallas(body, name=name, grid_spec=grid_spec, out_shape=_sds((2 * hr, cols), F32),
                   compiler_params=_params("parallel"))(kc_arr, part, got, got, got)


class _ShareRider(_Rider):
    def __init__(self, grads):
        self.ins = tuple(grads)
        self.n_sems = len(grads)

    def _copies(self, out, send, recv):
        x, y, c = _place()
        copies = []
        for w in range(len(self.ins)):
            hr = self.ins[w].shape[0] // 2
            mine = out[w].at[pl.ds(c * hr, hr), :]
            copies.append(pltpu.make_async_remote_copy(
                src_ref=mine, dst_ref=mine, send_sem=send.at[w], recv_sem=recv.at[w],
                device_id=(x, y, 1 - c), device_id_type=MESH))
        return copies

    def start(self, reads, out, new, send, recv):
        for cp in self._copies(out, send, recv):
            cp.start()

    def end(self, reads, out, new, send, recv):
        for cp in self._copies(out, send, recv):
            cp.wait()


def _small_allreduce_adamw(g_part, w, m, v):
    rows = g_part.shape[0]

    def body(g_ref, w_ref, m_ref, v_ref, go_ref, d_ref, nm_ref, nv_ref, all_ref, send_sems, recv_sems):
        x, y, c = _place()
        me = 4 * x + 2 * y + c
        all_ref[me] = g_ref[...]
        copies = []
        for r in range(1, 8):
            dx, dy, dc = (r >> 2) & 1, (r >> 1) & 1, r & 1
            peer = (1 - x if dx else x, 1 - y if dy else y, 1 - c if dc else c)
            copies.append(pltpu.make_async_remote_copy(
                src_ref=g_ref, dst_ref=all_ref.at[me], send_sem=send_sems.at[r - 1], recv_sem=recv_sems.at[r - 1],
                device_id=peer, device_id_type=MESH))
        for cp in copies:
            cp.start()
        for cp in copies:
            cp.wait()
        tot = all_ref[0]
        for d in range(1, 8):
            tot = tot + all_ref[d]
        go_ref[...] = tot
        d_ref[...], nm_ref[...], nv_ref[...] = _adamw_math(w_ref[...], tot, m_ref[...], v_ref[...])

    vm = pl.BlockSpec(memory_space=pltpu.VMEM)
    return _pallas(
        body, name="small_allreduce_adamw", in_specs=[vm] * 4, out_specs=[vm] * 4,
        out_shape=[_sds((rows, 128), F32)] * 4,
        scratch_shapes=[pltpu.VMEM((8, rows, 128), F32), pltpu.SemaphoreType.DMA((7,)), pltpu.SemaphoreType.DMA((7,))],
    )(g_part, w, m, v)


SMALL_SIZES = (2048, 1024, 128, 128, HEADS * REL_SIZE, 2048)
SMALL_PART_ROWS = tuple(-(-size // 1024) * 8 for size in SMALL_SIZES)
SMALL_ROWS = sum(SMALL_PART_ROWS)


def _pack_small(parts):
    rows = []
    for p, size, nr in zip(parts, SMALL_SIZES, SMALL_PART_ROWS):
        rows.append(jnp.pad(p.reshape(-1), (0, nr * 128 - size)).reshape(nr, 128))
    return jnp.concatenate(rows, axis=0)


def _unpack_small(slab, shapes):
    out, off = [], 0
    for size, nr, shape in zip(SMALL_SIZES, SMALL_PART_ROWS, shapes):
        out.append(slab[off:off + nr].reshape(-1)[:size].reshape(shape))
        off += nr
    return out


def kernel(x, norm1_g, w_in, ret_norm_g, q_norm_g, k_norm_g, rel_bias, w_out, norm2_g, w_ff1, w_ff2, loss_target, m_norm1_g, m_w_in, m_ret_norm_g, m_q_norm_g, m_k_norm_g, m_rel_bias, m_w_out, m_norm2_g, m_w_ff1, m_w_ff2, v_norm1_g, v_w_in, v_ret_norm_g, v_q_norm_g, v_k_norm_g, v_rel_bias, v_w_out, v_norm2_g, v_w_ff1, v_w_ff2):
    xs = x[0]
    tgt = loss_target[0]
    s, d = xs.shape
    d_in = N_CHIPS * w_in.shape[2]
    d_ff = N_CHIPS * w_ff1.shape[2]
    in_sh, ff_sh = w_in.shape[2], w_ff1.shape[2]
    tm = min(s, 1024)
    gi = s // tm
    c_arr = lax.axis_index("c").astype(jnp.int32).reshape(1)
    k_arr = (2 * lax.axis_index("x") + lax.axis_index("y")).astype(jnp.int32).reshape(1)
    tables = _tables(s)
    bias = _rel_bias_expand(rel_bias[0])

    blk_in, blk_out, blk_ff1, blk_ff2 = (
        _cast_bf16(w_in[0], k_arr, "cast_w_in"), _cast_bf16(w_out[0], k_arr, "cast_w_out"),
        _cast_bf16(w_ff1[0], k_arr, "cast_w_ff1"), _cast_bf16(w_ff2[0], k_arr, "cast_w_ff2"))
    ((wg_in,),) = _run_riders("all_gather_w_in", [_GatherRider([blk_in])])

    h1 = _rmsnorm_fwd(xs, norm1_g, "rmsnorm1")
    tn_in = in_sh // 2
    tk = d
    (proj,), ((wg_ff1,),) = _mm(
        "proj", h1, wg_in, NN, (gi, 2 * N_CHIPS, d // tk),
        pl.BlockSpec((tm, tk), lambda i, j, k: (i, k)),
        pl.BlockSpec((None, tk, tn_in), lambda i, j, k: (j // 2, k, j % 2)),
        [_sds((s, d_in), F32)], [pl.BlockSpec((tm, tn_in), lambda i, j, k: (i, j))], (tm, tn_in),
        riders=[_GatherRider([blk_ff1], (0, 3, 4))])
    (ret, y_ret, prev), ((wg_ff1,),) = _retention_fwd(proj, ret_norm_g, tables, riders=[_GatherRider([wg_ff1], (3, 4, 4))])
    att, ((wg_out,), (wg_ff2,)) = _attention_fwd(
        proj, q_norm_g, k_norm_g, bias, riders=[_GatherRider([blk_out]), _GatherRider([blk_ff2], (0, 1, 4))])
    wg_out = wg_out.reshape(d, d)
    mix = jnp.concatenate([ret, att], axis=-1)
    tn = 1024
    tile = pl.BlockSpec((tm, tn), lambda i, j, k: (i, j))
    (x1,) = _mm("out_proj", mix, wg_out, NN, (gi, d // tn, d // tk),
                pl.BlockSpec((tm, tk), lambda i, j, k: (i, k)), pl.BlockSpec((tk, tn), lambda i, j, k: (k, j)),
                [_sds((s, d), F32)], [tile], (tm, tn), extras=(xs,), extra_specs=(tile,),
                epi=lambda acc, r: (r + acc,))
    h2 = _rmsnorm_fwd(x1, norm2_g, "rmsnorm2")
    tn_ff = min(ff_sh, 1024)
    per = ff_sh // tn_ff

    def relu2(acc):
        r = jnp.maximum(acc, 0.0)
        return acc, r * r

    (u, act), ((wg_ff2,),) = _mm(
        "ff1", h2, wg_ff1, NN, (gi, N_CHIPS * per, d // tk),
        pl.BlockSpec((tm, tk), lambda i, j, k: (i, k)),
        pl.BlockSpec((None, tk, tn_ff), lambda i, j, k: (j // per, k, j % per)),
        [_sds((s, d_ff), F32), _sds((s, d_ff), BF16)],
        [pl.BlockSpec((tm, tn_ff), lambda i, j, k: (i, j))] * 2, (tm, tn_ff), epi=relu2,
        riders=[_GatherRider([wg_ff2], (1, 4, 4))])
    wg_ff2 = wg_ff2.reshape(d_ff, d)

    def loss_epi(acc, res, t):
        diff = (res + acc) - t
        dy = diff / d
        return dy, dy, jnp.sum(diff * diff, axis=0, keepdims=True)

    tn2 = 512
    tile2 = pl.BlockSpec((tm, tn2), lambda i, j, k: (i, j))
    dy, dyb, loss_cols = _mm(
        "ff2_loss", act, wg_ff2, NN, (gi, d // tn2, d_ff // tk),
        pl.BlockSpec((tm, tk), lambda i, j, k: (i, k)), pl.BlockSpec((tk, tn2), lambda i, j, k: (k, j)),
        [_sds((s, d), F32), _sds((s, d), BF16), _sds((gi, 1, d), F32)],
        [tile2, tile2, pl.BlockSpec((None, 1, tn2), lambda i, j, k: (i, 0, j))], (tm, tn2),
        extras=(x1, tgt), extra_specs=(tile2, tile2), epi=loss_epi)
    loss = lax.psum(0.5 * jnp.sum(loss_cols) / d, ("x", "y", "c"))

    (du,) = _mm("d_act", dyb, wg_ff2, NT, (gi, d_ff // tn, d // tk),
                pl.BlockSpec((tm, tk), lambda i, j, k: (i, k)), pl.BlockSpec((tn, tk), lambda i, j, k: (j, k)),
                [_sds((s, d_ff), BF16)], [tile], (tm, tn), extras=(u,), extra_specs=(tile,),
                epi=lambda acc, uu: (acc * (2.0 * jnp.maximum(uu, 0.0)),))
    ts = min(s, 2048)
    wtile = pl.BlockSpec((tn, tn), lambda i, j, k: (i, j))
    (g_ff2,) = _mm("dw_ff2", act, dyb, TN, (d_ff // tn, d // tn, s // ts),
                   pl.BlockSpec((ts, tn), lambda i, j, k: (k, i)), pl.BlockSpec((ts, tn), lambda i, j, k: (k, j)),
                   [_sds((d_ff, d), BF16)], [wtile], (tn, tn))
    g_ff2 = g_ff2.reshape(N_CHIPS, d_ff // N_CHIPS, d)
    (g_ff1,), ((got_ff2,),) = _mm(
        "dw_ff1", h2, du, TN, (d // tn, N_CHIPS * per, s // ts),
        pl.BlockSpec((ts, tn), lambda i, j, k: (k, i)), pl.BlockSpec((ts, tn_ff), lambda i, j, k: (k, j)),
        [_sds((N_CHIPS, d, ff_sh), BF16)],
        [pl.BlockSpec((None, tn, tn_ff), lambda i, j, k: (j // per, i, j % per))], (tn, tn_ff),
        riders=[_SwapRider([g_ff2])])
    p_ff2 = _add_half(g_ff2, got_ff2, c_arr, "chip_partial_w_ff2")
    tkf = min(tk, ff_sh)
    kper = ff_sh // tkf
    (dh2,), ((got2_ff2,), (got_ff1,)) = _mm(
        "d_h2", du, wg_ff1, NT, (gi, d // tn, d_ff // tkf),
        pl.BlockSpec((tm, tkf), lambda i, j, k: (i, k)),
        pl.BlockSpec((None, tn, tkf), lambda i, j, k: (k // kper, j, k % kper)),
        [_sds((s, d), F32)], [tile], (tm, tn),
        riders=[_SendPartialsRider([p_ff2], part=(0, 3, 4)), _SwapRider([g_ff1])])
    p_ff1 = _add_half(g_ff1, got_ff1, c_arr, "chip_partial_w_ff1")
    dx1, dx1b, g_norm2 = _rmsnorm_bwd(x1, norm2_g, dh2, dy, "rmsnorm2_bwd")

    (dmix,) = _mm("d_mix", dx1b, wg_out, NT, (gi, d // tn, d // tk),
                  pl.BlockSpec((tm, tk), lambda i, j, k: (i, k)), pl.BlockSpec((tn, tk), lambda i, j, k: (j, k)),
                  [_sds((s, d), F32)], [tile], (tm, tn))
    (g_out,) = _mm("dw_out", mix, dx1b, TN, (d // tn, d // tn, s // ts),
                   pl.BlockSpec((ts, tn), lambda i, j, k: (k, i)), pl.BlockSpec((ts, tn), lambda i, j, k: (k, j)),
                   [_sds((d, d), BF16)], [wtile], (tn, tn))
    g_out = g_out.reshape(N_CHIPS, d // N_CHIPS, d)
    ((got_out,),) = _run_riders("grad_swap_w_out", [_SwapRider([g_out])])
    p_out = _add_half(g_out, got_out, c_arr, "chip_partial_w_out")
    (d_rq, d_rk, d_rv, d_rg, g_gn), ((got2_ff2,), (got2_ff1,)) = _retention_bwd(
        proj, ret_norm_g, tables, y_ret, prev, dmix,
        riders=[_SendPartialsRider([p_ff2], got=[got2_ff2], part=(3, 4, 4)), _SendPartialsRider([p_ff1], part=(0, 2, 4))])
    (d_aq, d_ak, d_av, g_gq, g_gk, dbias), ((got2_ff1,), (got2_out,)) = _attention_bwd(
        proj, q_norm_g, k_norm_g, bias, dmix,
        riders=[_SendPartialsRider([p_ff1], got=[got2_ff1], part=(2, 4, 4)), _SendPartialsRider([p_out])])
    g_rel = _rel_bias_fold(dbias)
    dproj = jnp.concatenate([d_rq, d_rk, d_rv, d_rg, d_aq, d_ak, d_av], axis=-1)
    (g_in,) = _mm("dw_in", h1, dproj, TN, (d // tn, 2 * N_CHIPS, s // ts),
                  pl.BlockSpec((ts, tn), lambda i, j, k: (k, i)), pl.BlockSpec((ts, tn_in), lambda i, j, k: (k, j)),
                  [_sds((N_CHIPS, d, in_sh), BF16)],
                  [pl.BlockSpec((None, tn, tn_in), lambda i, j, k: (j // 2, i, j % 2))], (tn, tn_in))
    ((got_in,),) = _run_riders("grad_swap_w_in", [_SwapRider([g_in])])
    p_in = _add_half(g_in, got_in, c_arr, "chip_partial_w_in")
    (dh1,), ((got2_in,),) = _mm(
        "d_h1", dproj, wg_in, NT, (gi, d // tn, N_CHIPS),
        pl.BlockSpec((tm, in_sh), lambda i, j, k: (i, k)),
        pl.BlockSpec((None, tn, in_sh), lambda i, j, k: (k, j, 0)),
        [_sds((s, d), F32)], [tile], (tm, tn), riders=[_SendPartialsRider([p_in], part=(0, 3, 4))])
    (grad_x, _, g_norm1), ((got2_in,),) = _rmsnorm_bwd(
        xs, norm1_g, dh1, dx1, "rmsnorm1_bwd", riders=[_SendPartialsRider([p_in], got=[got2_in], part=(3, 4, 4))])

    names = ["w_in", "w_out", "w_ff1", "w_ff2"]
    kc_arr = jnp.concatenate([k_arr, c_arr])
    halves = [_sum_partials(p, r, kc_arr, "sum_partials_" + nm)
              for p, r, nm in zip((p_in, p_out, p_ff1, p_ff2), (got2_in, got2_out, got2_ff1, got2_ff2), names)]
    (g_big,) = _run_riders("grad_share_halves", [_ShareRider(halves)])
    big = []
    for g, w, m, v, nm in zip(g_big, (w_in, w_out, w_ff1, w_ff2), (m_w_in, m_w_out, m_w_ff1, m_w_ff2),
                              (v_w_in, v_w_out, v_w_ff1, v_w_ff2), names):
        delta, new_m, new_v = _adamw(w[0], g, m[0], v[0], "adamw_" + nm)
        big.append((g[None], delta[None], new_m[None], new_v[None]))

    small_w = (norm1_g, ret_norm_g, q_norm_g, k_norm_g, rel_bias, norm2_g)
    small_m = (m_norm1_g, m_ret_norm_g, m_q_norm_g, m_k_norm_g, m_rel_bias, m_norm2_g)
    small_v = (v_norm1_g, v_ret_norm_g, v_q_norm_g, v_k_norm_g, v_rel_bias, v_norm2_g)
    shapes = [p.shape for p in small_w]
    g_small = _pack_small([g_norm1, g_gn, g_gq, g_gk, g_rel, g_norm2])
    sg, sd, sm, sv = (_unpack_small(a, shapes) for a in _small_allreduce_adamw(
        g_small, _pack_small(small_w), _pack_small(small_m), _pack_small(small_v)))

    def ordered(kind):
        sm_ = (sg, sd, sm, sv)[kind]
        return (sm_[0], big[0][kind], sm_[1], sm_[2], sm_[3], sm_[4], big[1][kind], sm_[5], big[2][kind], big[3][kind])

    return (loss, grad_x[None], *ordered(0), *ordered(1), *ordered(2), *ordered(3))
```

```python
import functools

import jax
import jax.numpy as jnp
from jax import lax
from jax.experimental import pallas as pl
from jax.experimental.pallas import tpu as pltpu

F32 = jnp.float32
BF16 = jnp.bfloat16
MXU_DTYPE = jnp.bfloat16

CHUNK = 64
HEADS = 8
HEAD_DIM = 128
LEFT_CHUNKS = 8
BAND = (LEFT_CHUNKS + 1) * CHUNK
REL_CLIP = 128
REL_SIZE = (CHUNK - 1) + REL_CLIP + 1
RET_BLOCK_CHUNKS = 8
RET_ROWS = RET_BLOCK_CHUNKS * CHUNK
RET_SUB = 256
ROPE_BASE = 10000.0
EPS = 1e-6
GN_EPS = 1e-5
ADAM_LR, ADAM_B1, ADAM_B2, ADAM_EPS, ADAM_WD, ADAM_STEP = 0.001, 0.9, 0.999, 1e-08, 0.01, 10
N_CHIPS = 4
VMEM_LIMIT = 56 * 1024 * 1024
MESH = pl.DeviceIdType.MESH
ANY = pl.BlockSpec(memory_space=pl.ANY)

NN = (((1,), (0,)), ((), ()))
NT = (((1,), (1,)), ((), ()))
TN = (((0,), (0,)), ((), ()))


def _pallas(body, **kw):
    return pl.pallas_call(body, **kw)


def _params(*sem):
    return pltpu.CompilerParams(dimension_semantics=sem, vmem_limit_bytes=VMEM_LIMIT)


def _dot(a, b, dims):
    return lax.dot_general(a.astype(MXU_DTYPE), b.astype(MXU_DTYPE), dims, preferred_element_type=F32)


RIDER_MID = 0.8


def _mm(name, a, b, dims, grid, a_spec, b_spec, outs, o_specs, acc_shape, extras=(), extra_specs=(), epi=None,
        riders=()):
    ni, nj, nk = grid
    n_ex, n_out = len(extras), len(outs)
    n_in = 2 + n_ex
    rs = _Riders(riders, n_in, n_out)
    n_rin, n_rout = len(rs.arrays), len(rs.out_shapes)
    steps = ni * nj * nk

    def body(*refs):
        a_ref, b_ref = refs[0], refs[1]
        ex_refs = refs[2:n_in]
        o_refs = refs[n_in + n_rin:n_in + n_rin + n_out]
        acc_ref = refs[n_in + n_rin + n_out + n_rout]
        k = pl.program_id(2)
        if riders:
            bound = rs.bind(refs[n_in:n_in + n_rin], refs[n_in + n_rin + n_out:n_in + n_rin + n_out + n_rout],
                            refs[n_in + n_rin + n_out + n_rout + 1:])
            step = (pl.program_id(0) * nj + pl.program_id(1)) * nk + k
            pl.when(step == 0)(lambda: rs.run("start", bound))
            pl.when(step == int(steps * RIDER_MID))(lambda: rs.run("mid", bound))

        def finish(acc):
            vals = epi(acc, *[r[...] for r in ex_refs]) if epi is not None else (acc,)
            for r, v in zip(o_refs, vals):
                r[...] = v.astype(r.dtype)

        if nk == 1:
            finish(_dot(a_ref[...], b_ref[...], dims))
        else:
            @pl.when(k == 0)
            def _():
                acc_ref[...] = jnp.zeros_like(acc_ref)

            acc_ref[...] += _dot(a_ref[...], b_ref[...], dims)
            pl.when(k == nk - 1)(lambda: finish(acc_ref[...]))

        if riders:
            pl.when(step == steps - 1)(lambda: rs.run("end", bound))

    res = _pallas(
        body, name=name, grid=grid, in_specs=[a_spec, b_spec, *extra_specs, *rs.in_specs],
        out_specs=[*o_specs, *rs.out_specs], out_shape=[*outs, *rs.out_shapes], input_output_aliases=rs.aliases,
        scratch_shapes=[pltpu.VMEM(acc_shape if nk > 1 else (8, 128), F32), *rs.scratch],
        compiler_params=_params(*(("arbitrary",) * 3 if riders else ("parallel", "parallel", "arbitrary"))),
    )(a, b, *extras, *rs.arrays)
    return (res[:n_out], rs.split(res[n_out:])) if riders else res


def _sds(shape, dtype):
    return jax.ShapeDtypeStruct(shape, dtype)


def _cast_bf16(w, k_arr, name):
    r, c = w.shape
    tr = min(r, 256)

    def body(k_ref, w_ref, o_ref):
        o_ref[...] = w_ref[...].astype(BF16)

    grid_spec = pltpu.PrefetchScalarGridSpec(
        num_scalar_prefetch=1, grid=(r // tr,), in_specs=[pl.BlockSpec((tr, c), lambda i, k_ref: (i, 0))],
        out_specs=pl.BlockSpec((None, tr, c), lambda i, k_ref: (k_ref[0], i, 0)))
    return _pallas(body, name=name, grid_spec=grid_spec, out_shape=_sds((N_CHIPS, r, c), BF16),
                   compiler_params=_params("parallel"))(k_arr, w)


def _rmsnorm_fwd(x, g, name):
    s, d = x.shape
    tr = 256

    def body(x_ref, g_ref, o_ref):
        xv = x_ref[...]
        y = xv * lax.rsqrt(jnp.mean(xv * xv, axis=-1, keepdims=True) + EPS)
        o_ref[...] = (y * g_ref[...]).astype(o_ref.dtype)

    return _pallas(body, name=name, grid=(s // tr,),
                   in_specs=[pl.BlockSpec((tr, d), lambda i: (i, 0)), pl.BlockSpec((1, d), lambda i: (0, 0))],
                   out_specs=pl.BlockSpec((tr, d), lambda i: (i, 0)), out_shape=_sds((s, d), BF16),
                   compiler_params=_params("parallel"))(x, g)


def _rmsnorm_bwd(x, g, dh, res, name, riders=()):
    s, d = x.shape
    tr = 256

    def body(x_ref, g_ref, dh_ref, res_ref, dx_ref, dxb_ref, dg_ref):
        i = pl.program_id(0)
        xv = x_ref[...]
        rstd = lax.rsqrt(jnp.mean(xv * xv, axis=-1, keepdims=True) + EPS)
        xh = xv * rstd
        dhv = dh_ref[...]

        @pl.when(i == 0)
        def _():
            dg_ref[...] = jnp.zeros_like(dg_ref)

        dg_ref[...] += jnp.sum(dhv * xh, axis=0, keepdims=True)
        dxh = dhv * g_ref[...]
        dx = res_ref[...] + rstd * (dxh - xh * jnp.mean(dxh * xh, axis=-1, keepdims=True))
        dx_ref[...] = dx
        dxb_ref[...] = dx.astype(BF16)

    row = pl.BlockSpec((tr, d), lambda i: (i, 0))
    vec = pl.BlockSpec((1, d), lambda i: (0, 0))
    rs = _Riders(riders, 4, 3)
    out = _pallas(_with_riders(body, 4, 3, 0, rs, (s // tr,)), name=name, grid=(s // tr,),
                  in_specs=[row, vec, row, row, *rs.in_specs], out_specs=[row, row, vec, *rs.out_specs],
                  out_shape=[_sds((s, d), F32), _sds((s, d), BF16), _sds((1, d), F32), *rs.out_shapes],
                  input_output_aliases=rs.aliases, scratch_shapes=rs.scratch,
                  compiler_params=_params("arbitrary"))(x, g, dh, res, *rs.arrays)
    return (out[:3], rs.split(out[3:])) if riders else out


def _adamw_math(w, g, m, v):
    m = ADAM_B1 * m + (1.0 - ADAM_B1) * g
    v = ADAM_B2 * v + (1.0 - ADAM_B2) * (g * g)
    m_hat = m / (1.0 - ADAM_B1 ** ADAM_STEP)
    v_hat = v / (1.0 - ADAM_B2 ** ADAM_STEP)
    delta = -ADAM_LR * (m_hat / (jnp.sqrt(v_hat) + ADAM_EPS) + ADAM_WD * w)
    return delta, m, v


def _adamw(w, g, m, v, name):
    r, c = w.shape
    tr = 128

    def body(w_ref, g_ref, m_ref, v_ref, d_ref, nm_ref, nv_ref):
        d_ref[...], nm_ref[...], nv_ref[...] = _adamw_math(w_ref[...], g_ref[...], m_ref[...], v_ref[...])

    blk = pl.BlockSpec((tr, c), lambda i: (i, 0))
    return _pallas(body, name=name, grid=(r // tr,), in_specs=[blk] * 4, out_specs=[blk] * 3,
                   out_shape=[_sds((r, c), F32)] * 3, compiler_params=_params("parallel"))(w, g, m, v)


def _tables(s):
    half = HEAD_DIM // 2
    pos = jnp.arange(s, dtype=F32)
    inv_freq = ROPE_BASE ** (-jnp.arange(half, dtype=F32) / half)
    ang = pos[:, None] * inv_freq[None, :]
    cos, sin = jnp.cos(ang), jnp.sin(ang)
    cos_f = jnp.concatenate([cos, cos], axis=-1)
    sin_f = jnp.concatenate([-sin, sin], axis=-1)
    log_g = jnp.log1p(-jnp.exp2(-(5.0 + jnp.arange(HEADS, dtype=F32))))
    p = jnp.arange(CHUNK, dtype=F32)
    decay = jnp.exp(log_g[:, None, None] * jnp.abs(p[:, None] - p[None, :]))
    k_dec = jnp.exp(log_g[None, :] * (CHUNK - 1.0 - p)[:, None])
    q_dec = jnp.exp(log_g[None, :] * (p + 1.0)[:, None])
    c_dec = jnp.exp(log_g * CHUNK)
    k_dec = jnp.tile(jnp.broadcast_to(k_dec.T[:, :, None], (HEADS, CHUNK, HEAD_DIM)), (1, RET_BLOCK_CHUNKS, 1))
    q_dec = jnp.tile(jnp.broadcast_to(q_dec.T[:, :, None], (HEADS, CHUNK, HEAD_DIM)), (1, RET_BLOCK_CHUNKS, 1))
    c_dec = jnp.broadcast_to(c_dec[:, None, None], (HEADS, 1, HEAD_DIM))
    n = RET_SUB // CHUNK
    decay = (jnp.eye(n, dtype=F32)[None, :, None, :, None] * decay[:, None, :, None, :]).reshape(HEADS, RET_SUB, RET_SUB)
    return cos_f, sin_f, decay, k_dec, q_dec, c_dec


def _rot(x, cos_f, sin_f):
    return x * cos_f + pltpu.roll(x, HEAD_DIM // 2, 1) * sin_f


def _rot_bwd(d, cos_f, sin_f):
    return d * cos_f + pltpu.roll(d * sin_f, HEAD_DIM // 2, 1)


K_SCALE = HEAD_DIM ** -0.5


def _retention_fwd(proj, gn_g, tables, riders=()):
    s = proj.shape[0]
    nb = s // RET_ROWS
    nc = s // CHUNK
    cos_f, sin_f, decay, k_dec, q_dec, c_dec = tables

    def body(q_ref, k_ref, v_ref, g_ref, cos_ref, sin_ref, dec_ref, kd_ref, qd_ref, cd_ref, gn_ref,
             ret_ref, y_ref, prev_ref, state_ref):
        @pl.when(pl.program_id(1) == 0)
        def _():
            state_ref[...] = jnp.zeros_like(state_ref)

        cosv, sinv = cos_ref[...], sin_ref[...]
        q = _rot(q_ref[...], cosv, sinv)
        k = _rot(k_ref[...], cosv, sinv) * K_SCALE
        v = v_ref[...]
        rg = g_ref[...]
        dec, cd, gn = dec_ref[...], cd_ref[...], gn_ref[...]
        kdf, qdf = k * kd_ref[...], q * qd_ref[...]
        chunks = [slice(c * CHUNK, (c + 1) * CHUNK) for c in range(RET_BLOCK_CHUNKS)]
        contribs = [_dot(kdf[rows], v[rows], TN) for rows in chunks]
        state, states = state_ref[...], []
        for c in range(RET_BLOCK_CHUNKS):
            states.append(state)
            prev_ref[c] = state.astype(prev_ref.dtype)
            state = cd * state + contribs[c]
        state_ref[...] = state
        cross = jnp.concatenate([_dot(qdf[rows], st, NN) for rows, st in zip(chunks, states)], axis=0)
        intra = []
        for b in range(RET_ROWS // RET_SUB):
            rows = slice(b * RET_SUB, (b + 1) * RET_SUB)
            intra.append(_dot(_dot(q[rows], k[rows], NT) * dec, v[rows], NN))
        y = jnp.concatenate(intra, axis=0) + cross
        y_ref[...] = y
        mu = jnp.mean(y, axis=-1, keepdims=True)
        yc = y - mu
        var = jnp.mean(yc * yc, axis=-1, keepdims=True)
        yn = yc * lax.rsqrt(var + GN_EPS) * gn
        ret_ref[...] = (rg * jax.nn.sigmoid(rg) * yn).astype(ret_ref.dtype)

    def col(off):
        return pl.BlockSpec((RET_ROWS, HEAD_DIM), lambda h, i: (i, off + h))

    pos = pl.BlockSpec((RET_ROWS, HEAD_DIM), lambda h, i: (i, 0))
    per_head = lambda shape: pl.BlockSpec((None, *shape), lambda h, i: (h, 0, 0))
    rs = _Riders(riders, 11, 3)
    res = _pallas(
        _with_riders(body, 11, 3, 1, rs, (HEADS, nb)), name="retention_fwd", grid=(HEADS, nb),
        in_specs=[col(0), col(HEADS), col(2 * HEADS), col(3 * HEADS), pos, pos,
                  per_head((RET_SUB, RET_SUB)), per_head((RET_ROWS, HEAD_DIM)), per_head((RET_ROWS, HEAD_DIM)),
                  per_head((1, HEAD_DIM)), pl.BlockSpec((1, HEAD_DIM), lambda h, i: (0, h)), *rs.in_specs],
        out_specs=[col(0), col(0),
                   pl.BlockSpec((None, RET_BLOCK_CHUNKS, HEAD_DIM, HEAD_DIM), lambda h, i: (h, i, 0, 0)),
                   *rs.out_specs],
        out_shape=[_sds((s, 2 * HEADS * HEAD_DIM), BF16), _sds((s, HEADS * HEAD_DIM), F32),
                   _sds((HEADS, nc, HEAD_DIM, HEAD_DIM), MXU_DTYPE), *rs.out_shapes],
        input_output_aliases=rs.aliases,
        scratch_shapes=[pltpu.VMEM((HEAD_DIM, HEAD_DIM), F32), *rs.scratch],
        compiler_params=_params("arbitrary", "arbitrary"),
    )(proj, proj, proj, proj, cos_f, sin_f, decay, k_dec, q_dec, c_dec, gn_g, *rs.arrays)
    return res[:3], rs.split(res[3:])


def _retention_bwd(proj, gn_g, tables, y, prev, dmix, riders=()):
    s = proj.shape[0]
    nb = s // RET_ROWS
    cos_f, sin_f, decay, k_dec, q_dec, c_dec = tables

    def body(q_ref, k_ref, v_ref, g_ref, cos_ref, sin_ref, dec_ref, kd_ref, qd_ref, cd_ref, gn_ref,
             y_ref, prev_ref, dret_ref, dq_ref, dk_ref, dv_ref, dg_ref, dgn_ref, gstate_ref):
        @pl.when(pl.program_id(1) == 0)
        def _():
            gstate_ref[...] = jnp.zeros_like(gstate_ref)
            dgn_ref[...] = jnp.zeros_like(dgn_ref)

        cosv, sinv = cos_ref[...], sin_ref[...]
        q = _rot(q_ref[...], cosv, sinv)
        k = _rot(k_ref[...], cosv, sinv) * K_SCALE
        v = v_ref[...]
        dec, kd, qd, cd, gn = dec_ref[...], kd_ref[...], qd_ref[...], cd_ref[...], gn_ref[...]
        kdf, qdf = k * kd, q * qd
        rg = g_ref[...]
        yv = y_ref[...]
        dret = dret_ref[...]
        sig = jax.nn.sigmoid(rg)
        gate = rg * sig
        mu = jnp.mean(yv, axis=-1, keepdims=True)
        yc = yv - mu
        rstd = lax.rsqrt(jnp.mean(yc * yc, axis=-1, keepdims=True) + GN_EPS)
        z = yc * rstd
        dyn = dret * gate
        dg_ref[...] = (dret * (z * gn) * (sig * (1.0 + rg * (1.0 - sig)))).astype(dg_ref.dtype)
        dgn_ref[...] += jnp.sum(dyn * z, axis=0, keepdims=True)
        dz = dyn * gn
        dy = rstd * (dz - jnp.mean(dz, axis=-1, keepdims=True) - z * jnp.mean(dz * z, axis=-1, keepdims=True))
        chunks = [slice(c * CHUNK, (c + 1) * CHUNK) for c in range(RET_BLOCK_CHUNKS)]
        dprevs = [_dot(qdf[rows], dy[rows], TN) for rows in chunks]
        gst, gsts = gstate_ref[...], [None] * RET_BLOCK_CHUNKS
        for c in reversed(range(RET_BLOCK_CHUNKS)):
            gsts[c] = gst
            gst = dprevs[c] + cd * gst
        gstate_ref[...] = gst
        dq = jnp.concatenate([_dot(dy[rows], prev_ref[c], NT) for c, rows in enumerate(chunks)], axis=0) * qd
        dk = jnp.concatenate([_dot(v[rows], g, NT) for rows, g in zip(chunks, gsts)], axis=0) * kd
        dv = jnp.concatenate([_dot(kdf[rows], g, NN) for rows, g in zip(chunks, gsts)], axis=0)
        dqi, dki, dvi = [], [], []
        for b in range(RET_ROWS // RET_SUB):
            rows = slice(b * RET_SUB, (b + 1) * RET_SUB)
            qs, ks, vs, dys = q[rows], k[rows], v[rows], dy[rows]
            dvi.append(_dot(_dot(ks, qs, NT) * dec, dys, NN))
            dqi.append(_dot(_dot(dys, vs, NT) * dec, ks, NN))
            dki.append(_dot(_dot(vs, dys, NT) * dec, qs, NN))
        dq = dq + jnp.concatenate(dqi, axis=0)
        dk = dk + jnp.concatenate(dki, axis=0)
        dv = dv + jnp.concatenate(dvi, axis=0)
        dq_ref[...] = _rot_bwd(dq, cosv, sinv).astype(dq_ref.dtype)
        dk_ref[...] = _rot_bwd(dk * K_SCALE, cosv, sinv).astype(dk_ref.dtype)
        dv_ref[...] = dv.astype(dv_ref.dtype)

    rev = lambda i: nb - 1 - i

    def col(off):
        return pl.BlockSpec((RET_ROWS, HEAD_DIM), lambda h, i: (rev(i), off + h))

    pos = pl.BlockSpec((RET_ROWS, HEAD_DIM), lambda h, i: (rev(i), 0))
    per_head = lambda shape: pl.BlockSpec((None, *shape), lambda h, i: (h, 0, 0))
    outb = _sds((s, HEADS * HEAD_DIM), BF16)
    rs = _Riders(riders, 14, 5)
    res = _pallas(
        _with_riders(body, 14, 5, 1, rs, (HEADS, nb)), name="retention_bwd", grid=(HEADS, nb),
        in_specs=[col(0), col(HEADS), col(2 * HEADS), col(3 * HEADS), pos, pos,
                  per_head((RET_SUB, RET_SUB)), per_head((RET_ROWS, HEAD_DIM)), per_head((RET_ROWS, HEAD_DIM)),
                  per_head((1, HEAD_DIM)), pl.BlockSpec((1, HEAD_DIM), lambda h, i: (0, h)),
                  col(0), pl.BlockSpec((None, RET_BLOCK_CHUNKS, HEAD_DIM, HEAD_DIM), lambda h, i: (h, rev(i), 0, 0)),
                  col(0), *rs.in_specs],
        out_specs=[col(0), col(0), col(0), col(0), per_head((1, HEAD_DIM)), *rs.out_specs],
        out_shape=[outb, outb, outb, outb, _sds((HEADS, 1, HEAD_DIM), F32), *rs.out_shapes],
        input_output_aliases=rs.aliases,
        scratch_shapes=[pltpu.VMEM((HEAD_DIM, HEAD_DIM), F32), *rs.scratch],
        compiler_params=_params("arbitrary", "arbitrary"),
    )(proj, proj, proj, proj, cos_f, sin_f, decay, k_dec, q_dec, c_dec, gn_g, y, prev, dmix, *rs.arrays)
    return res[:5], rs.split(res[5:])


ATT_COL0 = 4 * HEADS
PAD_ROWS = LEFT_CHUNKS * CHUNK
NORM_ROWS = 512
GROUP_CHUNKS = 4
GROUP = GROUP_CHUNKS * CHUNK
WIN = (LEFT_CHUNKS + GROUP_CHUNKS) * CHUNK
MASKED = -1e30


def _qk_norm(x, g):
    return x * lax.rsqrt(jnp.mean(x * x, axis=-1, keepdims=True) + EPS) * g


def _band_probs(qb, kb, bias, g):
    sc = _dot(qb, kb, NT) * K_SCALE + bias
    win_chunk = lax.broadcasted_iota(jnp.int32, (GROUP, WIN), 1) // CHUNK
    sc = jnp.where(g * GROUP_CHUNKS - LEFT_CHUNKS + win_chunk >= 0, sc, MASKED)
    e = jnp.exp(sc - jnp.max(sc, axis=-1, keepdims=True))
    return e / jnp.sum(e, axis=-1, keepdims=True)


def _with_riders(core, n_in, n_out, n_scratch, rs, grid):
    n_rin, n_rout = len(rs.arrays), len(rs.out_shapes)
    if not rs.riders:
        return core
    steps = 1
    for n in grid:
        steps *= n

    def body(*refs):
        outs_at = n_in + n_rin
        scratch_at = outs_at + n_out + n_rout
        bound = rs.bind(refs[n_in:outs_at], refs[outs_at + n_out:scratch_at], refs[scratch_at + n_scratch:])
        step = 0
        for axis, n in enumerate(grid):
            step = step * n + pl.program_id(axis)
        pl.when(step == 0)(lambda: rs.run("start", bound))
        pl.when(step == int(steps * RIDER_MID))(lambda: rs.run("mid", bound))
        core(*refs[:n_in], *refs[outs_at:outs_at + n_out], *refs[scratch_at:scratch_at + n_scratch])
        pl.when(step == steps - 1)(lambda: rs.run("end", bound))

    return body


def _attention_fwd(proj, gq, gk, bias, mix, riders=()):
    s = proj.shape[0]
    rs = _Riders(riders, 7, 1)

    def body(q_ref, k_ref, v_ref, gq_ref, gk_ref, bias_ref, mix_ref, o_ref, kp_ref, vp_ref):
        kp_ref[0:PAD_ROWS, :] = jnp.zeros((PAD_ROWS, HEAD_DIM), kp_ref.dtype)
        vp_ref[0:PAD_ROWS, :] = jnp.zeros((PAD_ROWS, HEAD_DIM), vp_ref.dtype)
        gqv, gkv = gq_ref[...], gk_ref[...]

        def fill(b, carry):
            r0 = pl.multiple_of(b * NORM_ROWS, NORM_ROWS)
            kp_ref[pl.ds(PAD_ROWS + r0, NORM_ROWS), :] = _qk_norm(k_ref[pl.ds(r0, NORM_ROWS), :], gkv).astype(kp_ref.dtype)
            vp_ref[pl.ds(PAD_ROWS + r0, NORM_ROWS), :] = v_ref[pl.ds(r0, NORM_ROWS), :].astype(vp_ref.dtype)
            return carry

        lax.fori_loop(0, s // NORM_ROWS, fill, 0)

        def group(g, carry):
            r0 = pl.multiple_of(g * GROUP, GROUP)
            qn = _qk_norm(q_ref[pl.ds(r0, GROUP), :], gqv)
            p = _band_probs(qn, kp_ref[pl.ds(r0, WIN), :], bias_ref[...], g)
            o_ref[pl.ds(r0, GROUP), :] = _dot(p, vp_ref[pl.ds(r0, WIN), :], NN).astype(o_ref.dtype)
            return carry

        lax.fori_loop(0, s // GROUP, group, 0, unroll=2)

    def col(off):
        return pl.BlockSpec((s, HEAD_DIM), lambda h: (0, off + h))

    vec = pl.BlockSpec((1, HEAD_DIM), lambda h: (0, 0))
    res = _pallas(
        _with_riders(body, 7, 1, 2, rs, (HEADS,)), name="attention_fwd", grid=(HEADS,),
        in_specs=[col(ATT_COL0), col(ATT_COL0 + HEADS), col(ATT_COL0 + 2 * HEADS), vec, vec,
                  pl.BlockSpec((None, GROUP, WIN), lambda h: (h, 0, 0)), ANY, *rs.in_specs],
        out_specs=[col(HEADS), *rs.out_specs], out_shape=[_sds(mix.shape, mix.dtype), *rs.out_shapes],
        input_output_aliases={6: 0, **rs.aliases},
        scratch_shapes=[pltpu.VMEM((s + PAD_ROWS, HEAD_DIM), MXU_DTYPE), pltpu.VMEM((s + PAD_ROWS, HEAD_DIM), MXU_DTYPE),
                        *rs.scratch],
        compiler_params=_params("arbitrary"),
    )(proj, proj, proj, gq, gk, bias, mix, *rs.arrays)
    return res[0], rs.split(res[1:])


def _attention_bwd(proj, gq, gk, bias, dmix, riders=()):
    s = proj.shape[0]
    nc = s // CHUNK
    rs = _Riders(riders, 7, 6)

    def body(q_ref, k_ref, v_ref, gq_ref, gk_ref, bias_ref, do_ref,
             dq_ref, dk_ref, dv_ref, dgq_ref, dgk_ref, dbias_ref, kp_ref, vp_ref, dkp_ref, dvp_ref, dqn_ref):
        kp_ref[0:PAD_ROWS, :] = jnp.zeros((PAD_ROWS, HEAD_DIM), kp_ref.dtype)
        vp_ref[0:PAD_ROWS, :] = jnp.zeros((PAD_ROWS, HEAD_DIM), vp_ref.dtype)
        dkp_ref[...] = jnp.zeros_like(dkp_ref)
        dvp_ref[...] = jnp.zeros_like(dvp_ref)
        dbias_ref[...] = jnp.zeros_like(dbias_ref)
        gqv, gkv = gq_ref[...], gk_ref[...]

        def fill(b, carry):
            r0 = pl.multiple_of(b * NORM_ROWS, NORM_ROWS)
            kp_ref[pl.ds(PAD_ROWS + r0, NORM_ROWS), :] = _qk_norm(k_ref[pl.ds(r0, NORM_ROWS), :], gkv).astype(kp_ref.dtype)
            vp_ref[pl.ds(PAD_ROWS + r0, NORM_ROWS), :] = v_ref[pl.ds(r0, NORM_ROWS), :].astype(vp_ref.dtype)
            return carry

        lax.fori_loop(0, s // NORM_ROWS, fill, 0)

        def group(g, carry):
            r0 = pl.multiple_of(g * GROUP, GROUP)
            qn = _qk_norm(q_ref[pl.ds(r0, GROUP), :], gqv)
            kb = kp_ref[pl.ds(r0, WIN), :]
            vb = vp_ref[pl.ds(r0, WIN), :]
            p = _band_probs(qn, kb, bias_ref[...], g)
            do = do_ref[pl.ds(r0, GROUP), :]
            dvp_ref[pl.ds(r0, WIN), :] += _dot(p, do, TN)
            dp = _dot(do, vb, NT)
            ds = p * (dp - jnp.sum(dp * p, axis=-1, keepdims=True))
            dbias_ref[...] += ds
            dss = ds * K_SCALE
            dqn_ref[pl.ds(r0, GROUP), :] = _dot(dss, kb, NN)
            dkp_ref[pl.ds(r0, WIN), :] += _dot(dss, qn, TN)
            return carry

        lax.fori_loop(0, s // GROUP, group, 0, unroll=2)

        @pl.when(pl.program_id(0) == 0)
        def _():
            dgq_ref[...] = jnp.zeros_like(dgq_ref)
            dgk_ref[...] = jnp.zeros_like(dgk_ref)

        def norm_bwd(x, g, dn):
            rstd = lax.rsqrt(jnp.mean(x * x, axis=-1, keepdims=True) + EPS)
            xh = x * rstd
            dxh = dn * g
            return rstd * (dxh - xh * jnp.mean(dxh * xh, axis=-1, keepdims=True)), jnp.sum(dn * xh, axis=0, keepdims=True)

        def finish(b, carry):
            r0 = pl.multiple_of(b * NORM_ROWS, NORM_ROWS)
            rows = pl.ds(r0, NORM_ROWS)
            dq, dgq = norm_bwd(q_ref[rows, :], gqv, dqn_ref[rows, :])
            dk, dgk = norm_bwd(k_ref[rows, :], gkv, dkp_ref[pl.ds(PAD_ROWS + r0, NORM_ROWS), :])
            dq_ref[rows, :] = dq.astype(dq_ref.dtype)
            dk_ref[rows, :] = dk.astype(dk_ref.dtype)
            dv_ref[rows, :] = dvp_ref[pl.ds(PAD_ROWS + r0, NORM_ROWS), :].astype(dv_ref.dtype)
            dgq_ref[...] += dgq
            dgk_ref[...] += dgk
            return carry

        lax.fori_loop(0, s // NORM_ROWS, finish, 0)

    def col(off):
        return pl.BlockSpec((s, HEAD_DIM), lambda h: (0, off + h))

    vec = pl.BlockSpec((1, HEAD_DIM), lambda h: (0, 0))
    hbias = pl.BlockSpec((None, GROUP, WIN), lambda h: (h, 0, 0))
    outb = _sds((s, HEADS * HEAD_DIM), BF16)
    res = _pallas(
        _with_riders(body, 7, 6, 5, rs, (HEADS,)), name="attention_bwd", grid=(HEADS,),
        in_specs=[col(ATT_COL0), col(ATT_COL0 + HEADS), col(ATT_COL0 + 2 * HEADS), vec, vec, hbias, col(HEADS),
                  *rs.in_specs],
        out_specs=[col(0), col(0), col(0), vec, vec, hbias, *rs.out_specs],
        out_shape=[outb, outb, outb, _sds((1, HEAD_DIM), F32), _sds((1, HEAD_DIM), F32),
                   _sds((HEADS, GROUP, WIN), F32), *rs.out_shapes],
        input_output_aliases=rs.aliases,
        scratch_shapes=[pltpu.VMEM((s + PAD_ROWS, HEAD_DIM), MXU_DTYPE), pltpu.VMEM((s + PAD_ROWS, HEAD_DIM), MXU_DTYPE),
                        pltpu.VMEM((s + PAD_ROWS, HEAD_DIM), F32), pltpu.VMEM((s + PAD_ROWS, HEAD_DIM), F32),
                        pltpu.VMEM((s, HEAD_DIM), F32), *rs.scratch],
        compiler_params=_params("arbitrary"),
    )(proj, proj, proj, gq, gk, bias, dmix, *rs.arrays)
    return res[:6], rs.split(res[6:])


DIAG_SPLIT = (BAND + WIN - CHUNK) // 2


def _diag_bin(m):
    t = jnp.where(m < DIAG_SPLIT, m, m - WIN)
    return jnp.clip(LEFT_CHUNKS * CHUNK - t, -(CHUNK - 1), REL_CLIP) + (CHUNK - 1)


def _skew_rows(a, left):
    row = lax.broadcasted_iota(jnp.int32, (GROUP, WIN), 0)
    for b in range(GROUP.bit_length() - 1):
        step = 1 << b
        a = jnp.where(jnp.bitwise_and(row, step) != 0, pltpu.roll(a, WIN - step if left else step, 1), a)
    return a


def _rel_bias_expand(rel_bias):
    def body(rb_ref, o_ref):
        h = pl.program_id(0)
        bins = _diag_bin(lax.broadcasted_iota(jnp.int32, (8, WIN), 1))
        per_diag = lax.fori_loop(0, REL_SIZE, lambda r, acc: jnp.where(bins == r, rb_ref[h, r], acc),
                                 jnp.zeros((8, WIN), F32))
        table = _skew_rows(jnp.broadcast_to(per_diag[0:1], (GROUP, WIN)), left=False)
        row_chunk = lax.broadcasted_iota(jnp.int32, (GROUP, WIN), 0) // CHUNK
        col_chunk = lax.broadcasted_iota(jnp.int32, (GROUP, WIN), 1) // CHUNK
        in_band = jnp.logical_and(col_chunk >= row_chunk, col_chunk <= row_chunk + LEFT_CHUNKS)
        o_ref[...] = jnp.where(in_band, table, MASKED)

    return _pallas(body, name="rel_bias_expand", grid=(HEADS,), in_specs=[pl.BlockSpec(memory_space=pltpu.SMEM)],
                   out_specs=pl.BlockSpec((None, GROUP, WIN), lambda h: (h, 0, 0)),
                   out_shape=_sds((HEADS, GROUP, WIN), F32), compiler_params=_params("parallel"))(rel_bias)


def _rel_bias_fold(dbias):
    def body(a_ref, o_ref):
        diag = jnp.sum(_skew_rows(a_ref[...], left=True), axis=0, keepdims=True)
        onehot = (_diag_bin(lax.broadcasted_iota(jnp.int32, (WIN, REL_SIZE), 0))
                  == lax.broadcasted_iota(jnp.int32, (WIN, REL_SIZE), 1)).astype(MXU_DTYPE)
        rest = jnp.broadcast_to(diag, (8, WIN))
        out = jnp.zeros((8, REL_SIZE), F32)
        for _ in range(3):
            piece = rest.astype(BF16)
            out = out + _dot(piece, onehot, NN)
            rest = rest - piece.astype(F32)
        o_ref[...] = out[0:1]

    return _pallas(body, name="rel_bias_fold", grid=(HEADS,),
                   in_specs=[pl.BlockSpec((None, GROUP, WIN), lambda h: (h, 0, 0))],
                   out_specs=pl.BlockSpec((None, 1, REL_SIZE), lambda h: (h, 0, 0)),
                   out_shape=_sds((HEADS, 1, REL_SIZE), F32), compiler_params=_params("parallel"))(dbias)


def _place():
    return lax.axis_index("x"), lax.axis_index("y"), lax.axis_index("c")


def _other_chips(x, y):
    return [(1 - x, y), (x, 1 - y), (1 - x, 1 - y)]


class _Rider:
    reads, ins, new, n_sems = (), (), (), 1

    def start(self, reads, ins, new, send, recv):
        pass

    def mid(self, reads, ins, new, send, recv):
        pass

    def end(self, reads, ins, new, send, recv):
        pass


class _Riders:
    def __init__(self, riders, n_host_in, n_host_out):
        self.riders = list(riders)
        self.arrays, self.out_shapes, self.aliases, self.scratch = [], [], {}, []
        for r in self.riders:
            for t, a in enumerate(r.ins):
                self.aliases[n_host_in + len(self.arrays) + len(r.reads) + t] = n_host_out + len(self.out_shapes) + t
            self.arrays += [*r.reads, *r.ins]
            self.out_shapes += [_sds(a.shape, a.dtype) for a in r.ins] + list(r.new)
            self.scratch += [pltpu.SemaphoreType.DMA((r.n_sems,)), pltpu.SemaphoreType.DMA((r.n_sems,))]
        self.in_specs = [ANY] * len(self.arrays)
        self.out_specs = [ANY] * len(self.out_shapes)

    def bind(self, in_refs, out_refs, scratch_refs):
        bound, i, o = [], 0, 0
        for t, r in enumerate(self.riders):
            reads = in_refs[i:i + len(r.reads)]
            i += len(r.reads) + len(r.ins)
            ins = out_refs[o:o + len(r.ins)]
            new = out_refs[o + len(r.ins):o + len(r.ins) + len(r.new)]
            o += len(r.ins) + len(r.new)
            bound.append((reads, ins, new, scratch_refs[2 * t], scratch_refs[2 * t + 1]))
        return bound

    def run(self, phase, bound):
        for r, b in zip(self.riders, bound):
            getattr(r, phase)(*b)

    def split(self, outs):
        res, o = [], 0
        for r in self.riders:
            n = len(r.ins) + len(r.new)
            res.append(list(outs[o:o + n]))
            o += n
        return res


def _run_riders(name, riders):
    rs = _Riders(riders, 0, 0)
    n_in, n_out = len(rs.arrays), len(rs.out_shapes)

    def body(*refs):
        bound = rs.bind(refs[:n_in], refs[n_in:n_in + n_out], refs[n_in + n_out:])
        rs.run("start", bound)
        rs.run("mid", bound)
        rs.run("end", bound)

    outs = _pallas(body, name=name, in_specs=rs.in_specs, out_specs=rs.out_specs, out_shape=rs.out_shapes,
                   input_output_aliases=rs.aliases, scratch_shapes=rs.scratch)(*rs.arrays)
    return rs.split(outs)


class _GatherRider(_Rider):
    def __init__(self, blocks, part=(0, 1, 1)):
        self.ins = tuple(blocks)
        self.part = part
        self.n_sems = 6 * len(blocks)

    def _copy(self, out, send, recv, w, j, chip_from, cc, to):
        hr = self.ins[w].shape[1] // 2
        lo, hi, n = self.part
        half = out[w].at[2 * chip_from[0] + chip_from[1], pl.ds(cc * hr + lo * (hr // n), (hi - lo) * (hr // n)), :]
        return pltpu.make_async_remote_copy(src_ref=half, dst_ref=half, send_sem=send.at[6 * w + j],
                                            recv_sem=recv.at[6 * w + j], device_id=to, device_id_type=MESH)

    def start(self, reads, out, new, send, recv):
        x, y, c = _place()
        for w in range(len(self.ins)):
            for j, chip in enumerate(_other_chips(x, y)):
                self._copy(out, send, recv, w, j, (x, y), c, (*chip, c)).start()

    def mid(self, reads, out, new, send, recv):
        x, y, c = _place()
        for w in range(len(self.ins)):
            for j, chip in enumerate(_other_chips(x, y)):
                self._copy(out, send, recv, w, j, chip, c, (x, y, c)).wait_recv()
                self._copy(out, send, recv, w, 3 + j, chip, c, (x, y, 1 - c)).start()

    def end(self, reads, out, new, send, recv):
        x, y, c = _place()
        for w in range(len(self.ins)):
            for j, chip in enumerate(_other_chips(x, y)):
                self._copy(out, send, recv, w, 3 + j, chip, 1 - c, (x, y, c)).wait_recv()
        for w in range(len(self.ins)):
            for j, chip in enumerate(_other_chips(x, y)):
                self._copy(out, send, recv, w, j, (x, y), c, (*chip, c)).wait_send()
                self._copy(out, send, recv, w, 3 + j, chip, c, (x, y, 1 - c)).wait_send()


class _SwapRider(_Rider):
    def __init__(self, grads):
        self.reads = tuple(grads)
        self.new = tuple(_sds((N_CHIPS, g.shape[1] // 2, g.shape[2]), g.dtype) for g in grads)
        self.n_sems = len(grads)

    def _copies(self, src, new, send, recv):
        x, y, c = _place()
        copies = []
        for w in range(len(self.reads)):
            hr = self.reads[w].shape[1] // 2
            copies.append(pltpu.make_async_remote_copy(
                src_ref=src[w].at[:, pl.ds((1 - c) * hr, hr), :], dst_ref=new[w],
                send_sem=send.at[w], recv_sem=recv.at[w], device_id=(x, y, 1 - c), device_id_type=MESH))
        return copies

    def start(self, src, ins, new, send, recv):
        for cp in self._copies(src, new, send, recv):
            cp.start()

    def end(self, src, ins, new, send, recv):
        for cp in self._copies(src, new, send, recv):
            cp.wait()


def _add_half(g, got, c_arr, name):
    nk, r, cols = g.shape
    hr = r // 2
    tr = min(hr, 256)
    nb = hr // tr

    def body(c_ref, g_ref, got_ref, o_ref):
        o_ref[...] = (g_ref[...].astype(F32) + got_ref[...].astype(F32)).astype(o_ref.dtype)

    grid_spec = pltpu.PrefetchScalarGridSpec(
        num_scalar_prefetch=1, grid=(nk, nb),
        in_specs=[pl.BlockSpec((None, tr, cols), lambda k, i, c_ref: (k, c_ref[0] * nb + i, 0)),
                  pl.BlockSpec((None, tr, cols), lambda k, i, c_ref: (k, i, 0))],
        out_specs=pl.BlockSpec((None, tr, cols), lambda k, i, c_ref: (k, i, 0)))
    return _pallas(body, name=name, grid_spec=grid_spec, out_shape=_sds((nk, hr, cols), g.dtype),
                   compiler_params=_params("parallel", "parallel"))(c_arr, g, got)


class _SendPartialsRider(_Rider):
    def __init__(self, parts, got=None, part=(0, 1, 1)):
        self.reads = tuple(parts)
        if got is None:
            self.new = tuple(_sds((N_CHIPS - 1, *p.shape[1:]), p.dtype) for p in parts)
        else:
            self.ins = tuple(got)
        self.part = part
        self.n_sems = 3 * len(parts)

    def _copies(self, src, ins, new, send, recv):
        x, y, c = _place()
        land = ins if self.ins else new
        lo, hi, n = self.part
        copies = []
        for w in range(len(self.reads)):
            pr = self.reads[w].shape[1] // n
            rows = pl.ds(lo * pr, (hi - lo) * pr)
            for j, chip in enumerate(_other_chips(x, y)):
                copies.append(pltpu.make_async_remote_copy(
                    src_ref=src[w].at[2 * chip[0] + chip[1], rows, :], dst_ref=land[w].at[j, rows, :],
                    send_sem=send.at[3 * w + j], recv_sem=recv.at[3 * w + j], device_id=(*chip, c), device_id_type=MESH))
        return copies

    def start(self, src, ins, new, send, recv):
        for cp in self._copies(src, ins, new, send, recv):
            cp.start()

    def end(self, src, ins, new, send, recv):
        for cp in self._copies(src, ins, new, send, recv):
            cp.wait()


def _sum_partials(part, got, kc_arr, name):
    _, hr, cols = part.shape
    tr = min(hr, 256)
    nb = hr // tr

    def body(kc_ref, p_ref, g0_ref, g1_ref, g2_ref, o_ref):
        o_ref[...] = ((p_ref[...].astype(F32) + g0_ref[...].astype(F32)) + g1_ref[...].astype(F32)) + g2_ref[...].astype(F32)

    slot = lambda j: pl.BlockSpec((None, tr, cols), lambda i, kc_ref: (j, i, 0))
    grid_spec = pltpu.PrefetchScalarGridSpec(
        num_scalar_prefetch=1, grid=(nb,),
        in_specs=[pl.BlockSpec((None, tr, cols), lambda i, kc_ref: (kc_ref[0], i, 0)), slot(0), slot(1), slot(2)],
        out_specs=pl.BlockSpec((tr, cols), lambda i, kc_ref: (kc_ref[1] * nb + i, 0)))
    return _pallas(body, name=name, grid_spec=grid_spec, out_shape=_sds((2 * hr, cols), F32),
                   compiler_params=_params("parallel"))(kc_arr, part, got, got, got)


class _ShareRider(_Rider):
    def __init__(self, grads):
        self.ins = tuple(grads)
        self.n_sems = len(grads)

    def _copies(self, out, send, recv):
        x, y, c = _place()
        copies = []
        for w in range(len(self.ins)):
            hr = self.ins[w].shape[0] // 2
            mine = out[w].at[pl.ds(c * hr, hr), :]
            copies.append(pltpu.make_async_remote_copy(
                src_ref=mine, dst_ref=mine, send_sem=send.at[w], recv_sem=recv.at[w],
                device_id=(x, y, 1 - c), device_id_type=MESH))
        return copies

    def start(self, reads, out, new, send, recv):
        for cp in self._copies(out, send, recv):
            cp.start()

    def end(self, reads, out, new, send, recv):
        for cp in self._copies(out, send, recv):
            cp.wait()


def _small_allreduce_adamw(g_part, w, m, v):
    rows = g_part.shape[0]

    def body(g_ref, w_ref, m_ref, v_ref, go_ref, d_ref, nm_ref, nv_ref, all_ref, send_sems, recv_sems):
        x, y, c = _place()
        me = 4 * x + 2 * y + c
        all_ref[me] = g_ref[...]
        copies = []
        for r in range(1, 8):
            dx, dy, dc = (r >> 2) & 1, (r >> 1) & 1, r & 1
            peer = (1 - x if dx else x, 1 - y if dy else y, 1 - c if dc else c)
            copies.append(pltpu.make_async_remote_copy(
                src_ref=g_ref, dst_ref=all_ref.at[me], send_sem=send_sems.at[r - 1], recv_sem=recv_sems.at[r - 1],
                device_id=peer, device_id_type=MESH))
        for cp in copies:
            cp.start()
        for cp in copies:
            cp.wait()
        tot = all_ref[0]
        for d in range(1, 8):
            tot = tot + all_ref[d]
        go_ref[...] = tot
        d_ref[...], nm_ref[...], nv_ref[...] = _adamw_math(w_ref[...], tot, m_ref[...], v_ref[...])

    vm = pl.BlockSpec(memory_space=pltpu.VMEM)
    return _pallas(
        body, name="small_allreduce_adamw", in_specs=[vm] * 4, out_specs=[vm] * 4,
        out_shape=[_sds((rows, 128), F32)] * 4,
        scratch_shapes=[pltpu.VMEM((8, rows, 128), F32), pltpu.SemaphoreType.DMA((7,)), pltpu.SemaphoreType.DMA((7,))],
    )(g_part, w, m, v)


SMALL_SIZES = (2048, 1024, 128, 128, HEADS * REL_SIZE, 2048)
SMALL_PART_ROWS = tuple(-(-size // 1024) * 8 for size in SMALL_SIZES)
SMALL_ROWS = sum(SMALL_PART_ROWS)


def _pack_small(parts):
    rows = []
    for p, size, nr in zip(parts, SMALL_SIZES, SMALL_PART_ROWS):
        rows.append(jnp.pad(p.reshape(-1), (0, nr * 128 - size)).reshape(nr, 128))
    return jnp.concatenate(rows, axis=0)


def _unpack_small(slab, shapes):
    out, off = [], 0
    for size, nr, shape in zip(SMALL_SIZES, SMALL_PART_ROWS, shapes):
        out.append(slab[off:off + nr].reshape(-1)[:size].reshape(shape))
        off += nr
    return out


def kernel(x, norm1_g, w_in, ret_norm_g, q_norm_g, k_norm_g, rel_bias, w_out, norm2_g, w_ff1, w_ff2, loss_target, m_norm1_g, m_w_in, m_ret_norm_g, m_q_norm_g, m_k_norm_g, m_rel_bias, m_w_out, m_norm2_g, m_w_ff1, m_w_ff2, v_norm1_g, v_w_in, v_ret_norm_g, v_q_norm_g, v_k_norm_g, v_rel_bias, v_w_out, v_norm2_g, v_w_ff1, v_w_ff2):
    xs = x[0]
    tgt = loss_target[0]
    s, d = xs.shape
    d_in = N_CHIPS * w_in.shape[2]
    d_ff = N_CHIPS * w_ff1.shape[2]
    in_sh, ff_sh = w_in.shape[2], w_ff1.shape[2]
    tm = min(s, 1024)
    gi = s // tm
    c_arr = lax.axis_index("c").astype(jnp.int32).reshape(1)
    k_arr = (2 * lax.axis_index("x") + lax.axis_index("y")).astype(jnp.int32).reshape(1)
    tables = _tables(s)
    bias = _rel_bias_expand(rel_bias[0])

    blk_in, blk_out, blk_ff1, blk_ff2 = (
        _cast_bf16(w_in[0], k_arr, "cast_w_in"), _cast_bf16(w_out[0], k_arr, "cast_w_out"),
        _cast_bf16(w_ff1[0], k_arr, "cast_w_ff1"), _cast_bf16(w_ff2[0], k_arr, "cast_w_ff2"))
    ((wg_in,),) = _run_riders("all_gather_w_in", [_GatherRider([blk_in])])

    h1 = _rmsnorm_fwd(xs, norm1_g, "rmsnorm1")
    tn_in = in_sh // 2
    tk = d
    (proj,), ((wg_ff1,),) = _mm(
        "proj", h1, wg_in, NN, (gi, 2 * N_CHIPS, d // tk),
        pl.BlockSpec((tm, tk), lambda i, j, k: (i, k)),
        pl.BlockSpec((None, tk, tn_in), lambda i, j, k: (j // 2, k, j % 2)),
        [_sds((s, d_in), F32)], [pl.BlockSpec((tm, tn_in), lambda i, j, k: (i, j))], (tm, tn_in),
        riders=[_GatherRider([blk_ff1], (0, 3, 4))])
    (mix, y_ret, prev), ((wg_ff1,),) = _retention_fwd(proj, ret_norm_g, tables, riders=[_GatherRider([wg_ff1], (3, 4, 4))])
    mix, ((wg_out,), (wg_ff2,)) = _attention_fwd(
        proj, q_norm_g, k_norm_g, bias, mix, riders=[_GatherRider([blk_out]), _GatherRider([blk_ff2], (0, 1, 4))])
    wg_out = wg_out.reshape(d, d)
    tn = 1024
    tile = pl.BlockSpec((tm, tn), lambda i, j, k: (i, j))
    (x1,) = _mm("out_proj", mix, wg_out, NN, (gi, d // tn, d // tk),
                pl.BlockSpec((tm, tk), lambda i, j, k: (i, k)), pl.BlockSpec((tk, tn), lambda i, j, k: (k, j)),
                [_sds((s, d), F32)], [tile], (tm, tn), extras=(xs,), extra_specs=(tile,),
                epi=lambda acc, r: (r + acc,))
    h2 = _rmsnorm_fwd(x1, norm2_g, "rmsnorm2")
    tn_ff = min(ff_sh, 1024)
    per = ff_sh // tn_ff

    def relu2(acc):
        r = jnp.maximum(acc, 0.0)
        return acc, r * r

    (u, act), ((wg_ff2,),) = _mm(
        "ff1", h2, wg_ff1, NN, (gi, N_CHIPS * per, d // tk),
        pl.BlockSpec((tm, tk), lambda i, j, k: (i, k)),
        pl.BlockSpec((None, tk, tn_ff), lambda i, j, k: (j // per, k, j % per)),
        [_sds((s, d_ff), F32), _sds((s, d_ff), BF16)],
        [pl.BlockSpec((tm, tn_ff), lambda i, j, k: (i, j))] * 2, (tm, tn_ff), epi=relu2,
        riders=[_GatherRider([wg_ff2], (1, 4, 4))])
    wg_ff2 = wg_ff2.reshape(d_ff, d)

    def loss_epi(acc, res, t):
        diff = (res + acc) - t
        dy = diff / d
        return dy, dy, jnp.sum(diff * diff, axis=0, keepdims=True)

    tk2 = 1024
    dy, dyb, loss_cols = _mm(
        "ff2_loss", act, wg_ff2, NN, (gi, d // tn, d_ff // tk2),
        pl.BlockSpec((tm, tk2), lambda i, j, k: (i, k)), pl.BlockSpec((tk2, tn), lambda i, j, k: (k, j)),
        [_sds((s, d), F32), _sds((s, d), BF16), _sds((gi, 1, d), F32)],
        [tile, tile, pl.BlockSpec((None, 1, tn), lambda i, j, k: (i, 0, j))], (tm, tn),
        extras=(x1, tgt), extra_specs=(tile, tile), epi=loss_epi)
    loss = lax.psum(0.5 * jnp.sum(loss_cols) / d, ("x", "y", "c"))

    (du,) = _mm("d_act", dyb, wg_ff2, NT, (gi, d_ff // tn, d // tk),
                pl.BlockSpec((tm, tk), lambda i, j, k: (i, k)), pl.BlockSpec((tn, tk), lambda i, j, k: (j, k)),
                [_sds((s, d_ff), BF16)], [tile], (tm, tn), extras=(u,), extra_specs=(tile,),
                epi=lambda acc, uu: (acc * (2.0 * jnp.maximum(uu, 0.0)),))
    ts = min(s, 2048)
    wtile = pl.BlockSpec((tn, tn), lambda i, j, k: (i, j))
    (g_ff2,) = _mm("dw_ff2", act, dyb, TN, (d_ff // tn, d // tn, s // ts),
                   pl.BlockSpec((ts, tn), lambda i, j, k: (k, i)), pl.BlockSpec((ts, tn), lambda i, j, k: (k, j)),
                   [_sds((d_ff, d), BF16)], [wtile], (tn, tn))
    g_ff2 = g_ff2.reshape(N_CHIPS, d_ff // N_CHIPS, d)
    (g_ff1,), ((got_ff2,),) = _mm(
        "dw_ff1", h2, du, TN, (d // tn, N_CHIPS * per, s // ts),
        pl.BlockSpec((ts, tn), lambda i, j, k: (k, i)), pl.BlockSpec((ts, tn_ff), lambda i, j, k: (k, j)),
        [_sds((N_CHIPS, d, ff_sh), BF16)],
        [pl.BlockSpec((None, tn, tn_ff), lambda i, j, k: (j // per, i, j % per))], (tn, tn_ff),
        riders=[_SwapRider([g_ff2])])
    p_ff2 = _add_half(g_ff2, got_ff2, c_arr, "chip_partial_w_ff2")
    tkf = min(tk, ff_sh)
    kper = ff_sh // tkf
    (dh2,), ((got2_ff2,), (got_ff1,)) = _mm(
        "d_h2", du, wg_ff1, NT, (gi, d // tn, d_ff // tkf),
        pl.BlockSpec((tm, tkf), lambda i, j, k: (i, k)),
        pl.BlockSpec((None, tn, tkf), lambda i, j, k: (k // kper, j, k % kper)),
        [_sds((s, d), F32)], [tile], (tm, tn),
        riders=[_SendPartialsRider([p_ff2], part=(0, 3, 4)), _SwapRider([g_ff1])])
    p_ff1 = _add_half(g_ff1, got_ff1, c_arr, "chip_partial_w_ff1")
    dx1, dx1b, g_norm2 = _rmsnorm_bwd(x1, norm2_g, dh2, dy, "rmsnorm2_bwd")

    (dmix,) = _mm("d_mix", dx1b, wg_out, NT, (gi, d // tn, d // tk),
                  pl.BlockSpec((tm, tk), lambda i, j, k: (i, k)), pl.BlockSpec((tn, tk), lambda i, j, k: (j, k)),
                  [_sds((s, d), F32)], [tile], (tm, tn))
    (g_out,) = _mm("dw_out", mix, dx1b, TN, (d // tn, d // tn, s // ts),
                   pl.BlockSpec((ts, tn), lambda i, j, k: (k, i)), pl.BlockSpec((ts, tn), lambda i, j, k: (k, j)),
                   [_sds((d, d), BF16)], [wtile], (tn, tn))
    g_out = g_out.reshape(N_CHIPS, d // N_CHIPS, d)
    ((got_out,),) = _run_riders("grad_swap_w_out", [_SwapRider([g_out])])
    p_out = _add_half(g_out, got_out, c_arr, "chip_partial_w_out")
    (d_rq, d_rk, d_rv, d_rg, g_gn), ((got2_ff2,), (got2_ff1,)) = _retention_bwd(
        proj, ret_norm_g, tables, y_ret, prev, dmix,
        riders=[_SendPartialsRider([p_ff2], got=[got2_ff2], part=(3, 4, 4)), _SendPartialsRider([p_ff1], part=(0, 2, 4))])
    (d_aq, d_ak, d_av, g_gq, g_gk, dbias), ((got2_ff1,), (got2_out,)) = _attention_bwd(
        proj, q_norm_g, k_norm_g, bias, dmix,
        riders=[_SendPartialsRider([p_ff1], got=[got2_ff1], part=(2, 4, 4)), _SendPartialsRider([p_out])])
    g_rel = _rel_bias_fold(dbias)
    dproj = jnp.concatenate([d_rq, d_rk, d_rv, d_rg, d_aq, d_ak, d_av], axis=-1)
    (g_in,) = _mm("dw_in", h1, dproj, TN, (d // tn, 2 * N_CHIPS, s // ts),
                  pl.BlockSpec((ts, tn), lambda i, j, k: (k, i)), pl.BlockSpec((ts, tn_in), lambda i, j, k: (k, j)),
                  [_sds((N_CHIPS, d, in_sh), BF16)],
                  [pl.BlockSpec((None, tn, tn_in), lambda i, j, k: (j // 2, i, j % 2))], (tn, tn_in))
    ((got_in,),) = _run_riders("grad_swap_w_in", [_SwapRider([g_in])])
    p_in = _add_half(g_in, got_in, c_arr, "chip_partial_w_in")
    (dh1,), ((got2_in,),) = _mm(
        "d_h1", dproj, wg_in, NT, (gi, d // tn, N_CHIPS),
        pl.BlockSpec((tm, in_sh), lambda i, j, k: (i, k)),
        pl.BlockSpec((None, tn, in_sh), lambda i, j, k: (k, j, 0)),
        [_sds((s, d), F32)], [tile], (tm, tn), riders=[_SendPartialsRider([p_in], part=(0, 3, 4))])
    (grad_x, _, g_norm1), ((got2_in,),) = _rmsnorm_bwd(
        xs, norm1_g, dh1, dx1, "rmsnorm1_bwd", riders=[_SendPartialsRider([p_in], got=[got2_in], part=(3, 4, 4))])

    names = ["w_in", "w_out", "w_ff1", "w_ff2"]
    kc_arr = jnp.concatenate([k_arr, c_arr])
    halves = [_sum_partials(p, r, kc_arr, "sum_partials_" + nm)
              for p, r, nm in zip((p_in, p_out, p_ff1, p_ff2), (got2_in, got2_out, got2_ff1, got2_ff2), names)]
    (g_big,) = _run_riders("grad_share_halves", [_ShareRider(halves)])
    big = []
    for g, w, m, v, nm in zip(g_big, (w_in, w_out, w_ff1, w_ff2), (m_w_in, m_w_out, m_w_ff1, m_w_ff2),
                              (v_w_in, v_w_out, v_w_ff1, v_w_ff2), names):
        delta, new_m, new_v = _adamw(w[0], g, m[0], v[0], "adamw_" + nm)
        big.append((g[None], delta[None], new_m[None], new_v[None]))

    small_w = (norm1_g, ret_norm_g, q_norm_g, k_norm_g, rel_bias, norm2_g)
    small_m = (m_norm1_g, m_ret_norm_g, m_q_norm_g, m_k_norm_g, m_rel_bias, m_norm2_g)
    small_v = (v_norm1_g, v_ret_norm_g, v_q_norm_g, v_k_norm_g, v_rel_bias, v_norm2_g)
    shapes = [p.shape for p in small_w]
    g_small = _pack_small([g_norm1, g_gn, g_gq, g_gk, g_rel, g_norm2])
    sg, sd, sm, sv = (_unpack_small(a, shapes) for a in _small_allreduce_adamw(
        g_small, _pack_small(small_w), _pack_small(small_m), _pack_small(small_v)))

    def ordered(kind):
        sm_ = (sg, sd, sm, sv)[kind]
        return (sm_[0], big[0][kind], sm_[1], sm_[2], sm_[3], sm_[4], big[1][kind], sm_[5], big[2][kind], big[3][kind])

    return (loss, grad_x[None], *ordered(0), *ordered(1), *ordered(2), *ordered(3))
```

```python
import functools

import jax
import jax.numpy as jnp
from jax import lax
from jax.experimental import pallas as pl
from jax.experimental.pallas import tpu as pltpu

F32 = jnp.float32
BF16 = jnp.bfloat16
MXU_DTYPE = jnp.bfloat16

CHUNK = 64
HEADS = 8
HEAD_DIM = 128
LEFT_CHUNKS = 8
BAND = (LEFT_CHUNKS + 1) * CHUNK
REL_CLIP = 128
REL_SIZE = (CHUNK - 1) + REL_CLIP + 1
RET_BLOCK_CHUNKS = 8
RET_ROWS = RET_BLOCK_CHUNKS * CHUNK
RET_SUB = 256
ROPE_BASE = 10000.0
EPS = 1e-6
GN_EPS = 1e-5
ADAM_LR, ADAM_B1, ADAM_B2, ADAM_EPS, ADAM_WD, ADAM_STEP = 0.001, 0.9, 0.999, 1e-08, 0.01, 10
N_CHIPS = 4
VMEM_LIMIT = 56 * 1024 * 1024
MESH = pl.DeviceIdType.MESH
ANY = pl.BlockSpec(memory_space=pl.ANY)

NN = (((1,), (0,)), ((), ()))
NT = (((1,), (1,)), ((), ()))
TN = (((0,), (0,)), ((), ()))


def _pallas(body, **kw):
    return pl.pallas_call(body, **kw)


def _params(*sem):
    return pltpu.CompilerParams(dimension_semantics=sem, vmem_limit_bytes=VMEM_LIMIT)


def _dot(a, b, dims):
    return lax.dot_general(a.astype(MXU_DTYPE), b.astype(MXU_DTYPE), dims, preferred_element_type=F32)


RIDER_MID, RIDER_LATE = 0.6, 0.85


def _mm(name, a, b, dims, grid, a_spec, b_spec, outs, o_specs, acc_shape, extras=(), extra_specs=(), epi=None,
        riders=()):
    ni, nj, nk = grid
    n_ex, n_out = len(extras), len(outs)
    n_in = 2 + n_ex
    rs = _Riders(riders, n_in, n_out)
    n_rin, n_rout = len(rs.arrays), len(rs.out_shapes)
    steps = ni * nj * nk

    def body(*refs):
        a_ref, b_ref = refs[0], refs[1]
        ex_refs = refs[2:n_in]
        o_refs = refs[n_in + n_rin:n_in + n_rin + n_out]
        acc_ref = refs[n_in + n_rin + n_out + n_rout]
        k = pl.program_id(2)
        if riders:
            bound = rs.bind(refs[n_in:n_in + n_rin], refs[n_in + n_rin + n_out:n_in + n_rin + n_out + n_rout],
                            refs[n_in + n_rin + n_out + n_rout + 1:])
            step = (pl.program_id(0) * nj + pl.program_id(1)) * nk + k
            pl.when(step == 0)(lambda: rs.run("start", bound))
            pl.when(step == int(steps * RIDER_MID))(lambda: rs.run("mid", bound))
            pl.when(step == int(steps * RIDER_LATE))(lambda: rs.run("late", bound))

        def finish(acc):
            vals = epi(acc, *[r[...] for r in ex_refs]) if epi is not None else (acc,)
            for r, v in zip(o_refs, vals):
                r[...] = v.astype(r.dtype)

        if nk == 1:
            finish(_dot(a_ref[...], b_ref[...], dims))
        else:
            @pl.when(k == 0)
            def _():
                acc_ref[...] = jnp.zeros_like(acc_ref)

            acc_ref[...] += _dot(a_ref[...], b_ref[...], dims)
            pl.when(k == nk - 1)(lambda: finish(acc_ref[...]))

        if riders:
            pl.when(step == steps - 1)(lambda: rs.run("end", bound))

    res = _pallas(
        body, name=name, grid=grid, in_specs=[a_spec, b_spec, *extra_specs, *rs.in_specs],
        out_specs=[*o_specs, *rs.out_specs], out_shape=[*outs, *rs.out_shapes], input_output_aliases=rs.aliases,
        scratch_shapes=[pltpu.VMEM(acc_shape if nk > 1 else (8, 128), F32), *rs.scratch],
        compiler_params=_params(*(("arbitrary",) * 3 if riders else ("parallel", "parallel", "arbitrary"))),
    )(a, b, *extras, *rs.arrays)
    return (res[:n_out], rs.split(res[n_out:])) if riders else res


def _sds(shape, dtype):
    return jax.ShapeDtypeStruct(shape, dtype)


def _cast_bf16(w, k_arr, name):
    r, c = w.shape
    tr = min(r, 256)

    def body(k_ref, w_ref, o_ref):
        o_ref[...] = w_ref[...].astype(BF16)

    grid_spec = pltpu.PrefetchScalarGridSpec(
        num_scalar_prefetch=1, grid=(r // tr,), in_specs=[pl.BlockSpec((tr, c), lambda i, k_ref: (i, 0))],
        out_specs=pl.BlockSpec((None, tr, c), lambda i, k_ref: (k_ref[0], i, 0)))
    return _pallas(body, name=name, grid_spec=grid_spec, out_shape=_sds((N_CHIPS, r, c), BF16),
                   compiler_params=_params("parallel"))(k_arr, w)


def _rmsnorm_fwd(x, g, name):
    s, d = x.shape
    tr = 256

    def body(x_ref, g_ref, o_ref):
        xv = x_ref[...]
        y = xv * lax.rsqrt(jnp.mean(xv * xv, axis=-1, keepdims=True) + EPS)
        o_ref[...] = (y * g_ref[...]).astype(o_ref.dtype)

    return _pallas(body, name=name, grid=(s // tr,),
                   in_specs=[pl.BlockSpec((tr, d), lambda i: (i, 0)), pl.BlockSpec((1, d), lambda i: (0, 0))],
                   out_specs=pl.BlockSpec((tr, d), lambda i: (i, 0)), out_shape=_sds((s, d), BF16),
                   compiler_params=_params("parallel"))(x, g)


def _rmsnorm_bwd(x, g, dh, res, name, riders=()):
    s, d = x.shape
    tr = 256

    def body(x_ref, g_ref, dh_ref, res_ref, dx_ref, dxb_ref, dg_ref):
        i = pl.program_id(0)
        xv = x_ref[...]
        rstd = lax.rsqrt(jnp.mean(xv * xv, axis=-1, keepdims=True) + EPS)
        xh = xv * rstd
        dhv = dh_ref[...]

        @pl.when(i == 0)
        def _():
            dg_ref[...] = jnp.zeros_like(dg_ref)

        dg_ref[...] += jnp.sum(dhv * xh, axis=0, keepdims=True)
        dxh = dhv * g_ref[...]
        dx = res_ref[...] + rstd * (dxh - xh * jnp.mean(dxh * xh, axis=-1, keepdims=True))
        dx_ref[...] = dx
        dxb_ref[...] = dx.astype(BF16)

    row = pl.BlockSpec((tr, d), lambda i: (i, 0))
    vec = pl.BlockSpec((1, d), lambda i: (0, 0))
    rs = _Riders(riders, 4, 3)
    out = _pallas(_with_riders(body, 4, 3, 0, rs, (s // tr,)), name=name, grid=(s // tr,),
                  in_specs=[row, vec, row, row, *rs.in_specs], out_specs=[row, row, vec, *rs.out_specs],
                  out_shape=[_sds((s, d), F32), _sds((s, d), BF16), _sds((1, d), F32), *rs.out_shapes],
                  input_output_aliases=rs.aliases, scratch_shapes=rs.scratch,
                  compiler_params=_params("arbitrary"))(x, g, dh, res, *rs.arrays)
    return (out[:3], rs.split(out[3:])) if riders else out


def _adamw_math(w, g, m, v):
    m = ADAM_B1 * m + (1.0 - ADAM_B1) * g
    v = ADAM_B2 * v + (1.0 - ADAM_B2) * (g * g)
    m_hat = m / (1.0 - ADAM_B1 ** ADAM_STEP)
    v_hat = v / (1.0 - ADAM_B2 ** ADAM_STEP)
    delta = -ADAM_LR * (m_hat / (jnp.sqrt(v_hat) + ADAM_EPS) + ADAM_WD * w)
    return delta, m, v


def _adamw(w, g, m, v, name):
    r, c = w.shape
    tr = 128

    def body(w_ref, g_ref, m_ref, v_ref, d_ref, nm_ref, nv_ref):
        d_ref[...], nm_ref[...], nv_ref[...] = _adamw_math(w_ref[...], g_ref[...], m_ref[...], v_ref[...])

    blk = pl.BlockSpec((tr, c), lambda i: (i, 0))
    return _pallas(body, name=name, grid=(r // tr,), in_specs=[blk] * 4, out_specs=[blk] * 3,
                   out_shape=[_sds((r, c), F32)] * 3, compiler_params=_params("parallel"))(w, g, m, v)


def _tables(s):
    half = HEAD_DIM // 2
    pos = jnp.arange(s, dtype=F32)
    inv_freq = ROPE_BASE ** (-jnp.arange(half, dtype=F32) / half)
    ang = pos[:, None] * inv_freq[None, :]
    cos, sin = jnp.cos(ang), jnp.sin(ang)
    cos_f = jnp.concatenate([cos, cos], axis=-1)
    sin_f = jnp.concatenate([-sin, sin], axis=-1)
    log_g = jnp.log1p(-jnp.exp2(-(5.0 + jnp.arange(HEADS, dtype=F32))))
    p = jnp.arange(CHUNK, dtype=F32)
    decay = jnp.exp(log_g[:, None, None] * jnp.abs(p[:, None] - p[None, :]))
    k_dec = jnp.exp(log_g[None, :] * (CHUNK - 1.0 - p)[:, None])
    q_dec = jnp.exp(log_g[None, :] * (p + 1.0)[:, None])
    c_dec = jnp.exp(log_g * CHUNK)
    k_dec = jnp.tile(jnp.broadcast_to(k_dec.T[:, :, None], (HEADS, CHUNK, HEAD_DIM)), (1, RET_BLOCK_CHUNKS, 1))
    q_dec = jnp.tile(jnp.broadcast_to(q_dec.T[:, :, None], (HEADS, CHUNK, HEAD_DIM)), (1, RET_BLOCK_CHUNKS, 1))
    c_dec = jnp.broadcast_to(c_dec[:, None, None], (HEADS, 1, HEAD_DIM))
    n = RET_SUB // CHUNK
    decay = (jnp.eye(n, dtype=F32)[None, :, None, :, None] * decay[:, None, :, None, :]).reshape(HEADS, RET_SUB, RET_SUB)
    return cos_f, sin_f, decay, k_dec, q_dec, c_dec


def _rot(x, cos_f, sin_f):
    return x * cos_f + pltpu.roll(x, HEAD_DIM // 2, 1) * sin_f


def _rot_bwd(d, cos_f, sin_f):
    return d * cos_f + pltpu.roll(d * sin_f, HEAD_DIM // 2, 1)


K_SCALE = HEAD_DIM ** -0.5


def _retention_fwd(proj, gn_g, tables, riders=()):
    s = proj.shape[0]
    nb = s // RET_ROWS
    nc = s // CHUNK
    cos_f, sin_f, decay, k_dec, q_dec, c_dec = tables

    def body(q_ref, k_ref, v_ref, g_ref, cos_ref, sin_ref, dec_ref, kd_ref, qd_ref, cd_ref, gn_ref,
             ret_ref, y_ref, prev_ref, state_ref):
        @pl.when(pl.program_id(1) == 0)
        def _():
            state_ref[...] = jnp.zeros_like(state_ref)

        cosv, sinv = cos_ref[...], sin_ref[...]
        q = _rot(q_ref[...], cosv, sinv)
        k = _rot(k_ref[...], cosv, sinv) * K_SCALE
        v = v_ref[...]
        rg = g_ref[...]
        dec, cd, gn = dec_ref[...], cd_ref[...], gn_ref[...]
        kdf, qdf = k * kd_ref[...], q * qd_ref[...]
        chunks = [slice(c * CHUNK, (c + 1) * CHUNK) for c in range(RET_BLOCK_CHUNKS)]
        contribs = [_dot(kdf[rows], v[rows], TN) for rows in chunks]
        state, states = state_ref[...], []
        for c in range(RET_BLOCK_CHUNKS):
            states.append(state)
            prev_ref[c] = state.astype(prev_ref.dtype)
            state = cd * state + contribs[c]
        state_ref[...] = state
        cross = jnp.concatenate([_dot(qdf[rows], st, NN) for rows, st in zip(chunks, states)], axis=0)
        intra = []
        for b in range(RET_ROWS // RET_SUB):
            rows = slice(b * RET_SUB, (b + 1) * RET_SUB)
            intra.append(_dot(_dot(q[rows], k[rows], NT) * dec, v[rows], NN))
        y = jnp.concatenate(intra, axis=0) + cross
        y_ref[...] = y
        mu = jnp.mean(y, axis=-1, keepdims=True)
        yc = y - mu
        var = jnp.mean(yc * yc, axis=-1, keepdims=True)
        yn = yc * lax.rsqrt(var + GN_EPS) * gn
        ret_ref[...] = (rg * jax.nn.sigmoid(rg) * yn).astype(ret_ref.dtype)

    def col(off):
        return pl.BlockSpec((RET_ROWS, HEAD_DIM), lambda h, i: (i, off + h))

    pos = pl.BlockSpec((RET_ROWS, HEAD_DIM), lambda h, i: (i, 0))
    per_head = lambda shape: pl.BlockSpec((None, *shape), lambda h, i: (h, 0, 0))
    rs = _Riders(riders, 11, 3)
    res = _pallas(
        _with_riders(body, 11, 3, 1, rs, (HEADS, nb)), name="retention_fwd", grid=(HEADS, nb),
        in_specs=[col(0), col(HEADS), col(2 * HEADS), col(3 * HEADS), pos, pos,
                  per_head((RET_SUB, RET_SUB)), per_head((RET_ROWS, HEAD_DIM)), per_head((RET_ROWS, HEAD_DIM)),
                  per_head((1, HEAD_DIM)), pl.BlockSpec((1, HEAD_DIM), lambda h, i: (0, h)), *rs.in_specs],
        out_specs=[col(0), col(0),
                   pl.BlockSpec((None, RET_BLOCK_CHUNKS, HEAD_DIM, HEAD_DIM), lambda h, i: (h, i, 0, 0)),
                   *rs.out_specs],
        out_shape=[_sds((s, 2 * HEADS * HEAD_DIM), BF16), _sds((s, HEADS * HEAD_DIM), F32),
                   _sds((HEADS, nc, HEAD_DIM, HEAD_DIM), MXU_DTYPE), *rs.out_shapes],
        input_output_aliases=rs.aliases,
        scratch_shapes=[pltpu.VMEM((HEAD_DIM, HEAD_DIM), F32), *rs.scratch],
        compiler_params=_params("arbitrary", "arbitrary"),
    )(proj, proj, proj, proj, cos_f, sin_f, decay, k_dec, q_dec, c_dec, gn_g, *rs.arrays)
    return res[:3], rs.split(res[3:])


def _retention_bwd(proj, gn_g, tables, y, prev, dmix, riders=()):
    s = proj.shape[0]
    nb = s // RET_ROWS
    cos_f, sin_f, decay, k_dec, q_dec, c_dec = tables

    def body(q_ref, k_ref, v_ref, g_ref, cos_ref, sin_ref, dec_ref, kd_ref, qd_ref, cd_ref, gn_ref,
             y_ref, prev_ref, dret_ref, dq_ref, dk_ref, dv_ref, dg_ref, dgn_ref, gstate_ref):
        @pl.when(pl.program_id(1) == 0)
        def _():
            gstate_ref[...] = jnp.zeros_like(gstate_ref)
            dgn_ref[...] = jnp.zeros_like(dgn_ref)

        cosv, sinv = cos_ref[...], sin_ref[...]
        q = _rot(q_ref[...], cosv, sinv)
        k = _rot(k_ref[...], cosv, sinv) * K_SCALE
        v = v_ref[...]
        dec, kd, qd, cd, gn = dec_ref[...], kd_ref[...], qd_ref[...], cd_ref[...], gn_ref[...]
        kdf, qdf = k * kd, q * qd
        rg = g_ref[...]
        yv = y_ref[...]
        dret = dret_ref[...]
        sig = jax.nn.sigmoid(rg)
        gate = rg * sig
        mu = jnp.mean(yv, axis=-1, keepdims=True)
        yc = yv - mu
        rstd = lax.rsqrt(jnp.mean(yc * yc, axis=-1, keepdims=True) + GN_EPS)
        z = yc * rstd
        dyn = dret * gate
        dg_ref[...] = (dret * (z * gn) * (sig * (1.0 + rg * (1.0 - sig)))).astype(dg_ref.dtype)
        dgn_ref[...] += jnp.sum(dyn * z, axis=0, keepdims=True)
        dz = dyn * gn
        dy = rstd * (dz - jnp.mean(dz, axis=-1, keepdims=True) - z * jnp.mean(dz * z, axis=-1, keepdims=True))
        chunks = [slice(c * CHUNK, (c + 1) * CHUNK) for c in range(RET_BLOCK_CHUNKS)]
        dprevs = [_dot(qdf[rows], dy[rows], TN) for rows in chunks]
        gst, gsts = gstate_ref[...], [None] * RET_BLOCK_CHUNKS
        for c in reversed(range(RET_BLOCK_CHUNKS)):
            gsts[c] = gst
            gst = dprevs[c] + cd * gst
        gstate_ref[...] = gst
        dq = jnp.concatenate([_dot(dy[rows], prev_ref[c], NT) for c, rows in enumerate(chunks)], axis=0) * qd
        dk = jnp.concatenate([_dot(v[rows], g, NT) for rows, g in zip(chunks, gsts)], axis=0) * kd
        dv = jnp.concatenate([_dot(kdf[rows], g, NN) for rows, g in zip(chunks, gsts)], axis=0)
        dqi, dki, dvi = [], [], []
        for b in range(RET_ROWS // RET_SUB):
            rows = slice(b * RET_SUB, (b + 1) * RET_SUB)
            qs, ks, vs, dys = q[rows], k[rows], v[rows], dy[rows]
            dvi.append(_dot(_dot(ks, qs, NT) * dec, dys, NN))
            dqi.append(_dot(_dot(dys, vs, NT) * dec, ks, NN))
            dki.append(_dot(_dot(vs, dys, NT) * dec, qs, NN))
        dq = dq + jnp.concatenate(dqi, axis=0)
        dk = dk + jnp.concatenate(dki, axis=0)
        dv = dv + jnp.concatenate(dvi, axis=0)
        dq_ref[...] = _rot_bwd(dq, cosv, sinv).astype(dq_ref.dtype)
        dk_ref[...] = _rot_bwd(dk * K_SCALE, cosv, sinv).astype(dk_ref.dtype)
        dv_ref[...] = dv.astype(dv_ref.dtype)

    rev = lambda i: nb - 1 - i

    def col(off):
        return pl.BlockSpec((RET_ROWS, HEAD_DIM), lambda h, i: (rev(i), off + h))

    pos = pl.BlockSpec((RET_ROWS, HEAD_DIM), lambda h, i: (rev(i), 0))
    per_head = lambda shape: pl.BlockSpec((None, *shape), lambda h, i: (h, 0, 0))
    outb = _sds((s, HEADS * HEAD_DIM), BF16)
    rs = _Riders(riders, 14, 5)
    res = _pallas(
        _with_riders(body, 14, 5, 1, rs, (HEADS, nb)), name="retention_bwd", grid=(HEADS, nb),
        in_specs=[col(0), col(HEADS), col(2 * HEADS), col(3 * HEADS), pos, pos,
                  per_head((RET_SUB, RET_SUB)), per_head((RET_ROWS, HEAD_DIM)), per_head((RET_ROWS, HEAD_DIM)),
                  per_head((1, HEAD_DIM)), pl.BlockSpec((1, HEAD_DIM), lambda h, i: (0, h)),
                  col(0), pl.BlockSpec((None, RET_BLOCK_CHUNKS, HEAD_DIM, HEAD_DIM), lambda h, i: (h, rev(i), 0, 0)),
                  col(0), *rs.in_specs],
        out_specs=[col(0), col(0), col(0), col(0), per_head((1, HEAD_DIM)), *rs.out_specs],
        out_shape=[outb, outb, outb, outb, _sds((HEADS, 1, HEAD_DIM), F32), *rs.out_shapes],
        input_output_aliases=rs.aliases,
        scratch_shapes=[pltpu.VMEM((HEAD_DIM, HEAD_DIM), F32), *rs.scratch],
        compiler_params=_params("arbitrary", "arbitrary"),
    )(proj, proj, proj, proj, cos_f, sin_f, decay, k_dec, q_dec, c_dec, gn_g, y, prev, dmix, *rs.arrays)
    return res[:5], rs.split(res[5:])


ATT_COL0 = 4 * HEADS
PAD_ROWS = LEFT_CHUNKS * CHUNK
NORM_ROWS = 512
GROUP_CHUNKS = 4
GROUP = GROUP_CHUNKS * CHUNK
WIN = (LEFT_CHUNKS + GROUP_CHUNKS) * CHUNK
MASKED = -1e30


def _qk_norm(x, g):
    return x * lax.rsqrt(jnp.mean(x * x, axis=-1, keepdims=True) + EPS) * g


def _band_probs(qb, kb, bias, g):
    sc = _dot(qb, kb, NT) * K_SCALE + bias
    win_chunk = lax.broadcasted_iota(jnp.int32, (GROUP, WIN), 1) // CHUNK
    sc = jnp.where(g * GROUP_CHUNKS - LEFT_CHUNKS + win_chunk >= 0, sc, MASKED)
    e = jnp.exp(sc - jnp.max(sc, axis=-1, keepdims=True))
    return e / jnp.sum(e, axis=-1, keepdims=True)


def _with_riders(core, n_in, n_out, n_scratch, rs, grid):
    n_rin, n_rout = len(rs.arrays), len(rs.out_shapes)
    if not rs.riders:
        return core
    steps = 1
    for n in grid:
        steps *= n

    def body(*refs):
        outs_at = n_in + n_rin
        scratch_at = outs_at + n_out + n_rout
        bound = rs.bind(refs[n_in:outs_at], refs[outs_at + n_out:scratch_at], refs[scratch_at + n_scratch:])
        step = 0
        for axis, n in enumerate(grid):
            step = step * n + pl.program_id(axis)
        pl.when(step == 0)(lambda: rs.run("start", bound))
        pl.when(step == int(steps * RIDER_MID))(lambda: rs.run("mid", bound))
        pl.when(step == int(steps * RIDER_LATE))(lambda: rs.run("late", bound))
        core(*refs[:n_in], *refs[outs_at:outs_at + n_out], *refs[scratch_at:scratch_at + n_scratch])
        pl.when(step == steps - 1)(lambda: rs.run("end", bound))

    return body


def _attention_fwd(proj, gq, gk, bias, mix, riders=()):
    s = proj.shape[0]
    rs = _Riders(riders, 7, 1)

    def body(q_ref, k_ref, v_ref, gq_ref, gk_ref, bias_ref, mix_ref, o_ref, kp_ref, vp_ref):
        kp_ref[0:PAD_ROWS, :] = jnp.zeros((PAD_ROWS, HEAD_DIM), kp_ref.dtype)
        vp_ref[0:PAD_ROWS, :] = jnp.zeros((PAD_ROWS, HEAD_DIM), vp_ref.dtype)
        gqv, gkv = gq_ref[...], gk_ref[...]

        def fill(b, carry):
            r0 = pl.multiple_of(b * NORM_ROWS, NORM_ROWS)
            kp_ref[pl.ds(PAD_ROWS + r0, NORM_ROWS), :] = _qk_norm(k_ref[pl.ds(r0, NORM_ROWS), :], gkv).astype(kp_ref.dtype)
            vp_ref[pl.ds(PAD_ROWS + r0, NORM_ROWS), :] = v_ref[pl.ds(r0, NORM_ROWS), :].astype(vp_ref.dtype)
            return carry

        lax.fori_loop(0, s // NORM_ROWS, fill, 0)

        def group(g, carry):
            r0 = pl.multiple_of(g * GROUP, GROUP)
            qn = _qk_norm(q_ref[pl.ds(r0, GROUP), :], gqv)
            p = _band_probs(qn, kp_ref[pl.ds(r0, WIN), :], bias_ref[...], g)
            o_ref[pl.ds(r0, GROUP), :] = _dot(p, vp_ref[pl.ds(r0, WIN), :], NN).astype(o_ref.dtype)
            return carry

        lax.fori_loop(0, s // GROUP, group, 0, unroll=2)

    def col(off):
        return pl.BlockSpec((s, HEAD_DIM), lambda h: (0, off + h))

    vec = pl.BlockSpec((1, HEAD_DIM), lambda h: (0, 0))
    res = _pallas(
        _with_riders(body, 7, 1, 2, rs, (HEADS,)), name="attention_fwd", grid=(HEADS,),
        in_specs=[col(ATT_COL0), col(ATT_COL0 + HEADS), col(ATT_COL0 + 2 * HEADS), vec, vec,
                  pl.BlockSpec((None, GROUP, WIN), lambda h: (h, 0, 0)), ANY, *rs.in_specs],
        out_specs=[col(HEADS), *rs.out_specs], out_shape=[_sds(mix.shape, mix.dtype), *rs.out_shapes],
        input_output_aliases={6: 0, **rs.aliases},
        scratch_shapes=[pltpu.VMEM((s + PAD_ROWS, HEAD_DIM), MXU_DTYPE), pltpu.VMEM((s + PAD_ROWS, HEAD_DIM), MXU_DTYPE),
                        *rs.scratch],
        compiler_params=_params("arbitrary"),
    )(proj, proj, proj, gq, gk, bias, mix, *rs.arrays)
    return res[0], rs.split(res[1:])


def _attention_bwd(proj, gq, gk, bias, dmix, riders=()):
    s = proj.shape[0]
    nc = s // CHUNK
    rs = _Riders(riders, 7, 6)

    def body(q_ref, k_ref, v_ref, gq_ref, gk_ref, bias_ref, do_ref,
             dq_ref, dk_ref, dv_ref, dgq_ref, dgk_ref, dbias_ref, kp_ref, vp_ref, dkp_ref, dvp_ref, dqn_ref):
        kp_ref[0:PAD_ROWS, :] = jnp.zeros((PAD_ROWS, HEAD_DIM), kp_ref.dtype)
        vp_ref[0:PAD_ROWS, :] = jnp.zeros((PAD_ROWS, HEAD_DIM), vp_ref.dtype)
        dkp_ref[...] = jnp.zeros_like(dkp_ref)
        dvp_ref[...] = jnp.zeros_like(dvp_ref)
        dbias_ref[...] = jnp.zeros_like(dbias_ref)
        gqv, gkv = gq_ref[...], gk_ref[...]

        def fill(b, carry):
            r0 = pl.multiple_of(b * NORM_ROWS, NORM_ROWS)
            kp_ref[pl.ds(PAD_ROWS + r0, NORM_ROWS), :] = _qk_norm(k_ref[pl.ds(r0, NORM_ROWS), :], gkv).astype(kp_ref.dtype)
            vp_ref[pl.ds(PAD_ROWS + r0, NORM_ROWS), :] = v_ref[pl.ds(r0, NORM_ROWS), :].astype(vp_ref.dtype)
            return carry

        lax.fori_loop(0, s // NORM_ROWS, fill, 0)

        def group(g, carry):
            r0 = pl.multiple_of(g * GROUP, GROUP)
            qn = _qk_norm(q_ref[pl.ds(r0, GROUP), :], gqv)
            kb = kp_ref[pl.ds(r0, WIN), :]
            vb = vp_ref[pl.ds(r0, WIN), :]
            p = _band_probs(qn, kb, bias_ref[...], g)
            do = do_ref[pl.ds(r0, GROUP), :]
            dvp_ref[pl.ds(r0, WIN), :] += _dot(p, do, TN)
            dp = _dot(do, vb, NT)
            ds = p * (dp - jnp.sum(dp * p, axis=-1, keepdims=True))
            dbias_ref[...] += ds
            dss = ds * K_SCALE
            dqn_ref[pl.ds(r0, GROUP), :] = _dot(dss, kb, NN)
            dkp_ref[pl.ds(r0, WIN), :] += _dot(dss, qn, TN)
            return carry

        lax.fori_loop(0, s // GROUP, group, 0, unroll=2)

        @pl.when(pl.program_id(0) == 0)
        def _():
            dgq_ref[...] = jnp.zeros_like(dgq_ref)
            dgk_ref[...] = jnp.zeros_like(dgk_ref)

        def norm_bwd(x, g, dn):
            rstd = lax.rsqrt(jnp.mean(x * x, axis=-1, keepdims=True) + EPS)
            xh = x * rstd
            dxh = dn * g
            return rstd * (dxh - xh * jnp.mean(dxh * xh, axis=-1, keepdims=True)), jnp.sum(dn * xh, axis=0, keepdims=True)

        def finish(b, carry):
            r0 = pl.multiple_of(b * NORM_ROWS, NORM_ROWS)
            rows = pl.ds(r0, NORM_ROWS)
            dq, dgq = norm_bwd(q_ref[rows, :], gqv, dqn_ref[rows, :])
            dk, dgk = norm_bwd(k_ref[rows, :], gkv, dkp_ref[pl.ds(PAD_ROWS + r0, NORM_ROWS), :])
            dq_ref[rows, :] = dq.astype(dq_ref.dtype)
            dk_ref[rows, :] = dk.astype(dk_ref.dtype)
            dv_ref[rows, :] = dvp_ref[pl.ds(PAD_ROWS + r0, NORM_ROWS), :].astype(dv_ref.dtype)
            dgq_ref[...] += dgq
            dgk_ref[...] += dgk
            return carry

        lax.fori_loop(0, s // NORM_ROWS, finish, 0)

    def col(off):
        return pl.BlockSpec((s, HEAD_DIM), lambda h: (0, off + h))

    vec = pl.BlockSpec((1, HEAD_DIM), lambda h: (0, 0))
    hbias = pl.BlockSpec((None, GROUP, WIN), lambda h: (h, 0, 0))
    outb = _sds((s, HEADS * HEAD_DIM), BF16)
    res = _pallas(
        _with_riders(body, 7, 6, 5, rs, (HEADS,)), name="attention_bwd", grid=(HEADS,),
        in_specs=[col(ATT_COL0), col(ATT_COL0 + HEADS), col(ATT_COL0 + 2 * HEADS), vec, vec, hbias, col(HEADS),
                  *rs.in_specs],
        out_specs=[col(0), col(0), col(0), vec, vec, hbias, *rs.out_specs],
        out_shape=[outb, outb, outb, _sds((1, HEAD_DIM), F32), _sds((1, HEAD_DIM), F32),
                   _sds((HEADS, GROUP, WIN), F32), *rs.out_shapes],
        input_output_aliases=rs.aliases,
        scratch_shapes=[pltpu.VMEM((s + PAD_ROWS, HEAD_DIM), MXU_DTYPE), pltpu.VMEM((s + PAD_ROWS, HEAD_DIM), MXU_DTYPE),
                        pltpu.VMEM((s + PAD_ROWS, HEAD_DIM), F32), pltpu.VMEM((s + PAD_ROWS, HEAD_DIM), F32),
                        pltpu.VMEM((s, HEAD_DIM), F32), *rs.scratch],
        compiler_params=_params("arbitrary"),
    )(proj, proj, proj, gq, gk, bias, dmix, *rs.arrays)
    return res[:6], rs.split(res[6:])


DIAG_SPLIT = (BAND + WIN - CHUNK) // 2


def _diag_bin(m):
    t = jnp.where(m < DIAG_SPLIT, m, m - WIN)
    return jnp.clip(LEFT_CHUNKS * CHUNK - t, -(CHUNK - 1), REL_CLIP) + (CHUNK - 1)


def _skew_rows(a, left):
    row = lax.broadcasted_iota(jnp.int32, (GROUP, WIN), 0)
    for b in range(GROUP.bit_length() - 1):
        step = 1 << b
        a = jnp.where(jnp.bitwise_and(row, step) != 0, pltpu.roll(a, WIN - step if left else step, 1), a)
    return a


def _rel_bias_expand(rel_bias):
    def body(rb_ref, o_ref):
        h = pl.program_id(0)
        bins = _diag_bin(lax.broadcasted_iota(jnp.int32, (8, WIN), 1))
        per_diag = lax.fori_loop(0, REL_SIZE, lambda r, acc: jnp.where(bins == r, rb_ref[h, r], acc),
                                 jnp.zeros((8, WIN), F32))
        table = _skew_rows(jnp.broadcast_to(per_diag[0:1], (GROUP, WIN)), left=False)
        row_chunk = lax.broadcasted_iota(jnp.int32, (GROUP, WIN), 0) // CHUNK
        col_chunk = lax.broadcasted_iota(jnp.int32, (GROUP, WIN), 1) // CHUNK
        in_band = jnp.logical_and(col_chunk >= row_chunk, col_chunk <= row_chunk + LEFT_CHUNKS)
        o_ref[...] = jnp.where(in_band, table, MASKED)

    return _pallas(body, name="rel_bias_expand", grid=(HEADS,), in_specs=[pl.BlockSpec(memory_space=pltpu.SMEM)],
                   out_specs=pl.BlockSpec((None, GROUP, WIN), lambda h: (h, 0, 0)),
                   out_shape=_sds((HEADS, GROUP, WIN), F32), compiler_params=_params("parallel"))(rel_bias)


def _rel_bias_fold(dbias):
    def body(a_ref, o_ref):
        diag = jnp.sum(_skew_rows(a_ref[...], left=True), axis=0, keepdims=True)
        onehot = (_diag_bin(lax.broadcasted_iota(jnp.int32, (WIN, REL_SIZE), 0))
                  == lax.broadcasted_iota(jnp.int32, (WIN, REL_SIZE), 1)).astype(MXU_DTYPE)
        rest = jnp.broadcast_to(diag, (8, WIN))
        out = jnp.zeros((8, REL_SIZE), F32)
        for _ in range(3):
            piece = rest.astype(BF16)
            out = out + _dot(piece, onehot, NN)
            rest = rest - piece.astype(F32)
        o_ref[...] = out[0:1]

    return _pallas(body, name="rel_bias_fold", grid=(HEADS,),
                   in_specs=[pl.BlockSpec((None, GROUP, WIN), lambda h: (h, 0, 0))],
                   out_specs=pl.BlockSpec((None, 1, REL_SIZE), lambda h: (h, 0, 0)),
                   out_shape=_sds((HEADS, 1, REL_SIZE), F32), compiler_params=_params("parallel"))(dbias)


def _place():
    return lax.axis_index("x"), lax.axis_index("y"), lax.axis_index("c")


def _other_chips(x, y):
    return [(1 - x, y), (x, 1 - y), (1 - x, 1 - y)]


class _Rider:
    reads, ins, new, n_sems = (), (), (), 1

    def start(self, reads, ins, new, send, recv):
        pass

    def mid(self, reads, ins, new, send, recv):
        pass

    def late(self, reads, ins, new, send, recv):
        pass

    def end(self, reads, ins, new, send, recv):
        pass


class _Riders:
    def __init__(self, riders, n_host_in, n_host_out):
        self.riders = list(riders)
        self.arrays, self.out_shapes, self.aliases, self.scratch = [], [], {}, []
        for r in self.riders:
            for t, a in enumerate(r.ins):
                self.aliases[n_host_in + len(self.arrays) + len(r.reads) + t] = n_host_out + len(self.out_shapes) + t
            self.arrays += [*r.reads, *r.ins]
            self.out_shapes += [_sds(a.shape, a.dtype) for a in r.ins] + list(r.new)
            self.scratch += [pltpu.SemaphoreType.DMA((r.n_sems,)), pltpu.SemaphoreType.DMA((r.n_sems,))]
        self.in_specs = [ANY] * len(self.arrays)
        self.out_specs = [ANY] * len(self.out_shapes)

    def bind(self, in_refs, out_refs, scratch_refs):
        bound, i, o = [], 0, 0
        for t, r in enumerate(self.riders):
            reads = in_refs[i:i + len(r.reads)]
            i += len(r.reads) + len(r.ins)
            ins = out_refs[o:o + len(r.ins)]
            new = out_refs[o + len(r.ins):o + len(r.ins) + len(r.new)]
            o += len(r.ins) + len(r.new)
            bound.append((reads, ins, new, scratch_refs[2 * t], scratch_refs[2 * t + 1]))
        return bound

    def run(self, phase, bound):
        for r, b in zip(self.riders, bound):
            getattr(r, phase)(*b)

    def split(self, outs):
        res, o = [], 0
        for r in self.riders:
            n = len(r.ins) + len(r.new)
            res.append(list(outs[o:o + n]))
            o += n
        return res


def _run_riders(name, riders):
    rs = _Riders(riders, 0, 0)
    n_in, n_out = len(rs.arrays), len(rs.out_shapes)

    def body(*refs):
        bound = rs.bind(refs[:n_in], refs[n_in:n_in + n_out], refs[n_in + n_out:])
        rs.run("start", bound)
        rs.run("mid", bound)
        rs.run("late", bound)
        rs.run("end", bound)

    outs = _pallas(body, name=name, in_specs=rs.in_specs, out_specs=rs.out_specs, out_shape=rs.out_shapes,
                   input_output_aliases=rs.aliases, scratch_shapes=rs.scratch)(*rs.arrays)
    return rs.split(outs)


class _GatherRider(_Rider):
    X_LINK, Y_LINK, Y_PASS, X_PASS, D2D_X, D2D_Y, D2D_DIAG, N_SEMS = 0, 1, 2, 3, 4, 5, 6, 7

    def __init__(self, blocks, part=(0, 1, 1)):
        self.ins = tuple(blocks)
        self.part = part
        self.n_sems = self.N_SEMS * len(blocks)

    def _copy(self, out, send, recv, w, sem, chip_from, cc, to, sub=None):
        hr = self.ins[w].shape[1] // 2
        lo, hi, n = self.part
        first, size = cc * hr + lo * (hr // n), (hi - lo) * (hr // n)
        if sub is not None:
            size //= 2
            first += sub * size
        piece = out[w].at[2 * chip_from[0] + chip_from[1], pl.ds(first, size), :]
        return pltpu.make_async_remote_copy(src_ref=piece, dst_ref=piece, send_sem=send.at[self.N_SEMS * w + sem],
                                            recv_sem=recv.at[self.N_SEMS * w + sem], device_id=to, device_id_type=MESH)

    def _sent(self, out, send, recv, w):
        x, y, c = _place()
        me, sib = (x, y), (x, y, 1 - c)
        xn, yn, diag = _other_chips(x, y)
        cp = functools.partial(self._copy, out, send, recv, w)
        return [("start", cp(self.X_LINK, me, c, (*xn, c))), ("start", cp(self.Y_LINK, me, c, (*yn, c))),
                ("mid_x", cp(self.D2D_X, xn, c, sib)), ("mid_x", cp(self.Y_PASS, xn, c, (*yn, c), sub=0)),
                ("mid_y", cp(self.D2D_Y, yn, c, sib)), ("mid_y", cp(self.X_PASS, yn, c, (*xn, c), sub=1)),
                ("late", cp(self.D2D_DIAG, diag, c, sib))]

    def _go(self, out, send, recv, phase):
        for w in range(len(self.ins)):
            for ph, copy in self._sent(out, send, recv, w):
                if ph == phase:
                    copy.start()

    def start(self, reads, out, new, send, recv):
        self._go(out, send, recv, "start")

    def mid(self, reads, out, new, send, recv):
        x, y, c = _place()
        xn, yn, _ = _other_chips(x, y)
        for w in range(len(self.ins)):
            self._copy(out, send, recv, w, self.X_LINK, xn, c, (x, y, c)).wait_recv()
        self._go(out, send, recv, "mid_x")
        for w in range(len(self.ins)):
            self._copy(out, send, recv, w, self.Y_LINK, yn, c, (x, y, c)).wait_recv()
        self._go(out, send, recv, "mid_y")

    def late(self, reads, out, new, send, recv):
        x, y, c = _place()
        diag = _other_chips(x, y)[2]
        for w in range(len(self.ins)):
            self._copy(out, send, recv, w, self.Y_PASS, diag, c, (x, y, c), sub=0).wait_recv()
            self._copy(out, send, recv, w, self.X_PASS, diag, c, (x, y, c), sub=1).wait_recv()
        self._go(out, send, recv, "late")

    def end(self, reads, out, new, send, recv):
        x, y, c = _place()
        xn, yn, diag = _other_chips(x, y)
        for w in range(len(self.ins)):
            for sem, chip in ((self.D2D_X, xn), (self.D2D_Y, yn), (self.D2D_DIAG, diag)):
                self._copy(out, send, recv, w, sem, chip, 1 - c, (x, y, c)).wait_recv()
        for w in range(len(self.ins)):
            for _, copy in self._sent(out, send, recv, w):
                copy.wait_send()


class _SwapRider(_Rider):
    def __init__(self, grads):
        self.reads = tuple(grads)
        self.new = tuple(_sds((N_CHIPS, g.shape[1] // 2, g.shape[2]), g.dtype) for g in grads)
        self.n_sems = len(grads)

    def _copies(self, src, new, send, recv):
        x, y, c = _place()
        copies = []
        for w in range(len(self.reads)):
            hr = self.reads[w].shape[1] // 2
            copies.append(pltpu.make_async_remote_copy(
                src_ref=src[w].at[:, pl.ds((1 - c) * hr, hr), :], dst_ref=new[w],
                send_sem=send.at[w], recv_sem=recv.at[w], device_id=(x, y, 1 - c), device_id_type=MESH))
        return copies

    def start(self, src, ins, new, send, recv):
        for cp in self._copies(src, new, send, recv):
            cp.start()

    def end(self, src, ins, new, send, recv):
        for cp in self._copies(src, new, send, recv):
            cp.wait()


def _add_half(g, got, c_arr, name):
    nk, r, cols = g.shape
    hr = r // 2
    tr = min(hr, 256)
    nb = hr // tr

    def body(c_ref, g_ref, got_ref, o_ref):
        o_ref[...] = (g_ref[...].astype(F32) + got_ref[...].astype(F32)).astype(o_ref.dtype)

    grid_spec = pltpu.PrefetchScalarGridSpec(
        num_scalar_prefetch=1, grid=(nk, nb),
        in_specs=[pl.BlockSpec((None, tr, cols), lambda k, i, c_ref: (k, c_ref[0] * nb + i, 0)),
                  pl.BlockSpec((None, tr, cols), lambda k, i, c_ref: (k, i, 0))],
        out_specs=pl.BlockSpec((None, tr, cols), lambda k, i, c_ref: (k, i, 0)))
    return _pallas(body, name=name, grid_spec=grid_spec, out_shape=_sds((nk, hr, cols), g.dtype),
                   compiler_params=_params("parallel", "parallel"))(c_arr, g, got)


class _SendPartialsRider(_Rider):
    def __init__(self, parts, got=None, part=(0, 1, 1)):
        self.reads = tuple(parts)
        if got is None:
            self.new = tuple(_sds((N_CHIPS - 1, *p.shape[1:]), p.dtype) for p in parts)
        else:
            self.ins = tuple(got)
        self.part = part
        self.n_sems = 3 * len(parts)

    def _copies(self, src, ins, new, send, recv):
        x, y, c = _place()
        land = ins if self.ins else new
        lo, hi, n = self.part
        copies = []
        for w in range(len(self.reads)):
            pr = self.reads[w].shape[1] // n
            rows = pl.ds(lo * pr, (hi - lo) * pr)
            for j, chip in enumerate(_other_chips(x, y)):
                copies.append(pltpu.make_async_remote_copy(
                    src_ref=src[w].at[2 * chip[0] + chip[1], rows, :], dst_ref=land[w].at[j, rows, :],
                    send_sem=send.at[3 * w + j], recv_sem=recv.at[3 * w + j], device_id=(*chip, c), device_id_type=MESH))
        return copies

    def start(self, src, ins, new, send, recv):
        for cp in self._copies(src, ins, new, send, recv):
            cp.start()

    def end(self, src, ins, new, send, recv):
        for cp in self._copies(src, ins, new, send, recv):
            cp.wait()


def _sum_partials(part, got, kc_arr, name):
    _, hr, cols = part.shape
    tr = min(hr, 256)
    nb = hr // tr

    def body(kc_ref, p_ref, g0_ref, g1_ref, g2_ref, o_ref):
        o_ref[...] = ((p_ref[...].astype(F32) + g0_ref[...].astype(F32)) + g1_ref[...].astype(F32)) + g2_ref[...].astype(F32)

    slot = lambda j: pl.BlockSpec((None, tr, cols), lambda i, kc_ref: (j, i, 0))
    grid_spec = pltpu.PrefetchScalarGridSpec(
        num_scalar_prefetch=1, grid=(nb,),
        in_specs=[pl.BlockSpec((None, tr, cols), lambda i, kc_ref: (kc_ref[0], i, 0)), slot(0), slot(1), slot(2)],
        out_specs=pl.BlockSpec((tr, cols), lambda i, kc_ref: (kc_ref[1] * nb + i, 0)))
    return _pallas(body, name=name, grid_spec=grid_spec, out_shape=_sds((2 * hr, cols), F32),
                   compiler_params=_params("parallel"))(kc_arr, part, got, got, got)


class _ShareRider(_Rider):
    def __init__(self, grads):
        self.ins = tuple(grads)
        self.n_sems = len(grads)

    def _copies(self, out, send, recv):
        x, y, c = _place()
        copies = []
        for w in range(len(self.ins)):
            hr = self.ins[w].shape[0] // 2
            mine = out[w].at[pl.ds(c * hr, hr), :]
            copies.append(pltpu.make_async_remote_copy(
                src_ref=mine, dst_ref=mine, send_sem=send.at[w], recv_sem=recv.at[w],
                device_id=(x, y, 1 - c), device_id_type=MESH))
        return copies

    def start(self, reads, out, new, send, recv):
        for cp in self._copies(out, send, recv):
            cp.start()

    def end(self, reads, out, new, send, recv):
        for cp in self._copies(out, send, recv):
            cp.wait()


def _small_allreduce_adamw(g_part, w, m, v):
    rows = g_part.shape[0]

    def body(g_ref, w_ref, m_ref, v_ref, go_ref, d_ref, nm_ref, nv_ref, all_ref, send_sems, recv_sems):
        x, y, c = _place()
        me = 4 * x + 2 * y + c
        all_ref[me] = g_ref[...]
        copies = []
        for r in range(1, 8):
            dx, dy, dc = (r >> 2) & 1, (r >> 1) & 1, r & 1
            peer = (1 - x if dx else x, 1 - y if dy else y, 1 - c if dc else c)
            copies.append(pltpu.make_async_remote_copy(
                src_ref=g_ref, dst_ref=all_ref.at[me], send_sem=send_sems.at[r - 1], recv_sem=recv_sems.at[r - 1],
                device_id=peer, device_id_type=MESH))
        for cp in copies:
            cp.start()
        for cp in copies:
            cp.wait()
        tot = all_ref[0]
        for d in range(1, 8):
            tot = tot + all_ref[d]
        go_ref[...] = tot
        d_ref[...], nm_ref[...], nv_ref[...] = _adamw_math(w_ref[...], tot, m_ref[...], v_ref[...])

    vm = pl.BlockSpec(memory_space=pltpu.VMEM)
    return _pallas(
        body, name="small_allreduce_adamw", in_specs=[vm] * 4, out_specs=[vm] * 4,
        out_shape=[_sds((rows, 128), F32)] * 4,
        scratch_shapes=[pltpu.VMEM((8, rows, 128), F32), pltpu.SemaphoreType.DMA((7,)), pltpu.SemaphoreType.DMA((7,))],
    )(g_part, w, m, v)


SMALL_SIZES = (2048, 1024, 128, 128, HEADS * REL_SIZE, 2048)
SMALL_PART_ROWS = tuple(-(-size // 1024) * 8 for size in SMALL_SIZES)
SMALL_ROWS = sum(SMALL_PART_ROWS)


def _pack_small(parts):
    rows = []
    for p, size, nr in zip(parts, SMALL_SIZES, SMALL_PART_ROWS):
        rows.append(jnp.pad(p.reshape(-1), (0, nr * 128 - size)).reshape(nr, 128))
    return jnp.concatenate(rows, axis=0)


def _unpack_small(slab, shapes):
    out, off = [], 0
    for size, nr, shape in zip(SMALL_SIZES, SMALL_PART_ROWS, shapes):
        out.append(slab[off:off + nr].reshape(-1)[:size].reshape(shape))
        off += nr
    return out


def kernel(x, norm1_g, w_in, ret_norm_g, q_norm_g, k_norm_g, rel_bias, w_out, norm2_g, w_ff1, w_ff2, loss_target, m_norm1_g, m_w_in, m_ret_norm_g, m_q_norm_g, m_k_norm_g, m_rel_bias, m_w_out, m_norm2_g, m_w_ff1, m_w_ff2, v_norm1_g, v_w_in, v_ret_norm_g, v_q_norm_g, v_k_norm_g, v_rel_bias, v_w_out, v_norm2_g, v_w_ff1, v_w_ff2):
    xs = x[0]
    tgt = loss_target[0]
    s, d = xs.shape
    d_in = N_CHIPS * w_in.shape[2]
    d_ff = N_CHIPS * w_ff1.shape[2]
    in_sh, ff_sh = w_in.shape[2], w_ff1.shape[2]
    tm = min(s, 1024)
    gi = s // tm
    c_arr = lax.axis_index("c").astype(jnp.int32).reshape(1)
    k_arr = (2 * lax.axis_index("x") + lax.axis_index("y")).astype(jnp.int32).reshape(1)
    tables = _tables(s)
    bias = _rel_bias_expand(rel_bias[0])

    blk_in, blk_out, blk_ff1, blk_ff2 = (
        _cast_bf16(w_in[0], k_arr, "cast_w_in"), _cast_bf16(w_out[0], k_arr, "cast_w_out"),
        _cast_bf16(w_ff1[0], k_arr, "cast_w_ff1"), _cast_bf16(w_ff2[0], k_arr, "cast_w_ff2"))
    ((wg_in,),) = _run_riders("all_gather_w_in", [_GatherRider([blk_in])])

    h1 = _rmsnorm_fwd(xs, norm1_g, "rmsnorm1")
    tn_in = in_sh // 2
    tk = d
    (proj,), ((wg_ff1,),) = _mm(
        "proj", h1, wg_in, NN, (gi, 2 * N_CHIPS, d // tk),
        pl.BlockSpec((tm, tk), lambda i, j, k: (i, k)),
        pl.BlockSpec((None, tk, tn_in), lambda i, j, k: (j // 2, k, j % 2)),
        [_sds((s, d_in), F32)], [pl.BlockSpec((tm, tn_in), lambda i, j, k: (i, j))], (tm, tn_in),
        riders=[_GatherRider([blk_ff1], (0, 3, 4))])
    (mix, y_ret, prev), ((wg_ff1,),) = _retention_fwd(proj, ret_norm_g, tables, riders=[_GatherRider([wg_ff1], (3, 4, 4))])
    mix, ((wg_out,), (wg_ff2,)) = _attention_fwd(
        proj, q_norm_g, k_norm_g, bias, mix, riders=[_GatherRider([blk_out]), _GatherRider([blk_ff2], (0, 1, 4))])
    wg_out = wg_out.reshape(d, d)
    tn = 1024
    tile = pl.BlockSpec((tm, tn), lambda i, j, k: (i, j))
    (x1,) = _mm("out_proj", mix, wg_out, NN, (gi, d // tn, d // tk),
                pl.BlockSpec((tm, tk), lambda i, j, k: (i, k)), pl.BlockSpec((tk, tn), lambda i, j, k: (k, j)),
                [_sds((s, d), F32)], [tile], (tm, tn), extras=(xs,), extra_specs=(tile,),
                epi=lambda acc, r: (r + acc,))
    h2 = _rmsnorm_fwd(x1, norm2_g, "rmsnorm2")
    tn_ff = min(ff_sh, 1024)
    per = ff_sh // tn_ff

    def relu2(acc):
        r = jnp.maximum(acc, 0.0)
        return acc, r * r

    (u, act), ((wg_ff2,),) = _mm(
        "ff1", h2, wg_ff1, NN, (gi, N_CHIPS * per, d // tk),
        pl.BlockSpec((tm, tk), lambda i, j, k: (i, k)),
        pl.BlockSpec((None, tk, tn_ff), lambda i, j, k: (j // per, k, j % per)),
        [_sds((s, d_ff), F32), _sds((s, d_ff), BF16)],
        [pl.BlockSpec((tm, tn_ff), lambda i, j, k: (i, j))] * 2, (tm, tn_ff), epi=relu2,
        riders=[_GatherRider([wg_ff2], (1, 4, 4))])
    wg_ff2 = wg_ff2.reshape(d_ff, d)

    def loss_epi(acc, res, t):
        diff = (res + acc) - t
        dy = diff / d
        return dy, dy, jnp.sum(diff * diff, axis=0, keepdims=True)

    tk2 = 1024
    dy, dyb, loss_cols = _mm(
        "ff2_loss", act, wg_ff2, NN, (gi, d // tn, d_ff // tk2),
        pl.BlockSpec((tm, tk2), lambda i, j, k: (i, k)), pl.BlockSpec((tk2, tn), lambda i, j, k: (k, j)),
        [_sds((s, d), F32), _sds((s, d), BF16), _sds((gi, 1, d), F32)],
        [tile, tile, pl.BlockSpec((None, 1, tn), lambda i, j, k: (i, 0, j))], (tm, tn),
        extras=(x1, tgt), extra_specs=(tile, tile), epi=loss_epi)
    loss = lax.psum(0.5 * jnp.sum(loss_cols) / d, ("x", "y", "c"))

    (du,) = _mm("d_act", dyb, wg_ff2, NT, (gi, d_ff // tn, d // tk),
                pl.BlockSpec((tm, tk), lambda i, j, k: (i, k)), pl.BlockSpec((tn, tk), lambda i, j, k: (j, k)),
                [_sds((s, d_ff), BF16)], [tile], (tm, tn), extras=(u,), extra_specs=(tile,),
                epi=lambda acc, uu: (acc * (2.0 * jnp.maximum(uu, 0.0)),))
    ts = min(s, 2048)
    wtile = pl.BlockSpec((tn, tn), lambda i, j, k: (i, j))
    (g_ff2,) = _mm("dw_ff2", act, dyb, TN, (d_ff // tn, d // tn, s // ts),
                   pl.BlockSpec((ts, tn), lambda i, j, k: (k, i)), pl.BlockSpec((ts, tn), lambda i, j, k: (k, j)),
                   [_sds((d_ff, d), BF16)], [wtile], (tn, tn))
    g_ff2 = g_ff2.reshape(N_CHIPS, d_ff // N_CHIPS, d)
    (g_ff1,), ((got_ff2,),) = _mm(
        "dw_ff1", h2, du, TN, (d // tn, N_CHIPS * per, s // ts),
        pl.BlockSpec((ts, tn), lambda i, j, k: (k, i)), pl.BlockSpec((ts, tn_ff), lambda i, j, k: (k, j)),
        [_sds((N_CHIPS, d, ff_sh), BF16)],
        [pl.BlockSpec((None, tn, tn_ff), lambda i, j, k: (j // per, i, j % per))], (tn, tn_ff),
        riders=[_SwapRider([g_ff2])])
    p_ff2 = _add_half(g_ff2, got_ff2, c_arr, "chip_partial_w_ff2")
    tkf = min(tk, ff_sh)
    kper = ff_sh // tkf
    (dh2,), ((got2_ff2,), (got_ff1,)) = _mm(
        "d_h2", du, wg_ff1, NT, (gi, d // tn, d_ff // tkf),
        pl.BlockSpec((tm, tkf), lambda i, j, k: (i, k)),
        pl.BlockSpec((None, tn, tkf), lambda i, j, k: (k // kper, j, k % kper)),
        [_sds((s, d), F32)], [tile], (tm, tn),
        riders=[_SendPartialsRider([p_ff2], part=(0, 3, 4)), _SwapRider([g_ff1])])
    p_ff1 = _add_half(g_ff1, got_ff1, c_arr, "chip_partial_w_ff1")
    dx1, dx1b, g_norm2 = _rmsnorm_bwd(x1, norm2_g, dh2, dy, "rmsnorm2_bwd")

    (dmix,) = _mm("d_mix", dx1b, wg_out, NT, (gi, d // tn, d // tk),
                  pl.BlockSpec((tm, tk), lambda i, j, k: (i, k)), pl.BlockSpec((tn, tk), lambda i, j, k: (j, k)),
                  [_sds((s, d), F32)], [tile], (tm, tn))
    (g_out,) = _mm("dw_out", mix, dx1b, TN, (d // tn, d // tn, s // ts),
                   pl.BlockSpec((ts, tn), lambda i, j, k: (k, i)), pl.BlockSpec((ts, tn), lambda i, j, k: (k, j)),
                   [_sds((d, d), BF16)], [wtile], (tn, tn))
    g_out = g_out.reshape(N_CHIPS, d // N_CHIPS, d)
    ((got_out,),) = _run_riders("grad_swap_w_out", [_SwapRider([g_out])])
    p_out = _add_half(g_out, got_out, c_arr, "chip_partial_w_out")
    (d_rq, d_rk, d_rv, d_rg, g_gn), ((got2_ff2,), (got2_ff1,)) = _retention_bwd(
        proj, ret_norm_g, tables, y_ret, prev, dmix,
        riders=[_SendPartialsRider([p_ff2], got=[got2_ff2], part=(3, 4, 4)), _SendPartialsRider([p_ff1], part=(0, 2, 4))])
    (d_aq, d_ak, d_av, g_gq, g_gk, dbias), ((got2_ff1,), (got2_out,)) = _attention_bwd(
        proj, q_norm_g, k_norm_g, bias, dmix,
        riders=[_SendPartialsRider([p_ff1], got=[got2_ff1], part=(2, 4, 4)), _SendPartialsRider([p_out])])
    g_rel = _rel_bias_fold(dbias)
    dproj = jnp.concatenate([d_rq, d_rk, d_rv, d_rg, d_aq, d_ak, d_av], axis=-1)
    (g_in,) = _mm("dw_in", h1, dproj, TN, (d // tn, 2 * N_CHIPS, s // ts),
                  pl.BlockSpec((ts, tn), lambda i, j, k: (k, i)), pl.BlockSpec((ts, tn_in), lambda i, j, k: (k, j)),
                  [_sds((N_CHIPS, d, in_sh), BF16)],
                  [pl.BlockSpec((None, tn, tn_in), lambda i, j, k: (j // 2, i, j % 2))], (tn, tn_in))
    ((got_in,),) = _run_riders("grad_swap_w_in", [_SwapRider([g_in])])
    p_in = _add_half(g_in, got_in, c_arr, "chip_partial_w_in")
    (dh1,), ((got2_in,),) = _mm(
        "d_h1", dproj, wg_in, NT, (gi, d // tn, N_CHIPS),
        pl.BlockSpec((tm, in_sh), lambda i, j, k: (i, k)),
        pl.BlockSpec((None, tn, in_sh), lambda i, j, k: (k, j, 0)),
        [_sds((s, d), F32)], [tile], (tm, tn), riders=[_SendPartialsRider([p_in], part=(0, 3, 4))])
    (grad_x, _, g_norm1), ((got2_in,),) = _rmsnorm_bwd(
        xs, norm1_g, dh1, dx1, "rmsnorm1_bwd", riders=[_SendPartialsRider([p_in], got=[got2_in], part=(3, 4, 4))])

    names = ["w_in", "w_out", "w_ff1", "w_ff2"]
    kc_arr = jnp.concatenate([k_arr, c_arr])
    halves = [_sum_partials(p, r, kc_arr, "sum_partials_" + nm)
              for p, r, nm in zip((p_in, p_out, p_ff1, p_ff2), (got2_in, got2_out, got2_ff1, got2_ff2), names)]
    (g_big,) = _run_riders("grad_share_halves", [_ShareRider(halves)])
    big = []
    for g, w, m, v, nm in zip(g_big, (w_in, w_out, w_ff1, w_ff2), (m_w_in, m_w_out, m_w_ff1, m_w_ff2),
                              (v_w_in, v_w_out, v_w_ff1, v_w_ff2), names):
        delta, new_m, new_v = _adamw(w[0], g, m[0], v[0], "adamw_" + nm)
        big.append((g[None], delta[None], new_m[None], new_v[None]))

    small_w = (norm1_g, ret_norm_g, q_norm_g, k_norm_g, rel_bias, norm2_g)
    small_m = (m_norm1_g, m_ret_norm_g, m_q_norm_g, m_k_norm_g, m_rel_bias, m_norm2_g)
    small_v = (v_norm1_g, v_ret_norm_g, v_q_norm_g, v_k_norm_g, v_rel_bias, v_norm2_g)
    shapes = [p.shape for p in small_w]
    g_small = _pack_small([g_norm1, g_gn, g_gq, g_gk, g_rel, g_norm2])
    sg, sd, sm, sv = (_unpack_small(a, shapes) for a in _small_allreduce_adamw(
        g_small, _pack_small(small_w), _pack_small(small_m), _pack_small(small_v)))

    def ordered(kind):
        sm_ = (sg, sd, sm, sv)[kind]
        return (sm_[0], big[0][kind], sm_[1], sm_[2], sm_[3], sm_[4], big[1][kind], sm_[5], big[2][kind], big[3][kind])

    return (loss, grad_x[None], *ordered(0), *ordered(1), *ordered(2), *ordered(3))
```

```python
import functools

import jax
import jax.numpy as jnp
from jax import lax
from jax.experimental import pallas as pl
from jax.experimental.pallas import tpu as pltpu

F32 = jnp.float32
BF16 = jnp.bfloat16
MXU_DTYPE = jnp.bfloat16

CHUNK = 64
HEADS = 8
HEAD_DIM = 128
LEFT_CHUNKS = 8
BAND = (LEFT_CHUNKS + 1) * CHUNK
REL_CLIP = 128
REL_SIZE = (CHUNK - 1) + REL_CLIP + 1
RET_BLOCK_CHUNKS = 8
RET_ROWS = RET_BLOCK_CHUNKS * CHUNK
RET_SUB = 256
ROPE_BASE = 10000.0
EPS = 1e-6
GN_EPS = 1e-5
ADAM_LR, ADAM_B1, ADAM_B2, ADAM_EPS, ADAM_WD, ADAM_STEP = 0.001, 0.9, 0.999, 1e-08, 0.01, 10
N_CHIPS = 4
VMEM_LIMIT = 56 * 1024 * 1024
MESH = pl.DeviceIdType.MESH
ANY = pl.BlockSpec(memory_space=pl.ANY)

NN = (((1,), (0,)), ((), ()))
NT = (((1,), (1,)), ((), ()))
TN = (((0,), (0,)), ((), ()))


def _pallas(body, **kw):
    return pl.pallas_call(body, **kw)


def _params(*sem):
    return pltpu.CompilerParams(dimension_semantics=sem, vmem_limit_bytes=VMEM_LIMIT)


def _dot(a, b, dims):
    return lax.dot_general(a.astype(MXU_DTYPE), b.astype(MXU_DTYPE), dims, preferred_element_type=F32)


RIDER_MID, RIDER_LATE = 0.6, 0.85


def _mm(name, a, b, dims, grid, a_spec, b_spec, outs, o_specs, acc_shape, extras=(), extra_specs=(), epi=None,
        riders=()):
    ni, nj, nk = grid
    n_ex, n_out = len(extras), len(outs)
    n_in = 2 + n_ex
    rs = _Riders(riders, n_in, n_out)
    n_rin, n_rout = len(rs.arrays), len(rs.out_shapes)
    steps = ni * nj * nk

    def body(*refs):
        a_ref, b_ref = refs[0], refs[1]
        ex_refs = refs[2:n_in]
        o_refs = refs[n_in + n_rin:n_in + n_rin + n_out]
        acc_ref = refs[n_in + n_rin + n_out + n_rout]
        k = pl.program_id(2)
        if riders:
            bound = rs.bind(refs[n_in:n_in + n_rin], refs[n_in + n_rin + n_out:n_in + n_rin + n_out + n_rout],
                            refs[n_in + n_rin + n_out + n_rout + 1:])
            step = (pl.program_id(0) * nj + pl.program_id(1)) * nk + k
            pl.when(step == 0)(lambda: rs.run("start", bound))
            pl.when(step == int(steps * RIDER_MID))(lambda: rs.run("mid", bound))
            pl.when(step == int(steps * RIDER_LATE))(lambda: rs.run("late", bound))

        def finish(acc):
            vals = epi(acc, *[r[...] for r in ex_refs]) if epi is not None else (acc,)
            for r, v in zip(o_refs, vals):
                r[...] = v.astype(r.dtype)

        if nk == 1:
            finish(_dot(a_ref[...], b_ref[...], dims))
        else:
            @pl.when(k == 0)
            def _():
                acc_ref[...] = jnp.zeros_like(acc_ref)

            acc_ref[...] += _dot(a_ref[...], b_ref[...], dims)
            pl.when(k == nk - 1)(lambda: finish(acc_ref[...]))

        if riders:
            pl.when(step == steps - 1)(lambda: rs.run("end", bound))

    res = _pallas(
        body, name=name, grid=grid, in_specs=[a_spec, b_spec, *extra_specs, *rs.in_specs],
        out_specs=[*o_specs, *rs.out_specs], out_shape=[*outs, *rs.out_shapes], input_output_aliases=rs.aliases,
        scratch_shapes=[pltpu.VMEM(acc_shape if nk > 1 else (8, 128), F32), *rs.scratch],
        compiler_params=_params(*(("arbitrary",) * 3 if riders else ("parallel", "parallel", "arbitrary"))),
    )(a, b, *extras, *rs.arrays)
    return (res[:n_out], rs.split(res[n_out:])) if riders else res


def _sds(shape, dtype):
    return jax.ShapeDtypeStruct(shape, dtype)


def _cast_bf16(w, k_arr, name):
    r, c = w.shape
    tr = min(r, 256)

    def body(k_ref, w_ref, o_ref):
        o_ref[...] = w_ref[...].astype(BF16)

    grid_spec = pltpu.PrefetchScalarGridSpec(
        num_scalar_prefetch=1, grid=(r // tr,), in_specs=[pl.BlockSpec((tr, c), lambda i, k_ref: (i, 0))],
        out_specs=pl.BlockSpec((None, tr, c), lambda i, k_ref: (k_ref[0], i, 0)))
    return _pallas(body, name=name, grid_spec=grid_spec, out_shape=_sds((N_CHIPS, r, c), BF16),
                   compiler_params=_params("parallel"))(k_arr, w)


def _rmsnorm_fwd(x, g, name):
    s, d = x.shape
    tr = 256

    def body(x_ref, g_ref, o_ref):
        xv = x_ref[...]
        y = xv * lax.rsqrt(jnp.mean(xv * xv, axis=-1, keepdims=True) + EPS)
        o_ref[...] = (y * g_ref[...]).astype(o_ref.dtype)

    return _pallas(body, name=name, grid=(s // tr,),
                   in_specs=[pl.BlockSpec((tr, d), lambda i: (i, 0)), pl.BlockSpec((1, d), lambda i: (0, 0))],
                   out_specs=pl.BlockSpec((tr, d), lambda i: (i, 0)), out_shape=_sds((s, d), BF16),
                   compiler_params=_params("parallel"))(x, g)


def _rmsnorm_bwd(x, g, dh, res, name, riders=()):
    s, d = x.shape
    tr = 256

    def body(x_ref, g_ref, dh_ref, res_ref, dx_ref, dxb_ref, dg_ref):
        i = pl.program_id(0)
        xv = x_ref[...]
        rstd = lax.rsqrt(jnp.mean(xv * xv, axis=-1, keepdims=True) + EPS)
        xh = xv * rstd
        dhv = dh_ref[...]

        @pl.when(i == 0)
        def _():
            dg_ref[...] = jnp.zeros_like(dg_ref)

        dg_ref[...] += jnp.sum(dhv * xh, axis=0, keepdims=True)
        dxh = dhv * g_ref[...]
        dx = res_ref[...] + rstd * (dxh - xh * jnp.mean(dxh * xh, axis=-1, keepdims=True))
        dx_ref[...] = dx
        dxb_ref[...] = dx.astype(BF16)

    row = pl.BlockSpec((tr, d), lambda i: (i, 0))
    vec = pl.BlockSpec((1, d), lambda i: (0, 0))
    rs = _Riders(riders, 4, 3)
    out = _pallas(_with_riders(body, 4, 3, 0, rs, (s // tr,)), name=name, grid=(s // tr,),
                  in_specs=[row, vec, row, row, *rs.in_specs], out_specs=[row, row, vec, *rs.out_specs],
                  out_shape=[_sds((s, d), F32), _sds((s, d), BF16), _sds((1, d), F32), *rs.out_shapes],
                  input_output_aliases=rs.aliases, scratch_shapes=rs.scratch,
                  compiler_params=_params("arbitrary"))(x, g, dh, res, *rs.arrays)
    return (out[:3], rs.split(out[3:])) if riders else out


def _adamw_math(w, g, m, v):
    m = ADAM_B1 * m + (1.0 - ADAM_B1) * g
    v = ADAM_B2 * v + (1.0 - ADAM_B2) * (g * g)
    m_hat = m / (1.0 - ADAM_B1 ** ADAM_STEP)
    v_hat = v / (1.0 - ADAM_B2 ** ADAM_STEP)
    delta = -ADAM_LR * (m_hat / (jnp.sqrt(v_hat) + ADAM_EPS) + ADAM_WD * w)
    return delta, m, v


def _adamw(w, g, m, v, name):
    r, c = w.shape
    tr = 128

    def body(w_ref, g_ref, m_ref, v_ref, go_ref, d_ref, nm_ref, nv_ref):
        g = g_ref[...]
        go_ref[...] = g
        d_ref[...], nm_ref[...], nv_ref[...] = _adamw_math(w_ref[...], g, m_ref[...], v_ref[...])

    blk = pl.BlockSpec((tr, c), lambda i: (i, 0))
    return _pallas(body, name=name, grid=(r // tr,), in_specs=[blk] * 4, out_specs=[blk] * 4,
                   out_shape=[_sds((r, c), F32)] * 4, compiler_params=_params("parallel"))(w, g, m, v)


def _tables(s):
    half = HEAD_DIM // 2
    pos = jnp.arange(s, dtype=F32)
    inv_freq = ROPE_BASE ** (-jnp.arange(half, dtype=F32) / half)
    ang = pos[:, None] * inv_freq[None, :]
    cos, sin = jnp.cos(ang), jnp.sin(ang)
    cos_f = jnp.concatenate([cos, cos], axis=-1)
    sin_f = jnp.concatenate([-sin, sin], axis=-1)
    log_g = jnp.log1p(-jnp.exp2(-(5.0 + jnp.arange(HEADS, dtype=F32))))
    p = jnp.arange(CHUNK, dtype=F32)
    decay = jnp.exp(log_g[:, None, None] * jnp.abs(p[:, None] - p[None, :]))
    k_dec = jnp.exp(log_g[None, :] * (CHUNK - 1.0 - p)[:, None])
    q_dec = jnp.exp(log_g[None, :] * (p + 1.0)[:, None])
    c_dec = jnp.exp(log_g * CHUNK)
    k_dec = jnp.tile(jnp.broadcast_to(k_dec.T[:, :, None], (HEADS, CHUNK, HEAD_DIM)), (1, RET_BLOCK_CHUNKS, 1))
    q_dec = jnp.tile(jnp.broadcast_to(q_dec.T[:, :, None], (HEADS, CHUNK, HEAD_DIM)), (1, RET_BLOCK_CHUNKS, 1))
    c_dec = jnp.broadcast_to(c_dec[:, None, None], (HEADS, 1, HEAD_DIM))
    n = RET_SUB // CHUNK
    decay = (jnp.eye(n, dtype=F32)[None, :, None, :, None] * decay[:, None, :, None, :]).reshape(HEADS, RET_SUB, RET_SUB)
    return cos_f, sin_f, decay, k_dec, q_dec, c_dec


def _rot(x, cos_f, sin_f):
    return x * cos_f + pltpu.roll(x, HEAD_DIM // 2, 1) * sin_f


def _rot_bwd(d, cos_f, sin_f):
    return d * cos_f + pltpu.roll(d * sin_f, HEAD_DIM // 2, 1)


K_SCALE = HEAD_DIM ** -0.5


def _retention_fwd(proj, gn_g, tables, riders=()):
    s = proj.shape[0]
    nb = s // RET_ROWS
    nc = s // CHUNK
    cos_f, sin_f, decay, k_dec, q_dec, c_dec = tables

    def body(q_ref, k_ref, v_ref, g_ref, cos_ref, sin_ref, dec_ref, kd_ref, qd_ref, cd_ref, gn_ref,
             ret_ref, y_ref, prev_ref, state_ref):
        @pl.when(pl.program_id(1) == 0)
        def _():
            state_ref[...] = jnp.zeros_like(state_ref)

        cosv, sinv = cos_ref[...], sin_ref[...]
        q = _rot(q_ref[...], cosv, sinv)
        k = _rot(k_ref[...], cosv, sinv) * K_SCALE
        v = v_ref[...]
        rg = g_ref[...]
        dec, cd, gn = dec_ref[...], cd_ref[...], gn_ref[...]
        kdf, qdf = k * kd_ref[...], q * qd_ref[...]
        chunks = [slice(c * CHUNK, (c + 1) * CHUNK) for c in range(RET_BLOCK_CHUNKS)]
        contribs = [_dot(kdf[rows], v[rows], TN) for rows in chunks]
        state, states = state_ref[...], []
        for c in range(RET_BLOCK_CHUNKS):
            states.append(state)
            prev_ref[c] = state.astype(prev_ref.dtype)
            state = cd * state + contribs[c]
        state_ref[...] = state
        cross = jnp.concatenate([_dot(qdf[rows], st, NN) for rows, st in zip(chunks, states)], axis=0)
        intra = []
        for b in range(RET_ROWS // RET_SUB):
            rows = slice(b * RET_SUB, (b + 1) * RET_SUB)
            intra.append(_dot(_dot(q[rows], k[rows], NT) * dec, v[rows], NN))
        y = jnp.concatenate(intra, axis=0) + cross
        y_ref[...] = y
        mu = jnp.mean(y, axis=-1, keepdims=True)
        yc = y - mu
        var = jnp.mean(yc * yc, axis=-1, keepdims=True)
        yn = yc * lax.rsqrt(var + GN_EPS) * gn
        ret_ref[...] = (rg * jax.nn.sigmoid(rg) * yn).astype(ret_ref.dtype)

    def col(off):
        return pl.BlockSpec((RET_ROWS, HEAD_DIM), lambda h, i: (i, off + h))

    pos = pl.BlockSpec((RET_ROWS, HEAD_DIM), lambda h, i: (i, 0))
    per_head = lambda shape: pl.BlockSpec((None, *shape), lambda h, i: (h, 0, 0))
    rs = _Riders(riders, 11, 3)
    res = _pallas(
        _with_riders(body, 11, 3, 1, rs, (HEADS, nb)), name="retention_fwd", grid=(HEADS, nb),
        in_specs=[col(0), col(HEADS), col(2 * HEADS), col(3 * HEADS), pos, pos,
                  per_head((RET_SUB, RET_SUB)), per_head((RET_ROWS, HEAD_DIM)), per_head((RET_ROWS, HEAD_DIM)),
                  per_head((1, HEAD_DIM)), pl.BlockSpec((1, HEAD_DIM), lambda h, i: (0, h)), *rs.in_specs],
        out_specs=[col(0), col(0),
                   pl.BlockSpec((None, RET_BLOCK_CHUNKS, HEAD_DIM, HEAD_DIM), lambda h, i: (h, i, 0, 0)),
                   *rs.out_specs],
        out_shape=[_sds((s, 2 * HEADS * HEAD_DIM), BF16), _sds((s, HEADS * HEAD_DIM), F32),
                   _sds((HEADS, nc, HEAD_DIM, HEAD_DIM), MXU_DTYPE), *rs.out_shapes],
        input_output_aliases=rs.aliases,
        scratch_shapes=[pltpu.VMEM((HEAD_DIM, HEAD_DIM), F32), *rs.scratch],
        compiler_params=_params("arbitrary", "arbitrary"),
    )(proj, proj, proj, proj, cos_f, sin_f, decay, k_dec, q_dec, c_dec, gn_g, *rs.arrays)
    return res[:3], rs.split(res[3:])


def _retention_bwd(proj, gn_g, tables, y, prev, dmix, riders=()):
    s = proj.shape[0]
    nb = s // RET_ROWS
    cos_f, sin_f, decay, k_dec, q_dec, c_dec = tables

    def body(q_ref, k_ref, v_ref, g_ref, cos_ref, sin_ref, dec_ref, kd_ref, qd_ref, cd_ref, gn_ref,
             y_ref, prev_ref, dret_ref, dq_ref, dk_ref, dv_ref, dg_ref, dgn_ref, gstate_ref):
        @pl.when(pl.program_id(1) == 0)
        def _():
            gstate_ref[...] = jnp.zeros_like(gstate_ref)
            dgn_ref[...] = jnp.zeros_like(dgn_ref)

        cosv, sinv = cos_ref[...], sin_ref[...]
        q = _rot(q_ref[...], cosv, sinv)
        k = _rot(k_ref[...], cosv, sinv) * K_SCALE
        v = v_ref[...]
        dec, kd, qd, cd, gn = dec_ref[...], kd_ref[...], qd_ref[...], cd_ref[...], gn_ref[...]
        kdf, qdf = k * kd, q * qd
        rg = g_ref[...]
        yv = y_ref[...]
        dret = dret_ref[...]
        sig = jax.nn.sigmoid(rg)
        gate = rg * sig
        mu = jnp.mean(yv, axis=-1, keepdims=True)
        yc = yv - mu
        rstd = lax.rsqrt(jnp.mean(yc * yc, axis=-1, keepdims=True) + GN_EPS)
        z = yc * rstd
        dyn = dret * gate
        dg_ref[...] = (dret * (z * gn) * (sig * (1.0 + rg * (1.0 - sig)))).astype(dg_ref.dtype)
        dgn_ref[...] += jnp.sum(dyn * z, axis=0, keepdims=True)
        dz = dyn * gn
        dy = rstd * (dz - jnp.mean(dz, axis=-1, keepdims=True) - z * jnp.mean(dz * z, axis=-1, keepdims=True))
        chunks = [slice(c * CHUNK, (c + 1) * CHUNK) for c in range(RET_BLOCK_CHUNKS)]
        dprevs = [_dot(qdf[rows], dy[rows], TN) for rows in chunks]
        gst, gsts = gstate_ref[...], [None] * RET_BLOCK_CHUNKS
        for c in reversed(range(RET_BLOCK_CHUNKS)):
            gsts[c] = gst
            gst = dprevs[c] + cd * gst
        gstate_ref[...] = gst
        dq = jnp.concatenate([_dot(dy[rows], prev_ref[c], NT) for c, rows in enumerate(chunks)], axis=0) * qd
        dk = jnp.concatenate([_dot(v[rows], g, NT) for rows, g in zip(chunks, gsts)], axis=0) * kd
        dv = jnp.concatenate([_dot(kdf[rows], g, NN) for rows, g in zip(chunks, gsts)], axis=0)
        dqi, dki, dvi = [], [], []
        for b in range(RET_ROWS // RET_SUB):
            rows = slice(b * RET_SUB, (b + 1) * RET_SUB)
            qs, ks, vs, dys = q[rows], k[rows], v[rows], dy[rows]
            dvi.append(_dot(_dot(ks, qs, NT) * dec, dys, NN))
            dqi.append(_dot(_dot(dys, vs, NT) * dec, ks, NN))
            dki.append(_dot(_dot(vs, dys, NT) * dec, qs, NN))
        dq = dq + jnp.concatenate(dqi, axis=0)
        dk = dk + jnp.concatenate(dki, axis=0)
        dv = dv + jnp.concatenate(dvi, axis=0)
        dq_ref[...] = _rot_bwd(dq, cosv, sinv).astype(dq_ref.dtype)
        dk_ref[...] = _rot_bwd(dk * K_SCALE, cosv, sinv).astype(dk_ref.dtype)
        dv_ref[...] = dv.astype(dv_ref.dtype)

    rev = lambda i: nb - 1 - i

    def col(off):
        return pl.BlockSpec((RET_ROWS, HEAD_DIM), lambda h, i: (rev(i), off + h))

    pos = pl.BlockSpec((RET_ROWS, HEAD_DIM), lambda h, i: (rev(i), 0))
    per_head = lambda shape: pl.BlockSpec((None, *shape), lambda h, i: (h, 0, 0))
    outb = _sds((s, HEADS * HEAD_DIM), BF16)
    rs = _Riders(riders, 14, 5)
    res = _pallas(
        _with_riders(body, 14, 5, 1, rs, (HEADS, nb)), name="retention_bwd", grid=(HEADS, nb),
        in_specs=[col(0), col(HEADS), col(2 * HEADS), col(3 * HEADS), pos, pos,
                  per_head((RET_SUB, RET_SUB)), per_head((RET_ROWS, HEAD_DIM)), per_head((RET_ROWS, HEAD_DIM)),
                  per_head((1, HEAD_DIM)), pl.BlockSpec((1, HEAD_DIM), lambda h, i: (0, h)),
                  col(0), pl.BlockSpec((None, RET_BLOCK_CHUNKS, HEAD_DIM, HEAD_DIM), lambda h, i: (h, rev(i), 0, 0)),
                  col(0), *rs.in_specs],
        out_specs=[col(0), col(0), col(0), col(0), per_head((1, HEAD_DIM)), *rs.out_specs],
        out_shape=[outb, outb, outb, outb, _sds((HEADS, 1, HEAD_DIM), F32), *rs.out_shapes],
        input_output_aliases=rs.aliases,
        scratch_shapes=[pltpu.VMEM((HEAD_DIM, HEAD_DIM), F32), *rs.scratch],
        compiler_params=_params("arbitrary", "arbitrary"),
    )(proj, proj, proj, proj, cos_f, sin_f, decay, k_dec, q_dec, c_dec, gn_g, y, prev, dmix, *rs.arrays)
    return res[:5], rs.split(res[5:])


ATT_COL0 = 4 * HEADS
PAD_ROWS = LEFT_CHUNKS * CHUNK
NORM_ROWS = 512
GROUP_CHUNKS = 4
GROUP = GROUP_CHUNKS * CHUNK
WIN = (LEFT_CHUNKS + GROUP_CHUNKS) * CHUNK
MASKED = -1e30


def _qk_norm(x, g):
    return x * lax.rsqrt(jnp.mean(x * x, axis=-1, keepdims=True) + EPS) * g


def _band_probs(qb, kb, bias, g):
    sc = _dot(qb, kb, NT) * K_SCALE + bias
    win_chunk = lax.broadcasted_iota(jnp.int32, (GROUP, WIN), 1) // CHUNK
    sc = jnp.where(g * GROUP_CHUNKS - LEFT_CHUNKS + win_chunk >= 0, sc, MASKED)
    e = jnp.exp(sc - jnp.max(sc, axis=-1, keepdims=True))
    return e / jnp.sum(e, axis=-1, keepdims=True)


def _with_riders(core, n_in, n_out, n_scratch, rs, grid):
    n_rin, n_rout = len(rs.arrays), len(rs.out_shapes)
    if not rs.riders:
        return core
    steps = 1
    for n in grid:
        steps *= n

    def body(*refs):
        outs_at = n_in + n_rin
        scratch_at = outs_at + n_out + n_rout
        bound = rs.bind(refs[n_in:outs_at], refs[outs_at + n_out:scratch_at], refs[scratch_at + n_scratch:])
        step = 0
        for axis, n in enumerate(grid):
            step = step * n + pl.program_id(axis)
        pl.when(step == 0)(lambda: rs.run("start", bound))
        pl.when(step == int(steps * RIDER_MID))(lambda: rs.run("mid", bound))
        pl.when(step == int(steps * RIDER_LATE))(lambda: rs.run("late", bound))
        core(*refs[:n_in], *refs[outs_at:outs_at + n_out], *refs[scratch_at:scratch_at + n_scratch])
        pl.when(step == steps - 1)(lambda: rs.run("end", bound))

    return body


def _attention_fwd(proj, gq, gk, bias, mix, riders=()):
    s = proj.shape[0]
    rs = _Riders(riders, 7, 1)

    def body(q_ref, k_ref, v_ref, gq_ref, gk_ref, bias_ref, mix_ref, o_ref, kp_ref, vp_ref):
        kp_ref[0:PAD_ROWS, :] = jnp.zeros((PAD_ROWS, HEAD_DIM), kp_ref.dtype)
        vp_ref[0:PAD_ROWS, :] = jnp.zeros((PAD_ROWS, HEAD_DIM), vp_ref.dtype)
        gqv, gkv = gq_ref[...], gk_ref[...]

        def fill(b, carry):
            r0 = pl.multiple_of(b * NORM_ROWS, NORM_ROWS)
            kp_ref[pl.ds(PAD_ROWS + r0, NORM_ROWS), :] = _qk_norm(k_ref[pl.ds(r0, NORM_ROWS), :], gkv).astype(kp_ref.dtype)
            vp_ref[pl.ds(PAD_ROWS + r0, NORM_ROWS), :] = v_ref[pl.ds(r0, NORM_ROWS), :].astype(vp_ref.dtype)
            return carry

        lax.fori_loop(0, s // NORM_ROWS, fill, 0)

        def group(g, carry):
            r0 = pl.multiple_of(g * GROUP, GROUP)
            qn = _qk_norm(q_ref[pl.ds(r0, GROUP), :], gqv)
            p = _band_probs(qn, kp_ref[pl.ds(r0, WIN), :], bias_ref[...], g)
            o_ref[pl.ds(r0, GROUP), :] = _dot(p, vp_ref[pl.ds(r0, WIN), :], NN).astype(o_ref.dtype)
            return carry

        lax.fori_loop(0, s // GROUP, group, 0, unroll=2)

    def col(off):
        return pl.BlockSpec((s, HEAD_DIM), lambda h: (0, off + h))

    vec = pl.BlockSpec((1, HEAD_DIM), lambda h: (0, 0))
    res = _pallas(
        _with_riders(body, 7, 1, 2, rs, (HEADS,)), name="attention_fwd", grid=(HEADS,),
        in_specs=[col(ATT_COL0), col(ATT_COL0 + HEADS), col(ATT_COL0 + 2 * HEADS), vec, vec,
                  pl.BlockSpec((None, GROUP, WIN), lambda h: (h, 0, 0)), ANY, *rs.in_specs],
        out_specs=[col(HEADS), *rs.out_specs], out_shape=[_sds(mix.shape, mix.dtype), *rs.out_shapes],
        input_output_aliases={6: 0, **rs.aliases},
        scratch_shapes=[pltpu.VMEM((s + PAD_ROWS, HEAD_DIM), MXU_DTYPE), pltpu.VMEM((s + PAD_ROWS, HEAD_DIM), MXU_DTYPE),
                        *rs.scratch],
        compiler_params=_params("arbitrary"),
    )(proj, proj, proj, gq, gk, bias, mix, *rs.arrays)
    return res[0], rs.split(res[1:])


def _attention_bwd(proj, gq, gk, bias, dmix, riders=()):
    s = proj.shape[0]
    nc = s // CHUNK
    rs = _Riders(riders, 7, 6)

    def body(q_ref, k_ref, v_ref, gq_ref, gk_ref, bias_ref, do_ref,
             dq_ref, dk_ref, dv_ref, dgq_ref, dgk_ref, dbias_ref, kp_ref, vp_ref, dkp_ref, dvp_ref, dqn_ref):
        kp_ref[0:PAD_ROWS, :] = jnp.zeros((PAD_ROWS, HEAD_DIM), kp_ref.dtype)
        vp_ref[0:PAD_ROWS, :] = jnp.zeros((PAD_ROWS, HEAD_DIM), vp_ref.dtype)
        dkp_ref[...] = jnp.zeros_like(dkp_ref)
        dvp_ref[...] = jnp.zeros_like(dvp_ref)
        dbias_ref[...] = jnp.zeros_like(dbias_ref)
        gqv, gkv = gq_ref[...], gk_ref[...]

        def fill(b, carry):
            r0 = pl.multiple_of(b * NORM_ROWS, NORM_ROWS)
            kp_ref[pl.ds(PAD_ROWS + r0, NORM_ROWS), :] = _qk_norm(k_ref[pl.ds(r0, NORM_ROWS), :], gkv).astype(kp_ref.dtype)
            vp_ref[pl.ds(PAD_ROWS + r0, NORM_ROWS), :] = v_ref[pl.ds(r0, NORM_ROWS), :].astype(vp_ref.dtype)
            return carry

        lax.fori_loop(0, s // NORM_ROWS, fill, 0)

        def group(g, carry):
            r0 = pl.multiple_of(g * GROUP, GROUP)
            qn = _qk_norm(q_ref[pl.ds(r0, GROUP), :], gqv)
            kb = kp_ref[pl.ds(r0, WIN), :]
            vb = vp_ref[pl.ds(r0, WIN), :]
            p = _band_probs(qn, kb, bias_ref[...], g)
            do = do_ref[pl.ds(r0, GROUP), :]
            dvp_ref[pl.ds(r0, WIN), :] += _dot(p, do, TN)
            dp = _dot(do, vb, NT)
            ds = p * (dp - jnp.sum(dp * p, axis=-1, keepdims=True))
            dbias_ref[...] += ds
            dss = ds * K_SCALE
            dqn_ref[pl.ds(r0, GROUP), :] = _dot(dss, kb, NN)
            dkp_ref[pl.ds(r0, WIN), :] += _dot(dss, qn, TN)
            return carry

        lax.fori_loop(0, s // GROUP, group, 0, unroll=2)

        @pl.when(pl.program_id(0) == 0)
        def _():
            dgq_ref[...] = jnp.zeros_like(dgq_ref)
            dgk_ref[...] = jnp.zeros_like(dgk_ref)

        def norm_bwd(x, g, dn):
            rstd = lax.rsqrt(jnp.mean(x * x, axis=-1, keepdims=True) + EPS)
            xh = x * rstd
            dxh = dn * g
            return rstd * (dxh - xh * jnp.mean(dxh * xh, axis=-1, keepdims=True)), jnp.sum(dn * xh, axis=0, keepdims=True)

        def finish(b, carry):
            r0 = pl.multiple_of(b * NORM_ROWS, NORM_ROWS)
            rows = pl.ds(r0, NORM_ROWS)
            dq, dgq = norm_bwd(q_ref[rows, :], gqv, dqn_ref[rows, :])
            dk, dgk = norm_bwd(k_ref[rows, :], gkv, dkp_ref[pl.ds(PAD_ROWS + r0, NORM_ROWS), :])
            dq_ref[rows, :] = dq.astype(dq_ref.dtype)
            dk_ref[rows, :] = dk.astype(dk_ref.dtype)
            dv_ref[rows, :] = dvp_ref[pl.ds(PAD_ROWS + r0, NORM_ROWS), :].astype(dv_ref.dtype)
            dgq_ref[...] += dgq
            dgk_ref[...] += dgk
            return carry

        lax.fori_loop(0, s // NORM_ROWS, finish, 0)

    def col(off):
        return pl.BlockSpec((s, HEAD_DIM), lambda h: (0, off + h))

    vec = pl.BlockSpec((1, HEAD_DIM), lambda h: (0, 0))
    hbias = pl.BlockSpec((None, GROUP, WIN), lambda h: (h, 0, 0))
    outb = _sds((s, HEADS * HEAD_DIM), BF16)
    res = _pallas(
        _with_riders(body, 7, 6, 5, rs, (HEADS,)), name="attention_bwd", grid=(HEADS,),
        in_specs=[col(ATT_COL0), col(ATT_COL0 + HEADS), col(ATT_COL0 + 2 * HEADS), vec, vec, hbias, col(HEADS),
                  *rs.in_specs],
        out_specs=[col(0), col(0), col(0), vec, vec, hbias, *rs.out_specs],
        out_shape=[outb, outb, outb, _sds((1, HEAD_DIM), F32), _sds((1, HEAD_DIM), F32),
                   _sds((HEADS, GROUP, WIN), F32), *rs.out_shapes],
        input_output_aliases=rs.aliases,
        scratch_shapes=[pltpu.VMEM((s + PAD_ROWS, HEAD_DIM), MXU_DTYPE), pltpu.VMEM((s + PAD_ROWS, HEAD_DIM), MXU_DTYPE),
                        pltpu.VMEM((s + PAD_ROWS, HEAD_DIM), F32), pltpu.VMEM((s + PAD_ROWS, HEAD_DIM), F32),
                        pltpu.VMEM((s, HEAD_DIM), F32), *rs.scratch],
        compiler_params=_params("arbitrary"),
    )(proj, proj, proj, gq, gk, bias, dmix, *rs.arrays)
    return res[:6], rs.split(res[6:])


DIAG_SPLIT = (BAND + WIN - CHUNK) // 2


def _diag_bin(m):
    t = jnp.where(m < DIAG_SPLIT, m, m - WIN)
    return jnp.clip(LEFT_CHUNKS * CHUNK - t, -(CHUNK - 1), REL_CLIP) + (CHUNK - 1)


def _skew_rows(a, left):
    row = lax.broadcasted_iota(jnp.int32, (GROUP, WIN), 0)
    for b in range(GROUP.bit_length() - 1):
        step = 1 << b
        a = jnp.where(jnp.bitwise_and(row, step) != 0, pltpu.roll(a, WIN - step if left else step, 1), a)
    return a


def _rel_bias_expand(rel_bias):
    def body(rb_ref, o_ref):
        h = pl.program_id(0)
        bins = _diag_bin(lax.broadcasted_iota(jnp.int32, (8, WIN), 1))
        per_diag = lax.fori_loop(0, REL_SIZE, lambda r, acc: jnp.where(bins == r, rb_ref[h, r], acc),
                                 jnp.zeros((8, WIN), F32))
        table = _skew_rows(jnp.broadcast_to(per_diag[0:1], (GROUP, WIN)), left=False)
        row_chunk = lax.broadcasted_iota(jnp.int32, (GROUP, WIN), 0) // CHUNK
        col_chunk = lax.broadcasted_iota(jnp.int32, (GROUP, WIN), 1) // CHUNK
        in_band = jnp.logical_and(col_chunk >= row_chunk, col_chunk <= row_chunk + LEFT_CHUNKS)
        o_ref[...] = jnp.where(in_band, table, MASKED)

    return _pallas(body, name="rel_bias_expand", grid=(HEADS,), in_specs=[pl.BlockSpec(memory_space=pltpu.SMEM)],
                   out_specs=pl.BlockSpec((None, GROUP, WIN), lambda h: (h, 0, 0)),
                   out_shape=_sds((HEADS, GROUP, WIN), F32), compiler_params=_params("parallel"))(rel_bias)


def _rel_bias_fold(dbias):
    def body(a_ref, o_ref):
        diag = jnp.sum(_skew_rows(a_ref[...], left=True), axis=0, keepdims=True)
        onehot = (_diag_bin(lax.broadcasted_iota(jnp.int32, (WIN, REL_SIZE), 0))
                  == lax.broadcasted_iota(jnp.int32, (WIN, REL_SIZE), 1)).astype(MXU_DTYPE)
        rest = jnp.broadcast_to(diag, (8, WIN))
        out = jnp.zeros((8, REL_SIZE), F32)
        for _ in range(3):
            piece = rest.astype(BF16)
            out = out + _dot(piece, onehot, NN)
            rest = rest - piece.astype(F32)
        o_ref[...] = out[0:1]

    return _pallas(body, name="rel_bias_fold", grid=(HEADS,),
                   in_specs=[pl.BlockSpec((None, GROUP, WIN), lambda h: (h, 0, 0))],
                   out_specs=pl.BlockSpec((None, 1, REL_SIZE), lambda h: (h, 0, 0)),
                   out_shape=_sds((HEADS, 1, REL_SIZE), F32), compiler_params=_params("parallel"))(dbias)


def _place():
    return lax.axis_index("x"), lax.axis_index("y"), lax.axis_index("c")


def _other_chips(x, y):
    return [(1 - x, y), (x, 1 - y), (1 - x, 1 - y)]


class _Rider:
    reads, ins, new, n_sems = (), (), (), 1

    def start(self, reads, ins, new, send, recv):
        pass

    def mid(self, reads, ins, new, send, recv):
        pass

    def late(self, reads, ins, new, send, recv):
        pass

    def end(self, reads, ins, new, send, recv):
        pass


class _Riders:
    def __init__(self, riders, n_host_in, n_host_out):
        self.riders = list(riders)
        self.arrays, self.out_shapes, self.aliases, self.scratch = [], [], {}, []
        for r in self.riders:
            for t, a in enumerate(r.ins):
                self.aliases[n_host_in + len(self.arrays) + len(r.reads) + t] = n_host_out + len(self.out_shapes) + t
            self.arrays += [*r.reads, *r.ins]
            self.out_shapes += [_sds(a.shape, a.dtype) for a in r.ins] + list(r.new)
            self.scratch += [pltpu.SemaphoreType.DMA((r.n_sems,)), pltpu.SemaphoreType.DMA((r.n_sems,))]
        self.in_specs = [ANY] * len(self.arrays)
        self.out_specs = [ANY] * len(self.out_shapes)

    def bind(self, in_refs, out_refs, scratch_refs):
        bound, i, o = [], 0, 0
        for t, r in enumerate(self.riders):
            reads = in_refs[i:i + len(r.reads)]
            i += len(r.reads) + len(r.ins)
            ins = out_refs[o:o + len(r.ins)]
            new = out_refs[o + len(r.ins):o + len(r.ins) + len(r.new)]
            o += len(r.ins) + len(r.new)
            bound.append((reads, ins, new, scratch_refs[2 * t], scratch_refs[2 * t + 1]))
        return bound

    def run(self, phase, bound):
        for r, b in zip(self.riders, bound):
            getattr(r, phase)(*b)

    def split(self, outs):
        res, o = [], 0
        for r in self.riders:
            n = len(r.ins) + len(r.new)
            res.append(list(outs[o:o + n]))
            o += n
        return res


def _run_riders(name, riders):
    rs = _Riders(riders, 0, 0)
    n_in, n_out = len(rs.arrays), len(rs.out_shapes)

    def body(*refs):
        bound = rs.bind(refs[:n_in], refs[n_in:n_in + n_out], refs[n_in + n_out:])
        rs.run("start", bound)
        rs.run("mid", bound)
        rs.run("late", bound)
        rs.run("end", bound)

    outs = _pallas(body, name=name, in_specs=rs.in_specs, out_specs=rs.out_specs, out_shape=rs.out_shapes,
                   input_output_aliases=rs.aliases, scratch_shapes=rs.scratch)(*rs.arrays)
    return rs.split(outs)


class _GatherRider(_Rider):
    X_LINK, Y_LINK, Y_PASS, X_PASS, D2D_X, D2D_Y, D2D_DIAG, N_SEMS = 0, 1, 2, 3, 4, 5, 6, 7

    def __init__(self, blocks, part=(0, 1, 1)):
        self.ins = tuple(blocks)
        self.part = part
        self.n_sems = self.N_SEMS * len(blocks)

    def _copy(self, out, send, recv, w, sem, chip_from, cc, to, sub=None):
        hr = self.ins[w].shape[1] // 2
        lo, hi, n = self.part
        first, size = cc * hr + lo * (hr // n), (hi - lo) * (hr // n)
        if sub is not None:
            size //= 2
            first += sub * size
        piece = out[w].at[2 * chip_from[0] + chip_from[1], pl.ds(first, size), :]
        return pltpu.make_async_remote_copy(src_ref=piece, dst_ref=piece, send_sem=send.at[self.N_SEMS * w + sem],
                                            recv_sem=recv.at[self.N_SEMS * w + sem], device_id=to, device_id_type=MESH)

    def _sent(self, out, send, recv, w):
        x, y, c = _place()
        me, sib = (x, y), (x, y, 1 - c)
        xn, yn, diag = _other_chips(x, y)
        cp = functools.partial(self._copy, out, send, recv, w)
        return [("start", cp(self.X_LINK, me, c, (*xn, c))), ("start", cp(self.Y_LINK, me, c, (*yn, c))),
                ("mid_x", cp(self.D2D_X, xn, c, sib)), ("mid_x", cp(self.Y_PASS, xn, c, (*yn, c), sub=0)),
                ("mid_y", cp(self.D2D_Y, yn, c, sib)), ("mid_y", cp(self.X_PASS, yn, c, (*xn, c), sub=1)),
                ("late", cp(self.D2D_DIAG, diag, c, sib))]

    def _go(self, out, send, recv, phase):
        for w in range(len(self.ins)):
            for ph, copy in self._sent(out, send, recv, w):
                if ph == phase:
                    copy.start()

    def start(self, reads, out, new, send, recv):
        self._go(out, send, recv, "start")

    def mid(self, reads, out, new, send, recv):
        x, y, c = _place()
        xn, yn, _ = _other_chips(x, y)
        for w in range(len(self.ins)):
            self._copy(out, send, recv, w, self.X_LINK, xn, c, (x, y, c)).wait_recv()
        self._go(out, send, recv, "mid_x")
        for w in range(len(self.ins)):
            self._copy(out, send, recv, w, self.Y_LINK, yn, c, (x, y, c)).wait_recv()
        self._go(out, send, recv, "mid_y")

    def late(self, reads, out, new, send, recv):
        x, y, c = _place()
        diag = _other_chips(x, y)[2]
        for w in range(len(self.ins)):
            self._copy(out, send, recv, w, self.Y_PASS, diag, c, (x, y, c), sub=0).wait_recv()
            self._copy(out, send, recv, w, self.X_PASS, diag, c, (x, y, c), sub=1).wait_recv()
        self._go(out, send, recv, "late")

    def end(self, reads, out, new, send, recv):
        x, y, c = _place()
        xn, yn, diag = _other_chips(x, y)
        for w in range(len(self.ins)):
            for sem, chip in ((self.D2D_X, xn), (self.D2D_Y, yn), (self.D2D_DIAG, diag)):
                self._copy(out, send, recv, w, sem, chip, 1 - c, (x, y, c)).wait_recv()
        for w in range(len(self.ins)):
            for _, copy in self._sent(out, send, recv, w):
                copy.wait_send()


class _SwapRider(_Rider):
    def __init__(self, grads):
        self.reads = tuple(grads)
        self.new = tuple(_sds((N_CHIPS, g.shape[1] // 2, g.shape[2]), g.dtype) for g in grads)
        self.n_sems = len(grads)

    def _copies(self, src, new, send, recv):
        x, y, c = _place()
        copies = []
        for w in range(len(self.reads)):
            hr = self.reads[w].shape[1] // 2
            copies.append(pltpu.make_async_remote_copy(
                src_ref=src[w].at[:, pl.ds((1 - c) * hr, hr), :], dst_ref=new[w],
                send_sem=send.at[w], recv_sem=recv.at[w], device_id=(x, y, 1 - c), device_id_type=MESH))
        return copies

    def start(self, src, ins, new, send, recv):
        for cp in self._copies(src, new, send, recv):
            cp.start()

    def end(self, src, ins, new, send, recv):
        for cp in self._copies(src, new, send, recv):
            cp.wait()


def _add_half(g, got, c_arr, name):
    nk, r, cols = g.shape
    hr = r // 2
    tr = min(hr, 256)
    nb = hr // tr

    def body(c_ref, g_ref, got_ref, o_ref):
        o_ref[...] = (g_ref[...].astype(F32) + got_ref[...].astype(F32)).astype(o_ref.dtype)

    grid_spec = pltpu.PrefetchScalarGridSpec(
        num_scalar_prefetch=1, grid=(nk, nb),
        in_specs=[pl.BlockSpec((None, tr, cols), lambda k, i, c_ref: (k, c_ref[0] * nb + i, 0)),
                  pl.BlockSpec((None, tr, cols), lambda k, i, c_ref: (k, i, 0))],
        out_specs=pl.BlockSpec((None, tr, cols), lambda k, i, c_ref: (k, i, 0)))
    return _pallas(body, name=name, grid_spec=grid_spec, out_shape=_sds((nk, hr, cols), g.dtype),
                   compiler_params=_params("parallel", "parallel"))(c_arr, g, got)


class _SendPartialsRider(_Rider):
    def __init__(self, parts, got=None, part=(0, 1, 1)):
        self.reads = tuple(parts)
        if got is None:
            self.new = tuple(_sds((N_CHIPS - 1, *p.shape[1:]), p.dtype) for p in parts)
        else:
            self.ins = tuple(got)
        self.part = part
        self.n_sems = 3 * len(parts)

    def _copies(self, src, ins, new, send, recv):
        x, y, c = _place()
        land = ins if self.ins else new
        lo, hi, n = self.part
        copies = []
        for w in range(len(self.reads)):
            pr = self.reads[w].shape[1] // n
            rows = pl.ds(lo * pr, (hi - lo) * pr)
            for j, chip in enumerate(_other_chips(x, y)):
                copies.append(pltpu.make_async_remote_copy(
                    src_ref=src[w].at[2 * chip[0] + chip[1], rows, :], dst_ref=land[w].at[j, rows, :],
                    send_sem=send.at[3 * w + j], recv_sem=recv.at[3 * w + j], device_id=(*chip, c), device_id_type=MESH))
        return copies

    def start(self, src, ins, new, send, recv):
        for cp in self._copies(src, ins, new, send, recv):
            cp.start()

    def end(self, src, ins, new, send, recv):
        for cp in self._copies(src, ins, new, send, recv):
            cp.wait()


def _sum_partials(part, got, kc_arr, name):
    _, hr, cols = part.shape
    tr = min(hr, 256)
    nb = hr // tr

    def body(kc_ref, p_ref, g0_ref, g1_ref, g2_ref, o_ref):
        o_ref[...] = ((p_ref[...].astype(F32) + g0_ref[...].astype(F32)) + g1_ref[...].astype(F32)) + g2_ref[...].astype(F32)

    slot = lambda j: pl.BlockSpec((None, tr, cols), lambda i, kc_ref: (j, i, 0))
    grid_spec = pltpu.PrefetchScalarGridSpec(
        num_scalar_prefetch=1, grid=(nb,),
        in_specs=[pl.BlockSpec((None, tr, cols), lambda i, kc_ref: (kc_ref[0], i, 0)), slot(0), slot(1), slot(2)],
        out_specs=pl.BlockSpec((tr, cols), lambda i, kc_ref: (kc_ref[1] * nb + i, 0)))
    return _pallas(body, name=name, grid_spec=grid_spec, out_shape=_sds((2 * hr, cols), F32),
                   compiler_params=_params("parallel"))(kc_arr, part, got, got, got)


class _ShareRider(_Rider):
    def __init__(self, grads):
        self.ins = tuple(grads)
        self.n_sems = len(grads)

    def _copies(self, out, send, recv):
        x, y, c = _place()
        copies = []
        for w in range(len(self.ins)):
            hr = self.ins[w].shape[0] // 2
            mine = out[w].at[pl.ds(c * hr, hr), :]
            copies.append(pltpu.make_async_remote_copy(
                src_ref=mine, dst_ref=mine, send_sem=send.at[w], recv_sem=recv.at[w],
                device_id=(x, y, 1 - c), device_id_type=MESH))
        return copies

    def start(self, reads, out, new, send, recv):
        for cp in self._copies(out, send, recv):
            cp.start()

    def end(self, reads, out, new, send, recv):
        for cp in self._copies(out, send, recv):
            cp.wait()


def _small_allreduce_adamw(g_part, w, m, v):
    rows = g_part.shape[0]

    def body(g_ref, w_ref, m_ref, v_ref, go_ref, d_ref, nm_ref, nv_ref, all_ref, send_sems, recv_sems):
        x, y, c = _place()
        me = 4 * x + 2 * y + c
        all_ref[me] = g_ref[...]
        copies = []
        for r in range(1, 8):
            dx, dy, dc = (r >> 2) & 1, (r >> 1) & 1, r & 1
            peer = (1 - x if dx else x, 1 - y if dy else y, 1 - c if dc else c)
            copies.append(pltpu.make_async_remote_copy(
                src_ref=g_ref, dst_ref=all_ref.at[me], send_sem=send_sems.at[r - 1], recv_sem=recv_sems.at[r - 1],
                device_id=peer, device_id_type=MESH))
        for cp in copies:
            cp.start()
        for cp in copies:
            cp.wait()
        tot = all_ref[0]
        for d in range(1, 8):
            tot = tot + all_ref[d]
        go_ref[...] = tot
        d_ref[...], nm_ref[...], nv_ref[...] = _adamw_math(w_ref[...], tot, m_ref[...], v_ref[...])

    vm = pl.BlockSpec(memory_space=pltpu.VMEM)
    return _pallas(
        body, name="small_allreduce_adamw", in_specs=[vm] * 4, out_specs=[vm] * 4,
        out_shape=[_sds((rows, 128), F32)] * 4,
        scratch_shapes=[pltpu.VMEM((8, rows, 128), F32), pltpu.SemaphoreType.DMA((7,)), pltpu.SemaphoreType.DMA((7,))],
    )(g_part, w, m, v)


SMALL_SIZES = (2048, 1024, 128, 128, HEADS * REL_SIZE, 2048)
SMALL_PART_ROWS = tuple(-(-size // 1024) * 8 for size in SMALL_SIZES)
SMALL_ROWS = sum(SMALL_PART_ROWS)


def _pack_small(parts):
    rows = []
    for p, size, nr in zip(parts, SMALL_SIZES, SMALL_PART_ROWS):
        rows.append(jnp.pad(p.reshape(-1), (0, nr * 128 - size)).reshape(nr, 128))
    return jnp.concatenate(rows, axis=0)


def _unpack_small(slab, shapes):
    out, off = [], 0
    for size, nr, shape in zip(SMALL_SIZES, SMALL_PART_ROWS, shapes):
        out.append(slab[off:off + nr].reshape(-1)[:size].reshape(shape))
        off += nr
    return out


def kernel(x, norm1_g, w_in, ret_norm_g, q_norm_g, k_norm_g, rel_bias, w_out, norm2_g, w_ff1, w_ff2, loss_target, m_norm1_g, m_w_in, m_ret_norm_g, m_q_norm_g, m_k_norm_g, m_rel_bias, m_w_out, m_norm2_g, m_w_ff1, m_w_ff2, v_norm1_g, v_w_in, v_ret_norm_g, v_q_norm_g, v_k_norm_g, v_rel_bias, v_w_out, v_norm2_g, v_w_ff1, v_w_ff2):
    xs = x[0]
    tgt = loss_target[0]
    s, d = xs.shape
    d_in = N_CHIPS * w_in.shape[2]
    d_ff = N_CHIPS * w_ff1.shape[2]
    in_sh, ff_sh = w_in.shape[2], w_ff1.shape[2]
    tm = min(s, 1024)
    gi = s // tm
    c_arr = lax.axis_index("c").astype(jnp.int32).reshape(1)
    k_arr = (2 * lax.axis_index("x") + lax.axis_index("y")).astype(jnp.int32).reshape(1)
    tables = _tables(s)
    bias = _rel_bias_expand(rel_bias[0])

    blk_in, blk_out, blk_ff1, blk_ff2 = (
        _cast_bf16(w_in[0], k_arr, "cast_w_in"), _cast_bf16(w_out[0], k_arr, "cast_w_out"),
        _cast_bf16(w_ff1[0], k_arr, "cast_w_ff1"), _cast_bf16(w_ff2[0], k_arr, "cast_w_ff2"))
    ((wg_in,),) = _run_riders("all_gather_w_in", [_GatherRider([blk_in])])

    h1 = _rmsnorm_fwd(xs, norm1_g, "rmsnorm1")
    tn_in = in_sh // 2
    tk = d
    (proj,), ((wg_ff1,),) = _mm(
        "proj", h1, wg_in, NN, (gi, 2 * N_CHIPS, d // tk),
        pl.BlockSpec((tm, tk), lambda i, j, k: (i, k)),
        pl.BlockSpec((None, tk, tn_in), lambda i, j, k: (j // 2, k, j % 2)),
        [_sds((s, d_in), F32)], [pl.BlockSpec((tm, tn_in), lambda i, j, k: (i, j))], (tm, tn_in),
        riders=[_GatherRider([blk_ff1], (0, 3, 4))])
    (mix, y_ret, prev), ((wg_ff1,),) = _retention_fwd(proj, ret_norm_g, tables, riders=[_GatherRider([wg_ff1], (3, 4, 4))])
    mix, ((wg_out,), (wg_ff2,)) = _attention_fwd(
        proj, q_norm_g, k_norm_g, bias, mix, riders=[_GatherRider([blk_out]), _GatherRider([blk_ff2], (0, 1, 4))])
    wg_out = wg_out.reshape(d, d)
    tn = 1024
    tile = pl.BlockSpec((tm, tn), lambda i, j, k: (i, j))
    def residual_norm(acc, res, g):
        x1v = res + acc
        yv = x1v * lax.rsqrt(jnp.mean(x1v * x1v, axis=-1, keepdims=True) + EPS)
        return x1v, yv * g

    tmo = min(s, 512)
    rows = pl.BlockSpec((tmo, d), lambda i, j, k: (i, 0))
    x1, h2 = _mm("out_proj", mix, wg_out, NN, (s // tmo, 1, 1),
                 rows, pl.BlockSpec((d, d), lambda i, j, k: (0, 0)),
                 [_sds((s, d), F32), _sds((s, d), BF16)], [rows, rows], (tmo, d),
                 extras=(xs, norm2_g), extra_specs=(rows, pl.BlockSpec((1, d), lambda i, j, k: (0, 0))),
                 epi=residual_norm)
    tn_ff = min(ff_sh, 1024)
    per = ff_sh // tn_ff

    def relu2(acc):
        r = jnp.maximum(acc, 0.0)
        return acc, r * r

    (u, act), ((wg_ff2,),) = _mm(
        "ff1", h2, wg_ff1, NN, (gi, N_CHIPS * per, d // tk),
        pl.BlockSpec((tm, tk), lambda i, j, k: (i, k)),
        pl.BlockSpec((None, tk, tn_ff), lambda i, j, k: (j // per, k, j % per)),
        [_sds((s, d_ff), F32), _sds((s, d_ff), BF16)],
        [pl.BlockSpec((tm, tn_ff), lambda i, j, k: (i, j))] * 2, (tm, tn_ff), epi=relu2,
        riders=[_GatherRider([wg_ff2], (1, 4, 4))])
    wg_ff2 = wg_ff2.reshape(d_ff, d)

    def loss_epi(acc, res, t):
        diff = (res + acc) - t
        dy = diff / d
        return dy, dy, jnp.sum(diff * diff, axis=0, keepdims=True)

    tk2 = min(tk, 2048)
    dy, dyb, loss_cols = _mm(
        "ff2_loss", act, wg_ff2, NN, (gi, d // tn, d_ff // tk2),
        pl.BlockSpec((tm, tk2), lambda i, j, k: (i, k)), pl.BlockSpec((tk2, tn), lambda i, j, k: (k, j)),
        [_sds((s, d), F32), _sds((s, d), BF16), _sds((gi, 1, d), F32)],
        [tile, tile, pl.BlockSpec((None, 1, tn), lambda i, j, k: (i, 0, j))], (tm, tn),
        extras=(x1, tgt), extra_specs=(tile, tile), epi=loss_epi)
    loss = lax.psum(0.5 * jnp.sum(loss_cols) / d, ("x", "y", "c"))

    (du,) = _mm("d_act", dyb, wg_ff2, NT, (gi, d_ff // tn, d // tk),
                pl.BlockSpec((tm, tk), lambda i, j, k: (i, k)), pl.BlockSpec((tn, tk), lambda i, j, k: (j, k)),
                [_sds((s, d_ff), BF16)], [tile], (tm, tn), extras=(u,), extra_specs=(tile,),
                epi=lambda acc, uu: (acc * (2.0 * jnp.maximum(uu, 0.0)),))
    ts = min(s, 2048)
    wtile = pl.BlockSpec((tn, tn), lambda i, j, k: (i, j))
    (g_ff2,) = _mm("dw_ff2", act, dyb, TN, (d_ff // tn, d // tn, s // ts),
                   pl.BlockSpec((ts, tn), lambda i, j, k: (k, i)), pl.BlockSpec((ts, tn), lambda i, j, k: (k, j)),
                   [_sds((d_ff, d), BF16)], [wtile], (tn, tn))
    g_ff2 = g_ff2.reshape(N_CHIPS, d_ff // N_CHIPS, d)
    (g_ff1,), ((got_ff2,),) = _mm(
        "dw_ff1", h2, du, TN, (d // tn, N_CHIPS * per, s // ts),
        pl.BlockSpec((ts, tn), lambda i, j, k: (k, i)), pl.BlockSpec((ts, tn_ff), lambda i, j, k: (k, j)),
        [_sds((N_CHIPS, d, ff_sh), BF16)],
        [pl.BlockSpec((None, tn, tn_ff), lambda i, j, k: (j // per, i, j % per))], (tn, tn_ff),
        riders=[_SwapRider([g_ff2])])
    p_ff2 = _add_half(g_ff2, got_ff2, c_arr, "chip_partial_w_ff2")
    tkf = min(tk, ff_sh)
    kper = ff_sh // tkf
    (dh2,), ((got2_ff2,), (got_ff1,)) = _mm(
        "d_h2", du, wg_ff1, NT, (gi, d // tn, d_ff // tkf),
        pl.BlockSpec((tm, tkf), lambda i, j, k: (i, k)),
        pl.BlockSpec((None, tn, tkf), lambda i, j, k: (k // kper, j, k % kper)),
        [_sds((s, d), F32)], [tile], (tm, tn),
        riders=[_SendPartialsRider([p_ff2], part=(0, 3, 4)), _SwapRider([g_ff1])])
    p_ff1 = _add_half(g_ff1, got_ff1, c_arr, "chip_partial_w_ff1")
    dx1, dx1b, g_norm2 = _rmsnorm_bwd(x1, norm2_g, dh2, dy, "rmsnorm2_bwd")

    (dmix,) = _mm("d_mix", dx1b, wg_out, NT, (gi, d // tn, d // tk),
                  pl.BlockSpec((tm, tk), lambda i, j, k: (i, k)), pl.BlockSpec((tn, tk), lambda i, j, k: (j, k)),
                  [_sds((s, d), F32)], [tile], (tm, tn))
    (g_out,) = _mm("dw_out", mix, dx1b, TN, (d // tn, d // tn, s // ts),
                   pl.BlockSpec((ts, tn), lambda i, j, k: (k, i)), pl.BlockSpec((ts, tn), lambda i, j, k: (k, j)),
                   [_sds((d, d), BF16)], [wtile], (tn, tn))
    g_out = g_out.reshape(N_CHIPS, d // N_CHIPS, d)
    ((got_out,),) = _run_riders("grad_swap_w_out", [_SwapRider([g_out])])
    p_out = _add_half(g_out, got_out, c_arr, "chip_partial_w_out")
    (d_rq, d_rk, d_rv, d_rg, g_gn), ((got2_ff2,), (got2_ff1,)) = _retention_bwd(
        proj, ret_norm_g, tables, y_ret, prev, dmix,
        riders=[_SendPartialsRider([p_ff2], got=[got2_ff2], part=(3, 4, 4)), _SendPartialsRider([p_ff1], part=(0, 2, 4))])
    (d_aq, d_ak, d_av, g_gq, g_gk, dbias), ((got2_ff1,), (got2_out,)) = _attention_bwd(
        proj, q_norm_g, k_norm_g, bias, dmix,
        riders=[_SendPartialsRider([p_ff1], got=[got2_ff1], part=(2, 4, 4)), _SendPartialsRider([p_out])])
    g_rel = _rel_bias_fold(dbias)
    dproj = jnp.concatenate([d_rq, d_rk, d_rv, d_rg, d_aq, d_ak, d_av], axis=-1)
    (g_in,) = _mm("dw_in", h1, dproj, TN, (d // tn, 2 * N_CHIPS, s // ts),
                  pl.BlockSpec((ts, tn), lambda i, j, k: (k, i)), pl.BlockSpec((ts, tn_in), lambda i, j, k: (k, j)),
                  [_sds((N_CHIPS, d, in_sh), BF16)],
                  [pl.BlockSpec((None, tn, tn_in), lambda i, j, k: (j // 2, i, j % 2))], (tn, tn_in))
    ((got_in,),) = _run_riders("grad_swap_w_in", [_SwapRider([g_in])])
    p_in = _add_half(g_in, got_in, c_arr, "chip_partial_w_in")
    (dh1,), ((got2_in,),) = _mm(
        "d_h1", dproj, wg_in, NT, (gi, d // tn, N_CHIPS),
        pl.BlockSpec((tm, in_sh), lambda i, j, k: (i, k)),
        pl.BlockSpec((None, tn, in_sh), lambda i, j, k: (k, j, 0)),
        [_sds((s, d), F32)], [tile], (tm, tn), riders=[_SendPartialsRider([p_in], part=(0, 3, 4))])
    (grad_x, _, g_norm1), ((got2_in,),) = _rmsnorm_bwd(
        xs, norm1_g, dh1, dx1, "rmsnorm1_bwd", riders=[_SendPartialsRider([p_in], got=[got2_in], part=(3, 4, 4))])

    names = ["w_in", "w_out", "w_ff1", "w_ff2"]
    kc_arr = jnp.concatenate([k_arr, c_arr])
    halves = [_sum_partials(p, r, kc_arr, "sum_partials_" + nm)
              for p, r, nm in zip((p_in, p_out, p_ff1, p_ff2), (got2_in, got2_out, got2_ff1, got2_ff2), names)]
    (g_big,) = _run_riders("grad_share_halves", [_ShareRider(halves)])
    big = []
    for g, w, m, v, nm in zip(g_big, (w_in, w_out, w_ff1, w_ff2), (m_w_in, m_w_out, m_w_ff1, m_w_ff2),
                              (v_w_in, v_w_out, v_w_ff1, v_w_ff2), names):
        g, delta, new_m, new_v = _adamw(w[0], g, m[0], v[0], "adamw_" + nm)
        big.append((g[None], delta[None], new_m[None], new_v[None]))

    small_w = (norm1_g, ret_norm_g, q_norm_g, k_norm_g, rel_bias, norm2_g)
    small_m = (m_norm1_g, m_ret_norm_g, m_q_norm_g, m_k_norm_g, m_rel_bias, m_norm2_g)
    small_v = (v_norm1_g, v_ret_norm_g, v_q_norm_g, v_k_norm_g, v_rel_bias, v_norm2_g)
    shapes = [p.shape for p in small_w]
    g_small = _pack_small([g_norm1, g_gn, g_gq, g_gk, g_rel, g_norm2])
    sg, sd, sm, sv = (_unpack_small(a, shapes) for a in _small_allreduce_adamw(
        g_small, _pack_small(small_w), _pack_small(small_m), _pack_small(small_v)))

    def ordered(kind):
        sm_ = (sg, sd, sm, sv)[kind]
        return (sm_[0], big[0][kind], sm_[1], sm_[2], sm_[3], sm_[4], big[1][kind], sm_[5], big[2][kind], big[3][kind])

    return (loss, grad_x[None], *ordered(0), *ordered(1), *ordered(2), *ordered(3))
```

```python
import functools

import jax
import jax.numpy as jnp
from jax import lax
from jax.experimental import pallas as pl
from jax.experimental.pallas import tpu as pltpu

F32 = jnp.float32
BF16 = jnp.bfloat16
MXU_DTYPE = jnp.bfloat16

CHUNK = 64
HEADS = 8
HEAD_DIM = 128
LEFT_CHUNKS = 8
BAND = (LEFT_CHUNKS + 1) * CHUNK
REL_CLIP = 128
REL_SIZE = (CHUNK - 1) + REL_CLIP + 1
RET_BLOCK_CHUNKS = 8
RET_ROWS = RET_BLOCK_CHUNKS * CHUNK
RET_SUB = 256
ROPE_BASE = 10000.0
EPS = 1e-6
GN_EPS = 1e-5
ADAM_LR, ADAM_B1, ADAM_B2, ADAM_EPS, ADAM_WD, ADAM_STEP = 0.001, 0.9, 0.999, 1e-08, 0.01, 10
N_CHIPS = 4
VMEM_LIMIT = 56 * 1024 * 1024
MESH = pl.DeviceIdType.MESH
ANY = pl.BlockSpec(memory_space=pl.ANY)

NN = (((1,), (0,)), ((), ()))
NT = (((1,), (1,)), ((), ()))
TN = (((0,), (0,)), ((), ()))


def _pallas(body, **kw):
    return pl.pallas_call(body, **kw)


def _params(*sem):
    return pltpu.CompilerParams(dimension_semantics=sem, vmem_limit_bytes=VMEM_LIMIT)


def _dot(a, b, dims):
    return lax.dot_general(a.astype(MXU_DTYPE), b.astype(MXU_DTYPE), dims, preferred_element_type=F32)


RIDER_MID, RIDER_LATE = 0.6, 0.85


def _mm(name, a, b, dims, grid, a_spec, b_spec, outs, o_specs, acc_shape, extras=(), extra_specs=(), epi=None,
        riders=()):
    ni, nj, nk = grid
    n_ex, n_out = len(extras), len(outs)
    n_in = 2 + n_ex
    rs = _Riders(riders, n_in, n_out)
    n_rin, n_rout = len(rs.arrays), len(rs.out_shapes)
    steps = ni * nj * nk

    def body(*refs):
        a_ref, b_ref = refs[0], refs[1]
        ex_refs = refs[2:n_in]
        o_refs = refs[n_in + n_rin:n_in + n_rin + n_out]
        acc_ref = refs[n_in + n_rin + n_out + n_rout]
        k = pl.program_id(2)
        if riders:
            bound = rs.bind(refs[n_in:n_in + n_rin], refs[n_in + n_rin + n_out:n_in + n_rin + n_out + n_rout],
                            refs[n_in + n_rin + n_out + n_rout + 1:])
            step = (pl.program_id(0) * nj + pl.program_id(1)) * nk + k
            pl.when(step == 0)(lambda: rs.run("start", bound))
            pl.when(step == int(steps * RIDER_MID))(lambda: rs.run("mid", bound))
            pl.when(step == int(steps * RIDER_LATE))(lambda: rs.run("late", bound))

        def finish(acc):
            vals = epi(acc, *[r[...] for r in ex_refs]) if epi is not None else (acc,)
            for r, v in zip(o_refs, vals):
                r[...] = v.astype(r.dtype)

        if nk == 1:
            finish(_dot(a_ref[...], b_ref[...], dims))
        else:
            @pl.when(k == 0)
            def _():
                acc_ref[...] = jnp.zeros_like(acc_ref)

            acc_ref[...] += _dot(a_ref[...], b_ref[...], dims)
            pl.when(k == nk - 1)(lambda: finish(acc_ref[...]))

        if riders:
            pl.when(step == steps - 1)(lambda: rs.run("end", bound))

    res = _pallas(
        body, name=name, grid=grid, in_specs=[a_spec, b_spec, *extra_specs, *rs.in_specs],
        out_specs=[*o_specs, *rs.out_specs], out_shape=[*outs, *rs.out_shapes], input_output_aliases=rs.aliases,
        scratch_shapes=[pltpu.VMEM(acc_shape if nk > 1 else (8, 128), F32), *rs.scratch],
        compiler_params=_params(*(("arbitrary",) * 3 if riders else ("parallel", "parallel", "arbitrary"))),
    )(a, b, *extras, *rs.arrays)
    return (res[:n_out], rs.split(res[n_out:])) if riders else res


def _sds(shape, dtype):
    return jax.ShapeDtypeStruct(shape, dtype)


def _cast_bf16(w, k_arr, name):
    r, c = w.shape
    tr = min(r, 256)

    def body(k_ref, w_ref, o_ref):
        o_ref[...] = w_ref[...].astype(BF16)

    grid_spec = pltpu.PrefetchScalarGridSpec(
        num_scalar_prefetch=1, grid=(r // tr,), in_specs=[pl.BlockSpec((tr, c), lambda i, k_ref: (i, 0))],
        out_specs=pl.BlockSpec((None, tr, c), lambda i, k_ref: (k_ref[0], i, 0)))
    return _pallas(body, name=name, grid_spec=grid_spec, out_shape=_sds((N_CHIPS, r, c), BF16),
                   compiler_params=_params("parallel"))(k_arr, w)


def _rmsnorm_fwd(x, g, name):
    s, d = x.shape
    tr = 256

    def body(x_ref, g_ref, o_ref):
        xv = x_ref[...]
        y = xv * lax.rsqrt(jnp.mean(xv * xv, axis=-1, keepdims=True) + EPS)
        o_ref[...] = (y * g_ref[...]).astype(o_ref.dtype)

    return _pallas(body, name=name, grid=(s // tr,),
                   in_specs=[pl.BlockSpec((tr, d), lambda i: (i, 0)), pl.BlockSpec((1, d), lambda i: (0, 0))],
                   out_specs=pl.BlockSpec((tr, d), lambda i: (i, 0)), out_shape=_sds((s, d), BF16),
                   compiler_params=_params("parallel"))(x, g)


def _rmsnorm_bwd(x, g, dh, res, name, riders=()):
    s, d = x.shape
    tr = 256

    def body(x_ref, g_ref, dh_ref, res_ref, dx_ref, dxb_ref, dg_ref):
        i = pl.program_id(0)
        xv = x_ref[...]
        rstd = lax.rsqrt(jnp.mean(xv * xv, axis=-1, keepdims=True) + EPS)
        xh = xv * rstd
        dhv = dh_ref[...]

        @pl.when(i == 0)
        def _():
            dg_ref[...] = jnp.zeros_like(dg_ref)

        dg_ref[...] += jnp.sum(dhv * xh, axis=0, keepdims=True)
        dxh = dhv * g_ref[...]
        dx = res_ref[...] + rstd * (dxh - xh * jnp.mean(dxh * xh, axis=-1, keepdims=True))
        dx_ref[...] = dx
        dxb_ref[...] = dx.astype(BF16)

    row = pl.BlockSpec((tr, d), lambda i: (i, 0))
    vec = pl.BlockSpec((1, d), lambda i: (0, 0))
    rs = _Riders(riders, 4, 3)
    out = _pallas(_with_riders(body, 4, 3, 0, rs, (s // tr,)), name=name, grid=(s // tr,),
                  in_specs=[row, vec, row, row, *rs.in_specs], out_specs=[row, row, vec, *rs.out_specs],
                  out_shape=[_sds((s, d), F32), _sds((s, d), BF16), _sds((1, d), F32), *rs.out_shapes],
                  input_output_aliases=rs.aliases, scratch_shapes=rs.scratch,
                  compiler_params=_params("arbitrary"))(x, g, dh, res, *rs.arrays)
    return (out[:3], rs.split(out[3:])) if riders else out


def _adamw_math(w, g, m, v):
    m = ADAM_B1 * m + (1.0 - ADAM_B1) * g
    v = ADAM_B2 * v + (1.0 - ADAM_B2) * (g * g)
    m_hat = m / (1.0 - ADAM_B1 ** ADAM_STEP)
    v_hat = v / (1.0 - ADAM_B2 ** ADAM_STEP)
    delta = -ADAM_LR * (m_hat / (jnp.sqrt(v_hat) + ADAM_EPS) + ADAM_WD * w)
    return delta, m, v


def _adamw(w, g, m, v, name):
    r, c = w.shape
    tr = 128

    def body(w_ref, g_ref, m_ref, v_ref, go_ref, d_ref, nm_ref, nv_ref):
        g = g_ref[...]
        go_ref[...] = g
        d_ref[...], nm_ref[...], nv_ref[...] = _adamw_math(w_ref[...], g, m_ref[...], v_ref[...])

    blk = pl.BlockSpec((tr, c), lambda i: (i, 0))
    return _pallas(body, name=name, grid=(r // tr,), in_specs=[blk] * 4, out_specs=[blk] * 4,
                   out_shape=[_sds((r, c), F32)] * 4, compiler_params=_params("parallel"))(w, g, m, v)


def _tables(s):
    half = HEAD_DIM // 2
    pos = jnp.arange(s, dtype=F32)
    inv_freq = ROPE_BASE ** (-jnp.arange(half, dtype=F32) / half)
    ang = pos[:, None] * inv_freq[None, :]
    cos, sin = jnp.cos(ang), jnp.sin(ang)
    cos_f = jnp.concatenate([cos, cos], axis=-1)
    sin_f = jnp.concatenate([-sin, sin], axis=-1)
    log_g = jnp.log1p(-jnp.exp2(-(5.0 + jnp.arange(HEADS, dtype=F32))))
    p = jnp.arange(CHUNK, dtype=F32)
    decay = jnp.exp(log_g[:, None, None] * jnp.abs(p[:, None] - p[None, :]))
    k_dec = jnp.exp(log_g[None, :] * (CHUNK - 1.0 - p)[:, None])
    q_dec = jnp.exp(log_g[None, :] * (p + 1.0)[:, None])
    c_dec = jnp.exp(log_g * CHUNK)
    k_dec = jnp.tile(jnp.broadcast_to(k_dec.T[:, :, None], (HEADS, CHUNK, HEAD_DIM)), (1, RET_BLOCK_CHUNKS, 1))
    q_dec = jnp.tile(jnp.broadcast_to(q_dec.T[:, :, None], (HEADS, CHUNK, HEAD_DIM)), (1, RET_BLOCK_CHUNKS, 1))
    c_dec = jnp.broadcast_to(c_dec[:, None, None], (HEADS, 1, HEAD_DIM))
    n = RET_SUB // CHUNK
    decay = (jnp.eye(n, dtype=F32)[None, :, None, :, None] * decay[:, None, :, None, :]).reshape(HEADS, RET_SUB, RET_SUB)
    return cos_f, sin_f, decay, k_dec, q_dec, c_dec


def _rot(x, cos_f, sin_f):
    return x * cos_f + pltpu.roll(x, HEAD_DIM // 2, 1) * sin_f


def _rot_bwd(d, cos_f, sin_f):
    return d * cos_f + pltpu.roll(d * sin_f, HEAD_DIM // 2, 1)


K_SCALE = HEAD_DIM ** -0.5


def _retention_fwd(proj, gn_g, tables, riders=()):
    s = proj.shape[0]
    nb = s // RET_ROWS
    nc = s // CHUNK
    cos_f, sin_f, decay, k_dec, q_dec, c_dec = tables

    def body(q_ref, k_ref, v_ref, g_ref, cos_ref, sin_ref, dec_ref, kd_ref, qd_ref, cd_ref, gn_ref,
             ret_ref, y_ref, prev_ref, state_ref):
        @pl.when(pl.program_id(1) == 0)
        def _():
            state_ref[...] = jnp.zeros_like(state_ref)

        cosv, sinv = cos_ref[...], sin_ref[...]
        q = _rot(q_ref[...], cosv, sinv)
        k = _rot(k_ref[...], cosv, sinv) * K_SCALE
        v = v_ref[...]
        rg = g_ref[...]
        dec, cd, gn = dec_ref[...], cd_ref[...], gn_ref[...]
        kdf, qdf = k * kd_ref[...], q * qd_ref[...]
        chunks = [slice(c * CHUNK, (c + 1) * CHUNK) for c in range(RET_BLOCK_CHUNKS)]
        contribs = [_dot(kdf[rows], v[rows], TN) for rows in chunks]
        state, states = state_ref[...], []
        for c in range(RET_BLOCK_CHUNKS):
            states.append(state)
            prev_ref[c] = state.astype(prev_ref.dtype)
            state = cd * state + contribs[c]
        state_ref[...] = state
        cross = jnp.concatenate([_dot(qdf[rows], st, NN) for rows, st in zip(chunks, states)], axis=0)
        intra = []
        for b in range(RET_ROWS // RET_SUB):
            rows = slice(b * RET_SUB, (b + 1) * RET_SUB)
            intra.append(_dot(_dot(q[rows], k[rows], NT) * dec, v[rows], NN))
        y = jnp.concatenate(intra, axis=0) + cross
        y_ref[...] = y
        mu = jnp.mean(y, axis=-1, keepdims=True)
        yc = y - mu
        var = jnp.mean(yc * yc, axis=-1, keepdims=True)
        yn = yc * lax.rsqrt(var + GN_EPS) * gn
        ret_ref[...] = (rg * jax.nn.sigmoid(rg) * yn).astype(ret_ref.dtype)

    def col(off):
        return pl.BlockSpec((RET_ROWS, HEAD_DIM), lambda h, i: (i, off + h))

    pos = pl.BlockSpec((RET_ROWS, HEAD_DIM), lambda h, i: (i, 0))
    per_head = lambda shape: pl.BlockSpec((None, *shape), lambda h, i: (h, 0, 0))
    rs = _Riders(riders, 11, 3)
    res = _pallas(
        _with_riders(body, 11, 3, 1, rs, (HEADS, nb)), name="retention_fwd", grid=(HEADS, nb),
        in_specs=[col(0), col(HEADS), col(2 * HEADS), col(3 * HEADS), pos, pos,
                  per_head((RET_SUB, RET_SUB)), per_head((RET_ROWS, HEAD_DIM)), per_head((RET_ROWS, HEAD_DIM)),
                  per_head((1, HEAD_DIM)), pl.BlockSpec((1, HEAD_DIM), lambda h, i: (0, h)), *rs.in_specs],
        out_specs=[col(0), col(0),
                   pl.BlockSpec((None, RET_BLOCK_CHUNKS, HEAD_DIM, HEAD_DIM), lambda h, i: (h, i, 0, 0)),
                   *rs.out_specs],
        out_shape=[_sds((s, 2 * HEADS * HEAD_DIM), BF16), _sds((s, HEADS * HEAD_DIM), F32),
                   _sds((HEADS, nc, HEAD_DIM, HEAD_DIM), MXU_DTYPE), *rs.out_shapes],
        input_output_aliases=rs.aliases,
        scratch_shapes=[pltpu.VMEM((HEAD_DIM, HEAD_DIM), F32), *rs.scratch],
        compiler_params=_params("arbitrary", "arbitrary"),
    )(proj, proj, proj, proj, cos_f, sin_f, decay, k_dec, q_dec, c_dec, gn_g, *rs.arrays)
    return res[:3], rs.split(res[3:])


def _retention_bwd(proj, gn_g, tables, y, prev, dmix, riders=()):
    s = proj.shape[0]
    nb = s // RET_ROWS
    cos_f, sin_f, decay, k_dec, q_dec, c_dec = tables

    def body(q_ref, k_ref, v_ref, g_ref, cos_ref, sin_ref, dec_ref, kd_ref, qd_ref, cd_ref, gn_ref,
             y_ref, prev_ref, dret_ref, dproj_ref, dgn_ref, gstate_ref, stage_ref, stage_sems):
        head, blk = pl.program_id(0), pl.program_id(1)
        step = head * nb + blk
        slot = step % 2

        def writes(sl):
            rows = pl.ds(pl.multiple_of((nb - 1 - blk) * RET_ROWS, RET_ROWS), RET_ROWS)
            return [pltpu.make_async_copy(
                stage_ref.at[sl, g], dproj_ref.at[rows, pl.ds(pl.multiple_of((g * HEADS + head) * HEAD_DIM, HEAD_DIM), HEAD_DIM)],
                stage_sems.at[sl, g]) for g in range(4)]

        @pl.when(step >= 2)
        def _():
            for cp in writes(slot):
                cp.wait()

        @pl.when(blk == 0)
        def _():
            gstate_ref[...] = jnp.zeros_like(gstate_ref)
            dgn_ref[...] = jnp.zeros_like(dgn_ref)

        cosv, sinv = cos_ref[...], sin_ref[...]
        q = _rot(q_ref[...], cosv, sinv)
        k = _rot(k_ref[...], cosv, sinv) * K_SCALE
        v = v_ref[...]
        dec, kd, qd, cd, gn = dec_ref[...], kd_ref[...], qd_ref[...], cd_ref[...], gn_ref[...]
        kdf, qdf = k * kd, q * qd
        rg = g_ref[...]
        yv = y_ref[...]
        dret = dret_ref[...]
        sig = jax.nn.sigmoid(rg)
        gate = rg * sig
        mu = jnp.mean(yv, axis=-1, keepdims=True)
        yc = yv - mu
        rstd = lax.rsqrt(jnp.mean(yc * yc, axis=-1, keepdims=True) + GN_EPS)
        z = yc * rstd
        dyn = dret * gate
        stage_ref[slot, 3] = (dret * (z * gn) * (sig * (1.0 + rg * (1.0 - sig)))).astype(stage_ref.dtype)
        dgn_ref[...] += jnp.sum(dyn * z, axis=0, keepdims=True)
        dz = dyn * gn
        dy = rstd * (dz - jnp.mean(dz, axis=-1, keepdims=True) - z * jnp.mean(dz * z, axis=-1, keepdims=True))
        chunks = [slice(c * CHUNK, (c + 1) * CHUNK) for c in range(RET_BLOCK_CHUNKS)]
        dprevs = [_dot(qdf[rows], dy[rows], TN) for rows in chunks]
        gst, gsts = gstate_ref[...], [None] * RET_BLOCK_CHUNKS
        for c in reversed(range(RET_BLOCK_CHUNKS)):
            gsts[c] = gst
            gst = dprevs[c] + cd * gst
        gstate_ref[...] = gst
        dq = jnp.concatenate([_dot(dy[rows], prev_ref[c], NT) for c, rows in enumerate(chunks)], axis=0) * qd
        dk = jnp.concatenate([_dot(v[rows], g, NT) for rows, g in zip(chunks, gsts)], axis=0) * kd
        dv = jnp.concatenate([_dot(kdf[rows], g, NN) for rows, g in zip(chunks, gsts)], axis=0)
        dqi, dki, dvi = [], [], []
        for b in range(RET_ROWS // RET_SUB):
            rows = slice(b * RET_SUB, (b + 1) * RET_SUB)
            qs, ks, vs, dys = q[rows], k[rows], v[rows], dy[rows]
            dvi.append(_dot(_dot(ks, qs, NT) * dec, dys, NN))
            dqi.append(_dot(_dot(dys, vs, NT) * dec, ks, NN))
            dki.append(_dot(_dot(vs, dys, NT) * dec, qs, NN))
        dq = dq + jnp.concatenate(dqi, axis=0)
        dk = dk + jnp.concatenate(dki, axis=0)
        dv = dv + jnp.concatenate(dvi, axis=0)
        stage_ref[slot, 0] = _rot_bwd(dq, cosv, sinv).astype(stage_ref.dtype)
        stage_ref[slot, 1] = _rot_bwd(dk * K_SCALE, cosv, sinv).astype(stage_ref.dtype)
        stage_ref[slot, 2] = dv.astype(stage_ref.dtype)
        for cp in writes(slot):
            cp.start()

        @pl.when(step == HEADS * nb - 1)
        def _():
            for cp in writes(1 - slot) + writes(slot):
                cp.wait()

    rev = lambda i: nb - 1 - i

    def col(off):
        return pl.BlockSpec((RET_ROWS, HEAD_DIM), lambda h, i: (rev(i), off + h))

    pos = pl.BlockSpec((RET_ROWS, HEAD_DIM), lambda h, i: (rev(i), 0))
    per_head = lambda shape: pl.BlockSpec((None, *shape), lambda h, i: (h, 0, 0))
    rs = _Riders(riders, 14, 2)
    res = _pallas(
        _with_riders(body, 14, 2, 3, rs, (HEADS, nb)), name="retention_bwd", grid=(HEADS, nb),
        in_specs=[col(0), col(HEADS), col(2 * HEADS), col(3 * HEADS), pos, pos,
                  per_head((RET_SUB, RET_SUB)), per_head((RET_ROWS, HEAD_DIM)), per_head((RET_ROWS, HEAD_DIM)),
                  per_head((1, HEAD_DIM)), pl.BlockSpec((1, HEAD_DIM), lambda h, i: (0, h)),
                  col(0), pl.BlockSpec((None, RET_BLOCK_CHUNKS, HEAD_DIM, HEAD_DIM), lambda h, i: (h, rev(i), 0, 0)),
                  col(0), *rs.in_specs],
        out_specs=[ANY, per_head((1, HEAD_DIM)), *rs.out_specs],
        out_shape=[_sds((s, proj.shape[1]), BF16), _sds((HEADS, 1, HEAD_DIM), F32), *rs.out_shapes],
        input_output_aliases=rs.aliases,
        scratch_shapes=[pltpu.VMEM((HEAD_DIM, HEAD_DIM), F32), pltpu.VMEM((2, 4, RET_ROWS, HEAD_DIM), BF16),
                        pltpu.SemaphoreType.DMA((2, 4)), *rs.scratch],
        compiler_params=_params("arbitrary", "arbitrary"),
    )(proj, proj, proj, proj, cos_f, sin_f, decay, k_dec, q_dec, c_dec, gn_g, y, prev, dmix, *rs.arrays)
    return res[:2], rs.split(res[2:])


ATT_COL0 = 4 * HEADS
PAD_ROWS = LEFT_CHUNKS * CHUNK
NORM_ROWS = 512
GROUP_CHUNKS = 4
GROUP = GROUP_CHUNKS * CHUNK
WIN = (LEFT_CHUNKS + GROUP_CHUNKS) * CHUNK
MASKED = -1e30


def _qk_norm(x, g):
    return x * lax.rsqrt(jnp.mean(x * x, axis=-1, keepdims=True) + EPS) * g


def _band_probs(qb, kb, bias, g):
    sc = _dot(qb, kb, NT) * K_SCALE + bias
    win_chunk = lax.broadcasted_iota(jnp.int32, (GROUP, WIN), 1) // CHUNK
    sc = jnp.where(g * GROUP_CHUNKS - LEFT_CHUNKS + win_chunk >= 0, sc, MASKED)
    e = jnp.exp(sc - jnp.max(sc, axis=-1, keepdims=True))
    return e / jnp.sum(e, axis=-1, keepdims=True)


def _with_riders(core, n_in, n_out, n_scratch, rs, grid):
    n_rin, n_rout = len(rs.arrays), len(rs.out_shapes)
    if not rs.riders:
        return core
    steps = 1
    for n in grid:
        steps *= n

    def body(*refs):
        outs_at = n_in + n_rin
        scratch_at = outs_at + n_out + n_rout
        bound = rs.bind(refs[n_in:outs_at], refs[outs_at + n_out:scratch_at], refs[scratch_at + n_scratch:])
        step = 0
        for axis, n in enumerate(grid):
            step = step * n + pl.program_id(axis)
        pl.when(step == 0)(lambda: rs.run("start", bound))
        pl.when(step == int(steps * RIDER_MID))(lambda: rs.run("mid", bound))
        pl.when(step == int(steps * RIDER_LATE))(lambda: rs.run("late", bound))
        core(*refs[:n_in], *refs[outs_at:outs_at + n_out], *refs[scratch_at:scratch_at + n_scratch])
        pl.when(step == steps - 1)(lambda: rs.run("end", bound))

    return body


def _attention_fwd(proj, gq, gk, bias, mix, riders=()):
    s = proj.shape[0]
    rs = _Riders(riders, 7, 1)

    def body(q_ref, k_ref, v_ref, gq_ref, gk_ref, bias_ref, mix_ref, o_ref, kp_ref, vp_ref):
        kp_ref[0:PAD_ROWS, :] = jnp.zeros((PAD_ROWS, HEAD_DIM), kp_ref.dtype)
        vp_ref[0:PAD_ROWS, :] = jnp.zeros((PAD_ROWS, HEAD_DIM), vp_ref.dtype)
        gqv, gkv = gq_ref[...], gk_ref[...]

        def fill(b, carry):
            r0 = pl.multiple_of(b * NORM_ROWS, NORM_ROWS)
            kp_ref[pl.ds(PAD_ROWS + r0, NORM_ROWS), :] = _qk_norm(k_ref[pl.ds(r0, NORM_ROWS), :], gkv).astype(kp_ref.dtype)
            vp_ref[pl.ds(PAD_ROWS + r0, NORM_ROWS), :] = v_ref[pl.ds(r0, NORM_ROWS), :].astype(vp_ref.dtype)
            return carry

        lax.fori_loop(0, s // NORM_ROWS, fill, 0)

        def group(g, carry):
            r0 = pl.multiple_of(g * GROUP, GROUP)
            qn = _qk_norm(q_ref[pl.ds(r0, GROUP), :], gqv)
            p = _band_probs(qn, kp_ref[pl.ds(r0, WIN), :], bias_ref[...], g)
            o_ref[pl.ds(r0, GROUP), :] = _dot(p, vp_ref[pl.ds(r0, WIN), :], NN).astype(o_ref.dtype)
            return carry

        lax.fori_loop(0, s // GROUP, group, 0, unroll=2)

    def col(off):
        return pl.BlockSpec((s, HEAD_DIM), lambda h: (0, off + h))

    vec = pl.BlockSpec((1, HEAD_DIM), lambda h: (0, 0))
    res = _pallas(
        _with_riders(body, 7, 1, 2, rs, (HEADS,)), name="attention_fwd", grid=(HEADS,),
        in_specs=[col(ATT_COL0), col(ATT_COL0 + HEADS), col(ATT_COL0 + 2 * HEADS), vec, vec,
                  pl.BlockSpec((None, GROUP, WIN), lambda h: (h, 0, 0)), ANY, *rs.in_specs],
        out_specs=[col(HEADS), *rs.out_specs], out_shape=[_sds(mix.shape, mix.dtype), *rs.out_shapes],
        input_output_aliases={6: 0, **rs.aliases},
        scratch_shapes=[pltpu.VMEM((s + PAD_ROWS, HEAD_DIM), MXU_DTYPE), pltpu.VMEM((s + PAD_ROWS, HEAD_DIM), MXU_DTYPE),
                        *rs.scratch],
        compiler_params=_params("arbitrary"),
    )(proj, proj, proj, gq, gk, bias, mix, *rs.arrays)
    return res[0], rs.split(res[1:])


def _attention_bwd(proj, gq, gk, bias, dmix, dproj, riders=()):
    s = proj.shape[0]
    rs = _Riders(riders, 8, 4)

    def body(q_ref, k_ref, v_ref, gq_ref, gk_ref, bias_ref, do_ref, dproj_in_ref,
             dproj_ref, dgq_ref, dgk_ref, dbias_ref, kp_ref, vp_ref, dkp_ref, dvp_ref, dqn_ref, stage_ref, stage_sems):
        head = pl.program_id(0)

        def writes():
            return [pltpu.make_async_copy(
                stage_ref.at[g],
                dproj_ref.at[:, pl.ds(pl.multiple_of((ATT_COL0 + g * HEADS + head) * HEAD_DIM, HEAD_DIM), HEAD_DIM)],
                stage_sems.at[g]) for g in range(3)]

        kp_ref[0:PAD_ROWS, :] = jnp.zeros((PAD_ROWS, HEAD_DIM), kp_ref.dtype)
        vp_ref[0:PAD_ROWS, :] = jnp.zeros((PAD_ROWS, HEAD_DIM), vp_ref.dtype)
        dkp_ref[...] = jnp.zeros_like(dkp_ref)
        dvp_ref[...] = jnp.zeros_like(dvp_ref)
        dbias_ref[...] = jnp.zeros_like(dbias_ref)
        gqv, gkv = gq_ref[...], gk_ref[...]

        def fill(b, carry):
            r0 = pl.multiple_of(b * NORM_ROWS, NORM_ROWS)
            kp_ref[pl.ds(PAD_ROWS + r0, NORM_ROWS), :] = _qk_norm(k_ref[pl.ds(r0, NORM_ROWS), :], gkv).astype(kp_ref.dtype)
            vp_ref[pl.ds(PAD_ROWS + r0, NORM_ROWS), :] = v_ref[pl.ds(r0, NORM_ROWS), :].astype(vp_ref.dtype)
            return carry

        lax.fori_loop(0, s // NORM_ROWS, fill, 0)

        def group(g, carry):
            r0 = pl.multiple_of(g * GROUP, GROUP)
            qn = _qk_norm(q_ref[pl.ds(r0, GROUP), :], gqv)
            kb = kp_ref[pl.ds(r0, WIN), :]
            vb = vp_ref[pl.ds(r0, WIN), :]
            p = _band_probs(qn, kb, bias_ref[...], g)
            do = do_ref[pl.ds(r0, GROUP), :]
            dvp_ref[pl.ds(r0, WIN), :] += _dot(p, do, TN)
            dp = _dot(do, vb, NT)
            ds = p * (dp - jnp.sum(dp * p, axis=-1, keepdims=True))
            dbias_ref[...] += ds
            dss = ds * K_SCALE
            dqn_ref[pl.ds(r0, GROUP), :] = _dot(dss, kb, NN)
            dkp_ref[pl.ds(r0, WIN), :] += _dot(dss, qn, TN)
            return carry

        lax.fori_loop(0, s // GROUP, group, 0, unroll=2)

        @pl.when(head == 0)
        def _():
            dgq_ref[...] = jnp.zeros_like(dgq_ref)
            dgk_ref[...] = jnp.zeros_like(dgk_ref)

        @pl.when(head > 0)
        def _():
            for cp in writes():
                cp.wait()

        def norm_bwd(x, g, dn):
            rstd = lax.rsqrt(jnp.mean(x * x, axis=-1, keepdims=True) + EPS)
            xh = x * rstd
            dxh = dn * g
            return rstd * (dxh - xh * jnp.mean(dxh * xh, axis=-1, keepdims=True)), jnp.sum(dn * xh, axis=0, keepdims=True)

        def finish(b, carry):
            r0 = pl.multiple_of(b * NORM_ROWS, NORM_ROWS)
            rows = pl.ds(r0, NORM_ROWS)
            dq, dgq = norm_bwd(q_ref[rows, :], gqv, dqn_ref[rows, :])
            dk, dgk = norm_bwd(k_ref[rows, :], gkv, dkp_ref[pl.ds(PAD_ROWS + r0, NORM_ROWS), :])
            stage_ref[0, rows, :] = dq.astype(stage_ref.dtype)
            stage_ref[1, rows, :] = dk.astype(stage_ref.dtype)
            stage_ref[2, rows, :] = dvp_ref[pl.ds(PAD_ROWS + r0, NORM_ROWS), :].astype(stage_ref.dtype)
            dgq_ref[...] += dgq
            dgk_ref[...] += dgk
            return carry

        lax.fori_loop(0, s // NORM_ROWS, finish, 0)
        for cp in writes():
            cp.start()

        @pl.when(head == HEADS - 1)
        def _():
            for cp in writes():
                cp.wait()

    def col(off):
        return pl.BlockSpec((s, HEAD_DIM), lambda h: (0, off + h))

    vec = pl.BlockSpec((1, HEAD_DIM), lambda h: (0, 0))
    hbias = pl.BlockSpec((None, GROUP, WIN), lambda h: (h, 0, 0))
    res = _pallas(
        _with_riders(body, 8, 4, 7, rs, (HEADS,)), name="attention_bwd", grid=(HEADS,),
        in_specs=[col(ATT_COL0), col(ATT_COL0 + HEADS), col(ATT_COL0 + 2 * HEADS), vec, vec, hbias, col(HEADS), ANY,
                  *rs.in_specs],
        out_specs=[ANY, vec, vec, hbias, *rs.out_specs],
        out_shape=[_sds(dproj.shape, dproj.dtype), _sds((1, HEAD_DIM), F32), _sds((1, HEAD_DIM), F32),
                   _sds((HEADS, GROUP, WIN), F32), *rs.out_shapes],
        input_output_aliases={7: 0, **rs.aliases},
        scratch_shapes=[pltpu.VMEM((s + PAD_ROWS, HEAD_DIM), MXU_DTYPE), pltpu.VMEM((s + PAD_ROWS, HEAD_DIM), MXU_DTYPE),
                        pltpu.VMEM((s + PAD_ROWS, HEAD_DIM), F32), pltpu.VMEM((s + PAD_ROWS, HEAD_DIM), F32),
                        pltpu.VMEM((s, HEAD_DIM), F32), pltpu.VMEM((3, s, HEAD_DIM), BF16),
                        pltpu.SemaphoreType.DMA((3,)), *rs.scratch],
        compiler_params=_params("arbitrary"),
    )(proj, proj, proj, gq, gk, bias, dmix, dproj, *rs.arrays)
    return res[:4], rs.split(res[4:])


DIAG_SPLIT = (BAND + WIN - CHUNK) // 2


def _diag_bin(m):
    t = jnp.where(m < DIAG_SPLIT, m, m - WIN)
    return jnp.clip(LEFT_CHUNKS * CHUNK - t, -(CHUNK - 1), REL_CLIP) + (CHUNK - 1)


def _skew_rows(a, left):
    row = lax.broadcasted_iota(jnp.int32, (GROUP, WIN), 0)
    for b in range(GROUP.bit_length() - 1):
        step = 1 << b
        a = jnp.where(jnp.bitwise_and(row, step) != 0, pltpu.roll(a, WIN - step if left else step, 1), a)
    return a


def _rel_bias_expand(rel_bias):
    def body(rb_ref, o_ref):
        h = pl.program_id(0)
        bins = _diag_bin(lax.broadcasted_iota(jnp.int32, (8, WIN), 1))
        per_diag = lax.fori_loop(0, REL_SIZE, lambda r, acc: jnp.where(bins == r, rb_ref[h, r], acc),
                                 jnp.zeros((8, WIN), F32))
        table = _skew_rows(jnp.broadcast_to(per_diag[0:1], (GROUP, WIN)), left=False)
        row_chunk = lax.broadcasted_iota(jnp.int32, (GROUP, WIN), 0) // CHUNK
        col_chunk = lax.broadcasted_iota(jnp.int32, (GROUP, WIN), 1) // CHUNK
        in_band = jnp.logical_and(col_chunk >= row_chunk, col_chunk <= row_chunk + LEFT_CHUNKS)
        o_ref[...] = jnp.where(in_band, table, MASKED)

    return _pallas(body, name="rel_bias_expand", grid=(HEADS,), in_specs=[pl.BlockSpec(memory_space=pltpu.SMEM)],
                   out_specs=pl.BlockSpec((None, GROUP, WIN), lambda h: (h, 0, 0)),
                   out_shape=_sds((HEADS, GROUP, WIN), F32), compiler_params=_params("parallel"))(rel_bias)


def _rel_bias_fold(dbias):
    def body(a_ref, o_ref):
        diag = jnp.sum(_skew_rows(a_ref[...], left=True), axis=0, keepdims=True)
        onehot = (_diag_bin(lax.broadcasted_iota(jnp.int32, (WIN, REL_SIZE), 0))
                  == lax.broadcasted_iota(jnp.int32, (WIN, REL_SIZE), 1)).astype(MXU_DTYPE)
        rest = jnp.broadcast_to(diag, (8, WIN))
        out = jnp.zeros((8, REL_SIZE), F32)
        for _ in range(3):
            piece = rest.astype(BF16)
            out = out + _dot(piece, onehot, NN)
            rest = rest - piece.astype(F32)
        o_ref[...] = out[0:1]

    return _pallas(body, name="rel_bias_fold", grid=(HEADS,),
                   in_specs=[pl.BlockSpec((None, GROUP, WIN), lambda h: (h, 0, 0))],
                   out_specs=pl.BlockSpec((None, 1, REL_SIZE), lambda h: (h, 0, 0)),
                   out_shape=_sds((HEADS, 1, REL_SIZE), F32), compiler_params=_params("parallel"))(dbias)


def _place():
    return lax.axis_index("x"), lax.axis_index("y"), lax.axis_index("c")


def _other_chips(x, y):
    return [(1 - x, y), (x, 1 - y), (1 - x, 1 - y)]


class _Rider:
    reads, ins, new, n_sems = (), (), (), 1

    def start(self, reads, ins, new, send, recv):
        pass

    def mid(self, reads, ins, new, send, recv):
        pass

    def late(self, reads, ins, new, send, recv):
        pass

    def end(self, reads, ins, new, send, recv):
        pass


class _Riders:
    def __init__(self, riders, n_host_in, n_host_out):
        self.riders = list(riders)
        self.arrays, self.out_shapes, self.aliases, self.scratch = [], [], {}, []
        for r in self.riders:
            for t, a in enumerate(r.ins):
                self.aliases[n_host_in + len(self.arrays) + len(r.reads) + t] = n_host_out + len(self.out_shapes) + t
            self.arrays += [*r.reads, *r.ins]
            self.out_shapes += [_sds(a.shape, a.dtype) for a in r.ins] + list(r.new)
            self.scratch += [pltpu.SemaphoreType.DMA((r.n_sems,)), pltpu.SemaphoreType.DMA((r.n_sems,))]
        self.in_specs = [ANY] * len(self.arrays)
        self.out_specs = [ANY] * len(self.out_shapes)

    def bind(self, in_refs, out_refs, scratch_refs):
        bound, i, o = [], 0, 0
        for t, r in enumerate(self.riders):
            reads = in_refs[i:i + len(r.reads)]
            i += len(r.reads) + len(r.ins)
            ins = out_refs[o:o + len(r.ins)]
            new = out_refs[o + len(r.ins):o + len(r.ins) + len(r.new)]
            o += len(r.ins) + len(r.new)
            bound.append((reads, ins, new, scratch_refs[2 * t], scratch_refs[2 * t + 1]))
        return bound

    def run(self, phase, bound):
        for r, b in zip(self.riders, bound):
            getattr(r, phase)(*b)

    def split(self, outs):
        res, o = [], 0
        for r in self.riders:
            n = len(r.ins) + len(r.new)
            res.append(list(outs[o:o + n]))
            o += n
        return res


def _run_riders(name, riders):
    rs = _Riders(riders, 0, 0)
    n_in, n_out = len(rs.arrays), len(rs.out_shapes)

    def body(*refs):
        bound = rs.bind(refs[:n_in], refs[n_in:n_in + n_out], refs[n_in + n_out:])
        rs.run("start", bound)
        rs.run("mid", bound)
        rs.run("late", bound)
        rs.run("end", bound)

    outs = _pallas(body, name=name, in_specs=rs.in_specs, out_specs=rs.out_specs, out_shape=rs.out_shapes,
                   input_output_aliases=rs.aliases, scratch_shapes=rs.scratch)(*rs.arrays)
    return rs.split(outs)


class _GatherRider(_Rider):
    X_LINK, Y_LINK, Y_PASS, X_PASS, D2D_X, D2D_Y, D2D_DIAG, N_SEMS = 0, 1, 2, 3, 4, 5, 6, 7

    def __init__(self, blocks, part=(0, 1, 1)):
        self.ins = tuple(blocks)
        self.part = part
        self.n_sems = self.N_SEMS * len(blocks)

    def _copy(self, out, send, recv, w, sem, chip_from, cc, to, sub=None):
        hr = self.ins[w].shape[1] // 2
        lo, hi, n = self.part
        first, size = cc * hr + lo * (hr // n), (hi - lo) * (hr // n)
        if sub is not None:
            size //= 2
            first += sub * size
        piece = out[w].at[2 * chip_from[0] + chip_from[1], pl.ds(first, size), :]
        return pltpu.make_async_remote_copy(src_ref=piece, dst_ref=piece, send_sem=send.at[self.N_SEMS * w + sem],
                                            recv_sem=recv.at[self.N_SEMS * w + sem], device_id=to, device_id_type=MESH)

    def _sent(self, out, send, recv, w):
        x, y, c = _place()
        me, sib = (x, y), (x, y, 1 - c)
        xn, yn, diag = _other_chips(x, y)
        cp = functools.partial(self._copy, out, send, recv, w)
        return [("start", cp(self.X_LINK, me, c, (*xn, c))), ("start", cp(self.Y_LINK, me, c, (*yn, c))),
                ("mid_x", cp(self.D2D_X, xn, c, sib)), ("mid_x", cp(self.Y_PASS, xn, c, (*yn, c), sub=0)),
                ("mid_y", cp(self.D2D_Y, yn, c, sib)), ("mid_y", cp(self.X_PASS, yn, c, (*xn, c), sub=1)),
                ("late", cp(self.D2D_DIAG, diag, c, sib))]

    def _go(self, out, send, recv, phase):
        for w in range(len(self.ins)):
            for ph, copy in self._sent(out, send, recv, w):
                if ph == phase:
                    copy.start()

    def start(self, reads, out, new, send, recv):
        self._go(out, send, recv, "start")

    def mid(self, reads, out, new, send, recv):
        x, y, c = _place()
        xn, yn, _ = _other_chips(x, y)
        for w in range(len(self.ins)):
            self._copy(out, send, recv, w, self.X_LINK, xn, c, (x, y, c)).wait_recv()
        self._go(out, send, recv, "mid_x")
        for w in range(len(self.ins)):
            self._copy(out, send, recv, w, self.Y_LINK, yn, c, (x, y, c)).wait_recv()
        self._go(out, send, recv, "mid_y")

    def late(self, reads, out, new, send, recv):
        x, y, c = _place()
        diag = _other_chips(x, y)[2]
        for w in range(len(self.ins)):
            self._copy(out, send, recv, w, self.Y_PASS, diag, c, (x, y, c), sub=0).wait_recv()
            self._copy(out, send, recv, w, self.X_PASS, diag, c, (x, y, c), sub=1).wait_recv()
        self._go(out, send, recv, "late")

    def end(self, reads, out, new, send, recv):
        x, y, c = _place()
        xn, yn, diag = _other_chips(x, y)
        for w in range(len(self.ins)):
            for sem, chip in ((self.D2D_X, xn), (self.D2D_Y, yn), (self.D2D_DIAG, diag)):
                self._copy(out, send, recv, w, sem, chip, 1 - c, (x, y, c)).wait_recv()
        for w in range(len(self.ins)):
            for _, copy in self._sent(out, send, recv, w):
                copy.wait_send()


class _SwapRider(_Rider):
    def __init__(self, grads):
        self.reads = tuple(grads)
        self.new = tuple(_sds((N_CHIPS, g.shape[1] // 2, g.shape[2]), g.dtype) for g in grads)
        self.n_sems = len(grads)

    def _copies(self, src, new, send, recv):
        x, y, c = _place()
        copies = []
        for w in range(len(self.reads)):
            hr = self.reads[w].shape[1] // 2
            copies.append(pltpu.make_async_remote_copy(
                src_ref=src[w].at[:, pl.ds((1 - c) * hr, hr), :], dst_ref=new[w],
                send_sem=send.at[w], recv_sem=recv.at[w], device_id=(x, y, 1 - c), device_id_type=MESH))
        return copies

    def start(self, src, ins, new, send, recv):
        for cp in self._copies(src, new, send, recv):
            cp.start()

    def end(self, src, ins, new, send, recv):
        for cp in self._copies(src, new, send, recv):
            cp.wait()


def _add_half(g, got, c_arr, name):
    nk, r, cols = g.shape
    hr = r // 2
    tr = min(hr, 256)
    nb = hr // tr

    def body(c_ref, g_ref, got_ref, o_ref):
        o_ref[...] = (g_ref[...].astype(F32) + got_ref[...].astype(F32)).astype(o_ref.dtype)

    grid_spec = pltpu.PrefetchScalarGridSpec(
        num_scalar_prefetch=1, grid=(nk, nb),
        in_specs=[pl.BlockSpec((None, tr, cols), lambda k, i, c_ref: (k, c_ref[0] * nb + i, 0)),
                  pl.BlockSpec((None, tr, cols), lambda k, i, c_ref: (k, i, 0))],
        out_specs=pl.BlockSpec((None, tr, cols), lambda k, i, c_ref: (k, i, 0)))
    return _pallas(body, name=name, grid_spec=grid_spec, out_shape=_sds((nk, hr, cols), g.dtype),
                   compiler_params=_params("parallel", "parallel"))(c_arr, g, got)


class _SendPartialsRider(_Rider):
    def __init__(self, parts, got=None, part=(0, 1, 1)):
        self.reads = tuple(parts)
        if got is None:
            self.new = tuple(_sds((N_CHIPS - 1, *p.shape[1:]), p.dtype) for p in parts)
        else:
            self.ins = tuple(got)
        self.part = part
        self.n_sems = 3 * len(parts)

    def _copies(self, src, ins, new, send, recv):
        x, y, c = _place()
        land = ins if self.ins else new
        lo, hi, n = self.part
        copies = []
        for w in range(len(self.reads)):
            pr = self.reads[w].shape[1] // n
            rows = pl.ds(lo * pr, (hi - lo) * pr)
            for j, chip in enumerate(_other_chips(x, y)):
                copies.append(pltpu.make_async_remote_copy(
                    src_ref=src[w].at[2 * chip[0] + chip[1], rows, :], dst_ref=land[w].at[j, rows, :],
                    send_sem=send.at[3 * w + j], recv_sem=recv.at[3 * w + j], device_id=(*chip, c), device_id_type=MESH))
        return copies

    def start(self, src, ins, new, send, recv):
        for cp in self._copies(src, ins, new, send, recv):
            cp.start()

    def end(self, src, ins, new, send, recv):
        for cp in self._copies(src, ins, new, send, recv):
            cp.wait()


def _sum_partials(part, got, kc_arr, name):
    _, hr, cols = part.shape
    tr = min(hr, 256)
    nb = hr // tr

    def body(kc_ref, p_ref, g0_ref, g1_ref, g2_ref, o_ref):
        o_ref[...] = ((p_ref[...].astype(F32) + g0_ref[...].astype(F32)) + g1_ref[...].astype(F32)) + g2_ref[...].astype(F32)

    slot = lambda j: pl.BlockSpec((None, tr, cols), lambda i, kc_ref: (j, i, 0))
    grid_spec = pltpu.PrefetchScalarGridSpec(
        num_scalar_prefetch=1, grid=(nb,),
        in_specs=[pl.BlockSpec((None, tr, cols), lambda i, kc_ref: (kc_ref[0], i, 0)), slot(0), slot(1), slot(2)],
        out_specs=pl.BlockSpec((tr, cols), lambda i, kc_ref: (kc_ref[1] * nb + i, 0)))
    return _pallas(body, name=name, grid_spec=grid_spec, out_shape=_sds((2 * hr, cols), F32),
                   compiler_params=_params("parallel"))(kc_arr, part, got, got, got)


class _ShareRider(_Rider):
    def __init__(self, grads):
        self.ins = tuple(grads)
        self.n_sems = len(grads)

    def _copies(self, out, send, recv):
        x, y, c = _place()
        copies = []
        for w in range(len(self.ins)):
            hr = self.ins[w].shape[0] // 2
            mine = out[w].at[pl.ds(c * hr, hr), :]
            copies.append(pltpu.make_async_remote_copy(
                src_ref=mine, dst_ref=mine, send_sem=send.at[w], recv_sem=recv.at[w],
                device_id=(x, y, 1 - c), device_id_type=MESH))
        return copies

    def start(self, reads, out, new, send, recv):
        for cp in self._copies(out, send, recv):
            cp.start()

    def end(self, reads, out, new, send, recv):
        for cp in self._copies(out, send, recv):
            cp.wait()


def _small_allreduce_adamw(g_part, w, m, v):
    rows = g_part.shape[0]

    def body(g_ref, w_ref, m_ref, v_ref, go_ref, d_ref, nm_ref, nv_ref, all_ref, send_sems, recv_sems):
        x, y, c = _place()
        me = 4 * x + 2 * y + c
        all_ref[me] = g_ref[...]
        copies = []
        for r in range(1, 8):
            dx, dy, dc = (r >> 2) & 1, (r >> 1) & 1, r & 1
            peer = (1 - x if dx else x, 1 - y if dy else y, 1 - c if dc else c)
            copies.append(pltpu.make_async_remote_copy(
                src_ref=g_ref, dst_ref=all_ref.at[me], send_sem=send_sems.at[r - 1], recv_sem=recv_sems.at[r - 1],
                device_id=peer, device_id_type=MESH))
        for cp in copies:
            cp.start()
        for cp in copies:
            cp.wait()
        tot = all_ref[0]
        for d in range(1, 8):
            tot = tot + all_ref[d]
        go_ref[...] = tot
        d_ref[...], nm_ref[...], nv_ref[...] = _adamw_math(w_ref[...], tot, m_ref[...], v_ref[...])

    vm = pl.BlockSpec(memory_space=pltpu.VMEM)
    return _pallas(
        body, name="small_allreduce_adamw", in_specs=[vm] * 4, out_specs=[vm] * 4,
        out_shape=[_sds((rows, 128), F32)] * 4,
        scratch_shapes=[pltpu.VMEM((8, rows, 128), F32), pltpu.SemaphoreType.DMA((7,)), pltpu.SemaphoreType.DMA((7,))],
    )(g_part, w, m, v)


SMALL_SIZES = (2048, 1024, 128, 128, HEADS * REL_SIZE, 2048)
SMALL_PART_ROWS = tuple(-(-size // 1024) * 8 for size in SMALL_SIZES)
SMALL_ROWS = sum(SMALL_PART_ROWS)


def _pack_small(parts):
    rows = []
    for p, size, nr in zip(parts, SMALL_SIZES, SMALL_PART_ROWS):
        rows.append(jnp.pad(p.reshape(-1), (0, nr * 128 - size)).reshape(nr, 128))
    return jnp.concatenate(rows, axis=0)


def _unpack_small(slab, shapes):
    out, off = [], 0
    for size, nr, shape in zip(SMALL_SIZES, SMALL_PART_ROWS, shapes):
        out.append(slab[off:off + nr].reshape(-1)[:size].reshape(shape))
        off += nr
    return out


def kernel(x, norm1_g, w_in, ret_norm_g, q_norm_g, k_norm_g, rel_bias, w_out, norm2_g, w_ff1, w_ff2, loss_target, m_norm1_g, m_w_in, m_ret_norm_g, m_q_norm_g, m_k_norm_g, m_rel_bias, m_w_out, m_norm2_g, m_w_ff1, m_w_ff2, v_norm1_g, v_w_in, v_ret_norm_g, v_q_norm_g, v_k_norm_g, v_rel_bias, v_w_out, v_norm2_g, v_w_ff1, v_w_ff2):
    xs = x[0]
    tgt = loss_target[0]
    s, d = xs.shape
    d_in = N_CHIPS * w_in.shape[2]
    d_ff = N_CHIPS * w_ff1.shape[2]
    in_sh, ff_sh = w_in.shape[2], w_ff1.shape[2]
    tm = min(s, 1024)
    gi = s // tm
    c_arr = lax.axis_index("c").astype(jnp.int32).reshape(1)
    k_arr = (2 * lax.axis_index("x") + lax.axis_index("y")).astype(jnp.int32).reshape(1)
    tables = _tables(s)
    bias = _rel_bias_expand(rel_bias[0])

    blk_in, blk_out, blk_ff1, blk_ff2 = (
        _cast_bf16(w_in[0], k_arr, "cast_w_in"), _cast_bf16(w_out[0], k_arr, "cast_w_out"),
        _cast_bf16(w_ff1[0], k_arr, "cast_w_ff1"), _cast_bf16(w_ff2[0], k_arr, "cast_w_ff2"))
    ((wg_in,),) = _run_riders("all_gather_w_in", [_GatherRider([blk_in])])

    h1 = _rmsnorm_fwd(xs, norm1_g, "rmsnorm1")
    tn_in = in_sh // 2
    tk = d
    (proj,), ((wg_ff1,),) = _mm(
        "proj", h1, wg_in, NN, (gi, 2 * N_CHIPS, d // tk),
        pl.BlockSpec((tm, tk), lambda i, j, k: (i, k)),
        pl.BlockSpec((None, tk, tn_in), lambda i, j, k: (j // 2, k, j % 2)),
        [_sds((s, d_in), F32)], [pl.BlockSpec((tm, tn_in), lambda i, j, k: (i, j))], (tm, tn_in),
        riders=[_GatherRider([blk_ff1], (0, 3, 4))])
    (mix, y_ret, prev), ((wg_ff1,),) = _retention_fwd(proj, ret_norm_g, tables, riders=[_GatherRider([wg_ff1], (3, 4, 4))])
    mix, ((wg_out,), (wg_ff2,)) = _attention_fwd(
        proj, q_norm_g, k_norm_g, bias, mix, riders=[_GatherRider([blk_out]), _GatherRider([blk_ff2], (0, 1, 4))])
    wg_out = wg_out.reshape(d, d)
    tn = 1024
    tile = pl.BlockSpec((tm, tn), lambda i, j, k: (i, j))
    def residual_norm(acc, res, g):
        x1v = res + acc
        yv = x1v * lax.rsqrt(jnp.mean(x1v * x1v, axis=-1, keepdims=True) + EPS)
        return x1v, yv * g

    tmo = min(s, 512)
    rows = pl.BlockSpec((tmo, d), lambda i, j, k: (i, 0))
    x1, h2 = _mm("out_proj", mix, wg_out, NN, (s // tmo, 1, 1),
                 rows, pl.BlockSpec((d, d), lambda i, j, k: (0, 0)),
                 [_sds((s, d), F32), _sds((s, d), BF16)], [rows, rows], (tmo, d),
                 extras=(xs, norm2_g), extra_specs=(rows, pl.BlockSpec((1, d), lambda i, j, k: (0, 0))),
                 epi=residual_norm)
    tn_ff = min(ff_sh, 1024)
    per = ff_sh // tn_ff

    def relu2(acc):
        r = jnp.maximum(acc, 0.0)
        return acc, r * r

    (u, act), ((wg_ff2,),) = _mm(
        "ff1", h2, wg_ff1, NN, (gi, N_CHIPS * per, d // tk),
        pl.BlockSpec((tm, tk), lambda i, j, k: (i, k)),
        pl.BlockSpec((None, tk, tn_ff), lambda i, j, k: (j // per, k, j % per)),
        [_sds((s, d_ff), F32), _sds((s, d_ff), BF16)],
        [pl.BlockSpec((tm, tn_ff), lambda i, j, k: (i, j))] * 2, (tm, tn_ff), epi=relu2,
        riders=[_GatherRider([wg_ff2], (1, 4, 4))])
    wg_ff2 = wg_ff2.reshape(d_ff, d)

    def loss_epi(acc, res, t):
        diff = (res + acc) - t
        dy = diff / d
        return dy, dy, jnp.sum(diff * diff, axis=0, keepdims=True)

    tk2 = min(tk, 2048)
    dy, dyb, loss_cols = _mm(
        "ff2_loss", act, wg_ff2, NN, (gi, d // tn, d_ff // tk2),
        pl.BlockSpec((tm, tk2), lambda i, j, k: (i, k)), pl.BlockSpec((tk2, tn), lambda i, j, k: (k, j)),
        [_sds((s, d), F32), _sds((s, d), BF16), _sds((gi, 1, d), F32)],
        [tile, tile, pl.BlockSpec((None, 1, tn), lambda i, j, k: (i, 0, j))], (tm, tn),
        extras=(x1, tgt), extra_specs=(tile, tile), epi=loss_epi)
    loss = lax.psum(0.5 * jnp.sum(loss_cols) / d, ("x", "y", "c"))

    (du,) = _mm("d_act", dyb, wg_ff2, NT, (gi, d_ff // tn, d // tk),
                pl.BlockSpec((tm, tk), lambda i, j, k: (i, k)), pl.BlockSpec((tn, tk), lambda i, j, k: (j, k)),
                [_sds((s, d_ff), BF16)], [tile], (tm, tn), extras=(u,), extra_specs=(tile,),
                epi=lambda acc, uu: (acc * (2.0 * jnp.maximum(uu, 0.0)),))
    ts = min(s, 2048)
    wtile = pl.BlockSpec((tn, tn), lambda i, j, k: (i, j))
    (g_ff2,) = _mm("dw_ff2", act, dyb, TN, (d_ff // tn, d // tn, s // ts),
                   pl.BlockSpec((ts, tn), lambda i, j, k: (k, i)), pl.BlockSpec((ts, tn), lambda i, j, k: (k, j)),
                   [_sds((d_ff, d), BF16)], [wtile], (tn, tn))
    g_ff2 = g_ff2.reshape(N_CHIPS, d_ff // N_CHIPS, d)
    (g_ff1,), ((got_ff2,),) = _mm(
        "dw_ff1", h2, du, TN, (d // tn, N_CHIPS * per, s // ts),
        pl.BlockSpec((ts, tn), lambda i, j, k: (k, i)), pl.BlockSpec((ts, tn_ff), lambda i, j, k: (k, j)),
        [_sds((N_CHIPS, d, ff_sh), BF16)],
        [pl.BlockSpec((None, tn, tn_ff), lambda i, j, k: (j // per, i, j % per))], (tn, tn_ff),
        riders=[_SwapRider([g_ff2])])
    p_ff2 = _add_half(g_ff2, got_ff2, c_arr, "chip_partial_w_ff2")
    tkf = min(tk, ff_sh)
    kper = ff_sh // tkf
    (dh2,), ((got2_ff2,), (got_ff1,)) = _mm(
        "d_h2", du, wg_ff1, NT, (gi, d // tn, d_ff // tkf),
        pl.BlockSpec((tm, tkf), lambda i, j, k: (i, k)),
        pl.BlockSpec((None, tn, tkf), lambda i, j, k: (k // kper, j, k % kper)),
        [_sds((s, d), F32)], [tile], (tm, tn),
        riders=[_SendPartialsRider([p_ff2], part=(0, 3, 4)), _SwapRider([g_ff1])])
    p_ff1 = _add_half(g_ff1, got_ff1, c_arr, "chip_partial_w_ff1")
    dx1, dx1b, g_norm2 = _rmsnorm_bwd(x1, norm2_g, dh2, dy, "rmsnorm2_bwd")

    (dmix,) = _mm("d_mix", dx1b, wg_out, NT, (gi, d // tn, d // tk),
                  pl.BlockSpec((tm, tk), lambda i, j, k: (i, k)), pl.BlockSpec((tn, tk), lambda i, j, k: (j, k)),
                  [_sds((s, d), F32)], [tile], (tm, tn))
    (g_out,) = _mm("dw_out", mix, dx1b, TN, (d // tn, d // tn, s // ts),
                   pl.BlockSpec((ts, tn), lambda i, j, k: (k, i)), pl.BlockSpec((ts, tn), lambda i, j, k: (k, j)),
                   [_sds((d, d), BF16)], [wtile], (tn, tn))
    g_out = g_out.reshape(N_CHIPS, d // N_CHIPS, d)
    ((got_out,),) = _run_riders("grad_swap_w_out", [_SwapRider([g_out])])
    p_out = _add_half(g_out, got_out, c_arr, "chip_partial_w_out")
    (dproj, g_gn), ((got2_ff2,), (got2_ff1,)) = _retention_bwd(
        proj, ret_norm_g, tables, y_ret, prev, dmix,
        riders=[_SendPartialsRider([p_ff2], got=[got2_ff2], part=(3, 4, 4)), _SendPartialsRider([p_ff1], part=(0, 2, 4))])
    (dproj, g_gq, g_gk, dbias), ((got2_ff1,), (got2_out,)) = _attention_bwd(
        proj, q_norm_g, k_norm_g, bias, dmix, dproj,
        riders=[_SendPartialsRider([p_ff1], got=[got2_ff1], part=(2, 4, 4)), _SendPartialsRider([p_out])])
    g_rel = _rel_bias_fold(dbias)
    names = ["w_in", "w_out", "w_ff1", "w_ff2"]
    kc_arr = jnp.concatenate([k_arr, c_arr])
    early = [_sum_partials(p, r, kc_arr, "sum_partials_" + nm)
             for p, r, nm in zip((p_out, p_ff1, p_ff2), (got2_out, got2_ff1, got2_ff2), names[1:])]
    (g_in,), (early,) = _mm(
        "dw_in", h1, dproj, TN, (d // tn, 2 * N_CHIPS, s // ts),
        pl.BlockSpec((ts, tn), lambda i, j, k: (k, i)), pl.BlockSpec((ts, tn_in), lambda i, j, k: (k, j)),
        [_sds((N_CHIPS, d, in_sh), BF16)],
        [pl.BlockSpec((None, tn, tn_in), lambda i, j, k: (j // 2, i, j % 2))], (tn, tn_in), riders=[_ShareRider(early)])
    ((got_in,),) = _run_riders("grad_swap_w_in", [_SwapRider([g_in])])
    p_in = _add_half(g_in, got_in, c_arr, "chip_partial_w_in")
    (dh1,), ((got2_in,),) = _mm(
        "d_h1", dproj, wg_in, NT, (gi, d // tn, N_CHIPS),
        pl.BlockSpec((tm, in_sh), lambda i, j, k: (i, k)),
        pl.BlockSpec((None, tn, in_sh), lambda i, j, k: (k, j, 0)),
        [_sds((s, d), F32)], [tile], (tm, tn), riders=[_SendPartialsRider([p_in])])
    grad_x, _, g_norm1 = _rmsnorm_bwd(xs, norm1_g, dh1, dx1, "rmsnorm1_bwd")
    ((g_w_in,),) = _run_riders("grad_share_w_in", [_ShareRider([_sum_partials(p_in, got2_in, kc_arr, "sum_partials_w_in")])])
    g_big = [g_w_in, *early]
    big = []
    for g, w, m, v, nm in zip(g_big, (w_in, w_out, w_ff1, w_ff2), (m_w_in, m_w_out, m_w_ff1, m_w_ff2),
                              (v_w_in, v_w_out, v_w_ff1, v_w_ff2), names):
        g, delta, new_m, new_v = _adamw(w[0], g, m[0], v[0], "adamw_" + nm)
        big.append((g[None], delta[None], new_m[None], new_v[None]))

    small_w = (norm1_g, ret_norm_g, q_norm_g, k_norm_g, rel_bias, norm2_g)
    small_m = (m_norm1_g, m_ret_norm_g, m_q_norm_g, m_k_norm_g, m_rel_bias, m_norm2_g)
    small_v = (v_norm1_g, v_ret_norm_g, v_q_norm_g, v_k_norm_g, v_rel_bias, v_norm2_g)
    shapes = [p.shape for p in small_w]
    g_small = _pack_small([g_norm1, g_gn, g_gq, g_gk, g_rel, g_norm2])
    sg, sd, sm, sv = (_unpack_small(a, shapes) for a in _small_allreduce_adamw(
        g_small, _pack_small(small_w), _pack_small(small_m), _pack_small(small_v)))

    def ordered(kind):
        sm_ = (sg, sd, sm, sv)[kind]
        return (sm_[0], big[0][kind], sm_[1], sm_[2], sm_[3], sm_[4], big[1][kind], sm_[5], big[2][kind], big[3][kind])

    return (loss, grad_x[None], *ordered(0), *ordered(1), *ordered(2), *ordered(3))
```

```python
import functools

import jax
import jax.numpy as jnp
from jax import lax
from jax.experimental import pallas as pl
from jax.experimental.pallas import tpu as pltpu

F32 = jnp.float32
BF16 = jnp.bfloat16
MXU_DTYPE = jnp.bfloat16

CHUNK = 64
HEADS = 8
HEAD_DIM = 128
LEFT_CHUNKS = 8
BAND = (LEFT_CHUNKS + 1) * CHUNK
REL_CLIP = 128
REL_SIZE = (CHUNK - 1) + REL_CLIP + 1
RET_BLOCK_CHUNKS = 8
RET_ROWS = RET_BLOCK_CHUNKS * CHUNK
RET_SUB = 256
ROPE_BASE = 10000.0
EPS = 1e-6
GN_EPS = 1e-5
ADAM_LR, ADAM_B1, ADAM_B2, ADAM_EPS, ADAM_WD, ADAM_STEP = 0.001, 0.9, 0.999, 1e-08, 0.01, 10
N_CHIPS = 4
VMEM_LIMIT = 56 * 1024 * 1024
MESH = pl.DeviceIdType.MESH
ANY = pl.BlockSpec(memory_space=pl.ANY)

NN = (((1,), (0,)), ((), ()))
NT = (((1,), (1,)), ((), ()))
TN = (((0,), (0,)), ((), ()))


def _pallas(body, **kw):
    return pl.pallas_call(body, **kw)


def _params(*sem):
    return pltpu.CompilerParams(dimension_semantics=sem, vmem_limit_bytes=VMEM_LIMIT)


def _dot(a, b, dims):
    return lax.dot_general(a.astype(MXU_DTYPE), b.astype(MXU_DTYPE), dims, preferred_element_type=F32)


RIDER_MID, RIDER_LATE = 0.5, 0.8


def _mm(name, a, b, dims, grid, a_spec, b_spec, outs, o_specs, acc_shape, extras=(), extra_specs=(), epi=None,
        riders=(), sum_outs=0):
    ni, nj, nk = grid
    n_ex, n_out = len(extras), len(outs)
    n_in = 2 + n_ex
    rs = _Riders(riders, n_in, n_out)
    n_rin, n_rout = len(rs.arrays), len(rs.out_shapes)
    steps = ni * nj * nk

    def body(*refs):
        a_ref, b_ref = refs[0], refs[1]
        ex_refs = refs[2:n_in]
        o_refs = refs[n_in + n_rin:n_in + n_rin + n_out]
        acc_ref = refs[n_in + n_rin + n_out + n_rout]
        k = pl.program_id(2)
        if riders:
            bound = rs.bind(refs[n_in:n_in + n_rin], refs[n_in + n_rin + n_out:n_in + n_rin + n_out + n_rout],
                            refs[n_in + n_rin + n_out + n_rout + 1:])
            step = (pl.program_id(0) * nj + pl.program_id(1)) * nk + k
            pl.when(step == 0)(lambda: rs.run("start", bound))
            pl.when(step == int(steps * RIDER_MID))(lambda: rs.run("mid", bound))
            pl.when(step == int(steps * RIDER_LATE))(lambda: rs.run("late", bound))

        def finish(acc):
            vals = epi(acc, *[r[...] for r in ex_refs]) if epi is not None else (acc,)
            for t, (r, v) in enumerate(zip(o_refs, vals)):
                if t < n_out - sum_outs:
                    r[...] = v.astype(r.dtype)
                else:
                    first = jnp.logical_and(pl.program_id(0) == 0, pl.program_id(1) == 0)

                    def put(r=r, v=v):
                        r[...] = v.astype(r.dtype)

                    def add(r=r, v=v):
                        r[...] += v.astype(r.dtype)

                    pl.when(first)(put)
                    pl.when(jnp.logical_not(first))(add)

        if nk == 1:
            finish(_dot(a_ref[...], b_ref[...], dims))
        else:
            @pl.when(k == 0)
            def _():
                acc_ref[...] = jnp.zeros_like(acc_ref)

            acc_ref[...] += _dot(a_ref[...], b_ref[...], dims)
            pl.when(k == nk - 1)(lambda: finish(acc_ref[...]))

        if riders:
            pl.when(step == steps - 1)(lambda: rs.run("end", bound))

    res = _pallas(
        body, name=name, grid=grid, in_specs=[a_spec, b_spec, *extra_specs, *rs.in_specs],
        out_specs=[*o_specs, *rs.out_specs], out_shape=[*outs, *rs.out_shapes], input_output_aliases=rs.aliases,
        scratch_shapes=[pltpu.VMEM(acc_shape if nk > 1 else (8, 128), F32), *rs.scratch],
        compiler_params=_params(*(("arbitrary",) * 3 if riders or sum_outs else ("parallel", "parallel", "arbitrary"))),
    )(a, b, *extras, *rs.arrays)
    return (res[:n_out], rs.split(res[n_out:])) if riders else res


def _sds(shape, dtype):
    return jax.ShapeDtypeStruct(shape, dtype)


def _cast_bf16(w, k_arr, name):
    r, c = w.shape
    tr = min(r, 256)

    def body(k_ref, w_ref, o_ref):
        o_ref[...] = w_ref[...].astype(BF16)

    grid_spec = pltpu.PrefetchScalarGridSpec(
        num_scalar_prefetch=1, grid=(r // tr,), in_specs=[pl.BlockSpec((tr, c), lambda i, k_ref: (i, 0))],
        out_specs=pl.BlockSpec((None, tr, c), lambda i, k_ref: (k_ref[0], i, 0)))
    return _pallas(body, name=name, grid_spec=grid_spec, out_shape=_sds((N_CHIPS, r, c), BF16),
                   compiler_params=_params("parallel"))(k_arr, w)


def _rmsnorm_fwd(x, g, name):
    s, d = x.shape
    tr = 256

    def body(x_ref, g_ref, o_ref):
        xv = x_ref[...]
        y = xv * lax.rsqrt(jnp.mean(xv * xv, axis=-1, keepdims=True) + EPS)
        o_ref[...] = (y * g_ref[...]).astype(o_ref.dtype)

    return _pallas(body, name=name, grid=(s // tr,),
                   in_specs=[pl.BlockSpec((tr, d), lambda i: (i, 0)), pl.BlockSpec((1, d), lambda i: (0, 0))],
                   out_specs=pl.BlockSpec((tr, d), lambda i: (i, 0)), out_shape=_sds((s, d), BF16),
                   compiler_params=_params("parallel"))(x, g)


def _rmsnorm_bwd(x, g, dh, res, name, riders=()):
    s, d = x.shape
    tr = 256

    def body(x_ref, g_ref, dh_ref, res_ref, dx_ref, dxb_ref, dg_ref):
        i = pl.program_id(0)
        xv = x_ref[...]
        rstd = lax.rsqrt(jnp.mean(xv * xv, axis=-1, keepdims=True) + EPS)
        xh = xv * rstd
        dhv = dh_ref[...]

        @pl.when(i == 0)
        def _():
            dg_ref[...] = jnp.zeros_like(dg_ref)

        dg_ref[...] += jnp.sum(dhv * xh, axis=0, keepdims=True)
        dxh = dhv * g_ref[...]
        dx = res_ref[...] + rstd * (dxh - xh * jnp.mean(dxh * xh, axis=-1, keepdims=True))
        dx_ref[...] = dx
        dxb_ref[...] = dx.astype(BF16)

    row = pl.BlockSpec((tr, d), lambda i: (i, 0))
    vec = pl.BlockSpec((1, d), lambda i: (0, 0))
    rs = _Riders(riders, 4, 3)
    out = _pallas(_with_riders(body, 4, 3, 0, rs, (s // tr,)), name=name, grid=(s // tr,),
                  in_specs=[row, vec, row, row, *rs.in_specs], out_specs=[row, row, vec, *rs.out_specs],
                  out_shape=[_sds((s, d), F32), _sds((s, d), BF16), _sds((1, d), F32), *rs.out_shapes],
                  input_output_aliases=rs.aliases, scratch_shapes=rs.scratch,
                  compiler_params=_params("arbitrary"))(x, g, dh, res, *rs.arrays)
    return (out[:3], rs.split(out[3:])) if riders else out


def _adamw_math(w, g, m, v):
    m = ADAM_B1 * m + (1.0 - ADAM_B1) * g
    v = ADAM_B2 * v + (1.0 - ADAM_B2) * (g * g)
    m_hat = m / (1.0 - ADAM_B1 ** ADAM_STEP)
    v_hat = v / (1.0 - ADAM_B2 ** ADAM_STEP)
    delta = -ADAM_LR * (m_hat / (jnp.sqrt(v_hat) + ADAM_EPS) + ADAM_WD * w)
    return delta, m, v


def _adamw(w, g, m, v, name):
    r, c = w.shape
    tr = 128

    def body(w_ref, g_ref, m_ref, v_ref, go_ref, d_ref, nm_ref, nv_ref):
        g = g_ref[...]
        go_ref[...] = g
        d_ref[...], nm_ref[...], nv_ref[...] = _adamw_math(w_ref[...], g, m_ref[...], v_ref[...])

    blk = pl.BlockSpec((tr, c), lambda i: (i, 0))
    return _pallas(body, name=name, grid=(r // tr,), in_specs=[blk] * 4, out_specs=[blk] * 4,
                   out_shape=[_sds((r, c), F32)] * 4, compiler_params=_params("parallel"))(w, g, m, v)


def _tables(s):
    half = HEAD_DIM // 2
    pos = jnp.arange(s, dtype=F32)
    inv_freq = ROPE_BASE ** (-jnp.arange(half, dtype=F32) / half)
    ang = pos[:, None] * inv_freq[None, :]
    cos, sin = jnp.cos(ang), jnp.sin(ang)
    cos_f = jnp.concatenate([cos, cos], axis=-1)
    sin_f = jnp.concatenate([-sin, sin], axis=-1)
    log_g = jnp.log1p(-jnp.exp2(-(5.0 + jnp.arange(HEADS, dtype=F32))))
    p = jnp.arange(CHUNK, dtype=F32)
    decay = jnp.exp(log_g[:, None, None] * jnp.abs(p[:, None] - p[None, :]))
    k_dec = jnp.exp(log_g[None, :] * (CHUNK - 1.0 - p)[:, None])
    q_dec = jnp.exp(log_g[None, :] * (p + 1.0)[:, None])
    c_dec = jnp.exp(log_g * CHUNK)
    k_dec = jnp.tile(jnp.broadcast_to(k_dec.T[:, :, None], (HEADS, CHUNK, HEAD_DIM)), (1, RET_BLOCK_CHUNKS, 1))
    q_dec = jnp.tile(jnp.broadcast_to(q_dec.T[:, :, None], (HEADS, CHUNK, HEAD_DIM)), (1, RET_BLOCK_CHUNKS, 1))
    c_dec = jnp.broadcast_to(c_dec[:, None, None], (HEADS, 1, HEAD_DIM))
    n = RET_SUB // CHUNK
    decay = (jnp.eye(n, dtype=F32)[None, :, None, :, None] * decay[:, None, :, None, :]).reshape(HEADS, RET_SUB, RET_SUB)
    return cos_f, sin_f, decay, k_dec, q_dec, c_dec


def _rot(x, cos_f, sin_f):
    return x * cos_f + pltpu.roll(x, HEAD_DIM // 2, 1) * sin_f


def _rot_bwd(d, cos_f, sin_f):
    return d * cos_f + pltpu.roll(d * sin_f, HEAD_DIM // 2, 1)


K_SCALE = HEAD_DIM ** -0.5


def _retention_fwd(proj, gn_g, tables, riders=()):
    s = proj.shape[0]
    nb = s // RET_ROWS
    nc = s // CHUNK
    cos_f, sin_f, decay, k_dec, q_dec, c_dec = tables

    def body(q_ref, k_ref, v_ref, g_ref, cos_ref, sin_ref, dec_ref, kd_ref, qd_ref, cd_ref, gn_ref,
             ret_ref, y_ref, prev_ref, state_ref):
        @pl.when(pl.program_id(1) == 0)
        def _():
            state_ref[...] = jnp.zeros_like(state_ref)

        cosv, sinv = cos_ref[...], sin_ref[...]
        q = _rot(q_ref[...], cosv, sinv)
        k = _rot(k_ref[...], cosv, sinv) * K_SCALE
        v = v_ref[...]
        rg = g_ref[...]
        dec, cd, gn = dec_ref[...], cd_ref[...], gn_ref[...]
        kdf, qdf = k * kd_ref[...], q * qd_ref[...]
        chunks = [slice(c * CHUNK, (c + 1) * CHUNK) for c in range(RET_BLOCK_CHUNKS)]
        contribs = [_dot(kdf[rows], v[rows], TN) for rows in chunks]
        state, states = state_ref[...], []
        for c in range(RET_BLOCK_CHUNKS):
            states.append(state)
            prev_ref[c] = state.astype(prev_ref.dtype)
            state = cd * state + contribs[c]
        state_ref[...] = state
        cross = jnp.concatenate([_dot(qdf[rows], st, NN) for rows, st in zip(chunks, states)], axis=0)
        intra = []
        for b in range(RET_ROWS // RET_SUB):
            rows = slice(b * RET_SUB, (b + 1) * RET_SUB)
            intra.append(_dot(_dot(q[rows], k[rows], NT) * dec, v[rows], NN))
        y = jnp.concatenate(intra, axis=0) + cross
        y_ref[...] = y
        mu = jnp.mean(y, axis=-1, keepdims=True)
        yc = y - mu
        var = jnp.mean(yc * yc, axis=-1, keepdims=True)
        yn = yc * lax.rsqrt(var + GN_EPS) * gn
        ret_ref[...] = (rg * jax.nn.sigmoid(rg) * yn).astype(ret_ref.dtype)

    def col(off):
        return pl.BlockSpec((RET_ROWS, HEAD_DIM), lambda h, i: (i, off + h))

    pos = pl.BlockSpec((RET_ROWS, HEAD_DIM), lambda h, i: (i, 0))
    per_head = lambda shape: pl.BlockSpec((None, *shape), lambda h, i: (h, 0, 0))
    rs = _Riders(riders, 11, 3)
    res = _pallas(
        _with_riders(body, 11, 3, 1, rs, (HEADS, nb)), name="retention_fwd", grid=(HEADS, nb),
        in_specs=[col(0), col(HEADS), col(2 * HEADS), col(3 * HEADS), pos, pos,
                  per_head((RET_SUB, RET_SUB)), per_head((RET_ROWS, HEAD_DIM)), per_head((RET_ROWS, HEAD_DIM)),
                  per_head((1, HEAD_DIM)), pl.BlockSpec((1, HEAD_DIM), lambda h, i: (0, h)), *rs.in_specs],
        out_specs=[col(0), col(0),
                   pl.BlockSpec((None, RET_BLOCK_CHUNKS, HEAD_DIM, HEAD_DIM), lambda h, i: (h, i, 0, 0)),
                   *rs.out_specs],
        out_shape=[_sds((s, 2 * HEADS * HEAD_DIM), BF16), _sds((s, HEADS * HEAD_DIM), F32),
                   _sds((HEADS, nc, HEAD_DIM, HEAD_DIM), MXU_DTYPE), *rs.out_shapes],
        input_output_aliases=rs.aliases,
        scratch_shapes=[pltpu.VMEM((HEAD_DIM, HEAD_DIM), F32), *rs.scratch],
        compiler_params=_params("arbitrary", "arbitrary"),
    )(proj, proj, proj, proj, cos_f, sin_f, decay, k_dec, q_dec, c_dec, gn_g, *rs.arrays)
    return res[:3], rs.split(res[3:])


def _retention_bwd(proj, gn_g, tables, y, prev, dmix, riders=()):
    s = proj.shape[0]
    nb = s // RET_ROWS
    cos_f, sin_f, decay, k_dec, q_dec, c_dec = tables

    def body(q_ref, k_ref, v_ref, g_ref, cos_ref, sin_ref, dec_ref, kd_ref, qd_ref, cd_ref, gn_ref,
             y_ref, prev_ref, dret_ref, dproj_ref, dgn_ref, gstate_ref, stage_ref, stage_sems):
        head, blk = pl.program_id(0), pl.program_id(1)
        step = head * nb + blk
        slot = step % 2

        def writes(sl):
            rows = pl.ds(pl.multiple_of((nb - 1 - blk) * RET_ROWS, RET_ROWS), RET_ROWS)
            return [pltpu.make_async_copy(
                stage_ref.at[sl, g], dproj_ref.at[rows, pl.ds(pl.multiple_of((g * HEADS + head) * HEAD_DIM, HEAD_DIM), HEAD_DIM)],
                stage_sems.at[sl, g]) for g in range(4)]

        @pl.when(step >= 2)
        def _():
            for cp in writes(slot):
                cp.wait()

        @pl.when(blk == 0)
        def _():
            gstate_ref[...] = jnp.zeros_like(gstate_ref)
            dgn_ref[...] = jnp.zeros_like(dgn_ref)

        cosv, sinv = cos_ref[...], sin_ref[...]
        q = _rot(q_ref[...], cosv, sinv)
        k = _rot(k_ref[...], cosv, sinv) * K_SCALE
        v = v_ref[...]
        dec, kd, qd, cd, gn = dec_ref[...], kd_ref[...], qd_ref[...], cd_ref[...], gn_ref[...]
        kdf, qdf = k * kd, q * qd
        rg = g_ref[...]
        yv = y_ref[...]
        dret = dret_ref[...]
        sig = jax.nn.sigmoid(rg)
        gate = rg * sig
        mu = jnp.mean(yv, axis=-1, keepdims=True)
        yc = yv - mu
        rstd = lax.rsqrt(jnp.mean(yc * yc, axis=-1, keepdims=True) + GN_EPS)
        z = yc * rstd
        dyn = dret * gate
        stage_ref[slot, 3] = (dret * (z * gn) * (sig * (1.0 + rg * (1.0 - sig)))).astype(stage_ref.dtype)
        dgn_ref[...] += jnp.sum(dyn * z, axis=0, keepdims=True)
        dz = dyn * gn
        dy = rstd * (dz - jnp.mean(dz, axis=-1, keepdims=True) - z * jnp.mean(dz * z, axis=-1, keepdims=True))
        chunks = [slice(c * CHUNK, (c + 1) * CHUNK) for c in range(RET_BLOCK_CHUNKS)]
        dprevs = [_dot(qdf[rows], dy[rows], TN) for rows in chunks]
        gst, gsts = gstate_ref[...], [None] * RET_BLOCK_CHUNKS
        for c in reversed(range(RET_BLOCK_CHUNKS)):
            gsts[c] = gst
            gst = dprevs[c] + cd * gst
        gstate_ref[...] = gst
        dq = jnp.concatenate([_dot(dy[rows], prev_ref[c], NT) for c, rows in enumerate(chunks)], axis=0) * qd
        dk = jnp.concatenate([_dot(v[rows], g, NT) for rows, g in zip(chunks, gsts)], axis=0) * kd
        dv = jnp.concatenate([_dot(kdf[rows], g, NN) for rows, g in zip(chunks, gsts)], axis=0)
        dqi, dki, dvi = [], [], []
        for b in range(RET_ROWS // RET_SUB):
            rows = slice(b * RET_SUB, (b + 1) * RET_SUB)
            qs, ks, vs, dys = q[rows], k[rows], v[rows], dy[rows]
            dvi.append(_dot(_dot(ks, qs, NT) * dec, dys, NN))
            dqi.append(_dot(_dot(dys, vs, NT) * dec, ks, NN))
            dki.append(_dot(_dot(vs, dys, NT) * dec, qs, NN))
        dq = dq + jnp.concatenate(dqi, axis=0)
        dk = dk + jnp.concatenate(dki, axis=0)
        dv = dv + jnp.concatenate(dvi, axis=0)
        stage_ref[slot, 0] = _rot_bwd(dq, cosv, sinv).astype(stage_ref.dtype)
        stage_ref[slot, 1] = _rot_bwd(dk * K_SCALE, cosv, sinv).astype(stage_ref.dtype)
        stage_ref[slot, 2] = dv.astype(stage_ref.dtype)
        for cp in writes(slot):
            cp.start()

        @pl.when(step == HEADS * nb - 1)
        def _():
            for cp in writes(1 - slot) + writes(slot):
                cp.wait()

    rev = lambda i: nb - 1 - i

    def col(off):
        return pl.BlockSpec((RET_ROWS, HEAD_DIM), lambda h, i: (rev(i), off + h))

    pos = pl.BlockSpec((RET_ROWS, HEAD_DIM), lambda h, i: (rev(i), 0))
    per_head = lambda shape: pl.BlockSpec((None, *shape), lambda h, i: (h, 0, 0))
    rs = _Riders(riders, 14, 2)
    res = _pallas(
        _with_riders(body, 14, 2, 3, rs, (HEADS, nb)), name="retention_bwd", grid=(HEADS, nb),
        in_specs=[col(0), col(HEADS), col(2 * HEADS), col(3 * HEADS), pos, pos,
                  per_head((RET_SUB, RET_SUB)), per_head((RET_ROWS, HEAD_DIM)), per_head((RET_ROWS, HEAD_DIM)),
                  per_head((1, HEAD_DIM)), pl.BlockSpec((1, HEAD_DIM), lambda h, i: (0, h)),
                  col(0), pl.BlockSpec((None, RET_BLOCK_CHUNKS, HEAD_DIM, HEAD_DIM), lambda h, i: (h, rev(i), 0, 0)),
                  col(0), *rs.in_specs],
        out_specs=[ANY, per_head((1, HEAD_DIM)), *rs.out_specs],
        out_shape=[_sds((s, proj.shape[1]), BF16), _sds((HEADS, 1, HEAD_DIM), F32), *rs.out_shapes],
        input_output_aliases=rs.aliases,
        scratch_shapes=[pltpu.VMEM((HEAD_DIM, HEAD_DIM), F32), pltpu.VMEM((2, 4, RET_ROWS, HEAD_DIM), BF16),
                        pltpu.SemaphoreType.DMA((2, 4)), *rs.scratch],
        compiler_params=_params("arbitrary", "arbitrary"),
    )(proj, proj, proj, proj, cos_f, sin_f, decay, k_dec, q_dec, c_dec, gn_g, y, prev, dmix, *rs.arrays)
    return res[:2], rs.split(res[2:])


ATT_COL0 = 4 * HEADS
PAD_ROWS = LEFT_CHUNKS * CHUNK
NORM_ROWS = 512
GROUP_CHUNKS = 4
GROUP = GROUP_CHUNKS * CHUNK
WIN = (LEFT_CHUNKS + GROUP_CHUNKS) * CHUNK
MASKED = -1e30


def _qk_norm(x, g):
    return x * lax.rsqrt(jnp.mean(x * x, axis=-1, keepdims=True) + EPS) * g


def _band_probs(qb, kb, bias, g):
    sc = _dot(qb, kb, NT) * K_SCALE + bias
    win_chunk = lax.broadcasted_iota(jnp.int32, (GROUP, WIN), 1) // CHUNK
    sc = jnp.where(g * GROUP_CHUNKS - LEFT_CHUNKS + win_chunk >= 0, sc, MASKED)
    e = jnp.exp(sc - jnp.max(sc, axis=-1, keepdims=True))
    return e / jnp.sum(e, axis=-1, keepdims=True)


def _with_riders(core, n_in, n_out, n_scratch, rs, grid):
    n_rin, n_rout = len(rs.arrays), len(rs.out_shapes)
    if not rs.riders:
        return core
    steps = 1
    for n in grid:
        steps *= n

    def body(*refs):
        outs_at = n_in + n_rin
        scratch_at = outs_at + n_out + n_rout
        bound = rs.bind(refs[n_in:outs_at], refs[outs_at + n_out:scratch_at], refs[scratch_at + n_scratch:])
        step = 0
        for axis, n in enumerate(grid):
            step = step * n + pl.program_id(axis)
        pl.when(step == 0)(lambda: rs.run("start", bound))
        pl.when(step == int(steps * RIDER_MID))(lambda: rs.run("mid", bound))
        pl.when(step == int(steps * RIDER_LATE))(lambda: rs.run("late", bound))
        core(*refs[:n_in], *refs[outs_at:outs_at + n_out], *refs[scratch_at:scratch_at + n_scratch])
        pl.when(step == steps - 1)(lambda: rs.run("end", bound))

    return body


def _attention_fwd(proj, gq, gk, bias, mix, riders=()):
    s = proj.shape[0]
    rs = _Riders(riders, 7, 1)

    def body(q_ref, k_ref, v_ref, gq_ref, gk_ref, bias_ref, mix_ref, o_ref, kp_ref, vp_ref):
        kp_ref[0:PAD_ROWS, :] = jnp.zeros((PAD_ROWS, HEAD_DIM), kp_ref.dtype)
        vp_ref[0:PAD_ROWS, :] = jnp.zeros((PAD_ROWS, HEAD_DIM), vp_ref.dtype)
        gqv, gkv = gq_ref[...], gk_ref[...]

        def fill(b, carry):
            r0 = pl.multiple_of(b * NORM_ROWS, NORM_ROWS)
            kp_ref[pl.ds(PAD_ROWS + r0, NORM_ROWS), :] = _qk_norm(k_ref[pl.ds(r0, NORM_ROWS), :], gkv).astype(kp_ref.dtype)
            vp_ref[pl.ds(PAD_ROWS + r0, NORM_ROWS), :] = v_ref[pl.ds(r0, NORM_ROWS), :].astype(vp_ref.dtype)
            return carry

        lax.fori_loop(0, s // NORM_ROWS, fill, 0)

        def group(g, carry):
            r0 = pl.multiple_of(g * GROUP, GROUP)
            qn = _qk_norm(q_ref[pl.ds(r0, GROUP), :], gqv)
            p = _band_probs(qn, kp_ref[pl.ds(r0, WIN), :], bias_ref[...], g)
            o_ref[pl.ds(r0, GROUP), :] = _dot(p, vp_ref[pl.ds(r0, WIN), :], NN).astype(o_ref.dtype)
            return carry

        lax.fori_loop(0, s // GROUP, group, 0, unroll=2)

    def col(off):
        return pl.BlockSpec((s, HEAD_DIM), lambda h: (0, off + h))

    vec = pl.BlockSpec((1, HEAD_DIM), lambda h: (0, 0))
    res = _pallas(
        _with_riders(body, 7, 1, 2, rs, (HEADS,)), name="attention_fwd", grid=(HEADS,),
        in_specs=[col(ATT_COL0), col(ATT_COL0 + HEADS), col(ATT_COL0 + 2 * HEADS), vec, vec,
                  pl.BlockSpec((None, GROUP, WIN), lambda h: (h, 0, 0)), ANY, *rs.in_specs],
        out_specs=[col(HEADS), *rs.out_specs], out_shape=[_sds(mix.shape, mix.dtype), *rs.out_shapes],
        input_output_aliases={6: 0, **rs.aliases},
        scratch_shapes=[pltpu.VMEM((s + PAD_ROWS, HEAD_DIM), MXU_DTYPE), pltpu.VMEM((s + PAD_ROWS, HEAD_DIM), MXU_DTYPE),
                        *rs.scratch],
        compiler_params=_params("arbitrary"),
    )(proj, proj, proj, gq, gk, bias, mix, *rs.arrays)
    return res[0], rs.split(res[1:])


def _attention_bwd(proj, gq, gk, bias, dmix, dproj, riders=()):
    s = proj.shape[0]
    rs = _Riders(riders, 8, 4)

    def body(q_ref, k_ref, v_ref, gq_ref, gk_ref, bias_ref, do_ref, dproj_in_ref,
             dproj_ref, dgq_ref, dgk_ref, dbias_ref, kp_ref, vp_ref, dkp_ref, dvp_ref, dqn_ref, stage_ref, stage_sems):
        head = pl.program_id(0)

        def writes():
            return [pltpu.make_async_copy(
                stage_ref.at[g],
                dproj_ref.at[:, pl.ds(pl.multiple_of((ATT_COL0 + g * HEADS + head) * HEAD_DIM, HEAD_DIM), HEAD_DIM)],
                stage_sems.at[g]) for g in range(3)]

        kp_ref[0:PAD_ROWS, :] = jnp.zeros((PAD_ROWS, HEAD_DIM), kp_ref.dtype)
        vp_ref[0:PAD_ROWS, :] = jnp.zeros((PAD_ROWS, HEAD_DIM), vp_ref.dtype)
        dkp_ref[...] = jnp.zeros_like(dkp_ref)
        dvp_ref[...] = jnp.zeros_like(dvp_ref)
        dbias_ref[...] = jnp.zeros_like(dbias_ref)
        gqv, gkv = gq_ref[...], gk_ref[...]

        def fill(b, carry):
            r0 = pl.multiple_of(b * NORM_ROWS, NORM_ROWS)
            kp_ref[pl.ds(PAD_ROWS + r0, NORM_ROWS), :] = _qk_norm(k_ref[pl.ds(r0, NORM_ROWS), :], gkv).astype(kp_ref.dtype)
            vp_ref[pl.ds(PAD_ROWS + r0, NORM_ROWS), :] = v_ref[pl.ds(r0, NORM_ROWS), :].astype(vp_ref.dtype)
            return carry

        lax.fori_loop(0, s // NORM_ROWS, fill, 0)

        def group(g, carry):
            r0 = pl.multiple_of(g * GROUP, GROUP)
            qn = _qk_norm(q_ref[pl.ds(r0, GROUP), :], gqv)
            kb = kp_ref[pl.ds(r0, WIN), :]
            vb = vp_ref[pl.ds(r0, WIN), :]
            p = _band_probs(qn, kb, bias_ref[...], g)
            do = do_ref[pl.ds(r0, GROUP), :]
            dvp_ref[pl.ds(r0, WIN), :] += _dot(p, do, TN)
            dp = _dot(do, vb, NT)
            ds = p * (dp - jnp.sum(dp * p, axis=-1, keepdims=True))
            dbias_ref[...] += ds
            dss = ds * K_SCALE
            dqn_ref[pl.ds(r0, GROUP), :] = _dot(dss, kb, NN)
            dkp_ref[pl.ds(r0, WIN), :] += _dot(dss, qn, TN)
            return carry

        lax.fori_loop(0, s // GROUP, group, 0, unroll=2)

        @pl.when(head == 0)
        def _():
            dgq_ref[...] = jnp.zeros_like(dgq_ref)
            dgk_ref[...] = jnp.zeros_like(dgk_ref)

        @pl.when(head > 0)
        def _():
            for cp in writes():
                cp.wait()

        def norm_bwd(x, g, dn):
            rstd = lax.rsqrt(jnp.mean(x * x, axis=-1, keepdims=True) + EPS)
            xh = x * rstd
            dxh = dn * g
            return rstd * (dxh - xh * jnp.mean(dxh * xh, axis=-1, keepdims=True)), jnp.sum(dn * xh, axis=0, keepdims=True)

        def finish(b, carry):
            r0 = pl.multiple_of(b * NORM_ROWS, NORM_ROWS)
            rows = pl.ds(r0, NORM_ROWS)
            dq, dgq = norm_bwd(q_ref[rows, :], gqv, dqn_ref[rows, :])
            dk, dgk = norm_bwd(k_ref[rows, :], gkv, dkp_ref[pl.ds(PAD_ROWS + r0, NORM_ROWS), :])
            stage_ref[0, rows, :] = dq.astype(stage_ref.dtype)
            stage_ref[1, rows, :] = dk.astype(stage_ref.dtype)
            stage_ref[2, rows, :] = dvp_ref[pl.ds(PAD_ROWS + r0, NORM_ROWS), :].astype(stage_ref.dtype)
            dgq_ref[...] += dgq
            dgk_ref[...] += dgk
            return carry

        lax.fori_loop(0, s // NORM_ROWS, finish, 0)
        for cp in writes():
            cp.start()

        @pl.when(head == HEADS - 1)
        def _():
            for cp in writes():
                cp.wait()

    def col(off):
        return pl.BlockSpec((s, HEAD_DIM), lambda h: (0, off + h))

    vec = pl.BlockSpec((1, HEAD_DIM), lambda h: (0, 0))
    hbias = pl.BlockSpec((None, GROUP, WIN), lambda h: (h, 0, 0))
    res = _pallas(
        _with_riders(body, 8, 4, 7, rs, (HEADS,)), name="attention_bwd", grid=(HEADS,),
        in_specs=[col(ATT_COL0), col(ATT_COL0 + HEADS), col(ATT_COL0 + 2 * HEADS), vec, vec, hbias, col(HEADS), ANY,
                  *rs.in_specs],
        out_specs=[ANY, vec, vec, hbias, *rs.out_specs],
        out_shape=[_sds(dproj.shape, dproj.dtype), _sds((1, HEAD_DIM), F32), _sds((1, HEAD_DIM), F32),
                   _sds((HEADS, GROUP, WIN), F32), *rs.out_shapes],
        input_output_aliases={7: 0, **rs.aliases},
        scratch_shapes=[pltpu.VMEM((s + PAD_ROWS, HEAD_DIM), MXU_DTYPE), pltpu.VMEM((s + PAD_ROWS, HEAD_DIM), MXU_DTYPE),
                        pltpu.VMEM((s + PAD_ROWS, HEAD_DIM), F32), pltpu.VMEM((s + PAD_ROWS, HEAD_DIM), F32),
                        pltpu.VMEM((s, HEAD_DIM), F32), pltpu.VMEM((3, s, HEAD_DIM), BF16),
                        pltpu.SemaphoreType.DMA((3,)), *rs.scratch],
        compiler_params=_params("arbitrary"),
    )(proj, proj, proj, gq, gk, bias, dmix, dproj, *rs.arrays)
    return res[:4], rs.split(res[4:])


DIAG_SPLIT = (BAND + WIN - CHUNK) // 2


def _diag_bin(m):
    t = jnp.where(m < DIAG_SPLIT, m, m - WIN)
    return jnp.clip(LEFT_CHUNKS * CHUNK - t, -(CHUNK - 1), REL_CLIP) + (CHUNK - 1)


def _skew_rows(a, left):
    row = lax.broadcasted_iota(jnp.int32, (GROUP, WIN), 0)
    for b in range(GROUP.bit_length() - 1):
        step = 1 << b
        a = jnp.where(jnp.bitwise_and(row, step) != 0, pltpu.roll(a, WIN - step if left else step, 1), a)
    return a


def _rel_bias_expand(rel_bias):
    def body(rb_ref, o_ref):
        h = pl.program_id(0)
        bins = _diag_bin(lax.broadcasted_iota(jnp.int32, (8, WIN), 1))
        per_diag = lax.fori_loop(0, REL_SIZE, lambda r, acc: jnp.where(bins == r, rb_ref[h, r], acc),
                                 jnp.zeros((8, WIN), F32))
        table = _skew_rows(jnp.broadcast_to(per_diag[0:1], (GROUP, WIN)), left=False)
        row_chunk = lax.broadcasted_iota(jnp.int32, (GROUP, WIN), 0) // CHUNK
        col_chunk = lax.broadcasted_iota(jnp.int32, (GROUP, WIN), 1) // CHUNK
        in_band = jnp.logical_and(col_chunk >= row_chunk, col_chunk <= row_chunk + LEFT_CHUNKS)
        o_ref[...] = jnp.where(in_band, table, MASKED)

    return _pallas(body, name="rel_bias_expand", grid=(HEADS,), in_specs=[pl.BlockSpec(memory_space=pltpu.SMEM)],
                   out_specs=pl.BlockSpec((None, GROUP, WIN), lambda h: (h, 0, 0)),
                   out_shape=_sds((HEADS, GROUP, WIN), F32), compiler_params=_params("parallel"))(rel_bias)


def _rel_bias_fold(dbias):
    def body(a_ref, o_ref):
        diag = jnp.sum(_skew_rows(a_ref[...], left=True), axis=0, keepdims=True)
        onehot = (_diag_bin(lax.broadcasted_iota(jnp.int32, (WIN, REL_SIZE), 0))
                  == lax.broadcasted_iota(jnp.int32, (WIN, REL_SIZE), 1)).astype(MXU_DTYPE)
        rest = jnp.broadcast_to(diag, (8, WIN))
        out = jnp.zeros((8, REL_SIZE), F32)
        for _ in range(3):
            piece = rest.astype(BF16)
            out = out + _dot(piece, onehot, NN)
            rest = rest - piece.astype(F32)
        o_ref[...] = out[0:1]

    return _pallas(body, name="rel_bias_fold", grid=(HEADS,),
                   in_specs=[pl.BlockSpec((None, GROUP, WIN), lambda h: (h, 0, 0))],
                   out_specs=pl.BlockSpec((None, 1, REL_SIZE), lambda h: (h, 0, 0)),
                   out_shape=_sds((HEADS, 1, REL_SIZE), F32), compiler_params=_params("parallel"))(dbias)


def _place():
    return lax.axis_index("x"), lax.axis_index("y"), lax.axis_index("c")


def _other_chips(x, y):
    return [(1 - x, y), (x, 1 - y), (1 - x, 1 - y)]


class _Rider:
    reads, ins, new, n_sems = (), (), (), 1

    def start(self, reads, ins, new, send, recv):
        pass

    def mid(self, reads, ins, new, send, recv):
        pass

    def late(self, reads, ins, new, send, recv):
        pass

    def end(self, reads, ins, new, send, recv):
        pass


class _Riders:
    def __init__(self, riders, n_host_in, n_host_out):
        self.riders = list(riders)
        self.arrays, self.out_shapes, self.aliases, self.scratch = [], [], {}, []
        for r in self.riders:
            for t, a in enumerate(r.ins):
                self.aliases[n_host_in + len(self.arrays) + len(r.reads) + t] = n_host_out + len(self.out_shapes) + t
            self.arrays += [*r.reads, *r.ins]
            self.out_shapes += [_sds(a.shape, a.dtype) for a in r.ins] + list(r.new)
            self.scratch += [pltpu.SemaphoreType.DMA((r.n_sems,)), pltpu.SemaphoreType.DMA((r.n_sems,))]
        self.in_specs = [ANY] * len(self.arrays)
        self.out_specs = [ANY] * len(self.out_shapes)

    def bind(self, in_refs, out_refs, scratch_refs):
        bound, i, o = [], 0, 0
        for t, r in enumerate(self.riders):
            reads = in_refs[i:i + len(r.reads)]
            i += len(r.reads) + len(r.ins)
            ins = out_refs[o:o + len(r.ins)]
            new = out_refs[o + len(r.ins):o + len(r.ins) + len(r.new)]
            o += len(r.ins) + len(r.new)
            bound.append((reads, ins, new, scratch_refs[2 * t], scratch_refs[2 * t + 1]))
        return bound

    def run(self, phase, bound):
        for r, b in zip(self.riders, bound):
            getattr(r, phase)(*b)

    def split(self, outs):
        res, o = [], 0
        for r in self.riders:
            n = len(r.ins) + len(r.new)
            res.append(list(outs[o:o + n]))
            o += n
        return res


def _run_riders(name, riders):
    rs = _Riders(riders, 0, 0)
    n_in, n_out = len(rs.arrays), len(rs.out_shapes)

    def body(*refs):
        bound = rs.bind(refs[:n_in], refs[n_in:n_in + n_out], refs[n_in + n_out:])
        rs.run("start", bound)
        rs.run("mid", bound)
        rs.run("late", bound)
        rs.run("end", bound)

    outs = _pallas(body, name=name, in_specs=rs.in_specs, out_specs=rs.out_specs, out_shape=rs.out_shapes,
                   input_output_aliases=rs.aliases, scratch_shapes=rs.scratch)(*rs.arrays)
    return rs.split(outs)


class _GatherRider(_Rider):
    X_LINK, Y_LINK, Y_PASS, X_PASS, D2D_X, D2D_Y, D2D_DIAG, N_SEMS = 0, 1, 2, 3, 4, 5, 6, 7

    def __init__(self, blocks, part=(0, 1, 1)):
        self.ins = tuple(blocks)
        self.part = part
        self.n_sems = self.N_SEMS * len(blocks)

    def _copy(self, out, send, recv, w, sem, chip_from, cc, to, sub=None):
        hr = self.ins[w].shape[1] // 2
        lo, hi, n = self.part
        first, size = cc * hr + lo * (hr // n), (hi - lo) * (hr // n)
        if sub is not None:
            size //= 2
            first += sub * size
        piece = out[w].at[2 * chip_from[0] + chip_from[1], pl.ds(first, size), :]
        return pltpu.make_async_remote_copy(src_ref=piece, dst_ref=piece, send_sem=send.at[self.N_SEMS * w + sem],
                                            recv_sem=recv.at[self.N_SEMS * w + sem], device_id=to, device_id_type=MESH)

    def _sent(self, out, send, recv, w):
        x, y, c = _place()
        me, sib = (x, y), (x, y, 1 - c)
        xn, yn, diag = _other_chips(x, y)
        cp = functools.partial(self._copy, out, send, recv, w)
        return [("start", cp(self.X_LINK, me, c, (*xn, c))), ("start", cp(self.Y_LINK, me, c, (*yn, c))),
                ("mid_x", cp(self.D2D_X, xn, c, sib)), ("mid_x", cp(self.Y_PASS, xn, c, (*yn, c), sub=0)),
                ("mid_y", cp(self.D2D_Y, yn, c, sib)), ("mid_y", cp(self.X_PASS, yn, c, (*xn, c), sub=1)),
                ("late", cp(self.D2D_DIAG, diag, c, sib))]

    def _go(self, out, send, recv, phase):
        for w in range(len(self.ins)):
            for ph, copy in self._sent(out, send, recv, w):
                if ph == phase:
                    copy.start()

    def start(self, reads, out, new, send, recv):
        self._go(out, send, recv, "start")

    def mid(self, reads, out, new, send, recv):
        x, y, c = _place()
        xn, yn, _ = _other_chips(x, y)
        for w in range(len(self.ins)):
            self._copy(out, send, recv, w, self.X_LINK, xn, c, (x, y, c)).wait_recv()
        self._go(out, send, recv, "mid_x")
        for w in range(len(self.ins)):
            self._copy(out, send, recv, w, self.Y_LINK, yn, c, (x, y, c)).wait_recv()
        self._go(out, send, recv, "mid_y")

    def late(self, reads, out, new, send, recv):
        x, y, c = _place()
        diag = _other_chips(x, y)[2]
        for w in range(len(self.ins)):
            self._copy(out, send, recv, w, self.Y_PASS, diag, c, (x, y, c), sub=0).wait_recv()
            self._copy(out, send, recv, w, self.X_PASS, diag, c, (x, y, c), sub=1).wait_recv()
        self._go(out, send, recv, "late")

    def end(self, reads, out, new, send, recv):
        x, y, c = _place()
        xn, yn, diag = _other_chips(x, y)
        for w in range(len(self.ins)):
            for sem, chip in ((self.D2D_X, xn), (self.D2D_Y, yn), (self.D2D_DIAG, diag)):
                self._copy(out, send, recv, w, sem, chip, 1 - c, (x, y, c)).wait_recv()
        for w in range(len(self.ins)):
            for _, copy in self._sent(out, send, recv, w):
                copy.wait_send()


class _SwapRider(_Rider):
    def __init__(self, grads):
        self.reads = tuple(grads)
        self.new = tuple(_sds((N_CHIPS, g.shape[1] // 2, g.shape[2]), g.dtype) for g in grads)
        self.n_sems = len(grads)

    def _copies(self, src, new, send, recv):
        x, y, c = _place()
        copies = []
        for w in range(len(self.reads)):
            hr = self.reads[w].shape[1] // 2
            copies.append(pltpu.make_async_remote_copy(
                src_ref=src[w].at[:, pl.ds((1 - c) * hr, hr), :], dst_ref=new[w],
                send_sem=send.at[w], recv_sem=recv.at[w], device_id=(x, y, 1 - c), device_id_type=MESH))
        return copies

    def start(self, src, ins, new, send, recv):
        for cp in self._copies(src, new, send, recv):
            cp.start()

    def end(self, src, ins, new, send, recv):
        for cp in self._copies(src, new, send, recv):
            cp.wait()


def _add_half(g, got, c_arr, name):
    nk, r, cols = g.shape
    hr = r // 2
    tr = min(hr, 256)
    nb = hr // tr

    def body(c_ref, g_ref, got_ref, o_ref):
        o_ref[...] = (g_ref[...].astype(F32) + got_ref[...].astype(F32)).astype(o_ref.dtype)

    grid_spec = pltpu.PrefetchScalarGridSpec(
        num_scalar_prefetch=1, grid=(nk, nb),
        in_specs=[pl.BlockSpec((None, tr, cols), lambda k, i, c_ref: (k, c_ref[0] * nb + i, 0)),
                  pl.BlockSpec((None, tr, cols), lambda k, i, c_ref: (k, i, 0))],
        out_specs=pl.BlockSpec((None, tr, cols), lambda k, i, c_ref: (k, i, 0)))
    return _pallas(body, name=name, grid_spec=grid_spec, out_shape=_sds((nk, hr, cols), g.dtype),
                   compiler_params=_params("parallel", "parallel"))(c_arr, g, got)


class _SendPartialsRider(_Rider):
    def __init__(self, parts, got=None, part=(0, 1, 1)):
        self.reads = tuple(parts)
        if got is None:
            self.new = tuple(_sds((N_CHIPS - 1, *p.shape[1:]), p.dtype) for p in parts)
        else:
            self.ins = tuple(got)
        self.part = part
        self.n_sems = 3 * len(parts)

    def _copies(self, src, ins, new, send, recv):
        x, y, c = _place()
        land = ins if self.ins else new
        lo, hi, n = self.part
        copies = []
        for w in range(len(self.reads)):
            pr = self.reads[w].shape[1] // n
            rows = pl.ds(lo * pr, (hi - lo) * pr)
            for j, chip in enumerate(_other_chips(x, y)):
                copies.append(pltpu.make_async_remote_copy(
                    src_ref=src[w].at[2 * chip[0] + chip[1], rows, :], dst_ref=land[w].at[j, rows, :],
                    send_sem=send.at[3 * w + j], recv_sem=recv.at[3 * w + j], device_id=(*chip, c), device_id_type=MESH))
        return copies

    def start(self, src, ins, new, send, recv):
        for cp in self._copies(src, ins, new, send, recv):
            cp.start()

    def end(self, src, ins, new, send, recv):
        for cp in self._copies(src, ins, new, send, recv):
            cp.wait()


def _sum_partials(part, got, kc_arr, name):
    _, hr, cols = part.shape
    tr = min(hr, 256)
    nb = hr // tr

    def body(kc_ref, p_ref, g0_ref, g1_ref, g2_ref, o_ref):
        o_ref[...] = ((p_ref[...].astype(F32) + g0_ref[...].astype(F32)) + g1_ref[...].astype(F32)) + g2_ref[...].astype(F32)

    slot = lambda j: pl.BlockSpec((None, tr, cols), lambda i, kc_ref: (j, i, 0))
    grid_spec = pltpu.PrefetchScalarGridSpec(
        num_scalar_prefetch=1, grid=(nb,),
        in_specs=[pl.BlockSpec((None, tr, cols), lambda i, kc_ref: (kc_ref[0], i, 0)), slot(0), slot(1), slot(2)],
        out_specs=pl.BlockSpec((tr, cols), lambda i, kc_ref: (kc_ref[1] * nb + i, 0)))
    return _pallas(body, name=name, grid_spec=grid_spec, out_shape=_sds((2 * hr, cols), F32),
                   compiler_params=_params("parallel"))(kc_arr, part, got, got, got)


class _ShareRider(_Rider):
    def __init__(self, grads):
        self.ins = tuple(grads)
        self.n_sems = len(grads)

    def _copies(self, out, send, recv):
        x, y, c = _place()
        copies = []
        for w in range(len(self.ins)):
            hr = self.ins[w].shape[0] // 2
            mine = out[w].at[pl.ds(c * hr, hr), :]
            copies.append(pltpu.make_async_remote_copy(
                src_ref=mine, dst_ref=mine, send_sem=send.at[w], recv_sem=recv.at[w],
                device_id=(x, y, 1 - c), device_id_type=MESH))
        return copies

    def start(self, reads, out, new, send, recv):
        for cp in self._copies(out, send, recv):
            cp.start()

    def end(self, reads, out, new, send, recv):
        for cp in self._copies(out, send, recv):
            cp.wait()


def _small_allreduce_adamw(g_part, w, m, v):
    rows = g_part.shape[0]

    def body(g_ref, w_ref, m_ref, v_ref, go_ref, d_ref, nm_ref, nv_ref, all_ref, send_sems, recv_sems):
        x, y, c = _place()
        me = 4 * x + 2 * y + c
        all_ref[me] = g_ref[...]
        copies = []
        for r in range(1, 8):
            dx, dy, dc = (r >> 2) & 1, (r >> 1) & 1, r & 1
            peer = (1 - x if dx else x, 1 - y if dy else y, 1 - c if dc else c)
            copies.append(pltpu.make_async_remote_copy(
                src_ref=g_ref, dst_ref=all_ref.at[me], send_sem=send_sems.at[r - 1], recv_sem=recv_sems.at[r - 1],
                device_id=peer, device_id_type=MESH))
        for cp in copies:
            cp.start()
        for cp in copies:
            cp.wait()
        tot = all_ref[0]
        for d in range(1, 8):
            tot = tot + all_ref[d]
        go_ref[...] = tot
        d_ref[...], nm_ref[...], nv_ref[...] = _adamw_math(w_ref[...], tot, m_ref[...], v_ref[...])

    vm = pl.BlockSpec(memory_space=pltpu.VMEM)
    return _pallas(
        body, name="small_allreduce_adamw", in_specs=[vm] * 4, out_specs=[vm] * 4,
        out_shape=[_sds((rows, 128), F32)] * 4,
        scratch_shapes=[pltpu.VMEM((8, rows, 128), F32), pltpu.SemaphoreType.DMA((7,)), pltpu.SemaphoreType.DMA((7,))],
    )(g_part, w, m, v)


SMALL_SIZES = (2048, 1024, 128, 128, HEADS * REL_SIZE, 2048)
SMALL_PART_ROWS = tuple(-(-size // 1024) * 8 for size in SMALL_SIZES)
SMALL_ROWS = sum(SMALL_PART_ROWS)


def _pack_small(parts):
    rows = []
    for p, size, nr in zip(parts, SMALL_SIZES, SMALL_PART_ROWS):
        rows.append(jnp.pad(p.reshape(-1), (0, nr * 128 - size)).reshape(nr, 128))
    return jnp.concatenate(rows, axis=0)


def _unpack_small(slab, shapes):
    out, off = [], 0
    for size, nr, shape in zip(SMALL_SIZES, SMALL_PART_ROWS, shapes):
        out.append(slab[off:off + nr].reshape(-1)[:size].reshape(shape))
        off += nr
    return out


def kernel(x, norm1_g, w_in, ret_norm_g, q_norm_g, k_norm_g, rel_bias, w_out, norm2_g, w_ff1, w_ff2, loss_target, m_norm1_g, m_w_in, m_ret_norm_g, m_q_norm_g, m_k_norm_g, m_rel_bias, m_w_out, m_norm2_g, m_w_ff1, m_w_ff2, v_norm1_g, v_w_in, v_ret_norm_g, v_q_norm_g, v_k_norm_g, v_rel_bias, v_w_out, v_norm2_g, v_w_ff1, v_w_ff2):
    xs = x[0]
    tgt = loss_target[0]
    s, d = xs.shape
    d_in = N_CHIPS * w_in.shape[2]
    d_ff = N_CHIPS * w_ff1.shape[2]
    in_sh, ff_sh = w_in.shape[2], w_ff1.shape[2]
    tm = min(s, 1024)
    gi = s // tm
    c_arr = lax.axis_index("c").astype(jnp.int32).reshape(1)
    k_arr = (2 * lax.axis_index("x") + lax.axis_index("y")).astype(jnp.int32).reshape(1)
    tables = _tables(s)
    bias = _rel_bias_expand(rel_bias[0])

    blk_in, blk_out, blk_ff1, blk_ff2 = (
        _cast_bf16(w_in[0], k_arr, "cast_w_in"), _cast_bf16(w_out[0], k_arr, "cast_w_out"),
        _cast_bf16(w_ff1[0], k_arr, "cast_w_ff1"), _cast_bf16(w_ff2[0], k_arr, "cast_w_ff2"))
    ((wg_in,),) = _run_riders("all_gather_w_in", [_GatherRider([blk_in])])

    h1 = _rmsnorm_fwd(xs, norm1_g, "rmsnorm1")
    tn_in = in_sh // 2
    tk = d
    (proj,), ((wg_ff1,),) = _mm(
        "proj", h1, wg_in, NN, (gi, 2 * N_CHIPS, d // tk),
        pl.BlockSpec((tm, tk), lambda i, j, k: (i, k)),
        pl.BlockSpec((None, tk, tn_in), lambda i, j, k: (j // 2, k, j % 2)),
        [_sds((s, d_in), F32)], [pl.BlockSpec((tm, tn_in), lambda i, j, k: (i, j))], (tm, tn_in),
        riders=[_GatherRider([blk_ff1], (0, 3, 4))])
    (mix, y_ret, prev), ((wg_ff1,),) = _retention_fwd(proj, ret_norm_g, tables, riders=[_GatherRider([wg_ff1], (3, 4, 4))])
    mix, ((wg_out,), (wg_ff2,)) = _attention_fwd(
        proj, q_norm_g, k_norm_g, bias, mix, riders=[_GatherRider([blk_out]), _GatherRider([blk_ff2], (0, 1, 4))])
    wg_out = wg_out.reshape(d, d)
    tn = 1024
    tile = pl.BlockSpec((tm, tn), lambda i, j, k: (i, j))
    def residual_norm(acc, res, g):
        x1v = res + acc
        yv = x1v * lax.rsqrt(jnp.mean(x1v * x1v, axis=-1, keepdims=True) + EPS)
        return x1v, yv * g

    tmo = min(s, 512)
    rows = pl.BlockSpec((tmo, d), lambda i, j, k: (i, 0))
    x1, h2 = _mm("out_proj", mix, wg_out, NN, (s // tmo, 1, 1),
                 rows, pl.BlockSpec((d, d), lambda i, j, k: (0, 0)),
                 [_sds((s, d), F32), _sds((s, d), BF16)], [rows, rows], (tmo, d),
                 extras=(xs, norm2_g), extra_specs=(rows, pl.BlockSpec((1, d), lambda i, j, k: (0, 0))),
                 epi=residual_norm)
    tn_ff = min(ff_sh, 1024)
    per = ff_sh // tn_ff

    def relu2(acc):
        r = jnp.maximum(acc, 0.0)
        return acc, r * r

    (u, act), ((wg_ff2,),) = _mm(
        "ff1", h2, wg_ff1, NN, (gi, N_CHIPS * per, d // tk),
        pl.BlockSpec((tm, tk), lambda i, j, k: (i, k)),
        pl.BlockSpec((None, tk, tn_ff), lambda i, j, k: (j // per, k, j % per)),
        [_sds((s, d_ff), F32), _sds((s, d_ff), BF16)],
        [pl.BlockSpec((tm, tn_ff), lambda i, j, k: (i, j))] * 2, (tm, tn_ff), epi=relu2,
        riders=[_GatherRider([wg_ff2], (1, 4, 4))])
    wg_ff2 = wg_ff2.reshape(d_ff, d)

    def loss_epi(acc, res, t):
        diff = (res + acc) - t
        dy = diff / d
        return dy, dy, jnp.sum(diff * diff, axis=0, keepdims=True)

    tk2 = min(tk, 2048)
    dy, dyb, loss_cols = _mm(
        "ff2_loss", act, wg_ff2, NN, (gi, d // tn, d_ff // tk2),
        pl.BlockSpec((tm, tk2), lambda i, j, k: (i, k)), pl.BlockSpec((tk2, tn), lambda i, j, k: (k, j)),
        [_sds((s, d), F32), _sds((s, d), BF16), _sds((gi, 1, d), F32)],
        [tile, tile, pl.BlockSpec((None, 1, tn), lambda i, j, k: (i, 0, j))], (tm, tn),
        extras=(x1, tgt), extra_specs=(tile, tile), epi=loss_epi)
    loss = lax.psum(0.5 * jnp.sum(loss_cols) / d, ("x", "y", "c"))

    (du,) = _mm("d_act", dyb, wg_ff2, NT, (gi, d_ff // tn, d // tk),
                pl.BlockSpec((tm, tk), lambda i, j, k: (i, k)), pl.BlockSpec((tn, tk), lambda i, j, k: (j, k)),
                [_sds((s, d_ff), BF16)], [tile], (tm, tn), extras=(u,), extra_specs=(tile,),
                epi=lambda acc, uu: (acc * (2.0 * jnp.maximum(uu, 0.0)),))
    ts = min(s, 2048)
    wtile = pl.BlockSpec((tn, tn), lambda i, j, k: (i, j))
    (g_ff2,) = _mm("dw_ff2", act, dyb, TN, (d_ff // tn, d // tn, s // ts),
                   pl.BlockSpec((ts, tn), lambda i, j, k: (k, i)), pl.BlockSpec((ts, tn), lambda i, j, k: (k, j)),
                   [_sds((d_ff, d), BF16)], [wtile], (tn, tn))
    g_ff2 = g_ff2.reshape(N_CHIPS, d_ff // N_CHIPS, d)
    (g_ff1,), ((got_ff2,),) = _mm(
        "dw_ff1", h2, du, TN, (d // tn, N_CHIPS * per, s // ts),
        pl.BlockSpec((ts, tn), lambda i, j, k: (k, i)), pl.BlockSpec((ts, tn_ff), lambda i, j, k: (k, j)),
        [_sds((N_CHIPS, d, ff_sh), BF16)],
        [pl.BlockSpec((None, tn, tn_ff), lambda i, j, k: (j // per, i, j % per))], (tn, tn_ff),
        riders=[_SwapRider([g_ff2])])
    p_ff2 = _add_half(g_ff2, got_ff2, c_arr, "chip_partial_w_ff2")
    def norm2_bwd(dh2, x1v, g, dyv):
        rstd = lax.rsqrt(jnp.mean(x1v * x1v, axis=-1, keepdims=True) + EPS)
        xh = x1v * rstd
        dxh = dh2 * g
        dxv = dyv + rstd * (dxh - xh * jnp.mean(dxh * xh, axis=-1, keepdims=True))
        return dxv, dxv, jnp.sum(dh2 * xh, axis=0, keepdims=True)

    tkf = min(1024, ff_sh)
    kper = ff_sh // tkf
    vec_d = pl.BlockSpec((1, d), lambda i, j, k: (0, 0))
    (dx1, dx1b, g_norm2), ((got2_ff2,), (got_ff1,)) = _mm(
        "d_h2", du, wg_ff1, NT, (s // tmo, 1, d_ff // tkf),
        pl.BlockSpec((tmo, tkf), lambda i, j, k: (i, k)),
        pl.BlockSpec((None, d, tkf), lambda i, j, k: (k // kper, 0, k % kper)),
        [_sds((s, d), F32), _sds((s, d), BF16), _sds((1, d), F32)], [rows, rows, vec_d], (tmo, d),
        extras=(x1, norm2_g, dy), extra_specs=(rows, vec_d, rows), epi=norm2_bwd, sum_outs=1,
        riders=[_SendPartialsRider([p_ff2], part=(0, 3, 4)), _SwapRider([g_ff1])])
    p_ff1 = _add_half(g_ff1, got_ff1, c_arr, "chip_partial_w_ff1")

    (dmix,) = _mm("d_mix", dx1b, wg_out, NT, (gi, d // tn, d // tk),
                  pl.BlockSpec((tm, tk), lambda i, j, k: (i, k)), pl.BlockSpec((tn, tk), lambda i, j, k: (j, k)),
                  [_sds((s, d), F32)], [tile], (tm, tn))
    (g_out,) = _mm("dw_out", mix, dx1b, TN, (d // tn, d // tn, s // ts),
                   pl.BlockSpec((ts, tn), lambda i, j, k: (k, i)), pl.BlockSpec((ts, tn), lambda i, j, k: (k, j)),
                   [_sds((d, d), BF16)], [wtile], (tn, tn))
    g_out = g_out.reshape(N_CHIPS, d // N_CHIPS, d)
    (dproj, g_gn), ((got2_ff2,), (got2_ff1,), (got_out,)) = _retention_bwd(
        proj, ret_norm_g, tables, y_ret, prev, dmix,
        riders=[_SendPartialsRider([p_ff2], got=[got2_ff2], part=(3, 4, 4)), _SendPartialsRider([p_ff1], part=(0, 2, 4)),
                _SwapRider([g_out])])
    p_out = _add_half(g_out, got_out, c_arr, "chip_partial_w_out")
    (dproj, g_gq, g_gk, dbias), ((got2_ff1,), (got2_out,)) = _attention_bwd(
        proj, q_norm_g, k_norm_g, bias, dmix, dproj,
        riders=[_SendPartialsRider([p_ff1], got=[got2_ff1], part=(2, 4, 4)), _SendPartialsRider([p_out])])
    g_rel = _rel_bias_fold(dbias)
    names = ["w_in", "w_out", "w_ff1", "w_ff2"]
    kc_arr = jnp.concatenate([k_arr, c_arr])
    early = [_sum_partials(p, r, kc_arr, "sum_partials_" + nm)
             for p, r, nm in zip((p_out, p_ff1, p_ff2), (got2_out, got2_ff1, got2_ff2), names[1:])]
    (g_in,), (early,) = _mm(
        "dw_in", h1, dproj, TN, (d // tn, 2 * N_CHIPS, s // ts),
        pl.BlockSpec((ts, tn), lambda i, j, k: (k, i)), pl.BlockSpec((ts, tn_in), lambda i, j, k: (k, j)),
        [_sds((N_CHIPS, d, in_sh), BF16)],
        [pl.BlockSpec((None, tn, tn_in), lambda i, j, k: (j // 2, i, j % 2))], (tn, tn_in), riders=[_ShareRider(early)])
    ((got_in,),) = _run_riders("grad_swap_w_in", [_SwapRider([g_in])])
    p_in = _add_half(g_in, got_in, c_arr, "chip_partial_w_in")
    (dh1,), ((got2_in,),) = _mm(
        "d_h1", dproj, wg_in, NT, (gi, d // tn, N_CHIPS),
        pl.BlockSpec((tm, in_sh), lambda i, j, k: (i, k)),
        pl.BlockSpec((None, tn, in_sh), lambda i, j, k: (k, j, 0)),
        [_sds((s, d), F32)], [tile], (tm, tn), riders=[_SendPartialsRider([p_in])])
    grad_x, _, g_norm1 = _rmsnorm_bwd(xs, norm1_g, dh1, dx1, "rmsnorm1_bwd")
    ((g_w_in,),) = _run_riders("grad_share_w_in", [_ShareRider([_sum_partials(p_in, got2_in, kc_arr, "sum_partials_w_in")])])
    g_big = [g_w_in, *early]
    big = []
    for g, w, m, v, nm in zip(g_big, (w_in, w_out, w_ff1, w_ff2), (m_w_in, m_w_out, m_w_ff1, m_w_ff2),
                              (v_w_in, v_w_out, v_w_ff1, v_w_ff2), names):
        g, delta, new_m, new_v = _adamw(w[0], g, m[0], v[0], "adamw_" + nm)
        big.append((g[None], delta[None], new_m[None], new_v[None]))

    small_w = (norm1_g, ret_norm_g, q_norm_g, k_norm_g, rel_bias, norm2_g)
    small_m = (m_norm1_g, m_ret_norm_g, m_q_norm_g, m_k_norm_g, m_rel_bias, m_norm2_g)
    small_v = (v_norm1_g, v_ret_norm_g, v_q_norm_g, v_k_norm_g, v_rel_bias, v_norm2_g)
    shapes = [p.shape for p in small_w]
    g_small = _pack_small([g_norm1, g_gn, g_gq, g_gk, g_rel, g_norm2])
    sg, sd, sm, sv = (_unpack_small(a, shapes) for a in _small_allreduce_adamw(
        g_small, _pack_small(small_w), _pack_small(small_m), _pack_small(small_v)))

    def ordered(kind):
        sm_ = (sg, sd, sm, sv)[kind]
        return (sm_[0], big[0][kind], sm_[1], sm_[2], sm_[3], sm_[4], big[1][kind], sm_[5], big[2][kind], big[3][kind])

    return (loss, grad_x[None], *ordered(0), *ordered(1), *ordered(2), *ordered(3))
```

```python
import functools

import jax
import jax.numpy as jnp
from jax import lax
from jax.experimental import pallas as pl
from jax.experimental.pallas import tpu as pltpu

F32 = jnp.float32
BF16 = jnp.bfloat16
MXU_DTYPE = jnp.bfloat16

CHUNK = 64
HEADS = 8
HEAD_DIM = 128
LEFT_CHUNKS = 8
BAND = (LEFT_CHUNKS + 1) * CHUNK
REL_CLIP = 128
REL_SIZE = (CHUNK - 1) + REL_CLIP + 1
RET_BLOCK_CHUNKS = 8
RET_ROWS = RET_BLOCK_CHUNKS * CHUNK
RET_SUB = 256
ROPE_BASE = 10000.0
EPS = 1e-6
GN_EPS = 1e-5
ADAM_LR, ADAM_B1, ADAM_B2, ADAM_EPS, ADAM_WD, ADAM_STEP = 0.001, 0.9, 0.999, 1e-08, 0.01, 10
N_CHIPS = 4
VMEM_LIMIT = 56 * 1024 * 1024
MESH = pl.DeviceIdType.MESH
ANY = pl.BlockSpec(memory_space=pl.ANY)

NN = (((1,), (0,)), ((), ()))
NT = (((1,), (1,)), ((), ()))
TN = (((0,), (0,)), ((), ()))


def _pallas(body, **kw):
    return pl.pallas_call(body, **kw)


def _params(*sem):
    return pltpu.CompilerParams(dimension_semantics=sem, vmem_limit_bytes=VMEM_LIMIT)


def _dot(a, b, dims):
    return lax.dot_general(a.astype(MXU_DTYPE), b.astype(MXU_DTYPE), dims, preferred_element_type=F32)


RIDER_MID, RIDER_LATE = 0.5, 0.8


def _mm(name, a, b, dims, grid, a_spec, b_spec, outs, o_specs, acc_shape, extras=(), extra_specs=(), epi=None,
        riders=(), through=None):
    ni, nj, nk = grid
    n_ex, n_out = len(extras), len(outs)
    n_in = 2 + n_ex
    rs = _Riders(riders, n_in, n_out)
    n_rin, n_rout = len(rs.arrays), len(rs.out_shapes)
    steps = ni * nj * nk

    held = [] if through is None else [through]

    def body(*refs):
        a_ref, b_ref = refs[0], refs[1]
        ex_refs = refs[2:n_in]
        outs_at = n_in + n_rin + len(held)
        o_refs = refs[outs_at:outs_at + n_out]
        acc_ref = refs[outs_at + n_out + n_rout]
        k = pl.program_id(2)
        if riders:
            bound = rs.bind(refs[n_in:n_in + n_rin], refs[outs_at + n_out:outs_at + n_out + n_rout],
                            refs[outs_at + n_out + n_rout + 1:])
            step = (pl.program_id(0) * nj + pl.program_id(1)) * nk + k
            pl.when(step == 0)(lambda: rs.run("start", bound))
            pl.when(step == int(steps * RIDER_MID))(lambda: rs.run("mid", bound))
            pl.when(step == int(steps * RIDER_LATE))(lambda: rs.run("late", bound))

        def finish(acc):
            vals = epi(acc, *[r[...] for r in ex_refs]) if epi is not None else (acc,)
            for r, v in zip(o_refs, vals):
                r[...] = v.astype(r.dtype)

        if nk == 1:
            finish(_dot(a_ref[...], b_ref[...], dims))
        else:
            @pl.when(k == 0)
            def _():
                acc_ref[...] = jnp.zeros_like(acc_ref)

            acc_ref[...] += _dot(a_ref[...], b_ref[...], dims)
            pl.when(k == nk - 1)(lambda: finish(acc_ref[...]))

        if riders:
            pl.when(step == steps - 1)(lambda: rs.run("end", bound))

    res = _pallas(
        body, name=name, grid=grid, in_specs=[a_spec, b_spec, *extra_specs, *rs.in_specs, *[ANY for _ in held]],
        out_specs=[*o_specs, *rs.out_specs], out_shape=[*outs, *rs.out_shapes],
        input_output_aliases={**rs.aliases, **{n_in + n_rin: 0 for _ in held}},
        scratch_shapes=[pltpu.VMEM(acc_shape if nk > 1 else (8, 128), F32), *rs.scratch],
        compiler_params=_params(*(("arbitrary",) * 3 if riders else ("parallel", "parallel", "arbitrary"))),
    )(a, b, *extras, *rs.arrays, *held)
    return (res[:n_out], rs.split(res[n_out:])) if riders else res


def _sds(shape, dtype):
    return jax.ShapeDtypeStruct(shape, dtype)


def _cast_bf16(w, k_arr, name, cols=(0, 1)):
    r, c = w.shape[0], w.shape[1] // cols[1]
    tr = min(r, 256)

    def body(k_ref, w_ref, o_ref):
        o_ref[...] = w_ref[...].astype(BF16)

    grid_spec = pltpu.PrefetchScalarGridSpec(
        num_scalar_prefetch=1, grid=(r // tr,), in_specs=[pl.BlockSpec((tr, c), lambda i, k_ref: (i, cols[0]))],
        out_specs=pl.BlockSpec((None, tr, c), lambda i, k_ref: (k_ref[0], i, 0)))
    return _pallas(body, name=name, grid_spec=grid_spec, out_shape=_sds((N_CHIPS, r, c), BF16),
                   compiler_params=_params("parallel"))(k_arr, w)


def _rmsnorm_fwd(x, g, name):
    s, d = x.shape
    tr = 256

    def body(x_ref, g_ref, o_ref):
        xv = x_ref[...]
        y = xv * lax.rsqrt(jnp.mean(xv * xv, axis=-1, keepdims=True) + EPS)
        o_ref[...] = (y * g_ref[...]).astype(o_ref.dtype)

    return _pallas(body, name=name, grid=(s // tr,),
                   in_specs=[pl.BlockSpec((tr, d), lambda i: (i, 0)), pl.BlockSpec((1, d), lambda i: (0, 0))],
                   out_specs=pl.BlockSpec((tr, d), lambda i: (i, 0)), out_shape=_sds((s, d), BF16),
                   compiler_params=_params("parallel"))(x, g)


def _rmsnorm_bwd(x, g, dh, res, name, riders=()):
    s, d = x.shape
    tr = 256

    def body(x_ref, g_ref, dh_ref, res_ref, dx_ref, dxb_ref, dg_ref):
        i = pl.program_id(0)
        xv = x_ref[...]
        rstd = lax.rsqrt(jnp.mean(xv * xv, axis=-1, keepdims=True) + EPS)
        xh = xv * rstd
        dhv = dh_ref[...]

        @pl.when(i == 0)
        def _():
            dg_ref[...] = jnp.zeros_like(dg_ref)

        dg_ref[...] += jnp.sum(dhv * xh, axis=0, keepdims=True)
        dxh = dhv * g_ref[...]
        dx = res_ref[...] + rstd * (dxh - xh * jnp.mean(dxh * xh, axis=-1, keepdims=True))
        dx_ref[...] = dx
        dxb_ref[...] = dx.astype(BF16)

    row = pl.BlockSpec((tr, d), lambda i: (i, 0))
    vec = pl.BlockSpec((1, d), lambda i: (0, 0))
    rs = _Riders(riders, 4, 3)
    out = _pallas(_with_riders(body, 4, 3, 0, rs, (s // tr,)), name=name, grid=(s // tr,),
                  in_specs=[row, vec, row, row, *rs.in_specs], out_specs=[row, row, vec, *rs.out_specs],
                  out_shape=[_sds((s, d), F32), _sds((s, d), BF16), _sds((1, d), F32), *rs.out_shapes],
                  input_output_aliases=rs.aliases, scratch_shapes=rs.scratch,
                  compiler_params=_params("arbitrary"))(x, g, dh, res, *rs.arrays)
    return (out[:3], rs.split(out[3:])) if riders else out


def _adamw_math(w, g, m, v):
    m = ADAM_B1 * m + (1.0 - ADAM_B1) * g
    v = ADAM_B2 * v + (1.0 - ADAM_B2) * (g * g)
    m_hat = m / (1.0 - ADAM_B1 ** ADAM_STEP)
    v_hat = v / (1.0 - ADAM_B2 ** ADAM_STEP)
    delta = -ADAM_LR * (m_hat / (jnp.sqrt(v_hat) + ADAM_EPS) + ADAM_WD * w)
    return delta, m, v


def _adamw(w, g, m, v, name):
    r, c = w.shape
    tr = 128

    def body(w_ref, g_ref, m_ref, v_ref, go_ref, d_ref, nm_ref, nv_ref):
        g = g_ref[...]
        go_ref[...] = g
        d_ref[...], nm_ref[...], nv_ref[...] = _adamw_math(w_ref[...], g, m_ref[...], v_ref[...])

    blk = pl.BlockSpec((tr, c), lambda i: (i, 0))
    return _pallas(body, name=name, grid=(r // tr,), in_specs=[blk] * 4, out_specs=[blk] * 4,
                   out_shape=[_sds((r, c), F32)] * 4, compiler_params=_params("parallel"))(w, g, m, v)


def _tables(s):
    half = HEAD_DIM // 2
    pos = jnp.arange(s, dtype=F32)
    inv_freq = ROPE_BASE ** (-jnp.arange(half, dtype=F32) / half)
    ang = pos[:, None] * inv_freq[None, :]
    cos, sin = jnp.cos(ang), jnp.sin(ang)
    cos_f = jnp.concatenate([cos, cos], axis=-1)
    sin_f = jnp.concatenate([-sin, sin], axis=-1)
    log_g = jnp.log1p(-jnp.exp2(-(5.0 + jnp.arange(HEADS, dtype=F32))))
    p = jnp.arange(CHUNK, dtype=F32)
    decay = jnp.exp(log_g[:, None, None] * jnp.abs(p[:, None] - p[None, :]))
    k_dec = jnp.exp(log_g[None, :] * (CHUNK - 1.0 - p)[:, None])
    q_dec = jnp.exp(log_g[None, :] * (p + 1.0)[:, None])
    c_dec = jnp.exp(log_g * CHUNK)
    k_dec = jnp.tile(jnp.broadcast_to(k_dec.T[:, :, None], (HEADS, CHUNK, HEAD_DIM)), (1, RET_BLOCK_CHUNKS, 1))
    q_dec = jnp.tile(jnp.broadcast_to(q_dec.T[:, :, None], (HEADS, CHUNK, HEAD_DIM)), (1, RET_BLOCK_CHUNKS, 1))
    c_dec = jnp.broadcast_to(c_dec[:, None, None], (HEADS, 1, HEAD_DIM))
    n = RET_SUB // CHUNK
    decay = (jnp.eye(n, dtype=F32)[None, :, None, :, None] * decay[:, None, :, None, :]).reshape(HEADS, RET_SUB, RET_SUB)
    return cos_f, sin_f, decay, k_dec, q_dec, c_dec


def _rot(x, cos_f, sin_f):
    return x * cos_f + pltpu.roll(x, HEAD_DIM // 2, 1) * sin_f


def _rot_bwd(d, cos_f, sin_f):
    return d * cos_f + pltpu.roll(d * sin_f, HEAD_DIM // 2, 1)


K_SCALE = HEAD_DIM ** -0.5


def _retention_fwd(proj, gn_g, tables, riders=()):
    s = proj.shape[0]
    nb = s // RET_ROWS
    nc = s // CHUNK
    cos_f, sin_f, decay, k_dec, q_dec, c_dec = tables

    def body(q_ref, k_ref, v_ref, g_ref, cos_ref, sin_ref, dec_ref, kd_ref, qd_ref, cd_ref, gn_ref,
             ret_ref, y_ref, prev_ref, state_ref):
        @pl.when(pl.program_id(1) == 0)
        def _():
            state_ref[...] = jnp.zeros_like(state_ref)

        cosv, sinv = cos_ref[...], sin_ref[...]
        q = _rot(q_ref[...], cosv, sinv)
        k = _rot(k_ref[...], cosv, sinv) * K_SCALE
        v = v_ref[...]
        rg = g_ref[...]
        dec, cd, gn = dec_ref[...], cd_ref[...], gn_ref[...]
        kdf, qdf = k * kd_ref[...], q * qd_ref[...]
        chunks = [slice(c * CHUNK, (c + 1) * CHUNK) for c in range(RET_BLOCK_CHUNKS)]
        contribs = [_dot(kdf[rows], v[rows], TN) for rows in chunks]
        state, states = state_ref[...], []
        for c in range(RET_BLOCK_CHUNKS):
            states.append(state)
            prev_ref[c] = state.astype(prev_ref.dtype)
            state = cd * state + contribs[c]
        state_ref[...] = state
        cross = jnp.concatenate([_dot(qdf[rows], st, NN) for rows, st in zip(chunks, states)], axis=0)
        intra = []
        for b in range(RET_ROWS // RET_SUB):
            rows = slice(b * RET_SUB, (b + 1) * RET_SUB)
            intra.append(_dot(_dot(q[rows], k[rows], NT) * dec, v[rows], NN))
        y = jnp.concatenate(intra, axis=0) + cross
        y_ref[...] = y
        mu = jnp.mean(y, axis=-1, keepdims=True)
        yc = y - mu
        var = jnp.mean(yc * yc, axis=-1, keepdims=True)
        yn = yc * lax.rsqrt(var + GN_EPS) * gn
        ret_ref[...] = (rg * jax.nn.sigmoid(rg) * yn).astype(ret_ref.dtype)

    def col(off):
        return pl.BlockSpec((RET_ROWS, HEAD_DIM), lambda h, i: (i, off + h))

    pos = pl.BlockSpec((RET_ROWS, HEAD_DIM), lambda h, i: (i, 0))
    per_head = lambda shape: pl.BlockSpec((None, *shape), lambda h, i: (h, 0, 0))
    rs = _Riders(riders, 11, 3)
    res = _pallas(
        _with_riders(body, 11, 3, 1, rs, (HEADS, nb)), name="retention_fwd", grid=(HEADS, nb),
        in_specs=[col(0), col(HEADS), col(2 * HEADS), col(3 * HEADS), pos, pos,
                  per_head((RET_SUB, RET_SUB)), per_head((RET_ROWS, HEAD_DIM)), per_head((RET_ROWS, HEAD_DIM)),
                  per_head((1, HEAD_DIM)), pl.BlockSpec((1, HEAD_DIM), lambda h, i: (0, h)), *rs.in_specs],
        out_specs=[col(0), col(0),
                   pl.BlockSpec((None, RET_BLOCK_CHUNKS, HEAD_DIM, HEAD_DIM), lambda h, i: (h, i, 0, 0)),
                   *rs.out_specs],
        out_shape=[_sds((s, 2 * HEADS * HEAD_DIM), BF16), _sds((s, HEADS * HEAD_DIM), F32),
                   _sds((HEADS, nc, HEAD_DIM, HEAD_DIM), MXU_DTYPE), *rs.out_shapes],
        input_output_aliases=rs.aliases,
        scratch_shapes=[pltpu.VMEM((HEAD_DIM, HEAD_DIM), F32), *rs.scratch],
        compiler_params=_params("arbitrary", "arbitrary"),
    )(proj, proj, proj, proj, cos_f, sin_f, decay, k_dec, q_dec, c_dec, gn_g, *rs.arrays)
    return res[:3], rs.split(res[3:])


def _retention_bwd(proj, gn_g, tables, y, prev, dmix, riders=()):
    s = proj.shape[0]
    nb = s // RET_ROWS
    cos_f, sin_f, decay, k_dec, q_dec, c_dec = tables

    def body(q_ref, k_ref, v_ref, g_ref, cos_ref, sin_ref, dec_ref, kd_ref, qd_ref, cd_ref, gn_ref,
             y_ref, prev_ref, dret_ref, dproj_ref, dgn_ref, gstate_ref, stage_ref, stage_sems):
        head, blk = pl.program_id(0), pl.program_id(1)
        step = head * nb + blk
        slot = step % 2

        def writes(sl):
            rows = pl.ds(pl.multiple_of((nb - 1 - blk) * RET_ROWS, RET_ROWS), RET_ROWS)
            return [pltpu.make_async_copy(
                stage_ref.at[sl, g], dproj_ref.at[rows, pl.ds(pl.multiple_of((g * HEADS + head) * HEAD_DIM, HEAD_DIM), HEAD_DIM)],
                stage_sems.at[sl, g]) for g in range(4)]

        @pl.when(step >= 2)
        def _():
            for cp in writes(slot):
                cp.wait()

        @pl.when(blk == 0)
        def _():
            gstate_ref[...] = jnp.zeros_like(gstate_ref)
            dgn_ref[...] = jnp.zeros_like(dgn_ref)

        cosv, sinv = cos_ref[...], sin_ref[...]
        q = _rot(q_ref[...], cosv, sinv)
        k = _rot(k_ref[...], cosv, sinv) * K_SCALE
        v = v_ref[...]
        dec, kd, qd, cd, gn = dec_ref[...], kd_ref[...], qd_ref[...], cd_ref[...], gn_ref[...]
        kdf, qdf = k * kd, q * qd
        rg = g_ref[...]
        yv = y_ref[...]
        dret = dret_ref[...]
        sig = jax.nn.sigmoid(rg)
        gate = rg * sig
        mu = jnp.mean(yv, axis=-1, keepdims=True)
        yc = yv - mu
        rstd = lax.rsqrt(jnp.mean(yc * yc, axis=-1, keepdims=True) + GN_EPS)
        z = yc * rstd
        dyn = dret * gate
        stage_ref[slot, 3] = (dret * (z * gn) * (sig * (1.0 + rg * (1.0 - sig)))).astype(stage_ref.dtype)
        dgn_ref[...] += jnp.sum(dyn * z, axis=0, keepdims=True)
        dz = dyn * gn
        dy = rstd * (dz - jnp.mean(dz, axis=-1, keepdims=True) - z * jnp.mean(dz * z, axis=-1, keepdims=True))
        chunks = [slice(c * CHUNK, (c + 1) * CHUNK) for c in range(RET_BLOCK_CHUNKS)]
        dprevs = [_dot(qdf[rows], dy[rows], TN) for rows in chunks]
        gst, gsts = gstate_ref[...], [None] * RET_BLOCK_CHUNKS
        for c in reversed(range(RET_BLOCK_CHUNKS)):
            gsts[c] = gst
            gst = dprevs[c] + cd * gst
        gstate_ref[...] = gst
        dq = jnp.concatenate([_dot(dy[rows], prev_ref[c], NT) for c, rows in enumerate(chunks)], axis=0) * qd
        dk = jnp.concatenate([_dot(v[rows], g, NT) for rows, g in zip(chunks, gsts)], axis=0) * kd
        dv = jnp.concatenate([_dot(kdf[rows], g, NN) for rows, g in zip(chunks, gsts)], axis=0)
        dqi, dki, dvi = [], [], []
        for b in range(RET_ROWS // RET_SUB):
            rows = slice(b * RET_SUB, (b + 1) * RET_SUB)
            qs, ks, vs, dys = q[rows], k[rows], v[rows], dy[rows]
            dvi.append(_dot(_dot(ks, qs, NT) * dec, dys, NN))
            dqi.append(_dot(_dot(dys, vs, NT) * dec, ks, NN))
            dki.append(_dot(_dot(vs, dys, NT) * dec, qs, NN))
        dq = dq + jnp.concatenate(dqi, axis=0)
        dk = dk + jnp.concatenate(dki, axis=0)
        dv = dv + jnp.concatenate(dvi, axis=0)
        stage_ref[slot, 0] = _rot_bwd(dq, cosv, sinv).astype(stage_ref.dtype)
        stage_ref[slot, 1] = _rot_bwd(dk * K_SCALE, cosv, sinv).astype(stage_ref.dtype)
        stage_ref[slot, 2] = dv.astype(stage_ref.dtype)
        for cp in writes(slot):
            cp.start()

        @pl.when(step == HEADS * nb - 1)
        def _():
            for cp in writes(1 - slot) + writes(slot):
                cp.wait()

    rev = lambda i: nb - 1 - i

    def col(off):
        return pl.BlockSpec((RET_ROWS, HEAD_DIM), lambda h, i: (rev(i), off + h))

    pos = pl.BlockSpec((RET_ROWS, HEAD_DIM), lambda h, i: (rev(i), 0))
    per_head = lambda shape: pl.BlockSpec((None, *shape), lambda h, i: (h, 0, 0))
    rs = _Riders(riders, 14, 2)
    res = _pallas(
        _with_riders(body, 14, 2, 3, rs, (HEADS, nb)), name="retention_bwd", grid=(HEADS, nb),
        in_specs=[col(0), col(HEADS), col(2 * HEADS), col(3 * HEADS), pos, pos,
                  per_head((RET_SUB, RET_SUB)), per_head((RET_ROWS, HEAD_DIM)), per_head((RET_ROWS, HEAD_DIM)),
                  per_head((1, HEAD_DIM)), pl.BlockSpec((1, HEAD_DIM), lambda h, i: (0, h)),
                  col(0), pl.BlockSpec((None, RET_BLOCK_CHUNKS, HEAD_DIM, HEAD_DIM), lambda h, i: (h, rev(i), 0, 0)),
                  col(0), *rs.in_specs],
        out_specs=[ANY, per_head((1, HEAD_DIM)), *rs.out_specs],
        out_shape=[_sds((s, proj.shape[1]), BF16), _sds((HEADS, 1, HEAD_DIM), F32), *rs.out_shapes],
        input_output_aliases=rs.aliases,
        scratch_shapes=[pltpu.VMEM((HEAD_DIM, HEAD_DIM), F32), pltpu.VMEM((2, 4, RET_ROWS, HEAD_DIM), BF16),
                        pltpu.SemaphoreType.DMA((2, 4)), *rs.scratch],
        compiler_params=_params("arbitrary", "arbitrary"),
    )(proj, proj, proj, proj, cos_f, sin_f, decay, k_dec, q_dec, c_dec, gn_g, y, prev, dmix, *rs.arrays)
    return res[:2], rs.split(res[2:])


ATT_COL0 = 4 * HEADS
PAD_ROWS = LEFT_CHUNKS * CHUNK
NORM_ROWS = 512
GROUP_CHUNKS = 4
GROUP = GROUP_CHUNKS * CHUNK
WIN = (LEFT_CHUNKS + GROUP_CHUNKS) * CHUNK
MASKED = -1e30


def _qk_norm(x, g):
    return x * lax.rsqrt(jnp.mean(x * x, axis=-1, keepdims=True) + EPS) * g


def _band_probs(qb, kb, bias, g):
    sc = _dot(qb, kb, NT) * K_SCALE + bias
    win_chunk = lax.broadcasted_iota(jnp.int32, (GROUP, WIN), 1) // CHUNK
    sc = jnp.where(g * GROUP_CHUNKS - LEFT_CHUNKS + win_chunk >= 0, sc, MASKED)
    e = jnp.exp(sc - jnp.max(sc, axis=-1, keepdims=True))
    return e / jnp.sum(e, axis=-1, keepdims=True)


def _with_riders(core, n_in, n_out, n_scratch, rs, grid):
    n_rin, n_rout = len(rs.arrays), len(rs.out_shapes)
    if not rs.riders:
        return core
    steps = 1
    for n in grid:
        steps *= n

    def body(*refs):
        outs_at = n_in + n_rin
        scratch_at = outs_at + n_out + n_rout
        bound = rs.bind(refs[n_in:outs_at], refs[outs_at + n_out:scratch_at], refs[scratch_at + n_scratch:])
        step = 0
        for axis, n in enumerate(grid):
            step = step * n + pl.program_id(axis)
        pl.when(step == 0)(lambda: rs.run("start", bound))
        pl.when(step == int(steps * RIDER_MID))(lambda: rs.run("mid", bound))
        pl.when(step == int(steps * RIDER_LATE))(lambda: rs.run("late", bound))
        core(*refs[:n_in], *refs[outs_at:outs_at + n_out], *refs[scratch_at:scratch_at + n_scratch])
        pl.when(step == steps - 1)(lambda: rs.run("end", bound))

    return body


def _attention_fwd(proj, gq, gk, bias, mix, riders=()):
    s = proj.shape[0]
    rs = _Riders(riders, 7, 1)

    def body(q_ref, k_ref, v_ref, gq_ref, gk_ref, bias_ref, mix_ref, o_ref, kp_ref, vp_ref):
        kp_ref[0:PAD_ROWS, :] = jnp.zeros((PAD_ROWS, HEAD_DIM), kp_ref.dtype)
        vp_ref[0:PAD_ROWS, :] = jnp.zeros((PAD_ROWS, HEAD_DIM), vp_ref.dtype)
        gqv, gkv = gq_ref[...], gk_ref[...]

        def fill(b, carry):
            r0 = pl.multiple_of(b * NORM_ROWS, NORM_ROWS)
            kp_ref[pl.ds(PAD_ROWS + r0, NORM_ROWS), :] = _qk_norm(k_ref[pl.ds(r0, NORM_ROWS), :], gkv).astype(kp_ref.dtype)
            vp_ref[pl.ds(PAD_ROWS + r0, NORM_ROWS), :] = v_ref[pl.ds(r0, NORM_ROWS), :].astype(vp_ref.dtype)
            return carry

        lax.fori_loop(0, s // NORM_ROWS, fill, 0)

        def group(g, carry):
            r0 = pl.multiple_of(g * GROUP, GROUP)
            qn = _qk_norm(q_ref[pl.ds(r0, GROUP), :], gqv)
            p = _band_probs(qn, kp_ref[pl.ds(r0, WIN), :], bias_ref[...], g)
            o_ref[pl.ds(r0, GROUP), :] = _dot(p, vp_ref[pl.ds(r0, WIN), :], NN).astype(o_ref.dtype)
            return carry

        lax.fori_loop(0, s // GROUP, group, 0, unroll=2)

    def col(off):
        return pl.BlockSpec((s, HEAD_DIM), lambda h: (0, off + h))

    vec = pl.BlockSpec((1, HEAD_DIM), lambda h: (0, 0))
    res = _pallas(
        _with_riders(body, 7, 1, 2, rs, (HEADS,)), name="attention_fwd", grid=(HEADS,),
        in_specs=[col(ATT_COL0), col(ATT_COL0 + HEADS), col(ATT_COL0 + 2 * HEADS), vec, vec,
                  pl.BlockSpec((None, GROUP, WIN), lambda h: (h, 0, 0)), ANY, *rs.in_specs],
        out_specs=[col(HEADS), *rs.out_specs], out_shape=[_sds(mix.shape, mix.dtype), *rs.out_shapes],
        input_output_aliases={6: 0, **rs.aliases},
        scratch_shapes=[pltpu.VMEM((s + PAD_ROWS, HEAD_DIM), MXU_DTYPE), pltpu.VMEM((s + PAD_ROWS, HEAD_DIM), MXU_DTYPE),
                        *rs.scratch],
        compiler_params=_params("arbitrary"),
    )(proj, proj, proj, gq, gk, bias, mix, *rs.arrays)
    return res[0], rs.split(res[1:])


def _attention_bwd(proj, gq, gk, bias, dmix, dproj, riders=()):
    s = proj.shape[0]
    rs = _Riders(riders, 8, 4)

    def body(q_ref, k_ref, v_ref, gq_ref, gk_ref, bias_ref, do_ref, dproj_in_ref,
             dproj_ref, dgq_ref, dgk_ref, dbias_ref, kp_ref, vp_ref, dkp_ref, dvp_ref, dqn_ref, stage_ref, stage_sems):
        head = pl.program_id(0)

        def writes():
            return [pltpu.make_async_copy(
                stage_ref.at[g],
                dproj_ref.at[:, pl.ds(pl.multiple_of((ATT_COL0 + g * HEADS + head) * HEAD_DIM, HEAD_DIM), HEAD_DIM)],
                stage_sems.at[g]) for g in range(3)]

        kp_ref[0:PAD_ROWS, :] = jnp.zeros((PAD_ROWS, HEAD_DIM), kp_ref.dtype)
        vp_ref[0:PAD_ROWS, :] = jnp.zeros((PAD_ROWS, HEAD_DIM), vp_ref.dtype)
        dkp_ref[...] = jnp.zeros_like(dkp_ref)
        dvp_ref[...] = jnp.zeros_like(dvp_ref)
        dbias_ref[...] = jnp.zeros_like(dbias_ref)
        gqv, gkv = gq_ref[...], gk_ref[...]

        def fill(b, carry):
            r0 = pl.multiple_of(b * NORM_ROWS, NORM_ROWS)
            kp_ref[pl.ds(PAD_ROWS + r0, NORM_ROWS), :] = _qk_norm(k_ref[pl.ds(r0, NORM_ROWS), :], gkv).astype(kp_ref.dtype)
            vp_ref[pl.ds(PAD_ROWS + r0, NORM_ROWS), :] = v_ref[pl.ds(r0, NORM_ROWS), :].astype(vp_ref.dtype)
            return carry

        lax.fori_loop(0, s // NORM_ROWS, fill, 0)

        def group(g, carry):
            r0 = pl.multiple_of(g * GROUP, GROUP)
            qn = _qk_norm(q_ref[pl.ds(r0, GROUP), :], gqv)
            kb = kp_ref[pl.ds(r0, WIN), :]
            vb = vp_ref[pl.ds(r0, WIN), :]
            p = _band_probs(qn, kb, bias_ref[...], g)
            do = do_ref[pl.ds(r0, GROUP), :]
            dvp_ref[pl.ds(r0, WIN), :] += _dot(p, do, TN)
            dp = _dot(do, vb, NT)
            ds = p * (dp - jnp.sum(dp * p, axis=-1, keepdims=True))
            dbias_ref[...] += ds
            dss = ds * K_SCALE
            dqn_ref[pl.ds(r0, GROUP), :] = _dot(dss, kb, NN)
            dkp_ref[pl.ds(r0, WIN), :] += _dot(dss, qn, TN)
            return carry

        lax.fori_loop(0, s // GROUP, group, 0, unroll=2)

        @pl.when(head == 0)
        def _():
            dgq_ref[...] = jnp.zeros_like(dgq_ref)
            dgk_ref[...] = jnp.zeros_like(dgk_ref)

        @pl.when(head > 0)
        def _():
            for cp in writes():
                cp.wait()

        def norm_bwd(x, g, dn):
            rstd = lax.rsqrt(jnp.mean(x * x, axis=-1, keepdims=True) + EPS)
            xh = x * rstd
            dxh = dn * g
            return rstd * (dxh - xh * jnp.mean(dxh * xh, axis=-1, keepdims=True)), jnp.sum(dn * xh, axis=0, keepdims=True)

        def finish(b, carry):
            r0 = pl.multiple_of(b * NORM_ROWS, NORM_ROWS)
            rows = pl.ds(r0, NORM_ROWS)
            dq, dgq = norm_bwd(q_ref[rows, :], gqv, dqn_ref[rows, :])
            dk, dgk = norm_bwd(k_ref[rows, :], gkv, dkp_ref[pl.ds(PAD_ROWS + r0, NORM_ROWS), :])
            stage_ref[0, rows, :] = dq.astype(stage_ref.dtype)
            stage_ref[1, rows, :] = dk.astype(stage_ref.dtype)
            stage_ref[2, rows, :] = dvp_ref[pl.ds(PAD_ROWS + r0, NORM_ROWS), :].astype(stage_ref.dtype)
            dgq_ref[...] += dgq
            dgk_ref[...] += dgk
            return carry

        lax.fori_loop(0, s // NORM_ROWS, finish, 0)
        for cp in writes():
            cp.start()

        @pl.when(head == HEADS - 1)
        def _():
            for cp in writes():
                cp.wait()

    def col(off):
        return pl.BlockSpec((s, HEAD_DIM), lambda h: (0, off + h))

    vec = pl.BlockSpec((1, HEAD_DIM), lambda h: (0, 0))
    hbias = pl.BlockSpec((None, GROUP, WIN), lambda h: (h, 0, 0))
    res = _pallas(
        _with_riders(body, 8, 4, 7, rs, (HEADS,)), name="attention_bwd", grid=(HEADS,),
        in_specs=[col(ATT_COL0), col(ATT_COL0 + HEADS), col(ATT_COL0 + 2 * HEADS), vec, vec, hbias, col(HEADS), ANY,
                  *rs.in_specs],
        out_specs=[ANY, vec, vec, hbias, *rs.out_specs],
        out_shape=[_sds(dproj.shape, dproj.dtype), _sds((1, HEAD_DIM), F32), _sds((1, HEAD_DIM), F32),
                   _sds((HEADS, GROUP, WIN), F32), *rs.out_shapes],
        input_output_aliases={7: 0, **rs.aliases},
        scratch_shapes=[pltpu.VMEM((s + PAD_ROWS, HEAD_DIM), MXU_DTYPE), pltpu.VMEM((s + PAD_ROWS, HEAD_DIM), MXU_DTYPE),
                        pltpu.VMEM((s + PAD_ROWS, HEAD_DIM), F32), pltpu.VMEM((s + PAD_ROWS, HEAD_DIM), F32),
                        pltpu.VMEM((s, HEAD_DIM), F32), pltpu.VMEM((3, s, HEAD_DIM), BF16),
                        pltpu.SemaphoreType.DMA((3,)), *rs.scratch],
        compiler_params=_params("arbitrary"),
    )(proj, proj, proj, gq, gk, bias, dmix, dproj, *rs.arrays)
    return res[:4], rs.split(res[4:])


DIAG_SPLIT = (BAND + WIN - CHUNK) // 2


def _diag_bin(m):
    t = jnp.where(m < DIAG_SPLIT, m, m - WIN)
    return jnp.clip(LEFT_CHUNKS * CHUNK - t, -(CHUNK - 1), REL_CLIP) + (CHUNK - 1)


def _skew_rows(a, left):
    row = lax.broadcasted_iota(jnp.int32, (GROUP, WIN), 0)
    for b in range(GROUP.bit_length() - 1):
        step = 1 << b
        a = jnp.where(jnp.bitwise_and(row, step) != 0, pltpu.roll(a, WIN - step if left else step, 1), a)
    return a


def _rel_bias_expand(rel_bias):
    def body(rb_ref, o_ref):
        h = pl.program_id(0)
        bins = _diag_bin(lax.broadcasted_iota(jnp.int32, (8, WIN), 1))
        per_diag = lax.fori_loop(0, REL_SIZE, lambda r, acc: jnp.where(bins == r, rb_ref[h, r], acc),
                                 jnp.zeros((8, WIN), F32))
        table = _skew_rows(jnp.broadcast_to(per_diag[0:1], (GROUP, WIN)), left=False)
        row_chunk = lax.broadcasted_iota(jnp.int32, (GROUP, WIN), 0) // CHUNK
        col_chunk = lax.broadcasted_iota(jnp.int32, (GROUP, WIN), 1) // CHUNK
        in_band = jnp.logical_and(col_chunk >= row_chunk, col_chunk <= row_chunk + LEFT_CHUNKS)
        o_ref[...] = jnp.where(in_band, table, MASKED)

    return _pallas(body, name="rel_bias_expand", grid=(HEADS,), in_specs=[pl.BlockSpec(memory_space=pltpu.SMEM)],
                   out_specs=pl.BlockSpec((None, GROUP, WIN), lambda h: (h, 0, 0)),
                   out_shape=_sds((HEADS, GROUP, WIN), F32), compiler_params=_params("parallel"))(rel_bias)


def _rel_bias_fold(dbias):
    def body(a_ref, o_ref):
        diag = jnp.sum(_skew_rows(a_ref[...], left=True), axis=0, keepdims=True)
        onehot = (_diag_bin(lax.broadcasted_iota(jnp.int32, (WIN, REL_SIZE), 0))
                  == lax.broadcasted_iota(jnp.int32, (WIN, REL_SIZE), 1)).astype(MXU_DTYPE)
        rest = jnp.broadcast_to(diag, (8, WIN))
        out = jnp.zeros((8, REL_SIZE), F32)
        for _ in range(3):
            piece = rest.astype(BF16)
            out = out + _dot(piece, onehot, NN)
            rest = rest - piece.astype(F32)
        o_ref[...] = out[0:1]

    return _pallas(body, name="rel_bias_fold", grid=(HEADS,),
                   in_specs=[pl.BlockSpec((None, GROUP, WIN), lambda h: (h, 0, 0))],
                   out_specs=pl.BlockSpec((None, 1, REL_SIZE), lambda h: (h, 0, 0)),
                   out_shape=_sds((HEADS, 1, REL_SIZE), F32), compiler_params=_params("parallel"))(dbias)


def _place():
    return lax.axis_index("x"), lax.axis_index("y"), lax.axis_index("c")


def _other_chips(x, y):
    return [(1 - x, y), (x, 1 - y), (1 - x, 1 - y)]


class _Rider:
    reads, ins, new, n_sems = (), (), (), 1

    def start(self, reads, ins, new, send, recv):
        pass

    def mid(self, reads, ins, new, send, recv):
        pass

    def late(self, reads, ins, new, send, recv):
        pass

    def end(self, reads, ins, new, send, recv):
        pass


class _Riders:
    def __init__(self, riders, n_host_in, n_host_out):
        self.riders = list(riders)
        self.arrays, self.out_shapes, self.aliases, self.scratch = [], [], {}, []
        for r in self.riders:
            for t, a in enumerate(r.ins):
                self.aliases[n_host_in + len(self.arrays) + len(r.reads) + t] = n_host_out + len(self.out_shapes) + t
            self.arrays += [*r.reads, *r.ins]
            self.out_shapes += [_sds(a.shape, a.dtype) for a in r.ins] + list(r.new)
            self.scratch += [pltpu.SemaphoreType.DMA((r.n_sems,)), pltpu.SemaphoreType.DMA((r.n_sems,))]
        self.in_specs = [ANY] * len(self.arrays)
        self.out_specs = [ANY] * len(self.out_shapes)

    def bind(self, in_refs, out_refs, scratch_refs):
        bound, i, o = [], 0, 0
        for t, r in enumerate(self.riders):
            reads = in_refs[i:i + len(r.reads)]
            i += len(r.reads) + len(r.ins)
            ins = out_refs[o:o + len(r.ins)]
            new = out_refs[o + len(r.ins):o + len(r.ins) + len(r.new)]
            o += len(r.ins) + len(r.new)
            bound.append((reads, ins, new, scratch_refs[2 * t], scratch_refs[2 * t + 1]))
        return bound

    def run(self, phase, bound):
        for r, b in zip(self.riders, bound):
            getattr(r, phase)(*b)

    def split(self, outs):
        res, o = [], 0
        for r in self.riders:
            n = len(r.ins) + len(r.new)
            res.append(list(outs[o:o + n]))
            o += n
        return res


def _run_riders(name, riders):
    rs = _Riders(riders, 0, 0)
    n_in, n_out = len(rs.arrays), len(rs.out_shapes)

    def body(*refs):
        bound = rs.bind(refs[:n_in], refs[n_in:n_in + n_out], refs[n_in + n_out:])
        rs.run("start", bound)
        rs.run("mid", bound)
        rs.run("late", bound)
        rs.run("end", bound)

    outs = _pallas(body, name=name, in_specs=rs.in_specs, out_specs=rs.out_specs, out_shape=rs.out_shapes,
                   input_output_aliases=rs.aliases, scratch_shapes=rs.scratch)(*rs.arrays)
    return rs.split(outs)


class _GatherRider(_Rider):
    X_LINK, Y_LINK, Y_PASS, X_PASS, D2D_X, D2D_Y, D2D_DIAG, N_SEMS = 0, 1, 2, 3, 4, 5, 6, 7

    def __init__(self, blocks, part=(0, 1, 1)):
        self.ins = tuple(blocks)
        self.part = part
        self.n_sems = self.N_SEMS * len(blocks)

    def _copy(self, out, send, recv, w, sem, chip_from, cc, to, sub=None):
        hr = self.ins[w].shape[1] // 2
        lo, hi, n = self.part
        first, size = cc * hr + lo * (hr // n), (hi - lo) * (hr // n)
        if sub is not None:
            size //= 2
            first += sub * size
        piece = out[w].at[2 * chip_from[0] + chip_from[1], pl.ds(first, size), :]
        return pltpu.make_async_remote_copy(src_ref=piece, dst_ref=piece, send_sem=send.at[self.N_SEMS * w + sem],
                                            recv_sem=recv.at[self.N_SEMS * w + sem], device_id=to, device_id_type=MESH)

    def _sent(self, out, send, recv, w):
        x, y, c = _place()
        me, sib = (x, y), (x, y, 1 - c)
        xn, yn, diag = _other_chips(x, y)
        cp = functools.partial(self._copy, out, send, recv, w)
        return [("start", cp(self.X_LINK, me, c, (*xn, c))), ("start", cp(self.Y_LINK, me, c, (*yn, c))),
                ("mid_x", cp(self.D2D_X, xn, c, sib)), ("mid_x", cp(self.Y_PASS, xn, c, (*yn, c), sub=0)),
                ("mid_y", cp(self.D2D_Y, yn, c, sib)), ("mid_y", cp(self.X_PASS, yn, c, (*xn, c), sub=1)),
                ("late", cp(self.D2D_DIAG, diag, c, sib))]

    def _go(self, out, send, recv, phase):
        for w in range(len(self.ins)):
            for ph, copy in self._sent(out, send, recv, w):
                if ph == phase:
                    copy.start()

    def start(self, reads, out, new, send, recv):
        self._go(out, send, recv, "start")

    def mid(self, reads, out, new, send, recv):
        x, y, c = _place()
        xn, yn, _ = _other_chips(x, y)
        for w in range(len(self.ins)):
            self._copy(out, send, recv, w, self.X_LINK, xn, c, (x, y, c)).wait_recv()
        self._go(out, send, recv, "mid_x")
        for w in range(len(self.ins)):
            self._copy(out, send, recv, w, self.Y_LINK, yn, c, (x, y, c)).wait_recv()
        self._go(out, send, recv, "mid_y")

    def late(self, reads, out, new, send, recv):
        x, y, c = _place()
        diag = _other_chips(x, y)[2]
        for w in range(len(self.ins)):
            self._copy(out, send, recv, w, self.Y_PASS, diag, c, (x, y, c), sub=0).wait_recv()
            self._copy(out, send, recv, w, self.X_PASS, diag, c, (x, y, c), sub=1).wait_recv()
        self._go(out, send, recv, "late")

    def end(self, reads, out, new, send, recv):
        x, y, c = _place()
        xn, yn, diag = _other_chips(x, y)
        for w in range(len(self.ins)):
            for sem, chip in ((self.D2D_X, xn), (self.D2D_Y, yn), (self.D2D_DIAG, diag)):
                self._copy(out, send, recv, w, sem, chip, 1 - c, (x, y, c)).wait_recv()
        for w in range(len(self.ins)):
            for _, copy in self._sent(out, send, recv, w):
                copy.wait_send()


class _SwapRider(_Rider):
    def __init__(self, grads):
        self.reads = tuple(grads)
        self.new = tuple(_sds((N_CHIPS, g.shape[1] // 2, g.shape[2]), g.dtype) for g in grads)
        self.n_sems = len(grads)

    def _copies(self, src, new, send, recv):
        x, y, c = _place()
        copies = []
        for w in range(len(self.reads)):
            hr = self.reads[w].shape[1] // 2
            copies.append(pltpu.make_async_remote_copy(
                src_ref=src[w].at[:, pl.ds((1 - c) * hr, hr), :], dst_ref=new[w],
                send_sem=send.at[w], recv_sem=recv.at[w], device_id=(x, y, 1 - c), device_id_type=MESH))
        return copies

    def start(self, src, ins, new, send, recv):
        for cp in self._copies(src, new, send, recv):
            cp.start()

    def end(self, src, ins, new, send, recv):
        for cp in self._copies(src, new, send, recv):
            cp.wait()


def _add_half(g, got, c_arr, name):
    nk, r, cols = g.shape
    hr = r // 2
    tr = min(hr, 256)
    nb = hr // tr

    def body(c_ref, g_ref, got_ref, o_ref):
        o_ref[...] = (g_ref[...].astype(F32) + got_ref[...].astype(F32)).astype(o_ref.dtype)

    grid_spec = pltpu.PrefetchScalarGridSpec(
        num_scalar_prefetch=1, grid=(nk, nb),
        in_specs=[pl.BlockSpec((None, tr, cols), lambda k, i, c_ref: (k, c_ref[0] * nb + i, 0)),
                  pl.BlockSpec((None, tr, cols), lambda k, i, c_ref: (k, i, 0))],
        out_specs=pl.BlockSpec((None, tr, cols), lambda k, i, c_ref: (k, i, 0)))
    return _pallas(body, name=name, grid_spec=grid_spec, out_shape=_sds((nk, hr, cols), g.dtype),
                   compiler_params=_params("parallel", "parallel"))(c_arr, g, got)


class _SendPartialsRider(_Rider):
    def __init__(self, parts, got=None, part=(0, 1, 1)):
        self.reads = tuple(parts)
        if got is None:
            self.new = tuple(_sds((N_CHIPS - 1, *p.shape[1:]), p.dtype) for p in parts)
        else:
            self.ins = tuple(got)
        self.part = part
        self.n_sems = 3 * len(parts)

    def _copies(self, src, ins, new, send, recv):
        x, y, c = _place()
        land = ins if self.ins else new
        lo, hi, n = self.part
        copies = []
        for w in range(len(self.reads)):
            pr = self.reads[w].shape[1] // n
            rows = pl.ds(lo * pr, (hi - lo) * pr)
            for j, chip in enumerate(_other_chips(x, y)):
                copies.append(pltpu.make_async_remote_copy(
                    src_ref=src[w].at[2 * chip[0] + chip[1], rows, :], dst_ref=land[w].at[j, rows, :],
                    send_sem=send.at[3 * w + j], recv_sem=recv.at[3 * w + j], device_id=(*chip, c), device_id_type=MESH))
        return copies

    def start(self, src, ins, new, send, recv):
        for cp in self._copies(src, ins, new, send, recv):
            cp.start()

    def end(self, src, ins, new, send, recv):
        for cp in self._copies(src, ins, new, send, recv):
            cp.wait()


def _sum_partials(part, got, kc_arr, name):
    _, hr, cols = part.shape
    tr = min(hr, 256)
    nb = hr // tr

    def body(kc_ref, p_ref, g0_ref, g1_ref, g2_ref, o_ref):
        o_ref[...] = ((p_ref[...].astype(F32) + g0_ref[...].astype(F32)) + g1_ref[...].astype(F32)) + g2_ref[...].astype(F32)

    slot = lambda j: pl.BlockSpec((None, tr, cols), lambda i, kc_ref: (j, i, 0))
    grid_spec = pltpu.PrefetchScalarGridSpec(
        num_scalar_prefetch=1, grid=(nb,),
        in_specs=[pl.BlockSpec((None, tr, cols), lambda i, kc_ref: (kc_ref[0], i, 0)), slot(0), slot(1), slot(2)],
        out_specs=pl.BlockSpec((tr, cols), lambda i, kc_ref: (kc_ref[1] * nb + i, 0)))
    return _pallas(body, name=name, grid_spec=grid_spec, out_shape=_sds((2 * hr, cols), F32),
                   compiler_params=_params("parallel"))(kc_arr, part, got, got, got)


class _ShareRider(_Rider):
    def __init__(self, grads):
        self.ins = tuple(grads)
        self.n_sems = len(grads)

    def _copies(self, out, send, recv):
        x, y, c = _place()
        copies = []
        for w in range(len(self.ins)):
            hr = self.ins[w].shape[0] // 2
            mine = out[w].at[pl.ds(c * hr, hr), :]
            copies.append(pltpu.make_async_remote_copy(
                src_ref=mine, dst_ref=mine, send_sem=send.at[w], recv_sem=recv.at[w],
                device_id=(x, y, 1 - c), device_id_type=MESH))
        return copies

    def start(self, reads, out, new, send, recv):
        for cp in self._copies(out, send, recv):
            cp.start()

    def end(self, reads, out, new, send, recv):
        for cp in self._copies(out, send, recv):
            cp.wait()


def _small_allreduce_adamw(g_part, w, m, v):
    rows = g_part.shape[0]

    def body(g_ref, w_ref, m_ref, v_ref, go_ref, d_ref, nm_ref, nv_ref, all_ref, send_sems, recv_sems):
        x, y, c = _place()
        me = 4 * x + 2 * y + c
        all_ref[me] = g_ref[...]
        copies = []
        for r in range(1, 8):
            dx, dy, dc = (r >> 2) & 1, (r >> 1) & 1, r & 1
            peer = (1 - x if dx else x, 1 - y if dy else y, 1 - c if dc else c)
            copies.append(pltpu.make_async_remote_copy(
                src_ref=g_ref, dst_ref=all_ref.at[me], send_sem=send_sems.at[r - 1], recv_sem=recv_sems.at[r - 1],
                device_id=peer, device_id_type=MESH))
        for cp in copies:
            cp.start()
        for cp in copies:
            cp.wait()
        tot = all_ref[0]
        for d in range(1, 8):
            tot = tot + all_ref[d]
        go_ref[...] = tot
        d_ref[...], nm_ref[...], nv_ref[...] = _adamw_math(w_ref[...], tot, m_ref[...], v_ref[...])

    vm = pl.BlockSpec(memory_space=pltpu.VMEM)
    return _pallas(
        body, name="small_allreduce_adamw", in_specs=[vm] * 4, out_specs=[vm] * 4,
        out_shape=[_sds((rows, 128), F32)] * 4,
        scratch_shapes=[pltpu.VMEM((8, rows, 128), F32), pltpu.SemaphoreType.DMA((7,)), pltpu.SemaphoreType.DMA((7,))],
    )(g_part, w, m, v)


SMALL_SIZES = (2048, 1024, 128, 128, HEADS * REL_SIZE, 2048)
SMALL_PART_ROWS = tuple(-(-size // 1024) * 8 for size in SMALL_SIZES)
SMALL_ROWS = sum(SMALL_PART_ROWS)


def _pack_small(parts):
    rows = []
    for p, size, nr in zip(parts, SMALL_SIZES, SMALL_PART_ROWS):
        rows.append(jnp.pad(p.reshape(-1), (0, nr * 128 - size)).reshape(nr, 128))
    return jnp.concatenate(rows, axis=0)


def _unpack_small(slab, shapes):
    out, off = [], 0
    for size, nr, shape in zip(SMALL_SIZES, SMALL_PART_ROWS, shapes):
        out.append(slab[off:off + nr].reshape(-1)[:size].reshape(shape))
        off += nr
    return out


def kernel(x, norm1_g, w_in, ret_norm_g, q_norm_g, k_norm_g, rel_bias, w_out, norm2_g, w_ff1, w_ff2, loss_target, m_norm1_g, m_w_in, m_ret_norm_g, m_q_norm_g, m_k_norm_g, m_rel_bias, m_w_out, m_norm2_g, m_w_ff1, m_w_ff2, v_norm1_g, v_w_in, v_ret_norm_g, v_q_norm_g, v_k_norm_g, v_rel_bias, v_w_out, v_norm2_g, v_w_ff1, v_w_ff2):
    xs = x[0]
    tgt = loss_target[0]
    s, d = xs.shape
    d_in = N_CHIPS * w_in.shape[2]
    d_ff = N_CHIPS * w_ff1.shape[2]
    in_sh, ff_sh = w_in.shape[2], w_ff1.shape[2]
    tm = min(s, 1024)
    gi = s // tm
    c_arr = lax.axis_index("c").astype(jnp.int32).reshape(1)
    k_arr = (2 * lax.axis_index("x") + lax.axis_index("y")).astype(jnp.int32).reshape(1)
    tables = _tables(s)
    bias = _rel_bias_expand(rel_bias[0])

    blk_in = [_cast_bf16(w_in[0], k_arr, "cast_w_in_%d" % half, cols=(half, 2)) for half in range(2)]
    blk_out, blk_ff1, blk_ff2 = (_cast_bf16(w_out[0], k_arr, "cast_w_out"), _cast_bf16(w_ff1[0], k_arr, "cast_w_ff1"),
                                 _cast_bf16(w_ff2[0], k_arr, "cast_w_ff2"))
    ((wg_in0,),) = _run_riders("all_gather_w_in_0", [_GatherRider([blk_in[0]])])

    h1 = _rmsnorm_fwd(xs, norm1_g, "rmsnorm1")
    tn_in = in_sh // 2
    tk = d

    def proj_half(half, wg, through, riders):
        return _mm("proj_%d" % half, h1, wg, NN, (gi, N_CHIPS, 1),
                   pl.BlockSpec((tm, tk), lambda i, j, k: (i, 0)), pl.BlockSpec((None, tk, tn_in), lambda i, j, k: (j, 0, 0)),
                   [_sds((s, d_in), F32)], [pl.BlockSpec((tm, tn_in), lambda i, j, k: (i, 2 * j + half))], (tm, tn_in),
                   riders=riders, through=through)

    (proj,), ((wg_in1,),) = proj_half(0, wg_in0, None, [_GatherRider([blk_in[1]])])
    (proj,), ((wg_ff1,),) = proj_half(1, wg_in1, proj, [_GatherRider([blk_ff1], (0, 3, 8))])
    (mix, y_ret, prev), ((wg_ff1,),) = _retention_fwd(proj, ret_norm_g, tables, riders=[_GatherRider([wg_ff1], (3, 8, 8))])
    mix, ((wg_out,), (wg_ff2,)) = _attention_fwd(
        proj, q_norm_g, k_norm_g, bias, mix, riders=[_GatherRider([blk_out]), _GatherRider([blk_ff2], (0, 1, 4))])
    wg_out = wg_out.reshape(d, d)
    tn = 1024
    tile = pl.BlockSpec((tm, tn), lambda i, j, k: (i, j))
    def residual_norm(acc, res, g):
        x1v = res + acc
        yv = x1v * lax.rsqrt(jnp.mean(x1v * x1v, axis=-1, keepdims=True) + EPS)
        return x1v, yv * g

    tmo = min(s, 512)
    rows = pl.BlockSpec((tmo, d), lambda i, j, k: (i, 0))
    x1, h2 = _mm("out_proj", mix, wg_out, NN, (s // tmo, 1, 1),
                 rows, pl.BlockSpec((d, d), lambda i, j, k: (0, 0)),
                 [_sds((s, d), F32), _sds((s, d), BF16)], [rows, rows], (tmo, d),
                 extras=(xs, norm2_g), extra_specs=(rows, pl.BlockSpec((1, d), lambda i, j, k: (0, 0))),
                 epi=residual_norm)
    tn_ff = min(ff_sh, 1024)
    per = ff_sh // tn_ff

    def relu2(acc):
        r = jnp.maximum(acc, 0.0)
        return acc, r * r

    (u, act), ((wg_ff2,),) = _mm(
        "ff1", h2, wg_ff1, NN, (gi, N_CHIPS * per, d // tk),
        pl.BlockSpec((tm, tk), lambda i, j, k: (i, k)),
        pl.BlockSpec((None, tk, tn_ff), lambda i, j, k: (j // per, k, j % per)),
        [_sds((s, d_ff), F32), _sds((s, d_ff), BF16)],
        [pl.BlockSpec((tm, tn_ff), lambda i, j, k: (i, j))] * 2, (tm, tn_ff), epi=relu2,
        riders=[_GatherRider([wg_ff2], (1, 4, 4))])
    wg_ff2 = wg_ff2.reshape(d_ff, d)

    def loss_epi(acc, res, t):
        diff = (res + acc) - t
        dy = diff / d
        return dy, dy, jnp.sum(diff * diff, axis=0, keepdims=True)

    tk2 = min(tk, 2048)
    dy, dyb, loss_cols = _mm(
        "ff2_loss", act, wg_ff2, NN, (gi, d // tn, d_ff // tk2),
        pl.BlockSpec((tm, tk2), lambda i, j, k: (i, k)), pl.BlockSpec((tk2, tn), lambda i, j, k: (k, j)),
        [_sds((s, d), F32), _sds((s, d), BF16), _sds((gi, 1, d), F32)],
        [tile, tile, pl.BlockSpec((None, 1, tn), lambda i, j, k: (i, 0, j))], (tm, tn),
        extras=(x1, tgt), extra_specs=(tile, tile), epi=loss_epi)
    loss = lax.psum(0.5 * jnp.sum(loss_cols) / d, ("x", "y", "c"))

    (du,) = _mm("d_act", dyb, wg_ff2, NT, (gi, d_ff // tn, d // tk),
                pl.BlockSpec((tm, tk), lambda i, j, k: (i, k)), pl.BlockSpec((tn, tk), lambda i, j, k: (j, k)),
                [_sds((s, d_ff), BF16)], [tile], (tm, tn), extras=(u,), extra_specs=(tile,),
                epi=lambda acc, uu: (acc * (2.0 * jnp.maximum(uu, 0.0)),))
    ts = min(s, 2048)
    wtile = pl.BlockSpec((tn, tn), lambda i, j, k: (i, j))
    (g_ff2,) = _mm("dw_ff2", act, dyb, TN, (d_ff // tn, d // tn, s // ts),
                   pl.BlockSpec((ts, tn), lambda i, j, k: (k, i)), pl.BlockSpec((ts, tn), lambda i, j, k: (k, j)),
                   [_sds((d_ff, d), BF16)], [wtile], (tn, tn))
    g_ff2 = g_ff2.reshape(N_CHIPS, d_ff // N_CHIPS, d)
    (g_ff1,), ((got_ff2,),) = _mm(
        "dw_ff1", h2, du, TN, (d // tn, N_CHIPS * per, s // ts),
        pl.BlockSpec((ts, tn), lambda i, j, k: (k, i)), pl.BlockSpec((ts, tn_ff), lambda i, j, k: (k, j)),
        [_sds((N_CHIPS, d, ff_sh), BF16)],
        [pl.BlockSpec((None, tn, tn_ff), lambda i, j, k: (j // per, i, j % per))], (tn, tn_ff),
        riders=[_SwapRider([g_ff2])])
    p_ff2 = _add_half(g_ff2, got_ff2, c_arr, "chip_partial_w_ff2")
    tkf = min(tk, ff_sh)
    kper = ff_sh // tkf
    (dh2,), ((got2_ff2,), (got_ff1,)) = _mm(
        "d_h2", du, wg_ff1, NT, (gi, d // tn, d_ff // tkf),
        pl.BlockSpec((tm, tkf), lambda i, j, k: (i, k)),
        pl.BlockSpec((None, tn, tkf), lambda i, j, k: (k // kper, j, k % kper)),
        [_sds((s, d), F32)], [tile], (tm, tn),
        riders=[_SendPartialsRider([p_ff2], part=(0, 3, 4)), _SwapRider([g_ff1])])
    p_ff1 = _add_half(g_ff1, got_ff1, c_arr, "chip_partial_w_ff1")
    dx1, dx1b, g_norm2 = _rmsnorm_bwd(x1, norm2_g, dh2, dy, "rmsnorm2_bwd")

    (dmix,) = _mm("d_mix", dx1b, wg_out, NT, (gi, d // tn, d // tk),
                  pl.BlockSpec((tm, tk), lambda i, j, k: (i, k)), pl.BlockSpec((tn, tk), lambda i, j, k: (j, k)),
                  [_sds((s, d), F32)], [tile], (tm, tn))
    (g_out,) = _mm("dw_out", mix, dx1b, TN, (d // tn, d // tn, s // ts),
                   pl.BlockSpec((ts, tn), lambda i, j, k: (k, i)), pl.BlockSpec((ts, tn), lambda i, j, k: (k, j)),
                   [_sds((d, d), BF16)], [wtile], (tn, tn))
    g_out = g_out.reshape(N_CHIPS, d // N_CHIPS, d)
    (dproj, g_gn), ((got2_ff2,), (got2_ff1,), (got_out,)) = _retention_bwd(
        proj, ret_norm_g, tables, y_ret, prev, dmix,
        riders=[_SendPartialsRider([p_ff2], got=[got2_ff2], part=(3, 4, 4)), _SendPartialsRider([p_ff1], part=(0, 2, 4)),
                _SwapRider([g_out])])
    p_out = _add_half(g_out, got_out, c_arr, "chip_partial_w_out")
    (dproj, g_gq, g_gk, dbias), ((got2_ff1,), (got2_out,)) = _attention_bwd(
        proj, q_norm_g, k_norm_g, bias, dmix, dproj,
        riders=[_SendPartialsRider([p_ff1], got=[got2_ff1], part=(2, 4, 4)), _SendPartialsRider([p_out])])
    g_rel = _rel_bias_fold(dbias)
    names = ["w_in", "w_out", "w_ff1", "w_ff2"]
    kc_arr = jnp.concatenate([k_arr, c_arr])
    early = [_sum_partials(p, r, kc_arr, "sum_partials_" + nm)
             for p, r, nm in zip((p_out, p_ff1, p_ff2), (got2_out, got2_ff1, got2_ff2), names[1:])]
    (g_in,), (early,) = _mm(
        "dw_in", h1, dproj, TN, (d // tn, 2 * N_CHIPS, s // ts),
        pl.BlockSpec((ts, tn), lambda i, j, k: (k, i)), pl.BlockSpec((ts, tn_in), lambda i, j, k: (k, j)),
        [_sds((N_CHIPS, d, in_sh), BF16)],
        [pl.BlockSpec((None, tn, tn_in), lambda i, j, k: (j // 2, i, j % 2))], (tn, tn_in), riders=[_ShareRider(early)])
    ((got_in,),) = _run_riders("grad_swap_w_in", [_SwapRider([g_in])])
    p_in = _add_half(g_in, got_in, c_arr, "chip_partial_w_in")
    def d_h1_half(half, wg, so_far, riders):
        return _mm("d_h1_%d" % half, dproj, wg, NT, (gi, d // tn, N_CHIPS),
                   pl.BlockSpec((tm, tn_in), lambda i, j, k: (i, 2 * k + half)),
                   pl.BlockSpec((None, tn, tn_in), lambda i, j, k: (k, j, 0)),
                   [_sds((s, d), F32)], [tile], (tm, tn), extras=so_far, extra_specs=(tile,) * len(so_far),
                   epi=(lambda acc, prev: (prev + acc,)) if so_far else None, riders=riders)

    (dh1,), ((got2_in,),) = d_h1_half(0, wg_in0, (), [_SendPartialsRider([p_in], part=(0, 1, 2))])
    (dh1,), ((got2_in,),) = d_h1_half(1, wg_in1, (dh1,), [_SendPartialsRider([p_in], got=[got2_in], part=(1, 2, 2))])
    grad_x, _, g_norm1 = _rmsnorm_bwd(xs, norm1_g, dh1, dx1, "rmsnorm1_bwd")
    ((g_w_in,),) = _run_riders("grad_share_w_in", [_ShareRider([_sum_partials(p_in, got2_in, kc_arr, "sum_partials_w_in")])])
    g_big = [g_w_in, *early]
    big = []
    for g, w, m, v, nm in zip(g_big, (w_in, w_out, w_ff1, w_ff2), (m_w_in, m_w_out, m_w_ff1, m_w_ff2),
                              (v_w_in, v_w_out, v_w_ff1, v_w_ff2), names):
        g, delta, new_m, new_v = _adamw(w[0], g, m[0], v[0], "adamw_" + nm)
        big.append((g[None], delta[None], new_m[None], new_v[None]))

    small_w = (norm1_g, ret_norm_g, q_norm_g, k_norm_g, rel_bias, norm2_g)
    small_m = (m_norm1_g, m_ret_norm_g, m_q_norm_g, m_k_norm_g, m_rel_bias, m_norm2_g)
    small_v = (v_norm1_g, v_ret_norm_g, v_q_norm_g, v_k_norm_g, v_rel_bias, v_norm2_g)
    shapes = [p.shape for p in small_w]
    g_small = _pack_small([g_norm1, g_gn, g_gq, g_gk, g_rel, g_norm2])
    sg, sd, sm, sv = (_unpack_small(a, shapes) for a in _small_allreduce_adamw(
        g_small, _pack_small(small_w), _pack_small(small_m), _pack_small(small_v)))

    def ordered(kind):
        sm_ = (sg, sd, sm, sv)[kind]
        return (sm_[0], big[0][kind], sm_[1], sm_[2], sm_[3], sm_[4], big[1][kind], sm_[5], big[2][kind], big[3][kind])

    return (loss, grad_x[None], *ordered(0), *ordered(1), *ordered(2), *ordered(3))
```

```python
import functools

import jax
import jax.numpy as jnp
from jax import lax
from jax.experimental import pallas as pl
from jax.experimental.pallas import tpu as pltpu

F32 = jnp.float32
BF16 = jnp.bfloat16
MXU_DTYPE = jnp.bfloat16

CHUNK = 64
HEADS = 8
HEAD_DIM = 128
LEFT_CHUNKS = 8
BAND = (LEFT_CHUNKS + 1) * CHUNK
REL_CLIP = 128
REL_SIZE = (CHUNK - 1) + REL_CLIP + 1
RET_BLOCK_CHUNKS = 8
RET_ROWS = RET_BLOCK_CHUNKS * CHUNK
RET_SUB = 256
ROPE_BASE = 10000.0
EPS = 1e-6
GN_EPS = 1e-5
ADAM_LR, ADAM_B1, ADAM_B2, ADAM_EPS, ADAM_WD, ADAM_STEP = 0.001, 0.9, 0.999, 1e-08, 0.01, 10
N_CHIPS = 4
VMEM_LIMIT = 56 * 1024 * 1024
MESH = pl.DeviceIdType.MESH
ANY = pl.BlockSpec(memory_space=pl.ANY)

NN = (((1,), (0,)), ((), ()))
NT = (((1,), (1,)), ((), ()))
TN = (((0,), (0,)), ((), ()))


def _pallas(body, **kw):
    return pl.pallas_call(body, **kw)


def _params(*sem):
    return pltpu.CompilerParams(dimension_semantics=sem, vmem_limit_bytes=VMEM_LIMIT)


def _dot(a, b, dims):
    return lax.dot_general(a.astype(MXU_DTYPE), b.astype(MXU_DTYPE), dims, preferred_element_type=F32)


RIDER_MID, RIDER_LATE = 0.5, 0.8


def _mm(name, a, b, dims, grid, a_spec, b_spec, outs, o_specs, acc_shape, extras=(), extra_specs=(), epi=None,
        riders=(), through=None):
    ni, nj, nk = grid
    n_ex, n_out = len(extras), len(outs)
    bs = list(b) if isinstance(b, (list, tuple)) else [b]
    b_specs = list(b_spec) if isinstance(b, (list, tuple)) else [b_spec]
    extras, extra_specs = (*bs[1:], *extras), (*b_specs[1:], *extra_specs)
    b, b_spec, n_b = bs[0], b_specs[0], len(bs)
    n_in = 1 + n_b + n_ex
    rs = _Riders(riders, n_in, n_out)
    n_rin, n_rout = len(rs.arrays), len(rs.out_shapes)
    steps = ni * nj * nk

    held = [] if through is None else [through]

    def body(*refs):
        a_ref, b_refs = refs[0], refs[1:1 + n_b]
        b_ref = b_refs[0]
        ex_refs = refs[1 + n_b:n_in]
        outs_at = n_in + n_rin + len(held)
        o_refs = refs[outs_at:outs_at + n_out]
        acc_ref = refs[outs_at + n_out + n_rout]
        k = pl.program_id(2)
        if riders:
            bound = rs.bind(refs[n_in:n_in + n_rin], refs[outs_at + n_out:outs_at + n_out + n_rout],
                            refs[outs_at + n_out + n_rout + 1:])
            step = (pl.program_id(0) * nj + pl.program_id(1)) * nk + k
            pl.when(step == 0)(lambda: rs.run("start", bound))
            pl.when(step == int(steps * RIDER_MID))(lambda: rs.run("mid", bound))
            pl.when(step == int(steps * RIDER_LATE))(lambda: rs.run("late", bound))

        def finish(acc):
            vals = epi(acc, *[r[...] for r in ex_refs]) if epi is not None else (acc,)
            for r, v in zip(o_refs, vals):
                r[...] = v.astype(r.dtype)

        if nk == 1:
            finish(_dot(a_ref[...], b_ref[...], dims))
        else:
            @pl.when(k == 0)
            def _():
                acc_ref[...] = jnp.zeros_like(acc_ref)

            for t, ref in enumerate(b_refs):
                def step_with(ref=ref):
                    acc_ref[...] += _dot(a_ref[...], ref[...], dims)

                if n_b == 1:
                    step_with()
                else:
                    pl.when(k % n_b == t)(step_with)
            pl.when(k == nk - 1)(lambda: finish(acc_ref[...]))

        if riders:
            pl.when(step == steps - 1)(lambda: rs.run("end", bound))

    res = _pallas(
        body, name=name, grid=grid, in_specs=[a_spec, b_spec, *extra_specs, *rs.in_specs, *[ANY for _ in held]],
        out_specs=[*o_specs, *rs.out_specs], out_shape=[*outs, *rs.out_shapes],
        input_output_aliases={**rs.aliases, **{n_in + n_rin: 0 for _ in held}},
        scratch_shapes=[pltpu.VMEM(acc_shape if nk > 1 else (8, 128), F32), *rs.scratch],
        compiler_params=_params(*(("arbitrary",) * 3 if riders else ("parallel", "parallel", "arbitrary"))),
    )(a, b, *extras, *rs.arrays, *held)
    return (res[:n_out], rs.split(res[n_out:])) if riders else res


def _sds(shape, dtype):
    return jax.ShapeDtypeStruct(shape, dtype)


def _cast_bf16(w, k_arr, name, cols=(0, 1)):
    r, c = w.shape[0], w.shape[1] // cols[1]
    tr = min(r, 256)

    def body(k_ref, w_ref, o_ref):
        o_ref[...] = w_ref[...].astype(BF16)

    grid_spec = pltpu.PrefetchScalarGridSpec(
        num_scalar_prefetch=1, grid=(r // tr,), in_specs=[pl.BlockSpec((tr, c), lambda i, k_ref: (i, cols[0]))],
        out_specs=pl.BlockSpec((None, tr, c), lambda i, k_ref: (k_ref[0], i, 0)))
    return _pallas(body, name=name, grid_spec=grid_spec, out_shape=_sds((N_CHIPS, r, c), BF16),
                   compiler_params=_params("parallel"))(k_arr, w)


def _rmsnorm_fwd(x, g, name):
    s, d = x.shape
    tr = 256

    def body(x_ref, g_ref, o_ref):
        xv = x_ref[...]
        y = xv * lax.rsqrt(jnp.mean(xv * xv, axis=-1, keepdims=True) + EPS)
        o_ref[...] = (y * g_ref[...]).astype(o_ref.dtype)

    return _pallas(body, name=name, grid=(s // tr,),
                   in_specs=[pl.BlockSpec((tr, d), lambda i: (i, 0)), pl.BlockSpec((1, d), lambda i: (0, 0))],
                   out_specs=pl.BlockSpec((tr, d), lambda i: (i, 0)), out_shape=_sds((s, d), BF16),
                   compiler_params=_params("parallel"))(x, g)


def _rmsnorm_bwd(x, g, dh, res, name, riders=()):
    s, d = x.shape
    tr = 256

    def body(x_ref, g_ref, dh_ref, res_ref, dx_ref, dxb_ref, dg_ref):
        i = pl.program_id(0)
        xv = x_ref[...]
        rstd = lax.rsqrt(jnp.mean(xv * xv, axis=-1, keepdims=True) + EPS)
        xh = xv * rstd
        dhv = dh_ref[...]

        @pl.when(i == 0)
        def _():
            dg_ref[...] = jnp.zeros_like(dg_ref)

        dg_ref[...] += jnp.sum(dhv * xh, axis=0, keepdims=True)
        dxh = dhv * g_ref[...]
        dx = res_ref[...] + rstd * (dxh - xh * jnp.mean(dxh * xh, axis=-1, keepdims=True))
        dx_ref[...] = dx
        dxb_ref[...] = dx.astype(BF16)

    row = pl.BlockSpec((tr, d), lambda i: (i, 0))
    vec = pl.BlockSpec((1, d), lambda i: (0, 0))
    rs = _Riders(riders, 4, 3)
    out = _pallas(_with_riders(body, 4, 3, 0, rs, (s // tr,)), name=name, grid=(s // tr,),
                  in_specs=[row, vec, row, row, *rs.in_specs], out_specs=[row, row, vec, *rs.out_specs],
                  out_shape=[_sds((s, d), F32), _sds((s, d), BF16), _sds((1, d), F32), *rs.out_shapes],
                  input_output_aliases=rs.aliases, scratch_shapes=rs.scratch,
                  compiler_params=_params("arbitrary"))(x, g, dh, res, *rs.arrays)
    return (out[:3], rs.split(out[3:])) if riders else out


def _adamw_math(w, g, m, v):
    m = ADAM_B1 * m + (1.0 - ADAM_B1) * g
    v = ADAM_B2 * v + (1.0 - ADAM_B2) * (g * g)
    m_hat = m / (1.0 - ADAM_B1 ** ADAM_STEP)
    v_hat = v / (1.0 - ADAM_B2 ** ADAM_STEP)
    delta = -ADAM_LR * (m_hat / (jnp.sqrt(v_hat) + ADAM_EPS) + ADAM_WD * w)
    return delta, m, v


def _adamw(w, g, m, v, name):
    r, c = w.shape
    tr = 128

    def body(w_ref, g_ref, m_ref, v_ref, go_ref, d_ref, nm_ref, nv_ref):
        g = g_ref[...]
        go_ref[...] = g
        d_ref[...], nm_ref[...], nv_ref[...] = _adamw_math(w_ref[...], g, m_ref[...], v_ref[...])

    blk = pl.BlockSpec((tr, c), lambda i: (i, 0))
    return _pallas(body, name=name, grid=(r // tr,), in_specs=[blk] * 4, out_specs=[blk] * 4,
                   out_shape=[_sds((r, c), F32)] * 4, compiler_params=_params("parallel"))(w, g, m, v)


def _tables(s):
    half = HEAD_DIM // 2
    pos = jnp.arange(s, dtype=F32)
    inv_freq = ROPE_BASE ** (-jnp.arange(half, dtype=F32) / half)
    ang = pos[:, None] * inv_freq[None, :]
    cos, sin = jnp.cos(ang), jnp.sin(ang)
    cos_f = jnp.concatenate([cos, cos], axis=-1)
    sin_f = jnp.concatenate([-sin, sin], axis=-1)
    log_g = jnp.log1p(-jnp.exp2(-(5.0 + jnp.arange(HEADS, dtype=F32))))
    p = jnp.arange(CHUNK, dtype=F32)
    decay = jnp.exp(log_g[:, None, None] * jnp.abs(p[:, None] - p[None, :]))
    k_dec = jnp.exp(log_g[None, :] * (CHUNK - 1.0 - p)[:, None])
    q_dec = jnp.exp(log_g[None, :] * (p + 1.0)[:, None])
    c_dec = jnp.exp(log_g * CHUNK)
    k_dec = jnp.tile(jnp.broadcast_to(k_dec.T[:, :, None], (HEADS, CHUNK, HEAD_DIM)), (1, RET_BLOCK_CHUNKS, 1))
    q_dec = jnp.tile(jnp.broadcast_to(q_dec.T[:, :, None], (HEADS, CHUNK, HEAD_DIM)), (1, RET_BLOCK_CHUNKS, 1))
    c_dec = jnp.broadcast_to(c_dec[:, None, None], (HEADS, 1, HEAD_DIM))
    n = RET_SUB // CHUNK
    decay = (jnp.eye(n, dtype=F32)[None, :, None, :, None] * decay[:, None, :, None, :]).reshape(HEADS, RET_SUB, RET_SUB)
    return cos_f, sin_f, decay, k_dec, q_dec, c_dec


def _rot(x, cos_f, sin_f):
    return x * cos_f + pltpu.roll(x, HEAD_DIM // 2, 1) * sin_f


def _rot_bwd(d, cos_f, sin_f):
    return d * cos_f + pltpu.roll(d * sin_f, HEAD_DIM // 2, 1)


K_SCALE = HEAD_DIM ** -0.5


def _retention_fwd(proj, gn_g, tables, riders=()):
    s = proj.shape[0]
    nb = s // RET_ROWS
    nc = s // CHUNK
    cos_f, sin_f, decay, k_dec, q_dec, c_dec = tables

    def body(q_ref, k_ref, v_ref, g_ref, cos_ref, sin_ref, dec_ref, kd_ref, qd_ref, cd_ref, gn_ref,
             ret_ref, y_ref, prev_ref, state_ref):
        @pl.when(pl.program_id(1) == 0)
        def _():
            state_ref[...] = jnp.zeros_like(state_ref)

        cosv, sinv = cos_ref[...], sin_ref[...]
        q = _rot(q_ref[...], cosv, sinv)
        k = _rot(k_ref[...], cosv, sinv) * K_SCALE
        v = v_ref[...]
        rg = g_ref[...]
        dec, cd, gn = dec_ref[...], cd_ref[...], gn_ref[...]
        kdf, qdf = k * kd_ref[...], q * qd_ref[...]
        chunks = [slice(c * CHUNK, (c + 1) * CHUNK) for c in range(RET_BLOCK_CHUNKS)]
        contribs = [_dot(kdf[rows], v[rows], TN) for rows in chunks]
        state, states = state_ref[...], []
        for c in range(RET_BLOCK_CHUNKS):
            states.append(state)
            prev_ref[c] = state.astype(prev_ref.dtype)
            state = cd * state + contribs[c]
        state_ref[...] = state
        cross = jnp.concatenate([_dot(qdf[rows], st, NN) for rows, st in zip(chunks, states)], axis=0)
        intra = []
        for b in range(RET_ROWS // RET_SUB):
            rows = slice(b * RET_SUB, (b + 1) * RET_SUB)
            intra.append(_dot(_dot(q[rows], k[rows], NT) * dec, v[rows], NN))
        y = jnp.concatenate(intra, axis=0) + cross
        y_ref[...] = y
        mu = jnp.mean(y, axis=-1, keepdims=True)
        yc = y - mu
        var = jnp.mean(yc * yc, axis=-1, keepdims=True)
        yn = yc * lax.rsqrt(var + GN_EPS) * gn
        ret_ref[...] = (rg * jax.nn.sigmoid(rg) * yn).astype(ret_ref.dtype)

    def col(off):
        return pl.BlockSpec((RET_ROWS, HEAD_DIM), lambda h, i: (i, off + h))

    pos = pl.BlockSpec((RET_ROWS, HEAD_DIM), lambda h, i: (i, 0))
    per_head = lambda shape: pl.BlockSpec((None, *shape), lambda h, i: (h, 0, 0))
    rs = _Riders(riders, 11, 3)
    res = _pallas(
        _with_riders(body, 11, 3, 1, rs, (HEADS, nb)), name="retention_fwd", grid=(HEADS, nb),
        in_specs=[col(0), col(HEADS), col(2 * HEADS), col(3 * HEADS), pos, pos,
                  per_head((RET_SUB, RET_SUB)), per_head((RET_ROWS, HEAD_DIM)), per_head((RET_ROWS, HEAD_DIM)),
                  per_head((1, HEAD_DIM)), pl.BlockSpec((1, HEAD_DIM), lambda h, i: (0, h)), *rs.in_specs],
        out_specs=[col(0), col(0),
                   pl.BlockSpec((None, RET_BLOCK_CHUNKS, HEAD_DIM, HEAD_DIM), lambda h, i: (h, i, 0, 0)),
                   *rs.out_specs],
        out_shape=[_sds((s, 2 * HEADS * HEAD_DIM), BF16), _sds((s, HEADS * HEAD_DIM), F32),
                   _sds((HEADS, nc, HEAD_DIM, HEAD_DIM), MXU_DTYPE), *rs.out_shapes],
        input_output_aliases=rs.aliases,
        scratch_shapes=[pltpu.VMEM((HEAD_DIM, HEAD_DIM), F32), *rs.scratch],
        compiler_params=_params("arbitrary", "arbitrary"),
    )(proj, proj, proj, proj, cos_f, sin_f, decay, k_dec, q_dec, c_dec, gn_g, *rs.arrays)
    return res[:3], rs.split(res[3:])


def _retention_bwd(proj, gn_g, tables, y, prev, dmix, riders=()):
    s = proj.shape[0]
    nb = s // RET_ROWS
    cos_f, sin_f, decay, k_dec, q_dec, c_dec = tables

    def body(q_ref, k_ref, v_ref, g_ref, cos_ref, sin_ref, dec_ref, kd_ref, qd_ref, cd_ref, gn_ref,
             y_ref, prev_ref, dret_ref, dproj_ref, dgn_ref, gstate_ref, stage_ref, stage_sems):
        head, blk = pl.program_id(0), pl.program_id(1)
        step = head * nb + blk
        slot = step % 2

        def writes(sl):
            rows = pl.ds(pl.multiple_of((nb - 1 - blk) * RET_ROWS, RET_ROWS), RET_ROWS)
            return [pltpu.make_async_copy(
                stage_ref.at[sl, g], dproj_ref.at[rows, pl.ds(pl.multiple_of((g * HEADS + head) * HEAD_DIM, HEAD_DIM), HEAD_DIM)],
                stage_sems.at[sl, g]) for g in range(4)]

        @pl.when(step >= 2)
        def _():
            for cp in writes(slot):
                cp.wait()

        @pl.when(blk == 0)
        def _():
            gstate_ref[...] = jnp.zeros_like(gstate_ref)
            dgn_ref[...] = jnp.zeros_like(dgn_ref)

        cosv, sinv = cos_ref[...], sin_ref[...]
        q = _rot(q_ref[...], cosv, sinv)
        k = _rot(k_ref[...], cosv, sinv) * K_SCALE
        v = v_ref[...]
        dec, kd, qd, cd, gn = dec_ref[...], kd_ref[...], qd_ref[...], cd_ref[...], gn_ref[...]
        kdf, qdf = k * kd, q * qd
        rg = g_ref[...]
        yv = y_ref[...]
        dret = dret_ref[...]
        sig = jax.nn.sigmoid(rg)
        gate = rg * sig
        mu = jnp.mean(yv, axis=-1, keepdims=True)
        yc = yv - mu
        rstd = lax.rsqrt(jnp.mean(yc * yc, axis=-1, keepdims=True) + GN_EPS)
        z = yc * rstd
        dyn = dret * gate
        stage_ref[slot, 3] = (dret * (z * gn) * (sig * (1.0 + rg * (1.0 - sig)))).astype(stage_ref.dtype)
        dgn_ref[...] += jnp.sum(dyn * z, axis=0, keepdims=True)
        dz = dyn * gn
        dy = rstd * (dz - jnp.mean(dz, axis=-1, keepdims=True) - z * jnp.mean(dz * z, axis=-1, keepdims=True))
        chunks = [slice(c * CHUNK, (c + 1) * CHUNK) for c in range(RET_BLOCK_CHUNKS)]
        dprevs = [_dot(qdf[rows], dy[rows], TN) for rows in chunks]
        gst, gsts = gstate_ref[...], [None] * RET_BLOCK_CHUNKS
        for c in reversed(range(RET_BLOCK_CHUNKS)):
            gsts[c] = gst
            gst = dprevs[c] + cd * gst
        gstate_ref[...] = gst
        dq = jnp.concatenate([_dot(dy[rows], prev_ref[c], NT) for c, rows in enumerate(chunks)], axis=0) * qd
        dk = jnp.concatenate([_dot(v[rows], g, NT) for rows, g in zip(chunks, gsts)], axis=0) * kd
        dv = jnp.concatenate([_dot(kdf[rows], g, NN) for rows, g in zip(chunks, gsts)], axis=0)
        dqi, dki, dvi = [], [], []
        for b in range(RET_ROWS // RET_SUB):
            rows = slice(b * RET_SUB, (b + 1) * RET_SUB)
            qs, ks, vs, dys = q[rows], k[rows], v[rows], dy[rows]
            dvi.append(_dot(_dot(ks, qs, NT) * dec, dys, NN))
            dqi.append(_dot(_dot(dys, vs, NT) * dec, ks, NN))
            dki.append(_dot(_dot(vs, dys, NT) * dec, qs, NN))
        dq = dq + jnp.concatenate(dqi, axis=0)
        dk = dk + jnp.concatenate(dki, axis=0)
        dv = dv + jnp.concatenate(dvi, axis=0)
        stage_ref[slot, 0] = _rot_bwd(dq, cosv, sinv).astype(stage_ref.dtype)
        stage_ref[slot, 1] = _rot_bwd(dk * K_SCALE, cosv, sinv).astype(stage_ref.dtype)
        stage_ref[slot, 2] = dv.astype(stage_ref.dtype)
        for cp in writes(slot):
            cp.start()

        @pl.when(step == HEADS * nb - 1)
        def _():
            for cp in writes(1 - slot) + writes(slot):
                cp.wait()

    rev = lambda i: nb - 1 - i

    def col(off):
        return pl.BlockSpec((RET_ROWS, HEAD_DIM), lambda h, i: (rev(i), off + h))

    pos = pl.BlockSpec((RET_ROWS, HEAD_DIM), lambda h, i: (rev(i), 0))
    per_head = lambda shape: pl.BlockSpec((None, *shape), lambda h, i: (h, 0, 0))
    rs = _Riders(riders, 14, 2)
    res = _pallas(
        _with_riders(body, 14, 2, 3, rs, (HEADS, nb)), name="retention_bwd", grid=(HEADS, nb),
        in_specs=[col(0), col(HEADS), col(2 * HEADS), col(3 * HEADS), pos, pos,
                  per_head((RET_SUB, RET_SUB)), per_head((RET_ROWS, HEAD_DIM)), per_head((RET_ROWS, HEAD_DIM)),
                  per_head((1, HEAD_DIM)), pl.BlockSpec((1, HEAD_DIM), lambda h, i: (0, h)),
                  col(0), pl.BlockSpec((None, RET_BLOCK_CHUNKS, HEAD_DIM, HEAD_DIM), lambda h, i: (h, rev(i), 0, 0)),
                  col(0), *rs.in_specs],
        out_specs=[ANY, per_head((1, HEAD_DIM)), *rs.out_specs],
        out_shape=[_sds((s, proj.shape[1]), BF16), _sds((HEADS, 1, HEAD_DIM), F32), *rs.out_shapes],
        input_output_aliases=rs.aliases,
        scratch_shapes=[pltpu.VMEM((HEAD_DIM, HEAD_DIM), F32), pltpu.VMEM((2, 4, RET_ROWS, HEAD_DIM), BF16),
                        pltpu.SemaphoreType.DMA((2, 4)), *rs.scratch],
        compiler_params=_params("arbitrary", "arbitrary"),
    )(proj, proj, proj, proj, cos_f, sin_f, decay, k_dec, q_dec, c_dec, gn_g, y, prev, dmix, *rs.arrays)
    return res[:2], rs.split(res[2:])


ATT_COL0 = 4 * HEADS
PAD_ROWS = LEFT_CHUNKS * CHUNK
NORM_ROWS = 512
GROUP_CHUNKS = 4
GROUP = GROUP_CHUNKS * CHUNK
WIN = (LEFT_CHUNKS + GROUP_CHUNKS) * CHUNK
MASKED = -1e30


def _qk_norm(x, g):
    return x * lax.rsqrt(jnp.mean(x * x, axis=-1, keepdims=True) + EPS) * g


def _band_probs(qb, kb, bias, g):
    sc = _dot(qb, kb, NT) * K_SCALE + bias
    win_chunk = lax.broadcasted_iota(jnp.int32, (GROUP, WIN), 1) // CHUNK
    sc = jnp.where(g * GROUP_CHUNKS - LEFT_CHUNKS + win_chunk >= 0, sc, MASKED)
    e = jnp.exp(sc - jnp.max(sc, axis=-1, keepdims=True))
    return e / jnp.sum(e, axis=-1, keepdims=True)


def _with_riders(core, n_in, n_out, n_scratch, rs, grid):
    n_rin, n_rout = len(rs.arrays), len(rs.out_shapes)
    if not rs.riders:
        return core
    steps = 1
    for n in grid:
        steps *= n

    def body(*refs):
        outs_at = n_in + n_rin
        scratch_at = outs_at + n_out + n_rout
        bound = rs.bind(refs[n_in:outs_at], refs[outs_at + n_out:scratch_at], refs[scratch_at + n_scratch:])
        step = 0
        for axis, n in enumerate(grid):
            step = step * n + pl.program_id(axis)
        pl.when(step == 0)(lambda: rs.run("start", bound))
        pl.when(step == int(steps * RIDER_MID))(lambda: rs.run("mid", bound))
        pl.when(step == int(steps * RIDER_LATE))(lambda: rs.run("late", bound))
        core(*refs[:n_in], *refs[outs_at:outs_at + n_out], *refs[scratch_at:scratch_at + n_scratch])
        pl.when(step == steps - 1)(lambda: rs.run("end", bound))

    return body


def _attention_fwd(proj, gq, gk, bias, mix, riders=()):
    s = proj.shape[0]
    rs = _Riders(riders, 7, 1)

    def body(q_ref, k_ref, v_ref, gq_ref, gk_ref, bias_ref, mix_ref, o_ref, kp_ref, vp_ref):
        kp_ref[0:PAD_ROWS, :] = jnp.zeros((PAD_ROWS, HEAD_DIM), kp_ref.dtype)
        vp_ref[0:PAD_ROWS, :] = jnp.zeros((PAD_ROWS, HEAD_DIM), vp_ref.dtype)
        gqv, gkv = gq_ref[...], gk_ref[...]

        def fill(b, carry):
            r0 = pl.multiple_of(b * NORM_ROWS, NORM_ROWS)
            kp_ref[pl.ds(PAD_ROWS + r0, NORM_ROWS), :] = _qk_norm(k_ref[pl.ds(r0, NORM_ROWS), :], gkv).astype(kp_ref.dtype)
            vp_ref[pl.ds(PAD_ROWS + r0, NORM_ROWS), :] = v_ref[pl.ds(r0, NORM_ROWS), :].astype(vp_ref.dtype)
            return carry

        lax.fori_loop(0, s // NORM_ROWS, fill, 0)

        def group(g, carry):
            r0 = pl.multiple_of(g * GROUP, GROUP)
            qn = _qk_norm(q_ref[pl.ds(r0, GROUP), :], gqv)
            p = _band_probs(qn, kp_ref[pl.ds(r0, WIN), :], bias_ref[...], g)
            o_ref[pl.ds(r0, GROUP), :] = _dot(p, vp_ref[pl.ds(r0, WIN), :], NN).astype(o_ref.dtype)
            return carry

        lax.fori_loop(0, s // GROUP, group, 0, unroll=2)

    def col(off):
        return pl.BlockSpec((s, HEAD_DIM), lambda h: (0, off + h))

    vec = pl.BlockSpec((1, HEAD_DIM), lambda h: (0, 0))
    res = _pallas(
        _with_riders(body, 7, 1, 2, rs, (HEADS,)), name="attention_fwd", grid=(HEADS,),
        in_specs=[col(ATT_COL0), col(ATT_COL0 + HEADS), col(ATT_COL0 + 2 * HEADS), vec, vec,
                  pl.BlockSpec((None, GROUP, WIN), lambda h: (h, 0, 0)), ANY, *rs.in_specs],
        out_specs=[col(HEADS), *rs.out_specs], out_shape=[_sds(mix.shape, mix.dtype), *rs.out_shapes],
        input_output_aliases={6: 0, **rs.aliases},
        scratch_shapes=[pltpu.VMEM((s + PAD_ROWS, HEAD_DIM), MXU_DTYPE), pltpu.VMEM((s + PAD_ROWS, HEAD_DIM), MXU_DTYPE),
                        *rs.scratch],
        compiler_params=_params("arbitrary"),
    )(proj, proj, proj, gq, gk, bias, mix, *rs.arrays)
    return res[0], rs.split(res[1:])


def _attention_bwd(proj, gq, gk, bias, dmix, dproj, riders=()):
    s = proj.shape[0]
    rs = _Riders(riders, 8, 4)

    def body(q_ref, k_ref, v_ref, gq_ref, gk_ref, bias_ref, do_ref, dproj_in_ref,
             dproj_ref, dgq_ref, dgk_ref, dbias_ref, kp_ref, vp_ref, dkp_ref, dvp_ref, dqn_ref, stage_ref, stage_sems):
        head = pl.program_id(0)

        def writes():
            return [pltpu.make_async_copy(
                stage_ref.at[g],
                dproj_ref.at[:, pl.ds(pl.multiple_of((ATT_COL0 + g * HEADS + head) * HEAD_DIM, HEAD_DIM), HEAD_DIM)],
                stage_sems.at[g]) for g in range(3)]

        kp_ref[0:PAD_ROWS, :] = jnp.zeros((PAD_ROWS, HEAD_DIM), kp_ref.dtype)
        vp_ref[0:PAD_ROWS, :] = jnp.zeros((PAD_ROWS, HEAD_DIM), vp_ref.dtype)
        dkp_ref[...] = jnp.zeros_like(dkp_ref)
        dvp_ref[...] = jnp.zeros_like(dvp_ref)
        dbias_ref[...] = jnp.zeros_like(dbias_ref)
        gqv, gkv = gq_ref[...], gk_ref[...]

        def fill(b, carry):
            r0 = pl.multiple_of(b * NORM_ROWS, NORM_ROWS)
            kp_ref[pl.ds(PAD_ROWS + r0, NORM_ROWS), :] = _qk_norm(k_ref[pl.ds(r0, NORM_ROWS), :], gkv).astype(kp_ref.dtype)
            vp_ref[pl.ds(PAD_ROWS + r0, NORM_ROWS), :] = v_ref[pl.ds(r0, NORM_ROWS), :].astype(vp_ref.dtype)
            return carry

        lax.fori_loop(0, s // NORM_ROWS, fill, 0)

        def group(g, carry):
            r0 = pl.multiple_of(g * GROUP, GROUP)
            qn = _qk_norm(q_ref[pl.ds(r0, GROUP), :], gqv)
            kb = kp_ref[pl.ds(r0, WIN), :]
            vb = vp_ref[pl.ds(r0, WIN), :]
            p = _band_probs(qn, kb, bias_ref[...], g)
            do = do_ref[pl.ds(r0, GROUP), :]
            dvp_ref[pl.ds(r0, WIN), :] += _dot(p, do, TN)
            dp = _dot(do, vb, NT)
            ds = p * (dp - jnp.sum(dp * p, axis=-1, keepdims=True))
            dbias_ref[...] += ds
            dss = ds * K_SCALE
            dqn_ref[pl.ds(r0, GROUP), :] = _dot(dss, kb, NN)
            dkp_ref[pl.ds(r0, WIN), :] += _dot(dss, qn, TN)
            return carry

        lax.fori_loop(0, s // GROUP, group, 0, unroll=2)

        @pl.when(head == 0)
        def _():
            dgq_ref[...] = jnp.zeros_like(dgq_ref)
            dgk_ref[...] = jnp.zeros_like(dgk_ref)

        @pl.when(head > 0)
        def _():
            for cp in writes():
                cp.wait()

        def norm_bwd(x, g, dn):
            rstd = lax.rsqrt(jnp.mean(x * x, axis=-1, keepdims=True) + EPS)
            xh = x * rstd
            dxh = dn * g
            return rstd * (dxh - xh * jnp.mean(dxh * xh, axis=-1, keepdims=True)), jnp.sum(dn * xh, axis=0, keepdims=True)

        def finish(b, carry):
            r0 = pl.multiple_of(b * NORM_ROWS, NORM_ROWS)
            rows = pl.ds(r0, NORM_ROWS)
            dq, dgq = norm_bwd(q_ref[rows, :], gqv, dqn_ref[rows, :])
            dk, dgk = norm_bwd(k_ref[rows, :], gkv, dkp_ref[pl.ds(PAD_ROWS + r0, NORM_ROWS), :])
            stage_ref[0, rows, :] = dq.astype(stage_ref.dtype)
            stage_ref[1, rows, :] = dk.astype(stage_ref.dtype)
            stage_ref[2, rows, :] = dvp_ref[pl.ds(PAD_ROWS + r0, NORM_ROWS), :].astype(stage_ref.dtype)
            dgq_ref[...] += dgq
            dgk_ref[...] += dgk
            return carry

        lax.fori_loop(0, s // NORM_ROWS, finish, 0)
        for cp in writes():
            cp.start()

        @pl.when(head == HEADS - 1)
        def _():
            for cp in writes():
                cp.wait()

    def col(off):
        return pl.BlockSpec((s, HEAD_DIM), lambda h: (0, off + h))

    vec = pl.BlockSpec((1, HEAD_DIM), lambda h: (0, 0))
    hbias = pl.BlockSpec((None, GROUP, WIN), lambda h: (h, 0, 0))
    res = _pallas(
        _with_riders(body, 8, 4, 7, rs, (HEADS,)), name="attention_bwd", grid=(HEADS,),
        in_specs=[col(ATT_COL0), col(ATT_COL0 + HEADS), col(ATT_COL0 + 2 * HEADS), vec, vec, hbias, col(HEADS), ANY,
                  *rs.in_specs],
        out_specs=[ANY, vec, vec, hbias, *rs.out_specs],
        out_shape=[_sds(dproj.shape, dproj.dtype), _sds((1, HEAD_DIM), F32), _sds((1, HEAD_DIM), F32),
                   _sds((HEADS, GROUP, WIN), F32), *rs.out_shapes],
        input_output_aliases={7: 0, **rs.aliases},
        scratch_shapes=[pltpu.VMEM((s + PAD_ROWS, HEAD_DIM), MXU_DTYPE), pltpu.VMEM((s + PAD_ROWS, HEAD_DIM), MXU_DTYPE),
                        pltpu.VMEM((s + PAD_ROWS, HEAD_DIM), F32), pltpu.VMEM((s + PAD_ROWS, HEAD_DIM), F32),
                        pltpu.VMEM((s, HEAD_DIM), F32), pltpu.VMEM((3, s, HEAD_DIM), BF16),
                        pltpu.SemaphoreType.DMA((3,)), *rs.scratch],
        compiler_params=_params("arbitrary"),
    )(proj, proj, proj, gq, gk, bias, dmix, dproj, *rs.arrays)
    return res[:4], rs.split(res[4:])


DIAG_SPLIT = (BAND + WIN - CHUNK) // 2


def _diag_bin(m):
    t = jnp.where(m < DIAG_SPLIT, m, m - WIN)
    return jnp.clip(LEFT_CHUNKS * CHUNK - t, -(CHUNK - 1), REL_CLIP) + (CHUNK - 1)


def _skew_rows(a, left):
    row = lax.broadcasted_iota(jnp.int32, (GROUP, WIN), 0)
    for b in range(GROUP.bit_length() - 1):
        step = 1 << b
        a = jnp.where(jnp.bitwise_and(row, step) != 0, pltpu.roll(a, WIN - step if left else step, 1), a)
    return a


def _rel_bias_expand(rel_bias):
    def body(rb_ref, o_ref):
        h = pl.program_id(0)
        bins = _diag_bin(lax.broadcasted_iota(jnp.int32, (8, WIN), 1))
        per_diag = lax.fori_loop(0, REL_SIZE, lambda r, acc: jnp.where(bins == r, rb_ref[h, r], acc),
                                 jnp.zeros((8, WIN), F32))
        table = _skew_rows(jnp.broadcast_to(per_diag[0:1], (GROUP, WIN)), left=False)
        row_chunk = lax.broadcasted_iota(jnp.int32, (GROUP, WIN), 0) // CHUNK
        col_chunk = lax.broadcasted_iota(jnp.int32, (GROUP, WIN), 1) // CHUNK
        in_band = jnp.logical_and(col_chunk >= row_chunk, col_chunk <= row_chunk + LEFT_CHUNKS)
        o_ref[...] = jnp.where(in_band, table, MASKED)

    return _pallas(body, name="rel_bias_expand", grid=(HEADS,), in_specs=[pl.BlockSpec(memory_space=pltpu.SMEM)],
                   out_specs=pl.BlockSpec((None, GROUP, WIN), lambda h: (h, 0, 0)),
                   out_shape=_sds((HEADS, GROUP, WIN), F32), compiler_params=_params("parallel"))(rel_bias)


def _rel_bias_fold(dbias):
    def body(a_ref, o_ref):
        diag = jnp.sum(_skew_rows(a_ref[...], left=True), axis=0, keepdims=True)
        onehot = (_diag_bin(lax.broadcasted_iota(jnp.int32, (WIN, REL_SIZE), 0))
                  == lax.broadcasted_iota(jnp.int32, (WIN, REL_SIZE), 1)).astype(MXU_DTYPE)
        rest = jnp.broadcast_to(diag, (8, WIN))
        out = jnp.zeros((8, REL_SIZE), F32)
        for _ in range(3):
            piece = rest.astype(BF16)
            out = out + _dot(piece, onehot, NN)
            rest = rest - piece.astype(F32)
        o_ref[...] = out[0:1]

    return _pallas(body, name="rel_bias_fold", grid=(HEADS,),
                   in_specs=[pl.BlockSpec((None, GROUP, WIN), lambda h: (h, 0, 0))],
                   out_specs=pl.BlockSpec((None, 1, REL_SIZE), lambda h: (h, 0, 0)),
                   out_shape=_sds((HEADS, 1, REL_SIZE), F32), compiler_params=_params("parallel"))(dbias)


def _place():
    return lax.axis_index("x"), lax.axis_index("y"), lax.axis_index("c")


def _other_chips(x, y):
    return [(1 - x, y), (x, 1 - y), (1 - x, 1 - y)]


class _Rider:
    reads, ins, new, n_sems = (), (), (), 1

    def start(self, reads, ins, new, send, recv):
        pass

    def mid(self, reads, ins, new, send, recv):
        pass

    def late(self, reads, ins, new, send, recv):
        pass

    def end(self, reads, ins, new, send, recv):
        pass


class _Riders:
    def __init__(self, riders, n_host_in, n_host_out):
        self.riders = list(riders)
        self.arrays, self.out_shapes, self.aliases, self.scratch = [], [], {}, []
        for r in self.riders:
            for t, a in enumerate(r.ins):
                self.aliases[n_host_in + len(self.arrays) + len(r.reads) + t] = n_host_out + len(self.out_shapes) + t
            self.arrays += [*r.reads, *r.ins]
            self.out_shapes += [_sds(a.shape, a.dtype) for a in r.ins] + list(r.new)
            self.scratch += [pltpu.SemaphoreType.DMA((r.n_sems,)), pltpu.SemaphoreType.DMA((r.n_sems,))]
        self.in_specs = [ANY] * len(self.arrays)
        self.out_specs = [ANY] * len(self.out_shapes)

    def bind(self, in_refs, out_refs, scratch_refs):
        bound, i, o = [], 0, 0
        for t, r in enumerate(self.riders):
            reads = in_refs[i:i + len(r.reads)]
            i += len(r.reads) + len(r.ins)
            ins = out_refs[o:o + len(r.ins)]
            new = out_refs[o + len(r.ins):o + len(r.ins) + len(r.new)]
            o += len(r.ins) + len(r.new)
            bound.append((reads, ins, new, scratch_refs[2 * t], scratch_refs[2 * t + 1]))
        return bound

    def run(self, phase, bound):
        for r, b in zip(self.riders, bound):
            getattr(r, phase)(*b)

    def split(self, outs):
        res, o = [], 0
        for r in self.riders:
            n = len(r.ins) + len(r.new)
            res.append(list(outs[o:o + n]))
            o += n
        return res


def _run_riders(name, riders):
    rs = _Riders(riders, 0, 0)
    n_in, n_out = len(rs.arrays), len(rs.out_shapes)

    def body(*refs):
        bound = rs.bind(refs[:n_in], refs[n_in:n_in + n_out], refs[n_in + n_out:])
        rs.run("start", bound)
        rs.run("mid", bound)
        rs.run("late", bound)
        rs.run("end", bound)

    outs = _pallas(body, name=name, in_specs=rs.in_specs, out_specs=rs.out_specs, out_shape=rs.out_shapes,
                   input_output_aliases=rs.aliases, scratch_shapes=rs.scratch)(*rs.arrays)
    return rs.split(outs)


class _GatherRider(_Rider):
    X_LINK, Y_LINK, Y_PASS, X_PASS, D2D_X, D2D_Y, D2D_DIAG, N_SEMS = 0, 1, 2, 3, 4, 5, 6, 7

    def __init__(self, blocks, part=(0, 1, 1)):
        self.ins = tuple(blocks)
        self.part = part
        self.n_sems = self.N_SEMS * len(blocks)

    def _copy(self, out, send, recv, w, sem, chip_from, cc, to, sub=None):
        hr = self.ins[w].shape[1] // 2
        lo, hi, n = self.part
        first, size = cc * hr + lo * (hr // n), (hi - lo) * (hr // n)
        if sub is not None:
            size //= 2
            first += sub * size
        piece = out[w].at[2 * chip_from[0] + chip_from[1], pl.ds(first, size), :]
        return pltpu.make_async_remote_copy(src_ref=piece, dst_ref=piece, send_sem=send.at[self.N_SEMS * w + sem],
                                            recv_sem=recv.at[self.N_SEMS * w + sem], device_id=to, device_id_type=MESH)

    def _sent(self, out, send, recv, w):
        x, y, c = _place()
        me, sib = (x, y), (x, y, 1 - c)
        xn, yn, diag = _other_chips(x, y)
        cp = functools.partial(self._copy, out, send, recv, w)
        return [("start", cp(self.X_LINK, me, c, (*xn, c))), ("start", cp(self.Y_LINK, me, c, (*yn, c))),
                ("mid_x", cp(self.D2D_X, xn, c, sib)), ("mid_x", cp(self.Y_PASS, xn, c, (*yn, c), sub=0)),
                ("mid_y", cp(self.D2D_Y, yn, c, sib)), ("mid_y", cp(self.X_PASS, yn, c, (*xn, c), sub=1)),
                ("late", cp(self.D2D_DIAG, diag, c, sib))]

    def _go(self, out, send, recv, phase):
        for w in range(len(self.ins)):
            for ph, copy in self._sent(out, send, recv, w):
                if ph == phase:
                    copy.start()

    def start(self, reads, out, new, send, recv):
        self._go(out, send, recv, "start")

    def mid(self, reads, out, new, send, recv):
        x, y, c = _place()
        xn, yn, _ = _other_chips(x, y)
        for w in range(len(self.ins)):
            self._copy(out, send, recv, w, self.X_LINK, xn, c, (x, y, c)).wait_recv()
        self._go(out, send, recv, "mid_x")
        for w in range(len(self.ins)):
            self._copy(out, send, recv, w, self.Y_LINK, yn, c, (x, y, c)).wait_recv()
        self._go(out, send, recv, "mid_y")

    def late(self, reads, out, new, send, recv):
        x, y, c = _place()
        diag = _other_chips(x, y)[2]
        for w in range(len(self.ins)):
            self._copy(out, send, recv, w, self.Y_PASS, diag, c, (x, y, c), sub=0).wait_recv()
            self._copy(out, send, recv, w, self.X_PASS, diag, c, (x, y, c), sub=1).wait_recv()
        self._go(out, send, recv, "late")

    def end(self, reads, out, new, send, recv):
        x, y, c = _place()
        xn, yn, diag = _other_chips(x, y)
        for w in range(len(self.ins)):
            for sem, chip in ((self.D2D_X, xn), (self.D2D_Y, yn), (self.D2D_DIAG, diag)):
                self._copy(out, send, recv, w, sem, chip, 1 - c, (x, y, c)).wait_recv()
        for w in range(len(self.ins)):
            for _, copy in self._sent(out, send, recv, w):
                copy.wait_send()


class _SwapRider(_Rider):
    def __init__(self, grads):
        self.reads = tuple(grads)
        self.new = tuple(_sds((N_CHIPS, g.shape[1] // 2, g.shape[2]), g.dtype) for g in grads)
        self.n_sems = len(grads)

    def _copies(self, src, new, send, recv):
        x, y, c = _place()
        copies = []
        for w in range(len(self.reads)):
            hr = self.reads[w].shape[1] // 2
            copies.append(pltpu.make_async_remote_copy(
                src_ref=src[w].at[:, pl.ds((1 - c) * hr, hr), :], dst_ref=new[w],
                send_sem=send.at[w], recv_sem=recv.at[w], device_id=(x, y, 1 - c), device_id_type=MESH))
        return copies

    def start(self, src, ins, new, send, recv):
        for cp in self._copies(src, new, send, recv):
            cp.start()

    def end(self, src, ins, new, send, recv):
        for cp in self._copies(src, new, send, recv):
            cp.wait()


def _add_half(g, got, c_arr, name):
    nk, r, cols = g.shape
    hr = r // 2
    tr = min(hr, 256)
    nb = hr // tr

    def body(c_ref, g_ref, got_ref, o_ref):
        o_ref[...] = (g_ref[...].astype(F32) + got_ref[...].astype(F32)).astype(o_ref.dtype)

    grid_spec = pltpu.PrefetchScalarGridSpec(
        num_scalar_prefetch=1, grid=(nk, nb),
        in_specs=[pl.BlockSpec((None, tr, cols), lambda k, i, c_ref: (k, c_ref[0] * nb + i, 0)),
                  pl.BlockSpec((None, tr, cols), lambda k, i, c_ref: (k, i, 0))],
        out_specs=pl.BlockSpec((None, tr, cols), lambda k, i, c_ref: (k, i, 0)))
    return _pallas(body, name=name, grid_spec=grid_spec, out_shape=_sds((nk, hr, cols), g.dtype),
                   compiler_params=_params("parallel", "parallel"))(c_arr, g, got)


class _SendPartialsRider(_Rider):
    def __init__(self, parts, got=None, part=(0, 1, 1)):
        self.reads = tuple(parts)
        if got is None:
            self.new = tuple(_sds((N_CHIPS - 1, *p.shape[1:]), p.dtype) for p in parts)
        else:
            self.ins = tuple(got)
        self.part = part
        self.n_sems = 3 * len(parts)

    def _copies(self, src, ins, new, send, recv):
        x, y, c = _place()
        land = ins if self.ins else new
        lo, hi, n = self.part
        copies = []
        for w in range(len(self.reads)):
            pr = self.reads[w].shape[1] // n
            rows = pl.ds(lo * pr, (hi - lo) * pr)
            for j, chip in enumerate(_other_chips(x, y)):
                copies.append(pltpu.make_async_remote_copy(
                    src_ref=src[w].at[2 * chip[0] + chip[1], rows, :], dst_ref=land[w].at[j, rows, :],
                    send_sem=send.at[3 * w + j], recv_sem=recv.at[3 * w + j], device_id=(*chip, c), device_id_type=MESH))
        return copies

    def start(self, src, ins, new, send, recv):
        for cp in self._copies(src, ins, new, send, recv):
            cp.start()

    def end(self, src, ins, new, send, recv):
        for cp in self._copies(src, ins, new, send, recv):
            cp.wait()


def _sum_partials(part, got, kc_arr, name):
    _, hr, cols = part.shape
    tr = min(hr, 256)
    nb = hr // tr

    def body(kc_ref, p_ref, g0_ref, g1_ref, g2_ref, o_ref):
        o_ref[...] = ((p_ref[...].astype(F32) + g0_ref[...].astype(F32)) + g1_ref[...].astype(F32)) + g2_ref[...].astype(F32)

    slot = lambda j: pl.BlockSpec((None, tr, cols), lambda i, kc_ref: (j, i, 0))
    grid_spec = pltpu.PrefetchScalarGridSpec(
        num_scalar_prefetch=1, grid=(nb,),
        in_specs=[pl.BlockSpec((None, tr, cols), lambda i, kc_ref: (kc_ref[0], i, 0)), slot(0), slot(1), slot(2)],
        out_specs=pl.BlockSpec((tr, cols), lambda i, kc_ref: (kc_ref[1] * nb + i, 0)))
    return _pallas(body, name=name, grid_spec=grid_spec, out_shape=_sds((2 * hr, cols), F32),
                   compiler_params=_params("parallel"))(kc_arr, part, got, got, got)


class _ShareRider(_Rider):
    def __init__(self, grads):
        self.ins = tuple(grads)
        self.n_sems = len(grads)

    def _copies(self, out, send, recv):
        x, y, c = _place()
        copies = []
        for w in range(len(self.ins)):
            hr = self.ins[w].shape[0] // 2
            mine = out[w].at[pl.ds(c * hr, hr), :]
            copies.append(pltpu.make_async_remote_copy(
                src_ref=mine, dst_ref=mine, send_sem=send.at[w], recv_sem=recv.at[w],
                device_id=(x, y, 1 - c), device_id_type=MESH))
        return copies

    def start(self, reads, out, new, send, recv):
        for cp in self._copies(out, send, recv):
            cp.start()

    def end(self, reads, out, new, send, recv):
        for cp in self._copies(out, send, recv):
            cp.wait()


def _small_allreduce_adamw(g_part, w, m, v):
    rows = g_part.shape[0]

    def body(g_ref, w_ref, m_ref, v_ref, go_ref, d_ref, nm_ref, nv_ref, all_ref, send_sems, recv_sems):
        x, y, c = _place()
        me = 4 * x + 2 * y + c
        all_ref[me] = g_ref[...]
        copies = []
        for r in range(1, 8):
            dx, dy, dc = (r >> 2) & 1, (r >> 1) & 1, r & 1
            peer = (1 - x if dx else x, 1 - y if dy else y, 1 - c if dc else c)
            copies.append(pltpu.make_async_remote_copy(
                src_ref=g_ref, dst_ref=all_ref.at[me], send_sem=send_sems.at[r - 1], recv_sem=recv_sems.at[r - 1],
                device_id=peer, device_id_type=MESH))
        for cp in copies:
            cp.start()
        for cp in copies:
            cp.wait()
        tot = all_ref[0]
        for d in range(1, 8):
            tot = tot + all_ref[d]
        go_ref[...] = tot
        d_ref[...], nm_ref[...], nv_ref[...] = _adamw_math(w_ref[...], tot, m_ref[...], v_ref[...])

    vm = pl.BlockSpec(memory_space=pltpu.VMEM)
    return _pallas(
        body, name="small_allreduce_adamw", in_specs=[vm] * 4, out_specs=[vm] * 4,
        out_shape=[_sds((rows, 128), F32)] * 4,
        scratch_shapes=[pltpu.VMEM((8, rows, 128), F32), pltpu.SemaphoreType.DMA((7,)), pltpu.SemaphoreType.DMA((7,))],
    )(g_part, w, m, v)


SMALL_SIZES = (2048, 1024, 128, 128, HEADS * REL_SIZE, 2048)
SMALL_PART_ROWS = tuple(-(-size // 1024) * 8 for size in SMALL_SIZES)
SMALL_ROWS = sum(SMALL_PART_ROWS)


def _pack_small(parts):
    rows = []
    for p, size, nr in zip(parts, SMALL_SIZES, SMALL_PART_ROWS):
        rows.append(jnp.pad(p.reshape(-1), (0, nr * 128 - size)).reshape(nr, 128))
    return jnp.concatenate(rows, axis=0)


def _unpack_small(slab, shapes):
    out, off = [], 0
    for size, nr, shape in zip(SMALL_SIZES, SMALL_PART_ROWS, shapes):
        out.append(slab[off:off + nr].reshape(-1)[:size].reshape(shape))
        off += nr
    return out


def kernel(x, norm1_g, w_in, ret_norm_g, q_norm_g, k_norm_g, rel_bias, w_out, norm2_g, w_ff1, w_ff2, loss_target, m_norm1_g, m_w_in, m_ret_norm_g, m_q_norm_g, m_k_norm_g, m_rel_bias, m_w_out, m_norm2_g, m_w_ff1, m_w_ff2, v_norm1_g, v_w_in, v_ret_norm_g, v_q_norm_g, v_k_norm_g, v_rel_bias, v_w_out, v_norm2_g, v_w_ff1, v_w_ff2):
    xs = x[0]
    tgt = loss_target[0]
    s, d = xs.shape
    d_in = N_CHIPS * w_in.shape[2]
    d_ff = N_CHIPS * w_ff1.shape[2]
    in_sh, ff_sh = w_in.shape[2], w_ff1.shape[2]
    tm = min(s, 1024)
    gi = s // tm
    c_arr = lax.axis_index("c").astype(jnp.int32).reshape(1)
    k_arr = (2 * lax.axis_index("x") + lax.axis_index("y")).astype(jnp.int32).reshape(1)
    tables = _tables(s)
    bias = _rel_bias_expand(rel_bias[0])

    blk_in = [_cast_bf16(w_in[0], k_arr, "cast_w_in_%d" % half, cols=(half, 2)) for half in range(2)]
    blk_out, blk_ff1, blk_ff2 = (_cast_bf16(w_out[0], k_arr, "cast_w_out"), _cast_bf16(w_ff1[0], k_arr, "cast_w_ff1"),
                                 _cast_bf16(w_ff2[0], k_arr, "cast_w_ff2"))
    ((wg_in0,),) = _run_riders("all_gather_w_in_0", [_GatherRider([blk_in[0]])])

    h1 = _rmsnorm_fwd(xs, norm1_g, "rmsnorm1")
    tn_in = in_sh // 2
    tk = d

    def proj_half(half, wg, through, riders):
        return _mm("proj_%d" % half, h1, wg, NN, (gi, N_CHIPS, 1),
                   pl.BlockSpec((tm, tk), lambda i, j, k: (i, 0)), pl.BlockSpec((None, tk, tn_in), lambda i, j, k: (j, 0, 0)),
                   [_sds((s, d_in), F32)], [pl.BlockSpec((tm, tn_in), lambda i, j, k: (i, 2 * j + half))], (tm, tn_in),
                   riders=riders, through=through)

    (proj,), ((wg_in1,),) = proj_half(0, wg_in0, None, [_GatherRider([blk_in[1]])])
    (proj,), ((wg_ff1,),) = proj_half(1, wg_in1, proj, [_GatherRider([blk_ff1], (0, 3, 8))])
    (mix, y_ret, prev), ((wg_ff1,),) = _retention_fwd(proj, ret_norm_g, tables, riders=[_GatherRider([wg_ff1], (3, 6, 8))])
    mix, ((wg_out,), (wg_ff2,)) = _attention_fwd(
        proj, q_norm_g, k_norm_g, bias, mix, riders=[_GatherRider([blk_out]), _GatherRider([blk_ff2], (0, 1, 4))])
    wg_out = wg_out.reshape(d, d)
    tn = 1024
    tile = pl.BlockSpec((tm, tn), lambda i, j, k: (i, j))
    def residual_norm(acc, res, g):
        x1v = res + acc
        yv = x1v * lax.rsqrt(jnp.mean(x1v * x1v, axis=-1, keepdims=True) + EPS)
        return x1v, yv * g

    tmo = min(s, 512)
    rows = pl.BlockSpec((tmo, d), lambda i, j, k: (i, 0))
    (x1, h2), ((wg_ff1,),) = _mm(
        "out_proj", mix, wg_out, NN, (s // tmo, 1, 1),
        rows, pl.BlockSpec((d, d), lambda i, j, k: (0, 0)),
        [_sds((s, d), F32), _sds((s, d), BF16)], [rows, rows], (tmo, d),
        extras=(xs, norm2_g), extra_specs=(rows, pl.BlockSpec((1, d), lambda i, j, k: (0, 0))),
        epi=residual_norm, riders=[_GatherRider([wg_ff1], (6, 8, 8))])
    tn_ff = min(ff_sh, 1024)
    per = ff_sh // tn_ff

    def relu2(acc):
        r = jnp.maximum(acc, 0.0)
        return acc, r * r

    (u, act), ((wg_ff2,),) = _mm(
        "ff1", h2, wg_ff1, NN, (gi, N_CHIPS * per, d // tk),
        pl.BlockSpec((tm, tk), lambda i, j, k: (i, k)),
        pl.BlockSpec((None, tk, tn_ff), lambda i, j, k: (j // per, k, j % per)),
        [_sds((s, d_ff), F32), _sds((s, d_ff), BF16)],
        [pl.BlockSpec((tm, tn_ff), lambda i, j, k: (i, j))] * 2, (tm, tn_ff), epi=relu2,
        riders=[_GatherRider([wg_ff2], (1, 4, 4))])
    wg_ff2 = wg_ff2.reshape(d_ff, d)

    def loss_epi(acc, res, t):
        diff = (res + acc) - t
        dy = diff / d
        return dy, dy, jnp.sum(diff * diff, axis=0, keepdims=True)

    tk2 = min(tk, 2048)
    dy, dyb, loss_cols = _mm(
        "ff2_loss", act, wg_ff2, NN, (gi, d // tn, d_ff // tk2),
        pl.BlockSpec((tm, tk2), lambda i, j, k: (i, k)), pl.BlockSpec((tk2, tn), lambda i, j, k: (k, j)),
        [_sds((s, d), F32), _sds((s, d), BF16), _sds((gi, 1, d), F32)],
        [tile, tile, pl.BlockSpec((None, 1, tn), lambda i, j, k: (i, 0, j))], (tm, tn),
        extras=(x1, tgt), extra_specs=(tile, tile), epi=loss_epi)
    loss = lax.psum(0.5 * jnp.sum(loss_cols) / d, ("x", "y", "c"))

    (du,) = _mm("d_act", dyb, wg_ff2, NT, (gi, d_ff // tn, d // tk),
                pl.BlockSpec((tm, tk), lambda i, j, k: (i, k)), pl.BlockSpec((tn, tk), lambda i, j, k: (j, k)),
                [_sds((s, d_ff), BF16)], [tile], (tm, tn), extras=(u,), extra_specs=(tile,),
                epi=lambda acc, uu: (acc * (2.0 * jnp.maximum(uu, 0.0)),))
    ts = min(s, 2048)
    wtile = pl.BlockSpec((tn, tn), lambda i, j, k: (i, j))
    (g_ff2,) = _mm("dw_ff2", act, dyb, TN, (d_ff // tn, d // tn, s // ts),
                   pl.BlockSpec((ts, tn), lambda i, j, k: (k, i)), pl.BlockSpec((ts, tn), lambda i, j, k: (k, j)),
                   [_sds((d_ff, d), BF16)], [wtile], (tn, tn))
    g_ff2 = g_ff2.reshape(N_CHIPS, d_ff // N_CHIPS, d)
    (g_ff1,), ((got_ff2,),) = _mm(
        "dw_ff1", h2, du, TN, (d // tn, N_CHIPS * per, s // ts),
        pl.BlockSpec((ts, tn), lambda i, j, k: (k, i)), pl.BlockSpec((ts, tn_ff), lambda i, j, k: (k, j)),
        [_sds((N_CHIPS, d, ff_sh), BF16)],
        [pl.BlockSpec((None, tn, tn_ff), lambda i, j, k: (j // per, i, j % per))], (tn, tn_ff),
        riders=[_SwapRider([g_ff2])])
    p_ff2 = _add_half(g_ff2, got_ff2, c_arr, "chip_partial_w_ff2")
    tkf = min(tk, ff_sh)
    kper = ff_sh // tkf
    (dh2,), ((got2_ff2,), (got_ff1,)) = _mm(
        "d_h2", du, wg_ff1, NT, (gi, d // tn, d_ff // tkf),
        pl.BlockSpec((tm, tkf), lambda i, j, k: (i, k)),
        pl.BlockSpec((None, tn, tkf), lambda i, j, k: (k // kper, j, k % kper)),
        [_sds((s, d), F32)], [tile], (tm, tn),
        riders=[_SendPartialsRider([p_ff2], part=(0, 3, 4)), _SwapRider([g_ff1])])
    p_ff1 = _add_half(g_ff1, got_ff1, c_arr, "chip_partial_w_ff1")
    dx1, dx1b, g_norm2 = _rmsnorm_bwd(x1, norm2_g, dh2, dy, "rmsnorm2_bwd")

    (dmix,) = _mm("d_mix", dx1b, wg_out, NT, (gi, d // tn, d // tk),
                  pl.BlockSpec((tm, tk), lambda i, j, k: (i, k)), pl.BlockSpec((tn, tk), lambda i, j, k: (j, k)),
                  [_sds((s, d), F32)], [tile], (tm, tn))
    (g_out,) = _mm("dw_out", mix, dx1b, TN, (d // tn, d // tn, s // ts),
                   pl.BlockSpec((ts, tn), lambda i, j, k: (k, i)), pl.BlockSpec((ts, tn), lambda i, j, k: (k, j)),
                   [_sds((d, d), BF16)], [wtile], (tn, tn))
    g_out = g_out.reshape(N_CHIPS, d // N_CHIPS, d)
    (dproj, g_gn), ((got2_ff2,), (got2_ff1,), (got_out,)) = _retention_bwd(
        proj, ret_norm_g, tables, y_ret, prev, dmix,
        riders=[_SendPartialsRider([p_ff2], got=[got2_ff2], part=(3, 4, 4)), _SendPartialsRider([p_ff1], part=(0, 2, 4)),
                _SwapRider([g_out])])
    p_out = _add_half(g_out, got_out, c_arr, "chip_partial_w_out")
    (dproj, g_gq, g_gk, dbias), ((got2_ff1,), (got2_out,)) = _attention_bwd(
        proj, q_norm_g, k_norm_g, bias, dmix, dproj,
        riders=[_SendPartialsRider([p_ff1], got=[got2_ff1], part=(2, 4, 4)), _SendPartialsRider([p_out])])
    g_rel = _rel_bias_fold(dbias)
    names = ["w_in", "w_out", "w_ff1", "w_ff2"]
    kc_arr = jnp.concatenate([k_arr, c_arr])
    early = [_sum_partials(p, r, kc_arr, "sum_partials_" + nm)
             for p, r, nm in zip((p_out, p_ff1, p_ff2), (got2_out, got2_ff1, got2_ff2), names[1:])]
    (g_in,), (early,) = _mm(
        "dw_in", h1, dproj, TN, (d // tn, 2 * N_CHIPS, s // ts),
        pl.BlockSpec((ts, tn), lambda i, j, k: (k, i)), pl.BlockSpec((ts, tn_in), lambda i, j, k: (k, j)),
        [_sds((N_CHIPS, d, in_sh), BF16)],
        [pl.BlockSpec((None, tn, tn_in), lambda i, j, k: (j // 2, i, j % 2))], (tn, tn_in), riders=[_ShareRider(early)])
    ((got_in,),) = _run_riders("grad_swap_w_in", [_SwapRider([g_in])])
    p_in = _add_half(g_in, got_in, c_arr, "chip_partial_w_in")
    half_spec = pl.BlockSpec((None, tn, tn_in), lambda i, j, k: (k // 2, j, 0))
    (dh1,), ((got2_in,),) = _mm(
        "d_h1", dproj, [wg_in0, wg_in1], NT, (gi, d // tn, 2 * N_CHIPS),
        pl.BlockSpec((tm, tn_in), lambda i, j, k: (i, k)), [half_spec, half_spec],
        [_sds((s, d), F32)], [tile], (tm, tn), riders=[_SendPartialsRider([p_in])])
    grad_x, _, g_norm1 = _rmsnorm_bwd(xs, norm1_g, dh1, dx1, "rmsnorm1_bwd")
    ((g_w_in,),) = _run_riders("grad_share_w_in", [_ShareRider([_sum_partials(p_in, got2_in, kc_arr, "sum_partials_w_in")])])
    g_big = [g_w_in, *early]
    big = []
    for g, w, m, v, nm in zip(g_big, (w_in, w_out, w_ff1, w_ff2), (m_w_in, m_w_out, m_w_ff1, m_w_ff2),
                              (v_w_in, v_w_out, v_w_ff1, v_w_ff2), names):
        g, delta, new_m, new_v = _adamw(w[0], g, m[0], v[0], "adamw_" + nm)
        big.append((g[None], delta[None], new_m[None], new_v[None]))

    small_w = (norm1_g, ret_norm_g, q_norm_g, k_norm_g, rel_bias, norm2_g)
    small_m = (m_norm1_g, m_ret_norm_g, m_q_norm_g, m_k_norm_g, m_rel_bias, m_norm2_g)
    small_v = (v_norm1_g, v_ret_norm_g, v_q_norm_g, v_k_norm_g, v_rel_bias, v_norm2_g)
    shapes = [p.shape for p in small_w]
    g_small = _pack_small([g_norm1, g_gn, g_gq, g_gk, g_rel, g_norm2])
    sg, sd, sm, sv = (_unpack_small(a, shapes) for a in _small_allreduce_adamw(
        g_small, _pack_small(small_w), _pack_small(small_m), _pack_small(small_v)))

    def ordered(kind):
        sm_ = (sg, sd, sm, sv)[kind]
        return (sm_[0], big[0][kind], sm_[1], sm_[2], sm_[3], sm_[4], big[1][kind], sm_[5], big[2][kind], big[3][kind])

    return (loss, grad_x[None], *ordered(0), *ordered(1), *ordered(2), *ordered(3))
```

```python
import functools

import jax
import jax.numpy as jnp
from jax import lax
from jax.experimental import pallas as pl
from jax.experimental.pallas import tpu as pltpu

F32 = jnp.float32
BF16 = jnp.bfloat16
MXU_DTYPE = jnp.bfloat16

CHUNK = 64
HEADS = 8
HEAD_DIM = 128
LEFT_CHUNKS = 8
BAND = (LEFT_CHUNKS + 1) * CHUNK
REL_CLIP = 128
REL_SIZE = (CHUNK - 1) + REL_CLIP + 1
RET_BLOCK_CHUNKS = 8
RET_ROWS = RET_BLOCK_CHUNKS * CHUNK
RET_SUB = 256
ROPE_BASE = 10000.0
EPS = 1e-6
GN_EPS = 1e-5
ADAM_LR, ADAM_B1, ADAM_B2, ADAM_EPS, ADAM_WD, ADAM_STEP = 0.001, 0.9, 0.999, 1e-08, 0.01, 10
N_CHIPS = 4
VMEM_LIMIT = 56 * 1024 * 1024
MESH = pl.DeviceIdType.MESH
ANY = pl.BlockSpec(memory_space=pl.ANY)

NN = (((1,), (0,)), ((), ()))
NT = (((1,), (1,)), ((), ()))
TN = (((0,), (0,)), ((), ()))


def _pallas(body, **kw):
    return pl.pallas_call(body, **kw)


def _params(*sem):
    return pltpu.CompilerParams(dimension_semantics=sem, vmem_limit_bytes=VMEM_LIMIT)


def _dot(a, b, dims):
    return lax.dot_general(a.astype(MXU_DTYPE), b.astype(MXU_DTYPE), dims, preferred_element_type=F32)


RIDER_MID, RIDER_LATE = 0.5, 0.8


def _mm(name, a, b, dims, grid, a_spec, b_spec, outs, o_specs, acc_shape, extras=(), extra_specs=(), epi=None,
        riders=(), through=None):
    ni, nj, nk = grid
    n_ex, n_out = len(extras), len(outs)
    bs = list(b) if isinstance(b, (list, tuple)) else [b]
    b_specs = list(b_spec) if isinstance(b, (list, tuple)) else [b_spec]
    extras, extra_specs = (*bs[1:], *extras), (*b_specs[1:], *extra_specs)
    b, b_spec, n_b = bs[0], b_specs[0], len(bs)
    n_in = 1 + n_b + n_ex
    rs = _Riders(riders, n_in, n_out)
    n_rin, n_rout = len(rs.arrays), len(rs.out_shapes)
    steps = ni * nj * nk

    held = [] if through is None else [through]

    def body(*refs):
        a_ref, b_refs = refs[0], refs[1:1 + n_b]
        b_ref = b_refs[0]
        ex_refs = refs[1 + n_b:n_in]
        outs_at = n_in + n_rin + len(held)
        o_refs = refs[outs_at:outs_at + n_out]
        acc_ref = refs[outs_at + n_out + n_rout]
        k = pl.program_id(2)
        if riders:
            bound = rs.bind(refs[n_in:n_in + n_rin], refs[outs_at + n_out:outs_at + n_out + n_rout],
                            refs[outs_at + n_out + n_rout + 1:])
            step = (pl.program_id(0) * nj + pl.program_id(1)) * nk + k
            pl.when(step == 0)(lambda: rs.run("start", bound))
            pl.when(step == int(steps * RIDER_MID))(lambda: rs.run("mid", bound))
            pl.when(step == int(steps * RIDER_LATE))(lambda: rs.run("late", bound))

        def finish(acc):
            vals = epi(acc, *[r[...] for r in ex_refs]) if epi is not None else (acc,)
            for r, v in zip(o_refs, vals):
                r[...] = v.astype(r.dtype)

        if nk == 1:
            finish(_dot(a_ref[...], b_ref[...], dims))
        else:
            @pl.when(k == 0)
            def _():
                acc_ref[...] = jnp.zeros_like(acc_ref)

            for t, ref in enumerate(b_refs):
                def step_with(ref=ref):
                    acc_ref[...] += _dot(a_ref[...], ref[...], dims)

                if n_b == 1:
                    step_with()
                else:
                    pl.when(k % n_b == t)(step_with)
            pl.when(k == nk - 1)(lambda: finish(acc_ref[...]))

        if riders:
            pl.when(step == steps - 1)(lambda: rs.run("end", bound))

    res = _pallas(
        body, name=name, grid=grid, in_specs=[a_spec, b_spec, *extra_specs, *rs.in_specs, *[ANY for _ in held]],
        out_specs=[*o_specs, *rs.out_specs], out_shape=[*outs, *rs.out_shapes],
        input_output_aliases={**rs.aliases, **{n_in + n_rin: 0 for _ in held}},
        scratch_shapes=[pltpu.VMEM(acc_shape if nk > 1 else (8, 128), F32), *rs.scratch],
        compiler_params=_params(*(("arbitrary",) * 3 if riders else ("parallel", "parallel", "arbitrary"))),
    )(a, b, *extras, *rs.arrays, *held)
    return (res[:n_out], rs.split(res[n_out:])) if riders else res


def _sds(shape, dtype):
    return jax.ShapeDtypeStruct(shape, dtype)


def _cast_bf16(w, k_arr, name, cols=(0, 1)):
    r, c = w.shape[0], w.shape[1] // cols[1]
    tr = min(r, 256)

    def body(k_ref, w_ref, o_ref):
        o_ref[...] = w_ref[...].astype(BF16)

    grid_spec = pltpu.PrefetchScalarGridSpec(
        num_scalar_prefetch=1, grid=(r // tr,), in_specs=[pl.BlockSpec((tr, c), lambda i, k_ref: (i, cols[0]))],
        out_specs=pl.BlockSpec((None, tr, c), lambda i, k_ref: (k_ref[0], i, 0)))
    return _pallas(body, name=name, grid_spec=grid_spec, out_shape=_sds((N_CHIPS, r, c), BF16),
                   compiler_params=_params("parallel"))(k_arr, w)


def _rmsnorm_fwd(x, g, name):
    s, d = x.shape
    tr = 256

    def body(x_ref, g_ref, o_ref):
        xv = x_ref[...]
        y = xv * lax.rsqrt(jnp.mean(xv * xv, axis=-1, keepdims=True) + EPS)
        o_ref[...] = (y * g_ref[...]).astype(o_ref.dtype)

    return _pallas(body, name=name, grid=(s // tr,),
                   in_specs=[pl.BlockSpec((tr, d), lambda i: (i, 0)), pl.BlockSpec((1, d), lambda i: (0, 0))],
                   out_specs=pl.BlockSpec((tr, d), lambda i: (i, 0)), out_shape=_sds((s, d), BF16),
                   compiler_params=_params("parallel"))(x, g)


def _rmsnorm_bwd(x, g, dh, res, name, riders=()):
    s, d = x.shape
    tr = 256

    def body(x_ref, g_ref, dh_ref, res_ref, dx_ref, dxb_ref, dg_ref):
        i = pl.program_id(0)
        xv = x_ref[...]
        rstd = lax.rsqrt(jnp.mean(xv * xv, axis=-1, keepdims=True) + EPS)
        xh = xv * rstd
        dhv = dh_ref[...]

        @pl.when(i == 0)
        def _():
            dg_ref[...] = jnp.zeros_like(dg_ref)

        dg_ref[...] += jnp.sum(dhv * xh, axis=0, keepdims=True)
        dxh = dhv * g_ref[...]
        dx = res_ref[...] + rstd * (dxh - xh * jnp.mean(dxh * xh, axis=-1, keepdims=True))
        dx_ref[...] = dx
        dxb_ref[...] = dx.astype(BF16)

    row = pl.BlockSpec((tr, d), lambda i: (i, 0))
    vec = pl.BlockSpec((1, d), lambda i: (0, 0))
    rs = _Riders(riders, 4, 3)
    out = _pallas(_with_riders(body, 4, 3, 0, rs, (s // tr,)), name=name, grid=(s // tr,),
                  in_specs=[row, vec, row, row, *rs.in_specs], out_specs=[row, row, vec, *rs.out_specs],
                  out_shape=[_sds((s, d), F32), _sds((s, d), BF16), _sds((1, d), F32), *rs.out_shapes],
                  input_output_aliases=rs.aliases, scratch_shapes=rs.scratch,
                  compiler_params=_params("arbitrary"))(x, g, dh, res, *rs.arrays)
    return (out[:3], rs.split(out[3:])) if riders else out


def _rmsnorm_bwd_rows(x, g, dh, res, name, row0, n_rows, prev=None, riders=()):
    s, d = x.shape
    tr = 256
    blk0 = row0 // tr
    held = [] if prev is None else list(prev)

    def body(x_ref, g_ref, dh_ref, res_ref, *refs):
        dx_ref, dg_ref = refs[len(held):]
        xv = x_ref[...]
        rstd = lax.rsqrt(jnp.mean(xv * xv, axis=-1, keepdims=True) + EPS)
        xh = xv * rstd
        dhv = dh_ref[...]

        @pl.when(pl.program_id(0) == 0)
        def _():
            dg_ref[...] = refs[1][...] if held else jnp.zeros_like(dg_ref)

        dg_ref[...] += jnp.sum(dhv * xh, axis=0, keepdims=True)
        dxh = dhv * g_ref[...]
        dx_ref[...] = res_ref[...] + rstd * (dxh - xh * jnp.mean(dxh * xh, axis=-1, keepdims=True))

    row = pl.BlockSpec((tr, d), lambda i: (blk0 + i, 0))
    vec = pl.BlockSpec((1, d), lambda i: (0, 0))
    n_in = 4 + len(held)
    rs = _Riders(riders, n_in, 2)
    out = _pallas(_with_riders(body, n_in, 2, 0, rs, (n_rows // tr,)), name=name, grid=(n_rows // tr,),
                  in_specs=[row, vec, row, row, *([ANY, vec] if held else []), *rs.in_specs],
                  out_specs=[row, vec, *rs.out_specs],
                  out_shape=[_sds((s, d), F32), _sds((1, d), F32), *rs.out_shapes],
                  input_output_aliases={**({4: 0} if held else {}), **rs.aliases}, scratch_shapes=rs.scratch,
                  compiler_params=_params("arbitrary"))(x, g, dh, res, *held, *rs.arrays)
    return out[:2], rs.split(out[2:])


def _adamw_math(w, g, m, v):
    m = ADAM_B1 * m + (1.0 - ADAM_B1) * g
    v = ADAM_B2 * v + (1.0 - ADAM_B2) * (g * g)
    m_hat = m / (1.0 - ADAM_B1 ** ADAM_STEP)
    v_hat = v / (1.0 - ADAM_B2 ** ADAM_STEP)
    delta = -ADAM_LR * (m_hat / (jnp.sqrt(v_hat) + ADAM_EPS) + ADAM_WD * w)
    return delta, m, v


def _adamw(w, g, m, v, name):
    r, c = w.shape
    tr = 128

    def body(w_ref, g_ref, m_ref, v_ref, go_ref, d_ref, nm_ref, nv_ref):
        g = g_ref[...]
        go_ref[...] = g
        d_ref[...], nm_ref[...], nv_ref[...] = _adamw_math(w_ref[...], g, m_ref[...], v_ref[...])

    blk = pl.BlockSpec((tr, c), lambda i: (i, 0))
    return _pallas(body, name=name, grid=(r // tr,), in_specs=[blk] * 4, out_specs=[blk] * 4,
                   out_shape=[_sds((r, c), F32)] * 4, compiler_params=_params("parallel"))(w, g, m, v)


def _tables(s):
    half = HEAD_DIM // 2
    pos = jnp.arange(s, dtype=F32)
    inv_freq = ROPE_BASE ** (-jnp.arange(half, dtype=F32) / half)
    ang = pos[:, None] * inv_freq[None, :]
    cos, sin = jnp.cos(ang), jnp.sin(ang)
    cos_f = jnp.concatenate([cos, cos], axis=-1)
    sin_f = jnp.concatenate([-sin, sin], axis=-1)
    log_g = jnp.log1p(-jnp.exp2(-(5.0 + jnp.arange(HEADS, dtype=F32))))
    p = jnp.arange(CHUNK, dtype=F32)
    decay = jnp.exp(log_g[:, None, None] * jnp.abs(p[:, None] - p[None, :]))
    k_dec = jnp.exp(log_g[None, :] * (CHUNK - 1.0 - p)[:, None])
    q_dec = jnp.exp(log_g[None, :] * (p + 1.0)[:, None])
    c_dec = jnp.exp(log_g * CHUNK)
    k_dec = jnp.tile(jnp.broadcast_to(k_dec.T[:, :, None], (HEADS, CHUNK, HEAD_DIM)), (1, RET_BLOCK_CHUNKS, 1))
    q_dec = jnp.tile(jnp.broadcast_to(q_dec.T[:, :, None], (HEADS, CHUNK, HEAD_DIM)), (1, RET_BLOCK_CHUNKS, 1))
    c_dec = jnp.broadcast_to(c_dec[:, None, None], (HEADS, 1, HEAD_DIM))
    n = RET_SUB // CHUNK
    decay = (jnp.eye(n, dtype=F32)[None, :, None, :, None] * decay[:, None, :, None, :]).reshape(HEADS, RET_SUB, RET_SUB)
    return cos_f, sin_f, decay, k_dec, q_dec, c_dec


def _rot(x, cos_f, sin_f):
    return x * cos_f + pltpu.roll(x, HEAD_DIM // 2, 1) * sin_f


def _rot_bwd(d, cos_f, sin_f):
    return d * cos_f + pltpu.roll(d * sin_f, HEAD_DIM // 2, 1)


K_SCALE = HEAD_DIM ** -0.5


def _retention_fwd(proj, gn_g, tables, riders=()):
    s = proj.shape[0]
    nb = s // RET_ROWS
    nc = s // CHUNK
    cos_f, sin_f, decay, k_dec, q_dec, c_dec = tables

    def body(q_ref, k_ref, v_ref, g_ref, cos_ref, sin_ref, dec_ref, kd_ref, qd_ref, cd_ref, gn_ref,
             ret_ref, y_ref, prev_ref, state_ref):
        @pl.when(pl.program_id(1) == 0)
        def _():
            state_ref[...] = jnp.zeros_like(state_ref)

        cosv, sinv = cos_ref[...], sin_ref[...]
        q = _rot(q_ref[...], cosv, sinv)
        k = _rot(k_ref[...], cosv, sinv) * K_SCALE
        v = v_ref[...]
        rg = g_ref[...]
        dec, cd, gn = dec_ref[...], cd_ref[...], gn_ref[...]
        kdf, qdf = k * kd_ref[...], q * qd_ref[...]
        chunks = [slice(c * CHUNK, (c + 1) * CHUNK) for c in range(RET_BLOCK_CHUNKS)]
        contribs = [_dot(kdf[rows], v[rows], TN) for rows in chunks]
        state, states = state_ref[...], []
        for c in range(RET_BLOCK_CHUNKS):
            states.append(state)
            prev_ref[c] = state.astype(prev_ref.dtype)
            state = cd * state + contribs[c]
        state_ref[...] = state
        cross = jnp.concatenate([_dot(qdf[rows], st, NN) for rows, st in zip(chunks, states)], axis=0)
        intra = []
        for b in range(RET_ROWS // RET_SUB):
            rows = slice(b * RET_SUB, (b + 1) * RET_SUB)
            intra.append(_dot(_dot(q[rows], k[rows], NT) * dec, v[rows], NN))
        y = jnp.concatenate(intra, axis=0) + cross
        y_ref[...] = y
        mu = jnp.mean(y, axis=-1, keepdims=True)
        yc = y - mu
        var = jnp.mean(yc * yc, axis=-1, keepdims=True)
        yn = yc * lax.rsqrt(var + GN_EPS) * gn
        ret_ref[...] = (rg * jax.nn.sigmoid(rg) * yn).astype(ret_ref.dtype)

    def col(off):
        return pl.BlockSpec((RET_ROWS, HEAD_DIM), lambda h, i: (i, off + h))

    pos = pl.BlockSpec((RET_ROWS, HEAD_DIM), lambda h, i: (i, 0))
    per_head = lambda shape: pl.BlockSpec((None, *shape), lambda h, i: (h, 0, 0))
    rs = _Riders(riders, 11, 3)
    res = _pallas(
        _with_riders(body, 11, 3, 1, rs, (HEADS, nb)), name="retention_fwd", grid=(HEADS, nb),
        in_specs=[col(0), col(HEADS), col(2 * HEADS), col(3 * HEADS), pos, pos,
                  per_head((RET_SUB, RET_SUB)), per_head((RET_ROWS, HEAD_DIM)), per_head((RET_ROWS, HEAD_DIM)),
                  per_head((1, HEAD_DIM)), pl.BlockSpec((1, HEAD_DIM), lambda h, i: (0, h)), *rs.in_specs],
        out_specs=[col(0), col(0),
                   pl.BlockSpec((None, RET_BLOCK_CHUNKS, HEAD_DIM, HEAD_DIM), lambda h, i: (h, i, 0, 0)),
                   *rs.out_specs],
        out_shape=[_sds((s, 2 * HEADS * HEAD_DIM), BF16), _sds((s, HEADS * HEAD_DIM), F32),
                   _sds((HEADS, nc, HEAD_DIM, HEAD_DIM), MXU_DTYPE), *rs.out_shapes],
        input_output_aliases=rs.aliases,
        scratch_shapes=[pltpu.VMEM((HEAD_DIM, HEAD_DIM), F32), *rs.scratch],
        compiler_params=_params("arbitrary", "arbitrary"),
    )(proj, proj, proj, proj, cos_f, sin_f, decay, k_dec, q_dec, c_dec, gn_g, *rs.arrays)
    return res[:3], rs.split(res[3:])


def _retention_bwd(proj, gn_g, tables, y, prev, dmix, riders=()):
    s = proj.shape[0]
    nb = s // RET_ROWS
    cos_f, sin_f, decay, k_dec, q_dec, c_dec = tables

    def body(q_ref, k_ref, v_ref, g_ref, cos_ref, sin_ref, dec_ref, kd_ref, qd_ref, cd_ref, gn_ref,
             y_ref, prev_ref, dret_ref, dproj_ref, dgn_ref, gstate_ref, stage_ref, stage_sems):
        head, blk = pl.program_id(0), pl.program_id(1)
        step = head * nb + blk
        slot = step % 2

        def writes(sl):
            rows = pl.ds(pl.multiple_of((nb - 1 - blk) * RET_ROWS, RET_ROWS), RET_ROWS)
            return [pltpu.make_async_copy(
                stage_ref.at[sl, g], dproj_ref.at[rows, pl.ds(pl.multiple_of((g * HEADS + head) * HEAD_DIM, HEAD_DIM), HEAD_DIM)],
                stage_sems.at[sl, g]) for g in range(4)]

        @pl.when(step >= 2)
        def _():
            for cp in writes(slot):
                cp.wait()

        @pl.when(blk == 0)
        def _():
            gstate_ref[...] = jnp.zeros_like(gstate_ref)
            dgn_ref[...] = jnp.zeros_like(dgn_ref)

        cosv, sinv = cos_ref[...], sin_ref[...]
        q = _rot(q_ref[...], cosv, sinv)
        k = _rot(k_ref[...], cosv, sinv) * K_SCALE
        v = v_ref[...]
        dec, kd, qd, cd, gn = dec_ref[...], kd_ref[...], qd_ref[...], cd_ref[...], gn_ref[...]
        kdf, qdf = k * kd, q * qd
        rg = g_ref[...]
        yv = y_ref[...]
        dret = dret_ref[...]
        sig = jax.nn.sigmoid(rg)
        gate = rg * sig
        mu = jnp.mean(yv, axis=-1, keepdims=True)
        yc = yv - mu
        rstd = lax.rsqrt(jnp.mean(yc * yc, axis=-1, keepdims=True) + GN_EPS)
        z = yc * rstd
        dyn = dret * gate
        stage_ref[slot, 3] = (dret * (z * gn) * (sig * (1.0 + rg * (1.0 - sig)))).astype(stage_ref.dtype)
        dgn_ref[...] += jnp.sum(dyn * z, axis=0, keepdims=True)
        dz = dyn * gn
        dy = rstd * (dz - jnp.mean(dz, axis=-1, keepdims=True) - z * jnp.mean(dz * z, axis=-1, keepdims=True))
        chunks = [slice(c * CHUNK, (c + 1) * CHUNK) for c in range(RET_BLOCK_CHUNKS)]
        dprevs = [_dot(qdf[rows], dy[rows], TN) for rows in chunks]
        gst, gsts = gstate_ref[...], [None] * RET_BLOCK_CHUNKS
        for c in reversed(range(RET_BLOCK_CHUNKS)):
            gsts[c] = gst
            gst = dprevs[c] + cd * gst
        gstate_ref[...] = gst
        dq = jnp.concatenate([_dot(dy[rows], prev_ref[c], NT) for c, rows in enumerate(chunks)], axis=0) * qd
        dk = jnp.concatenate([_dot(v[rows], g, NT) for rows, g in zip(chunks, gsts)], axis=0) * kd
        dv = jnp.concatenate([_dot(kdf[rows], g, NN) for rows, g in zip(chunks, gsts)], axis=0)
        dqi, dki, dvi = [], [], []
        for b in range(RET_ROWS // RET_SUB):
            rows = slice(b * RET_SUB, (b + 1) * RET_SUB)
            qs, ks, vs, dys = q[rows], k[rows], v[rows], dy[rows]
            dvi.append(_dot(_dot(ks, qs, NT) * dec, dys, NN))
            dqi.append(_dot(_dot(dys, vs, NT) * dec, ks, NN))
            dki.append(_dot(_dot(vs, dys, NT) * dec, qs, NN))
        dq = dq + jnp.concatenate(dqi, axis=0)
        dk = dk + jnp.concatenate(dki, axis=0)
        dv = dv + jnp.concatenate(dvi, axis=0)
        stage_ref[slot, 0] = _rot_bwd(dq, cosv, sinv).astype(stage_ref.dtype)
        stage_ref[slot, 1] = _rot_bwd(dk * K_SCALE, cosv, sinv).astype(stage_ref.dtype)
        stage_ref[slot, 2] = dv.astype(stage_ref.dtype)
        for cp in writes(slot):
            cp.start()

        @pl.when(step == HEADS * nb - 1)
        def _():
            for cp in writes(1 - slot) + writes(slot):
                cp.wait()

    rev = lambda i: nb - 1 - i

    def col(off):
        return pl.BlockSpec((RET_ROWS, HEAD_DIM), lambda h, i: (rev(i), off + h))

    pos = pl.BlockSpec((RET_ROWS, HEAD_DIM), lambda h, i: (rev(i), 0))
    per_head = lambda shape: pl.BlockSpec((None, *shape), lambda h, i: (h, 0, 0))
    rs = _Riders(riders, 14, 2)
    res = _pallas(
        _with_riders(body, 14, 2, 3, rs, (HEADS, nb)), name="retention_bwd", grid=(HEADS, nb),
        in_specs=[col(0), col(HEADS), col(2 * HEADS), col(3 * HEADS), pos, pos,
                  per_head((RET_SUB, RET_SUB)), per_head((RET_ROWS, HEAD_DIM)), per_head((RET_ROWS, HEAD_DIM)),
                  per_head((1, HEAD_DIM)), pl.BlockSpec((1, HEAD_DIM), lambda h, i: (0, h)),
                  col(0), pl.BlockSpec((None, RET_BLOCK_CHUNKS, HEAD_DIM, HEAD_DIM), lambda h, i: (h, rev(i), 0, 0)),
                  col(0), *rs.in_specs],
        out_specs=[ANY, per_head((1, HEAD_DIM)), *rs.out_specs],
        out_shape=[_sds((s, proj.shape[1]), BF16), _sds((HEADS, 1, HEAD_DIM), F32), *rs.out_shapes],
        input_output_aliases=rs.aliases,
        scratch_shapes=[pltpu.VMEM((HEAD_DIM, HEAD_DIM), F32), pltpu.VMEM((2, 4, RET_ROWS, HEAD_DIM), BF16),
                        pltpu.SemaphoreType.DMA((2, 4)), *rs.scratch],
        compiler_params=_params("arbitrary", "arbitrary"),
    )(proj, proj, proj, proj, cos_f, sin_f, decay, k_dec, q_dec, c_dec, gn_g, y, prev, dmix, *rs.arrays)
    return res[:2], rs.split(res[2:])


ATT_COL0 = 4 * HEADS
PAD_ROWS = LEFT_CHUNKS * CHUNK
NORM_ROWS = 512
GROUP_CHUNKS = 4
GROUP = GROUP_CHUNKS * CHUNK
WIN = (LEFT_CHUNKS + GROUP_CHUNKS) * CHUNK
MASKED = -1e30


def _qk_norm(x, g):
    return x * lax.rsqrt(jnp.mean(x * x, axis=-1, keepdims=True) + EPS) * g


def _band_probs(qb, kb, bias, g):
    sc = _dot(qb, kb, NT) * K_SCALE + bias
    win_chunk = lax.broadcasted_iota(jnp.int32, (GROUP, WIN), 1) // CHUNK
    sc = jnp.where(g * GROUP_CHUNKS - LEFT_CHUNKS + win_chunk >= 0, sc, MASKED)
    e = jnp.exp(sc - jnp.max(sc, axis=-1, keepdims=True))
    return e / jnp.sum(e, axis=-1, keepdims=True)


def _with_riders(core, n_in, n_out, n_scratch, rs, grid):
    n_rin, n_rout = len(rs.arrays), len(rs.out_shapes)
    if not rs.riders:
        return core
    steps = 1
    for n in grid:
        steps *= n

    def body(*refs):
        outs_at = n_in + n_rin
        scratch_at = outs_at + n_out + n_rout
        bound = rs.bind(refs[n_in:outs_at], refs[outs_at + n_out:scratch_at], refs[scratch_at + n_scratch:])
        step = 0
        for axis, n in enumerate(grid):
            step = step * n + pl.program_id(axis)
        pl.when(step == 0)(lambda: rs.run("start", bound))
        pl.when(step == int(steps * RIDER_MID))(lambda: rs.run("mid", bound))
        pl.when(step == int(steps * RIDER_LATE))(lambda: rs.run("late", bound))
        core(*refs[:n_in], *refs[outs_at:outs_at + n_out], *refs[scratch_at:scratch_at + n_scratch])
        pl.when(step == steps - 1)(lambda: rs.run("end", bound))

    return body


def _attention_fwd(proj, gq, gk, bias, mix, riders=()):
    s = proj.shape[0]
    rs = _Riders(riders, 7, 1)

    def body(q_ref, k_ref, v_ref, gq_ref, gk_ref, bias_ref, mix_ref, o_ref, kp_ref, vp_ref):
        kp_ref[0:PAD_ROWS, :] = jnp.zeros((PAD_ROWS, HEAD_DIM), kp_ref.dtype)
        vp_ref[0:PAD_ROWS, :] = jnp.zeros((PAD_ROWS, HEAD_DIM), vp_ref.dtype)
        gqv, gkv = gq_ref[...], gk_ref[...]

        def fill(b, carry):
            r0 = pl.multiple_of(b * NORM_ROWS, NORM_ROWS)
            kp_ref[pl.ds(PAD_ROWS + r0, NORM_ROWS), :] = _qk_norm(k_ref[pl.ds(r0, NORM_ROWS), :], gkv).astype(kp_ref.dtype)
            vp_ref[pl.ds(PAD_ROWS + r0, NORM_ROWS), :] = v_ref[pl.ds(r0, NORM_ROWS), :].astype(vp_ref.dtype)
            return carry

        lax.fori_loop(0, s // NORM_ROWS, fill, 0)

        def group(g, carry):
            r0 = pl.multiple_of(g * GROUP, GROUP)
            qn = _qk_norm(q_ref[pl.ds(r0, GROUP), :], gqv)
            p = _band_probs(qn, kp_ref[pl.ds(r0, WIN), :], bias_ref[...], g)
            o_ref[pl.ds(r0, GROUP), :] = _dot(p, vp_ref[pl.ds(r0, WIN), :], NN).astype(o_ref.dtype)
            return carry

        lax.fori_loop(0, s // GROUP, group, 0, unroll=2)

    def col(off):
        return pl.BlockSpec((s, HEAD_DIM), lambda h: (0, off + h))

    vec = pl.BlockSpec((1, HEAD_DIM), lambda h: (0, 0))
    res = _pallas(
        _with_riders(body, 7, 1, 2, rs, (HEADS,)), name="attention_fwd", grid=(HEADS,),
        in_specs=[col(ATT_COL0), col(ATT_COL0 + HEADS), col(ATT_COL0 + 2 * HEADS), vec, vec,
                  pl.BlockSpec((None, GROUP, WIN), lambda h: (h, 0, 0)), ANY, *rs.in_specs],
        out_specs=[col(HEADS), *rs.out_specs], out_shape=[_sds(mix.shape, mix.dtype), *rs.out_shapes],
        input_output_aliases={6: 0, **rs.aliases},
        scratch_shapes=[pltpu.VMEM((s + PAD_ROWS, HEAD_DIM), MXU_DTYPE), pltpu.VMEM((s + PAD_ROWS, HEAD_DIM), MXU_DTYPE),
                        *rs.scratch],
        compiler_params=_params("arbitrary"),
    )(proj, proj, proj, gq, gk, bias, mix, *rs.arrays)
    return res[0], rs.split(res[1:])


def _attention_bwd(proj, gq, gk, bias, dmix, dproj, riders=()):
    s = proj.shape[0]
    rs = _Riders(riders, 8, 4)

    def body(q_ref, k_ref, v_ref, gq_ref, gk_ref, bias_ref, do_ref, dproj_in_ref,
             dproj_ref, dgq_ref, dgk_ref, dbias_ref, kp_ref, vp_ref, dkp_ref, dvp_ref, dqn_ref, stage_ref, stage_sems):
        head = pl.program_id(0)

        def writes():
            return [pltpu.make_async_copy(
                stage_ref.at[g],
                dproj_ref.at[:, pl.ds(pl.multiple_of((ATT_COL0 + g * HEADS + head) * HEAD_DIM, HEAD_DIM), HEAD_DIM)],
                stage_sems.at[g]) for g in range(3)]

        kp_ref[0:PAD_ROWS, :] = jnp.zeros((PAD_ROWS, HEAD_DIM), kp_ref.dtype)
        vp_ref[0:PAD_ROWS, :] = jnp.zeros((PAD_ROWS, HEAD_DIM), vp_ref.dtype)
        dkp_ref[...] = jnp.zeros_like(dkp_ref)
        dvp_ref[...] = jnp.zeros_like(dvp_ref)
        dbias_ref[...] = jnp.zeros_like(dbias_ref)
        gqv, gkv = gq_ref[...], gk_ref[...]

        def fill(b, carry):
            r0 = pl.multiple_of(b * NORM_ROWS, NORM_ROWS)
            kp_ref[pl.ds(PAD_ROWS + r0, NORM_ROWS), :] = _qk_norm(k_ref[pl.ds(r0, NORM_ROWS), :], gkv).astype(kp_ref.dtype)
            vp_ref[pl.ds(PAD_ROWS + r0, NORM_ROWS), :] = v_ref[pl.ds(r0, NORM_ROWS), :].astype(vp_ref.dtype)
            return carry

        lax.fori_loop(0, s // NORM_ROWS, fill, 0)

        def group(g, carry):
            r0 = pl.multiple_of(g * GROUP, GROUP)
            qn = _qk_norm(q_ref[pl.ds(r0, GROUP), :], gqv)
            kb = kp_ref[pl.ds(r0, WIN), :]
            vb = vp_ref[pl.ds(r0, WIN), :]
            p = _band_probs(qn, kb, bias_ref[...], g)
            do = do_ref[pl.ds(r0, GROUP), :]
            dvp_ref[pl.ds(r0, WIN), :] += _dot(p, do, TN)
            dp = _dot(do, vb, NT)
            ds = p * (dp - jnp.sum(dp * p, axis=-1, keepdims=True))
            dbias_ref[...] += ds
            dss = ds * K_SCALE
            dqn_ref[pl.ds(r0, GROUP), :] = _dot(dss, kb, NN)
            dkp_ref[pl.ds(r0, WIN), :] += _dot(dss, qn, TN)
            return carry

        lax.fori_loop(0, s // GROUP, group, 0, unroll=2)

        @pl.when(head == 0)
        def _():
            dgq_ref[...] = jnp.zeros_like(dgq_ref)
            dgk_ref[...] = jnp.zeros_like(dgk_ref)

        @pl.when(head > 0)
        def _():
            for cp in writes():
                cp.wait()

        def norm_bwd(x, g, dn):
            rstd = lax.rsqrt(jnp.mean(x * x, axis=-1, keepdims=True) + EPS)
            xh = x * rstd
            dxh = dn * g
            return rstd * (dxh - xh * jnp.mean(dxh * xh, axis=-1, keepdims=True)), jnp.sum(dn * xh, axis=0, keepdims=True)

        def finish(b, carry):
            r0 = pl.multiple_of(b * NORM_ROWS, NORM_ROWS)
            rows = pl.ds(r0, NORM_ROWS)
            dq, dgq = norm_bwd(q_ref[rows, :], gqv, dqn_ref[rows, :])
            dk, dgk = norm_bwd(k_ref[rows, :], gkv, dkp_ref[pl.ds(PAD_ROWS + r0, NORM_ROWS), :])
            stage_ref[0, rows, :] = dq.astype(stage_ref.dtype)
            stage_ref[1, rows, :] = dk.astype(stage_ref.dtype)
            stage_ref[2, rows, :] = dvp_ref[pl.ds(PAD_ROWS + r0, NORM_ROWS), :].astype(stage_ref.dtype)
            dgq_ref[...] += dgq
            dgk_ref[...] += dgk
            return carry

        lax.fori_loop(0, s // NORM_ROWS, finish, 0)
        for cp in writes():
            cp.start()

        @pl.when(head == HEADS - 1)
        def _():
            for cp in writes():
                cp.wait()

    def col(off):
        return pl.BlockSpec((s, HEAD_DIM), lambda h: (0, off + h))

    vec = pl.BlockSpec((1, HEAD_DIM), lambda h: (0, 0))
    hbias = pl.BlockSpec((None, GROUP, WIN), lambda h: (h, 0, 0))
    res = _pallas(
        _with_riders(body, 8, 4, 7, rs, (HEADS,)), name="attention_bwd", grid=(HEADS,),
        in_specs=[col(ATT_COL0), col(ATT_COL0 + HEADS), col(ATT_COL0 + 2 * HEADS), vec, vec, hbias, col(HEADS), ANY,
                  *rs.in_specs],
        out_specs=[ANY, vec, vec, hbias, *rs.out_specs],
        out_shape=[_sds(dproj.shape, dproj.dtype), _sds((1, HEAD_DIM), F32), _sds((1, HEAD_DIM), F32),
                   _sds((HEADS, GROUP, WIN), F32), *rs.out_shapes],
        input_output_aliases={7: 0, **rs.aliases},
        scratch_shapes=[pltpu.VMEM((s + PAD_ROWS, HEAD_DIM), MXU_DTYPE), pltpu.VMEM((s + PAD_ROWS, HEAD_DIM), MXU_DTYPE),
                        pltpu.VMEM((s + PAD_ROWS, HEAD_DIM), F32), pltpu.VMEM((s + PAD_ROWS, HEAD_DIM), F32),
                        pltpu.VMEM((s, HEAD_DIM), F32), pltpu.VMEM((3, s, HEAD_DIM), BF16),
                        pltpu.SemaphoreType.DMA((3,)), *rs.scratch],
        compiler_params=_params("arbitrary"),
    )(proj, proj, proj, gq, gk, bias, dmix, dproj, *rs.arrays)
    return res[:4], rs.split(res[4:])


DIAG_SPLIT = (BAND + WIN - CHUNK) // 2


def _diag_bin(m):
    t = jnp.where(m < DIAG_SPLIT, m, m - WIN)
    return jnp.clip(LEFT_CHUNKS * CHUNK - t, -(CHUNK - 1), REL_CLIP) + (CHUNK - 1)


def _skew_rows(a, left):
    row = lax.broadcasted_iota(jnp.int32, (GROUP, WIN), 0)
    for b in range(GROUP.bit_length() - 1):
        step = 1 << b
        a = jnp.where(jnp.bitwise_and(row, step) != 0, pltpu.roll(a, WIN - step if left else step, 1), a)
    return a


def _rel_bias_expand(rel_bias):
    def body(rb_ref, o_ref):
        h = pl.program_id(0)
        bins = _diag_bin(lax.broadcasted_iota(jnp.int32, (8, WIN), 1))
        per_diag = lax.fori_loop(0, REL_SIZE, lambda r, acc: jnp.where(bins == r, rb_ref[h, r], acc),
                                 jnp.zeros((8, WIN), F32))
        table = _skew_rows(jnp.broadcast_to(per_diag[0:1], (GROUP, WIN)), left=False)
        row_chunk = lax.broadcasted_iota(jnp.int32, (GROUP, WIN), 0) // CHUNK
        col_chunk = lax.broadcasted_iota(jnp.int32, (GROUP, WIN), 1) // CHUNK
        in_band = jnp.logical_and(col_chunk >= row_chunk, col_chunk <= row_chunk + LEFT_CHUNKS)
        o_ref[...] = jnp.where(in_band, table, MASKED)

    return _pallas(body, name="rel_bias_expand", grid=(HEADS,), in_specs=[pl.BlockSpec(memory_space=pltpu.SMEM)],
                   out_specs=pl.BlockSpec((None, GROUP, WIN), lambda h: (h, 0, 0)),
                   out_shape=_sds((HEADS, GROUP, WIN), F32), compiler_params=_params("parallel"))(rel_bias)


def _rel_bias_fold(dbias):
    def body(a_ref, o_ref):
        diag = jnp.sum(_skew_rows(a_ref[...], left=True), axis=0, keepdims=True)
        onehot = (_diag_bin(lax.broadcasted_iota(jnp.int32, (WIN, REL_SIZE), 0))
                  == lax.broadcasted_iota(jnp.int32, (WIN, REL_SIZE), 1)).astype(MXU_DTYPE)
        rest = jnp.broadcast_to(diag, (8, WIN))
        out = jnp.zeros((8, REL_SIZE), F32)
        for _ in range(3):
            piece = rest.astype(BF16)
            out = out + _dot(piece, onehot, NN)
            rest = rest - piece.astype(F32)
        o_ref[...] = out[0:1]

    return _pallas(body, name="rel_bias_fold", grid=(HEADS,),
                   in_specs=[pl.BlockSpec((None, GROUP, WIN), lambda h: (h, 0, 0))],
                   out_specs=pl.BlockSpec((None, 1, REL_SIZE), lambda h: (h, 0, 0)),
                   out_shape=_sds((HEADS, 1, REL_SIZE), F32), compiler_params=_params("parallel"))(dbias)


def _place():
    return lax.axis_index("x"), lax.axis_index("y"), lax.axis_index("c")


def _other_chips(x, y):
    return [(1 - x, y), (x, 1 - y), (1 - x, 1 - y)]


class _Rider:
    reads, ins, new, n_sems = (), (), (), 1

    def start(self, reads, ins, new, send, recv):
        pass

    def mid(self, reads, ins, new, send, recv):
        pass

    def late(self, reads, ins, new, send, recv):
        pass

    def end(self, reads, ins, new, send, recv):
        pass


class _Riders:
    def __init__(self, riders, n_host_in, n_host_out):
        self.riders = list(riders)
        self.arrays, self.out_shapes, self.aliases, self.scratch = [], [], {}, []
        for r in self.riders:
            for t, a in enumerate(r.ins):
                self.aliases[n_host_in + len(self.arrays) + len(r.reads) + t] = n_host_out + len(self.out_shapes) + t
            self.arrays += [*r.reads, *r.ins]
            self.out_shapes += [_sds(a.shape, a.dtype) for a in r.ins] + list(r.new)
            self.scratch += [pltpu.SemaphoreType.DMA((r.n_sems,)), pltpu.SemaphoreType.DMA((r.n_sems,))]
        self.in_specs = [ANY] * len(self.arrays)
        self.out_specs = [ANY] * len(self.out_shapes)

    def bind(self, in_refs, out_refs, scratch_refs):
        bound, i, o = [], 0, 0
        for t, r in enumerate(self.riders):
            reads = in_refs[i:i + len(r.reads)]
            i += len(r.reads) + len(r.ins)
            ins = out_refs[o:o + len(r.ins)]
            new = out_refs[o + len(r.ins):o + len(r.ins) + len(r.new)]
            o += len(r.ins) + len(r.new)
            bound.append((reads, ins, new, scratch_refs[2 * t], scratch_refs[2 * t + 1]))
        return bound

    def run(self, phase, bound):
        for r, b in zip(self.riders, bound):
            getattr(r, phase)(*b)

    def split(self, outs):
        res, o = [], 0
        for r in self.riders:
            n = len(r.ins) + len(r.new)
            res.append(list(outs[o:o + n]))
            o += n
        return res


def _run_riders(name, riders):
    rs = _Riders(riders, 0, 0)
    n_in, n_out = len(rs.arrays), len(rs.out_shapes)

    def body(*refs):
        bound = rs.bind(refs[:n_in], refs[n_in:n_in + n_out], refs[n_in + n_out:])
        rs.run("start", bound)
        rs.run("mid", bound)
        rs.run("late", bound)
        rs.run("end", bound)

    outs = _pallas(body, name=name, in_specs=rs.in_specs, out_specs=rs.out_specs, out_shape=rs.out_shapes,
                   input_output_aliases=rs.aliases, scratch_shapes=rs.scratch)(*rs.arrays)
    return rs.split(outs)


class _GatherRider(_Rider):
    X_LINK, Y_LINK, Y_PASS, X_PASS, D2D_X, D2D_Y, D2D_DIAG, N_SEMS = 0, 1, 2, 3, 4, 5, 6, 7

    def __init__(self, blocks, part=(0, 1, 1)):
        self.ins = tuple(blocks)
        self.part = part
        self.n_sems = self.N_SEMS * len(blocks)

    def _copy(self, out, send, recv, w, sem, chip_from, cc, to, sub=None):
        hr = self.ins[w].shape[1] // 2
        lo, hi, n = self.part
        first, size = cc * hr + lo * (hr // n), (hi - lo) * (hr // n)
        if sub is not None:
            size //= 2
            first += sub * size
        piece = out[w].at[2 * chip_from[0] + chip_from[1], pl.ds(first, size), :]
        return pltpu.make_async_remote_copy(src_ref=piece, dst_ref=piece, send_sem=send.at[self.N_SEMS * w + sem],
                                            recv_sem=recv.at[self.N_SEMS * w + sem], device_id=to, device_id_type=MESH)

    def _sent(self, out, send, recv, w):
        x, y, c = _place()
        me, sib = (x, y), (x, y, 1 - c)
        xn, yn, diag = _other_chips(x, y)
        cp = functools.partial(self._copy, out, send, recv, w)
        return [("start", cp(self.X_LINK, me, c, (*xn, c))), ("start", cp(self.Y_LINK, me, c, (*yn, c))),
                ("mid_x", cp(self.D2D_X, xn, c, sib)), ("mid_x", cp(self.Y_PASS, xn, c, (*yn, c), sub=0)),
                ("mid_y", cp(self.D2D_Y, yn, c, sib)), ("mid_y", cp(self.X_PASS, yn, c, (*xn, c), sub=1)),
                ("late", cp(self.D2D_DIAG, diag, c, sib))]

    def _go(self, out, send, recv, phase):
        for w in range(len(self.ins)):
            for ph, copy in self._sent(out, send, recv, w):
                if ph == phase:
                    copy.start()

    def start(self, reads, out, new, send, recv):
        self._go(out, send, recv, "start")

    def mid(self, reads, out, new, send, recv):
        x, y, c = _place()
        xn, yn, _ = _other_chips(x, y)
        for w in range(len(self.ins)):
            self._copy(out, send, recv, w, self.X_LINK, xn, c, (x, y, c)).wait_recv()
        self._go(out, send, recv, "mid_x")
        for w in range(len(self.ins)):
            self._copy(out, send, recv, w, self.Y_LINK, yn, c, (x, y, c)).wait_recv()
        self._go(out, send, recv, "mid_y")

    def late(self, reads, out, new, send, recv):
        x, y, c = _place()
        diag = _other_chips(x, y)[2]
        for w in range(len(self.ins)):
            self._copy(out, send, recv, w, self.Y_PASS, diag, c, (x, y, c), sub=0).wait_recv()
            self._copy(out, send, recv, w, self.X_PASS, diag, c, (x, y, c), sub=1).wait_recv()
        self._go(out, send, recv, "late")

    def end(self, reads, out, new, send, recv):
        x, y, c = _place()
        xn, yn, diag = _other_chips(x, y)
        for w in range(len(self.ins)):
            for sem, chip in ((self.D2D_X, xn), (self.D2D_Y, yn), (self.D2D_DIAG, diag)):
                self._copy(out, send, recv, w, sem, chip, 1 - c, (x, y, c)).wait_recv()
        for w in range(len(self.ins)):
            for _, copy in self._sent(out, send, recv, w):
                copy.wait_send()


class _SwapRider(_Rider):
    def __init__(self, grads):
        self.reads = tuple(grads)
        self.new = tuple(_sds((N_CHIPS, g.shape[1] // 2, g.shape[2]), g.dtype) for g in grads)
        self.n_sems = len(grads)

    def _copies(self, src, new, send, recv):
        x, y, c = _place()
        copies = []
        for w in range(len(self.reads)):
            hr = self.reads[w].shape[1] // 2
            copies.append(pltpu.make_async_remote_copy(
                src_ref=src[w].at[:, pl.ds((1 - c) * hr, hr), :], dst_ref=new[w],
                send_sem=send.at[w], recv_sem=recv.at[w], device_id=(x, y, 1 - c), device_id_type=MESH))
        return copies

    def start(self, src, ins, new, send, recv):
        for cp in self._copies(src, new, send, recv):
            cp.start()

    def end(self, src, ins, new, send, recv):
        for cp in self._copies(src, new, send, recv):
            cp.wait()


def _add_half(g, got, c_arr, name):
    nk, r, cols = g.shape
    hr = r // 2
    tr = min(hr, 256)
    nb = hr // tr

    def body(c_ref, g_ref, got_ref, o_ref):
        o_ref[...] = (g_ref[...].astype(F32) + got_ref[...].astype(F32)).astype(o_ref.dtype)

    grid_spec = pltpu.PrefetchScalarGridSpec(
        num_scalar_prefetch=1, grid=(nk, nb),
        in_specs=[pl.BlockSpec((None, tr, cols), lambda k, i, c_ref: (k, c_ref[0] * nb + i, 0)),
                  pl.BlockSpec((None, tr, cols), lambda k, i, c_ref: (k, i, 0))],
        out_specs=pl.BlockSpec((None, tr, cols), lambda k, i, c_ref: (k, i, 0)))
    return _pallas(body, name=name, grid_spec=grid_spec, out_shape=_sds((nk, hr, cols), g.dtype),
                   compiler_params=_params("parallel", "parallel"))(c_arr, g, got)


class _SendPartialsRider(_Rider):
    def __init__(self, parts, got=None, part=(0, 1, 1)):
        self.reads = tuple(parts)
        if got is None:
            self.new = tuple(_sds((N_CHIPS - 1, *p.shape[1:]), p.dtype) for p in parts)
        else:
            self.ins = tuple(got)
        self.part = part
        self.n_sems = 3 * len(parts)

    def _copies(self, src, ins, new, send, recv):
        x, y, c = _place()
        land = ins if self.ins else new
        lo, hi, n = self.part
        copies = []
        for w in range(len(self.reads)):
            pr = self.reads[w].shape[1] // n
            rows = pl.ds(lo * pr, (hi - lo) * pr)
            for j, chip in enumerate(_other_chips(x, y)):
                copies.append(pltpu.make_async_remote_copy(
                    src_ref=src[w].at[2 * chip[0] + chip[1], rows, :], dst_ref=land[w].at[j, rows, :],
                    send_sem=send.at[3 * w + j], recv_sem=recv.at[3 * w + j], device_id=(*chip, c), device_id_type=MESH))
        return copies

    def start(self, src, ins, new, send, recv):
        for cp in self._copies(src, ins, new, send, recv):
            cp.start()

    def end(self, src, ins, new, send, recv):
        for cp in self._copies(src, ins, new, send, recv):
            cp.wait()


def _sum_partials(part, got, kc_arr, name):
    _, hr, cols = part.shape
    tr = min(hr, 256)
    nb = hr // tr

    def body(kc_ref, p_ref, g0_ref, g1_ref, g2_ref, o_ref):
        o_ref[...] = ((p_ref[...].astype(F32) + g0_ref[...].astype(F32)) + g1_ref[...].astype(F32)) + g2_ref[...].astype(F32)

    slot = lambda j: pl.BlockSpec((None, tr, cols), lambda i, kc_ref: (j, i, 0))
    grid_spec = pltpu.PrefetchScalarGridSpec(
        num_scalar_prefetch=1, grid=(nb,),
        in_specs=[pl.BlockSpec((None, tr, cols), lambda i, kc_ref: (kc_ref[0], i, 0)), slot(0), slot(1), slot(2)],
        out_specs=pl.BlockSpec((tr, cols), lambda i, kc_ref: (kc_ref[1] * nb + i, 0)))
    return _pallas(body, name=name, grid_spec=grid_spec, out_shape=_sds((2 * hr, cols), F32),
                   compiler_params=_params("parallel"))(kc_arr, part, got, got, got)


class _ShareRider(_Rider):
    def __init__(self, grads):
        self.ins = tuple(grads)
        self.n_sems = len(grads)

    def _copies(self, out, send, recv):
        x, y, c = _place()
        copies = []
        for w in range(len(self.ins)):
            hr = self.ins[w].shape[0] // 2
            mine = out[w].at[pl.ds(c * hr, hr), :]
            copies.append(pltpu.make_async_remote_copy(
                src_ref=mine, dst_ref=mine, send_sem=send.at[w], recv_sem=recv.at[w],
                device_id=(x, y, 1 - c), device_id_type=MESH))
        return copies

    def start(self, reads, out, new, send, recv):
        for cp in self._copies(out, send, recv):
            cp.start()

    def end(self, reads, out, new, send, recv):
        for cp in self._copies(out, send, recv):
            cp.wait()


def _small_allreduce_adamw(g_part, w, m, v):
    rows = g_part.shape[0]

    def body(g_ref, w_ref, m_ref, v_ref, go_ref, d_ref, nm_ref, nv_ref, all_ref, send_sems, recv_sems):
        x, y, c = _place()
        me = 4 * x + 2 * y + c
        all_ref[me] = g_ref[...]
        copies = []
        for r in range(1, 8):
            dx, dy, dc = (r >> 2) & 1, (r >> 1) & 1, r & 1
            peer = (1 - x if dx else x, 1 - y if dy else y, 1 - c if dc else c)
            copies.append(pltpu.make_async_remote_copy(
                src_ref=g_ref, dst_ref=all_ref.at[me], send_sem=send_sems.at[r - 1], recv_sem=recv_sems.at[r - 1],
                device_id=peer, device_id_type=MESH))
        for cp in copies:
            cp.start()
        for cp in copies:
            cp.wait()
        tot = all_ref[0]
        for d in range(1, 8):
            tot = tot + all_ref[d]
        go_ref[...] = tot
        d_ref[...], nm_ref[...], nv_ref[...] = _adamw_math(w_ref[...], tot, m_ref[...], v_ref[...])

    vm = pl.BlockSpec(memory_space=pltpu.VMEM)
    return _pallas(
        body, name="small_allreduce_adamw", in_specs=[vm] * 4, out_specs=[vm] * 4,
        out_shape=[_sds((rows, 128), F32)] * 4,
        scratch_shapes=[pltpu.VMEM((8, rows, 128), F32), pltpu.SemaphoreType.DMA((7,)), pltpu.SemaphoreType.DMA((7,))],
    )(g_part, w, m, v)


SMALL_SIZES = (2048, 1024, 128, 128, HEADS * REL_SIZE, 2048)
SMALL_PART_ROWS = tuple(-(-size // 1024) * 8 for size in SMALL_SIZES)
SMALL_ROWS = sum(SMALL_PART_ROWS)


def _pack_small(parts):
    rows = []
    for p, size, nr in zip(parts, SMALL_SIZES, SMALL_PART_ROWS):
        rows.append(jnp.pad(p.reshape(-1), (0, nr * 128 - size)).reshape(nr, 128))
    return jnp.concatenate(rows, axis=0)


def _unpack_small(slab, shapes):
    out, off = [], 0
    for size, nr, shape in zip(SMALL_SIZES, SMALL_PART_ROWS, shapes):
        out.append(slab[off:off + nr].reshape(-1)[:size].reshape(shape))
        off += nr
    return out


def kernel(x, norm1_g, w_in, ret_norm_g, q_norm_g, k_norm_g, rel_bias, w_out, norm2_g, w_ff1, w_ff2, loss_target, m_norm1_g, m_w_in, m_ret_norm_g, m_q_norm_g, m_k_norm_g, m_rel_bias, m_w_out, m_norm2_g, m_w_ff1, m_w_ff2, v_norm1_g, v_w_in, v_ret_norm_g, v_q_norm_g, v_k_norm_g, v_rel_bias, v_w_out, v_norm2_g, v_w_ff1, v_w_ff2):
    xs = x[0]
    tgt = loss_target[0]
    s, d = xs.shape
    d_in = N_CHIPS * w_in.shape[2]
    d_ff = N_CHIPS * w_ff1.shape[2]
    in_sh, ff_sh = w_in.shape[2], w_ff1.shape[2]
    tm = min(s, 1024)
    gi = s // tm
    c_arr = lax.axis_index("c").astype(jnp.int32).reshape(1)
    k_arr = (2 * lax.axis_index("x") + lax.axis_index("y")).astype(jnp.int32).reshape(1)
    tables = _tables(s)
    bias = _rel_bias_expand(rel_bias[0])

    blk_in = [_cast_bf16(w_in[0], k_arr, "cast_w_in_%d" % half, cols=(half, 2)) for half in range(2)]
    blk_out, blk_ff1, blk_ff2 = (_cast_bf16(w_out[0], k_arr, "cast_w_out"), _cast_bf16(w_ff1[0], k_arr, "cast_w_ff1"),
                                 _cast_bf16(w_ff2[0], k_arr, "cast_w_ff2"))
    ((wg_in0,),) = _run_riders("all_gather_w_in_0", [_GatherRider([blk_in[0]])])

    h1 = _rmsnorm_fwd(xs, norm1_g, "rmsnorm1")
    tn_in = in_sh // 2
    tk = d

    def proj_half(half, wg, through, riders):
        return _mm("proj_%d" % half, h1, wg, NN, (gi, N_CHIPS, 1),
                   pl.BlockSpec((tm, tk), lambda i, j, k: (i, 0)), pl.BlockSpec((None, tk, tn_in), lambda i, j, k: (j, 0, 0)),
                   [_sds((s, d_in), F32)], [pl.BlockSpec((tm, tn_in), lambda i, j, k: (i, 2 * j + half))], (tm, tn_in),
                   riders=riders, through=through)

    (proj,), ((wg_in1,),) = proj_half(0, wg_in0, None, [_GatherRider([blk_in[1]])])
    (proj,), ((wg_ff1,),) = proj_half(1, wg_in1, proj, [_GatherRider([blk_ff1], (0, 3, 8))])
    (mix, y_ret, prev), ((wg_ff1,),) = _retention_fwd(proj, ret_norm_g, tables, riders=[_GatherRider([wg_ff1], (3, 6, 8))])
    mix, ((wg_out,), (wg_ff2,)) = _attention_fwd(
        proj, q_norm_g, k_norm_g, bias, mix, riders=[_GatherRider([blk_out]), _GatherRider([blk_ff2], (0, 1, 4))])
    wg_out = wg_out.reshape(d, d)
    tn = 1024
    tile = pl.BlockSpec((tm, tn), lambda i, j, k: (i, j))
    def residual_norm(acc, res, g):
        x1v = res + acc
        yv = x1v * lax.rsqrt(jnp.mean(x1v * x1v, axis=-1, keepdims=True) + EPS)
        return x1v, yv * g

    tmo = min(s, 512)
    rows = pl.BlockSpec((tmo, d), lambda i, j, k: (i, 0))
    (x1, h2), ((wg_ff1,),) = _mm(
        "out_proj", mix, wg_out, NN, (s // tmo, 1, 1),
        rows, pl.BlockSpec((d, d), lambda i, j, k: (0, 0)),
        [_sds((s, d), F32), _sds((s, d), BF16)], [rows, rows], (tmo, d),
        extras=(xs, norm2_g), extra_specs=(rows, pl.BlockSpec((1, d), lambda i, j, k: (0, 0))),
        epi=residual_norm, riders=[_GatherRider([wg_ff1], (6, 8, 8))])
    tn_ff = min(ff_sh, 1024)
    per = ff_sh // tn_ff

    def relu2(acc):
        r = jnp.maximum(acc, 0.0)
        return acc, r * r

    (u, act), ((wg_ff2,),) = _mm(
        "ff1", h2, wg_ff1, NN, (gi, N_CHIPS * per, d // tk),
        pl.BlockSpec((tm, tk), lambda i, j, k: (i, k)),
        pl.BlockSpec((None, tk, tn_ff), lambda i, j, k: (j // per, k, j % per)),
        [_sds((s, d_ff), F32), _sds((s, d_ff), BF16)],
        [pl.BlockSpec((tm, tn_ff), lambda i, j, k: (i, j))] * 2, (tm, tn_ff), epi=relu2,
        riders=[_GatherRider([wg_ff2], (1, 4, 4))])
    wg_ff2 = wg_ff2.reshape(d_ff, d)

    def loss_epi(acc, res, t):
        diff = (res + acc) - t
        dy = diff / d
        return dy, dy, jnp.sum(diff * diff, axis=0, keepdims=True)

    tk2 = min(tk, 2048)
    dy, dyb, loss_cols = _mm(
        "ff2_loss", act, wg_ff2, NN, (gi, d // tn, d_ff // tk2),
        pl.BlockSpec((tm, tk2), lambda i, j, k: (i, k)), pl.BlockSpec((tk2, tn), lambda i, j, k: (k, j)),
        [_sds((s, d), F32), _sds((s, d), BF16), _sds((gi, 1, d), F32)],
        [tile, tile, pl.BlockSpec((None, 1, tn), lambda i, j, k: (i, 0, j))], (tm, tn),
        extras=(x1, tgt), extra_specs=(tile, tile), epi=loss_epi)
    loss = lax.psum(0.5 * jnp.sum(loss_cols) / d, ("x", "y", "c"))

    (du,) = _mm("d_act", dyb, wg_ff2, NT, (gi, d_ff // tn, d // tk),
                pl.BlockSpec((tm, tk), lambda i, j, k: (i, k)), pl.BlockSpec((tn, tk), lambda i, j, k: (j, k)),
                [_sds((s, d_ff), BF16)], [tile], (tm, tn), extras=(u,), extra_specs=(tile,),
                epi=lambda acc, uu: (acc * (2.0 * jnp.maximum(uu, 0.0)),))
    ts = min(s, 2048)
    wtile = pl.BlockSpec((tn, tn), lambda i, j, k: (i, j))
    (g_ff2,) = _mm("dw_ff2", act, dyb, TN, (d_ff // tn, d // tn, s // ts),
                   pl.BlockSpec((ts, tn), lambda i, j, k: (k, i)), pl.BlockSpec((ts, tn), lambda i, j, k: (k, j)),
                   [_sds((d_ff, d), BF16)], [wtile], (tn, tn))
    g_ff2 = g_ff2.reshape(N_CHIPS, d_ff // N_CHIPS, d)
    (g_ff1,), ((got_ff2,),) = _mm(
        "dw_ff1", h2, du, TN, (d // tn, N_CHIPS * per, s // ts),
        pl.BlockSpec((ts, tn), lambda i, j, k: (k, i)), pl.BlockSpec((ts, tn_ff), lambda i, j, k: (k, j)),
        [_sds((N_CHIPS, d, ff_sh), BF16)],
        [pl.BlockSpec((None, tn, tn_ff), lambda i, j, k: (j // per, i, j % per))], (tn, tn_ff),
        riders=[_SwapRider([g_ff2])])
    p_ff2 = _add_half(g_ff2, got_ff2, c_arr, "chip_partial_w_ff2")
    tkf = min(tk, ff_sh)
    kper = ff_sh // tkf
    (dh2,), ((got2_ff2,), (got_ff1,)) = _mm(
        "d_h2", du, wg_ff1, NT, (gi, d // tn, d_ff // tkf),
        pl.BlockSpec((tm, tkf), lambda i, j, k: (i, k)),
        pl.BlockSpec((None, tn, tkf), lambda i, j, k: (k // kper, j, k % kper)),
        [_sds((s, d), F32)], [tile], (tm, tn),
        riders=[_SendPartialsRider([p_ff2], part=(0, 3, 4)), _SwapRider([g_ff1])])
    p_ff1 = _add_half(g_ff1, got_ff1, c_arr, "chip_partial_w_ff1")
    dx1, dx1b, g_norm2 = _rmsnorm_bwd(x1, norm2_g, dh2, dy, "rmsnorm2_bwd")

    (dmix,) = _mm("d_mix", dx1b, wg_out, NT, (gi, d // tn, d // tk),
                  pl.BlockSpec((tm, tk), lambda i, j, k: (i, k)), pl.BlockSpec((tn, tk), lambda i, j, k: (j, k)),
                  [_sds((s, d), F32)], [tile], (tm, tn))
    (g_out,) = _mm("dw_out", mix, dx1b, TN, (d // tn, d // tn, s // ts),
                   pl.BlockSpec((ts, tn), lambda i, j, k: (k, i)), pl.BlockSpec((ts, tn), lambda i, j, k: (k, j)),
                   [_sds((d, d), BF16)], [wtile], (tn, tn))
    g_out = g_out.reshape(N_CHIPS, d // N_CHIPS, d)
    (dproj, g_gn), ((got2_ff2,), (got2_ff1,), (got_out,)) = _retention_bwd(
        proj, ret_norm_g, tables, y_ret, prev, dmix,
        riders=[_SendPartialsRider([p_ff2], got=[got2_ff2], part=(3, 4, 4)), _SendPartialsRider([p_ff1], part=(0, 2, 4)),
                _SwapRider([g_out])])
    p_out = _add_half(g_out, got_out, c_arr, "chip_partial_w_out")
    (dproj, g_gq, g_gk, dbias), ((got2_ff1,), (got2_out,)) = _attention_bwd(
        proj, q_norm_g, k_norm_g, bias, dmix, dproj,
        riders=[_SendPartialsRider([p_ff1], got=[got2_ff1], part=(2, 4, 4)), _SendPartialsRider([p_out])])
    g_rel = _rel_bias_fold(dbias)
    names = ["w_in", "w_out", "w_ff1", "w_ff2"]
    kc_arr = jnp.concatenate([k_arr, c_arr])
    early = [_sum_partials(p, r, kc_arr, "sum_partials_" + nm)
             for p, r, nm in zip((p_out, p_ff1, p_ff2), (got2_out, got2_ff1, got2_ff2), names[1:])]
    (g_in,), (early,) = _mm(
        "dw_in", h1, dproj, TN, (d // tn, 2 * N_CHIPS, s // ts),
        pl.BlockSpec((ts, tn), lambda i, j, k: (k, i)), pl.BlockSpec((ts, tn_in), lambda i, j, k: (k, j)),
        [_sds((N_CHIPS, d, in_sh), BF16)],
        [pl.BlockSpec((None, tn, tn_in), lambda i, j, k: (j // 2, i, j % 2))], (tn, tn_in), riders=[_ShareRider(early)])
    ((got_in,),) = _run_riders("grad_swap_w_in", [_SwapRider([g_in])])
    p_in = _add_half(g_in, got_in, c_arr, "chip_partial_w_in")
    half_spec = pl.BlockSpec((None, tn, tn_in), lambda i, j, k: (k // 2, j, 0))
    (dh1,), ((got2_in,),) = _mm(
        "d_h1", dproj, [wg_in0, wg_in1], NT, (gi, d // tn, 2 * N_CHIPS),
        pl.BlockSpec((tm, tn_in), lambda i, j, k: (i, k)), [half_spec, half_spec],
        [_sds((s, d), F32)], [tile], (tm, tn), riders=[_SendPartialsRider([p_in], part=(0, 13, 16))])
    first_rows = 3 * s // 4
    part_a, ((got2_in,),) = _rmsnorm_bwd_rows(
        xs, norm1_g, dh1, dx1, "rmsnorm1_bwd_a", 0, first_rows,
        riders=[_SendPartialsRider([p_in], got=[got2_in], part=(13, 16, 16))])
    (grad_x, g_norm1), _ = _rmsnorm_bwd_rows(xs, norm1_g, dh1, dx1, "rmsnorm1_bwd_b", first_rows, s - first_rows, prev=part_a)
    ((g_w_in,),) = _run_riders("grad_share_w_in", [_ShareRider([_sum_partials(p_in, got2_in, kc_arr, "sum_partials_w_in")])])
    g_big = [g_w_in, *early]
    big = []
    for g, w, m, v, nm in zip(g_big, (w_in, w_out, w_ff1, w_ff2), (m_w_in, m_w_out, m_w_ff1, m_w_ff2),
                              (v_w_in, v_w_out, v_w_ff1, v_w_ff2), names):
        g, delta, new_m, new_v = _adamw(w[0], g, m[0], v[0], "adamw_" + nm)
        big.append((g[None], delta[None], new_m[None], new_v[None]))

    small_w = (norm1_g, ret_norm_g, q_norm_g, k_norm_g, rel_bias, norm2_g)
    small_m = (m_norm1_g, m_ret_norm_g, m_q_norm_g, m_k_norm_g, m_rel_bias, m_norm2_g)
    small_v = (v_norm1_g, v_ret_norm_g, v_q_norm_g, v_k_norm_g, v_rel_bias, v_norm2_g)
    shapes = [p.shape for p in small_w]
    g_small = _pack_small([g_norm1, g_gn, g_gq, g_gk, g_rel, g_norm2])
    sg, sd, sm, sv = (_unpack_small(a, shapes) for a in _small_allreduce_adamw(
        g_small, _pack_small(small_w), _pack_small(small_m), _pack_small(small_v)))

    def ordered(kind):
        sm_ = (sg, sd, sm, sv)[kind]
        return (sm_[0], big[0][kind], sm_[1], sm_[2], sm_[3], sm_[4], big[1][kind], sm_[5], big[2][kind], big[3][kind])

    return (loss, grad_x[None], *ordered(0), *ordered(1), *ordered(2), *ordered(3))
```

```python
import functools

import jax
import jax.numpy as jnp
from jax import lax
from jax.experimental import pallas as pl
from jax.experimental.pallas import tpu as pltpu

F32 = jnp.float32
BF16 = jnp.bfloat16
MXU_DTYPE = jnp.bfloat16

CHUNK = 64
HEADS = 8
HEAD_DIM = 128
LEFT_CHUNKS = 8
BAND = (LEFT_CHUNKS + 1) * CHUNK
REL_CLIP = 128
REL_SIZE = (CHUNK - 1) + REL_CLIP + 1
RET_BLOCK_CHUNKS = 8
RET_ROWS = RET_BLOCK_CHUNKS * CHUNK
RET_SUB = 256
ROPE_BASE = 10000.0
EPS = 1e-6
GN_EPS = 1e-5
ADAM_LR, ADAM_B1, ADAM_B2, ADAM_EPS, ADAM_WD, ADAM_STEP = 0.001, 0.9, 0.999, 1e-08, 0.01, 10
N_CHIPS = 4
VMEM_LIMIT = 56 * 1024 * 1024
MESH = pl.DeviceIdType.MESH
ANY = pl.BlockSpec(memory_space=pl.ANY)

NN = (((1,), (0,)), ((), ()))
NT = (((1,), (1,)), ((), ()))
TN = (((0,), (0,)), ((), ()))


def _pallas(body, **kw):
    return pl.pallas_call(body, **kw)


def _params(*sem):
    return pltpu.CompilerParams(dimension_semantics=sem, vmem_limit_bytes=VMEM_LIMIT)


def _dot(a, b, dims):
    return lax.dot_general(a.astype(MXU_DTYPE), b.astype(MXU_DTYPE), dims, preferred_element_type=F32)


RIDER_MID, RIDER_LATE = 0.5, 0.8


def _mm(name, a, b, dims, grid, a_spec, b_spec, outs, o_specs, acc_shape, extras=(), extra_specs=(), epi=None,
        riders=(), through=None):
    ni, nj, nk = grid
    n_ex, n_out = len(extras), len(outs)
    bs = list(b) if isinstance(b, (list, tuple)) else [b]
    b_specs = list(b_spec) if isinstance(b, (list, tuple)) else [b_spec]
    extras, extra_specs = (*bs[1:], *extras), (*b_specs[1:], *extra_specs)
    b, b_spec, n_b = bs[0], b_specs[0], len(bs)
    n_in = 1 + n_b + n_ex
    rs = _Riders(riders, n_in, n_out)
    n_rin, n_rout = len(rs.arrays), len(rs.out_shapes)
    steps = ni * nj * nk

    held = [] if through is None else [through]

    def body(*refs):
        a_ref, b_refs = refs[0], refs[1:1 + n_b]
        b_ref = b_refs[0]
        ex_refs = refs[1 + n_b:n_in]
        outs_at = n_in + n_rin + len(held)
        o_refs = refs[outs_at:outs_at + n_out]
        acc_ref = refs[outs_at + n_out + n_rout]
        k = pl.program_id(2)
        if riders:
            bound = rs.bind(refs[n_in:n_in + n_rin], refs[outs_at + n_out:outs_at + n_out + n_rout],
                            refs[outs_at + n_out + n_rout + 1:])
            step = (pl.program_id(0) * nj + pl.program_id(1)) * nk + k
            pl.when(step == 0)(lambda: rs.run("start", bound))
            pl.when(step == int(steps * RIDER_MID))(lambda: rs.run("mid", bound))
            pl.when(step == int(steps * RIDER_LATE))(lambda: rs.run("late", bound))

        def finish(acc):
            vals = epi(acc, *[r[...] for r in ex_refs]) if epi is not None else (acc,)
            for r, v in zip(o_refs, vals):
                r[...] = v.astype(r.dtype)

        if nk == 1:
            finish(_dot(a_ref[...], b_ref[...], dims))
        else:
            @pl.when(k == 0)
            def _():
                acc_ref[...] = jnp.zeros_like(acc_ref)

            for t, ref in enumerate(b_refs):
                def step_with(ref=ref):
                    acc_ref[...] += _dot(a_ref[...], ref[...], dims)

                if n_b == 1:
                    step_with()
                else:
                    pl.when(k % n_b == t)(step_with)
            pl.when(k == nk - 1)(lambda: finish(acc_ref[...]))

        if riders:
            pl.when(step == steps - 1)(lambda: rs.run("end", bound))

    res = _pallas(
        body, name=name, grid=grid, in_specs=[a_spec, b_spec, *extra_specs, *rs.in_specs, *[ANY for _ in held]],
        out_specs=[*o_specs, *rs.out_specs], out_shape=[*outs, *rs.out_shapes],
        input_output_aliases={**rs.aliases, **{n_in + n_rin: 0 for _ in held}},
        scratch_shapes=[pltpu.VMEM(acc_shape if nk > 1 else (8, 128), F32), *rs.scratch],
        compiler_params=_params(*(("arbitrary",) * 3 if riders else ("parallel", "parallel", "arbitrary"))),
    )(a, b, *extras, *rs.arrays, *held)
    return (res[:n_out], rs.split(res[n_out:])) if riders else res


def _sds(shape, dtype):
    return jax.ShapeDtypeStruct(shape, dtype)


def _cast_bf16(w, k_arr, name, cols=(0, 1)):
    r, c = w.shape[0], w.shape[1] // cols[1]
    tr = min(r, 512)

    def body(k_ref, w_ref, o_ref):
        o_ref[...] = w_ref[...].astype(BF16)

    grid_spec = pltpu.PrefetchScalarGridSpec(
        num_scalar_prefetch=1, grid=(r // tr,), in_specs=[pl.BlockSpec((tr, c), lambda i, k_ref: (i, cols[0]))],
        out_specs=pl.BlockSpec((None, tr, c), lambda i, k_ref: (k_ref[0], i, 0)))
    return _pallas(body, name=name, grid_spec=grid_spec, out_shape=_sds((N_CHIPS, r, c), BF16),
                   compiler_params=_params("parallel"))(k_arr, w)


def _rmsnorm_fwd(x, g, name):
    s, d = x.shape
    tr = 512

    def body(x_ref, g_ref, o_ref):
        xv = x_ref[...]
        y = xv * lax.rsqrt(jnp.mean(xv * xv, axis=-1, keepdims=True) + EPS)
        o_ref[...] = (y * g_ref[...]).astype(o_ref.dtype)

    return _pallas(body, name=name, grid=(s // tr,),
                   in_specs=[pl.BlockSpec((tr, d), lambda i: (i, 0)), pl.BlockSpec((1, d), lambda i: (0, 0))],
                   out_specs=pl.BlockSpec((tr, d), lambda i: (i, 0)), out_shape=_sds((s, d), BF16),
                   compiler_params=_params("parallel"))(x, g)


def _rmsnorm_bwd(x, g, dh, res, name, riders=()):
    s, d = x.shape
    tr = 512

    def body(x_ref, g_ref, dh_ref, res_ref, dx_ref, dxb_ref, dg_ref):
        i = pl.program_id(0)
        xv = x_ref[...]
        rstd = lax.rsqrt(jnp.mean(xv * xv, axis=-1, keepdims=True) + EPS)
        xh = xv * rstd
        dhv = dh_ref[...]

        @pl.when(i == 0)
        def _():
            dg_ref[...] = jnp.zeros_like(dg_ref)

        dg_ref[...] += jnp.sum(dhv * xh, axis=0, keepdims=True)
        dxh = dhv * g_ref[...]
        dx = res_ref[...] + rstd * (dxh - xh * jnp.mean(dxh * xh, axis=-1, keepdims=True))
        dx_ref[...] = dx
        dxb_ref[...] = dx.astype(BF16)

    row = pl.BlockSpec((tr, d), lambda i: (i, 0))
    vec = pl.BlockSpec((1, d), lambda i: (0, 0))
    rs = _Riders(riders, 4, 3)
    out = _pallas(_with_riders(body, 4, 3, 0, rs, (s // tr,)), name=name, grid=(s // tr,),
                  in_specs=[row, vec, row, row, *rs.in_specs], out_specs=[row, row, vec, *rs.out_specs],
                  out_shape=[_sds((s, d), F32), _sds((s, d), BF16), _sds((1, d), F32), *rs.out_shapes],
                  input_output_aliases=rs.aliases, scratch_shapes=rs.scratch,
                  compiler_params=_params("arbitrary"))(x, g, dh, res, *rs.arrays)
    return (out[:3], rs.split(out[3:])) if riders else out


def _adamw_math(w, g, m, v):
    m = ADAM_B1 * m + (1.0 - ADAM_B1) * g
    v = ADAM_B2 * v + (1.0 - ADAM_B2) * (g * g)
    m_hat = m / (1.0 - ADAM_B1 ** ADAM_STEP)
    v_hat = v / (1.0 - ADAM_B2 ** ADAM_STEP)
    delta = -ADAM_LR * (m_hat / (jnp.sqrt(v_hat) + ADAM_EPS) + ADAM_WD * w)
    return delta, m, v


def _adamw(w, g, m, v, name):
    r, c = w.shape
    tr = 256

    def body(w_ref, g_ref, m_ref, v_ref, go_ref, d_ref, nm_ref, nv_ref):
        g = g_ref[...]
        go_ref[...] = g
        d_ref[...], nm_ref[...], nv_ref[...] = _adamw_math(w_ref[...], g, m_ref[...], v_ref[...])

    blk = pl.BlockSpec((tr, c), lambda i: (i, 0))
    return _pallas(body, name=name, grid=(r // tr,), in_specs=[blk] * 4, out_specs=[blk] * 4,
                   out_shape=[_sds((r, c), F32)] * 4, compiler_params=_params("parallel"))(w, g, m, v)


def _tables(s):
    half = HEAD_DIM // 2
    pos = jnp.arange(s, dtype=F32)
    inv_freq = ROPE_BASE ** (-jnp.arange(half, dtype=F32) / half)
    ang = pos[:, None] * inv_freq[None, :]
    cos, sin = jnp.cos(ang), jnp.sin(ang)
    cos_f = jnp.concatenate([cos, cos], axis=-1)
    sin_f = jnp.concatenate([-sin, sin], axis=-1)
    log_g = jnp.log1p(-jnp.exp2(-(5.0 + jnp.arange(HEADS, dtype=F32))))
    p = jnp.arange(CHUNK, dtype=F32)
    decay = jnp.exp(log_g[:, None, None] * jnp.abs(p[:, None] - p[None, :]))
    k_dec = jnp.exp(log_g[None, :] * (CHUNK - 1.0 - p)[:, None])
    q_dec = jnp.exp(log_g[None, :] * (p + 1.0)[:, None])
    c_dec = jnp.exp(log_g * CHUNK)
    k_dec = jnp.tile(jnp.broadcast_to(k_dec.T[:, :, None], (HEADS, CHUNK, HEAD_DIM)), (1, RET_BLOCK_CHUNKS, 1))
    q_dec = jnp.tile(jnp.broadcast_to(q_dec.T[:, :, None], (HEADS, CHUNK, HEAD_DIM)), (1, RET_BLOCK_CHUNKS, 1))
    c_dec = jnp.broadcast_to(c_dec[:, None, None], (HEADS, 1, HEAD_DIM))
    n = RET_SUB // CHUNK
    decay = (jnp.eye(n, dtype=F32)[None, :, None, :, None] * decay[:, None, :, None, :]).reshape(HEADS, RET_SUB, RET_SUB)
    return cos_f, sin_f, decay, k_dec, q_dec, c_dec


def _rot(x, cos_f, sin_f):
    return x * cos_f + pltpu.roll(x, HEAD_DIM // 2, 1) * sin_f


def _rot_bwd(d, cos_f, sin_f):
    return d * cos_f + pltpu.roll(d * sin_f, HEAD_DIM // 2, 1)


K_SCALE = HEAD_DIM ** -0.5


def _retention_fwd(proj, gn_g, tables, riders=()):
    s = proj.shape[0]
    nb = s // RET_ROWS
    nc = s // CHUNK
    cos_f, sin_f, decay, k_dec, q_dec, c_dec = tables

    def body(q_ref, k_ref, v_ref, g_ref, cos_ref, sin_ref, dec_ref, kd_ref, qd_ref, cd_ref, gn_ref,
             ret_ref, y_ref, prev_ref, state_ref):
        @pl.when(pl.program_id(1) == 0)
        def _():
            state_ref[...] = jnp.zeros_like(state_ref)

        cosv, sinv = cos_ref[...], sin_ref[...]
        q = _rot(q_ref[...], cosv, sinv)
        k = _rot(k_ref[...], cosv, sinv) * K_SCALE
        v = v_ref[...]
        rg = g_ref[...]
        dec, cd, gn = dec_ref[...], cd_ref[...], gn_ref[...]
        kdf, qdf = k * kd_ref[...], q * qd_ref[...]
        chunks = [slice(c * CHUNK, (c + 1) * CHUNK) for c in range(RET_BLOCK_CHUNKS)]
        contribs = [_dot(kdf[rows], v[rows], TN) for rows in chunks]
        state, states = state_ref[...], []
        for c in range(RET_BLOCK_CHUNKS):
            states.append(state)
            prev_ref[c] = state.astype(prev_ref.dtype)
            state = cd * state + contribs[c]
        state_ref[...] = state
        cross = jnp.concatenate([_dot(qdf[rows], st, NN) for rows, st in zip(chunks, states)], axis=0)
        intra = []
        for b in range(RET_ROWS // RET_SUB):
            rows = slice(b * RET_SUB, (b + 1) * RET_SUB)
            intra.append(_dot(_dot(q[rows], k[rows], NT) * dec, v[rows], NN))
        y = jnp.concatenate(intra, axis=0) + cross
        y_ref[...] = y
        mu = jnp.mean(y, axis=-1, keepdims=True)
        yc = y - mu
        var = jnp.mean(yc * yc, axis=-1, keepdims=True)
        yn = yc * lax.rsqrt(var + GN_EPS) * gn
        ret_ref[...] = (rg * jax.nn.sigmoid(rg) * yn).astype(ret_ref.dtype)

    def col(off):
        return pl.BlockSpec((RET_ROWS, HEAD_DIM), lambda h, i: (i, off + h))

    pos = pl.BlockSpec((RET_ROWS, HEAD_DIM), lambda h, i: (i, 0))
    per_head = lambda shape: pl.BlockSpec((None, *shape), lambda h, i: (h, 0, 0))
    rs = _Riders(riders, 11, 3)
    res = _pallas(
        _with_riders(body, 11, 3, 1, rs, (HEADS, nb)), name="retention_fwd", grid=(HEADS, nb),
        in_specs=[col(0), col(HEADS), col(2 * HEADS), col(3 * HEADS), pos, pos,
                  per_head((RET_SUB, RET_SUB)), per_head((RET_ROWS, HEAD_DIM)), per_head((RET_ROWS, HEAD_DIM)),
                  per_head((1, HEAD_DIM)), pl.BlockSpec((1, HEAD_DIM), lambda h, i: (0, h)), *rs.in_specs],
        out_specs=[col(0), col(0),
                   pl.BlockSpec((None, RET_BLOCK_CHUNKS, HEAD_DIM, HEAD_DIM), lambda h, i: (h, i, 0, 0)),
                   *rs.out_specs],
        out_shape=[_sds((s, 2 * HEADS * HEAD_DIM), BF16), _sds((s, HEADS * HEAD_DIM), F32),
                   _sds((HEADS, nc, HEAD_DIM, HEAD_DIM), MXU_DTYPE), *rs.out_shapes],
        input_output_aliases=rs.aliases,
        scratch_shapes=[pltpu.VMEM((HEAD_DIM, HEAD_DIM), F32), *rs.scratch],
        compiler_params=_params("arbitrary", "arbitrary"),
    )(proj, proj, proj, proj, cos_f, sin_f, decay, k_dec, q_dec, c_dec, gn_g, *rs.arrays)
    return res[:3], rs.split(res[3:])


def _retention_bwd(proj, gn_g, tables, y, prev, dmix, riders=()):
    s = proj.shape[0]
    nb = s // RET_ROWS
    cos_f, sin_f, decay, k_dec, q_dec, c_dec = tables

    def body(q_ref, k_ref, v_ref, g_ref, cos_ref, sin_ref, dec_ref, kd_ref, qd_ref, cd_ref, gn_ref,
             y_ref, prev_ref, dret_ref, dproj_ref, dgn_ref, gstate_ref, stage_ref, stage_sems):
        head, blk = pl.program_id(0), pl.program_id(1)
        step = head * nb + blk
        slot = step % 2

        def writes(sl):
            rows = pl.ds(pl.multiple_of((nb - 1 - blk) * RET_ROWS, RET_ROWS), RET_ROWS)
            return [pltpu.make_async_copy(
                stage_ref.at[sl, g], dproj_ref.at[rows, pl.ds(pl.multiple_of((g * HEADS + head) * HEAD_DIM, HEAD_DIM), HEAD_DIM)],
                stage_sems.at[sl, g]) for g in range(4)]

        @pl.when(step >= 2)
        def _():
            for cp in writes(slot):
                cp.wait()

        @pl.when(blk == 0)
        def _():
            gstate_ref[...] = jnp.zeros_like(gstate_ref)
            dgn_ref[...] = jnp.zeros_like(dgn_ref)

        cosv, sinv = cos_ref[...], sin_ref[...]
        q = _rot(q_ref[...], cosv, sinv)
        k = _rot(k_ref[...], cosv, sinv) * K_SCALE
        v = v_ref[...]
        dec, kd, qd, cd, gn = dec_ref[...], kd_ref[...], qd_ref[...], cd_ref[...], gn_ref[...]
        kdf, qdf = k * kd, q * qd
        rg = g_ref[...]
        yv = y_ref[...]
        dret = dret_ref[...]
        sig = jax.nn.sigmoid(rg)
        gate = rg * sig
        mu = jnp.mean(yv, axis=-1, keepdims=True)
        yc = yv - mu
        rstd = lax.rsqrt(jnp.mean(yc * yc, axis=-1, keepdims=True) + GN_EPS)
        z = yc * rstd
        dyn = dret * gate
        stage_ref[slot, 3] = (dret * (z * gn) * (sig * (1.0 + rg * (1.0 - sig)))).astype(stage_ref.dtype)
        dgn_ref[...] += jnp.sum(dyn * z, axis=0, keepdims=True)
        dz = dyn * gn
        dy = rstd * (dz - jnp.mean(dz, axis=-1, keepdims=True) - z * jnp.mean(dz * z, axis=-1, keepdims=True))
        chunks = [slice(c * CHUNK, (c + 1) * CHUNK) for c in range(RET_BLOCK_CHUNKS)]
        dprevs = [_dot(qdf[rows], dy[rows], TN) for rows in chunks]
        gst, gsts = gstate_ref[...], [None] * RET_BLOCK_CHUNKS
        for c in reversed(range(RET_BLOCK_CHUNKS)):
            gsts[c] = gst
            gst = dprevs[c] + cd * gst
        gstate_ref[...] = gst
        dq = jnp.concatenate([_dot(dy[rows], prev_ref[c], NT) for c, rows in enumerate(chunks)], axis=0) * qd
        dk = jnp.concatenate([_dot(v[rows], g, NT) for rows, g in zip(chunks, gsts)], axis=0) * kd
        dv = jnp.concatenate([_dot(kdf[rows], g, NN) for rows, g in zip(chunks, gsts)], axis=0)
        dqi, dki, dvi = [], [], []
        for b in range(RET_ROWS // RET_SUB):
            rows = slice(b * RET_SUB, (b + 1) * RET_SUB)
            qs, ks, vs, dys = q[rows], k[rows], v[rows], dy[rows]
            dvi.append(_dot(_dot(ks, qs, NT) * dec, dys, NN))
            dqi.append(_dot(_dot(dys, vs, NT) * dec, ks, NN))
            dki.append(_dot(_dot(vs, dys, NT) * dec, qs, NN))
        dq = dq + jnp.concatenate(dqi, axis=0)
        dk = dk + jnp.concatenate(dki, axis=0)
        dv = dv + jnp.concatenate(dvi, axis=0)
        stage_ref[slot, 0] = _rot_bwd(dq, cosv, sinv).astype(stage_ref.dtype)
        stage_ref[slot, 1] = _rot_bwd(dk * K_SCALE, cosv, sinv).astype(stage_ref.dtype)
        stage_ref[slot, 2] = dv.astype(stage_ref.dtype)
        for cp in writes(slot):
            cp.start()

        @pl.when(step == HEADS * nb - 1)
        def _():
            for cp in writes(1 - slot) + writes(slot):
                cp.wait()

    rev = lambda i: nb - 1 - i

    def col(off):
        return pl.BlockSpec((RET_ROWS, HEAD_DIM), lambda h, i: (rev(i), off + h))

    pos = pl.BlockSpec((RET_ROWS, HEAD_DIM), lambda h, i: (rev(i), 0))
    per_head = lambda shape: pl.BlockSpec((None, *shape), lambda h, i: (h, 0, 0))
    rs = _Riders(riders, 14, 2)
    res = _pallas(
        _with_riders(body, 14, 2, 3, rs, (HEADS, nb)), name="retention_bwd", grid=(HEADS, nb),
        in_specs=[col(0), col(HEADS), col(2 * HEADS), col(3 * HEADS), pos, pos,
                  per_head((RET_SUB, RET_SUB)), per_head((RET_ROWS, HEAD_DIM)), per_head((RET_ROWS, HEAD_DIM)),
                  per_head((1, HEAD_DIM)), pl.BlockSpec((1, HEAD_DIM), lambda h, i: (0, h)),
                  col(0), pl.BlockSpec((None, RET_BLOCK_CHUNKS, HEAD_DIM, HEAD_DIM), lambda h, i: (h, rev(i), 0, 0)),
                  col(0), *rs.in_specs],
        out_specs=[ANY, per_head((1, HEAD_DIM)), *rs.out_specs],
        out_shape=[_sds((s, proj.shape[1]), BF16), _sds((HEADS, 1, HEAD_DIM), F32), *rs.out_shapes],
        input_output_aliases=rs.aliases,
        scratch_shapes=[pltpu.VMEM((HEAD_DIM, HEAD_DIM), F32), pltpu.VMEM((2, 4, RET_ROWS, HEAD_DIM), BF16),
                        pltpu.SemaphoreType.DMA((2, 4)), *rs.scratch],
        compiler_params=_params("arbitrary", "arbitrary"),
    )(proj, proj, proj, proj, cos_f, sin_f, decay, k_dec, q_dec, c_dec, gn_g, y, prev, dmix, *rs.arrays)
    return res[:2], rs.split(res[2:])


ATT_COL0 = 4 * HEADS
PAD_ROWS = LEFT_CHUNKS * CHUNK
NORM_ROWS = 512
GROUP_CHUNKS = 4
GROUP = GROUP_CHUNKS * CHUNK
WIN = (LEFT_CHUNKS + GROUP_CHUNKS) * CHUNK
MASKED = -1e30


def _qk_norm(x, g):
    return x * lax.rsqrt(jnp.mean(x * x, axis=-1, keepdims=True) + EPS) * g


def _band_probs(qb, kb, bias, g):
    sc = _dot(qb, kb, NT) * K_SCALE + bias
    win_chunk = lax.broadcasted_iota(jnp.int32, (GROUP, WIN), 1) // CHUNK
    sc = jnp.where(g * GROUP_CHUNKS - LEFT_CHUNKS + win_chunk >= 0, sc, MASKED)
    e = jnp.exp(sc - jnp.max(sc, axis=-1, keepdims=True))
    return e / jnp.sum(e, axis=-1, keepdims=True)


def _with_riders(core, n_in, n_out, n_scratch, rs, grid):
    n_rin, n_rout = len(rs.arrays), len(rs.out_shapes)
    if not rs.riders:
        return core
    steps = 1
    for n in grid:
        steps *= n

    def body(*refs):
        outs_at = n_in + n_rin
        scratch_at = outs_at + n_out + n_rout
        bound = rs.bind(refs[n_in:outs_at], refs[outs_at + n_out:scratch_at], refs[scratch_at + n_scratch:])
        step = 0
        for axis, n in enumerate(grid):
            step = step * n + pl.program_id(axis)
        pl.when(step == 0)(lambda: rs.run("start", bound))
        pl.when(step == int(steps * RIDER_MID))(lambda: rs.run("mid", bound))
        pl.when(step == int(steps * RIDER_LATE))(lambda: rs.run("late", bound))
        core(*refs[:n_in], *refs[outs_at:outs_at + n_out], *refs[scratch_at:scratch_at + n_scratch])
        pl.when(step == steps - 1)(lambda: rs.run("end", bound))

    return body


def _attention_fwd(proj, gq, gk, bias, mix, riders=()):
    s = proj.shape[0]
    rs = _Riders(riders, 7, 1)

    def body(q_ref, k_ref, v_ref, gq_ref, gk_ref, bias_ref, mix_ref, o_ref, kp_ref, vp_ref):
        kp_ref[0:PAD_ROWS, :] = jnp.zeros((PAD_ROWS, HEAD_DIM), kp_ref.dtype)
        vp_ref[0:PAD_ROWS, :] = jnp.zeros((PAD_ROWS, HEAD_DIM), vp_ref.dtype)
        gqv, gkv = gq_ref[...], gk_ref[...]

        def fill(b, carry):
            r0 = pl.multiple_of(b * NORM_ROWS, NORM_ROWS)
            kp_ref[pl.ds(PAD_ROWS + r0, NORM_ROWS), :] = _qk_norm(k_ref[pl.ds(r0, NORM_ROWS), :], gkv).astype(kp_ref.dtype)
            vp_ref[pl.ds(PAD_ROWS + r0, NORM_ROWS), :] = v_ref[pl.ds(r0, NORM_ROWS), :].astype(vp_ref.dtype)
            return carry

        lax.fori_loop(0, s // NORM_ROWS, fill, 0)

        def group(g, carry):
            r0 = pl.multiple_of(g * GROUP, GROUP)
            qn = _qk_norm(q_ref[pl.ds(r0, GROUP), :], gqv)
            p = _band_probs(qn, kp_ref[pl.ds(r0, WIN), :], bias_ref[...], g)
            o_ref[pl.ds(r0, GROUP), :] = _dot(p, vp_ref[pl.ds(r0, WIN), :], NN).astype(o_ref.dtype)
            return carry

        lax.fori_loop(0, s // GROUP, group, 0, unroll=2)

    def col(off):
        return pl.BlockSpec((s, HEAD_DIM), lambda h: (0, off + h))

    vec = pl.BlockSpec((1, HEAD_DIM), lambda h: (0, 0))
    res = _pallas(
        _with_riders(body, 7, 1, 2, rs, (HEADS,)), name="attention_fwd", grid=(HEADS,),
        in_specs=[col(ATT_COL0), col(ATT_COL0 + HEADS), col(ATT_COL0 + 2 * HEADS), vec, vec,
                  pl.BlockSpec((None, GROUP, WIN), lambda h: (h, 0, 0)), ANY, *rs.in_specs],
        out_specs=[col(HEADS), *rs.out_specs], out_shape=[_sds(mix.shape, mix.dtype), *rs.out_shapes],
        input_output_aliases={6: 0, **rs.aliases},
        scratch_shapes=[pltpu.VMEM((s + PAD_ROWS, HEAD_DIM), MXU_DTYPE), pltpu.VMEM((s + PAD_ROWS, HEAD_DIM), MXU_DTYPE),
                        *rs.scratch],
        compiler_params=_params("arbitrary"),
    )(proj, proj, proj, gq, gk, bias, mix, *rs.arrays)
    return res[0], rs.split(res[1:])


def _attention_bwd(proj, gq, gk, bias, dmix, dproj, riders=()):
    s = proj.shape[0]
    rs = _Riders(riders, 8, 4)

    def body(q_ref, k_ref, v_ref, gq_ref, gk_ref, bias_ref, do_ref, dproj_in_ref,
             dproj_ref, dgq_ref, dgk_ref, dbias_ref, kp_ref, vp_ref, dkp_ref, dvp_ref, dqn_ref, stage_ref, stage_sems):
        head = pl.program_id(0)

        def writes():
            return [pltpu.make_async_copy(
                stage_ref.at[g],
                dproj_ref.at[:, pl.ds(pl.multiple_of((ATT_COL0 + g * HEADS + head) * HEAD_DIM, HEAD_DIM), HEAD_DIM)],
                stage_sems.at[g]) for g in range(3)]

        kp_ref[0:PAD_ROWS, :] = jnp.zeros((PAD_ROWS, HEAD_DIM), kp_ref.dtype)
        vp_ref[0:PAD_ROWS, :] = jnp.zeros((PAD_ROWS, HEAD_DIM), vp_ref.dtype)
        dkp_ref[...] = jnp.zeros_like(dkp_ref)
        dvp_ref[...] = jnp.zeros_like(dvp_ref)
        dbias_ref[...] = jnp.zeros_like(dbias_ref)
        gqv, gkv = gq_ref[...], gk_ref[...]

        def fill(b, carry):
            r0 = pl.multiple_of(b * NORM_ROWS, NORM_ROWS)
            kp_ref[pl.ds(PAD_ROWS + r0, NORM_ROWS), :] = _qk_norm(k_ref[pl.ds(r0, NORM_ROWS), :], gkv).astype(kp_ref.dtype)
            vp_ref[pl.ds(PAD_ROWS + r0, NORM_ROWS), :] = v_ref[pl.ds(r0, NORM_ROWS), :].astype(vp_ref.dtype)
            return carry

        lax.fori_loop(0, s // NORM_ROWS, fill, 0)

        def group(g, carry):
            r0 = pl.multiple_of(g * GROUP, GROUP)
            qn = _qk_norm(q_ref[pl.ds(r0, GROUP), :], gqv)
            kb = kp_ref[pl.ds(r0, WIN), :]
            vb = vp_ref[pl.ds(r0, WIN), :]
            p = _band_probs(qn, kb, bias_ref[...], g)
            do = do_ref[pl.ds(r0, GROUP), :]
            dvp_ref[pl.ds(r0, WIN), :] += _dot(p, do, TN)
            dp = _dot(do, vb, NT)
            ds = p * (dp - jnp.sum(dp * p, axis=-1, keepdims=True))
            dbias_ref[...] += ds
            dss = ds * K_SCALE
            dqn_ref[pl.ds(r0, GROUP), :] = _dot(dss, kb, NN)
            dkp_ref[pl.ds(r0, WIN), :] += _dot(dss, qn, TN)
            return carry

        lax.fori_loop(0, s // GROUP, group, 0, unroll=2)

        @pl.when(head == 0)
        def _():
            dgq_ref[...] = jnp.zeros_like(dgq_ref)
            dgk_ref[...] = jnp.zeros_like(dgk_ref)

        @pl.when(head > 0)
        def _():
            for cp in writes():
                cp.wait()

        def norm_bwd(x, g, dn):
            rstd = lax.rsqrt(jnp.mean(x * x, axis=-1, keepdims=True) + EPS)
            xh = x * rstd
            dxh = dn * g
            return rstd * (dxh - xh * jnp.mean(dxh * xh, axis=-1, keepdims=True)), jnp.sum(dn * xh, axis=0, keepdims=True)

        def finish(b, carry):
            r0 = pl.multiple_of(b * NORM_ROWS, NORM_ROWS)
            rows = pl.ds(r0, NORM_ROWS)
            dq, dgq = norm_bwd(q_ref[rows, :], gqv, dqn_ref[rows, :])
            dk, dgk = norm_bwd(k_ref[rows, :], gkv, dkp_ref[pl.ds(PAD_ROWS + r0, NORM_ROWS), :])
            stage_ref[0, rows, :] = dq.astype(stage_ref.dtype)
            stage_ref[1, rows, :] = dk.astype(stage_ref.dtype)
            stage_ref[2, rows, :] = dvp_ref[pl.ds(PAD_ROWS + r0, NORM_ROWS), :].astype(stage_ref.dtype)
            dgq_ref[...] += dgq
            dgk_ref[...] += dgk
            return carry

        lax.fori_loop(0, s // NORM_ROWS, finish, 0)
        for cp in writes():
            cp.start()

        @pl.when(head == HEADS - 1)
        def _():
            for cp in writes():
                cp.wait()

    def col(off):
        return pl.BlockSpec((s, HEAD_DIM), lambda h: (0, off + h))

    vec = pl.BlockSpec((1, HEAD_DIM), lambda h: (0, 0))
    hbias = pl.BlockSpec((None, GROUP, WIN), lambda h: (h, 0, 0))
    res = _pallas(
        _with_riders(body, 8, 4, 7, rs, (HEADS,)), name="attention_bwd", grid=(HEADS,),
        in_specs=[col(ATT_COL0), col(ATT_COL0 + HEADS), col(ATT_COL0 + 2 * HEADS), vec, vec, hbias, col(HEADS), ANY,
                  *rs.in_specs],
        out_specs=[ANY, vec, vec, hbias, *rs.out_specs],
        out_shape=[_sds(dproj.shape, dproj.dtype), _sds((1, HEAD_DIM), F32), _sds((1, HEAD_DIM), F32),
                   _sds((HEADS, GROUP, WIN), F32), *rs.out_shapes],
        input_output_aliases={7: 0, **rs.aliases},
        scratch_shapes=[pltpu.VMEM((s + PAD_ROWS, HEAD_DIM), MXU_DTYPE), pltpu.VMEM((s + PAD_ROWS, HEAD_DIM), MXU_DTYPE),
                        pltpu.VMEM((s + PAD_ROWS, HEAD_DIM), F32), pltpu.VMEM((s + PAD_ROWS, HEAD_DIM), F32),
                        pltpu.VMEM((s, HEAD_DIM), F32), pltpu.VMEM((3, s, HEAD_DIM), BF16),
                        pltpu.SemaphoreType.DMA((3,)), *rs.scratch],
        compiler_params=_params("arbitrary"),
    )(proj, proj, proj, gq, gk, bias, dmix, dproj, *rs.arrays)
    return res[:4], rs.split(res[4:])


DIAG_SPLIT = (BAND + WIN - CHUNK) // 2


def _diag_bin(m):
    t = jnp.where(m < DIAG_SPLIT, m, m - WIN)
    return jnp.clip(LEFT_CHUNKS * CHUNK - t, -(CHUNK - 1), REL_CLIP) + (CHUNK - 1)


def _skew_rows(a, left):
    row = lax.broadcasted_iota(jnp.int32, (GROUP, WIN), 0)
    for b in range(GROUP.bit_length() - 1):
        step = 1 << b
        a = jnp.where(jnp.bitwise_and(row, step) != 0, pltpu.roll(a, WIN - step if left else step, 1), a)
    return a


def _rel_bias_expand(rel_bias):
    def body(rb_ref, o_ref):
        h = pl.program_id(0)
        bins = _diag_bin(lax.broadcasted_iota(jnp.int32, (8, WIN), 1))
        per_diag = lax.fori_loop(0, REL_SIZE, lambda r, acc: jnp.where(bins == r, rb_ref[h, r], acc),
                                 jnp.zeros((8, WIN), F32))
        table = _skew_rows(jnp.broadcast_to(per_diag[0:1], (GROUP, WIN)), left=False)
        row_chunk = lax.broadcasted_iota(jnp.int32, (GROUP, WIN), 0) // CHUNK
        col_chunk = lax.broadcasted_iota(jnp.int32, (GROUP, WIN), 1) // CHUNK
        in_band = jnp.logical_and(col_chunk >= row_chunk, col_chunk <= row_chunk + LEFT_CHUNKS)
        o_ref[...] = jnp.where(in_band, table, MASKED)

    return _pallas(body, name="rel_bias_expand", grid=(HEADS,), in_specs=[pl.BlockSpec(memory_space=pltpu.SMEM)],
                   out_specs=pl.BlockSpec((None, GROUP, WIN), lambda h: (h, 0, 0)),
                   out_shape=_sds((HEADS, GROUP, WIN), F32), compiler_params=_params("parallel"))(rel_bias)


def _rel_bias_fold(dbias):
    def body(a_ref, o_ref):
        diag = jnp.sum(_skew_rows(a_ref[...], left=True), axis=0, keepdims=True)
        onehot = (_diag_bin(lax.broadcasted_iota(jnp.int32, (WIN, REL_SIZE), 0))
                  == lax.broadcasted_iota(jnp.int32, (WIN, REL_SIZE), 1)).astype(MXU_DTYPE)
        rest = jnp.broadcast_to(diag, (8, WIN))
        out = jnp.zeros((8, REL_SIZE), F32)
        for _ in range(3):
            piece = rest.astype(BF16)
            out = out + _dot(piece, onehot, NN)
            rest = rest - piece.astype(F32)
        o_ref[...] = out[0:1]

    return _pallas(body, name="rel_bias_fold", grid=(HEADS,),
                   in_specs=[pl.BlockSpec((None, GROUP, WIN), lambda h: (h, 0, 0))],
                   out_specs=pl.BlockSpec((None, 1, REL_SIZE), lambda h: (h, 0, 0)),
                   out_shape=_sds((HEADS, 1, REL_SIZE), F32), compiler_params=_params("parallel"))(dbias)


def _place():
    return lax.axis_index("x"), lax.axis_index("y"), lax.axis_index("c")


def _other_chips(x, y):
    return [(1 - x, y), (x, 1 - y), (1 - x, 1 - y)]


class _Rider:
    reads, ins, new, n_sems = (), (), (), 1

    def start(self, reads, ins, new, send, recv):
        pass

    def mid(self, reads, ins, new, send, recv):
        pass

    def late(self, reads, ins, new, send, recv):
        pass

    def end(self, reads, ins, new, send, recv):
        pass


class _Riders:
    def __init__(self, riders, n_host_in, n_host_out):
        self.riders = list(riders)
        self.arrays, self.out_shapes, self.aliases, self.scratch = [], [], {}, []
        for r in self.riders:
            for t, a in enumerate(r.ins):
                self.aliases[n_host_in + len(self.arrays) + len(r.reads) + t] = n_host_out + len(self.out_shapes) + t
            self.arrays += [*r.reads, *r.ins]
            self.out_shapes += [_sds(a.shape, a.dtype) for a in r.ins] + list(r.new)
            self.scratch += [pltpu.SemaphoreType.DMA((r.n_sems,)), pltpu.SemaphoreType.DMA((r.n_sems,))]
        self.in_specs = [ANY] * len(self.arrays)
        self.out_specs = [ANY] * len(self.out_shapes)

    def bind(self, in_refs, out_refs, scratch_refs):
        bound, i, o = [], 0, 0
        for t, r in enumerate(self.riders):
            reads = in_refs[i:i + len(r.reads)]
            i += len(r.reads) + len(r.ins)
            ins = out_refs[o:o + len(r.ins)]
            new = out_refs[o + len(r.ins):o + len(r.ins) + len(r.new)]
            o += len(r.ins) + len(r.new)
            bound.append((reads, ins, new, scratch_refs[2 * t], scratch_refs[2 * t + 1]))
        return bound

    def run(self, phase, bound):
        for r, b in zip(self.riders, bound):
            getattr(r, phase)(*b)

    def split(self, outs):
        res, o = [], 0
        for r in self.riders:
            n = len(r.ins) + len(r.new)
            res.append(list(outs[o:o + n]))
            o += n
        return res


def _run_riders(name, riders):
    rs = _Riders(riders, 0, 0)
    n_in, n_out = len(rs.arrays), len(rs.out_shapes)

    def body(*refs):
        bound = rs.bind(refs[:n_in], refs[n_in:n_in + n_out], refs[n_in + n_out:])
        rs.run("start", bound)
        rs.run("mid", bound)
        rs.run("late", bound)
        rs.run("end", bound)

    outs = _pallas(body, name=name, in_specs=rs.in_specs, out_specs=rs.out_specs, out_shape=rs.out_shapes,
                   input_output_aliases=rs.aliases, scratch_shapes=rs.scratch)(*rs.arrays)
    return rs.split(outs)


class _GatherRider(_Rider):
    X_LINK, Y_LINK, Y_PASS, X_PASS, D2D_X, D2D_Y, D2D_DIAG, N_SEMS = 0, 1, 2, 3, 4, 5, 6, 7

    def __init__(self, blocks, part=(0, 1, 1)):
        self.ins = tuple(blocks)
        self.part = part
        self.n_sems = self.N_SEMS * len(blocks)

    def _copy(self, out, send, recv, w, sem, chip_from, cc, to, sub=None):
        hr = self.ins[w].shape[1] // 2
        lo, hi, n = self.part
        first, size = cc * hr + lo * (hr // n), (hi - lo) * (hr // n)
        if sub is not None:
            size //= 2
            first += sub * size
        piece = out[w].at[2 * chip_from[0] + chip_from[1], pl.ds(first, size), :]
        return pltpu.make_async_remote_copy(src_ref=piece, dst_ref=piece, send_sem=send.at[self.N_SEMS * w + sem],
                                            recv_sem=recv.at[self.N_SEMS * w + sem], device_id=to, device_id_type=MESH)

    def _sent(self, out, send, recv, w):
        x, y, c = _place()
        me, sib = (x, y), (x, y, 1 - c)
        xn, yn, diag = _other_chips(x, y)
        cp = functools.partial(self._copy, out, send, recv, w)
        return [("start", cp(self.X_LINK, me, c, (*xn, c))), ("start", cp(self.Y_LINK, me, c, (*yn, c))),
                ("mid_x", cp(self.D2D_X, xn, c, sib)), ("mid_x", cp(self.Y_PASS, xn, c, (*yn, c), sub=0)),
                ("mid_y", cp(self.D2D_Y, yn, c, sib)), ("mid_y", cp(self.X_PASS, yn, c, (*xn, c), sub=1)),
                ("late", cp(self.D2D_DIAG, diag, c, sib))]

    def _go(self, out, send, recv, phase):
        for w in range(len(self.ins)):
            for ph, copy in self._sent(out, send, recv, w):
                if ph == phase:
                    copy.start()

    def start(self, reads, out, new, send, recv):
        self._go(out, send, recv, "start")

    def mid(self, reads, out, new, send, recv):
        x, y, c = _place()
        xn, yn, _ = _other_chips(x, y)
        for w in range(len(self.ins)):
            self._copy(out, send, recv, w, self.X_LINK, xn, c, (x, y, c)).wait_recv()
        self._go(out, send, recv, "mid_x")
        for w in range(len(self.ins)):
            self._copy(out, send, recv, w, self.Y_LINK, yn, c, (x, y, c)).wait_recv()
        self._go(out, send, recv, "mid_y")

    def late(self, reads, out, new, send, recv):
        x, y, c = _place()
        diag = _other_chips(x, y)[2]
        for w in range(len(self.ins)):
            self._copy(out, send, recv, w, self.Y_PASS, diag, c, (x, y, c), sub=0).wait_recv()
            self._copy(out, send, recv, w, self.X_PASS, diag, c, (x, y, c), sub=1).wait_recv()
        self._go(out, send, recv, "late")

    def end(self, reads, out, new, send, recv):
        x, y, c = _place()
        xn, yn, diag = _other_chips(x, y)
        for w in range(len(self.ins)):
            for sem, chip in ((self.D2D_X, xn), (self.D2D_Y, yn), (self.D2D_DIAG, diag)):
                self._copy(out, send, recv, w, sem, chip, 1 - c, (x, y, c)).wait_recv()
        for w in range(len(self.ins)):
            for _, copy in self._sent(out, send, recv, w):
                copy.wait_send()


class _SwapRider(_Rider):
    def __init__(self, grads):
        self.reads = tuple(grads)
        self.new = tuple(_sds((N_CHIPS, g.shape[1] // 2, g.shape[2]), g.dtype) for g in grads)
        self.n_sems = len(grads)

    def _copies(self, src, new, send, recv):
        x, y, c = _place()
        copies = []
        for w in range(len(self.reads)):
            hr = self.reads[w].shape[1] // 2
            copies.append(pltpu.make_async_remote_copy(
                src_ref=src[w].at[:, pl.ds((1 - c) * hr, hr), :], dst_ref=new[w],
                send_sem=send.at[w], recv_sem=recv.at[w], device_id=(x, y, 1 - c), device_id_type=MESH))
        return copies

    def start(self, src, ins, new, send, recv):
        for cp in self._copies(src, new, send, recv):
            cp.start()

    def end(self, src, ins, new, send, recv):
        for cp in self._copies(src, new, send, recv):
            cp.wait()


def _add_half(g, got, c_arr, name):
    nk, r, cols = g.shape
    hr = r // 2
    tr = min(hr, 1024)
    nb = hr // tr

    def body(c_ref, g_ref, got_ref, o_ref):
        o_ref[...] = (g_ref[...].astype(F32) + got_ref[...].astype(F32)).astype(o_ref.dtype)

    grid_spec = pltpu.PrefetchScalarGridSpec(
        num_scalar_prefetch=1, grid=(nk, nb),
        in_specs=[pl.BlockSpec((None, tr, cols), lambda k, i, c_ref: (k, c_ref[0] * nb + i, 0)),
                  pl.BlockSpec((None, tr, cols), lambda k, i, c_ref: (k, i, 0))],
        out_specs=pl.BlockSpec((None, tr, cols), lambda k, i, c_ref: (k, i, 0)))
    return _pallas(body, name=name, grid_spec=grid_spec, out_shape=_sds((nk, hr, cols), g.dtype),
                   compiler_params=_params("parallel", "parallel"))(c_arr, g, got)


class _SendPartialsRider(_Rider):
    def __init__(self, parts, got=None, part=(0, 1, 1)):
        self.reads = tuple(parts)
        if got is None:
            self.new = tuple(_sds((N_CHIPS - 1, *p.shape[1:]), p.dtype) for p in parts)
        else:
            self.ins = tuple(got)
        self.part = part
        self.n_sems = 3 * len(parts)

    def _copies(self, src, ins, new, send, recv):
        x, y, c = _place()
        land = ins if self.ins else new
        lo, hi, n = self.part
        copies = []
        for w in range(len(self.reads)):
            pr = self.reads[w].shape[1] // n
            rows = pl.ds(lo * pr, (hi - lo) * pr)
            for j, chip in enumerate(_other_chips(x, y)):
                copies.append(pltpu.make_async_remote_copy(
                    src_ref=src[w].at[2 * chip[0] + chip[1], rows, :], dst_ref=land[w].at[j, rows, :],
                    send_sem=send.at[3 * w + j], recv_sem=recv.at[3 * w + j], device_id=(*chip, c), device_id_type=MESH))
        return copies

    def start(self, src, ins, new, send, recv):
        for cp in self._copies(src, ins, new, send, recv):
            cp.start()

    def end(self, src, ins, new, send, recv):
        for cp in self._copies(src, ins, new, send, recv):
            cp.wait()


def _sum_partials(part, got, kc_arr, name):
    _, hr, cols = part.shape
    tr = min(hr, 512)
    nb = hr // tr

    def body(kc_ref, p_ref, g0_ref, g1_ref, g2_ref, o_ref):
        o_ref[...] = ((p_ref[...].astype(F32) + g0_ref[...].astype(F32)) + g1_ref[...].astype(F32)) + g2_ref[...].astype(F32)

    slot = lambda j: pl.BlockSpec((None, tr, cols), lambda i, kc_ref: (j, i, 0))
    grid_spec = pltpu.PrefetchScalarGridSpec(
        num_scalar_prefetch=1, grid=(nb,),
        in_specs=[pl.BlockSpec((None, tr, cols), lambda i, kc_ref: (kc_ref[0], i, 0)), slot(0), slot(1), slot(2)],
        out_specs=pl.BlockSpec((tr, cols), lambda i, kc_ref: (kc_ref[1] * nb + i, 0)))
    return _pallas(body, name=name, grid_spec=grid_spec, out_shape=_sds((2 * hr, cols), F32),
                   compiler_params=_params("parallel"))(kc_arr, part, got, got, got)


class _ShareRider(_Rider):
    def __init__(self, grads):
        self.ins = tuple(grads)
        self.n_sems = len(grads)

    def _copies(self, out, send, recv):
        x, y, c = _place()
        copies = []
        for w in range(len(self.ins)):
            hr = self.ins[w].shape[0] // 2
            mine = out[w].at[pl.ds(c * hr, hr), :]
            copies.append(pltpu.make_async_remote_copy(
                src_ref=mine, dst_ref=mine, send_sem=send.at[w], recv_sem=recv.at[w],
                device_id=(x, y, 1 - c), device_id_type=MESH))
        return copies

    def start(self, reads, out, new, send, recv):
        for cp in self._copies(out, send, recv):
            cp.start()

    def end(self, reads, out, new, send, recv):
        for cp in self._copies(out, send, recv):
            cp.wait()


def _small_allreduce_adamw(g_part, w, m, v):
    rows = g_part.shape[0]

    def body(g_ref, w_ref, m_ref, v_ref, go_ref, d_ref, nm_ref, nv_ref, all_ref, send_sems, recv_sems):
        x, y, c = _place()
        me = 4 * x + 2 * y + c
        all_ref[me] = g_ref[...]
        copies = []
        for r in range(1, 8):
            dx, dy, dc = (r >> 2) & 1, (r >> 1) & 1, r & 1
            peer = (1 - x if dx else x, 1 - y if dy else y, 1 - c if dc else c)
            copies.append(pltpu.make_async_remote_copy(
                src_ref=g_ref, dst_ref=all_ref.at[me], send_sem=send_sems.at[r - 1], recv_sem=recv_sems.at[r - 1],
                device_id=peer, device_id_type=MESH))
        for cp in copies:
            cp.start()
        for cp in copies:
            cp.wait()
        tot = all_ref[0]
        for d in range(1, 8):
            tot = tot + all_ref[d]
        go_ref[...] = tot
        d_ref[...], nm_ref[...], nv_ref[...] = _adamw_math(w_ref[...], tot, m_ref[...], v_ref[...])

    vm = pl.BlockSpec(memory_space=pltpu.VMEM)
    return _pallas(
        body, name="small_allreduce_adamw", in_specs=[vm] * 4, out_specs=[vm] * 4,
        out_shape=[_sds((rows, 128), F32)] * 4,
        scratch_shapes=[pltpu.VMEM((8, rows, 128), F32), pltpu.SemaphoreType.DMA((7,)), pltpu.SemaphoreType.DMA((7,))],
    )(g_part, w, m, v)


SMALL_SIZES = (2048, 1024, 128, 128, HEADS * REL_SIZE, 2048)
SMALL_PART_ROWS = tuple(-(-size // 1024) * 8 for size in SMALL_SIZES)
SMALL_ROWS = sum(SMALL_PART_ROWS)


def _pack_small(parts):
    rows = []
    for p, size, nr in zip(parts, SMALL_SIZES, SMALL_PART_ROWS):
        rows.append(jnp.pad(p.reshape(-1), (0, nr * 128 - size)).reshape(nr, 128))
    return jnp.concatenate(rows, axis=0)


def _unpack_small(slab, shapes):
    out, off = [], 0
    for size, nr, shape in zip(SMALL_SIZES, SMALL_PART_ROWS, shapes):
        out.append(slab[off:off + nr].reshape(-1)[:size].reshape(shape))
        off += nr
    return out


def kernel(x, norm1_g, w_in, ret_norm_g, q_norm_g, k_norm_g, rel_bias, w_out, norm2_g, w_ff1, w_ff2, loss_target, m_norm1_g, m_w_in, m_ret_norm_g, m_q_norm_g, m_k_norm_g, m_rel_bias, m_w_out, m_norm2_g, m_w_ff1, m_w_ff2, v_norm1_g, v_w_in, v_ret_norm_g, v_q_norm_g, v_k_norm_g, v_rel_bias, v_w_out, v_norm2_g, v_w_ff1, v_w_ff2):
    xs = x[0]
    tgt = loss_target[0]
    s, d = xs.shape
    d_in = N_CHIPS * w_in.shape[2]
    d_ff = N_CHIPS * w_ff1.shape[2]
    in_sh, ff_sh = w_in.shape[2], w_ff1.shape[2]
    tm = min(s, 1024)
    gi = s // tm
    c_arr = lax.axis_index("c").astype(jnp.int32).reshape(1)
    k_arr = (2 * lax.axis_index("x") + lax.axis_index("y")).astype(jnp.int32).reshape(1)
    tables = _tables(s)
    bias = _rel_bias_expand(rel_bias[0])

    blk_in = [_cast_bf16(w_in[0], k_arr, "cast_w_in_%d" % half, cols=(half, 2)) for half in range(2)]
    blk_out, blk_ff1, blk_ff2 = (_cast_bf16(w_out[0], k_arr, "cast_w_out"), _cast_bf16(w_ff1[0], k_arr, "cast_w_ff1"),
                                 _cast_bf16(w_ff2[0], k_arr, "cast_w_ff2"))
    ((wg_in0,),) = _run_riders("all_gather_w_in_0", [_GatherRider([blk_in[0]])])

    h1 = _rmsnorm_fwd(xs, norm1_g, "rmsnorm1")
    tn_in = in_sh // 2
    tk = d

    def proj_half(half, wg, through, riders):
        return _mm("proj_%d" % half, h1, wg, NN, (gi, N_CHIPS, 1),
                   pl.BlockSpec((tm, tk), lambda i, j, k: (i, 0)), pl.BlockSpec((None, tk, tn_in), lambda i, j, k: (j, 0, 0)),
                   [_sds((s, d_in), F32)], [pl.BlockSpec((tm, tn_in), lambda i, j, k: (i, 2 * j + half))], (tm, tn_in),
                   riders=riders, through=through)

    (proj,), ((wg_in1,),) = proj_half(0, wg_in0, None, [_GatherRider([blk_in[1]])])
    (proj,), ((wg_ff1,),) = proj_half(1, wg_in1, proj, [_GatherRider([blk_ff1], (0, 3, 8))])
    (mix, y_ret, prev), ((wg_ff1,),) = _retention_fwd(proj, ret_norm_g, tables, riders=[_GatherRider([wg_ff1], (3, 6, 8))])
    mix, ((wg_out,), (wg_ff2,)) = _attention_fwd(
        proj, q_norm_g, k_norm_g, bias, mix, riders=[_GatherRider([blk_out]), _GatherRider([blk_ff2], (0, 1, 4))])
    wg_out = wg_out.reshape(d, d)
    tn = 1024
    tile = pl.BlockSpec((tm, tn), lambda i, j, k: (i, j))
    def residual_norm(acc, res, g):
        x1v = res + acc
        yv = x1v * lax.rsqrt(jnp.mean(x1v * x1v, axis=-1, keepdims=True) + EPS)
        return x1v, yv * g

    tmo = min(s, 512)
    rows = pl.BlockSpec((tmo, d), lambda i, j, k: (i, 0))
    (x1, h2), ((wg_ff1,),) = _mm(
        "out_proj", mix, wg_out, NN, (s // tmo, 1, 1),
        rows, pl.BlockSpec((d, d), lambda i, j, k: (0, 0)),
        [_sds((s, d), F32), _sds((s, d), BF16)], [rows, rows], (tmo, d),
        extras=(xs, norm2_g), extra_specs=(rows, pl.BlockSpec((1, d), lambda i, j, k: (0, 0))),
        epi=residual_norm, riders=[_GatherRider([wg_ff1], (6, 8, 8))])
    tn_ff = min(ff_sh, 1024)
    per = ff_sh // tn_ff

    def relu2(acc):
        r = jnp.maximum(acc, 0.0)
        return acc, r * r

    (u, act), ((wg_ff2,),) = _mm(
        "ff1", h2, wg_ff1, NN, (gi, N_CHIPS * per, d // tk),
        pl.BlockSpec((tm, tk), lambda i, j, k: (i, k)),
        pl.BlockSpec((None, tk, tn_ff), lambda i, j, k: (j // per, k, j % per)),
        [_sds((s, d_ff), F32), _sds((s, d_ff), BF16)],
        [pl.BlockSpec((tm, tn_ff), lambda i, j, k: (i, j))] * 2, (tm, tn_ff), epi=relu2,
        riders=[_GatherRider([wg_ff2], (1, 4, 4))])
    wg_ff2 = wg_ff2.reshape(d_ff, d)

    def loss_epi(acc, res, t):
        diff = (res + acc) - t
        dy = diff / d
        return dy, dy, jnp.sum(diff * diff, axis=0, keepdims=True)

    tk2 = min(tk, 2048)
    dy, dyb, loss_cols = _mm(
        "ff2_loss", act, wg_ff2, NN, (gi, d // tn, d_ff // tk2),
        pl.BlockSpec((tm, tk2), lambda i, j, k: (i, k)), pl.BlockSpec((tk2, tn), lambda i, j, k: (k, j)),
        [_sds((s, d), F32), _sds((s, d), BF16), _sds((gi, 1, d), F32)],
        [tile, tile, pl.BlockSpec((None, 1, tn), lambda i, j, k: (i, 0, j))], (tm, tn),
        extras=(x1, tgt), extra_specs=(tile, tile), epi=loss_epi)
    loss = lax.psum(0.5 * jnp.sum(loss_cols) / d, ("x", "y", "c"))

    (du,) = _mm("d_act", dyb, wg_ff2, NT, (gi, d_ff // tn, d // tk),
                pl.BlockSpec((tm, tk), lambda i, j, k: (i, k)), pl.BlockSpec((tn, tk), lambda i, j, k: (j, k)),
                [_sds((s, d_ff), BF16)], [tile], (tm, tn), extras=(u,), extra_specs=(tile,),
                epi=lambda acc, uu: (acc * (2.0 * jnp.maximum(uu, 0.0)),))
    ts = min(s, 2048)
    wtile = pl.BlockSpec((tn, tn), lambda i, j, k: (i, j))
    (g_ff2,) = _mm("dw_ff2", act, dyb, TN, (d_ff // tn, d // tn, s // ts),
                   pl.BlockSpec((ts, tn), lambda i, j, k: (k, i)), pl.BlockSpec((ts, tn), lambda i, j, k: (k, j)),
                   [_sds((d_ff, d), BF16)], [wtile], (tn, tn))
    g_ff2 = g_ff2.reshape(N_CHIPS, d_ff // N_CHIPS, d)
    (g_ff1,), ((got_ff2,),) = _mm(
        "dw_ff1", h2, du, TN, (d // tn, N_CHIPS * per, s // ts),
        pl.BlockSpec((ts, tn), lambda i, j, k: (k, i)), pl.BlockSpec((ts, tn_ff), lambda i, j, k: (k, j)),
        [_sds((N_CHIPS, d, ff_sh), BF16)],
        [pl.BlockSpec((None, tn, tn_ff), lambda i, j, k: (j // per, i, j % per))], (tn, tn_ff),
        riders=[_SwapRider([g_ff2])])
    p_ff2 = _add_half(g_ff2, got_ff2, c_arr, "chip_partial_w_ff2")
    tkf = min(tk, ff_sh)
    kper = ff_sh // tkf
    (dh2,), ((got2_ff2,), (got_ff1,)) = _mm(
        "d_h2", du, wg_ff1, NT, (gi, d // tn, d_ff // tkf),
        pl.BlockSpec((tm, tkf), lambda i, j, k: (i, k)),
        pl.BlockSpec((None, tn, tkf), lambda i, j, k: (k // kper, j, k % kper)),
        [_sds((s, d), F32)], [tile], (tm, tn),
        riders=[_SendPartialsRider([p_ff2], part=(0, 3, 4)), _SwapRider([g_ff1])])
    p_ff1 = _add_half(g_ff1, got_ff1, c_arr, "chip_partial_w_ff1")
    dx1, dx1b, g_norm2 = _rmsnorm_bwd(x1, norm2_g, dh2, dy, "rmsnorm2_bwd")

    (dmix,) = _mm("d_mix", dx1b, wg_out, NT, (gi, d // tn, d // tk),
                  pl.BlockSpec((tm, tk), lambda i, j, k: (i, k)), pl.BlockSpec((tn, tk), lambda i, j, k: (j, k)),
                  [_sds((s, d), F32)], [tile], (tm, tn))
    (g_out,) = _mm("dw_out", mix, dx1b, TN, (d // tn, d // tn, s // ts),
                   pl.BlockSpec((ts, tn), lambda i, j, k: (k, i)), pl.BlockSpec((ts, tn), lambda i, j, k: (k, j)),
                   [_sds((d, d), BF16)], [wtile], (tn, tn))
    g_out = g_out.reshape(N_CHIPS, d // N_CHIPS, d)
    (dproj, g_gn), ((got2_ff2,), (got2_ff1,), (got_out,)) = _retention_bwd(
        proj, ret_norm_g, tables, y_ret, prev, dmix,
        riders=[_SendPartialsRider([p_ff2], got=[got2_ff2], part=(3, 4, 4)), _SendPartialsRider([p_ff1], part=(0, 2, 4)),
                _SwapRider([g_out])])
    p_out = _add_half(g_out, got_out, c_arr, "chip_partial_w_out")
    (dproj, g_gq, g_gk, dbias), ((got2_ff1,), (got2_out,)) = _attention_bwd(
        proj, q_norm_g, k_norm_g, bias, dmix, dproj,
        riders=[_SendPartialsRider([p_ff1], got=[got2_ff1], part=(2, 4, 4)), _SendPartialsRider([p_out])])
    g_rel = _rel_bias_fold(dbias)
    names = ["w_in", "w_out", "w_ff1", "w_ff2"]
    kc_arr = jnp.concatenate([k_arr, c_arr])
    early = [_sum_partials(p, r, kc_arr, "sum_partials_" + nm)
             for p, r, nm in zip((p_out, p_ff1, p_ff2), (got2_out, got2_ff1, got2_ff2), names[1:])]
    (g_in,), (early,) = _mm(
        "dw_in", h1, dproj, TN, (d // tn, 2 * N_CHIPS, s // ts),
        pl.BlockSpec((ts, tn), lambda i, j, k: (k, i)), pl.BlockSpec((ts, tn_in), lambda i, j, k: (k, j)),
        [_sds((N_CHIPS, d, in_sh), BF16)],
        [pl.BlockSpec((None, tn, tn_in), lambda i, j, k: (j // 2, i, j % 2))], (tn, tn_in), riders=[_ShareRider(early)])
    ((got_in,),) = _run_riders("grad_swap_w_in", [_SwapRider([g_in])])
    p_in = _add_half(g_in, got_in, c_arr, "chip_partial_w_in")
    half_spec = pl.BlockSpec((None, tn, tn_in), lambda i, j, k: (k // 2, j, 0))
    (dh1,), ((got2_in,),) = _mm(
        "d_h1", dproj, [wg_in0, wg_in1], NT, (gi, d // tn, 2 * N_CHIPS),
        pl.BlockSpec((tm, tn_in), lambda i, j, k: (i, k)), [half_spec, half_spec],
        [_sds((s, d), F32)], [tile], (tm, tn), riders=[_SendPartialsRider([p_in])])
    grad_x, _, g_norm1 = _rmsnorm_bwd(xs, norm1_g, dh1, dx1, "rmsnorm1_bwd")
    ((g_w_in,),) = _run_riders("grad_share_w_in", [_ShareRider([_sum_partials(p_in, got2_in, kc_arr, "sum_partials_w_in")])])
    g_big = [g_w_in, *early]
    big = []
    for g, w, m, v, nm in zip(g_big, (w_in, w_out, w_ff1, w_ff2), (m_w_in, m_w_out, m_w_ff1, m_w_ff2),
                              (v_w_in, v_w_out, v_w_ff1, v_w_ff2), names):
        g, delta, new_m, new_v = _adamw(w[0], g, m[0], v[0], "adamw_" + nm)
        big.append((g[None], delta[None], new_m[None], new_v[None]))

    small_w = (norm1_g, ret_norm_g, q_norm_g, k_norm_g, rel_bias, norm2_g)
    small_m = (m_norm1_g, m_ret_norm_g, m_q_norm_g, m_k_norm_g, m_rel_bias, m_norm2_g)
    small_v = (v_norm1_g, v_ret_norm_g, v_q_norm_g, v_k_norm_g, v_rel_bias, v_norm2_g)
    shapes = [p.shape for p in small_w]
    g_small = _pack_small([g_norm1, g_gn, g_gq, g_gk, g_rel, g_norm2])
    sg, sd, sm, sv = (_unpack_small(a, shapes) for a in _small_allreduce_adamw(
        g_small, _pack_small(small_w), _pack_small(small_m), _pack_small(small_v)))

    def ordered(kind):
        sm_ = (sg, sd, sm, sv)[kind]
        return (sm_[0], big[0][kind], sm_[1], sm_[2], sm_[3], sm_[4], big[1][kind], sm_[5], big[2][kind], big[3][kind])

    return (loss, grad_x[None], *ordered(0), *ordered(1), *ordered(2), *ordered(3))
```

```python
import functools

import jax
import jax.numpy as jnp
from jax import lax
from jax.experimental import pallas as pl
from jax.experimental.pallas import tpu as pltpu

F32 = jnp.float32
BF16 = jnp.bfloat16
MXU_DTYPE = jnp.bfloat16

CHUNK = 64
HEADS = 8
HEAD_DIM = 128
LEFT_CHUNKS = 8
BAND = (LEFT_CHUNKS + 1) * CHUNK
REL_CLIP = 128
REL_SIZE = (CHUNK - 1) + REL_CLIP + 1
RET_BLOCK_CHUNKS = 8
RET_ROWS = RET_BLOCK_CHUNKS * CHUNK
RET_SUB = 256
ROPE_BASE = 10000.0
EPS = 1e-6
GN_EPS = 1e-5
ADAM_LR, ADAM_B1, ADAM_B2, ADAM_EPS, ADAM_WD, ADAM_STEP = 0.001, 0.9, 0.999, 1e-08, 0.01, 10
N_CHIPS = 4
VMEM_LIMIT = 56 * 1024 * 1024
MESH = pl.DeviceIdType.MESH
ANY = pl.BlockSpec(memory_space=pl.ANY)

NN = (((1,), (0,)), ((), ()))
NT = (((1,), (1,)), ((), ()))
TN = (((0,), (0,)), ((), ()))


def _pallas(body, **kw):
    return pl.pallas_call(body, **kw)


def _params(*sem):
    return pltpu.CompilerParams(dimension_semantics=sem, vmem_limit_bytes=VMEM_LIMIT)


def _dot(a, b, dims):
    return lax.dot_general(a.astype(MXU_DTYPE), b.astype(MXU_DTYPE), dims, preferred_element_type=F32)


RIDER_MID, RIDER_LATE = 0.5, 0.8


def _mm(name, a, b, dims, grid, a_spec, b_spec, outs, o_specs, acc_shape, extras=(), extra_specs=(), epi=None,
        riders=(), through=None):
    ni, nj, nk = grid
    n_ex, n_out = len(extras), len(outs)
    bs = list(b) if isinstance(b, (list, tuple)) else [b]
    b_specs = list(b_spec) if isinstance(b, (list, tuple)) else [b_spec]
    extras, extra_specs = (*bs[1:], *extras), (*b_specs[1:], *extra_specs)
    b, b_spec, n_b = bs[0], b_specs[0], len(bs)
    n_in = 1 + n_b + n_ex
    rs = _Riders(riders, n_in, n_out)
    n_rin, n_rout = len(rs.arrays), len(rs.out_shapes)
    steps = ni * nj * nk

    held = [] if through is None else [through]

    def body(*refs):
        a_ref, b_refs = refs[0], refs[1:1 + n_b]
        b_ref = b_refs[0]
        ex_refs = refs[1 + n_b:n_in]
        outs_at = n_in + n_rin + len(held)
        o_refs = refs[outs_at:outs_at + n_out]
        acc_ref = refs[outs_at + n_out + n_rout]
        k = pl.program_id(2)
        if riders:
            bound = rs.bind(refs[n_in:n_in + n_rin], refs[outs_at + n_out:outs_at + n_out + n_rout],
                            refs[outs_at + n_out + n_rout + 1:])
            step = (pl.program_id(0) * nj + pl.program_id(1)) * nk + k
            pl.when(step == 0)(lambda: rs.run("start", bound))
            pl.when(step == int(steps * RIDER_MID))(lambda: rs.run("mid", bound))
            pl.when(step == int(steps * RIDER_LATE))(lambda: rs.run("late", bound))

        def finish(acc):
            vals = epi(acc, *[r[...] for r in ex_refs]) if epi is not None else (acc,)
            for r, v in zip(o_refs, vals):
                r[...] = v.astype(r.dtype)

        if nk == 1:
            finish(_dot(a_ref[...], b_ref[...], dims))
        else:
            @pl.when(k == 0)
            def _():
                acc_ref[...] = jnp.zeros_like(acc_ref)

            for t, ref in enumerate(b_refs):
                def step_with(ref=ref):
                    acc_ref[...] += _dot(a_ref[...], ref[...], dims)

                if n_b == 1:
                    step_with()
                else:
                    pl.when(k % n_b == t)(step_with)
            pl.when(k == nk - 1)(lambda: finish(acc_ref[...]))

        if riders:
            pl.when(step == steps - 1)(lambda: rs.run("end", bound))

    res = _pallas(
        body, name=name, grid=grid, in_specs=[a_spec, b_spec, *extra_specs, *rs.in_specs, *[ANY for _ in held]],
        out_specs=[*o_specs, *rs.out_specs], out_shape=[*outs, *rs.out_shapes],
        input_output_aliases={**rs.aliases, **{n_in + n_rin: 0 for _ in held}},
        scratch_shapes=[pltpu.VMEM(acc_shape if nk > 1 else (8, 128), F32), *rs.scratch],
        compiler_params=_params(*(("arbitrary",) * 3 if riders else ("parallel", "parallel", "arbitrary"))),
    )(a, b, *extras, *rs.arrays, *held)
    return (res[:n_out], rs.split(res[n_out:])) if riders else res


def _sds(shape, dtype):
    return jax.ShapeDtypeStruct(shape, dtype)


def _cast_bf16(w, k_arr, name, cols=(0, 1)):
    r, c = w.shape[0], w.shape[1] // cols[1]
    tr = min(r, 512)

    def body(k_ref, w_ref, o_ref):
        o_ref[...] = w_ref[...].astype(BF16)

    grid_spec = pltpu.PrefetchScalarGridSpec(
        num_scalar_prefetch=1, grid=(r // tr,), in_specs=[pl.BlockSpec((tr, c), lambda i, k_ref: (i, cols[0]))],
        out_specs=pl.BlockSpec((None, tr, c), lambda i, k_ref: (k_ref[0], i, 0)))
    return _pallas(body, name=name, grid_spec=grid_spec, out_shape=_sds((N_CHIPS, r, c), BF16),
                   compiler_params=_params("parallel"))(k_arr, w)


def _rmsnorm_fwd(x, g, name, riders=()):
    s, d = x.shape
    tr = 512

    def body(x_ref, g_ref, o_ref):
        xv = x_ref[...]
        y = xv * lax.rsqrt(jnp.mean(xv * xv, axis=-1, keepdims=True) + EPS)
        o_ref[...] = (y * g_ref[...]).astype(o_ref.dtype)

    rs = _Riders(riders, 2, 1)
    out = _pallas(_with_riders(body, 2, 1, 0, rs, (s // tr,)), name=name, grid=(s // tr,),
                  in_specs=[pl.BlockSpec((tr, d), lambda i: (i, 0)), pl.BlockSpec((1, d), lambda i: (0, 0)), *rs.in_specs],
                  out_specs=[pl.BlockSpec((tr, d), lambda i: (i, 0)), *rs.out_specs],
                  out_shape=[_sds((s, d), BF16), *rs.out_shapes], input_output_aliases=rs.aliases,
                  scratch_shapes=rs.scratch, compiler_params=_params("arbitrary"))(x, g, *rs.arrays)
    return out[0], rs.split(out[1:])


def _rmsnorm_bwd(x, g, dh, res, name, riders=()):
    s, d = x.shape
    tr = 512

    def body(x_ref, g_ref, dh_ref, res_ref, dx_ref, dxb_ref, dg_ref):
        i = pl.program_id(0)
        xv = x_ref[...]
        rstd = lax.rsqrt(jnp.mean(xv * xv, axis=-1, keepdims=True) + EPS)
        xh = xv * rstd
        dhv = dh_ref[...]

        @pl.when(i == 0)
        def _():
            dg_ref[...] = jnp.zeros_like(dg_ref)

        dg_ref[...] += jnp.sum(dhv * xh, axis=0, keepdims=True)
        dxh = dhv * g_ref[...]
        dx = res_ref[...] + rstd * (dxh - xh * jnp.mean(dxh * xh, axis=-1, keepdims=True))
        dx_ref[...] = dx
        dxb_ref[...] = dx.astype(BF16)

    row = pl.BlockSpec((tr, d), lambda i: (i, 0))
    vec = pl.BlockSpec((1, d), lambda i: (0, 0))
    rs = _Riders(riders, 4, 3)
    out = _pallas(_with_riders(body, 4, 3, 0, rs, (s // tr,)), name=name, grid=(s // tr,),
                  in_specs=[row, vec, row, row, *rs.in_specs], out_specs=[row, row, vec, *rs.out_specs],
                  out_shape=[_sds((s, d), F32), _sds((s, d), BF16), _sds((1, d), F32), *rs.out_shapes],
                  input_output_aliases=rs.aliases, scratch_shapes=rs.scratch,
                  compiler_params=_params("arbitrary"))(x, g, dh, res, *rs.arrays)
    return (out[:3], rs.split(out[3:])) if riders else out


def _adamw_math(w, g, m, v):
    m = ADAM_B1 * m + (1.0 - ADAM_B1) * g
    v = ADAM_B2 * v + (1.0 - ADAM_B2) * (g * g)
    m_hat = m / (1.0 - ADAM_B1 ** ADAM_STEP)
    v_hat = v / (1.0 - ADAM_B2 ** ADAM_STEP)
    delta = -ADAM_LR * (m_hat / (jnp.sqrt(v_hat) + ADAM_EPS) + ADAM_WD * w)
    return delta, m, v


def _adamw(w, g, m, v, name):
    r, c = w.shape
    tr = 256

    def body(w_ref, g_ref, m_ref, v_ref, go_ref, d_ref, nm_ref, nv_ref):
        g = g_ref[...]
        go_ref[...] = g
        d_ref[...], nm_ref[...], nv_ref[...] = _adamw_math(w_ref[...], g, m_ref[...], v_ref[...])

    blk = pl.BlockSpec((tr, c), lambda i: (i, 0))
    return _pallas(body, name=name, grid=(r // tr,), in_specs=[blk] * 4, out_specs=[blk] * 4,
                   out_shape=[_sds((r, c), F32)] * 4, compiler_params=_params("parallel"))(w, g, m, v)


def _tables(s):
    half = HEAD_DIM // 2
    pos = jnp.arange(s, dtype=F32)
    inv_freq = ROPE_BASE ** (-jnp.arange(half, dtype=F32) / half)
    ang = pos[:, None] * inv_freq[None, :]
    cos, sin = jnp.cos(ang), jnp.sin(ang)
    cos_f = jnp.concatenate([cos, cos], axis=-1)
    sin_f = jnp.concatenate([-sin, sin], axis=-1)
    log_g = jnp.log1p(-jnp.exp2(-(5.0 + jnp.arange(HEADS, dtype=F32))))
    p = jnp.arange(CHUNK, dtype=F32)
    decay = jnp.exp(log_g[:, None, None] * jnp.abs(p[:, None] - p[None, :]))
    k_dec = jnp.exp(log_g[None, :] * (CHUNK - 1.0 - p)[:, None])
    q_dec = jnp.exp(log_g[None, :] * (p + 1.0)[:, None])
    c_dec = jnp.exp(log_g * CHUNK)
    k_dec = jnp.tile(jnp.broadcast_to(k_dec.T[:, :, None], (HEADS, CHUNK, HEAD_DIM)), (1, RET_BLOCK_CHUNKS, 1))
    q_dec = jnp.tile(jnp.broadcast_to(q_dec.T[:, :, None], (HEADS, CHUNK, HEAD_DIM)), (1, RET_BLOCK_CHUNKS, 1))
    c_dec = jnp.broadcast_to(c_dec[:, None, None], (HEADS, 1, HEAD_DIM))
    n = RET_SUB // CHUNK
    decay = (jnp.eye(n, dtype=F32)[None, :, None, :, None] * decay[:, None, :, None, :]).reshape(HEADS, RET_SUB, RET_SUB)
    return cos_f, sin_f, decay, k_dec, q_dec, c_dec


def _rot(x, cos_f, sin_f):
    return x * cos_f + pltpu.roll(x, HEAD_DIM // 2, 1) * sin_f


def _rot_bwd(d, cos_f, sin_f):
    return d * cos_f + pltpu.roll(d * sin_f, HEAD_DIM // 2, 1)


K_SCALE = HEAD_DIM ** -0.5


def _retention_fwd(proj, gn_g, tables, riders=()):
    s = proj.shape[0]
    nb = s // RET_ROWS
    nc = s // CHUNK
    cos_f, sin_f, decay, k_dec, q_dec, c_dec = tables

    def body(q_ref, k_ref, v_ref, g_ref, cos_ref, sin_ref, dec_ref, kd_ref, qd_ref, cd_ref, gn_ref,
             ret_ref, y_ref, prev_ref, state_ref):
        @pl.when(pl.program_id(1) == 0)
        def _():
            state_ref[...] = jnp.zeros_like(state_ref)

        cosv, sinv = cos_ref[...], sin_ref[...]
        q = _rot(q_ref[...], cosv, sinv)
        k = _rot(k_ref[...], cosv, sinv) * K_SCALE
        v = v_ref[...]
        rg = g_ref[...]
        dec, cd, gn = dec_ref[...], cd_ref[...], gn_ref[...]
        kdf, qdf = k * kd_ref[...], q * qd_ref[...]
        chunks = [slice(c * CHUNK, (c + 1) * CHUNK) for c in range(RET_BLOCK_CHUNKS)]
        contribs = [_dot(kdf[rows], v[rows], TN) for rows in chunks]
        state, states = state_ref[...], []
        for c in range(RET_BLOCK_CHUNKS):
            states.append(state)
            prev_ref[c] = state.astype(prev_ref.dtype)
            state = cd * state + contribs[c]
        state_ref[...] = state
        cross = jnp.concatenate([_dot(qdf[rows], st, NN) for rows, st in zip(chunks, states)], axis=0)
        intra = []
        for b in range(RET_ROWS // RET_SUB):
            rows = slice(b * RET_SUB, (b + 1) * RET_SUB)
            intra.append(_dot(_dot(q[rows], k[rows], NT) * dec, v[rows], NN))
        y = jnp.concatenate(intra, axis=0) + cross
        y_ref[...] = y
        mu = jnp.mean(y, axis=-1, keepdims=True)
        yc = y - mu
        var = jnp.mean(yc * yc, axis=-1, keepdims=True)
        yn = yc * lax.rsqrt(var + GN_EPS) * gn
        ret_ref[...] = (rg * jax.nn.sigmoid(rg) * yn).astype(ret_ref.dtype)

    def col(off):
        return pl.BlockSpec((RET_ROWS, HEAD_DIM), lambda h, i: (i, off + h))

    pos = pl.BlockSpec((RET_ROWS, HEAD_DIM), lambda h, i: (i, 0))
    per_head = lambda shape: pl.BlockSpec((None, *shape), lambda h, i: (h, 0, 0))
    rs = _Riders(riders, 11, 3)
    res = _pallas(
        _with_riders(body, 11, 3, 1, rs, (HEADS, nb)), name="retention_fwd", grid=(HEADS, nb),
        in_specs=[col(0), col(HEADS), col(2 * HEADS), col(3 * HEADS), pos, pos,
                  per_head((RET_SUB, RET_SUB)), per_head((RET_ROWS, HEAD_DIM)), per_head((RET_ROWS, HEAD_DIM)),
                  per_head((1, HEAD_DIM)), pl.BlockSpec((1, HEAD_DIM), lambda h, i: (0, h)), *rs.in_specs],
        out_specs=[col(0), col(0),
                   pl.BlockSpec((None, RET_BLOCK_CHUNKS, HEAD_DIM, HEAD_DIM), lambda h, i: (h, i, 0, 0)),
                   *rs.out_specs],
        out_shape=[_sds((s, 2 * HEADS * HEAD_DIM), BF16), _sds((s, HEADS * HEAD_DIM), F32),
                   _sds((HEADS, nc, HEAD_DIM, HEAD_DIM), MXU_DTYPE), *rs.out_shapes],
        input_output_aliases=rs.aliases,
        scratch_shapes=[pltpu.VMEM((HEAD_DIM, HEAD_DIM), F32), *rs.scratch],
        compiler_params=_params("arbitrary", "arbitrary"),
    )(proj, proj, proj, proj, cos_f, sin_f, decay, k_dec, q_dec, c_dec, gn_g, *rs.arrays)
    return res[:3], rs.split(res[3:])


def _retention_bwd(proj, gn_g, tables, y, prev, dmix, riders=()):
    s = proj.shape[0]
    nb = s // RET_ROWS
    cos_f, sin_f, decay, k_dec, q_dec, c_dec = tables

    def body(q_ref, k_ref, v_ref, g_ref, cos_ref, sin_ref, dec_ref, kd_ref, qd_ref, cd_ref, gn_ref,
             y_ref, prev_ref, dret_ref, dproj_ref, dgn_ref, gstate_ref, stage_ref, stage_sems):
        head, blk = pl.program_id(0), pl.program_id(1)
        step = head * nb + blk
        slot = step % 2

        def writes(sl):
            rows = pl.ds(pl.multiple_of((nb - 1 - blk) * RET_ROWS, RET_ROWS), RET_ROWS)
            return [pltpu.make_async_copy(
                stage_ref.at[sl, g], dproj_ref.at[rows, pl.ds(pl.multiple_of((g * HEADS + head) * HEAD_DIM, HEAD_DIM), HEAD_DIM)],
                stage_sems.at[sl, g]) for g in range(4)]

        @pl.when(step >= 2)
        def _():
            for cp in writes(slot):
                cp.wait()

        @pl.when(blk == 0)
        def _():
            gstate_ref[...] = jnp.zeros_like(gstate_ref)
            dgn_ref[...] = jnp.zeros_like(dgn_ref)

        cosv, sinv = cos_ref[...], sin_ref[...]
        q = _rot(q_ref[...], cosv, sinv)
        k = _rot(k_ref[...], cosv, sinv) * K_SCALE
        v = v_ref[...]
        dec, kd, qd, cd, gn = dec_ref[...], kd_ref[...], qd_ref[...], cd_ref[...], gn_ref[...]
        kdf, qdf = k * kd, q * qd
        rg = g_ref[...]
        yv = y_ref[...]
        dret = dret_ref[...]
        sig = jax.nn.sigmoid(rg)
        gate = rg * sig
        mu = jnp.mean(yv, axis=-1, keepdims=True)
        yc = yv - mu
        rstd = lax.rsqrt(jnp.mean(yc * yc, axis=-1, keepdims=True) + GN_EPS)
        z = yc * rstd
        dyn = dret * gate
        stage_ref[slot, 3] = (dret * (z * gn) * (sig * (1.0 + rg * (1.0 - sig)))).astype(stage_ref.dtype)
        dgn_ref[...] += jnp.sum(dyn * z, axis=0, keepdims=True)
        dz = dyn * gn
        dy = rstd * (dz - jnp.mean(dz, axis=-1, keepdims=True) - z * jnp.mean(dz * z, axis=-1, keepdims=True))
        chunks = [slice(c * CHUNK, (c + 1) * CHUNK) for c in range(RET_BLOCK_CHUNKS)]
        dprevs = [_dot(qdf[rows], dy[rows], TN) for rows in chunks]
        gst, gsts = gstate_ref[...], [None] * RET_BLOCK_CHUNKS
        for c in reversed(range(RET_BLOCK_CHUNKS)):
            gsts[c] = gst
            gst = dprevs[c] + cd * gst
        gstate_ref[...] = gst
        dq = jnp.concatenate([_dot(dy[rows], prev_ref[c], NT) for c, rows in enumerate(chunks)], axis=0) * qd
        dk = jnp.concatenate([_dot(v[rows], g, NT) for rows, g in zip(chunks, gsts)], axis=0) * kd
        dv = jnp.concatenate([_dot(kdf[rows], g, NN) for rows, g in zip(chunks, gsts)], axis=0)
        dqi, dki, dvi = [], [], []
        for b in range(RET_ROWS // RET_SUB):
            rows = slice(b * RET_SUB, (b + 1) * RET_SUB)
            qs, ks, vs, dys = q[rows], k[rows], v[rows], dy[rows]
            dvi.append(_dot(_dot(ks, qs, NT) * dec, dys, NN))
            dqi.append(_dot(_dot(dys, vs, NT) * dec, ks, NN))
            dki.append(_dot(_dot(vs, dys, NT) * dec, qs, NN))
        dq = dq + jnp.concatenate(dqi, axis=0)
        dk = dk + jnp.concatenate(dki, axis=0)
        dv = dv + jnp.concatenate(dvi, axis=0)
        stage_ref[slot, 0] = _rot_bwd(dq, cosv, sinv).astype(stage_ref.dtype)
        stage_ref[slot, 1] = _rot_bwd(dk * K_SCALE, cosv, sinv).astype(stage_ref.dtype)
        stage_ref[slot, 2] = dv.astype(stage_ref.dtype)
        for cp in writes(slot):
            cp.start()

        @pl.when(step == HEADS * nb - 1)
        def _():
            for cp in writes(1 - slot) + writes(slot):
                cp.wait()

    rev = lambda i: nb - 1 - i

    def col(off):
        return pl.BlockSpec((RET_ROWS, HEAD_DIM), lambda h, i: (rev(i), off + h))

    pos = pl.BlockSpec((RET_ROWS, HEAD_DIM), lambda h, i: (rev(i), 0))
    per_head = lambda shape: pl.BlockSpec((None, *shape), lambda h, i: (h, 0, 0))
    rs = _Riders(riders, 14, 2)
    res = _pallas(
        _with_riders(body, 14, 2, 3, rs, (HEADS, nb)), name="retention_bwd", grid=(HEADS, nb),
        in_specs=[col(0), col(HEADS), col(2 * HEADS), col(3 * HEADS), pos, pos,
                  per_head((RET_SUB, RET_SUB)), per_head((RET_ROWS, HEAD_DIM)), per_head((RET_ROWS, HEAD_DIM)),
                  per_head((1, HEAD_DIM)), pl.BlockSpec((1, HEAD_DIM), lambda h, i: (0, h)),
                  col(0), pl.BlockSpec((None, RET_BLOCK_CHUNKS, HEAD_DIM, HEAD_DIM), lambda h, i: (h, rev(i), 0, 0)),
                  col(0), *rs.in_specs],
        out_specs=[ANY, per_head((1, HEAD_DIM)), *rs.out_specs],
        out_shape=[_sds((s, proj.shape[1]), BF16), _sds((HEADS, 1, HEAD_DIM), F32), *rs.out_shapes],
        input_output_aliases=rs.aliases,
        scratch_shapes=[pltpu.VMEM((HEAD_DIM, HEAD_DIM), F32), pltpu.VMEM((2, 4, RET_ROWS, HEAD_DIM), BF16),
                        pltpu.SemaphoreType.DMA((2, 4)), *rs.scratch],
        compiler_params=_params("arbitrary", "arbitrary"),
    )(proj, proj, proj, proj, cos_f, sin_f, decay, k_dec, q_dec, c_dec, gn_g, y, prev, dmix, *rs.arrays)
    return res[:2], rs.split(res[2:])


ATT_COL0 = 4 * HEADS
PAD_ROWS = LEFT_CHUNKS * CHUNK
NORM_ROWS = 512
GROUP_CHUNKS = 4
GROUP = GROUP_CHUNKS * CHUNK
WIN = (LEFT_CHUNKS + GROUP_CHUNKS) * CHUNK
MASKED = -1e30


def _qk_norm(x, g):
    return x * lax.rsqrt(jnp.mean(x * x, axis=-1, keepdims=True) + EPS) * g


def _band_probs(qb, kb, bias, g):
    sc = _dot(qb, kb, NT) * K_SCALE + bias
    win_chunk = lax.broadcasted_iota(jnp.int32, (GROUP, WIN), 1) // CHUNK
    sc = jnp.where(g * GROUP_CHUNKS - LEFT_CHUNKS + win_chunk >= 0, sc, MASKED)
    e = jnp.exp(sc - jnp.max(sc, axis=-1, keepdims=True))
    return e / jnp.sum(e, axis=-1, keepdims=True)


def _with_riders(core, n_in, n_out, n_scratch, rs, grid):
    n_rin, n_rout = len(rs.arrays), len(rs.out_shapes)
    if not rs.riders:
        return core
    steps = 1
    for n in grid:
        steps *= n

    def body(*refs):
        outs_at = n_in + n_rin
        scratch_at = outs_at + n_out + n_rout
        bound = rs.bind(refs[n_in:outs_at], refs[outs_at + n_out:scratch_at], refs[scratch_at + n_scratch:])
        step = 0
        for axis, n in enumerate(grid):
            step = step * n + pl.program_id(axis)
        pl.when(step == 0)(lambda: rs.run("start", bound))
        pl.when(step == int(steps * RIDER_MID))(lambda: rs.run("mid", bound))
        pl.when(step == int(steps * RIDER_LATE))(lambda: rs.run("late", bound))
        core(*refs[:n_in], *refs[outs_at:outs_at + n_out], *refs[scratch_at:scratch_at + n_scratch])
        pl.when(step == steps - 1)(lambda: rs.run("end", bound))

    return body


def _attention_fwd(proj, gq, gk, bias, mix, riders=()):
    s = proj.shape[0]
    rs = _Riders(riders, 7, 1)

    def body(q_ref, k_ref, v_ref, gq_ref, gk_ref, bias_ref, mix_ref, o_ref, kp_ref, vp_ref):
        kp_ref[0:PAD_ROWS, :] = jnp.zeros((PAD_ROWS, HEAD_DIM), kp_ref.dtype)
        vp_ref[0:PAD_ROWS, :] = jnp.zeros((PAD_ROWS, HEAD_DIM), vp_ref.dtype)
        gqv, gkv = gq_ref[...], gk_ref[...]

        def fill(b, carry):
            r0 = pl.multiple_of(b * NORM_ROWS, NORM_ROWS)
            kp_ref[pl.ds(PAD_ROWS + r0, NORM_ROWS), :] = _qk_norm(k_ref[pl.ds(r0, NORM_ROWS), :], gkv).astype(kp_ref.dtype)
            vp_ref[pl.ds(PAD_ROWS + r0, NORM_ROWS), :] = v_ref[pl.ds(r0, NORM_ROWS), :].astype(vp_ref.dtype)
            return carry

        lax.fori_loop(0, s // NORM_ROWS, fill, 0)

        def group(g, carry):
            r0 = pl.multiple_of(g * GROUP, GROUP)
            qn = _qk_norm(q_ref[pl.ds(r0, GROUP), :], gqv)
            p = _band_probs(qn, kp_ref[pl.ds(r0, WIN), :], bias_ref[...], g)
            o_ref[pl.ds(r0, GROUP), :] = _dot(p, vp_ref[pl.ds(r0, WIN), :], NN).astype(o_ref.dtype)
            return carry

        lax.fori_loop(0, s // GROUP, group, 0, unroll=2)

    def col(off):
        return pl.BlockSpec((s, HEAD_DIM), lambda h: (0, off + h))

    vec = pl.BlockSpec((1, HEAD_DIM), lambda h: (0, 0))
    res = _pallas(
        _with_riders(body, 7, 1, 2, rs, (HEADS,)), name="attention_fwd", grid=(HEADS,),
        in_specs=[col(ATT_COL0), col(ATT_COL0 + HEADS), col(ATT_COL0 + 2 * HEADS), vec, vec,
                  pl.BlockSpec((None, GROUP, WIN), lambda h: (h, 0, 0)), ANY, *rs.in_specs],
        out_specs=[col(HEADS), *rs.out_specs], out_shape=[_sds(mix.shape, mix.dtype), *rs.out_shapes],
        input_output_aliases={6: 0, **rs.aliases},
        scratch_shapes=[pltpu.VMEM((s + PAD_ROWS, HEAD_DIM), MXU_DTYPE), pltpu.VMEM((s + PAD_ROWS, HEAD_DIM), MXU_DTYPE),
                        *rs.scratch],
        compiler_params=_params("arbitrary"),
    )(proj, proj, proj, gq, gk, bias, mix, *rs.arrays)
    return res[0], rs.split(res[1:])


def _attention_bwd(proj, gq, gk, bias, dmix, dproj, riders=()):
    s = proj.shape[0]
    rs = _Riders(riders, 8, 4)

    def body(q_ref, k_ref, v_ref, gq_ref, gk_ref, bias_ref, do_ref, dproj_in_ref,
             dproj_ref, dgq_ref, dgk_ref, dbias_ref, kp_ref, vp_ref, dkp_ref, dvp_ref, dqn_ref, stage_ref, stage_sems):
        head = pl.program_id(0)

        def writes():
            return [pltpu.make_async_copy(
                stage_ref.at[g],
                dproj_ref.at[:, pl.ds(pl.multiple_of((ATT_COL0 + g * HEADS + head) * HEAD_DIM, HEAD_DIM), HEAD_DIM)],
                stage_sems.at[g]) for g in range(3)]

        kp_ref[0:PAD_ROWS, :] = jnp.zeros((PAD_ROWS, HEAD_DIM), kp_ref.dtype)
        vp_ref[0:PAD_ROWS, :] = jnp.zeros((PAD_ROWS, HEAD_DIM), vp_ref.dtype)
        dkp_ref[...] = jnp.zeros_like(dkp_ref)
        dvp_ref[...] = jnp.zeros_like(dvp_ref)
        dbias_ref[...] = jnp.zeros_like(dbias_ref)
        gqv, gkv = gq_ref[...], gk_ref[...]

        def fill(b, carry):
            r0 = pl.multiple_of(b * NORM_ROWS, NORM_ROWS)
            kp_ref[pl.ds(PAD_ROWS + r0, NORM_ROWS), :] = _qk_norm(k_ref[pl.ds(r0, NORM_ROWS), :], gkv).astype(kp_ref.dtype)
            vp_ref[pl.ds(PAD_ROWS + r0, NORM_ROWS), :] = v_ref[pl.ds(r0, NORM_ROWS), :].astype(vp_ref.dtype)
            return carry

        lax.fori_loop(0, s // NORM_ROWS, fill, 0)

        def group(g, carry):
            r0 = pl.multiple_of(g * GROUP, GROUP)
            qn = _qk_norm(q_ref[pl.ds(r0, GROUP), :], gqv)
            kb = kp_ref[pl.ds(r0, WIN), :]
            vb = vp_ref[pl.ds(r0, WIN), :]
            p = _band_probs(qn, kb, bias_ref[...], g)
            do = do_ref[pl.ds(r0, GROUP), :]
            dvp_ref[pl.ds(r0, WIN), :] += _dot(p, do, TN)
            dp = _dot(do, vb, NT)
            ds = p * (dp - jnp.sum(dp * p, axis=-1, keepdims=True))
            dbias_ref[...] += ds
            dss = ds * K_SCALE
            dqn_ref[pl.ds(r0, GROUP), :] = _dot(dss, kb, NN)
            dkp_ref[pl.ds(r0, WIN), :] += _dot(dss, qn, TN)
            return carry

        lax.fori_loop(0, s // GROUP, group, 0, unroll=2)

        @pl.when(head == 0)
        def _():
            dgq_ref[...] = jnp.zeros_like(dgq_ref)
            dgk_ref[...] = jnp.zeros_like(dgk_ref)

        @pl.when(head > 0)
        def _():
            for cp in writes():
                cp.wait()

        def norm_bwd(x, g, dn):
            rstd = lax.rsqrt(jnp.mean(x * x, axis=-1, keepdims=True) + EPS)
            xh = x * rstd
            dxh = dn * g
            return rstd * (dxh - xh * jnp.mean(dxh * xh, axis=-1, keepdims=True)), jnp.sum(dn * xh, axis=0, keepdims=True)

        def finish(b, carry):
            r0 = pl.multiple_of(b * NORM_ROWS, NORM_ROWS)
            rows = pl.ds(r0, NORM_ROWS)
            dq, dgq = norm_bwd(q_ref[rows, :], gqv, dqn_ref[rows, :])
            dk, dgk = norm_bwd(k_ref[rows, :], gkv, dkp_ref[pl.ds(PAD_ROWS + r0, NORM_ROWS), :])
            stage_ref[0, rows, :] = dq.astype(stage_ref.dtype)
            stage_ref[1, rows, :] = dk.astype(stage_ref.dtype)
            stage_ref[2, rows, :] = dvp_ref[pl.ds(PAD_ROWS + r0, NORM_ROWS), :].astype(stage_ref.dtype)
            dgq_ref[...] += dgq
            dgk_ref[...] += dgk
            return carry

        lax.fori_loop(0, s // NORM_ROWS, finish, 0)
        for cp in writes():
            cp.start()

        @pl.when(head == HEADS - 1)
        def _():
            for cp in writes():
                cp.wait()

    def col(off):
        return pl.BlockSpec((s, HEAD_DIM), lambda h: (0, off + h))

    vec = pl.BlockSpec((1, HEAD_DIM), lambda h: (0, 0))
    hbias = pl.BlockSpec((None, GROUP, WIN), lambda h: (h, 0, 0))
    res = _pallas(
        _with_riders(body, 8, 4, 7, rs, (HEADS,)), name="attention_bwd", grid=(HEADS,),
        in_specs=[col(ATT_COL0), col(ATT_COL0 + HEADS), col(ATT_COL0 + 2 * HEADS), vec, vec, hbias, col(HEADS), ANY,
                  *rs.in_specs],
        out_specs=[ANY, vec, vec, hbias, *rs.out_specs],
        out_shape=[_sds(dproj.shape, dproj.dtype), _sds((1, HEAD_DIM), F32), _sds((1, HEAD_DIM), F32),
                   _sds((HEADS, GROUP, WIN), F32), *rs.out_shapes],
        input_output_aliases={7: 0, **rs.aliases},
        scratch_shapes=[pltpu.VMEM((s + PAD_ROWS, HEAD_DIM), MXU_DTYPE), pltpu.VMEM((s + PAD_ROWS, HEAD_DIM), MXU_DTYPE),
                        pltpu.VMEM((s + PAD_ROWS, HEAD_DIM), F32), pltpu.VMEM((s + PAD_ROWS, HEAD_DIM), F32),
                        pltpu.VMEM((s, HEAD_DIM), F32), pltpu.VMEM((3, s, HEAD_DIM), BF16),
                        pltpu.SemaphoreType.DMA((3,)), *rs.scratch],
        compiler_params=_params("arbitrary"),
    )(proj, proj, proj, gq, gk, bias, dmix, dproj, *rs.arrays)
    return res[:4], rs.split(res[4:])


DIAG_SPLIT = (BAND + WIN - CHUNK) // 2


def _diag_bin(m):
    t = jnp.where(m < DIAG_SPLIT, m, m - WIN)
    return jnp.clip(LEFT_CHUNKS * CHUNK - t, -(CHUNK - 1), REL_CLIP) + (CHUNK - 1)


def _skew_rows(a, left):
    row = lax.broadcasted_iota(jnp.int32, (GROUP, WIN), 0)
    for b in range(GROUP.bit_length() - 1):
        step = 1 << b
        a = jnp.where(jnp.bitwise_and(row, step) != 0, pltpu.roll(a, WIN - step if left else step, 1), a)
    return a


def _rel_bias_expand(rel_bias):
    def body(rb_ref, o_ref):
        onehot = (lax.broadcasted_iota(jnp.int32, (REL_SIZE, WIN), 0)
                  == _diag_bin(lax.broadcasted_iota(jnp.int32, (REL_SIZE, WIN), 1))).astype(MXU_DTYPE)
        rest = jnp.broadcast_to(rb_ref[...], (8, REL_SIZE))
        per_diag = jnp.zeros((8, WIN), F32)
        for _ in range(3):
            piece = rest.astype(BF16)
            per_diag = per_diag + _dot(piece, onehot, NN)
            rest = rest - piece.astype(F32)
        table = _skew_rows(jnp.broadcast_to(per_diag[0:1], (GROUP, WIN)), left=False)
        row_chunk = lax.broadcasted_iota(jnp.int32, (GROUP, WIN), 0) // CHUNK
        col_chunk = lax.broadcasted_iota(jnp.int32, (GROUP, WIN), 1) // CHUNK
        in_band = jnp.logical_and(col_chunk >= row_chunk, col_chunk <= row_chunk + LEFT_CHUNKS)
        o_ref[...] = jnp.where(in_band, table, MASKED)

    return _pallas(body, name="rel_bias_expand", grid=(HEADS,),
                   in_specs=[pl.BlockSpec((None, 1, REL_SIZE), lambda h: (h, 0, 0))],
                   out_specs=pl.BlockSpec((None, GROUP, WIN), lambda h: (h, 0, 0)),
                   out_shape=_sds((HEADS, GROUP, WIN), F32), compiler_params=_params("parallel"))(rel_bias)


def _rel_bias_fold(dbias):
    def body(a_ref, o_ref):
        diag = jnp.sum(_skew_rows(a_ref[...], left=True), axis=0, keepdims=True)
        onehot = (_diag_bin(lax.broadcasted_iota(jnp.int32, (WIN, REL_SIZE), 0))
                  == lax.broadcasted_iota(jnp.int32, (WIN, REL_SIZE), 1)).astype(MXU_DTYPE)
        rest = jnp.broadcast_to(diag, (8, WIN))
        out = jnp.zeros((8, REL_SIZE), F32)
        for _ in range(3):
            piece = rest.astype(BF16)
            out = out + _dot(piece, onehot, NN)
            rest = rest - piece.astype(F32)
        o_ref[...] = out[0:1]

    return _pallas(body, name="rel_bias_fold", grid=(HEADS,),
                   in_specs=[pl.BlockSpec((None, GROUP, WIN), lambda h: (h, 0, 0))],
                   out_specs=pl.BlockSpec((None, 1, REL_SIZE), lambda h: (h, 0, 0)),
                   out_shape=_sds((HEADS, 1, REL_SIZE), F32), compiler_params=_params("parallel"))(dbias)


def _place():
    return lax.axis_index("x"), lax.axis_index("y"), lax.axis_index("c")


def _other_chips(x, y):
    return [(1 - x, y), (x, 1 - y), (1 - x, 1 - y)]


class _Rider:
    reads, ins, new, n_sems = (), (), (), 1

    def start(self, reads, ins, new, send, recv):
        pass

    def mid(self, reads, ins, new, send, recv):
        pass

    def late(self, reads, ins, new, send, recv):
        pass

    def end(self, reads, ins, new, send, recv):
        pass


class _Riders:
    def __init__(self, riders, n_host_in, n_host_out):
        self.riders = list(riders)
        self.arrays, self.out_shapes, self.aliases, self.scratch = [], [], {}, []
        for r in self.riders:
            for t, a in enumerate(r.ins):
                self.aliases[n_host_in + len(self.arrays) + len(r.reads) + t] = n_host_out + len(self.out_shapes) + t
            self.arrays += [*r.reads, *r.ins]
            self.out_shapes += [_sds(a.shape, a.dtype) for a in r.ins] + list(r.new)
            self.scratch += [pltpu.SemaphoreType.DMA((r.n_sems,)), pltpu.SemaphoreType.DMA((r.n_sems,))]
        self.in_specs = [ANY] * len(self.arrays)
        self.out_specs = [ANY] * len(self.out_shapes)

    def bind(self, in_refs, out_refs, scratch_refs):
        bound, i, o = [], 0, 0
        for t, r in enumerate(self.riders):
            reads = in_refs[i:i + len(r.reads)]
            i += len(r.reads) + len(r.ins)
            ins = out_refs[o:o + len(r.ins)]
            new = out_refs[o + len(r.ins):o + len(r.ins) + len(r.new)]
            o += len(r.ins) + len(r.new)
            bound.append((reads, ins, new, scratch_refs[2 * t], scratch_refs[2 * t + 1]))
        return bound

    def run(self, phase, bound):
        for r, b in zip(self.riders, bound):
            getattr(r, phase)(*b)

    def split(self, outs):
        res, o = [], 0
        for r in self.riders:
            n = len(r.ins) + len(r.new)
            res.append(list(outs[o:o + n]))
            o += n
        return res


def _run_riders(name, riders):
    rs = _Riders(riders, 0, 0)
    n_in, n_out = len(rs.arrays), len(rs.out_shapes)

    def body(*refs):
        bound = rs.bind(refs[:n_in], refs[n_in:n_in + n_out], refs[n_in + n_out:])
        rs.run("start", bound)
        rs.run("mid", bound)
        rs.run("late", bound)
        rs.run("end", bound)

    outs = _pallas(body, name=name, in_specs=rs.in_specs, out_specs=rs.out_specs, out_shape=rs.out_shapes,
                   input_output_aliases=rs.aliases, scratch_shapes=rs.scratch)(*rs.arrays)
    return rs.split(outs)


class _GatherRider(_Rider):
    X_LINK, Y_LINK, Y_PASS, X_PASS, D2D_X, D2D_Y, D2D_DIAG, N_SEMS = 0, 1, 2, 3, 4, 5, 6, 7

    def __init__(self, blocks, part=(0, 1, 1)):
        self.ins = tuple(blocks)
        self.part = part
        self.n_sems = self.N_SEMS * len(blocks)

    def _copy(self, out, send, recv, w, sem, chip_from, cc, to, sub=None):
        hr = self.ins[w].shape[1] // 2
        lo, hi, n = self.part
        first, size = cc * hr + lo * (hr // n), (hi - lo) * (hr // n)
        if sub is not None:
            size //= 2
            first += sub * size
        piece = out[w].at[2 * chip_from[0] + chip_from[1], pl.ds(first, size), :]
        return pltpu.make_async_remote_copy(src_ref=piece, dst_ref=piece, send_sem=send.at[self.N_SEMS * w + sem],
                                            recv_sem=recv.at[self.N_SEMS * w + sem], device_id=to, device_id_type=MESH)

    def _sent(self, out, send, recv, w):
        x, y, c = _place()
        me, sib = (x, y), (x, y, 1 - c)
        xn, yn, diag = _other_chips(x, y)
        cp = functools.partial(self._copy, out, send, recv, w)
        return [("start", cp(self.X_LINK, me, c, (*xn, c))), ("start", cp(self.Y_LINK, me, c, (*yn, c))),
                ("mid_x", cp(self.D2D_X, xn, c, sib)), ("mid_x", cp(self.Y_PASS, xn, c, (*yn, c), sub=0)),
                ("mid_y", cp(self.D2D_Y, yn, c, sib)), ("mid_y", cp(self.X_PASS, yn, c, (*xn, c), sub=1)),
                ("late", cp(self.D2D_DIAG, diag, c, sib))]

    def _go(self, out, send, recv, phase):
        for w in range(len(self.ins)):
            for ph, copy in self._sent(out, send, recv, w):
                if ph == phase:
                    copy.start()

    def start(self, reads, out, new, send, recv):
        self._go(out, send, recv, "start")

    def mid(self, reads, out, new, send, recv):
        x, y, c = _place()
        xn, yn, _ = _other_chips(x, y)
        for w in range(len(self.ins)):
            self._copy(out, send, recv, w, self.X_LINK, xn, c, (x, y, c)).wait_recv()
        self._go(out, send, recv, "mid_x")
        for w in range(len(self.ins)):
            self._copy(out, send, recv, w, self.Y_LINK, yn, c, (x, y, c)).wait_recv()
        self._go(out, send, recv, "mid_y")

    def late(self, reads, out, new, send, recv):
        x, y, c = _place()
        diag = _other_chips(x, y)[2]
        for w in range(len(self.ins)):
            self._copy(out, send, recv, w, self.Y_PASS, diag, c, (x, y, c), sub=0).wait_recv()
            self._copy(out, send, recv, w, self.X_PASS, diag, c, (x, y, c), sub=1).wait_recv()
        self._go(out, send, recv, "late")

    def end(self, reads, out, new, send, recv):
        x, y, c = _place()
        xn, yn, diag = _other_chips(x, y)
        for w in range(len(self.ins)):
            for sem, chip in ((self.D2D_X, xn), (self.D2D_Y, yn), (self.D2D_DIAG, diag)):
                self._copy(out, send, recv, w, sem, chip, 1 - c, (x, y, c)).wait_recv()
        for w in range(len(self.ins)):
            for _, copy in self._sent(out, send, recv, w):
                copy.wait_send()


class _SwapRider(_Rider):
    def __init__(self, grads):
        self.reads = tuple(grads)
        self.new = tuple(_sds((N_CHIPS, g.shape[1] // 2, g.shape[2]), g.dtype) for g in grads)
        self.n_sems = len(grads)

    def _copies(self, src, new, send, recv):
        x, y, c = _place()
        copies = []
        for w in range(len(self.reads)):
            hr = self.reads[w].shape[1] // 2
            copies.append(pltpu.make_async_remote_copy(
                src_ref=src[w].at[:, pl.ds((1 - c) * hr, hr), :], dst_ref=new[w],
                send_sem=send.at[w], recv_sem=recv.at[w], device_id=(x, y, 1 - c), device_id_type=MESH))
        return copies

    def start(self, src, ins, new, send, recv):
        for cp in self._copies(src, new, send, recv):
            cp.start()

    def end(self, src, ins, new, send, recv):
        for cp in self._copies(src, new, send, recv):
            cp.wait()


def _add_half(g, got, c_arr, name):
    nk, r, cols = g.shape
    hr = r // 2
    tr = min(hr, 1024)
    nb = hr // tr

    def body(c_ref, g_ref, got_ref, o_ref):
        o_ref[...] = (g_ref[...].astype(F32) + got_ref[...].astype(F32)).astype(o_ref.dtype)

    grid_spec = pltpu.PrefetchScalarGridSpec(
        num_scalar_prefetch=1, grid=(nk, nb),
        in_specs=[pl.BlockSpec((None, tr, cols), lambda k, i, c_ref: (k, c_ref[0] * nb + i, 0)),
                  pl.BlockSpec((None, tr, cols), lambda k, i, c_ref: (k, i, 0))],
        out_specs=pl.BlockSpec((None, tr, cols), lambda k, i, c_ref: (k, i, 0)))
    return _pallas(body, name=name, grid_spec=grid_spec, out_shape=_sds((nk, hr, cols), g.dtype),
                   compiler_params=_params("parallel", "parallel"))(c_arr, g, got)


class _SendPartialsRider(_Rider):
    def __init__(self, parts, got=None, part=(0, 1, 1)):
        self.reads = tuple(parts)
        if got is None:
            self.new = tuple(_sds((N_CHIPS - 1, *p.shape[1:]), p.dtype) for p in parts)
        else:
            self.ins = tuple(got)
        self.part = part
        self.n_sems = 3 * len(parts)

    def _copies(self, src, ins, new, send, recv):
        x, y, c = _place()
        land = ins if self.ins else new
        lo, hi, n = self.part
        copies = []
        for w in range(len(self.reads)):
            pr = self.reads[w].shape[1] // n
            rows = pl.ds(lo * pr, (hi - lo) * pr)
            for j, chip in enumerate(_other_chips(x, y)):
                copies.append(pltpu.make_async_remote_copy(
                    src_ref=src[w].at[2 * chip[0] + chip[1], rows, :], dst_ref=land[w].at[j, rows, :],
                    send_sem=send.at[3 * w + j], recv_sem=recv.at[3 * w + j], device_id=(*chip, c), device_id_type=MESH))
        return copies

    def start(self, src, ins, new, send, recv):
        for cp in self._copies(src, ins, new, send, recv):
            cp.start()

    def end(self, src, ins, new, send, recv):
        for cp in self._copies(src, ins, new, send, recv):
            cp.wait()


def _sum_partials(part, got, kc_arr, name):
    _, hr, cols = part.shape
    tr = min(hr, 512)
    nb = hr // tr

    def body(kc_ref, p_ref, g0_ref, g1_ref, g2_ref, o_ref):
        o_ref[...] = ((p_ref[...].astype(F32) + g0_ref[...].astype(F32)) + g1_ref[...].astype(F32)) + g2_ref[...].astype(F32)

    slot = lambda j: pl.BlockSpec((None, tr, cols), lambda i, kc_ref: (j, i, 0))
    grid_spec = pltpu.PrefetchScalarGridSpec(
        num_scalar_prefetch=1, grid=(nb,),
        in_specs=[pl.BlockSpec((None, tr, cols), lambda i, kc_ref: (kc_ref[0], i, 0)), slot(0), slot(1), slot(2)],
        out_specs=pl.BlockSpec((tr, cols), lambda i, kc_ref: (kc_ref[1] * nb + i, 0)))
    return _pallas(body, name=name, grid_spec=grid_spec, out_shape=_sds((2 * hr, cols), F32),
                   compiler_params=_params("parallel"))(kc_arr, part, got, got, got)


class _ShareRider(_Rider):
    def __init__(self, grads):
        self.ins = tuple(grads)
        self.n_sems = len(grads)

    def _copies(self, out, send, recv):
        x, y, c = _place()
        copies = []
        for w in range(len(self.ins)):
            hr = self.ins[w].shape[0] // 2
            mine = out[w].at[pl.ds(c * hr, hr), :]
            copies.append(pltpu.make_async_remote_copy(
                src_ref=mine, dst_ref=mine, send_sem=send.at[w], recv_sem=recv.at[w],
                device_id=(x, y, 1 - c), device_id_type=MESH))
        return copies

    def start(self, reads, out, new, send, recv):
        for cp in self._copies(out, send, recv):
            cp.start()

    def end(self, reads, out, new, send, recv):
        for cp in self._copies(out, send, recv):
            cp.wait()


def _small_allreduce_adamw(g_part, w, m, v, riders=()):
    rows = g_part.shape[0]
    rs = _Riders(riders, 4, 4)
    n_rin, n_rout = len(rs.arrays), len(rs.out_shapes)

    def body(*refs):
        g_ref, w_ref, m_ref, v_ref = refs[:4]
        go_ref, d_ref, nm_ref, nv_ref = refs[4 + n_rin:8 + n_rin]
        all_ref, send_sems, recv_sems = refs[8 + n_rin + n_rout:11 + n_rin + n_rout]
        bound = rs.bind(refs[4:4 + n_rin], refs[8 + n_rin:8 + n_rin + n_rout], refs[11 + n_rin + n_rout:])
        rs.run("start", bound)
        x, y, c = _place()
        me = 4 * x + 2 * y + c
        all_ref[me] = g_ref[...]
        copies = []
        for r in range(1, 8):
            dx, dy, dc = (r >> 2) & 1, (r >> 1) & 1, r & 1
            peer = (1 - x if dx else x, 1 - y if dy else y, 1 - c if dc else c)
            copies.append(pltpu.make_async_remote_copy(
                src_ref=g_ref, dst_ref=all_ref.at[me], send_sem=send_sems.at[r - 1], recv_sem=recv_sems.at[r - 1],
                device_id=peer, device_id_type=MESH))
        for cp in copies:
            cp.start()
        for cp in copies:
            cp.wait()
        tot = all_ref[0]
        for d in range(1, 8):
            tot = tot + all_ref[d]
        go_ref[...] = tot
        d_ref[...], nm_ref[...], nv_ref[...] = _adamw_math(w_ref[...], tot, m_ref[...], v_ref[...])
        for phase in ("mid", "late", "end"):
            rs.run(phase, bound)

    vm = pl.BlockSpec(memory_space=pltpu.VMEM)
    out = _pallas(
        body, name="small_allreduce_adamw", in_specs=[vm] * 4 + rs.in_specs, out_specs=[vm] * 4 + rs.out_specs,
        out_shape=[_sds((rows, 128), F32)] * 4 + rs.out_shapes, input_output_aliases=rs.aliases,
        scratch_shapes=[pltpu.VMEM((8, rows, 128), F32), pltpu.SemaphoreType.DMA((7,)), pltpu.SemaphoreType.DMA((7,)),
                        *rs.scratch],
    )(g_part, w, m, v, *rs.arrays)
    return out[:4], rs.split(out[4:])


SMALL_SIZES = (2048, 1024, 128, 128, HEADS * REL_SIZE, 2048)
SMALL_PART_ROWS = tuple(-(-size // 1024) * 8 for size in SMALL_SIZES)
SMALL_ROWS = sum(SMALL_PART_ROWS)


def _pack_small(parts):
    rows = []
    for p, size, nr in zip(parts, SMALL_SIZES, SMALL_PART_ROWS):
        rows.append(jnp.pad(p.reshape(-1), (0, nr * 128 - size)).reshape(nr, 128))
    return jnp.concatenate(rows, axis=0)


def _unpack_small(slab, shapes):
    out, off = [], 0
    for size, nr, shape in zip(SMALL_SIZES, SMALL_PART_ROWS, shapes):
        out.append(slab[off:off + nr].reshape(-1)[:size].reshape(shape))
        off += nr
    return out


def kernel(x, norm1_g, w_in, ret_norm_g, q_norm_g, k_norm_g, rel_bias, w_out, norm2_g, w_ff1, w_ff2, loss_target, m_norm1_g, m_w_in, m_ret_norm_g, m_q_norm_g, m_k_norm_g, m_rel_bias, m_w_out, m_norm2_g, m_w_ff1, m_w_ff2, v_norm1_g, v_w_in, v_ret_norm_g, v_q_norm_g, v_k_norm_g, v_rel_bias, v_w_out, v_norm2_g, v_w_ff1, v_w_ff2):
    xs = x[0]
    tgt = loss_target[0]
    s, d = xs.shape
    d_in = N_CHIPS * w_in.shape[2]
    d_ff = N_CHIPS * w_ff1.shape[2]
    in_sh, ff_sh = w_in.shape[2], w_ff1.shape[2]
    tm = min(s, 1024)
    gi = s // tm
    c_arr = lax.axis_index("c").astype(jnp.int32).reshape(1)
    k_arr = (2 * lax.axis_index("x") + lax.axis_index("y")).astype(jnp.int32).reshape(1)
    tables = _tables(s)
    bias = _rel_bias_expand(rel_bias[0][:, None, :])

    blk_in = [_cast_bf16(w_in[0], k_arr, "cast_w_in_%d" % half, cols=(half, 2)) for half in range(2)]
    blk_out, blk_ff1, blk_ff2 = (_cast_bf16(w_out[0], k_arr, "cast_w_out"), _cast_bf16(w_ff1[0], k_arr, "cast_w_ff1"),
                                 _cast_bf16(w_ff2[0], k_arr, "cast_w_ff2"))

    h1, ((wg_in0,),) = _rmsnorm_fwd(xs, norm1_g, "rmsnorm1", riders=[_GatherRider([blk_in[0]])])
    tn_in = in_sh // 2
    tk = d

    def proj_half(half, wg, through, riders):
        return _mm("proj_%d" % half, h1, wg, NN, (gi, N_CHIPS, 1),
                   pl.BlockSpec((tm, tk), lambda i, j, k: (i, 0)), pl.BlockSpec((None, tk, tn_in), lambda i, j, k: (j, 0, 0)),
                   [_sds((s, d_in), F32)], [pl.BlockSpec((tm, tn_in), lambda i, j, k: (i, 2 * j + half))], (tm, tn_in),
                   riders=riders, through=through)

    (proj,), ((wg_in1,),) = proj_half(0, wg_in0, None, [_GatherRider([blk_in[1]])])
    (proj,), ((wg_ff1,),) = proj_half(1, wg_in1, proj, [_GatherRider([blk_ff1], (0, 3, 8))])
    (mix, y_ret, prev), ((wg_ff1,),) = _retention_fwd(proj, ret_norm_g, tables, riders=[_GatherRider([wg_ff1], (3, 6, 8))])
    mix, ((wg_out,), (wg_ff2,)) = _attention_fwd(
        proj, q_norm_g, k_norm_g, bias, mix, riders=[_GatherRider([blk_out]), _GatherRider([blk_ff2], (0, 1, 4))])
    wg_out = wg_out.reshape(d, d)
    tn = 1024
    tile = pl.BlockSpec((tm, tn), lambda i, j, k: (i, j))
    def residual_norm(acc, res, g):
        x1v = res + acc
        yv = x1v * lax.rsqrt(jnp.mean(x1v * x1v, axis=-1, keepdims=True) + EPS)
        return x1v, yv * g

    tmo = min(s, 512)
    rows = pl.BlockSpec((tmo, d), lambda i, j, k: (i, 0))
    (x1, h2), ((wg_ff1,),) = _mm(
        "out_proj", mix, wg_out, NN, (s // tmo, 1, 1),
        rows, pl.BlockSpec((d, d), lambda i, j, k: (0, 0)),
        [_sds((s, d), F32), _sds((s, d), BF16)], [rows, rows], (tmo, d),
        extras=(xs, norm2_g), extra_specs=(rows, pl.BlockSpec((1, d), lambda i, j, k: (0, 0))),
        epi=residual_norm, riders=[_GatherRider([wg_ff1], (6, 8, 8))])
    tn_ff = min(ff_sh, 1024)
    per = ff_sh // tn_ff

    def relu2(acc):
        r = jnp.maximum(acc, 0.0)
        return acc, r * r

    (u, act), ((wg_ff2,),) = _mm(
        "ff1", h2, wg_ff1, NN, (gi, N_CHIPS * per, d // tk),
        pl.BlockSpec((tm, tk), lambda i, j, k: (i, k)),
        pl.BlockSpec((None, tk, tn_ff), lambda i, j, k: (j // per, k, j % per)),
        [_sds((s, d_ff), F32), _sds((s, d_ff), BF16)],
        [pl.BlockSpec((tm, tn_ff), lambda i, j, k: (i, j))] * 2, (tm, tn_ff), epi=relu2,
        riders=[_GatherRider([wg_ff2], (1, 4, 4))])
    wg_ff2 = wg_ff2.reshape(d_ff, d)

    def loss_epi(acc, res, t):
        diff = (res + acc) - t
        dy = diff / d
        return dy, dy, jnp.sum(diff * diff, axis=0, keepdims=True)

    tk2 = min(tk, 2048)
    dy, dyb, loss_cols = _mm(
        "ff2_loss", act, wg_ff2, NN, (gi, d // tn, d_ff // tk2),
        pl.BlockSpec((tm, tk2), lambda i, j, k: (i, k)), pl.BlockSpec((tk2, tn), lambda i, j, k: (k, j)),
        [_sds((s, d), F32), _sds((s, d), BF16), _sds((gi, 1, d), F32)],
        [tile, tile, pl.BlockSpec((None, 1, tn), lambda i, j, k: (i, 0, j))], (tm, tn),
        extras=(x1, tgt), extra_specs=(tile, tile), epi=loss_epi)
    loss = lax.psum(0.5 * jnp.sum(loss_cols) / d, ("x", "y", "c"))

    (du,) = _mm("d_act", dyb, wg_ff2, NT, (gi, d_ff // tn, d // tk),
                pl.BlockSpec((tm, tk), lambda i, j, k: (i, k)), pl.BlockSpec((tn, tk), lambda i, j, k: (j, k)),
                [_sds((s, d_ff), BF16)], [tile], (tm, tn), extras=(u,), extra_specs=(tile,),
                epi=lambda acc, uu: (acc * (2.0 * jnp.maximum(uu, 0.0)),))
    ts = min(s, 2048)
    wtile = pl.BlockSpec((tn, tn), lambda i, j, k: (i, j))
    (g_ff2,) = _mm("dw_ff2", act, dyb, TN, (d_ff // tn, d // tn, s // ts),
                   pl.BlockSpec((ts, tn), lambda i, j, k: (k, i)), pl.BlockSpec((ts, tn), lambda i, j, k: (k, j)),
                   [_sds((d_ff, d), BF16)], [wtile], (tn, tn))
    g_ff2 = g_ff2.reshape(N_CHIPS, d_ff // N_CHIPS, d)
    (g_ff1,), ((got_ff2,),) = _mm(
        "dw_ff1", h2, du, TN, (d // tn, N_CHIPS * per, s // ts),
        pl.BlockSpec((ts, tn), lambda i, j, k: (k, i)), pl.BlockSpec((ts, tn_ff), lambda i, j, k: (k, j)),
        [_sds((N_CHIPS, d, ff_sh), BF16)],
        [pl.BlockSpec((None, tn, tn_ff), lambda i, j, k: (j // per, i, j % per))], (tn, tn_ff),
        riders=[_SwapRider([g_ff2])])
    p_ff2 = _add_half(g_ff2, got_ff2, c_arr, "chip_partial_w_ff2")
    tkf = min(tk, ff_sh)
    kper = ff_sh // tkf
    (dh2,), ((got2_ff2,), (got_ff1,)) = _mm(
        "d_h2", du, wg_ff1, NT, (gi, d // tn, d_ff // tkf),
        pl.BlockSpec((tm, tkf), lambda i, j, k: (i, k)),
        pl.BlockSpec((None, tn, tkf), lambda i, j, k: (k // kper, j, k % kper)),
        [_sds((s, d), F32)], [tile], (tm, tn),
        riders=[_SendPartialsRider([p_ff2], part=(0, 3, 4)), _SwapRider([g_ff1])])
    p_ff1 = _add_half(g_ff1, got_ff1, c_arr, "chip_partial_w_ff1")
    dx1, dx1b, g_norm2 = _rmsnorm_bwd(x1, norm2_g, dh2, dy, "rmsnorm2_bwd")

    (dmix,) = _mm("d_mix", dx1b, wg_out, NT, (gi, d // tn, d // tk),
                  pl.BlockSpec((tm, tk), lambda i, j, k: (i, k)), pl.BlockSpec((tn, tk), lambda i, j, k: (j, k)),
                  [_sds((s, d), F32)], [tile], (tm, tn))
    (g_out,) = _mm("dw_out", mix, dx1b, TN, (d // tn, d // tn, s // ts),
                   pl.BlockSpec((ts, tn), lambda i, j, k: (k, i)), pl.BlockSpec((ts, tn), lambda i, j, k: (k, j)),
                   [_sds((d, d), BF16)], [wtile], (tn, tn))
    g_out = g_out.reshape(N_CHIPS, d // N_CHIPS, d)
    (dproj, g_gn), ((got2_ff2,), (got2_ff1,), (got_out,)) = _retention_bwd(
        proj, ret_norm_g, tables, y_ret, prev, dmix,
        riders=[_SendPartialsRider([p_ff2], got=[got2_ff2], part=(3, 4, 4)), _SendPartialsRider([p_ff1], part=(0, 2, 4)),
                _SwapRider([g_out])])
    p_out = _add_half(g_out, got_out, c_arr, "chip_partial_w_out")
    (dproj, g_gq, g_gk, dbias), ((got2_ff1,), (got2_out,)) = _attention_bwd(
        proj, q_norm_g, k_norm_g, bias, dmix, dproj,
        riders=[_SendPartialsRider([p_ff1], got=[got2_ff1], part=(2, 4, 4)), _SendPartialsRider([p_out])])
    g_rel = _rel_bias_fold(dbias)
    names = ["w_in", "w_out", "w_ff1", "w_ff2"]
    kc_arr = jnp.concatenate([k_arr, c_arr])
    early = [_sum_partials(p, r, kc_arr, "sum_partials_" + nm)
             for p, r, nm in zip((p_out, p_ff1, p_ff2), (got2_out, got2_ff1, got2_ff2), names[1:])]
    (g_in,), (early,) = _mm(
        "dw_in", h1, dproj, TN, (d // tn, 2 * N_CHIPS, s // ts),
        pl.BlockSpec((ts, tn), lambda i, j, k: (k, i)), pl.BlockSpec((ts, tn_in), lambda i, j, k: (k, j)),
        [_sds((N_CHIPS, d, in_sh), BF16)],
        [pl.BlockSpec((None, tn, tn_in), lambda i, j, k: (j // 2, i, j % 2))], (tn, tn_in), riders=[_ShareRider(early)])
    ((got_in,),) = _run_riders("grad_swap_w_in", [_SwapRider([g_in])])
    p_in = _add_half(g_in, got_in, c_arr, "chip_partial_w_in")
    half_spec = pl.BlockSpec((None, tn, tn_in), lambda i, j, k: (k // 2, j, 0))
    (dh1,), ((got2_in,),) = _mm(
        "d_h1", dproj, [wg_in0, wg_in1], NT, (gi, d // tn, 2 * N_CHIPS),
        pl.BlockSpec((tm, tn_in), lambda i, j, k: (i, k)), [half_spec, half_spec],
        [_sds((s, d), F32)], [tile], (tm, tn), riders=[_SendPartialsRider([p_in])])
    grad_x, _, g_norm1 = _rmsnorm_bwd(xs, norm1_g, dh1, dx1, "rmsnorm1_bwd")

    small_w = (norm1_g, ret_norm_g, q_norm_g, k_norm_g, rel_bias, norm2_g)
    small_m = (m_norm1_g, m_ret_norm_g, m_q_norm_g, m_k_norm_g, m_rel_bias, m_norm2_g)
    small_v = (v_norm1_g, v_ret_norm_g, v_q_norm_g, v_k_norm_g, v_rel_bias, v_norm2_g)
    shapes = [p.shape for p in small_w]
    g_small = _pack_small([g_norm1, g_gn, g_gq, g_gk, g_rel, g_norm2])
    small_out, ((g_w_in,),) = _small_allreduce_adamw(
        g_small, _pack_small(small_w), _pack_small(small_m), _pack_small(small_v),
        riders=[_ShareRider([_sum_partials(p_in, got2_in, kc_arr, "sum_partials_w_in")])])
    sg, sd, sm, sv = (_unpack_small(a, shapes) for a in small_out)

    g_big = [g_w_in, *early]
    big = []
    for g, w, m, v, nm in zip(g_big, (w_in, w_out, w_ff1, w_ff2), (m_w_in, m_w_out, m_w_ff1, m_w_ff2),
                              (v_w_in, v_w_out, v_w_ff1, v_w_ff2), names):
        g, delta, new_m, new_v = _adamw(w[0], g, m[0], v[0], "adamw_" + nm)
        big.append((g[None], delta[None], new_m[None], new_v[None]))

    def ordered(kind):
        sm_ = (sg, sd, sm, sv)[kind]
        return (sm_[0], big[0][kind], sm_[1], sm_[2], sm_[3], sm_[4], big[1][kind], sm_[5], big[2][kind], big[3][kind])

    return (loss, grad_x[None], *ordered(0), *ordered(1), *ordered(2), *ordered(3))
```

```python
import functools

import jax
import jax.numpy as jnp
from jax import lax
from jax.experimental import pallas as pl
from jax.experimental.pallas import tpu as pltpu

F32 = jnp.float32
BF16 = jnp.bfloat16
MXU_DTYPE = jnp.bfloat16

CHUNK = 64
HEADS = 8
HEAD_DIM = 128
LEFT_CHUNKS = 8
BAND = (LEFT_CHUNKS + 1) * CHUNK
REL_CLIP = 128
REL_SIZE = (CHUNK - 1) + REL_CLIP + 1
RET_BLOCK_CHUNKS = 8
RET_ROWS = RET_BLOCK_CHUNKS * CHUNK
RET_SUB = 256
ROPE_BASE = 10000.0
EPS = 1e-6
GN_EPS = 1e-5
ADAM_LR, ADAM_B1, ADAM_B2, ADAM_EPS, ADAM_WD, ADAM_STEP = 0.001, 0.9, 0.999, 1e-08, 0.01, 10
N_CHIPS = 4
VMEM_LIMIT = 56 * 1024 * 1024
MESH = pl.DeviceIdType.MESH
ANY = pl.BlockSpec(memory_space=pl.ANY)

NN = (((1,), (0,)), ((), ()))
NT = (((1,), (1,)), ((), ()))
TN = (((0,), (0,)), ((), ()))


def _pallas(body, **kw):
    return pl.pallas_call(body, **kw)


def _params(*sem):
    return pltpu.CompilerParams(dimension_semantics=sem, vmem_limit_bytes=VMEM_LIMIT)


def _dot(a, b, dims):
    return lax.dot_general(a.astype(MXU_DTYPE), b.astype(MXU_DTYPE), dims, preferred_element_type=F32)


RIDER_MID, RIDER_LATE = 0.5, 0.8


def _mm(name, a, b, dims, grid, a_spec, b_spec, outs, o_specs, acc_shape, extras=(), extra_specs=(), epi=None,
        riders=(), through=None):
    ni, nj, nk = grid
    n_ex, n_out = len(extras), len(outs)
    bs = list(b) if isinstance(b, (list, tuple)) else [b]
    b_specs = list(b_spec) if isinstance(b, (list, tuple)) else [b_spec]
    extras, extra_specs = (*bs[1:], *extras), (*b_specs[1:], *extra_specs)
    b, b_spec, n_b = bs[0], b_specs[0], len(bs)
    n_in = 1 + n_b + n_ex
    rs = _Riders(riders, n_in, n_out)
    n_rin, n_rout = len(rs.arrays), len(rs.out_shapes)
    steps = ni * nj * nk

    held = [] if through is None else [through]

    def body(*refs):
        a_ref, b_refs = refs[0], refs[1:1 + n_b]
        b_ref = b_refs[0]
        ex_refs = refs[1 + n_b:n_in]
        outs_at = n_in + n_rin + len(held)
        o_refs = refs[outs_at:outs_at + n_out]
        acc_ref = refs[outs_at + n_out + n_rout]
        k = pl.program_id(2)
        if riders:
            bound = rs.bind(refs[n_in:n_in + n_rin], refs[outs_at + n_out:outs_at + n_out + n_rout],
                            refs[outs_at + n_out + n_rout + 1:])
            step = (pl.program_id(0) * nj + pl.program_id(1)) * nk + k
            pl.when(step == 0)(lambda: rs.run("start", bound))
            pl.when(step == int(steps * RIDER_MID))(lambda: rs.run("mid", bound))
            pl.when(step == int(steps * RIDER_LATE))(lambda: rs.run("late", bound))

        def finish(acc):
            vals = epi(acc, *[r[...] for r in ex_refs]) if epi is not None else (acc,)
            for r, v in zip(o_refs, vals):
                r[...] = v.astype(r.dtype)

        if nk == 1:
            finish(_dot(a_ref[...], b_ref[...], dims))
        else:
            @pl.when(k == 0)
            def _():
                acc_ref[...] = jnp.zeros_like(acc_ref)

            for t, ref in enumerate(b_refs):
                def step_with(ref=ref):
                    acc_ref[...] += _dot(a_ref[...], ref[...], dims)

                if n_b == 1:
                    step_with()
                else:
                    pl.when(k % n_b == t)(step_with)
            pl.when(k == nk - 1)(lambda: finish(acc_ref[...]))

        if riders:
            pl.when(step == steps - 1)(lambda: rs.run("end", bound))

    res = _pallas(
        body, name=name, grid=grid, in_specs=[a_spec, b_spec, *extra_specs, *rs.in_specs, *[ANY for _ in held]],
        out_specs=[*o_specs, *rs.out_specs], out_shape=[*outs, *rs.out_shapes],
        input_output_aliases={**rs.aliases, **{n_in + n_rin: 0 for _ in held}},
        scratch_shapes=[pltpu.VMEM(acc_shape if nk > 1 else (8, 128), F32), *rs.scratch],
        compiler_params=_params(*(("arbitrary",) * 3 if riders else ("parallel", "parallel", "arbitrary"))),
    )(a, b, *extras, *rs.arrays, *held)
    return (res[:n_out], rs.split(res[n_out:])) if riders else res


def _sds(shape, dtype):
    return jax.ShapeDtypeStruct(shape, dtype)


def _cast_bf16(w, k_arr, name, cols=(0, 1)):
    r, c = w.shape[0], w.shape[1] // cols[1]
    tr = min(r, 512)

    def body(k_ref, w_ref, o_ref):
        o_ref[...] = w_ref[...].astype(BF16)

    grid_spec = pltpu.PrefetchScalarGridSpec(
        num_scalar_prefetch=1, grid=(r // tr,), in_specs=[pl.BlockSpec((tr, c), lambda i, k_ref: (i, cols[0]))],
        out_specs=pl.BlockSpec((None, tr, c), lambda i, k_ref: (k_ref[0], i, 0)))
    return _pallas(body, name=name, grid_spec=grid_spec, out_shape=_sds((N_CHIPS, r, c), BF16),
                   compiler_params=_params("parallel"))(k_arr, w)


def _rmsnorm_fwd(x, g, name, riders=()):
    s, d = x.shape
    tr = 512

    def body(x_ref, g_ref, o_ref):
        xv = x_ref[...]
        y = xv * lax.rsqrt(jnp.mean(xv * xv, axis=-1, keepdims=True) + EPS)
        o_ref[...] = (y * g_ref[...]).astype(o_ref.dtype)

    rs = _Riders(riders, 2, 1)
    out = _pallas(_with_riders(body, 2, 1, 0, rs, (s // tr,)), name=name, grid=(s // tr,),
                  in_specs=[pl.BlockSpec((tr, d), lambda i: (i, 0)), pl.BlockSpec((1, d), lambda i: (0, 0)), *rs.in_specs],
                  out_specs=[pl.BlockSpec((tr, d), lambda i: (i, 0)), *rs.out_specs],
                  out_shape=[_sds((s, d), BF16), *rs.out_shapes], input_output_aliases=rs.aliases,
                  scratch_shapes=rs.scratch, compiler_params=_params("arbitrary"))(x, g, *rs.arrays)
    return out[0], rs.split(out[1:])


def _rmsnorm_bwd(x, g, dh, res, name, riders=()):
    s, d = x.shape
    tr = 512

    def body(x_ref, g_ref, dh_ref, res_ref, dx_ref, dxb_ref, dg_ref):
        i = pl.program_id(0)
        xv = x_ref[...]
        rstd = lax.rsqrt(jnp.mean(xv * xv, axis=-1, keepdims=True) + EPS)
        xh = xv * rstd
        dhv = dh_ref[...]

        @pl.when(i == 0)
        def _():
            dg_ref[...] = jnp.zeros_like(dg_ref)

        dg_ref[...] += jnp.sum(dhv * xh, axis=0, keepdims=True)
        dxh = dhv * g_ref[...]
        dx = res_ref[...] + rstd * (dxh - xh * jnp.mean(dxh * xh, axis=-1, keepdims=True))
        dx_ref[...] = dx
        dxb_ref[...] = dx.astype(BF16)

    row = pl.BlockSpec((tr, d), lambda i: (i, 0))
    vec = pl.BlockSpec((1, d), lambda i: (0, 0))
    rs = _Riders(riders, 4, 3)
    out = _pallas(_with_riders(body, 4, 3, 0, rs, (s // tr,)), name=name, grid=(s // tr,),
                  in_specs=[row, vec, row, row, *rs.in_specs], out_specs=[row, row, vec, *rs.out_specs],
                  out_shape=[_sds((s, d), F32), _sds((s, d), BF16), _sds((1, d), F32), *rs.out_shapes],
                  input_output_aliases=rs.aliases, scratch_shapes=rs.scratch,
                  compiler_params=_params("arbitrary"))(x, g, dh, res, *rs.arrays)
    return (out[:3], rs.split(out[3:])) if riders else out


def _adamw_math(w, g, m, v):
    m = ADAM_B1 * m + (1.0 - ADAM_B1) * g
    v = ADAM_B2 * v + (1.0 - ADAM_B2) * (g * g)
    m_hat = m / (1.0 - ADAM_B1 ** ADAM_STEP)
    v_hat = v / (1.0 - ADAM_B2 ** ADAM_STEP)
    delta = -ADAM_LR * (m_hat / (jnp.sqrt(v_hat) + ADAM_EPS) + ADAM_WD * w)
    return delta, m, v


def _adamw(w, g, m, v, name):
    r, c = w.shape
    tr = 256

    def body(w_ref, g_ref, m_ref, v_ref, go_ref, d_ref, nm_ref, nv_ref):
        g = g_ref[...]
        go_ref[...] = g
        d_ref[...], nm_ref[...], nv_ref[...] = _adamw_math(w_ref[...], g, m_ref[...], v_ref[...])

    blk = pl.BlockSpec((tr, c), lambda i: (i, 0))
    return _pallas(body, name=name, grid=(r // tr,), in_specs=[blk] * 4, out_specs=[blk] * 4,
                   out_shape=[_sds((r, c), F32)] * 4, compiler_params=_params("parallel"))(w, g, m, v)


def _tables(s):
    half = HEAD_DIM // 2
    pos = jnp.arange(s, dtype=F32)
    inv_freq = ROPE_BASE ** (-jnp.arange(half, dtype=F32) / half)
    ang = pos[:, None] * inv_freq[None, :]
    cos, sin = jnp.cos(ang), jnp.sin(ang)
    cos_f = jnp.concatenate([cos, cos], axis=-1)
    sin_f = jnp.concatenate([-sin, sin], axis=-1)
    log_g = jnp.log1p(-jnp.exp2(-(5.0 + jnp.arange(HEADS, dtype=F32))))
    p = jnp.arange(CHUNK, dtype=F32)
    decay = jnp.exp(log_g[:, None, None] * jnp.abs(p[:, None] - p[None, :]))
    k_dec = jnp.exp(log_g[None, :] * (CHUNK - 1.0 - p)[:, None])
    q_dec = jnp.exp(log_g[None, :] * (p + 1.0)[:, None])
    c_dec = jnp.exp(log_g * CHUNK)
    k_dec = jnp.tile(jnp.broadcast_to(k_dec.T[:, :, None], (HEADS, CHUNK, HEAD_DIM)), (1, RET_BLOCK_CHUNKS, 1))
    q_dec = jnp.tile(jnp.broadcast_to(q_dec.T[:, :, None], (HEADS, CHUNK, HEAD_DIM)), (1, RET_BLOCK_CHUNKS, 1))
    c_dec = jnp.broadcast_to(c_dec[:, None, None], (HEADS, 1, HEAD_DIM))
    n = RET_SUB // CHUNK
    decay = (jnp.eye(n, dtype=F32)[None, :, None, :, None] * decay[:, None, :, None, :]).reshape(HEADS, RET_SUB, RET_SUB)
    return cos_f, sin_f, decay, k_dec, q_dec, c_dec


def _rot(x, cos_f, sin_f):
    return x * cos_f + pltpu.roll(x, HEAD_DIM // 2, 1) * sin_f


def _rot_bwd(d, cos_f, sin_f):
    return d * cos_f + pltpu.roll(d * sin_f, HEAD_DIM // 2, 1)


K_SCALE = HEAD_DIM ** -0.5


def _retention_fwd(proj, gn_g, tables, riders=()):
    s = proj.shape[0]
    nb = s // RET_ROWS
    nc = s // CHUNK
    cos_f, sin_f, decay, k_dec, q_dec, c_dec = tables

    def body(q_ref, k_ref, v_ref, g_ref, cos_ref, sin_ref, dec_ref, kd_ref, qd_ref, cd_ref, gn_ref,
             ret_ref, y_ref, prev_ref, state_ref):
        @pl.when(pl.program_id(1) == 0)
        def _():
            state_ref[...] = jnp.zeros_like(state_ref)

        cosv, sinv = cos_ref[...], sin_ref[...]
        q = _rot(q_ref[...], cosv, sinv)
        k = _rot(k_ref[...], cosv, sinv) * K_SCALE
        v = v_ref[...]
        rg = g_ref[...]
        dec, cd, gn = dec_ref[...], cd_ref[...], gn_ref[...]
        kdf, qdf = k * kd_ref[...], q * qd_ref[...]
        chunks = [slice(c * CHUNK, (c + 1) * CHUNK) for c in range(RET_BLOCK_CHUNKS)]
        contribs = [_dot(kdf[rows], v[rows], TN) for rows in chunks]
        state, states = state_ref[...], []
        for c in range(RET_BLOCK_CHUNKS):
            states.append(state)
            prev_ref[c] = state.astype(prev_ref.dtype)
            state = cd * state + contribs[c]
        state_ref[...] = state
        cross = jnp.concatenate([_dot(qdf[rows], st, NN) for rows, st in zip(chunks, states)], axis=0)
        intra = []
        for b in range(RET_ROWS // RET_SUB):
            rows = slice(b * RET_SUB, (b + 1) * RET_SUB)
            intra.append(_dot(_dot(q[rows], k[rows], NT) * dec, v[rows], NN))
        y = jnp.concatenate(intra, axis=0) + cross
        y_ref[...] = y
        mu = jnp.mean(y, axis=-1, keepdims=True)
        yc = y - mu
        var = jnp.mean(yc * yc, axis=-1, keepdims=True)
        yn = yc * lax.rsqrt(var + GN_EPS) * gn
        ret_ref[...] = (rg * jax.nn.sigmoid(rg) * yn).astype(ret_ref.dtype)

    def col(off):
        return pl.BlockSpec((RET_ROWS, HEAD_DIM), lambda h, i: (i, off + h))

    pos = pl.BlockSpec((RET_ROWS, HEAD_DIM), lambda h, i: (i, 0))
    per_head = lambda shape: pl.BlockSpec((None, *shape), lambda h, i: (h, 0, 0))
    rs = _Riders(riders, 11, 3)
    res = _pallas(
        _with_riders(body, 11, 3, 1, rs, (HEADS, nb)), name="retention_fwd", grid=(HEADS, nb),
        in_specs=[col(0), col(HEADS), col(2 * HEADS), col(3 * HEADS), pos, pos,
                  per_head((RET_SUB, RET_SUB)), per_head((RET_ROWS, HEAD_DIM)), per_head((RET_ROWS, HEAD_DIM)),
                  per_head((1, HEAD_DIM)), pl.BlockSpec((1, HEAD_DIM), lambda h, i: (0, h)), *rs.in_specs],
        out_specs=[col(0), col(0),
                   pl.BlockSpec((None, RET_BLOCK_CHUNKS, HEAD_DIM, HEAD_DIM), lambda h, i: (h, i, 0, 0)),
                   *rs.out_specs],
        out_shape=[_sds((s, 2 * HEADS * HEAD_DIM), BF16), _sds((s, HEADS * HEAD_DIM), F32),
                   _sds((HEADS, nc, HEAD_DIM, HEAD_DIM), MXU_DTYPE), *rs.out_shapes],
        input_output_aliases=rs.aliases,
        scratch_shapes=[pltpu.VMEM((HEAD_DIM, HEAD_DIM), F32), *rs.scratch],
        compiler_params=_params("arbitrary", "arbitrary"),
    )(proj, proj, proj, proj, cos_f, sin_f, decay, k_dec, q_dec, c_dec, gn_g, *rs.arrays)
    return res[:3], rs.split(res[3:])


def _retention_bwd(proj, gn_g, tables, y, prev, dmix, riders=()):
    s = proj.shape[0]
    nb = s // RET_ROWS
    cos_f, sin_f, decay, k_dec, q_dec, c_dec = tables

    def body(q_ref, k_ref, v_ref, g_ref, cos_ref, sin_ref, dec_ref, kd_ref, qd_ref, cd_ref, gn_ref,
             y_ref, prev_ref, dret_ref, dproj_ref, dgn_ref, gstate_ref, stage_ref, stage_sems):
        head, blk = pl.program_id(0), pl.program_id(1)
        step = head * nb + blk
        slot = step % 2

        def writes(sl):
            rows = pl.ds(pl.multiple_of((nb - 1 - blk) * RET_ROWS, RET_ROWS), RET_ROWS)
            return [pltpu.make_async_copy(
                stage_ref.at[sl, g], dproj_ref.at[rows, pl.ds(pl.multiple_of((g * HEADS + head) * HEAD_DIM, HEAD_DIM), HEAD_DIM)],
                stage_sems.at[sl, g]) for g in range(4)]

        @pl.when(step >= 2)
        def _():
            for cp in writes(slot):
                cp.wait()

        @pl.when(blk == 0)
        def _():
            gstate_ref[...] = jnp.zeros_like(gstate_ref)
            dgn_ref[...] = jnp.zeros_like(dgn_ref)

        cosv, sinv = cos_ref[...], sin_ref[...]
        q = _rot(q_ref[...], cosv, sinv)
        k = _rot(k_ref[...], cosv, sinv) * K_SCALE
        v = v_ref[...]
        dec, kd, qd, cd, gn = dec_ref[...], kd_ref[...], qd_ref[...], cd_ref[...], gn_ref[...]
        kdf, qdf = k * kd, q * qd
        rg = g_ref[...]
        yv = y_ref[...]
        dret = dret_ref[...]
        sig = jax.nn.sigmoid(rg)
        gate = rg * sig
        mu = jnp.mean(yv, axis=-1, keepdims=True)
        yc = yv - mu
        rstd = lax.rsqrt(jnp.mean(yc * yc, axis=-1, keepdims=True) + GN_EPS)
        z = yc * rstd
        dyn = dret * gate
        stage_ref[slot, 3] = (dret * (z * gn) * (sig * (1.0 + rg * (1.0 - sig)))).astype(stage_ref.dtype)
        dgn_ref[...] += jnp.sum(dyn * z, axis=0, keepdims=True)
        dz = dyn * gn
        dy = rstd * (dz - jnp.mean(dz, axis=-1, keepdims=True) - z * jnp.mean(dz * z, axis=-1, keepdims=True))
        chunks = [slice(c * CHUNK, (c + 1) * CHUNK) for c in range(RET_BLOCK_CHUNKS)]
        dprevs = [_dot(qdf[rows], dy[rows], TN) for rows in chunks]
        gst, gsts = gstate_ref[...], [None] * RET_BLOCK_CHUNKS
        for c in reversed(range(RET_BLOCK_CHUNKS)):
            gsts[c] = gst
            gst = dprevs[c] + cd * gst
        gstate_ref[...] = gst
        dq = jnp.concatenate([_dot(dy[rows], prev_ref[c], NT) for c, rows in enumerate(chunks)], axis=0) * qd
        dk = jnp.concatenate([_dot(v[rows], g, NT) for rows, g in zip(chunks, gsts)], axis=0) * kd
        dv = jnp.concatenate([_dot(kdf[rows], g, NN) for rows, g in zip(chunks, gsts)], axis=0)
        dqi, dki, dvi = [], [], []
        for b in range(RET_ROWS // RET_SUB):
            rows = slice(b * RET_SUB, (b + 1) * RET_SUB)
            qs, ks, vs, dys = q[rows], k[rows], v[rows], dy[rows]
            dvi.append(_dot(_dot(ks, qs, NT) * dec, dys, NN))
            dqi.append(_dot(_dot(dys, vs, NT) * dec, ks, NN))
            dki.append(_dot(_dot(vs, dys, NT) * dec, qs, NN))
        dq = dq + jnp.concatenate(dqi, axis=0)
        dk = dk + jnp.concatenate(dki, axis=0)
        dv = dv + jnp.concatenate(dvi, axis=0)
        stage_ref[slot, 0] = _rot_bwd(dq, cosv, sinv).astype(stage_ref.dtype)
        stage_ref[slot, 1] = _rot_bwd(dk * K_SCALE, cosv, sinv).astype(stage_ref.dtype)
        stage_ref[slot, 2] = dv.astype(stage_ref.dtype)
        for cp in writes(slot):
            cp.start()

        @pl.when(step == HEADS * nb - 1)
        def _():
            for cp in writes(1 - slot) + writes(slot):
                cp.wait()

    rev = lambda i: nb - 1 - i

    def col(off):
        return pl.BlockSpec((RET_ROWS, HEAD_DIM), lambda h, i: (rev(i), off + h))

    pos = pl.BlockSpec((RET_ROWS, HEAD_DIM), lambda h, i: (rev(i), 0))
    per_head = lambda shape: pl.BlockSpec((None, *shape), lambda h, i: (h, 0, 0))
    rs = _Riders(riders, 14, 2)
    res = _pallas(
        _with_riders(body, 14, 2, 3, rs, (HEADS, nb)), name="retention_bwd", grid=(HEADS, nb),
        in_specs=[col(0), col(HEADS), col(2 * HEADS), col(3 * HEADS), pos, pos,
                  per_head((RET_SUB, RET_SUB)), per_head((RET_ROWS, HEAD_DIM)), per_head((RET_ROWS, HEAD_DIM)),
                  per_head((1, HEAD_DIM)), pl.BlockSpec((1, HEAD_DIM), lambda h, i: (0, h)),
                  col(0), pl.BlockSpec((None, RET_BLOCK_CHUNKS, HEAD_DIM, HEAD_DIM), lambda h, i: (h, rev(i), 0, 0)),
                  col(0), *rs.in_specs],
        out_specs=[ANY, per_head((1, HEAD_DIM)), *rs.out_specs],
        out_shape=[_sds((s, proj.shape[1]), BF16), _sds((HEADS, 1, HEAD_DIM), F32), *rs.out_shapes],
        input_output_aliases=rs.aliases,
        scratch_shapes=[pltpu.VMEM((HEAD_DIM, HEAD_DIM), F32), pltpu.VMEM((2, 4, RET_ROWS, HEAD_DIM), BF16),
                        pltpu.SemaphoreType.DMA((2, 4)), *rs.scratch],
        compiler_params=_params("arbitrary", "arbitrary"),
    )(proj, proj, proj, proj, cos_f, sin_f, decay, k_dec, q_dec, c_dec, gn_g, y, prev, dmix, *rs.arrays)
    return res[:2], rs.split(res[2:])


ATT_COL0 = 4 * HEADS
PAD_ROWS = LEFT_CHUNKS * CHUNK
NORM_ROWS = 512
GROUP_CHUNKS = 4
GROUP = GROUP_CHUNKS * CHUNK
WIN = (LEFT_CHUNKS + GROUP_CHUNKS) * CHUNK
MASKED = -1e30


def _qk_norm(x, g):
    return x * lax.rsqrt(jnp.mean(x * x, axis=-1, keepdims=True) + EPS) * g


def _band_probs(qb, kb, bias, g):
    sc = _dot(qb, kb, NT) * K_SCALE + bias
    win_chunk = lax.broadcasted_iota(jnp.int32, (GROUP, WIN), 1) // CHUNK
    sc = jnp.where(g * GROUP_CHUNKS - LEFT_CHUNKS + win_chunk >= 0, sc, MASKED)
    e = jnp.exp(sc - jnp.max(sc, axis=-1, keepdims=True))
    return e / jnp.sum(e, axis=-1, keepdims=True)


def _with_riders(core, n_in, n_out, n_scratch, rs, grid):
    n_rin, n_rout = len(rs.arrays), len(rs.out_shapes)
    if not rs.riders:
        return core
    steps = 1
    for n in grid:
        steps *= n

    def body(*refs):
        outs_at = n_in + n_rin
        scratch_at = outs_at + n_out + n_rout
        bound = rs.bind(refs[n_in:outs_at], refs[outs_at + n_out:scratch_at], refs[scratch_at + n_scratch:])
        step = 0
        for axis, n in enumerate(grid):
            step = step * n + pl.program_id(axis)
        pl.when(step == 0)(lambda: rs.run("start", bound))
        pl.when(step == int(steps * RIDER_MID))(lambda: rs.run("mid", bound))
        pl.when(step == int(steps * RIDER_LATE))(lambda: rs.run("late", bound))
        core(*refs[:n_in], *refs[outs_at:outs_at + n_out], *refs[scratch_at:scratch_at + n_scratch])
        pl.when(step == steps - 1)(lambda: rs.run("end", bound))

    return body


def _attention_fwd(proj, gq, gk, bias, mix, riders=()):
    s = proj.shape[0]
    rs = _Riders(riders, 7, 1)

    def body(q_ref, k_ref, v_ref, gq_ref, gk_ref, bias_ref, mix_ref, o_ref, kp_ref, vp_ref):
        kp_ref[0:PAD_ROWS, :] = jnp.zeros((PAD_ROWS, HEAD_DIM), kp_ref.dtype)
        vp_ref[0:PAD_ROWS, :] = jnp.zeros((PAD_ROWS, HEAD_DIM), vp_ref.dtype)
        gqv, gkv = gq_ref[...], gk_ref[...]

        def fill(b, carry):
            r0 = pl.multiple_of(b * NORM_ROWS, NORM_ROWS)
            kp_ref[pl.ds(PAD_ROWS + r0, NORM_ROWS), :] = _qk_norm(k_ref[pl.ds(r0, NORM_ROWS), :], gkv).astype(kp_ref.dtype)
            vp_ref[pl.ds(PAD_ROWS + r0, NORM_ROWS), :] = v_ref[pl.ds(r0, NORM_ROWS), :].astype(vp_ref.dtype)
            return carry

        lax.fori_loop(0, s // NORM_ROWS, fill, 0, unroll=2)

        def group(g, carry):
            r0 = pl.multiple_of(g * GROUP, GROUP)
            qn = _qk_norm(q_ref[pl.ds(r0, GROUP), :], gqv)
            p = _band_probs(qn, kp_ref[pl.ds(r0, WIN), :], bias_ref[...], g)
            o_ref[pl.ds(r0, GROUP), :] = _dot(p, vp_ref[pl.ds(r0, WIN), :], NN).astype(o_ref.dtype)
            return carry

        lax.fori_loop(0, s // GROUP, group, 0, unroll=8)

    def col(off):
        return pl.BlockSpec((s, HEAD_DIM), lambda h: (0, off + h))

    vec = pl.BlockSpec((1, HEAD_DIM), lambda h: (0, 0))
    res = _pallas(
        _with_riders(body, 7, 1, 2, rs, (HEADS,)), name="attention_fwd", grid=(HEADS,),
        in_specs=[col(ATT_COL0), col(ATT_COL0 + HEADS), col(ATT_COL0 + 2 * HEADS), vec, vec,
                  pl.BlockSpec((None, GROUP, WIN), lambda h: (h, 0, 0)), ANY, *rs.in_specs],
        out_specs=[col(HEADS), *rs.out_specs], out_shape=[_sds(mix.shape, mix.dtype), *rs.out_shapes],
        input_output_aliases={6: 0, **rs.aliases},
        scratch_shapes=[pltpu.VMEM((s + PAD_ROWS, HEAD_DIM), MXU_DTYPE), pltpu.VMEM((s + PAD_ROWS, HEAD_DIM), MXU_DTYPE),
                        *rs.scratch],
        compiler_params=_params("arbitrary"),
    )(proj, proj, proj, gq, gk, bias, mix, *rs.arrays)
    return res[0], rs.split(res[1:])


def _attention_bwd(proj, gq, gk, bias, dmix, dproj, riders=()):
    s = proj.shape[0]
    rs = _Riders(riders, 8, 4)

    def body(q_ref, k_ref, v_ref, gq_ref, gk_ref, bias_ref, do_ref, dproj_in_ref,
             dproj_ref, dgq_ref, dgk_ref, dbias_ref, kp_ref, vp_ref, dkp_ref, dvp_ref, dqn_ref, stage_ref, stage_sems):
        head = pl.program_id(0)

        def writes():
            return [pltpu.make_async_copy(
                stage_ref.at[g],
                dproj_ref.at[:, pl.ds(pl.multiple_of((ATT_COL0 + g * HEADS + head) * HEAD_DIM, HEAD_DIM), HEAD_DIM)],
                stage_sems.at[g]) for g in range(3)]

        kp_ref[0:PAD_ROWS, :] = jnp.zeros((PAD_ROWS, HEAD_DIM), kp_ref.dtype)
        vp_ref[0:PAD_ROWS, :] = jnp.zeros((PAD_ROWS, HEAD_DIM), vp_ref.dtype)
        dkp_ref[...] = jnp.zeros_like(dkp_ref)
        dvp_ref[...] = jnp.zeros_like(dvp_ref)
        dbias_ref[...] = jnp.zeros_like(dbias_ref)
        gqv, gkv = gq_ref[...], gk_ref[...]

        def fill(b, carry):
            r0 = pl.multiple_of(b * NORM_ROWS, NORM_ROWS)
            kp_ref[pl.ds(PAD_ROWS + r0, NORM_ROWS), :] = _qk_norm(k_ref[pl.ds(r0, NORM_ROWS), :], gkv).astype(kp_ref.dtype)
            vp_ref[pl.ds(PAD_ROWS + r0, NORM_ROWS), :] = v_ref[pl.ds(r0, NORM_ROWS), :].astype(vp_ref.dtype)
            return carry

        lax.fori_loop(0, s // NORM_ROWS, fill, 0, unroll=2)

        def group(g, carry):
            r0 = pl.multiple_of(g * GROUP, GROUP)
            qn = _qk_norm(q_ref[pl.ds(r0, GROUP), :], gqv)
            kb = kp_ref[pl.ds(r0, WIN), :]
            vb = vp_ref[pl.ds(r0, WIN), :]
            p = _band_probs(qn, kb, bias_ref[...], g)
            do = do_ref[pl.ds(r0, GROUP), :]
            dvp_ref[pl.ds(r0, WIN), :] += _dot(p, do, TN)
            dp = _dot(do, vb, NT)
            ds = p * (dp - jnp.sum(dp * p, axis=-1, keepdims=True))
            dbias_ref[...] += ds
            dss = ds * K_SCALE
            dqn_ref[pl.ds(r0, GROUP), :] = _dot(dss, kb, NN)
            dkp_ref[pl.ds(r0, WIN), :] += _dot(dss, qn, TN)
            return carry

        lax.fori_loop(0, s // GROUP, group, 0, unroll=8)

        @pl.when(head == 0)
        def _():
            dgq_ref[...] = jnp.zeros_like(dgq_ref)
            dgk_ref[...] = jnp.zeros_like(dgk_ref)

        @pl.when(head > 0)
        def _():
            for cp in writes():
                cp.wait()

        def norm_bwd(x, g, dn):
            rstd = lax.rsqrt(jnp.mean(x * x, axis=-1, keepdims=True) + EPS)
            xh = x * rstd
            dxh = dn * g
            return rstd * (dxh - xh * jnp.mean(dxh * xh, axis=-1, keepdims=True)), jnp.sum(dn * xh, axis=0, keepdims=True)

        def finish(b, carry):
            r0 = pl.multiple_of(b * NORM_ROWS, NORM_ROWS)
            rows = pl.ds(r0, NORM_ROWS)
            dq, dgq = norm_bwd(q_ref[rows, :], gqv, dqn_ref[rows, :])
            dk, dgk = norm_bwd(k_ref[rows, :], gkv, dkp_ref[pl.ds(PAD_ROWS + r0, NORM_ROWS), :])
            stage_ref[0, rows, :] = dq.astype(stage_ref.dtype)
            stage_ref[1, rows, :] = dk.astype(stage_ref.dtype)
            stage_ref[2, rows, :] = dvp_ref[pl.ds(PAD_ROWS + r0, NORM_ROWS), :].astype(stage_ref.dtype)
            dgq_ref[...] += dgq
            dgk_ref[...] += dgk
            return carry

        lax.fori_loop(0, s // NORM_ROWS, finish, 0, unroll=2)
        for cp in writes():
            cp.start()

        @pl.when(head == HEADS - 1)
        def _():
            for cp in writes():
                cp.wait()

    def col(off):
        return pl.BlockSpec((s, HEAD_DIM), lambda h: (0, off + h))

    vec = pl.BlockSpec((1, HEAD_DIM), lambda h: (0, 0))
    hbias = pl.BlockSpec((None, GROUP, WIN), lambda h: (h, 0, 0))
    res = _pallas(
        _with_riders(body, 8, 4, 7, rs, (HEADS,)), name="attention_bwd", grid=(HEADS,),
        in_specs=[col(ATT_COL0), col(ATT_COL0 + HEADS), col(ATT_COL0 + 2 * HEADS), vec, vec, hbias, col(HEADS), ANY,
                  *rs.in_specs],
        out_specs=[ANY, vec, vec, hbias, *rs.out_specs],
        out_shape=[_sds(dproj.shape, dproj.dtype), _sds((1, HEAD_DIM), F32), _sds((1, HEAD_DIM), F32),
                   _sds((HEADS, GROUP, WIN), F32), *rs.out_shapes],
        input_output_aliases={7: 0, **rs.aliases},
        scratch_shapes=[pltpu.VMEM((s + PAD_ROWS, HEAD_DIM), MXU_DTYPE), pltpu.VMEM((s + PAD_ROWS, HEAD_DIM), MXU_DTYPE),
                        pltpu.VMEM((s + PAD_ROWS, HEAD_DIM), F32), pltpu.VMEM((s + PAD_ROWS, HEAD_DIM), F32),
                        pltpu.VMEM((s, HEAD_DIM), F32), pltpu.VMEM((3, s, HEAD_DIM), BF16),
                        pltpu.SemaphoreType.DMA((3,)), *rs.scratch],
        compiler_params=_params("arbitrary"),
    )(proj, proj, proj, gq, gk, bias, dmix, dproj, *rs.arrays)
    return res[:4], rs.split(res[4:])


DIAG_SPLIT = (BAND + WIN - CHUNK) // 2


def _diag_bin(m):
    t = jnp.where(m < DIAG_SPLIT, m, m - WIN)
    return jnp.clip(LEFT_CHUNKS * CHUNK - t, -(CHUNK - 1), REL_CLIP) + (CHUNK - 1)


def _skew_rows(a, left):
    row = lax.broadcasted_iota(jnp.int32, (GROUP, WIN), 0)
    for b in range(GROUP.bit_length() - 1):
        step = 1 << b
        a = jnp.where(jnp.bitwise_and(row, step) != 0, pltpu.roll(a, WIN - step if left else step, 1), a)
    return a


def _rel_bias_expand(rel_bias):
    def body(rb_ref, o_ref):
        onehot = (lax.broadcasted_iota(jnp.int32, (REL_SIZE, WIN), 0)
                  == _diag_bin(lax.broadcasted_iota(jnp.int32, (REL_SIZE, WIN), 1))).astype(MXU_DTYPE)
        rest = jnp.broadcast_to(rb_ref[...], (8, REL_SIZE))
        per_diag = jnp.zeros((8, WIN), F32)
        for _ in range(3):
            piece = rest.astype(BF16)
            per_diag = per_diag + _dot(piece, onehot, NN)
            rest = rest - piece.astype(F32)
        table = _skew_rows(jnp.broadcast_to(per_diag[0:1], (GROUP, WIN)), left=False)
        row_chunk = lax.broadcasted_iota(jnp.int32, (GROUP, WIN), 0) // CHUNK
        col_chunk = lax.broadcasted_iota(jnp.int32, (GROUP, WIN), 1) // CHUNK
        in_band = jnp.logical_and(col_chunk >= row_chunk, col_chunk <= row_chunk + LEFT_CHUNKS)
        o_ref[...] = jnp.where(in_band, table, MASKED)

    return _pallas(body, name="rel_bias_expand", grid=(HEADS,),
                   in_specs=[pl.BlockSpec((None, 1, REL_SIZE), lambda h: (h, 0, 0))],
                   out_specs=pl.BlockSpec((None, GROUP, WIN), lambda h: (h, 0, 0)),
                   out_shape=_sds((HEADS, GROUP, WIN), F32), compiler_params=_params("parallel"))(rel_bias)


def _rel_bias_fold(dbias):
    def body(a_ref, o_ref):
        diag = jnp.sum(_skew_rows(a_ref[...], left=True), axis=0, keepdims=True)
        onehot = (_diag_bin(lax.broadcasted_iota(jnp.int32, (WIN, REL_SIZE), 0))
                  == lax.broadcasted_iota(jnp.int32, (WIN, REL_SIZE), 1)).astype(MXU_DTYPE)
        rest = jnp.broadcast_to(diag, (8, WIN))
        out = jnp.zeros((8, REL_SIZE), F32)
        for _ in range(3):
            piece = rest.astype(BF16)
            out = out + _dot(piece, onehot, NN)
            rest = rest - piece.astype(F32)
        o_ref[...] = out[0:1]

    return _pallas(body, name="rel_bias_fold", grid=(HEADS,),
                   in_specs=[pl.BlockSpec((None, GROUP, WIN), lambda h: (h, 0, 0))],
                   out_specs=pl.BlockSpec((None, 1, REL_SIZE), lambda h: (h, 0, 0)),
                   out_shape=_sds((HEADS, 1, REL_SIZE), F32), compiler_params=_params("parallel"))(dbias)


def _place():
    return lax.axis_index("x"), lax.axis_index("y"), lax.axis_index("c")


def _other_chips(x, y):
    return [(1 - x, y), (x, 1 - y), (1 - x, 1 - y)]


class _Rider:
    reads, ins, new, n_sems = (), (), (), 1

    def start(self, reads, ins, new, send, recv):
        pass

    def mid(self, reads, ins, new, send, recv):
        pass

    def late(self, reads, ins, new, send, recv):
        pass

    def end(self, reads, ins, new, send, recv):
        pass


class _Riders:
    def __init__(self, riders, n_host_in, n_host_out):
        self.riders = list(riders)
        self.arrays, self.out_shapes, self.aliases, self.scratch = [], [], {}, []
        for r in self.riders:
            for t, a in enumerate(r.ins):
                self.aliases[n_host_in + len(self.arrays) + len(r.reads) + t] = n_host_out + len(self.out_shapes) + t
            self.arrays += [*r.reads, *r.ins]
            self.out_shapes += [_sds(a.shape, a.dtype) for a in r.ins] + list(r.new)
            self.scratch += [pltpu.SemaphoreType.DMA((r.n_sems,)), pltpu.SemaphoreType.DMA((r.n_sems,))]
        self.in_specs = [ANY] * len(self.arrays)
        self.out_specs = [ANY] * len(self.out_shapes)

    def bind(self, in_refs, out_refs, scratch_refs):
        bound, i, o = [], 0, 0
        for t, r in enumerate(self.riders):
            reads = in_refs[i:i + len(r.reads)]
            i += len(r.reads) + len(r.ins)
            ins = out_refs[o:o + len(r.ins)]
            new = out_refs[o + len(r.ins):o + len(r.ins) + len(r.new)]
            o += len(r.ins) + len(r.new)
            bound.append((reads, ins, new, scratch_refs[2 * t], scratch_refs[2 * t + 1]))
        return bound

    def run(self, phase, bound):
        for r, b in zip(self.riders, bound):
            getattr(r, phase)(*b)

    def split(self, outs):
        res, o = [], 0
        for r in self.riders:
            n = len(r.ins) + len(r.new)
            res.append(list(outs[o:o + n]))
            o += n
        return res


def _run_riders(name, riders):
    rs = _Riders(riders, 0, 0)
    n_in, n_out = len(rs.arrays), len(rs.out_shapes)

    def body(*refs):
        bound = rs.bind(refs[:n_in], refs[n_in:n_in + n_out], refs[n_in + n_out:])
        rs.run("start", bound)
        rs.run("mid", bound)
        rs.run("late", bound)
        rs.run("end", bound)

    outs = _pallas(body, name=name, in_specs=rs.in_specs, out_specs=rs.out_specs, out_shape=rs.out_shapes,
                   input_output_aliases=rs.aliases, scratch_shapes=rs.scratch)(*rs.arrays)
    return rs.split(outs)


class _GatherRider(_Rider):
    X_LINK, Y_LINK, Y_PASS, X_PASS, D2D_X, D2D_Y, D2D_DIAG, N_SEMS = 0, 1, 2, 3, 4, 5, 6, 7

    def __init__(self, blocks, part=(0, 1, 1)):
        self.ins = tuple(blocks)
        self.part = part
        self.n_sems = self.N_SEMS * len(blocks)

    def _copy(self, out, send, recv, w, sem, chip_from, cc, to, sub=None):
        hr = self.ins[w].shape[1] // 2
        lo, hi, n = self.part
        first, size = cc * hr + lo * (hr // n), (hi - lo) * (hr // n)
        if sub is not None:
            size //= 2
            first += sub * size
        piece = out[w].at[2 * chip_from[0] + chip_from[1], pl.ds(first, size), :]
        return pltpu.make_async_remote_copy(src_ref=piece, dst_ref=piece, send_sem=send.at[self.N_SEMS * w + sem],
                                            recv_sem=recv.at[self.N_SEMS * w + sem], device_id=to, device_id_type=MESH)

    def _sent(self, out, send, recv, w):
        x, y, c = _place()
        me, sib = (x, y), (x, y, 1 - c)
        xn, yn, diag = _other_chips(x, y)
        cp = functools.partial(self._copy, out, send, recv, w)
        return [("start", cp(self.X_LINK, me, c, (*xn, c))), ("start", cp(self.Y_LINK, me, c, (*yn, c))),
                ("mid_x", cp(self.D2D_X, xn, c, sib)), ("mid_x", cp(self.Y_PASS, xn, c, (*yn, c), sub=0)),
                ("mid_y", cp(self.D2D_Y, yn, c, sib)), ("mid_y", cp(self.X_PASS, yn, c, (*xn, c), sub=1)),
                ("late", cp(self.D2D_DIAG, diag, c, sib))]

    def _go(self, out, send, recv, phase):
        for w in range(len(self.ins)):
            for ph, copy in self._sent(out, send, recv, w):
                if ph == phase:
                    copy.start()

    def start(self, reads, out, new, send, recv):
        self._go(out, send, recv, "start")

    def mid(self, reads, out, new, send, recv):
        x, y, c = _place()
        xn, yn, _ = _other_chips(x, y)
        for w in range(len(self.ins)):
            self._copy(out, send, recv, w, self.X_LINK, xn, c, (x, y, c)).wait_recv()
        self._go(out, send, recv, "mid_x")
        for w in range(len(self.ins)):
            self._copy(out, send, recv, w, self.Y_LINK, yn, c, (x, y, c)).wait_recv()
        self._go(out, send, recv, "mid_y")

    def late(self, reads, out, new, send, recv):
        x, y, c = _place()
        diag = _other_chips(x, y)[2]
        for w in range(len(self.ins)):
            self._copy(out, send, recv, w, self.Y_PASS, diag, c, (x, y, c), sub=0).wait_recv()
            self._copy(out, send, recv, w, self.X_PASS, diag, c, (x, y, c), sub=1).wait_recv()
        self._go(out, send, recv, "late")

    def end(self, reads, out, new, send, recv):
        x, y, c = _place()
        xn, yn, diag = _other_chips(x, y)
        for w in range(len(self.ins)):
            for sem, chip in ((self.D2D_X, xn), (self.D2D_Y, yn), (self.D2D_DIAG, diag)):
                self._copy(out, send, recv, w, sem, chip, 1 - c, (x, y, c)).wait_recv()
        for w in range(len(self.ins)):
            for _, copy in self._sent(out, send, recv, w):
                copy.wait_send()


class _SwapRider(_Rider):
    def __init__(self, grads):
        self.reads = tuple(grads)
        self.new = tuple(_sds((N_CHIPS, g.shape[1] // 2, g.shape[2]), g.dtype) for g in grads)
        self.n_sems = len(grads)

    def _copies(self, src, new, send, recv):
        x, y, c = _place()
        copies = []
        for w in range(len(self.reads)):
            hr = self.reads[w].shape[1] // 2
            copies.append(pltpu.make_async_remote_copy(
                src_ref=src[w].at[:, pl.ds((1 - c) * hr, hr), :], dst_ref=new[w],
                send_sem=send.at[w], recv_sem=recv.at[w], device_id=(x, y, 1 - c), device_id_type=MESH))
        return copies

    def start(self, src, ins, new, send, recv):
        for cp in self._copies(src, new, send, recv):
            cp.start()

    def end(self, src, ins, new, send, recv):
        for cp in self._copies(src, new, send, recv):
            cp.wait()


def _add_half(g, got, c_arr, name):
    nk, r, cols = g.shape
    hr = r // 2
    tr = min(hr, 1024)
    nb = hr // tr

    def body(c_ref, g_ref, got_ref, o_ref):
        o_ref[...] = (g_ref[...].astype(F32) + got_ref[...].astype(F32)).astype(o_ref.dtype)

    grid_spec = pltpu.PrefetchScalarGridSpec(
        num_scalar_prefetch=1, grid=(nk, nb),
        in_specs=[pl.BlockSpec((None, tr, cols), lambda k, i, c_ref: (k, c_ref[0] * nb + i, 0)),
                  pl.BlockSpec((None, tr, cols), lambda k, i, c_ref: (k, i, 0))],
        out_specs=pl.BlockSpec((None, tr, cols), lambda k, i, c_ref: (k, i, 0)))
    return _pallas(body, name=name, grid_spec=grid_spec, out_shape=_sds((nk, hr, cols), g.dtype),
                   compiler_params=_params("parallel", "parallel"))(c_arr, g, got)


class _SendPartialsRider(_Rider):
    def __init__(self, parts, got=None, part=(0, 1, 1)):
        self.reads = tuple(parts)
        if got is None:
            self.new = tuple(_sds((N_CHIPS - 1, *p.shape[1:]), p.dtype) for p in parts)
        else:
            self.ins = tuple(got)
        self.part = part
        self.n_sems = 3 * len(parts)

    def _copies(self, src, ins, new, send, recv):
        x, y, c = _place()
        land = ins if self.ins else new
        lo, hi, n = self.part
        copies = []
        for w in range(len(self.reads)):
            pr = self.reads[w].shape[1] // n
            rows = pl.ds(lo * pr, (hi - lo) * pr)
            for j, chip in enumerate(_other_chips(x, y)):
                copies.append(pltpu.make_async_remote_copy(
                    src_ref=src[w].at[2 * chip[0] + chip[1], rows, :], dst_ref=land[w].at[j, rows, :],
                    send_sem=send.at[3 * w + j], recv_sem=recv.at[3 * w + j], device_id=(*chip, c), device_id_type=MESH))
        return copies

    def start(self, src, ins, new, send, recv):
        for cp in self._copies(src, ins, new, send, recv):
            cp.start()

    def end(self, src, ins, new, send, recv):
        for cp in self._copies(src, ins, new, send, recv):
            cp.wait()


def _sum_partials(part, got, kc_arr, name):
    _, hr, cols = part.shape
    tr = min(hr, 512)
    nb = hr // tr

    def body(kc_ref, p_ref, g0_ref, g1_ref, g2_ref, o_ref):
        o_ref[...] = ((p_ref[...].astype(F32) + g0_ref[...].astype(F32)) + g1_ref[...].astype(F32)) + g2_ref[...].astype(F32)

    slot = lambda j: pl.BlockSpec((None, tr, cols), lambda i, kc_ref: (j, i, 0))
    grid_spec = pltpu.PrefetchScalarGridSpec(
        num_scalar_prefetch=1, grid=(nb,),
        in_specs=[pl.BlockSpec((None, tr, cols), lambda i, kc_ref: (kc_ref[0], i, 0)), slot(0), slot(1), slot(2)],
        out_specs=pl.BlockSpec((tr, cols), lambda i, kc_ref: (kc_ref[1] * nb + i, 0)))
    return _pallas(body, name=name, grid_spec=grid_spec, out_shape=_sds((2 * hr, cols), F32),
                   compiler_params=_params("parallel"))(kc_arr, part, got, got, got)


class _ShareRider(_Rider):
    def __init__(self, grads):
        self.ins = tuple(grads)
        self.n_sems = len(grads)

    def _copies(self, out, send, recv):
        x, y, c = _place()
        copies = []
        for w in range(len(self.ins)):
            hr = self.ins[w].shape[0] // 2
            mine = out[w].at[pl.ds(c * hr, hr), :]
            copies.append(pltpu.make_async_remote_copy(
                src_ref=mine, dst_ref=mine, send_sem=send.at[w], recv_sem=recv.at[w],
                device_id=(x, y, 1 - c), device_id_type=MESH))
        return copies

    def start(self, reads, out, new, send, recv):
        for cp in self._copies(out, send, recv):
            cp.start()

    def end(self, reads, out, new, send, recv):
        for cp in self._copies(out, send, recv):
            cp.wait()


def _small_allreduce_adamw(g_part, w, m, v, riders=()):
    rows = g_part.shape[0]
    rs = _Riders(riders, 4, 4)
    n_rin, n_rout = len(rs.arrays), len(rs.out_shapes)

    def body(*refs):
        g_ref, w_ref, m_ref, v_ref = refs[:4]
        go_ref, d_ref, nm_ref, nv_ref = refs[4 + n_rin:8 + n_rin]
        all_ref, send_sems, recv_sems = refs[8 + n_rin + n_rout:11 + n_rin + n_rout]
        bound = rs.bind(refs[4:4 + n_rin], refs[8 + n_rin:8 + n_rin + n_rout], refs[11 + n_rin + n_rout:])
        rs.run("start", bound)
        x, y, c = _place()
        me = 4 * x + 2 * y + c
        all_ref[me] = g_ref[...]
        copies = []
        for r in range(1, 8):
            dx, dy, dc = (r >> 2) & 1, (r >> 1) & 1, r & 1
            peer = (1 - x if dx else x, 1 - y if dy else y, 1 - c if dc else c)
            copies.append(pltpu.make_async_remote_copy(
                src_ref=g_ref, dst_ref=all_ref.at[me], send_sem=send_sems.at[r - 1], recv_sem=recv_sems.at[r - 1],
                device_id=peer, device_id_type=MESH))
        for cp in copies:
            cp.start()
        for cp in copies:
            cp.wait()
        tot = all_ref[0]
        for d in range(1, 8):
            tot = tot + all_ref[d]
        go_ref[...] = tot
        d_ref[...], nm_ref[...], nv_ref[...] = _adamw_math(w_ref[...], tot, m_ref[...], v_ref[...])
        for phase in ("mid", "late", "end"):
            rs.run(phase, bound)

    vm = pl.BlockSpec(memory_space=pltpu.VMEM)
    out = _pallas(
        body, name="small_allreduce_adamw", in_specs=[vm] * 4 + rs.in_specs, out_specs=[vm] * 4 + rs.out_specs,
        out_shape=[_sds((rows, 128), F32)] * 4 + rs.out_shapes, input_output_aliases=rs.aliases,
        scratch_shapes=[pltpu.VMEM((8, rows, 128), F32), pltpu.SemaphoreType.DMA((7,)), pltpu.SemaphoreType.DMA((7,)),
                        *rs.scratch],
    )(g_part, w, m, v, *rs.arrays)
    return out[:4], rs.split(out[4:])


SMALL_SIZES = (2048, 1024, 128, 128, HEADS * REL_SIZE, 2048)
SMALL_PART_ROWS = tuple(-(-size // 1024) * 8 for size in SMALL_SIZES)
SMALL_ROWS = sum(SMALL_PART_ROWS)


def _pack_small(parts):
    rows = []
    for p, size, nr in zip(parts, SMALL_SIZES, SMALL_PART_ROWS):
        rows.append(jnp.pad(p.reshape(-1), (0, nr * 128 - size)).reshape(nr, 128))
    return jnp.concatenate(rows, axis=0)


def _unpack_small(slab, shapes):
    out, off = [], 0
    for size, nr, shape in zip(SMALL_SIZES, SMALL_PART_ROWS, shapes):
        out.append(slab[off:off + nr].reshape(-1)[:size].reshape(shape))
        off += nr
    return out


def kernel(x, norm1_g, w_in, ret_norm_g, q_norm_g, k_norm_g, rel_bias, w_out, norm2_g, w_ff1, w_ff2, loss_target, m_norm1_g, m_w_in, m_ret_norm_g, m_q_norm_g, m_k_norm_g, m_rel_bias, m_w_out, m_norm2_g, m_w_ff1, m_w_ff2, v_norm1_g, v_w_in, v_ret_norm_g, v_q_norm_g, v_k_norm_g, v_rel_bias, v_w_out, v_norm2_g, v_w_ff1, v_w_ff2):
    xs = x[0]
    tgt = loss_target[0]
    s, d = xs.shape
    d_in = N_CHIPS * w_in.shape[2]
    d_ff = N_CHIPS * w_ff1.shape[2]
    in_sh, ff_sh = w_in.shape[2], w_ff1.shape[2]
    tm = min(s, 1024)
    gi = s // tm
    c_arr = lax.axis_index("c").astype(jnp.int32).reshape(1)
    k_arr = (2 * lax.axis_index("x") + lax.axis_index("y")).astype(jnp.int32).reshape(1)
    tables = _tables(s)
    bias = _rel_bias_expand(rel_bias[0][:, None, :])

    blk_in = [_cast_bf16(w_in[0], k_arr, "cast_w_in_%d" % half, cols=(half, 2)) for half in range(2)]
    blk_out, blk_ff1, blk_ff2 = (_cast_bf16(w_out[0], k_arr, "cast_w_out"), _cast_bf16(w_ff1[0], k_arr, "cast_w_ff1"),
                                 _cast_bf16(w_ff2[0], k_arr, "cast_w_ff2"))

    h1, ((wg_in0,),) = _rmsnorm_fwd(xs, norm1_g, "rmsnorm1", riders=[_GatherRider([blk_in[0]])])
    tn_in = in_sh // 2
    tk = d

    def proj_half(half, wg, through, riders):
        return _mm("proj_%d" % half, h1, wg, NN, (gi, N_CHIPS, 1),
                   pl.BlockSpec((tm, tk), lambda i, j, k: (i, 0)), pl.BlockSpec((None, tk, tn_in), lambda i, j, k: (j, 0, 0)),
                   [_sds((s, d_in), F32)], [pl.BlockSpec((tm, tn_in), lambda i, j, k: (i, 2 * j + half))], (tm, tn_in),
                   riders=riders, through=through)

    (proj,), ((wg_in1,),) = proj_half(0, wg_in0, None, [_GatherRider([blk_in[1]])])
    (proj,), ((wg_ff1,),) = proj_half(1, wg_in1, proj, [_GatherRider([blk_ff1], (0, 3, 8))])
    (mix, y_ret, prev), ((wg_ff1,),) = _retention_fwd(proj, ret_norm_g, tables, riders=[_GatherRider([wg_ff1], (3, 6, 8))])
    mix, ((wg_out,), (wg_ff2,)) = _attention_fwd(
        proj, q_norm_g, k_norm_g, bias, mix, riders=[_GatherRider([blk_out]), _GatherRider([blk_ff2], (0, 1, 4))])
    wg_out = wg_out.reshape(d, d)
    tn = 1024
    tile = pl.BlockSpec((tm, tn), lambda i, j, k: (i, j))
    def residual_norm(acc, res, g):
        x1v = res + acc
        yv = x1v * lax.rsqrt(jnp.mean(x1v * x1v, axis=-1, keepdims=True) + EPS)
        return x1v, yv * g

    tmo = min(s, 512)
    rows = pl.BlockSpec((tmo, d), lambda i, j, k: (i, 0))
    (x1, h2), ((wg_ff1,),) = _mm(
        "out_proj", mix, wg_out, NN, (s // tmo, 1, 1),
        rows, pl.BlockSpec((d, d), lambda i, j, k: (0, 0)),
        [_sds((s, d), F32), _sds((s, d), BF16)], [rows, rows], (tmo, d),
        extras=(xs, norm2_g), extra_specs=(rows, pl.BlockSpec((1, d), lambda i, j, k: (0, 0))),
        epi=residual_norm, riders=[_GatherRider([wg_ff1], (6, 8, 8))])
    tn_ff = min(ff_sh, 1024)
    per = ff_sh // tn_ff

    def relu2(acc):
        r = jnp.maximum(acc, 0.0)
        return acc, r * r

    (u, act), ((wg_ff2,),) = _mm(
        "ff1", h2, wg_ff1, NN, (gi, N_CHIPS * per, d // tk),
        pl.BlockSpec((tm, tk), lambda i, j, k: (i, k)),
        pl.BlockSpec((None, tk, tn_ff), lambda i, j, k: (j // per, k, j % per)),
        [_sds((s, d_ff), F32), _sds((s, d_ff), BF16)],
        [pl.BlockSpec((tm, tn_ff), lambda i, j, k: (i, j))] * 2, (tm, tn_ff), epi=relu2,
        riders=[_GatherRider([wg_ff2], (1, 4, 4))])
    wg_ff2 = wg_ff2.reshape(d_ff, d)

    def loss_epi(acc, res, t):
        diff = (res + acc) - t
        dy = diff / d
        return dy, dy, jnp.sum(diff * diff, axis=0, keepdims=True)

    tk2 = min(tk, 2048)
    dy, dyb, loss_cols = _mm(
        "ff2_loss", act, wg_ff2, NN, (gi, d // tn, d_ff // tk2),
        pl.BlockSpec((tm, tk2), lambda i, j, k: (i, k)), pl.BlockSpec((tk2, tn), lambda i, j, k: (k, j)),
        [_sds((s, d), F32), _sds((s, d), BF16), _sds((gi, 1, d), F32)],
        [tile, tile, pl.BlockSpec((None, 1, tn), lambda i, j, k: (i, 0, j))], (tm, tn),
        extras=(x1, tgt), extra_specs=(tile, tile), epi=loss_epi)
    loss = lax.psum(0.5 * jnp.sum(loss_cols) / d, ("x", "y", "c"))

    (du,) = _mm("d_act", dyb, wg_ff2, NT, (gi, d_ff // tn, d // tk),
                pl.BlockSpec((tm, tk), lambda i, j, k: (i, k)), pl.BlockSpec((tn, tk), lambda i, j, k: (j, k)),
                [_sds((s, d_ff), BF16)], [tile], (tm, tn), extras=(u,), extra_specs=(tile,),
                epi=lambda acc, uu: (acc * (2.0 * jnp.maximum(uu, 0.0)),))
    ts = min(s, 2048)
    wtile = pl.BlockSpec((tn, tn), lambda i, j, k: (i, j))
    (g_ff2,) = _mm("dw_ff2", act, dyb, TN, (d_ff // tn, d // tn, s // ts),
                   pl.BlockSpec((ts, tn), lambda i, j, k: (k, i)), pl.BlockSpec((ts, tn), lambda i, j, k: (k, j)),
                   [_sds((d_ff, d), BF16)], [wtile], (tn, tn))
    g_ff2 = g_ff2.reshape(N_CHIPS, d_ff // N_CHIPS, d)
    (g_ff1,), ((got_ff2,),) = _mm(
        "dw_ff1", h2, du, TN, (d // tn, N_CHIPS * per, s // ts),
        pl.BlockSpec((ts, tn), lambda i, j, k: (k, i)), pl.BlockSpec((ts, tn_ff), lambda i, j, k: (k, j)),
        [_sds((N_CHIPS, d, ff_sh), BF16)],
        [pl.BlockSpec((None, tn, tn_ff), lambda i, j, k: (j // per, i, j % per))], (tn, tn_ff),
        riders=[_SwapRider([g_ff2])])
    p_ff2 = _add_half(g_ff2, got_ff2, c_arr, "chip_partial_w_ff2")
    tkf = min(tk, ff_sh)
    kper = ff_sh // tkf
    (dh2,), ((got2_ff2,), (got_ff1,)) = _mm(
        "d_h2", du, wg_ff1, NT, (gi, d // tn, d_ff // tkf),
        pl.BlockSpec((tm, tkf), lambda i, j, k: (i, k)),
        pl.BlockSpec((None, tn, tkf), lambda i, j, k: (k // kper, j, k % kper)),
        [_sds((s, d), F32)], [tile], (tm, tn),
        riders=[_SendPartialsRider([p_ff2], part=(0, 3, 4)), _SwapRider([g_ff1])])
    p_ff1 = _add_half(g_ff1, got_ff1, c_arr, "chip_partial_w_ff1")
    dx1, dx1b, g_norm2 = _rmsnorm_bwd(x1, norm2_g, dh2, dy, "rmsnorm2_bwd")

    (dmix,) = _mm("d_mix", dx1b, wg_out, NT, (gi, d // tn, d // tk),
                  pl.BlockSpec((tm, tk), lambda i, j, k: (i, k)), pl.BlockSpec((tn, tk), lambda i, j, k: (j, k)),
                  [_sds((s, d), F32)], [tile], (tm, tn))
    (g_out,) = _mm("dw_out", mix, dx1b, TN, (d // tn, d // tn, s // ts),
                   pl.BlockSpec((ts, tn), lambda i, j, k: (k, i)), pl.BlockSpec((ts, tn), lambda i, j, k: (k, j)),
                   [_sds((d, d), BF16)], [wtile], (tn, tn))
    g_out = g_out.reshape(N_CHIPS, d // N_CHIPS, d)
    (dproj, g_gn), ((got2_ff2,), (got2_ff1,), (got_out,)) = _retention_bwd(
        proj, ret_norm_g, tables, y_ret, prev, dmix,
        riders=[_SendPartialsRider([p_ff2], got=[got2_ff2], part=(3, 4, 4)), _SendPartialsRider([p_ff1], part=(0, 2, 4)),
                _SwapRider([g_out])])
    p_out = _add_half(g_out, got_out, c_arr, "chip_partial_w_out")
    (dproj, g_gq, g_gk, dbias), ((got2_ff1,), (got2_out,)) = _attention_bwd(
        proj, q_norm_g, k_norm_g, bias, dmix, dproj,
        riders=[_SendPartialsRider([p_ff1], got=[got2_ff1], part=(2, 4, 4)), _SendPartialsRider([p_out])])
    g_rel = _rel_bias_fold(dbias)
    names = ["w_in", "w_out", "w_ff1", "w_ff2"]
    kc_arr = jnp.concatenate([k_arr, c_arr])
    early = [_sum_partials(p, r, kc_arr, "sum_partials_" + nm)
             for p, r, nm in zip((p_out, p_ff1, p_ff2), (got2_out, got2_ff1, got2_ff2), names[1:])]
    (g_in,), (early,) = _mm(
        "dw_in", h1, dproj, TN, (d // tn, 2 * N_CHIPS, s // ts),
        pl.BlockSpec((ts, tn), lambda i, j, k: (k, i)), pl.BlockSpec((ts, tn_in), lambda i, j, k: (k, j)),
        [_sds((N_CHIPS, d, in_sh), BF16)],
        [pl.BlockSpec((None, tn, tn_in), lambda i, j, k: (j // 2, i, j % 2))], (tn, tn_in), riders=[_ShareRider(early)])
    ((got_in,),) = _run_riders("grad_swap_w_in", [_SwapRider([g_in])])
    p_in = _add_half(g_in, got_in, c_arr, "chip_partial_w_in")
    half_spec = pl.BlockSpec((None, tn, tn_in), lambda i, j, k: (k // 2, j, 0))
    (dh1,), ((got2_in,),) = _mm(
        "d_h1", dproj, [wg_in0, wg_in1], NT, (gi, d // tn, 2 * N_CHIPS),
        pl.BlockSpec((tm, tn_in), lambda i, j, k: (i, k)), [half_spec, half_spec],
        [_sds((s, d), F32)], [tile], (tm, tn), riders=[_SendPartialsRider([p_in])])
    grad_x, _, g_norm1 = _rmsnorm_bwd(xs, norm1_g, dh1, dx1, "rmsnorm1_bwd")

    small_w = (norm1_g, ret_norm_g, q_norm_g, k_norm_g, rel_bias, norm2_g)
    small_m = (m_norm1_g, m_ret_norm_g, m_q_norm_g, m_k_norm_g, m_rel_bias, m_norm2_g)
    small_v = (v_norm1_g, v_ret_norm_g, v_q_norm_g, v_k_norm_g, v_rel_bias, v_norm2_g)
    shapes = [p.shape for p in small_w]
    g_small = _pack_small([g_norm1, g_gn, g_gq, g_gk, g_rel, g_norm2])
    small_out, ((g_w_in,),) = _small_allreduce_adamw(
        g_small, _pack_small(small_w), _pack_small(small_m), _pack_small(small_v),
        riders=[_ShareRider([_sum_partials(p_in, got2_in, kc_arr, "sum_partials_w_in")])])
    sg, sd, sm, sv = (_unpack_small(a, shapes) for a in small_out)

    g_big = [g_w_in, *early]
    big = []
    for g, w, m, v, nm in zip(g_big, (w_in, w_out, w_ff1, w_ff2), (m_w_in, m_w_out, m_w_ff1, m_w_ff2),
                              (v_w_in, v_w_out, v_w_ff1, v_w_ff2), names):
        g, delta, new_m, new_v = _adamw(w[0], g, m[0], v[0], "adamw_" + nm)
        big.append((g[None], delta[None], new_m[None], new_v[None]))

    def ordered(kind):
        sm_ = (sg, sd, sm, sv)[kind]
        return (sm_[0], big[0][kind], sm_[1], sm_[2], sm_[3], sm_[4], big[1][kind], sm_[5], big[2][kind], big[3][kind])

    return (loss, grad_x[None], *ordered(0), *ordered(1), *ordered(2), *ordered(3))
```

```python
import functools

import jax
import jax.numpy as jnp
from jax import lax
from jax.experimental import pallas as pl
from jax.experimental.pallas import tpu as pltpu

F32 = jnp.float32
BF16 = jnp.bfloat16
MXU_DTYPE = jnp.bfloat16

CHUNK = 64
HEADS = 8
HEAD_DIM = 128
LEFT_CHUNKS = 8
BAND = (LEFT_CHUNKS + 1) * CHUNK
REL_CLIP = 128
REL_SIZE = (CHUNK - 1) + REL_CLIP + 1
RET_BLOCK_CHUNKS = 16
RET_ROWS = RET_BLOCK_CHUNKS * CHUNK
RET_SUB = 256
ROPE_BASE = 10000.0
EPS = 1e-6
GN_EPS = 1e-5
ADAM_LR, ADAM_B1, ADAM_B2, ADAM_EPS, ADAM_WD, ADAM_STEP = 0.001, 0.9, 0.999, 1e-08, 0.01, 10
N_CHIPS = 4
VMEM_LIMIT = 56 * 1024 * 1024
MESH = pl.DeviceIdType.MESH
ANY = pl.BlockSpec(memory_space=pl.ANY)

NN = (((1,), (0,)), ((), ()))
NT = (((1,), (1,)), ((), ()))
TN = (((0,), (0,)), ((), ()))


def _pallas(body, **kw):
    return pl.pallas_call(body, **kw)


def _params(*sem):
    return pltpu.CompilerParams(dimension_semantics=sem, vmem_limit_bytes=VMEM_LIMIT)


def _dot(a, b, dims):
    return lax.dot_general(a.astype(MXU_DTYPE), b.astype(MXU_DTYPE), dims, preferred_element_type=F32)


RIDER_MID, RIDER_LATE = 0.5, 0.8


def _mm(name, a, b, dims, grid, a_spec, b_spec, outs, o_specs, acc_shape, extras=(), extra_specs=(), epi=None,
        riders=(), through=None):
    ni, nj, nk = grid
    n_ex, n_out = len(extras), len(outs)
    bs = list(b) if isinstance(b, (list, tuple)) else [b]
    b_specs = list(b_spec) if isinstance(b, (list, tuple)) else [b_spec]
    extras, extra_specs = (*bs[1:], *extras), (*b_specs[1:], *extra_specs)
    b, b_spec, n_b = bs[0], b_specs[0], len(bs)
    n_in = 1 + n_b + n_ex
    rs = _Riders(riders, n_in, n_out)
    n_rin, n_rout = len(rs.arrays), len(rs.out_shapes)
    steps = ni * nj * nk

    held = [] if through is None else [through]

    def body(*refs):
        a_ref, b_refs = refs[0], refs[1:1 + n_b]
        b_ref = b_refs[0]
        ex_refs = refs[1 + n_b:n_in]
        outs_at = n_in + n_rin + len(held)
        o_refs = refs[outs_at:outs_at + n_out]
        acc_ref = refs[outs_at + n_out + n_rout]
        k = pl.program_id(2)
        if riders:
            bound = rs.bind(refs[n_in:n_in + n_rin], refs[outs_at + n_out:outs_at + n_out + n_rout],
                            refs[outs_at + n_out + n_rout + 1:])
            step = (pl.program_id(0) * nj + pl.program_id(1)) * nk + k
            pl.when(step == 0)(lambda: rs.run("start", bound))
            pl.when(step == int(steps * RIDER_MID))(lambda: rs.run("mid", bound))
            pl.when(step == int(steps * RIDER_LATE))(lambda: rs.run("late", bound))

        def finish(acc):
            vals = epi(acc, *[r[...] for r in ex_refs]) if epi is not None else (acc,)
            for r, v in zip(o_refs, vals):
                r[...] = v.astype(r.dtype)

        if nk == 1:
            finish(_dot(a_ref[...], b_ref[...], dims))
        else:
            @pl.when(k == 0)
            def _():
                acc_ref[...] = jnp.zeros_like(acc_ref)

            for t, ref in enumerate(b_refs):
                def step_with(ref=ref):
                    acc_ref[...] += _dot(a_ref[...], ref[...], dims)

                if n_b == 1:
                    step_with()
                else:
                    pl.when(k % n_b == t)(step_with)
            pl.when(k == nk - 1)(lambda: finish(acc_ref[...]))

        if riders:
            pl.when(step == steps - 1)(lambda: rs.run("end", bound))

    res = _pallas(
        body, name=name, grid=grid, in_specs=[a_spec, b_spec, *extra_specs, *rs.in_specs, *[ANY for _ in held]],
        out_specs=[*o_specs, *rs.out_specs], out_shape=[*outs, *rs.out_shapes],
        input_output_aliases={**rs.aliases, **{n_in + n_rin: 0 for _ in held}},
        scratch_shapes=[pltpu.VMEM(acc_shape if nk > 1 else (8, 128), F32), *rs.scratch],
        compiler_params=_params(*(("arbitrary",) * 3 if riders else ("parallel", "parallel", "arbitrary"))),
    )(a, b, *extras, *rs.arrays, *held)
    return (res[:n_out], rs.split(res[n_out:])) if riders else res


def _sds(shape, dtype):
    return jax.ShapeDtypeStruct(shape, dtype)


def _cast_bf16(w, k_arr, name, cols=(0, 1)):
    r, c = w.shape[0], w.shape[1] // cols[1]
    tr = min(r, 512)

    def body(k_ref, w_ref, o_ref):
        o_ref[...] = w_ref[...].astype(BF16)

    grid_spec = pltpu.PrefetchScalarGridSpec(
        num_scalar_prefetch=1, grid=(r // tr,), in_specs=[pl.BlockSpec((tr, c), lambda i, k_ref: (i, cols[0]))],
        out_specs=pl.BlockSpec((None, tr, c), lambda i, k_ref: (k_ref[0], i, 0)))
    return _pallas(body, name=name, grid_spec=grid_spec, out_shape=_sds((N_CHIPS, r, c), BF16),
                   compiler_params=_params("parallel"))(k_arr, w)


def _rmsnorm_fwd(x, g, name, riders=()):
    s, d = x.shape
    tr = 512

    def body(x_ref, g_ref, o_ref):
        xv = x_ref[...]
        y = xv * lax.rsqrt(jnp.mean(xv * xv, axis=-1, keepdims=True) + EPS)
        o_ref[...] = (y * g_ref[...]).astype(o_ref.dtype)

    rs = _Riders(riders, 2, 1)
    out = _pallas(_with_riders(body, 2, 1, 0, rs, (s // tr,)), name=name, grid=(s // tr,),
                  in_specs=[pl.BlockSpec((tr, d), lambda i: (i, 0)), pl.BlockSpec((1, d), lambda i: (0, 0)), *rs.in_specs],
                  out_specs=[pl.BlockSpec((tr, d), lambda i: (i, 0)), *rs.out_specs],
                  out_shape=[_sds((s, d), BF16), *rs.out_shapes], input_output_aliases=rs.aliases,
                  scratch_shapes=rs.scratch, compiler_params=_params("arbitrary"))(x, g, *rs.arrays)
    return out[0], rs.split(out[1:])


def _rmsnorm_bwd(x, g, dh, res, name, riders=()):
    s, d = x.shape
    tr = 512

    def body(x_ref, g_ref, dh_ref, res_ref, dx_ref, dxb_ref, dg_ref):
        i = pl.program_id(0)
        xv = x_ref[...]
        rstd = lax.rsqrt(jnp.mean(xv * xv, axis=-1, keepdims=True) + EPS)
        xh = xv * rstd
        dhv = dh_ref[...]

        @pl.when(i == 0)
        def _():
            dg_ref[...] = jnp.zeros_like(dg_ref)

        dg_ref[...] += jnp.sum(dhv * xh, axis=0, keepdims=True)
        dxh = dhv * g_ref[...]
        dx = res_ref[...] + rstd * (dxh - xh * jnp.mean(dxh * xh, axis=-1, keepdims=True))
        dx_ref[...] = dx
        dxb_ref[...] = dx.astype(BF16)

    row = pl.BlockSpec((tr, d), lambda i: (i, 0))
    vec = pl.BlockSpec((1, d), lambda i: (0, 0))
    rs = _Riders(riders, 4, 3)
    out = _pallas(_with_riders(body, 4, 3, 0, rs, (s // tr,)), name=name, grid=(s // tr,),
                  in_specs=[row, vec, row, row, *rs.in_specs], out_specs=[row, row, vec, *rs.out_specs],
                  out_shape=[_sds((s, d), F32), _sds((s, d), BF16), _sds((1, d), F32), *rs.out_shapes],
                  input_output_aliases=rs.aliases, scratch_shapes=rs.scratch,
                  compiler_params=_params("arbitrary"))(x, g, dh, res, *rs.arrays)
    return (out[:3], rs.split(out[3:])) if riders else out


def _adamw_math(w, g, m, v):
    m = ADAM_B1 * m + (1.0 - ADAM_B1) * g
    v = ADAM_B2 * v + (1.0 - ADAM_B2) * (g * g)
    m_hat = m / (1.0 - ADAM_B1 ** ADAM_STEP)
    v_hat = v / (1.0 - ADAM_B2 ** ADAM_STEP)
    delta = -ADAM_LR * (m_hat / (jnp.sqrt(v_hat) + ADAM_EPS) + ADAM_WD * w)
    return delta, m, v


def _adamw(w, g, m, v, name):
    r, c = w.shape
    tr = 256

    def body(w_ref, g_ref, m_ref, v_ref, go_ref, d_ref, nm_ref, nv_ref):
        g = g_ref[...]
        go_ref[...] = g
        d_ref[...], nm_ref[...], nv_ref[...] = _adamw_math(w_ref[...], g, m_ref[...], v_ref[...])

    blk = pl.BlockSpec((tr, c), lambda i: (i, 0))
    return _pallas(body, name=name, grid=(r // tr,), in_specs=[blk] * 4, out_specs=[blk] * 4,
                   out_shape=[_sds((r, c), F32)] * 4, compiler_params=_params("parallel"))(w, g, m, v)


def _tables(s):
    half = HEAD_DIM // 2
    pos = jnp.arange(s, dtype=F32)
    inv_freq = ROPE_BASE ** (-jnp.arange(half, dtype=F32) / half)
    ang = pos[:, None] * inv_freq[None, :]
    cos, sin = jnp.cos(ang), jnp.sin(ang)
    cos_f = jnp.concatenate([cos, cos], axis=-1)
    sin_f = jnp.concatenate([-sin, sin], axis=-1)
    log_g = jnp.log1p(-jnp.exp2(-(5.0 + jnp.arange(HEADS, dtype=F32))))
    p = jnp.arange(CHUNK, dtype=F32)
    decay = jnp.exp(log_g[:, None, None] * jnp.abs(p[:, None] - p[None, :]))
    k_dec = jnp.exp(log_g[None, :] * (CHUNK - 1.0 - p)[:, None])
    q_dec = jnp.exp(log_g[None, :] * (p + 1.0)[:, None])
    c_dec = jnp.exp(log_g * CHUNK)
    k_dec = jnp.tile(jnp.broadcast_to(k_dec.T[:, :, None], (HEADS, CHUNK, HEAD_DIM)), (1, RET_BLOCK_CHUNKS, 1))
    q_dec = jnp.tile(jnp.broadcast_to(q_dec.T[:, :, None], (HEADS, CHUNK, HEAD_DIM)), (1, RET_BLOCK_CHUNKS, 1))
    c_dec = jnp.broadcast_to(c_dec[:, None, None], (HEADS, 1, HEAD_DIM))
    n = RET_SUB // CHUNK
    decay = (jnp.eye(n, dtype=F32)[None, :, None, :, None] * decay[:, None, :, None, :]).reshape(HEADS, RET_SUB, RET_SUB)
    return cos_f, sin_f, decay, k_dec, q_dec, c_dec


def _rot(x, cos_f, sin_f):
    return x * cos_f + pltpu.roll(x, HEAD_DIM // 2, 1) * sin_f


def _rot_bwd(d, cos_f, sin_f):
    return d * cos_f + pltpu.roll(d * sin_f, HEAD_DIM // 2, 1)


K_SCALE = HEAD_DIM ** -0.5


def _retention_fwd(proj, gn_g, tables, riders=()):
    s = proj.shape[0]
    nb = s // RET_ROWS
    nc = s // CHUNK
    cos_f, sin_f, decay, k_dec, q_dec, c_dec = tables

    def body(q_ref, k_ref, v_ref, g_ref, cos_ref, sin_ref, dec_ref, kd_ref, qd_ref, cd_ref, gn_ref,
             ret_ref, y_ref, prev_ref, state_ref):
        @pl.when(pl.program_id(1) == 0)
        def _():
            state_ref[...] = jnp.zeros_like(state_ref)

        cosv, sinv = cos_ref[...], sin_ref[...]
        q = _rot(q_ref[...], cosv, sinv)
        k = _rot(k_ref[...], cosv, sinv) * K_SCALE
        v = v_ref[...]
        rg = g_ref[...]
        dec, cd, gn = dec_ref[...], cd_ref[...], gn_ref[...]
        kdf, qdf = k * kd_ref[...], q * qd_ref[...]
        chunks = [slice(c * CHUNK, (c + 1) * CHUNK) for c in range(RET_BLOCK_CHUNKS)]
        contribs = [_dot(kdf[rows], v[rows], TN) for rows in chunks]
        state, states = state_ref[...], []
        for c in range(RET_BLOCK_CHUNKS):
            states.append(state)
            prev_ref[c] = state.astype(prev_ref.dtype)
            state = cd * state + contribs[c]
        state_ref[...] = state
        cross = jnp.concatenate([_dot(qdf[rows], st, NN) for rows, st in zip(chunks, states)], axis=0)
        intra = []
        for b in range(RET_ROWS // RET_SUB):
            rows = slice(b * RET_SUB, (b + 1) * RET_SUB)
            intra.append(_dot(_dot(q[rows], k[rows], NT) * dec, v[rows], NN))
        y = jnp.concatenate(intra, axis=0) + cross
        y_ref[...] = y
        mu = jnp.mean(y, axis=-1, keepdims=True)
        yc = y - mu
        var = jnp.mean(yc * yc, axis=-1, keepdims=True)
        yn = yc * lax.rsqrt(var + GN_EPS) * gn
        ret_ref[...] = (rg * jax.nn.sigmoid(rg) * yn).astype(ret_ref.dtype)

    def col(off):
        return pl.BlockSpec((RET_ROWS, HEAD_DIM), lambda h, i: (i, off + h))

    pos = pl.BlockSpec((RET_ROWS, HEAD_DIM), lambda h, i: (i, 0))
    per_head = lambda shape: pl.BlockSpec((None, *shape), lambda h, i: (h, 0, 0))
    rs = _Riders(riders, 11, 3)
    res = _pallas(
        _with_riders(body, 11, 3, 1, rs, (HEADS, nb)), name="retention_fwd", grid=(HEADS, nb),
        in_specs=[col(0), col(HEADS), col(2 * HEADS), col(3 * HEADS), pos, pos,
                  per_head((RET_SUB, RET_SUB)), per_head((RET_ROWS, HEAD_DIM)), per_head((RET_ROWS, HEAD_DIM)),
                  per_head((1, HEAD_DIM)), pl.BlockSpec((1, HEAD_DIM), lambda h, i: (0, h)), *rs.in_specs],
        out_specs=[col(0), col(0),
                   pl.BlockSpec((None, RET_BLOCK_CHUNKS, HEAD_DIM, HEAD_DIM), lambda h, i: (h, i, 0, 0)),
                   *rs.out_specs],
        out_shape=[_sds((s, 2 * HEADS * HEAD_DIM), BF16), _sds((s, HEADS * HEAD_DIM), F32),
                   _sds((HEADS, nc, HEAD_DIM, HEAD_DIM), MXU_DTYPE), *rs.out_shapes],
        input_output_aliases=rs.aliases,
        scratch_shapes=[pltpu.VMEM((HEAD_DIM, HEAD_DIM), F32), *rs.scratch],
        compiler_params=_params("arbitrary", "arbitrary"),
    )(proj, proj, proj, proj, cos_f, sin_f, decay, k_dec, q_dec, c_dec, gn_g, *rs.arrays)
    return res[:3], rs.split(res[3:])


def _retention_bwd(proj, gn_g, tables, y, prev, dmix, riders=()):
    s = proj.shape[0]
    nb = s // RET_ROWS
    cos_f, sin_f, decay, k_dec, q_dec, c_dec = tables

    def body(q_ref, k_ref, v_ref, g_ref, cos_ref, sin_ref, dec_ref, kd_ref, qd_ref, cd_ref, gn_ref,
             y_ref, prev_ref, dret_ref, dproj_ref, dgn_ref, gstate_ref, stage_ref, stage_sems):
        head, blk = pl.program_id(0), pl.program_id(1)
        step = head * nb + blk
        slot = step % 2

        def writes(sl):
            rows = pl.ds(pl.multiple_of((nb - 1 - blk) * RET_ROWS, RET_ROWS), RET_ROWS)
            return [pltpu.make_async_copy(
                stage_ref.at[sl, g], dproj_ref.at[rows, pl.ds(pl.multiple_of((g * HEADS + head) * HEAD_DIM, HEAD_DIM), HEAD_DIM)],
                stage_sems.at[sl, g]) for g in range(4)]

        @pl.when(step >= 2)
        def _():
            for cp in writes(slot):
                cp.wait()

        @pl.when(blk == 0)
        def _():
            gstate_ref[...] = jnp.zeros_like(gstate_ref)
            dgn_ref[...] = jnp.zeros_like(dgn_ref)

        cosv, sinv = cos_ref[...], sin_ref[...]
        q = _rot(q_ref[...], cosv, sinv)
        k = _rot(k_ref[...], cosv, sinv) * K_SCALE
        v = v_ref[...]
        dec, kd, qd, cd, gn = dec_ref[...], kd_ref[...], qd_ref[...], cd_ref[...], gn_ref[...]
        kdf, qdf = k * kd, q * qd
        rg = g_ref[...]
        yv = y_ref[...]
        dret = dret_ref[...]
        sig = jax.nn.sigmoid(rg)
        gate = rg * sig
        mu = jnp.mean(yv, axis=-1, keepdims=True)
        yc = yv - mu
        rstd = lax.rsqrt(jnp.mean(yc * yc, axis=-1, keepdims=True) + GN_EPS)
        z = yc * rstd
        dyn = dret * gate
        stage_ref[slot, 3] = (dret * (z * gn) * (sig * (1.0 + rg * (1.0 - sig)))).astype(stage_ref.dtype)
        dgn_ref[...] += jnp.sum(dyn * z, axis=0, keepdims=True)
        dz = dyn * gn
        dy = rstd * (dz - jnp.mean(dz, axis=-1, keepdims=True) - z * jnp.mean(dz * z, axis=-1, keepdims=True))
        chunks = [slice(c * CHUNK, (c + 1) * CHUNK) for c in range(RET_BLOCK_CHUNKS)]
        dprevs = [_dot(qdf[rows], dy[rows], TN) for rows in chunks]
        gst, gsts = gstate_ref[...], [None] * RET_BLOCK_CHUNKS
        for c in reversed(range(RET_BLOCK_CHUNKS)):
            gsts[c] = gst
            gst = dprevs[c] + cd * gst
        gstate_ref[...] = gst
        dq = jnp.concatenate([_dot(dy[rows], prev_ref[c], NT) for c, rows in enumerate(chunks)], axis=0) * qd
        dk = jnp.concatenate([_dot(v[rows], g, NT) for rows, g in zip(chunks, gsts)], axis=0) * kd
        dv = jnp.concatenate([_dot(kdf[rows], g, NN) for rows, g in zip(chunks, gsts)], axis=0)
        dqi, dki, dvi = [], [], []
        for b in range(RET_ROWS // RET_SUB):
            rows = slice(b * RET_SUB, (b + 1) * RET_SUB)
            qs, ks, vs, dys = q[rows], k[rows], v[rows], dy[rows]
            dvi.append(_dot(_dot(ks, qs, NT) * dec, dys, NN))
            dqi.append(_dot(_dot(dys, vs, NT) * dec, ks, NN))
            dki.append(_dot(_dot(vs, dys, NT) * dec, qs, NN))
        dq = dq + jnp.concatenate(dqi, axis=0)
        dk = dk + jnp.concatenate(dki, axis=0)
        dv = dv + jnp.concatenate(dvi, axis=0)
        stage_ref[slot, 0] = _rot_bwd(dq, cosv, sinv).astype(stage_ref.dtype)
        stage_ref[slot, 1] = _rot_bwd(dk * K_SCALE, cosv, sinv).astype(stage_ref.dtype)
        stage_ref[slot, 2] = dv.astype(stage_ref.dtype)
        for cp in writes(slot):
            cp.start()

        @pl.when(step == HEADS * nb - 1)
        def _():
            for cp in writes(1 - slot) + writes(slot):
                cp.wait()

    rev = lambda i: nb - 1 - i

    def col(off):
        return pl.BlockSpec((RET_ROWS, HEAD_DIM), lambda h, i: (rev(i), off + h))

    pos = pl.BlockSpec((RET_ROWS, HEAD_DIM), lambda h, i: (rev(i), 0))
    per_head = lambda shape: pl.BlockSpec((None, *shape), lambda h, i: (h, 0, 0))
    rs = _Riders(riders, 14, 2)
    res = _pallas(
        _with_riders(body, 14, 2, 3, rs, (HEADS, nb)), name="retention_bwd", grid=(HEADS, nb),
        in_specs=[col(0), col(HEADS), col(2 * HEADS), col(3 * HEADS), pos, pos,
                  per_head((RET_SUB, RET_SUB)), per_head((RET_ROWS, HEAD_DIM)), per_head((RET_ROWS, HEAD_DIM)),
                  per_head((1, HEAD_DIM)), pl.BlockSpec((1, HEAD_DIM), lambda h, i: (0, h)),
                  col(0), pl.BlockSpec((None, RET_BLOCK_CHUNKS, HEAD_DIM, HEAD_DIM), lambda h, i: (h, rev(i), 0, 0)),
                  col(0), *rs.in_specs],
        out_specs=[ANY, per_head((1, HEAD_DIM)), *rs.out_specs],
        out_shape=[_sds((s, proj.shape[1]), BF16), _sds((HEADS, 1, HEAD_DIM), F32), *rs.out_shapes],
        input_output_aliases=rs.aliases,
        scratch_shapes=[pltpu.VMEM((HEAD_DIM, HEAD_DIM), F32), pltpu.VMEM((2, 4, RET_ROWS, HEAD_DIM), BF16),
                        pltpu.SemaphoreType.DMA((2, 4)), *rs.scratch],
        compiler_params=_params("arbitrary", "arbitrary"),
    )(proj, proj, proj, proj, cos_f, sin_f, decay, k_dec, q_dec, c_dec, gn_g, y, prev, dmix, *rs.arrays)
    return res[:2], rs.split(res[2:])


ATT_COL0 = 4 * HEADS
PAD_ROWS = LEFT_CHUNKS * CHUNK
NORM_ROWS = 512
GROUP_CHUNKS = 4
GROUP = GROUP_CHUNKS * CHUNK
WIN = (LEFT_CHUNKS + GROUP_CHUNKS) * CHUNK
MASKED = -1e30


def _qk_norm(x, g):
    return x * lax.rsqrt(jnp.mean(x * x, axis=-1, keepdims=True) + EPS) * g


def _band_probs(qb, kb, bias, g):
    sc = _dot(qb, kb, NT) * K_SCALE + bias
    win_chunk = lax.broadcasted_iota(jnp.int32, (GROUP, WIN), 1) // CHUNK
    sc = jnp.where(g * GROUP_CHUNKS - LEFT_CHUNKS + win_chunk >= 0, sc, MASKED)
    e = jnp.exp(sc - jnp.max(sc, axis=-1, keepdims=True))
    return e / jnp.sum(e, axis=-1, keepdims=True)


def _with_riders(core, n_in, n_out, n_scratch, rs, grid):
    n_rin, n_rout = len(rs.arrays), len(rs.out_shapes)
    if not rs.riders:
        return core
    steps = 1
    for n in grid:
        steps *= n

    def body(*refs):
        outs_at = n_in + n_rin
        scratch_at = outs_at + n_out + n_rout
        bound = rs.bind(refs[n_in:outs_at], refs[outs_at + n_out:scratch_at], refs[scratch_at + n_scratch:])
        step = 0
        for axis, n in enumerate(grid):
            step = step * n + pl.program_id(axis)
        pl.when(step == 0)(lambda: rs.run("start", bound))
        pl.when(step == int(steps * RIDER_MID))(lambda: rs.run("mid", bound))
        pl.when(step == int(steps * RIDER_LATE))(lambda: rs.run("late", bound))
        core(*refs[:n_in], *refs[outs_at:outs_at + n_out], *refs[scratch_at:scratch_at + n_scratch])
        pl.when(step == steps - 1)(lambda: rs.run("end", bound))

    return body


def _attention_fwd(proj, gq, gk, bias, mix, riders=()):
    s = proj.shape[0]
    rs = _Riders(riders, 7, 1)

    def body(q_ref, k_ref, v_ref, gq_ref, gk_ref, bias_ref, mix_ref, o_ref, kp_ref, vp_ref):
        kp_ref[0:PAD_ROWS, :] = jnp.zeros((PAD_ROWS, HEAD_DIM), kp_ref.dtype)
        vp_ref[0:PAD_ROWS, :] = jnp.zeros((PAD_ROWS, HEAD_DIM), vp_ref.dtype)
        gqv, gkv = gq_ref[...], gk_ref[...]

        def fill(b, carry):
            r0 = pl.multiple_of(b * NORM_ROWS, NORM_ROWS)
            kp_ref[pl.ds(PAD_ROWS + r0, NORM_ROWS), :] = _qk_norm(k_ref[pl.ds(r0, NORM_ROWS), :], gkv).astype(kp_ref.dtype)
            vp_ref[pl.ds(PAD_ROWS + r0, NORM_ROWS), :] = v_ref[pl.ds(r0, NORM_ROWS), :].astype(vp_ref.dtype)
            return carry

        lax.fori_loop(0, s // NORM_ROWS, fill, 0, unroll=2)

        def group(g, carry):
            r0 = pl.multiple_of(g * GROUP, GROUP)
            qn = _qk_norm(q_ref[pl.ds(r0, GROUP), :], gqv)
            p = _band_probs(qn, kp_ref[pl.ds(r0, WIN), :], bias_ref[...], g)
            o_ref[pl.ds(r0, GROUP), :] = _dot(p, vp_ref[pl.ds(r0, WIN), :], NN).astype(o_ref.dtype)
            return carry

        lax.fori_loop(0, s // GROUP, group, 0, unroll=8)

    def col(off):
        return pl.BlockSpec((s, HEAD_DIM), lambda h: (0, off + h))

    vec = pl.BlockSpec((1, HEAD_DIM), lambda h: (0, 0))
    res = _pallas(
        _with_riders(body, 7, 1, 2, rs, (HEADS,)), name="attention_fwd", grid=(HEADS,),
        in_specs=[col(ATT_COL0), col(ATT_COL0 + HEADS), col(ATT_COL0 + 2 * HEADS), vec, vec,
                  pl.BlockSpec((None, GROUP, WIN), lambda h: (h, 0, 0)), ANY, *rs.in_specs],
        out_specs=[col(HEADS), *rs.out_specs], out_shape=[_sds(mix.shape, mix.dtype), *rs.out_shapes],
        input_output_aliases={6: 0, **rs.aliases},
        scratch_shapes=[pltpu.VMEM((s + PAD_ROWS, HEAD_DIM), MXU_DTYPE), pltpu.VMEM((s + PAD_ROWS, HEAD_DIM), MXU_DTYPE),
                        *rs.scratch],
        compiler_params=_params("arbitrary"),
    )(proj, proj, proj, gq, gk, bias, mix, *rs.arrays)
    return res[0], rs.split(res[1:])


def _attention_bwd(proj, gq, gk, bias, dmix, dproj, riders=()):
    s = proj.shape[0]
    rs = _Riders(riders, 8, 4)

    def body(q_ref, k_ref, v_ref, gq_ref, gk_ref, bias_ref, do_ref, dproj_in_ref,
             dproj_ref, dgq_ref, dgk_ref, dbias_ref, kp_ref, vp_ref, dkp_ref, dvp_ref, dqn_ref, stage_ref, stage_sems):
        head = pl.program_id(0)

        def writes():
            return [pltpu.make_async_copy(
                stage_ref.at[g],
                dproj_ref.at[:, pl.ds(pl.multiple_of((ATT_COL0 + g * HEADS + head) * HEAD_DIM, HEAD_DIM), HEAD_DIM)],
                stage_sems.at[g]) for g in range(3)]

        kp_ref[0:PAD_ROWS, :] = jnp.zeros((PAD_ROWS, HEAD_DIM), kp_ref.dtype)
        vp_ref[0:PAD_ROWS, :] = jnp.zeros((PAD_ROWS, HEAD_DIM), vp_ref.dtype)
        dkp_ref[...] = jnp.zeros_like(dkp_ref)
        dvp_ref[...] = jnp.zeros_like(dvp_ref)
        dbias_ref[...] = jnp.zeros_like(dbias_ref)
        gqv, gkv = gq_ref[...], gk_ref[...]

        def fill(b, carry):
            r0 = pl.multiple_of(b * NORM_ROWS, NORM_ROWS)
            kp_ref[pl.ds(PAD_ROWS + r0, NORM_ROWS), :] = _qk_norm(k_ref[pl.ds(r0, NORM_ROWS), :], gkv).astype(kp_ref.dtype)
            vp_ref[pl.ds(PAD_ROWS + r0, NORM_ROWS), :] = v_ref[pl.ds(r0, NORM_ROWS), :].astype(vp_ref.dtype)
            return carry

        lax.fori_loop(0, s // NORM_ROWS, fill, 0, unroll=2)

        def group(g, carry):
            r0 = pl.multiple_of(g * GROUP, GROUP)
            qn = _qk_norm(q_ref[pl.ds(r0, GROUP), :], gqv)
            kb = kp_ref[pl.ds(r0, WIN), :]
            vb = vp_ref[pl.ds(r0, WIN), :]
            p = _band_probs(qn, kb, bias_ref[...], g)
            do = do_ref[pl.ds(r0, GROUP), :]
            dvp_ref[pl.ds(r0, WIN), :] += _dot(p, do, TN)
            dp = _dot(do, vb, NT)
            ds = p * (dp - jnp.sum(dp * p, axis=-1, keepdims=True))
            dbias_ref[...] += ds
            dss = ds * K_SCALE
            dqn_ref[pl.ds(r0, GROUP), :] = _dot(dss, kb, NN)
            dkp_ref[pl.ds(r0, WIN), :] += _dot(dss, qn, TN)
            return carry

        lax.fori_loop(0, s // GROUP, group, 0, unroll=8)

        @pl.when(head == 0)
        def _():
            dgq_ref[...] = jnp.zeros_like(dgq_ref)
            dgk_ref[...] = jnp.zeros_like(dgk_ref)

        @pl.when(head > 0)
        def _():
            for cp in writes():
                cp.wait()

        def norm_bwd(x, g, dn):
            rstd = lax.rsqrt(jnp.mean(x * x, axis=-1, keepdims=True) + EPS)
            xh = x * rstd
            dxh = dn * g
            return rstd * (dxh - xh * jnp.mean(dxh * xh, axis=-1, keepdims=True)), jnp.sum(dn * xh, axis=0, keepdims=True)

        def finish(b, carry):
            r0 = pl.multiple_of(b * NORM_ROWS, NORM_ROWS)
            rows = pl.ds(r0, NORM_ROWS)
            dq, dgq = norm_bwd(q_ref[rows, :], gqv, dqn_ref[rows, :])
            dk, dgk = norm_bwd(k_ref[rows, :], gkv, dkp_ref[pl.ds(PAD_ROWS + r0, NORM_ROWS), :])
            stage_ref[0, rows, :] = dq.astype(stage_ref.dtype)
            stage_ref[1, rows, :] = dk.astype(stage_ref.dtype)
            stage_ref[2, rows, :] = dvp_ref[pl.ds(PAD_ROWS + r0, NORM_ROWS), :].astype(stage_ref.dtype)
            dgq_ref[...] += dgq
            dgk_ref[...] += dgk
            return carry

        lax.fori_loop(0, s // NORM_ROWS, finish, 0, unroll=2)
        for cp in writes():
            cp.start()

        @pl.when(head == HEADS - 1)
        def _():
            for cp in writes():
                cp.wait()

    def col(off):
        return pl.BlockSpec((s, HEAD_DIM), lambda h: (0, off + h))

    vec = pl.BlockSpec((1, HEAD_DIM), lambda h: (0, 0))
    hbias = pl.BlockSpec((None, GROUP, WIN), lambda h: (h, 0, 0))
    res = _pallas(
        _with_riders(body, 8, 4, 7, rs, (HEADS,)), name="attention_bwd", grid=(HEADS,),
        in_specs=[col(ATT_COL0), col(ATT_COL0 + HEADS), col(ATT_COL0 + 2 * HEADS), vec, vec, hbias, col(HEADS), ANY,
                  *rs.in_specs],
        out_specs=[ANY, vec, vec, hbias, *rs.out_specs],
        out_shape=[_sds(dproj.shape, dproj.dtype), _sds((1, HEAD_DIM), F32), _sds((1, HEAD_DIM), F32),
                   _sds((HEADS, GROUP, WIN), F32), *rs.out_shapes],
        input_output_aliases={7: 0, **rs.aliases},
        scratch_shapes=[pltpu.VMEM((s + PAD_ROWS, HEAD_DIM), MXU_DTYPE), pltpu.VMEM((s + PAD_ROWS, HEAD_DIM), MXU_DTYPE),
                        pltpu.VMEM((s + PAD_ROWS, HEAD_DIM), F32), pltpu.VMEM((s + PAD_ROWS, HEAD_DIM), F32),
                        pltpu.VMEM((s, HEAD_DIM), F32), pltpu.VMEM((3, s, HEAD_DIM), BF16),
                        pltpu.SemaphoreType.DMA((3,)), *rs.scratch],
        compiler_params=_params("arbitrary"),
    )(proj, proj, proj, gq, gk, bias, dmix, dproj, *rs.arrays)
    return res[:4], rs.split(res[4:])


DIAG_SPLIT = (BAND + WIN - CHUNK) // 2


def _diag_bin(m):
    t = jnp.where(m < DIAG_SPLIT, m, m - WIN)
    return jnp.clip(LEFT_CHUNKS * CHUNK - t, -(CHUNK - 1), REL_CLIP) + (CHUNK - 1)


def _skew_rows(a, left):
    row = lax.broadcasted_iota(jnp.int32, (GROUP, WIN), 0)
    for b in range(GROUP.bit_length() - 1):
        step = 1 << b
        a = jnp.where(jnp.bitwise_and(row, step) != 0, pltpu.roll(a, WIN - step if left else step, 1), a)
    return a


def _rel_bias_expand(rel_bias):
    def body(rb_ref, o_ref):
        onehot = (lax.broadcasted_iota(jnp.int32, (REL_SIZE, WIN), 0)
                  == _diag_bin(lax.broadcasted_iota(jnp.int32, (REL_SIZE, WIN), 1))).astype(MXU_DTYPE)
        rest = jnp.broadcast_to(rb_ref[...], (8, REL_SIZE))
        per_diag = jnp.zeros((8, WIN), F32)
        for _ in range(3):
            piece = rest.astype(BF16)
            per_diag = per_diag + _dot(piece, onehot, NN)
            rest = rest - piece.astype(F32)
        table = _skew_rows(jnp.broadcast_to(per_diag[0:1], (GROUP, WIN)), left=False)
        row_chunk = lax.broadcasted_iota(jnp.int32, (GROUP, WIN), 0) // CHUNK
        col_chunk = lax.broadcasted_iota(jnp.int32, (GROUP, WIN), 1) // CHUNK
        in_band = jnp.logical_and(col_chunk >= row_chunk, col_chunk <= row_chunk + LEFT_CHUNKS)
        o_ref[...] = jnp.where(in_band, table, MASKED)

    return _pallas(body, name="rel_bias_expand", grid=(HEADS,),
                   in_specs=[pl.BlockSpec((None, 1, REL_SIZE), lambda h: (h, 0, 0))],
                   out_specs=pl.BlockSpec((None, GROUP, WIN), lambda h: (h, 0, 0)),
                   out_shape=_sds((HEADS, GROUP, WIN), F32), compiler_params=_params("parallel"))(rel_bias)


def _rel_bias_fold(dbias):
    def body(a_ref, o_ref):
        diag = jnp.sum(_skew_rows(a_ref[...], left=True), axis=0, keepdims=True)
        onehot = (_diag_bin(lax.broadcasted_iota(jnp.int32, (WIN, REL_SIZE), 0))
                  == lax.broadcasted_iota(jnp.int32, (WIN, REL_SIZE), 1)).astype(MXU_DTYPE)
        rest = jnp.broadcast_to(diag, (8, WIN))
        out = jnp.zeros((8, REL_SIZE), F32)
        for _ in range(3):
            piece = rest.astype(BF16)
            out = out + _dot(piece, onehot, NN)
            rest = rest - piece.astype(F32)
        o_ref[...] = out[0:1]

    return _pallas(body, name="rel_bias_fold", grid=(HEADS,),
                   in_specs=[pl.BlockSpec((None, GROUP, WIN), lambda h: (h, 0, 0))],
                   out_specs=pl.BlockSpec((None, 1, REL_SIZE), lambda h: (h, 0, 0)),
                   out_shape=_sds((HEADS, 1, REL_SIZE), F32), compiler_params=_params("parallel"))(dbias)


def _place():
    return lax.axis_index("x"), lax.axis_index("y"), lax.axis_index("c")


def _other_chips(x, y):
    return [(1 - x, y), (x, 1 - y), (1 - x, 1 - y)]


class _Rider:
    reads, ins, new, n_sems = (), (), (), 1

    def start(self, reads, ins, new, send, recv):
        pass

    def mid(self, reads, ins, new, send, recv):
        pass

    def late(self, reads, ins, new, send, recv):
        pass

    def end(self, reads, ins, new, send, recv):
        pass


class _Riders:
    def __init__(self, riders, n_host_in, n_host_out):
        self.riders = list(riders)
        self.arrays, self.out_shapes, self.aliases, self.scratch = [], [], {}, []
        for r in self.riders:
            for t, a in enumerate(r.ins):
                self.aliases[n_host_in + len(self.arrays) + len(r.reads) + t] = n_host_out + len(self.out_shapes) + t
            self.arrays += [*r.reads, *r.ins]
            self.out_shapes += [_sds(a.shape, a.dtype) for a in r.ins] + list(r.new)
            self.scratch += [pltpu.SemaphoreType.DMA((r.n_sems,)), pltpu.SemaphoreType.DMA((r.n_sems,))]
        self.in_specs = [ANY] * len(self.arrays)
        self.out_specs = [ANY] * len(self.out_shapes)

    def bind(self, in_refs, out_refs, scratch_refs):
        bound, i, o = [], 0, 0
        for t, r in enumerate(self.riders):
            reads = in_refs[i:i + len(r.reads)]
            i += len(r.reads) + len(r.ins)
            ins = out_refs[o:o + len(r.ins)]
            new = out_refs[o + len(r.ins):o + len(r.ins) + len(r.new)]
            o += len(r.ins) + len(r.new)
            bound.append((reads, ins, new, scratch_refs[2 * t], scratch_refs[2 * t + 1]))
        return bound

    def run(self, phase, bound):
        for r, b in zip(self.riders, bound):
            getattr(r, phase)(*b)

    def split(self, outs):
        res, o = [], 0
        for r in self.riders:
            n = len(r.ins) + len(r.new)
            res.append(list(outs[o:o + n]))
            o += n
        return res


def _run_riders(name, riders):
    rs = _Riders(riders, 0, 0)
    n_in, n_out = len(rs.arrays), len(rs.out_shapes)

    def body(*refs):
        bound = rs.bind(refs[:n_in], refs[n_in:n_in + n_out], refs[n_in + n_out:])
        rs.run("start", bound)
        rs.run("mid", bound)
        rs.run("late", bound)
        rs.run("end", bound)

    outs = _pallas(body, name=name, in_specs=rs.in_specs, out_specs=rs.out_specs, out_shape=rs.out_shapes,
                   input_output_aliases=rs.aliases, scratch_shapes=rs.scratch)(*rs.arrays)
    return rs.split(outs)


class _GatherRider(_Rider):
    X_LINK, Y_LINK, Y_PASS, X_PASS, D2D_X, D2D_Y, D2D_DIAG, N_SEMS = 0, 1, 2, 3, 4, 5, 6, 7

    def __init__(self, blocks, part=(0, 1, 1)):
        self.ins = tuple(blocks)
        self.part = part
        self.n_sems = self.N_SEMS * len(blocks)

    def _copy(self, out, send, recv, w, sem, chip_from, cc, to, sub=None):
        hr = self.ins[w].shape[1] // 2
        lo, hi, n = self.part
        first, size = cc * hr + lo * (hr // n), (hi - lo) * (hr // n)
        if sub is not None:
            size //= 2
            first += sub * size
        piece = out[w].at[2 * chip_from[0] + chip_from[1], pl.ds(first, size), :]
        return pltpu.make_async_remote_copy(src_ref=piece, dst_ref=piece, send_sem=send.at[self.N_SEMS * w + sem],
                                            recv_sem=recv.at[self.N_SEMS * w + sem], device_id=to, device_id_type=MESH)

    def _sent(self, out, send, recv, w):
        x, y, c = _place()
        me, sib = (x, y), (x, y, 1 - c)
        xn, yn, diag = _other_chips(x, y)
        cp = functools.partial(self._copy, out, send, recv, w)
        return [("start", cp(self.X_LINK, me, c, (*xn, c))), ("start", cp(self.Y_LINK, me, c, (*yn, c))),
                ("mid_x", cp(self.D2D_X, xn, c, sib)), ("mid_x", cp(self.Y_PASS, xn, c, (*yn, c), sub=0)),
                ("mid_y", cp(self.D2D_Y, yn, c, sib)), ("mid_y", cp(self.X_PASS, yn, c, (*xn, c), sub=1)),
                ("late", cp(self.D2D_DIAG, diag, c, sib))]

    def _go(self, out, send, recv, phase):
        for w in range(len(self.ins)):
            for ph, copy in self._sent(out, send, recv, w):
                if ph == phase:
                    copy.start()

    def start(self, reads, out, new, send, recv):
        self._go(out, send, recv, "start")

    def mid(self, reads, out, new, send, recv):
        x, y, c = _place()
        xn, yn, _ = _other_chips(x, y)
        for w in range(len(self.ins)):
            self._copy(out, send, recv, w, self.X_LINK, xn, c, (x, y, c)).wait_recv()
        self._go(out, send, recv, "mid_x")
        for w in range(len(self.ins)):
            self._copy(out, send, recv, w, self.Y_LINK, yn, c, (x, y, c)).wait_recv()
        self._go(out, send, recv, "mid_y")

    def late(self, reads, out, new, send, recv):
        x, y, c = _place()
        diag = _other_chips(x, y)[2]
        for w in range(len(self.ins)):
            self._copy(out, send, recv, w, self.Y_PASS, diag, c, (x, y, c), sub=0).wait_recv()
            self._copy(out, send, recv, w, self.X_PASS, diag, c, (x, y, c), sub=1).wait_recv()
        self._go(out, send, recv, "late")

    def end(self, reads, out, new, send, recv):
        x, y, c = _place()
        xn, yn, diag = _other_chips(x, y)
        for w in range(len(self.ins)):
            for sem, chip in ((self.D2D_X, xn), (self.D2D_Y, yn), (self.D2D_DIAG, diag)):
                self._copy(out, send, recv, w, sem, chip, 1 - c, (x, y, c)).wait_recv()
        for w in range(len(self.ins)):
            for _, copy in self._sent(out, send, recv, w):
                copy.wait_send()


class _SwapRider(_Rider):
    def __init__(self, grads):
        self.reads = tuple(grads)
        self.new = tuple(_sds((N_CHIPS, g.shape[1] // 2, g.shape[2]), g.dtype) for g in grads)
        self.n_sems = len(grads)

    def _copies(self, src, new, send, recv):
        x, y, c = _place()
        copies = []
        for w in range(len(self.reads)):
            hr = self.reads[w].shape[1] // 2
            copies.append(pltpu.make_async_remote_copy(
                src_ref=src[w].at[:, pl.ds((1 - c) * hr, hr), :], dst_ref=new[w],
                send_sem=send.at[w], recv_sem=recv.at[w], device_id=(x, y, 1 - c), device_id_type=MESH))
        return copies

    def start(self, src, ins, new, send, recv):
        for cp in self._copies(src, new, send, recv):
            cp.start()

    def end(self, src, ins, new, send, recv):
        for cp in self._copies(src, new, send, recv):
            cp.wait()


def _add_half(g, got, c_arr, name):
    nk, r, cols = g.shape
    hr = r // 2
    tr = min(hr, 1024)
    nb = hr // tr

    def body(c_ref, g_ref, got_ref, o_ref):
        o_ref[...] = (g_ref[...].astype(F32) + got_ref[...].astype(F32)).astype(o_ref.dtype)

    grid_spec = pltpu.PrefetchScalarGridSpec(
        num_scalar_prefetch=1, grid=(nk, nb),
        in_specs=[pl.BlockSpec((None, tr, cols), lambda k, i, c_ref: (k, c_ref[0] * nb + i, 0)),
                  pl.BlockSpec((None, tr, cols), lambda k, i, c_ref: (k, i, 0))],
        out_specs=pl.BlockSpec((None, tr, cols), lambda k, i, c_ref: (k, i, 0)))
    return _pallas(body, name=name, grid_spec=grid_spec, out_shape=_sds((nk, hr, cols), g.dtype),
                   compiler_params=_params("parallel", "parallel"))(c_arr, g, got)


class _SendPartialsRider(_Rider):
    def __init__(self, parts, got=None, part=(0, 1, 1)):
        self.reads = tuple(parts)
        if got is None:
            self.new = tuple(_sds((N_CHIPS - 1, *p.shape[1:]), p.dtype) for p in parts)
        else:
            self.ins = tuple(got)
        self.part = part
        self.n_sems = 3 * len(parts)

    def _copies(self, src, ins, new, send, recv):
        x, y, c = _place()
        land = ins if self.ins else new
        lo, hi, n = self.part
        copies = []
        for w in range(len(self.reads)):
            pr = self.reads[w].shape[1] // n
            rows = pl.ds(lo * pr, (hi - lo) * pr)
            for j, chip in enumerate(_other_chips(x, y)):
                copies.append(pltpu.make_async_remote_copy(
                    src_ref=src[w].at[2 * chip[0] + chip[1], rows, :], dst_ref=land[w].at[j, rows, :],
                    send_sem=send.at[3 * w + j], recv_sem=recv.at[3 * w + j], device_id=(*chip, c), device_id_type=MESH))
        return copies

    def start(self, src, ins, new, send, recv):
        for cp in self._copies(src, ins, new, send, recv):
            cp.start()

    def end(self, src, ins, new, send, recv):
        for cp in self._copies(src, ins, new, send, recv):
            cp.wait()


def _sum_partials(part, got, kc_arr, name):
    _, hr, cols = part.shape
    tr = min(hr, 512)
    nb = hr // tr

    def body(kc_ref, p_ref, g0_ref, g1_ref, g2_ref, o_ref):
        o_ref[...] = ((p_ref[...].astype(F32) + g0_ref[...].astype(F32)) + g1_ref[...].astype(F32)) + g2_ref[...].astype(F32)

    slot = lambda j: pl.BlockSpec((None, tr, cols), lambda i, kc_ref: (j, i, 0))
    grid_spec = pltpu.PrefetchScalarGridSpec(
        num_scalar_prefetch=1, grid=(nb,),
        in_specs=[pl.BlockSpec((None, tr, cols), lambda i, kc_ref: (kc_ref[0], i, 0)), slot(0), slot(1), slot(2)],
        out_specs=pl.BlockSpec((tr, cols), lambda i, kc_ref: (kc_ref[1] * nb + i, 0)))
    return _pallas(body, name=name, grid_spec=grid_spec, out_shape=_sds((2 * hr, cols), F32),
                   compiler_params=_params("parallel"))(kc_arr, part, got, got, got)


class _ShareRider(_Rider):
    def __init__(self, grads):
        self.ins = tuple(grads)
        self.n_sems = len(grads)

    def _copies(self, out, send, recv):
        x, y, c = _place()
        copies = []
        for w in range(len(self.ins)):
            hr = self.ins[w].shape[0] // 2
            mine = out[w].at[pl.ds(c * hr, hr), :]
            copies.append(pltpu.make_async_remote_copy(
                src_ref=mine, dst_ref=mine, send_sem=send.at[w], recv_sem=recv.at[w],
                device_id=(x, y, 1 - c), device_id_type=MESH))
        return copies

    def start(self, reads, out, new, send, recv):
        for cp in self._copies(out, send, recv):
            cp.start()

    def end(self, reads, out, new, send, recv):
        for cp in self._copies(out, send, recv):
            cp.wait()


def _small_allreduce_adamw(g_part, w, m, v, riders=()):
    rows = g_part.shape[0]
    rs = _Riders(riders, 4, 4)
    n_rin, n_rout = len(rs.arrays), len(rs.out_shapes)

    def body(*refs):
        g_ref, w_ref, m_ref, v_ref = refs[:4]
        go_ref, d_ref, nm_ref, nv_ref = refs[4 + n_rin:8 + n_rin]
        all_ref, send_sems, recv_sems = refs[8 + n_rin + n_rout:11 + n_rin + n_rout]
        bound = rs.bind(refs[4:4 + n_rin], refs[8 + n_rin:8 + n_rin + n_rout], refs[11 + n_rin + n_rout:])
        rs.run("start", bound)
        x, y, c = _place()
        me = 4 * x + 2 * y + c
        all_ref[me] = g_ref[...]
        copies = []
        for r in range(1, 8):
            dx, dy, dc = (r >> 2) & 1, (r >> 1) & 1, r & 1
            peer = (1 - x if dx else x, 1 - y if dy else y, 1 - c if dc else c)
            copies.append(pltpu.make_async_remote_copy(
                src_ref=g_ref, dst_ref=all_ref.at[me], send_sem=send_sems.at[r - 1], recv_sem=recv_sems.at[r - 1],
                device_id=peer, device_id_type=MESH))
        for cp in copies:
            cp.start()
        for cp in copies:
            cp.wait()
        tot = all_ref[0]
        for d in range(1, 8):
            tot = tot + all_ref[d]
        go_ref[...] = tot
        d_ref[...], nm_ref[...], nv_ref[...] = _adamw_math(w_ref[...], tot, m_ref[...], v_ref[...])
        for phase in ("mid", "late", "end"):
            rs.run(phase, bound)

    vm = pl.BlockSpec(memory_space=pltpu.VMEM)
    out = _pallas(
        body, name="small_allreduce_adamw", in_specs=[vm] * 4 + rs.in_specs, out_specs=[vm] * 4 + rs.out_specs,
        out_shape=[_sds((rows, 128), F32)] * 4 + rs.out_shapes, input_output_aliases=rs.aliases,
        scratch_shapes=[pltpu.VMEM((8, rows, 128), F32), pltpu.SemaphoreType.DMA((7,)), pltpu.SemaphoreType.DMA((7,)),
                        *rs.scratch],
    )(g_part, w, m, v, *rs.arrays)
    return out[:4], rs.split(out[4:])


SMALL_SIZES = (2048, 1024, 128, 128, HEADS * REL_SIZE, 2048)
SMALL_PART_ROWS = tuple(-(-size // 1024) * 8 for size in SMALL_SIZES)
SMALL_ROWS = sum(SMALL_PART_ROWS)


def _pack_small(parts):
    rows = []
    for p, size, nr in zip(parts, SMALL_SIZES, SMALL_PART_ROWS):
        rows.append(jnp.pad(p.reshape(-1), (0, nr * 128 - size)).reshape(nr, 128))
    return jnp.concatenate(rows, axis=0)


def _unpack_small(slab, shapes):
    out, off = [], 0
    for size, nr, shape in zip(SMALL_SIZES, SMALL_PART_ROWS, shapes):
        out.append(slab[off:off + nr].reshape(-1)[:size].reshape(shape))
        off += nr
    return out


def kernel(x, norm1_g, w_in, ret_norm_g, q_norm_g, k_norm_g, rel_bias, w_out, norm2_g, w_ff1, w_ff2, loss_target, m_norm1_g, m_w_in, m_ret_norm_g, m_q_norm_g, m_k_norm_g, m_rel_bias, m_w_out, m_norm2_g, m_w_ff1, m_w_ff2, v_norm1_g, v_w_in, v_ret_norm_g, v_q_norm_g, v_k_norm_g, v_rel_bias, v_w_out, v_norm2_g, v_w_ff1, v_w_ff2):
    xs = x[0]
    tgt = loss_target[0]
    s, d = xs.shape
    d_in = N_CHIPS * w_in.shape[2]
    d_ff = N_CHIPS * w_ff1.shape[2]
    in_sh, ff_sh = w_in.shape[2], w_ff1.shape[2]
    tm = min(s, 1024)
    gi = s // tm
    c_arr = lax.axis_index("c").astype(jnp.int32).reshape(1)
    k_arr = (2 * lax.axis_index("x") + lax.axis_index("y")).astype(jnp.int32).reshape(1)
    tables = _tables(s)
    bias = _rel_bias_expand(rel_bias[0][:, None, :])

    blk_in = [_cast_bf16(w_in[0], k_arr, "cast_w_in_%d" % half, cols=(half, 2)) for half in range(2)]
    blk_out, blk_ff1, blk_ff2 = (_cast_bf16(w_out[0], k_arr, "cast_w_out"), _cast_bf16(w_ff1[0], k_arr, "cast_w_ff1"),
                                 _cast_bf16(w_ff2[0], k_arr, "cast_w_ff2"))

    h1, ((wg_in0,),) = _rmsnorm_fwd(xs, norm1_g, "rmsnorm1", riders=[_GatherRider([blk_in[0]])])
    tn_in = in_sh // 2
    tk = d

    def proj_half(half, wg, through, riders):
        return _mm("proj_%d" % half, h1, wg, NN, (gi, N_CHIPS, 1),
                   pl.BlockSpec((tm, tk), lambda i, j, k: (i, 0)), pl.BlockSpec((None, tk, tn_in), lambda i, j, k: (j, 0, 0)),
                   [_sds((s, d_in), F32)], [pl.BlockSpec((tm, tn_in), lambda i, j, k: (i, 2 * j + half))], (tm, tn_in),
                   riders=riders, through=through)

    (proj,), ((wg_in1,),) = proj_half(0, wg_in0, None, [_GatherRider([blk_in[1]])])
    (proj,), ((wg_ff1,),) = proj_half(1, wg_in1, proj, [_GatherRider([blk_ff1], (0, 3, 8))])
    (mix, y_ret, prev), ((wg_ff1,),) = _retention_fwd(proj, ret_norm_g, tables, riders=[_GatherRider([wg_ff1], (3, 6, 8))])
    mix, ((wg_out,), (wg_ff2,)) = _attention_fwd(
        proj, q_norm_g, k_norm_g, bias, mix, riders=[_GatherRider([blk_out]), _GatherRider([blk_ff2], (0, 1, 4))])
    wg_out = wg_out.reshape(d, d)
    tn = 1024
    tile = pl.BlockSpec((tm, tn), lambda i, j, k: (i, j))
    def residual_norm(acc, res, g):
        x1v = res + acc
        yv = x1v * lax.rsqrt(jnp.mean(x1v * x1v, axis=-1, keepdims=True) + EPS)
        return x1v, yv * g

    tmo = min(s, 512)
    rows = pl.BlockSpec((tmo, d), lambda i, j, k: (i, 0))
    (x1, h2), ((wg_ff1,),) = _mm(
        "out_proj", mix, wg_out, NN, (s // tmo, 1, 1),
        rows, pl.BlockSpec((d, d), lambda i, j, k: (0, 0)),
        [_sds((s, d), F32), _sds((s, d), BF16)], [rows, rows], (tmo, d),
        extras=(xs, norm2_g), extra_specs=(rows, pl.BlockSpec((1, d), lambda i, j, k: (0, 0))),
        epi=residual_norm, riders=[_GatherRider([wg_ff1], (6, 8, 8))])
    tn_ff = min(ff_sh, 1024)
    per = ff_sh // tn_ff

    def relu2(acc):
        r = jnp.maximum(acc, 0.0)
        return acc, r * r

    (u, act), ((wg_ff2,),) = _mm(
        "ff1", h2, wg_ff1, NN, (gi, N_CHIPS * per, d // tk),
        pl.BlockSpec((tm, tk), lambda i, j, k: (i, k)),
        pl.BlockSpec((None, tk, tn_ff), lambda i, j, k: (j // per, k, j % per)),
        [_sds((s, d_ff), F32), _sds((s, d_ff), BF16)],
        [pl.BlockSpec((tm, tn_ff), lambda i, j, k: (i, j))] * 2, (tm, tn_ff), epi=relu2,
        riders=[_GatherRider([wg_ff2], (1, 4, 4))])
    wg_ff2 = wg_ff2.reshape(d_ff, d)

    def loss_epi(acc, res, t):
        diff = (res + acc) - t
        dy = diff / d
        return dy, dy, jnp.sum(diff * diff, axis=0, keepdims=True)

    tk2 = min(tk, 2048)
    dy, dyb, loss_cols = _mm(
        "ff2_loss", act, wg_ff2, NN, (gi, d // tn, d_ff // tk2),
        pl.BlockSpec((tm, tk2), lambda i, j, k: (i, k)), pl.BlockSpec((tk2, tn), lambda i, j, k: (k, j)),
        [_sds((s, d), F32), _sds((s, d), BF16), _sds((gi, 1, d), F32)],
        [tile, tile, pl.BlockSpec((None, 1, tn), lambda i, j, k: (i, 0, j))], (tm, tn),
        extras=(x1, tgt), extra_specs=(tile, tile), epi=loss_epi)
    loss = lax.psum(0.5 * jnp.sum(loss_cols) / d, ("x", "y", "c"))

    (du,) = _mm("d_act", dyb, wg_ff2, NT, (gi, d_ff // tn, d // tk),
                pl.BlockSpec((tm, tk), lambda i, j, k: (i, k)), pl.BlockSpec((tn, tk), lambda i, j, k: (j, k)),
                [_sds((s, d_ff), BF16)], [tile], (tm, tn), extras=(u,), extra_specs=(tile,),
                epi=lambda acc, uu: (acc * (2.0 * jnp.maximum(uu, 0.0)),))
    ts = min(s, 2048)
    wtile = pl.BlockSpec((tn, tn), lambda i, j, k: (i, j))
    (g_ff2,) = _mm("dw_ff2", act, dyb, TN, (d_ff // tn, d // tn, s // ts),
                   pl.BlockSpec((ts, tn), lambda i, j, k: (k, i)), pl.BlockSpec((ts, tn), lambda i, j, k: (k, j)),
                   [_sds((d_ff, d), BF16)], [wtile], (tn, tn))
    g_ff2 = g_ff2.reshape(N_CHIPS, d_ff // N_CHIPS, d)
    (g_ff1,), ((got_ff2,),) = _mm(
        "dw_ff1", h2, du, TN, (d // tn, N_CHIPS * per, s // ts),
        pl.BlockSpec((ts, tn), lambda i, j, k: (k, i)), pl.BlockSpec((ts, tn_ff), lambda i, j, k: (k, j)),
        [_sds((N_CHIPS, d, ff_sh), BF16)],
        [pl.BlockSpec((None, tn, tn_ff), lambda i, j, k: (j // per, i, j % per))], (tn, tn_ff),
        riders=[_SwapRider([g_ff2])])
    p_ff2 = _add_half(g_ff2, got_ff2, c_arr, "chip_partial_w_ff2")
    tkf = min(tk, ff_sh)
    kper = ff_sh // tkf
    (dh2,), ((got2_ff2,), (got_ff1,)) = _mm(
        "d_h2", du, wg_ff1, NT, (gi, d // tn, d_ff // tkf),
        pl.BlockSpec((tm, tkf), lambda i, j, k: (i, k)),
        pl.BlockSpec((None, tn, tkf), lambda i, j, k: (k // kper, j, k % kper)),
        [_sds((s, d), F32)], [tile], (tm, tn),
        riders=[_SendPartialsRider([p_ff2], part=(0, 3, 4)), _SwapRider([g_ff1])])
    p_ff1 = _add_half(g_ff1, got_ff1, c_arr, "chip_partial_w_ff1")
    dx1, dx1b, g_norm2 = _rmsnorm_bwd(x1, norm2_g, dh2, dy, "rmsnorm2_bwd")

    (dmix,) = _mm("d_mix", dx1b, wg_out, NT, (gi, d // tn, d // tk),
                  pl.BlockSpec((tm, tk), lambda i, j, k: (i, k)), pl.BlockSpec((tn, tk), lambda i, j, k: (j, k)),
                  [_sds((s, d), F32)], [tile], (tm, tn))
    (g_out,) = _mm("dw_out", mix, dx1b, TN, (d // tn, d // tn, s // ts),
                   pl.BlockSpec((ts, tn), lambda i, j, k: (k, i)), pl.BlockSpec((ts, tn), lambda i, j, k: (k, j)),
                   [_sds((d, d), BF16)], [wtile], (tn, tn))
    g_out = g_out.reshape(N_CHIPS, d // N_CHIPS, d)
    (dproj, g_gn), ((got2_ff2,), (got2_ff1,), (got_out,)) = _retention_bwd(
        proj, ret_norm_g, tables, y_ret, prev, dmix,
        riders=[_SendPartialsRider([p_ff2], got=[got2_ff2], part=(3, 4, 4)), _SendPartialsRider([p_ff1], part=(0, 2, 4)),
                _SwapRider([g_out])])
    p_out = _add_half(g_out, got_out, c_arr, "chip_partial_w_out")
    (dproj, g_gq, g_gk, dbias), ((got2_ff1,), (got2_out,)) = _attention_bwd(
        proj, q_norm_g, k_norm_g, bias, dmix, dproj,
        riders=[_SendPartialsRider([p_ff1], got=[got2_ff1], part=(2, 4, 4)), _SendPartialsRider([p_out])])
    g_rel = _rel_bias_fold(dbias)
    names = ["w_in", "w_out", "w_ff1", "w_ff2"]
    kc_arr = jnp.concatenate([k_arr, c_arr])
    early = [_sum_partials(p, r, kc_arr, "sum_partials_" + nm)
             for p, r, nm in zip((p_out, p_ff1, p_ff2), (got2_out, got2_ff1, got2_ff2), names[1:])]
    (g_in,), (early,) = _mm(
        "dw_in", h1, dproj, TN, (d // tn, 2 * N_CHIPS, s // ts),
        pl.BlockSpec((ts, tn), lambda i, j, k: (k, i)), pl.BlockSpec((ts, tn_in), lambda i, j, k: (k, j)),
        [_sds((N_CHIPS, d, in_sh), BF16)],
        [pl.BlockSpec((None, tn, tn_in), lambda i, j, k: (j // 2, i, j % 2))], (tn, tn_in), riders=[_ShareRider(early)])
    ((got_in,),) = _run_riders("grad_swap_w_in", [_SwapRider([g_in])])
    p_in = _add_half(g_in, got_in, c_arr, "chip_partial_w_in")
    half_spec = pl.BlockSpec((None, tn, tn_in), lambda i, j, k: (k // 2, j, 0))
    (dh1,), ((got2_in,),) = _mm(
        "d_h1", dproj, [wg_in0, wg_in1], NT, (gi, d // tn, 2 * N_CHIPS),
        pl.BlockSpec((tm, tn_in), lambda i, j, k: (i, k)), [half_spec, half_spec],
        [_sds((s, d), F32)], [tile], (tm, tn), riders=[_SendPartialsRider([p_in])])
    grad_x, _, g_norm1 = _rmsnorm_bwd(xs, norm1_g, dh1, dx1, "rmsnorm1_bwd")

    small_w = (norm1_g, ret_norm_g, q_norm_g, k_norm_g, rel_bias, norm2_g)
    small_m = (m_norm1_g, m_ret_norm_g, m_q_norm_g, m_k_norm_g, m_rel_bias, m_norm2_g)
    small_v = (v_norm1_g, v_ret_norm_g, v_q_norm_g, v_k_norm_g, v_rel_bias, v_norm2_g)
    shapes = [p.shape for p in small_w]
    g_small = _pack_small([g_norm1, g_gn, g_gq, g_gk, g_rel, g_norm2])
    small_out, ((g_w_in,),) = _small_allreduce_adamw(
        g_small, _pack_small(small_w), _pack_small(small_m), _pack_small(small_v),
        riders=[_ShareRider([_sum_partials(p_in, got2_in, kc_arr, "sum_partials_w_in")])])
    sg, sd, sm, sv = (_unpack_small(a, shapes) for a in small_out)

    g_big = [g_w_in, *early]
    big = []
    for g, w, m, v, nm in zip(g_big, (w_in, w_out, w_ff1, w_ff2), (m_w_in, m_w_out, m_w_ff1, m_w_ff2),
                              (v_w_in, v_w_out, v_w_ff1, v_w_ff2), names):
        g, delta, new_m, new_v = _adamw(w[0], g, m[0], v[0], "adamw_" + nm)
        big.append((g[None], delta[None], new_m[None], new_v[None]))

    def ordered(kind):
        sm_ = (sg, sd, sm, sv)[kind]
        return (sm_[0], big[0][kind], sm_[1], sm_[2], sm_[3], sm_[4], big[1][kind], sm_[5], big[2][kind], big[3][kind])

    return (loss, grad_x[None], *ordered(0), *ordered(1), *ordered(2), *ordered(3))
```

```python
import functools

import jax
import jax.numpy as jnp
from jax import lax
from jax.experimental import pallas as pl
from jax.experimental.pallas import tpu as pltpu

F32 = jnp.float32
BF16 = jnp.bfloat16
MXU_DTYPE = jnp.bfloat16

CHUNK = 64
HEADS = 8
HEAD_DIM = 128
LEFT_CHUNKS = 8
BAND = (LEFT_CHUNKS + 1) * CHUNK
REL_CLIP = 128
REL_SIZE = (CHUNK - 1) + REL_CLIP + 1
RET_BLOCK_CHUNKS = 16
RET_ROWS = RET_BLOCK_CHUNKS * CHUNK
RET_SUB = 256
ROPE_BASE = 10000.0
EPS = 1e-6
GN_EPS = 1e-5
ADAM_LR, ADAM_B1, ADAM_B2, ADAM_EPS, ADAM_WD, ADAM_STEP = 0.001, 0.9, 0.999, 1e-08, 0.01, 10
N_CHIPS = 4
VMEM_LIMIT = 56 * 1024 * 1024
MESH = pl.DeviceIdType.MESH
ANY = pl.BlockSpec(memory_space=pl.ANY)

NN = (((1,), (0,)), ((), ()))
NT = (((1,), (1,)), ((), ()))
TN = (((0,), (0,)), ((), ()))


def _pallas(body, **kw):
    return pl.pallas_call(body, **kw)


def _params(*sem):
    return pltpu.CompilerParams(dimension_semantics=sem, vmem_limit_bytes=VMEM_LIMIT)


def _dot(a, b, dims):
    return lax.dot_general(a.astype(MXU_DTYPE), b.astype(MXU_DTYPE), dims, preferred_element_type=F32)


RIDER_MID, RIDER_LATE = 0.6, 0.9


def _mm(name, a, b, dims, grid, a_spec, b_spec, outs, o_specs, acc_shape, extras=(), extra_specs=(), epi=None,
        riders=(), through=None):
    ni, nj, nk = grid
    n_ex, n_out = len(extras), len(outs)
    bs = list(b) if isinstance(b, (list, tuple)) else [b]
    b_specs = list(b_spec) if isinstance(b, (list, tuple)) else [b_spec]
    extras, extra_specs = (*bs[1:], *extras), (*b_specs[1:], *extra_specs)
    b, b_spec, n_b = bs[0], b_specs[0], len(bs)
    n_in = 1 + n_b + n_ex
    rs = _Riders(riders, n_in, n_out)
    n_rin, n_rout = len(rs.arrays), len(rs.out_shapes)
    steps = ni * nj * nk

    held = [] if through is None else [through]

    def body(*refs):
        a_ref, b_refs = refs[0], refs[1:1 + n_b]
        b_ref = b_refs[0]
        ex_refs = refs[1 + n_b:n_in]
        outs_at = n_in + n_rin + len(held)
        o_refs = refs[outs_at:outs_at + n_out]
        acc_ref = refs[outs_at + n_out + n_rout]
        k = pl.program_id(2)
        if riders:
            bound = rs.bind(refs[n_in:n_in + n_rin], refs[outs_at + n_out:outs_at + n_out + n_rout],
                            refs[outs_at + n_out + n_rout + 1:])
            step = (pl.program_id(0) * nj + pl.program_id(1)) * nk + k
            rs.hooks(step, steps, bound)

        def finish(acc):
            vals = epi(acc, *[r[...] for r in ex_refs]) if epi is not None else (acc,)
            for r, v in zip(o_refs, vals):
                r[...] = v.astype(r.dtype)

        if nk == 1:
            finish(_dot(a_ref[...], b_ref[...], dims))
        else:
            @pl.when(k == 0)
            def _():
                acc_ref[...] = jnp.zeros_like(acc_ref)

            for t, ref in enumerate(b_refs):
                def step_with(ref=ref):
                    acc_ref[...] += _dot(a_ref[...], ref[...], dims)

                if n_b == 1:
                    step_with()
                else:
                    pl.when(k % n_b == t)(step_with)
            pl.when(k == nk - 1)(lambda: finish(acc_ref[...]))

        if riders:
            pl.when(step == steps - 1)(lambda: rs.run("end", bound))

    res = _pallas(
        body, name=name, grid=grid, in_specs=[a_spec, b_spec, *extra_specs, *rs.in_specs, *[ANY for _ in held]],
        out_specs=[*o_specs, *rs.out_specs], out_shape=[*outs, *rs.out_shapes],
        input_output_aliases={**rs.aliases, **{n_in + n_rin: 0 for _ in held}},
        scratch_shapes=[pltpu.VMEM(acc_shape if nk > 1 else (8, 128), F32), *rs.scratch],
        compiler_params=_params(*(("arbitrary",) * 3 if riders else ("parallel", "parallel", "arbitrary"))),
    )(a, b, *extras, *rs.arrays, *held)
    return (res[:n_out], rs.split(res[n_out:])) if riders else res


def _sds(shape, dtype):
    return jax.ShapeDtypeStruct(shape, dtype)


def _cast_bf16(w, k_arr, name, cols=(0, 1)):
    r, c = w.shape[0], w.shape[1] // cols[1]
    tr = min(r, 512)

    def body(k_ref, w_ref, o_ref):
        o_ref[...] = w_ref[...].astype(BF16)

    grid_spec = pltpu.PrefetchScalarGridSpec(
        num_scalar_prefetch=1, grid=(r // tr,), in_specs=[pl.BlockSpec((tr, c), lambda i, k_ref: (i, cols[0]))],
        out_specs=pl.BlockSpec((None, tr, c), lambda i, k_ref: (k_ref[0], i, 0)))
    return _pallas(body, name=name, grid_spec=grid_spec, out_shape=_sds((N_CHIPS, r, c), BF16),
                   compiler_params=_params("parallel"))(k_arr, w)


def _rmsnorm_fwd(x, g, name, riders=()):
    s, d = x.shape
    tr = 512

    def body(x_ref, g_ref, o_ref):
        xv = x_ref[...]
        y = xv * lax.rsqrt(jnp.mean(xv * xv, axis=-1, keepdims=True) + EPS)
        o_ref[...] = (y * g_ref[...]).astype(o_ref.dtype)

    rs = _Riders(riders, 2, 1)
    out = _pallas(_with_riders(body, 2, 1, 0, rs, (s // tr,)), name=name, grid=(s // tr,),
                  in_specs=[pl.BlockSpec((tr, d), lambda i: (i, 0)), pl.BlockSpec((1, d), lambda i: (0, 0)), *rs.in_specs],
                  out_specs=[pl.BlockSpec((tr, d), lambda i: (i, 0)), *rs.out_specs],
                  out_shape=[_sds((s, d), BF16), *rs.out_shapes], input_output_aliases=rs.aliases,
                  scratch_shapes=rs.scratch, compiler_params=_params("arbitrary"))(x, g, *rs.arrays)
    return out[0], rs.split(out[1:])


def _rmsnorm_bwd(x, g, dh, res, name, riders=()):
    s, d = x.shape
    tr = 512

    def body(x_ref, g_ref, dh_ref, res_ref, dx_ref, dxb_ref, dg_ref):
        i = pl.program_id(0)
        xv = x_ref[...]
        rstd = lax.rsqrt(jnp.mean(xv * xv, axis=-1, keepdims=True) + EPS)
        xh = xv * rstd
        dhv = dh_ref[...]

        @pl.when(i == 0)
        def _():
            dg_ref[...] = jnp.zeros_like(dg_ref)

        dg_ref[...] += jnp.sum(dhv * xh, axis=0, keepdims=True)
        dxh = dhv * g_ref[...]
        dx = res_ref[...] + rstd * (dxh - xh * jnp.mean(dxh * xh, axis=-1, keepdims=True))
        dx_ref[...] = dx
        dxb_ref[...] = dx.astype(BF16)

    row = pl.BlockSpec((tr, d), lambda i: (i, 0))
    vec = pl.BlockSpec((1, d), lambda i: (0, 0))
    rs = _Riders(riders, 4, 3)
    out = _pallas(_with_riders(body, 4, 3, 0, rs, (s // tr,)), name=name, grid=(s // tr,),
                  in_specs=[row, vec, row, row, *rs.in_specs], out_specs=[row, row, vec, *rs.out_specs],
                  out_shape=[_sds((s, d), F32), _sds((s, d), BF16), _sds((1, d), F32), *rs.out_shapes],
                  input_output_aliases=rs.aliases, scratch_shapes=rs.scratch,
                  compiler_params=_params("arbitrary"))(x, g, dh, res, *rs.arrays)
    return (out[:3], rs.split(out[3:])) if riders else out


def _adamw_math(w, g, m, v):
    m = ADAM_B1 * m + (1.0 - ADAM_B1) * g
    v = ADAM_B2 * v + (1.0 - ADAM_B2) * (g * g)
    m_hat = m / (1.0 - ADAM_B1 ** ADAM_STEP)
    v_hat = v / (1.0 - ADAM_B2 ** ADAM_STEP)
    delta = -ADAM_LR * (m_hat / (jnp.sqrt(v_hat) + ADAM_EPS) + ADAM_WD * w)
    return delta, m, v


def _adamw(w, g, m, v, name):
    r, c = w.shape
    tr = 256

    def body(w_ref, g_ref, m_ref, v_ref, go_ref, d_ref, nm_ref, nv_ref):
        g = g_ref[...]
        go_ref[...] = g
        d_ref[...], nm_ref[...], nv_ref[...] = _adamw_math(w_ref[...], g, m_ref[...], v_ref[...])

    blk = pl.BlockSpec((tr, c), lambda i: (i, 0))
    return _pallas(body, name=name, grid=(r // tr,), in_specs=[blk] * 4, out_specs=[blk] * 4,
                   out_shape=[_sds((r, c), F32)] * 4, compiler_params=_params("parallel"))(w, g, m, v)


def _tables(s):
    half = HEAD_DIM // 2
    pos = jnp.arange(s, dtype=F32)
    inv_freq = ROPE_BASE ** (-jnp.arange(half, dtype=F32) / half)
    ang = pos[:, None] * inv_freq[None, :]
    cos, sin = jnp.cos(ang), jnp.sin(ang)
    cos_f = jnp.concatenate([cos, cos], axis=-1)
    sin_f = jnp.concatenate([-sin, sin], axis=-1)
    log_g = jnp.log1p(-jnp.exp2(-(5.0 + jnp.arange(HEADS, dtype=F32))))
    p = jnp.arange(CHUNK, dtype=F32)
    decay = jnp.exp(log_g[:, None, None] * jnp.abs(p[:, None] - p[None, :]))
    k_dec = jnp.exp(log_g[None, :] * (CHUNK - 1.0 - p)[:, None])
    q_dec = jnp.exp(log_g[None, :] * (p + 1.0)[:, None])
    c_dec = jnp.exp(log_g * CHUNK)
    k_dec = jnp.tile(jnp.broadcast_to(k_dec.T[:, :, None], (HEADS, CHUNK, HEAD_DIM)), (1, RET_BLOCK_CHUNKS, 1))
    q_dec = jnp.tile(jnp.broadcast_to(q_dec.T[:, :, None], (HEADS, CHUNK, HEAD_DIM)), (1, RET_BLOCK_CHUNKS, 1))
    c_dec = jnp.broadcast_to(c_dec[:, None, None], (HEADS, 1, HEAD_DIM))
    n = RET_SUB // CHUNK
    decay = (jnp.eye(n, dtype=F32)[None, :, None, :, None] * decay[:, None, :, None, :]).reshape(HEADS, RET_SUB, RET_SUB)
    return cos_f, sin_f, decay, k_dec, q_dec, c_dec


def _rot(x, cos_f, sin_f):
    return x * cos_f + pltpu.roll(x, HEAD_DIM // 2, 1) * sin_f


def _rot_bwd(d, cos_f, sin_f):
    return d * cos_f + pltpu.roll(d * sin_f, HEAD_DIM // 2, 1)


K_SCALE = HEAD_DIM ** -0.5


def _retention_fwd(proj, gn_g, tables, riders=()):
    s = proj.shape[0]
    nb = s // RET_ROWS
    nc = s // CHUNK
    cos_f, sin_f, decay, k_dec, q_dec, c_dec = tables

    def body(q_ref, k_ref, v_ref, g_ref, cos_ref, sin_ref, dec_ref, kd_ref, qd_ref, cd_ref, gn_ref,
             ret_ref, y_ref, prev_ref, state_ref):
        @pl.when(pl.program_id(1) == 0)
        def _():
            state_ref[...] = jnp.zeros_like(state_ref)

        cosv, sinv = cos_ref[...], sin_ref[...]
        q = _rot(q_ref[...], cosv, sinv)
        k = _rot(k_ref[...], cosv, sinv) * K_SCALE
        v = v_ref[...]
        rg = g_ref[...]
        dec, cd, gn = dec_ref[...], cd_ref[...], gn_ref[...]
        kdf, qdf = k * kd_ref[...], q * qd_ref[...]
        chunks = [slice(c * CHUNK, (c + 1) * CHUNK) for c in range(RET_BLOCK_CHUNKS)]
        contribs = [_dot(kdf[rows], v[rows], TN) for rows in chunks]
        state, states = state_ref[...], []
        for c in range(RET_BLOCK_CHUNKS):
            states.append(state)
            prev_ref[c] = state.astype(prev_ref.dtype)
            state = cd * state + contribs[c]
        state_ref[...] = state
        cross = jnp.concatenate([_dot(qdf[rows], st, NN) for rows, st in zip(chunks, states)], axis=0)
        intra = []
        for b in range(RET_ROWS // RET_SUB):
            rows = slice(b * RET_SUB, (b + 1) * RET_SUB)
            intra.append(_dot(_dot(q[rows], k[rows], NT) * dec, v[rows], NN))
        y = jnp.concatenate(intra, axis=0) + cross
        y_ref[...] = y
        mu = jnp.mean(y, axis=-1, keepdims=True)
        yc = y - mu
        var = jnp.mean(yc * yc, axis=-1, keepdims=True)
        yn = yc * lax.rsqrt(var + GN_EPS) * gn
        ret_ref[...] = (rg * jax.nn.sigmoid(rg) * yn).astype(ret_ref.dtype)

    def col(off):
        return pl.BlockSpec((RET_ROWS, HEAD_DIM), lambda h, i: (i, off + h))

    pos = pl.BlockSpec((RET_ROWS, HEAD_DIM), lambda h, i: (i, 0))
    per_head = lambda shape: pl.BlockSpec((None, *shape), lambda h, i: (h, 0, 0))
    rs = _Riders(riders, 11, 3)
    res = _pallas(
        _with_riders(body, 11, 3, 1, rs, (HEADS, nb)), name="retention_fwd", grid=(HEADS, nb),
        in_specs=[col(0), col(HEADS), col(2 * HEADS), col(3 * HEADS), pos, pos,
                  per_head((RET_SUB, RET_SUB)), per_head((RET_ROWS, HEAD_DIM)), per_head((RET_ROWS, HEAD_DIM)),
                  per_head((1, HEAD_DIM)), pl.BlockSpec((1, HEAD_DIM), lambda h, i: (0, h)), *rs.in_specs],
        out_specs=[col(0), col(0),
                   pl.BlockSpec((None, RET_BLOCK_CHUNKS, HEAD_DIM, HEAD_DIM), lambda h, i: (h, i, 0, 0)),
                   *rs.out_specs],
        out_shape=[_sds((s, 2 * HEADS * HEAD_DIM), BF16), _sds((s, HEADS * HEAD_DIM), F32),
                   _sds((HEADS, nc, HEAD_DIM, HEAD_DIM), MXU_DTYPE), *rs.out_shapes],
        input_output_aliases=rs.aliases,
        scratch_shapes=[pltpu.VMEM((HEAD_DIM, HEAD_DIM), F32), *rs.scratch],
        compiler_params=_params("arbitrary", "arbitrary"),
    )(proj, proj, proj, proj, cos_f, sin_f, decay, k_dec, q_dec, c_dec, gn_g, *rs.arrays)
    return res[:3], rs.split(res[3:])


def _retention_bwd(proj, gn_g, tables, y, prev, dmix, riders=()):
    s = proj.shape[0]
    nb = s // RET_ROWS
    cos_f, sin_f, decay, k_dec, q_dec, c_dec = tables

    def body(q_ref, k_ref, v_ref, g_ref, cos_ref, sin_ref, dec_ref, kd_ref, qd_ref, cd_ref, gn_ref,
             y_ref, prev_ref, dret_ref, dproj_ref, dgn_ref, gstate_ref, stage_ref, stage_sems):
        head, blk = pl.program_id(0), pl.program_id(1)
        step = head * nb + blk
        slot = step % 2

        def writes(sl):
            rows = pl.ds(pl.multiple_of((nb - 1 - blk) * RET_ROWS, RET_ROWS), RET_ROWS)
            return [pltpu.make_async_copy(
                stage_ref.at[sl, g], dproj_ref.at[rows, pl.ds(pl.multiple_of((g * HEADS + head) * HEAD_DIM, HEAD_DIM), HEAD_DIM)],
                stage_sems.at[sl, g]) for g in range(4)]

        @pl.when(step >= 2)
        def _():
            for cp in writes(slot):
                cp.wait()

        @pl.when(blk == 0)
        def _():
            gstate_ref[...] = jnp.zeros_like(gstate_ref)
            dgn_ref[...] = jnp.zeros_like(dgn_ref)

        cosv, sinv = cos_ref[...], sin_ref[...]
        q = _rot(q_ref[...], cosv, sinv)
        k = _rot(k_ref[...], cosv, sinv) * K_SCALE
        v = v_ref[...]
        dec, kd, qd, cd, gn = dec_ref[...], kd_ref[...], qd_ref[...], cd_ref[...], gn_ref[...]
        kdf, qdf = k * kd, q * qd
        rg = g_ref[...]
        yv = y_ref[...]
        dret = dret_ref[...]
        sig = jax.nn.sigmoid(rg)
        gate = rg * sig
        mu = jnp.mean(yv, axis=-1, keepdims=True)
        yc = yv - mu
        rstd = lax.rsqrt(jnp.mean(yc * yc, axis=-1, keepdims=True) + GN_EPS)
        z = yc * rstd
        dyn = dret * gate
        stage_ref[slot, 3] = (dret * (z * gn) * (sig * (1.0 + rg * (1.0 - sig)))).astype(stage_ref.dtype)
        dgn_ref[...] += jnp.sum(dyn * z, axis=0, keepdims=True)
        dz = dyn * gn
        dy = rstd * (dz - jnp.mean(dz, axis=-1, keepdims=True) - z * jnp.mean(dz * z, axis=-1, keepdims=True))
        chunks = [slice(c * CHUNK, (c + 1) * CHUNK) for c in range(RET_BLOCK_CHUNKS)]
        dprevs = [_dot(qdf[rows], dy[rows], TN) for rows in chunks]
        gst, gsts = gstate_ref[...], [None] * RET_BLOCK_CHUNKS
        for c in reversed(range(RET_BLOCK_CHUNKS)):
            gsts[c] = gst
            gst = dprevs[c] + cd * gst
        gstate_ref[...] = gst
        dq = jnp.concatenate([_dot(dy[rows], prev_ref[c], NT) for c, rows in enumerate(chunks)], axis=0) * qd
        dk = jnp.concatenate([_dot(v[rows], g, NT) for rows, g in zip(chunks, gsts)], axis=0) * kd
        dv = jnp.concatenate([_dot(kdf[rows], g, NN) for rows, g in zip(chunks, gsts)], axis=0)
        dqi, dki, dvi = [], [], []
        for b in range(RET_ROWS // RET_SUB):
            rows = slice(b * RET_SUB, (b + 1) * RET_SUB)
            qs, ks, vs, dys = q[rows], k[rows], v[rows], dy[rows]
            dvi.append(_dot(_dot(ks, qs, NT) * dec, dys, NN))
            dqi.append(_dot(_dot(dys, vs, NT) * dec, ks, NN))
            dki.append(_dot(_dot(vs, dys, NT) * dec, qs, NN))
        dq = dq + jnp.concatenate(dqi, axis=0)
        dk = dk + jnp.concatenate(dki, axis=0)
        dv = dv + jnp.concatenate(dvi, axis=0)
        stage_ref[slot, 0] = _rot_bwd(dq, cosv, sinv).astype(stage_ref.dtype)
        stage_ref[slot, 1] = _rot_bwd(dk * K_SCALE, cosv, sinv).astype(stage_ref.dtype)
        stage_ref[slot, 2] = dv.astype(stage_ref.dtype)
        for cp in writes(slot):
            cp.start()

        @pl.when(step == HEADS * nb - 1)
        def _():
            for cp in writes(1 - slot) + writes(slot):
                cp.wait()

    rev = lambda i: nb - 1 - i

    def col(off):
        return pl.BlockSpec((RET_ROWS, HEAD_DIM), lambda h, i: (rev(i), off + h))

    pos = pl.BlockSpec((RET_ROWS, HEAD_DIM), lambda h, i: (rev(i), 0))
    per_head = lambda shape: pl.BlockSpec((None, *shape), lambda h, i: (h, 0, 0))
    rs = _Riders(riders, 14, 2)
    res = _pallas(
        _with_riders(body, 14, 2, 3, rs, (HEADS, nb)), name="retention_bwd", grid=(HEADS, nb),
        in_specs=[col(0), col(HEADS), col(2 * HEADS), col(3 * HEADS), pos, pos,
                  per_head((RET_SUB, RET_SUB)), per_head((RET_ROWS, HEAD_DIM)), per_head((RET_ROWS, HEAD_DIM)),
                  per_head((1, HEAD_DIM)), pl.BlockSpec((1, HEAD_DIM), lambda h, i: (0, h)),
                  col(0), pl.BlockSpec((None, RET_BLOCK_CHUNKS, HEAD_DIM, HEAD_DIM), lambda h, i: (h, rev(i), 0, 0)),
                  col(0), *rs.in_specs],
        out_specs=[ANY, per_head((1, HEAD_DIM)), *rs.out_specs],
        out_shape=[_sds((s, proj.shape[1]), BF16), _sds((HEADS, 1, HEAD_DIM), F32), *rs.out_shapes],
        input_output_aliases=rs.aliases,
        scratch_shapes=[pltpu.VMEM((HEAD_DIM, HEAD_DIM), F32), pltpu.VMEM((2, 4, RET_ROWS, HEAD_DIM), BF16),
                        pltpu.SemaphoreType.DMA((2, 4)), *rs.scratch],
        compiler_params=_params("arbitrary", "arbitrary"),
    )(proj, proj, proj, proj, cos_f, sin_f, decay, k_dec, q_dec, c_dec, gn_g, y, prev, dmix, *rs.arrays)
    return res[:2], rs.split(res[2:])


ATT_COL0 = 4 * HEADS
PAD_ROWS = LEFT_CHUNKS * CHUNK
NORM_ROWS = 512
GROUP_CHUNKS = 4
GROUP = GROUP_CHUNKS * CHUNK
WIN = (LEFT_CHUNKS + GROUP_CHUNKS) * CHUNK
MASKED = -1e30


def _qk_norm(x, g):
    return x * lax.rsqrt(jnp.mean(x * x, axis=-1, keepdims=True) + EPS) * g


def _band_probs(qb, kb, bias, g):
    sc = _dot(qb, kb, NT) * K_SCALE + bias
    win_chunk = lax.broadcasted_iota(jnp.int32, (GROUP, WIN), 1) // CHUNK
    sc = jnp.where(g * GROUP_CHUNKS - LEFT_CHUNKS + win_chunk >= 0, sc, MASKED)
    e = jnp.exp(sc - jnp.max(sc, axis=-1, keepdims=True))
    return e / jnp.sum(e, axis=-1, keepdims=True)


def _with_riders(core, n_in, n_out, n_scratch, rs, grid):
    n_rin, n_rout = len(rs.arrays), len(rs.out_shapes)
    if not rs.riders:
        return core
    steps = 1
    for n in grid:
        steps *= n

    def body(*refs):
        outs_at = n_in + n_rin
        scratch_at = outs_at + n_out + n_rout
        bound = rs.bind(refs[n_in:outs_at], refs[outs_at + n_out:scratch_at], refs[scratch_at + n_scratch:])
        step = 0
        for axis, n in enumerate(grid):
            step = step * n + pl.program_id(axis)
        rs.hooks(step, steps, bound)
        core(*refs[:n_in], *refs[outs_at:outs_at + n_out], *refs[scratch_at:scratch_at + n_scratch])
        pl.when(step == steps - 1)(lambda: rs.run("end", bound))

    return body


def _attention_fwd(proj, gq, gk, bias, mix, riders=()):
    s = proj.shape[0]
    rs = _Riders(riders, 7, 1)

    def body(q_ref, k_ref, v_ref, gq_ref, gk_ref, bias_ref, mix_ref, o_ref, kp_ref, vp_ref):
        kp_ref[0:PAD_ROWS, :] = jnp.zeros((PAD_ROWS, HEAD_DIM), kp_ref.dtype)
        vp_ref[0:PAD_ROWS, :] = jnp.zeros((PAD_ROWS, HEAD_DIM), vp_ref.dtype)
        gqv, gkv = gq_ref[...], gk_ref[...]

        def fill(b, carry):
            r0 = pl.multiple_of(b * NORM_ROWS, NORM_ROWS)
            kp_ref[pl.ds(PAD_ROWS + r0, NORM_ROWS), :] = _qk_norm(k_ref[pl.ds(r0, NORM_ROWS), :], gkv).astype(kp_ref.dtype)
            vp_ref[pl.ds(PAD_ROWS + r0, NORM_ROWS), :] = v_ref[pl.ds(r0, NORM_ROWS), :].astype(vp_ref.dtype)
            return carry

        lax.fori_loop(0, s // NORM_ROWS, fill, 0, unroll=2)

        def group(g, carry):
            r0 = pl.multiple_of(g * GROUP, GROUP)
            qn = _qk_norm(q_ref[pl.ds(r0, GROUP), :], gqv)
            p = _band_probs(qn, kp_ref[pl.ds(r0, WIN), :], bias_ref[...], g)
            o_ref[pl.ds(r0, GROUP), :] = _dot(p, vp_ref[pl.ds(r0, WIN), :], NN).astype(o_ref.dtype)
            return carry

        lax.fori_loop(0, s // GROUP, group, 0, unroll=8)

    def col(off):
        return pl.BlockSpec((s, HEAD_DIM), lambda h: (0, off + h))

    vec = pl.BlockSpec((1, HEAD_DIM), lambda h: (0, 0))
    res = _pallas(
        _with_riders(body, 7, 1, 2, rs, (HEADS,)), name="attention_fwd", grid=(HEADS,),
        in_specs=[col(ATT_COL0), col(ATT_COL0 + HEADS), col(ATT_COL0 + 2 * HEADS), vec, vec,
                  pl.BlockSpec((None, GROUP, WIN), lambda h: (h, 0, 0)), ANY, *rs.in_specs],
        out_specs=[col(HEADS), *rs.out_specs], out_shape=[_sds(mix.shape, mix.dtype), *rs.out_shapes],
        input_output_aliases={6: 0, **rs.aliases},
        scratch_shapes=[pltpu.VMEM((s + PAD_ROWS, HEAD_DIM), MXU_DTYPE), pltpu.VMEM((s + PAD_ROWS, HEAD_DIM), MXU_DTYPE),
                        *rs.scratch],
        compiler_params=_params("arbitrary"),
    )(proj, proj, proj, gq, gk, bias, mix, *rs.arrays)
    return res[0], rs.split(res[1:])


def _attention_bwd(proj, gq, gk, bias, dmix, dproj, riders=()):
    s = proj.shape[0]
    rs = _Riders(riders, 8, 4)

    def body(q_ref, k_ref, v_ref, gq_ref, gk_ref, bias_ref, do_ref, dproj_in_ref,
             dproj_ref, dgq_ref, dgk_ref, dbias_ref, kp_ref, vp_ref, dkp_ref, dvp_ref, dqn_ref, stage_ref, stage_sems):
        head = pl.program_id(0)

        def writes():
            return [pltpu.make_async_copy(
                stage_ref.at[g],
                dproj_ref.at[:, pl.ds(pl.multiple_of((ATT_COL0 + g * HEADS + head) * HEAD_DIM, HEAD_DIM), HEAD_DIM)],
                stage_sems.at[g]) for g in range(3)]

        kp_ref[0:PAD_ROWS, :] = jnp.zeros((PAD_ROWS, HEAD_DIM), kp_ref.dtype)
        vp_ref[0:PAD_ROWS, :] = jnp.zeros((PAD_ROWS, HEAD_DIM), vp_ref.dtype)
        dkp_ref[...] = jnp.zeros_like(dkp_ref)
        dvp_ref[...] = jnp.zeros_like(dvp_ref)
        dbias_ref[...] = jnp.zeros_like(dbias_ref)
        gqv, gkv = gq_ref[...], gk_ref[...]

        def fill(b, carry):
            r0 = pl.multiple_of(b * NORM_ROWS, NORM_ROWS)
            kp_ref[pl.ds(PAD_ROWS + r0, NORM_ROWS), :] = _qk_norm(k_ref[pl.ds(r0, NORM_ROWS), :], gkv).astype(kp_ref.dtype)
            vp_ref[pl.ds(PAD_ROWS + r0, NORM_ROWS), :] = v_ref[pl.ds(r0, NORM_ROWS), :].astype(vp_ref.dtype)
            return carry

        lax.fori_loop(0, s // NORM_ROWS, fill, 0, unroll=2)

        def group(g, carry):
            r0 = pl.multiple_of(g * GROUP, GROUP)
            qn = _qk_norm(q_ref[pl.ds(r0, GROUP), :], gqv)
            kb = kp_ref[pl.ds(r0, WIN), :]
            vb = vp_ref[pl.ds(r0, WIN), :]
            p = _band_probs(qn, kb, bias_ref[...], g)
            do = do_ref[pl.ds(r0, GROUP), :]
            dvp_ref[pl.ds(r0, WIN), :] += _dot(p, do, TN)
            dp = _dot(do, vb, NT)
            ds = p * (dp - jnp.sum(dp * p, axis=-1, keepdims=True))
            dbias_ref[...] += ds
            dss = ds * K_SCALE
            dqn_ref[pl.ds(r0, GROUP), :] = _dot(dss, kb, NN)
            dkp_ref[pl.ds(r0, WIN), :] += _dot(dss, qn, TN)
            return carry

        lax.fori_loop(0, s // GROUP, group, 0, unroll=8)

        @pl.when(head == 0)
        def _():
            dgq_ref[...] = jnp.zeros_like(dgq_ref)
            dgk_ref[...] = jnp.zeros_like(dgk_ref)

        @pl.when(head > 0)
        def _():
            for cp in writes():
                cp.wait()

        def norm_bwd(x, g, dn):
            rstd = lax.rsqrt(jnp.mean(x * x, axis=-1, keepdims=True) + EPS)
            xh = x * rstd
            dxh = dn * g
            return rstd * (dxh - xh * jnp.mean(dxh * xh, axis=-1, keepdims=True)), jnp.sum(dn * xh, axis=0, keepdims=True)

        def finish(b, carry):
            r0 = pl.multiple_of(b * NORM_ROWS, NORM_ROWS)
            rows = pl.ds(r0, NORM_ROWS)
            dq, dgq = norm_bwd(q_ref[rows, :], gqv, dqn_ref[rows, :])
            dk, dgk = norm_bwd(k_ref[rows, :], gkv, dkp_ref[pl.ds(PAD_ROWS + r0, NORM_ROWS), :])
            stage_ref[0, rows, :] = dq.astype(stage_ref.dtype)
            stage_ref[1, rows, :] = dk.astype(stage_ref.dtype)
            stage_ref[2, rows, :] = dvp_ref[pl.ds(PAD_ROWS + r0, NORM_ROWS), :].astype(stage_ref.dtype)
            dgq_ref[...] += dgq
            dgk_ref[...] += dgk
            return carry

        lax.fori_loop(0, s // NORM_ROWS, finish, 0, unroll=2)
        for cp in writes():
            cp.start()

        @pl.when(head == HEADS - 1)
        def _():
            for cp in writes():
                cp.wait()

    def col(off):
        return pl.BlockSpec((s, HEAD_DIM), lambda h: (0, off + h))

    vec = pl.BlockSpec((1, HEAD_DIM), lambda h: (0, 0))
    hbias = pl.BlockSpec((None, GROUP, WIN), lambda h: (h, 0, 0))
    res = _pallas(
        _with_riders(body, 8, 4, 7, rs, (HEADS,)), name="attention_bwd", grid=(HEADS,),
        in_specs=[col(ATT_COL0), col(ATT_COL0 + HEADS), col(ATT_COL0 + 2 * HEADS), vec, vec, hbias, col(HEADS), ANY,
                  *rs.in_specs],
        out_specs=[ANY, vec, vec, hbias, *rs.out_specs],
        out_shape=[_sds(dproj.shape, dproj.dtype), _sds((1, HEAD_DIM), F32), _sds((1, HEAD_DIM), F32),
                   _sds((HEADS, GROUP, WIN), F32), *rs.out_shapes],
        input_output_aliases={7: 0, **rs.aliases},
        scratch_shapes=[pltpu.VMEM((s + PAD_ROWS, HEAD_DIM), MXU_DTYPE), pltpu.VMEM((s + PAD_ROWS, HEAD_DIM), MXU_DTYPE),
                        pltpu.VMEM((s + PAD_ROWS, HEAD_DIM), F32), pltpu.VMEM((s + PAD_ROWS, HEAD_DIM), F32),
                        pltpu.VMEM((s, HEAD_DIM), F32), pltpu.VMEM((3, s, HEAD_DIM), BF16),
                        pltpu.SemaphoreType.DMA((3,)), *rs.scratch],
        compiler_params=_params("arbitrary"),
    )(proj, proj, proj, gq, gk, bias, dmix, dproj, *rs.arrays)
    return res[:4], rs.split(res[4:])


DIAG_SPLIT = (BAND + WIN - CHUNK) // 2


def _diag_bin(m):
    t = jnp.where(m < DIAG_SPLIT, m, m - WIN)
    return jnp.clip(LEFT_CHUNKS * CHUNK - t, -(CHUNK - 1), REL_CLIP) + (CHUNK - 1)


def _skew_rows(a, left):
    row = lax.broadcasted_iota(jnp.int32, (GROUP, WIN), 0)
    for b in range(GROUP.bit_length() - 1):
        step = 1 << b
        a = jnp.where(jnp.bitwise_and(row, step) != 0, pltpu.roll(a, WIN - step if left else step, 1), a)
    return a


def _rel_bias_expand(rel_bias):
    def body(rb_ref, o_ref):
        onehot = (lax.broadcasted_iota(jnp.int32, (REL_SIZE, WIN), 0)
                  == _diag_bin(lax.broadcasted_iota(jnp.int32, (REL_SIZE, WIN), 1))).astype(MXU_DTYPE)
        rest = jnp.broadcast_to(rb_ref[...], (8, REL_SIZE))
        per_diag = jnp.zeros((8, WIN), F32)
        for _ in range(3):
            piece = rest.astype(BF16)
            per_diag = per_diag + _dot(piece, onehot, NN)
            rest = rest - piece.astype(F32)
        table = _skew_rows(jnp.broadcast_to(per_diag[0:1], (GROUP, WIN)), left=False)
        row_chunk = lax.broadcasted_iota(jnp.int32, (GROUP, WIN), 0) // CHUNK
        col_chunk = lax.broadcasted_iota(jnp.int32, (GROUP, WIN), 1) // CHUNK
        in_band = jnp.logical_and(col_chunk >= row_chunk, col_chunk <= row_chunk + LEFT_CHUNKS)
        o_ref[...] = jnp.where(in_band, table, MASKED)

    return _pallas(body, name="rel_bias_expand", grid=(HEADS,),
                   in_specs=[pl.BlockSpec((None, 1, REL_SIZE), lambda h: (h, 0, 0))],
                   out_specs=pl.BlockSpec((None, GROUP, WIN), lambda h: (h, 0, 0)),
                   out_shape=_sds((HEADS, GROUP, WIN), F32), compiler_params=_params("parallel"))(rel_bias)


def _rel_bias_fold(dbias):
    def body(a_ref, o_ref):
        diag = jnp.sum(_skew_rows(a_ref[...], left=True), axis=0, keepdims=True)
        onehot = (_diag_bin(lax.broadcasted_iota(jnp.int32, (WIN, REL_SIZE), 0))
                  == lax.broadcasted_iota(jnp.int32, (WIN, REL_SIZE), 1)).astype(MXU_DTYPE)
        rest = jnp.broadcast_to(diag, (8, WIN))
        out = jnp.zeros((8, REL_SIZE), F32)
        for _ in range(3):
            piece = rest.astype(BF16)
            out = out + _dot(piece, onehot, NN)
            rest = rest - piece.astype(F32)
        o_ref[...] = out[0:1]

    return _pallas(body, name="rel_bias_fold", grid=(HEADS,),
                   in_specs=[pl.BlockSpec((None, GROUP, WIN), lambda h: (h, 0, 0))],
                   out_specs=pl.BlockSpec((None, 1, REL_SIZE), lambda h: (h, 0, 0)),
                   out_shape=_sds((HEADS, 1, REL_SIZE), F32), compiler_params=_params("parallel"))(dbias)


def _place():
    return lax.axis_index("x"), lax.axis_index("y"), lax.axis_index("c")


def _other_chips(x, y):
    return [(1 - x, y), (x, 1 - y), (1 - x, 1 - y)]


class _Rider:
    reads, ins, new, n_sems = (), (), (), 1

    def schedule(self):
        return [(0.0, self.start), (RIDER_MID, self.mid), (RIDER_LATE, self.late)]

    def start(self, reads, ins, new, send, recv):
        pass

    def mid(self, reads, ins, new, send, recv):
        pass

    def late(self, reads, ins, new, send, recv):
        pass

    def end(self, reads, ins, new, send, recv):
        pass


class _Riders:
    def __init__(self, riders, n_host_in, n_host_out):
        self.riders = list(riders)
        self.arrays, self.out_shapes, self.aliases, self.scratch = [], [], {}, []
        for r in self.riders:
            for t, a in enumerate(r.ins):
                self.aliases[n_host_in + len(self.arrays) + len(r.reads) + t] = n_host_out + len(self.out_shapes) + t
            self.arrays += [*r.reads, *r.ins]
            self.out_shapes += [_sds(a.shape, a.dtype) for a in r.ins] + list(r.new)
            self.scratch += [pltpu.SemaphoreType.DMA((r.n_sems,)), pltpu.SemaphoreType.DMA((r.n_sems,))]
        self.in_specs = [ANY] * len(self.arrays)
        self.out_specs = [ANY] * len(self.out_shapes)

    def bind(self, in_refs, out_refs, scratch_refs):
        bound, i, o = [], 0, 0
        for t, r in enumerate(self.riders):
            reads = in_refs[i:i + len(r.reads)]
            i += len(r.reads) + len(r.ins)
            ins = out_refs[o:o + len(r.ins)]
            new = out_refs[o + len(r.ins):o + len(r.ins) + len(r.new)]
            o += len(r.ins) + len(r.new)
            bound.append((reads, ins, new, scratch_refs[2 * t], scratch_refs[2 * t + 1]))
        return bound

    def run(self, phase, bound):
        for r, b in zip(self.riders, bound):
            getattr(r, phase)(*b)

    def hooks(self, step, steps, bound):
        at = {}
        for r, b in zip(self.riders, bound):
            for frac, action in r.schedule():
                at.setdefault(min(int(steps * frac), steps - 1), []).append((action, b))
        for at_step, todo in sorted(at.items()):
            def go(todo=todo):
                for action, b in todo:
                    action(*b)

            pl.when(step == at_step)(go)

    def split(self, outs):
        res, o = [], 0
        for r in self.riders:
            n = len(r.ins) + len(r.new)
            res.append(list(outs[o:o + n]))
            o += n
        return res


def _run_riders(name, riders):
    rs = _Riders(riders, 0, 0)
    n_in, n_out = len(rs.arrays), len(rs.out_shapes)

    def body(*refs):
        bound = rs.bind(refs[:n_in], refs[n_in:n_in + n_out], refs[n_in + n_out:])
        rs.run("start", bound)
        rs.run("mid", bound)
        rs.run("late", bound)
        rs.run("end", bound)

    outs = _pallas(body, name=name, in_specs=rs.in_specs, out_specs=rs.out_specs, out_shape=rs.out_shapes,
                   input_output_aliases=rs.aliases, scratch_shapes=rs.scratch)(*rs.arrays)
    return rs.split(outs)


class _GatherRider(_Rider):
    X_LINK, Y_LINK, Y_PASS, X_PASS, D2D_X, D2D_Y, D2D_DIAG, N_SEMS = 0, 1, 2, 3, 4, 5, 6, 7

    def __init__(self, blocks, part=(0, 1, 1)):
        self.ins = tuple(blocks)
        self.part = part
        self.n_sems = self.N_SEMS * len(blocks)

    def _copy(self, out, send, recv, w, sem, chip_from, cc, to, sub=None):
        hr = self.ins[w].shape[1] // 2
        lo, hi, n = self.part
        first, size = cc * hr + lo * (hr // n), (hi - lo) * (hr // n)
        if sub is not None:
            size //= 2
            first += sub * size
        piece = out[w].at[2 * chip_from[0] + chip_from[1], pl.ds(first, size), :]
        return pltpu.make_async_remote_copy(src_ref=piece, dst_ref=piece, send_sem=send.at[self.N_SEMS * w + sem],
                                            recv_sem=recv.at[self.N_SEMS * w + sem], device_id=to, device_id_type=MESH)

    def _sent(self, out, send, recv, w):
        x, y, c = _place()
        me, sib = (x, y), (x, y, 1 - c)
        xn, yn, diag = _other_chips(x, y)
        cp = functools.partial(self._copy, out, send, recv, w)
        return [("start", cp(self.X_LINK, me, c, (*xn, c))), ("start", cp(self.Y_LINK, me, c, (*yn, c))),
                ("mid_x", cp(self.D2D_X, xn, c, sib)), ("mid_x", cp(self.Y_PASS, xn, c, (*yn, c), sub=0)),
                ("mid_y", cp(self.D2D_Y, yn, c, sib)), ("mid_y", cp(self.X_PASS, yn, c, (*xn, c), sub=1)),
                ("late", cp(self.D2D_DIAG, diag, c, sib))]

    def _go(self, out, send, recv, phase):
        for w in range(len(self.ins)):
            for ph, copy in self._sent(out, send, recv, w):
                if ph == phase:
                    copy.start()

    def start(self, reads, out, new, send, recv):
        self._go(out, send, recv, "start")

    def mid(self, reads, out, new, send, recv):
        x, y, c = _place()
        xn, yn, _ = _other_chips(x, y)
        for w in range(len(self.ins)):
            self._copy(out, send, recv, w, self.X_LINK, xn, c, (x, y, c)).wait_recv()
        self._go(out, send, recv, "mid_x")
        for w in range(len(self.ins)):
            self._copy(out, send, recv, w, self.Y_LINK, yn, c, (x, y, c)).wait_recv()
        self._go(out, send, recv, "mid_y")

    def late(self, reads, out, new, send, recv):
        x, y, c = _place()
        diag = _other_chips(x, y)[2]
        for w in range(len(self.ins)):
            self._copy(out, send, recv, w, self.Y_PASS, diag, c, (x, y, c), sub=0).wait_recv()
            self._copy(out, send, recv, w, self.X_PASS, diag, c, (x, y, c), sub=1).wait_recv()
        self._go(out, send, recv, "late")

    def end(self, reads, out, new, send, recv):
        x, y, c = _place()
        xn, yn, diag = _other_chips(x, y)
        for w in range(len(self.ins)):
            for sem, chip in ((self.D2D_X, xn), (self.D2D_Y, yn), (self.D2D_DIAG, diag)):
                self._copy(out, send, recv, w, sem, chip, 1 - c, (x, y, c)).wait_recv()
        for w in range(len(self.ins)):
            for _, copy in self._sent(out, send, recv, w):
                copy.wait_send()


class _SwapRider(_Rider):
    def __init__(self, grads):
        self.reads = tuple(grads)
        self.new = tuple(_sds((N_CHIPS, g.shape[1] // 2, g.shape[2]), g.dtype) for g in grads)
        self.n_sems = len(grads)

    def _copies(self, src, new, send, recv):
        x, y, c = _place()
        copies = []
        for w in range(len(self.reads)):
            hr = self.reads[w].shape[1] // 2
            copies.append(pltpu.make_async_remote_copy(
                src_ref=src[w].at[:, pl.ds((1 - c) * hr, hr), :], dst_ref=new[w],
                send_sem=send.at[w], recv_sem=recv.at[w], device_id=(x, y, 1 - c), device_id_type=MESH))
        return copies

    def start(self, src, ins, new, send, recv):
        for cp in self._copies(src, new, send, recv):
            cp.start()

    def end(self, src, ins, new, send, recv):
        for cp in self._copies(src, new, send, recv):
            cp.wait()


def _add_half(g, got, c_arr, name):
    nk, r, cols = g.shape
    hr = r // 2
    tr = min(hr, 1024)
    nb = hr // tr

    def body(c_ref, g_ref, got_ref, o_ref):
        o_ref[...] = (g_ref[...].astype(F32) + got_ref[...].astype(F32)).astype(o_ref.dtype)

    grid_spec = pltpu.PrefetchScalarGridSpec(
        num_scalar_prefetch=1, grid=(nk, nb),
        in_specs=[pl.BlockSpec((None, tr, cols), lambda k, i, c_ref: (k, c_ref[0] * nb + i, 0)),
                  pl.BlockSpec((None, tr, cols), lambda k, i, c_ref: (k, i, 0))],
        out_specs=pl.BlockSpec((None, tr, cols), lambda k, i, c_ref: (k, i, 0)))
    return _pallas(body, name=name, grid_spec=grid_spec, out_shape=_sds((nk, hr, cols), g.dtype),
                   compiler_params=_params("parallel", "parallel"))(c_arr, g, got)


class _SendPartialsRider(_Rider):
    def __init__(self, parts, got=None, part=(0, 1, 1)):
        self.reads = tuple(parts)
        if got is None:
            self.new = tuple(_sds((N_CHIPS - 1, *p.shape[1:]), p.dtype) for p in parts)
        else:
            self.ins = tuple(got)
        self.part = part
        self.n_sems = 3 * len(parts)

    def _copies(self, src, ins, new, send, recv):
        x, y, c = _place()
        land = ins if self.ins else new
        lo, hi, n = self.part
        copies = []
        for w in range(len(self.reads)):
            pr = self.reads[w].shape[1] // n
            rows = pl.ds(lo * pr, (hi - lo) * pr)
            for j, chip in enumerate(_other_chips(x, y)):
                copies.append(pltpu.make_async_remote_copy(
                    src_ref=src[w].at[2 * chip[0] + chip[1], rows, :], dst_ref=land[w].at[j, rows, :],
                    send_sem=send.at[3 * w + j], recv_sem=recv.at[3 * w + j], device_id=(*chip, c), device_id_type=MESH))
        return copies

    def start(self, src, ins, new, send, recv):
        for cp in self._copies(src, ins, new, send, recv):
            cp.start()

    def end(self, src, ins, new, send, recv):
        for cp in self._copies(src, ins, new, send, recv):
            cp.wait()


def _sum_partials(part, got, kc_arr, name):
    _, hr, cols = part.shape
    tr = min(hr, 512)
    nb = hr // tr

    def body(kc_ref, p_ref, g0_ref, g1_ref, g2_ref, o_ref):
        o_ref[...] = ((p_ref[...].astype(F32) + g0_ref[...].astype(F32)) + g1_ref[...].astype(F32)) + g2_ref[...].astype(F32)

    slot = lambda j: pl.BlockSpec((None, tr, cols), lambda i, kc_ref: (j, i, 0))
    grid_spec = pltpu.PrefetchScalarGridSpec(
        num_scalar_prefetch=1, grid=(nb,),
        in_specs=[pl.BlockSpec((None, tr, cols), lambda i, kc_ref: (kc_ref[0], i, 0)), slot(0), slot(1), slot(2)],
        out_specs=pl.BlockSpec((tr, cols), lambda i, kc_ref: (kc_ref[1] * nb + i, 0)))
    return _pallas(body, name=name, grid_spec=grid_spec, out_shape=_sds((2 * hr, cols), F32),
                   compiler_params=_params("parallel"))(kc_arr, part, got, got, got)


class _ShareRider(_Rider):
    def __init__(self, grads):
        self.ins = tuple(grads)
        self.n_sems = len(grads)

    def _copies(self, out, send, recv):
        x, y, c = _place()
        copies = []
        for w in range(len(self.ins)):
            hr = self.ins[w].shape[0] // 2
            mine = out[w].at[pl.ds(c * hr, hr), :]
            copies.append(pltpu.make_async_remote_copy(
                src_ref=mine, dst_ref=mine, send_sem=send.at[w], recv_sem=recv.at[w],
                device_id=(x, y, 1 - c), device_id_type=MESH))
        return copies

    def start(self, reads, out, new, send, recv):
        for cp in self._copies(out, send, recv):
            cp.start()

    def end(self, reads, out, new, send, recv):
        for cp in self._copies(out, send, recv):
            cp.wait()


def _small_allreduce_adamw(g_part, w, m, v, riders=()):
    rows = g_part.shape[0]
    rs = _Riders(riders, 4, 4)
    n_rin, n_rout = len(rs.arrays), len(rs.out_shapes)

    def body(*refs):
        g_ref, w_ref, m_ref, v_ref = refs[:4]
        go_ref, d_ref, nm_ref, nv_ref = refs[4 + n_rin:8 + n_rin]
        all_ref, send_sems, recv_sems = refs[8 + n_rin + n_rout:11 + n_rin + n_rout]
        bound = rs.bind(refs[4:4 + n_rin], refs[8 + n_rin:8 + n_rin + n_rout], refs[11 + n_rin + n_rout:])
        rs.run("start", bound)
        x, y, c = _place()
        me = 4 * x + 2 * y + c
        all_ref[me] = g_ref[...]
        copies = []
        for r in range(1, 8):
            dx, dy, dc = (r >> 2) & 1, (r >> 1) & 1, r & 1
            peer = (1 - x if dx else x, 1 - y if dy else y, 1 - c if dc else c)
            copies.append(pltpu.make_async_remote_copy(
                src_ref=g_ref, dst_ref=all_ref.at[me], send_sem=send_sems.at[r - 1], recv_sem=recv_sems.at[r - 1],
                device_id=peer, device_id_type=MESH))
        for cp in copies:
            cp.start()
        for cp in copies:
            cp.wait()
        tot = all_ref[0]
        for d in range(1, 8):
            tot = tot + all_ref[d]
        go_ref[...] = tot
        d_ref[...], nm_ref[...], nv_ref[...] = _adamw_math(w_ref[...], tot, m_ref[...], v_ref[...])
        for phase in ("mid", "late", "end"):
            rs.run(phase, bound)

    vm = pl.BlockSpec(memory_space=pltpu.VMEM)
    out = _pallas(
        body, name="small_allreduce_adamw", in_specs=[vm] * 4 + rs.in_specs, out_specs=[vm] * 4 + rs.out_specs,
        out_shape=[_sds((rows, 128), F32)] * 4 + rs.out_shapes, input_output_aliases=rs.aliases,
        scratch_shapes=[pltpu.VMEM((8, rows, 128), F32), pltpu.SemaphoreType.DMA((7,)), pltpu.SemaphoreType.DMA((7,)),
                        *rs.scratch],
    )(g_part, w, m, v, *rs.arrays)
    return out[:4], rs.split(out[4:])


SMALL_SIZES = (2048, 1024, 128, 128, HEADS * REL_SIZE, 2048)
SMALL_PART_ROWS = tuple(-(-size // 1024) * 8 for size in SMALL_SIZES)
SMALL_ROWS = sum(SMALL_PART_ROWS)


def _pack_small(parts):
    rows = []
    for p, size, nr in zip(parts, SMALL_SIZES, SMALL_PART_ROWS):
        rows.append(jnp.pad(p.reshape(-1), (0, nr * 128 - size)).reshape(nr, 128))
    return jnp.concatenate(rows, axis=0)


def _unpack_small(slab, shapes):
    out, off = [], 0
    for size, nr, shape in zip(SMALL_SIZES, SMALL_PART_ROWS, shapes):
        out.append(slab[off:off + nr].reshape(-1)[:size].reshape(shape))
        off += nr
    return out


def kernel(x, norm1_g, w_in, ret_norm_g, q_norm_g, k_norm_g, rel_bias, w_out, norm2_g, w_ff1, w_ff2, loss_target, m_norm1_g, m_w_in, m_ret_norm_g, m_q_norm_g, m_k_norm_g, m_rel_bias, m_w_out, m_norm2_g, m_w_ff1, m_w_ff2, v_norm1_g, v_w_in, v_ret_norm_g, v_q_norm_g, v_k_norm_g, v_rel_bias, v_w_out, v_norm2_g, v_w_ff1, v_w_ff2):
    xs = x[0]
    tgt = loss_target[0]
    s, d = xs.shape
    d_in = N_CHIPS * w_in.shape[2]
    d_ff = N_CHIPS * w_ff1.shape[2]
    in_sh, ff_sh = w_in.shape[2], w_ff1.shape[2]
    tm = min(s, 1024)
    gi = s // tm
    c_arr = lax.axis_index("c").astype(jnp.int32).reshape(1)
    k_arr = (2 * lax.axis_index("x") + lax.axis_index("y")).astype(jnp.int32).reshape(1)
    tables = _tables(s)
    bias = _rel_bias_expand(rel_bias[0][:, None, :])

    blk_in = [_cast_bf16(w_in[0], k_arr, "cast_w_in_%d" % half, cols=(half, 2)) for half in range(2)]
    blk_out, blk_ff1, blk_ff2 = (_cast_bf16(w_out[0], k_arr, "cast_w_out"), _cast_bf16(w_ff1[0], k_arr, "cast_w_ff1"),
                                 _cast_bf16(w_ff2[0], k_arr, "cast_w_ff2"))

    h1, ((wg_in0,),) = _rmsnorm_fwd(xs, norm1_g, "rmsnorm1", riders=[_GatherRider([blk_in[0]])])
    tn_in = in_sh // 2
    tk = d

    def proj_half(half, wg, through, riders):
        return _mm("proj_%d" % half, h1, wg, NN, (gi, N_CHIPS, 1),
                   pl.BlockSpec((tm, tk), lambda i, j, k: (i, 0)), pl.BlockSpec((None, tk, tn_in), lambda i, j, k: (j, 0, 0)),
                   [_sds((s, d_in), F32)], [pl.BlockSpec((tm, tn_in), lambda i, j, k: (i, 2 * j + half))], (tm, tn_in),
                   riders=riders, through=through)

    (proj,), ((wg_in1,),) = proj_half(0, wg_in0, None, [_GatherRider([blk_in[1]])])
    (proj,), ((wg_ff1,),) = proj_half(1, wg_in1, proj, [_GatherRider([blk_ff1], (0, 3, 8))])
    (mix, y_ret, prev), ((wg_ff1,),) = _retention_fwd(proj, ret_norm_g, tables, riders=[_GatherRider([wg_ff1], (3, 6, 8))])
    mix, ((wg_out,), (wg_ff2,)) = _attention_fwd(
        proj, q_norm_g, k_norm_g, bias, mix, riders=[_GatherRider([blk_out]), _GatherRider([blk_ff2], (0, 1, 4))])
    wg_out = wg_out.reshape(d, d)
    tn = 1024
    tile = pl.BlockSpec((tm, tn), lambda i, j, k: (i, j))
    def residual_norm(acc, res, g):
        x1v = res + acc
        yv = x1v * lax.rsqrt(jnp.mean(x1v * x1v, axis=-1, keepdims=True) + EPS)
        return x1v, yv * g

    tmo = min(s, 512)
    rows = pl.BlockSpec((tmo, d), lambda i, j, k: (i, 0))
    (x1, h2), ((wg_ff1,),) = _mm(
        "out_proj", mix, wg_out, NN, (s // tmo, 1, 1),
        rows, pl.BlockSpec((d, d), lambda i, j, k: (0, 0)),
        [_sds((s, d), F32), _sds((s, d), BF16)], [rows, rows], (tmo, d),
        extras=(xs, norm2_g), extra_specs=(rows, pl.BlockSpec((1, d), lambda i, j, k: (0, 0))),
        epi=residual_norm, riders=[_GatherRider([wg_ff1], (6, 8, 8))])
    tn_ff = min(ff_sh, 1024)
    per = ff_sh // tn_ff

    def relu2(acc):
        r = jnp.maximum(acc, 0.0)
        return acc, r * r

    (u, act), ((wg_ff2,),) = _mm(
        "ff1", h2, wg_ff1, NN, (gi, N_CHIPS * per, d // tk),
        pl.BlockSpec((tm, tk), lambda i, j, k: (i, k)),
        pl.BlockSpec((None, tk, tn_ff), lambda i, j, k: (j // per, k, j % per)),
        [_sds((s, d_ff), F32), _sds((s, d_ff), BF16)],
        [pl.BlockSpec((tm, tn_ff), lambda i, j, k: (i, j))] * 2, (tm, tn_ff), epi=relu2,
        riders=[_GatherRider([wg_ff2], (1, 4, 4))])
    wg_ff2 = wg_ff2.reshape(d_ff, d)

    def loss_epi(acc, res, t):
        diff = (res + acc) - t
        dy = diff / d
        return dy, dy, jnp.sum(diff * diff, axis=0, keepdims=True)

    tk2 = min(tk, 2048)
    dy, dyb, loss_cols = _mm(
        "ff2_loss", act, wg_ff2, NN, (gi, d // tn, d_ff // tk2),
        pl.BlockSpec((tm, tk2), lambda i, j, k: (i, k)), pl.BlockSpec((tk2, tn), lambda i, j, k: (k, j)),
        [_sds((s, d), F32), _sds((s, d), BF16), _sds((gi, 1, d), F32)],
        [tile, tile, pl.BlockSpec((None, 1, tn), lambda i, j, k: (i, 0, j))], (tm, tn),
        extras=(x1, tgt), extra_specs=(tile, tile), epi=loss_epi)
    loss = lax.psum(0.5 * jnp.sum(loss_cols) / d, ("x", "y", "c"))

    (du,) = _mm("d_act", dyb, wg_ff2, NT, (gi, d_ff // tn, d // tk),
                pl.BlockSpec((tm, tk), lambda i, j, k: (i, k)), pl.BlockSpec((tn, tk), lambda i, j, k: (j, k)),
                [_sds((s, d_ff), BF16)], [tile], (tm, tn), extras=(u,), extra_specs=(tile,),
                epi=lambda acc, uu: (acc * (2.0 * jnp.maximum(uu, 0.0)),))
    ts = min(s, 2048)
    wtile = pl.BlockSpec((tn, tn), lambda i, j, k: (i, j))
    (g_ff2,) = _mm("dw_ff2", act, dyb, TN, (d_ff // tn, d // tn, s // ts),
                   pl.BlockSpec((ts, tn), lambda i, j, k: (k, i)), pl.BlockSpec((ts, tn), lambda i, j, k: (k, j)),
                   [_sds((d_ff, d), BF16)], [wtile], (tn, tn))
    g_ff2 = g_ff2.reshape(N_CHIPS, d_ff // N_CHIPS, d)
    (g_ff1,), ((got_ff2,),) = _mm(
        "dw_ff1", h2, du, TN, (d // tn, N_CHIPS * per, s // ts),
        pl.BlockSpec((ts, tn), lambda i, j, k: (k, i)), pl.BlockSpec((ts, tn_ff), lambda i, j, k: (k, j)),
        [_sds((N_CHIPS, d, ff_sh), BF16)],
        [pl.BlockSpec((None, tn, tn_ff), lambda i, j, k: (j // per, i, j % per))], (tn, tn_ff),
        riders=[_SwapRider([g_ff2])])
    p_ff2 = _add_half(g_ff2, got_ff2, c_arr, "chip_partial_w_ff2")
    tkf = min(tk, ff_sh)
    kper = ff_sh // tkf
    (dh2,), ((got2_ff2,), (got_ff1,)) = _mm(
        "d_h2", du, wg_ff1, NT, (gi, d // tn, d_ff // tkf),
        pl.BlockSpec((tm, tkf), lambda i, j, k: (i, k)),
        pl.BlockSpec((None, tn, tkf), lambda i, j, k: (k // kper, j, k % kper)),
        [_sds((s, d), F32)], [tile], (tm, tn),
        riders=[_SendPartialsRider([p_ff2], part=(0, 3, 4)), _SwapRider([g_ff1])])
    p_ff1 = _add_half(g_ff1, got_ff1, c_arr, "chip_partial_w_ff1")
    dx1, dx1b, g_norm2 = _rmsnorm_bwd(x1, norm2_g, dh2, dy, "rmsnorm2_bwd")

    (dmix,) = _mm("d_mix", dx1b, wg_out, NT, (gi, d // tn, d // tk),
                  pl.BlockSpec((tm, tk), lambda i, j, k: (i, k)), pl.BlockSpec((tn, tk), lambda i, j, k: (j, k)),
                  [_sds((s, d), F32)], [tile], (tm, tn))
    (g_out,) = _mm("dw_out", mix, dx1b, TN, (d // tn, d // tn, s // ts),
                   pl.BlockSpec((ts, tn), lambda i, j, k: (k, i)), pl.BlockSpec((ts, tn), lambda i, j, k: (k, j)),
                   [_sds((d, d), BF16)], [wtile], (tn, tn))
    g_out = g_out.reshape(N_CHIPS, d // N_CHIPS, d)
    (dproj, g_gn), ((got2_ff2,), (got2_ff1,), (got_out,)) = _retention_bwd(
        proj, ret_norm_g, tables, y_ret, prev, dmix,
        riders=[_SendPartialsRider([p_ff2], got=[got2_ff2], part=(3, 4, 4)), _SendPartialsRider([p_ff1], part=(0, 2, 4)),
                _SwapRider([g_out])])
    p_out = _add_half(g_out, got_out, c_arr, "chip_partial_w_out")
    (dproj, g_gq, g_gk, dbias), ((got2_ff1,), (got2_out,)) = _attention_bwd(
        proj, q_norm_g, k_norm_g, bias, dmix, dproj,
        riders=[_SendPartialsRider([p_ff1], got=[got2_ff1], part=(2, 4, 4)), _SendPartialsRider([p_out])])
    g_rel = _rel_bias_fold(dbias)
    names = ["w_in", "w_out", "w_ff1", "w_ff2"]
    kc_arr = jnp.concatenate([k_arr, c_arr])
    early = [_sum_partials(p, r, kc_arr, "sum_partials_" + nm)
             for p, r, nm in zip((p_out, p_ff1, p_ff2), (got2_out, got2_ff1, got2_ff2), names[1:])]
    (g_in,), (early,) = _mm(
        "dw_in", h1, dproj, TN, (d // tn, 2 * N_CHIPS, s // ts),
        pl.BlockSpec((ts, tn), lambda i, j, k: (k, i)), pl.BlockSpec((ts, tn_in), lambda i, j, k: (k, j)),
        [_sds((N_CHIPS, d, in_sh), BF16)],
        [pl.BlockSpec((None, tn, tn_in), lambda i, j, k: (j // 2, i, j % 2))], (tn, tn_in), riders=[_ShareRider(early)])
    ((got_in,),) = _run_riders("grad_swap_w_in", [_SwapRider([g_in])])
    p_in = _add_half(g_in, got_in, c_arr, "chip_partial_w_in")
    half_spec = pl.BlockSpec((None, tn, tn_in), lambda i, j, k: (k // 2, j, 0))
    (dh1,), ((got2_in,),) = _mm(
        "d_h1", dproj, [wg_in0, wg_in1], NT, (gi, d // tn, 2 * N_CHIPS),
        pl.BlockSpec((tm, tn_in), lambda i, j, k: (i, k)), [half_spec, half_spec],
        [_sds((s, d), F32)], [tile], (tm, tn), riders=[_SendPartialsRider([p_in])])
    grad_x, _, g_norm1 = _rmsnorm_bwd(xs, norm1_g, dh1, dx1, "rmsnorm1_bwd")

    small_w = (norm1_g, ret_norm_g, q_norm_g, k_norm_g, rel_bias, norm2_g)
    small_m = (m_norm1_g, m_ret_norm_g, m_q_norm_g, m_k_norm_g, m_rel_bias, m_norm2_g)
    small_v = (v_norm1_g, v_ret_norm_g, v_q_norm_g, v_k_norm_g, v_rel_bias, v_norm2_g)
    shapes = [p.shape for p in small_w]
    g_small = _pack_small([g_norm1, g_gn, g_gq, g_gk, g_rel, g_norm2])
    small_out, ((g_w_in,),) = _small_allreduce_adamw(
        g_small, _pack_small(small_w), _pack_small(small_m), _pack_small(small_v),
        riders=[_ShareRider([_sum_partials(p_in, got2_in, kc_arr, "sum_partials_w_in")])])
    sg, sd, sm, sv = (_unpack_small(a, shapes) for a in small_out)

    g_big = [g_w_in, *early]
    big = []
    for g, w, m, v, nm in zip(g_big, (w_in, w_out, w_ff1, w_ff2), (m_w_in, m_w_out, m_w_ff1, m_w_ff2),
                              (v_w_in, v_w_out, v_w_ff1, v_w_ff2), names):
        g, delta, new_m, new_v = _adamw(w[0], g, m[0], v[0], "adamw_" + nm)
        big.append((g[None], delta[None], new_m[None], new_v[None]))

    def ordered(kind):
        sm_ = (sg, sd, sm, sv)[kind]
        return (sm_[0], big[0][kind], sm_[1], sm_[2], sm_[3], sm_[4], big[1][kind], sm_[5], big[2][kind], big[3][kind])

    return (loss, grad_x[None], *ordered(0), *ordered(1), *ordered(2), *ordered(3))
```

```python
import functools

import jax
import jax.numpy as jnp
from jax import lax
from jax.experimental import pallas as pl
from jax.experimental.pallas import tpu as pltpu

F32 = jnp.float32
BF16 = jnp.bfloat16
MXU_DTYPE = jnp.bfloat16

CHUNK = 64
HEADS = 8
HEAD_DIM = 128
LEFT_CHUNKS = 8
BAND = (LEFT_CHUNKS + 1) * CHUNK
REL_CLIP = 128
REL_SIZE = (CHUNK - 1) + REL_CLIP + 1
RET_BLOCK_CHUNKS = 16
RET_ROWS = RET_BLOCK_CHUNKS * CHUNK
RET_SUB = 256
ROPE_BASE = 10000.0
EPS = 1e-6
GN_EPS = 1e-5
ADAM_LR, ADAM_B1, ADAM_B2, ADAM_EPS, ADAM_WD, ADAM_STEP = 0.001, 0.9, 0.999, 1e-08, 0.01, 10
N_CHIPS = 4
VMEM_LIMIT = 56 * 1024 * 1024
MESH = pl.DeviceIdType.MESH
ANY = pl.BlockSpec(memory_space=pl.ANY)

NN = (((1,), (0,)), ((), ()))
NT = (((1,), (1,)), ((), ()))
TN = (((0,), (0,)), ((), ()))


def _pallas(body, **kw):
    return pl.pallas_call(body, **kw)


def _params(*sem):
    return pltpu.CompilerParams(dimension_semantics=sem, vmem_limit_bytes=VMEM_LIMIT)


def _dot(a, b, dims):
    return lax.dot_general(a.astype(MXU_DTYPE), b.astype(MXU_DTYPE), dims, preferred_element_type=F32)


RIDER_MID, RIDER_LATE = 0.5, 0.8


def _mm(name, a, b, dims, grid, a_spec, b_spec, outs, o_specs, acc_shape, extras=(), extra_specs=(), epi=None,
        riders=(), through=None):
    ni, nj, nk = grid
    n_ex, n_out = len(extras), len(outs)
    bs = list(b) if isinstance(b, (list, tuple)) else [b]
    b_specs = list(b_spec) if isinstance(b, (list, tuple)) else [b_spec]
    extras, extra_specs = (*bs[1:], *extras), (*b_specs[1:], *extra_specs)
    b, b_spec, n_b = bs[0], b_specs[0], len(bs)
    n_in = 1 + n_b + n_ex
    rs = _Riders(riders, n_in, n_out)
    n_rin, n_rout = len(rs.arrays), len(rs.out_shapes)
    steps = ni * nj * nk

    held = [] if through is None else [through]

    def body(*refs):
        a_ref, b_refs = refs[0], refs[1:1 + n_b]
        b_ref = b_refs[0]
        ex_refs = refs[1 + n_b:n_in]
        outs_at = n_in + n_rin + len(held)
        o_refs = refs[outs_at:outs_at + n_out]
        acc_ref = refs[outs_at + n_out + n_rout]
        k = pl.program_id(2)
        if riders:
            bound = rs.bind(refs[n_in:n_in + n_rin], refs[outs_at + n_out:outs_at + n_out + n_rout],
                            refs[outs_at + n_out + n_rout + 1:])
            step = (pl.program_id(0) * nj + pl.program_id(1)) * nk + k
            rs.hooks(step, steps, bound)

        def finish(acc):
            vals = epi(acc, *[r[...] for r in ex_refs]) if epi is not None else (acc,)
            for r, v in zip(o_refs, vals):
                r[...] = v.astype(r.dtype)

        if nk == 1:
            finish(_dot(a_ref[...], b_ref[...], dims))
        else:
            @pl.when(k == 0)
            def _():
                acc_ref[...] = jnp.zeros_like(acc_ref)

            for t, ref in enumerate(b_refs):
                def step_with(ref=ref):
                    acc_ref[...] += _dot(a_ref[...], ref[...], dims)

                if n_b == 1:
                    step_with()
                else:
                    pl.when(k % n_b == t)(step_with)
            pl.when(k == nk - 1)(lambda: finish(acc_ref[...]))

        if riders:
            pl.when(step == steps - 1)(lambda: rs.run("end", bound))

    res = _pallas(
        body, name=name, grid=grid, in_specs=[a_spec, b_spec, *extra_specs, *rs.in_specs, *[ANY for _ in held]],
        out_specs=[*o_specs, *rs.out_specs], out_shape=[*outs, *rs.out_shapes],
        input_output_aliases={**rs.aliases, **{n_in + n_rin: 0 for _ in held}},
        scratch_shapes=[pltpu.VMEM(acc_shape if nk > 1 else (8, 128), F32), *rs.scratch],
        compiler_params=_params(*(("arbitrary",) * 3 if riders else ("parallel", "parallel", "arbitrary"))),
    )(a, b, *extras, *rs.arrays, *held)
    return (res[:n_out], rs.split(res[n_out:])) if riders else res


def _sds(shape, dtype):
    return jax.ShapeDtypeStruct(shape, dtype)


def _cast_bf16(w, k_arr, name, cols=(0, 1)):
    r, c = w.shape[0], w.shape[1] // cols[1]
    tr = min(r, 512)

    def body(k_ref, w_ref, o_ref):
        o_ref[...] = w_ref[...].astype(BF16)

    grid_spec = pltpu.PrefetchScalarGridSpec(
        num_scalar_prefetch=1, grid=(r // tr,), in_specs=[pl.BlockSpec((tr, c), lambda i, k_ref: (i, cols[0]))],
        out_specs=pl.BlockSpec((None, tr, c), lambda i, k_ref: (k_ref[0], i, 0)))
    return _pallas(body, name=name, grid_spec=grid_spec, out_shape=_sds((N_CHIPS, r, c), BF16),
                   compiler_params=_params("parallel"))(k_arr, w)


def _rmsnorm_fwd(x, g, name, riders=()):
    s, d = x.shape
    tr = 512

    def body(x_ref, g_ref, o_ref):
        xv = x_ref[...]
        y = xv * lax.rsqrt(jnp.mean(xv * xv, axis=-1, keepdims=True) + EPS)
        o_ref[...] = (y * g_ref[...]).astype(o_ref.dtype)

    rs = _Riders(riders, 2, 1)
    out = _pallas(_with_riders(body, 2, 1, 0, rs, (s // tr,)), name=name, grid=(s // tr,),
                  in_specs=[pl.BlockSpec((tr, d), lambda i: (i, 0)), pl.BlockSpec((1, d), lambda i: (0, 0)), *rs.in_specs],
                  out_specs=[pl.BlockSpec((tr, d), lambda i: (i, 0)), *rs.out_specs],
                  out_shape=[_sds((s, d), BF16), *rs.out_shapes], input_output_aliases=rs.aliases,
                  scratch_shapes=rs.scratch, compiler_params=_params("arbitrary"))(x, g, *rs.arrays)
    return out[0], rs.split(out[1:])


def _rmsnorm_bwd(x, g, dh, res, name, riders=()):
    s, d = x.shape
    tr = 512

    def body(x_ref, g_ref, dh_ref, res_ref, dx_ref, dxb_ref, dg_ref):
        i = pl.program_id(0)
        xv = x_ref[...]
        rstd = lax.rsqrt(jnp.mean(xv * xv, axis=-1, keepdims=True) + EPS)
        xh = xv * rstd
        dhv = dh_ref[...]

        @pl.when(i == 0)
        def _():
            dg_ref[...] = jnp.zeros_like(dg_ref)

        dg_ref[...] += jnp.sum(dhv * xh, axis=0, keepdims=True)
        dxh = dhv * g_ref[...]
        dx = res_ref[...] + rstd * (dxh - xh * jnp.mean(dxh * xh, axis=-1, keepdims=True))
        dx_ref[...] = dx
        dxb_ref[...] = dx.astype(BF16)

    row = pl.BlockSpec((tr, d), lambda i: (i, 0))
    vec = pl.BlockSpec((1, d), lambda i: (0, 0))
    rs = _Riders(riders, 4, 3)
    out = _pallas(_with_riders(body, 4, 3, 0, rs, (s // tr,)), name=name, grid=(s // tr,),
                  in_specs=[row, vec, row, row, *rs.in_specs], out_specs=[row, row, vec, *rs.out_specs],
                  out_shape=[_sds((s, d), F32), _sds((s, d), BF16), _sds((1, d), F32), *rs.out_shapes],
                  input_output_aliases=rs.aliases, scratch_shapes=rs.scratch,
                  compiler_params=_params("arbitrary"))(x, g, dh, res, *rs.arrays)
    return (out[:3], rs.split(out[3:])) if riders else out


def _adamw_math(w, g, m, v):
    m = ADAM_B1 * m + (1.0 - ADAM_B1) * g
    v = ADAM_B2 * v + (1.0 - ADAM_B2) * (g * g)
    m_hat = m / (1.0 - ADAM_B1 ** ADAM_STEP)
    v_hat = v / (1.0 - ADAM_B2 ** ADAM_STEP)
    delta = -ADAM_LR * (m_hat / (jnp.sqrt(v_hat) + ADAM_EPS) + ADAM_WD * w)
    return delta, m, v


def _adamw(w, g, m, v, name):
    r, c = w.shape
    tr = 256

    def body(w_ref, g_ref, m_ref, v_ref, go_ref, d_ref, nm_ref, nv_ref):
        g = g_ref[...]
        go_ref[...] = g
        d_ref[...], nm_ref[...], nv_ref[...] = _adamw_math(w_ref[...], g, m_ref[...], v_ref[...])

    blk = pl.BlockSpec((tr, c), lambda i: (i, 0))
    return _pallas(body, name=name, grid=(r // tr,), in_specs=[blk] * 4, out_specs=[blk] * 4,
                   out_shape=[_sds((r, c), F32)] * 4, compiler_params=_params("parallel"))(w, g, m, v)


def _tables(s):
    half = HEAD_DIM // 2
    pos = jnp.arange(s, dtype=F32)
    inv_freq = ROPE_BASE ** (-jnp.arange(half, dtype=F32) / half)
    ang = pos[:, None] * inv_freq[None, :]
    cos, sin = jnp.cos(ang), jnp.sin(ang)
    cos_f = jnp.concatenate([cos, cos], axis=-1)
    sin_f = jnp.concatenate([-sin, sin], axis=-1)
    log_g = jnp.log1p(-jnp.exp2(-(5.0 + jnp.arange(HEADS, dtype=F32))))
    p = jnp.arange(CHUNK, dtype=F32)
    decay = jnp.exp(log_g[:, None, None] * jnp.abs(p[:, None] - p[None, :]))
    k_dec = jnp.exp(log_g[None, :] * (CHUNK - 1.0 - p)[:, None])
    q_dec = jnp.exp(log_g[None, :] * (p + 1.0)[:, None])
    c_dec = jnp.exp(log_g * CHUNK)
    k_dec = jnp.tile(jnp.broadcast_to(k_dec.T[:, :, None], (HEADS, CHUNK, HEAD_DIM)), (1, RET_BLOCK_CHUNKS, 1))
    q_dec = jnp.tile(jnp.broadcast_to(q_dec.T[:, :, None], (HEADS, CHUNK, HEAD_DIM)), (1, RET_BLOCK_CHUNKS, 1))
    c_dec = jnp.broadcast_to(c_dec[:, None, None], (HEADS, 1, HEAD_DIM))
    n = RET_SUB // CHUNK
    decay = (jnp.eye(n, dtype=F32)[None, :, None, :, None] * decay[:, None, :, None, :]).reshape(HEADS, RET_SUB, RET_SUB)
    return cos_f, sin_f, decay, k_dec, q_dec, c_dec


def _rot(x, cos_f, sin_f):
    return x * cos_f + pltpu.roll(x, HEAD_DIM // 2, 1) * sin_f


def _rot_bwd(d, cos_f, sin_f):
    return d * cos_f + pltpu.roll(d * sin_f, HEAD_DIM // 2, 1)


K_SCALE = HEAD_DIM ** -0.5


def _retention_fwd(proj, gn_g, tables, riders=()):
    s = proj.shape[0]
    nb = s // RET_ROWS
    nc = s // CHUNK
    cos_f, sin_f, decay, k_dec, q_dec, c_dec = tables

    def body(q_ref, k_ref, v_ref, g_ref, cos_ref, sin_ref, dec_ref, kd_ref, qd_ref, cd_ref, gn_ref,
             ret_ref, y_ref, prev_ref, state_ref):
        @pl.when(pl.program_id(1) == 0)
        def _():
            state_ref[...] = jnp.zeros_like(state_ref)

        cosv, sinv = cos_ref[...], sin_ref[...]
        q = _rot(q_ref[...], cosv, sinv)
        k = _rot(k_ref[...], cosv, sinv) * K_SCALE
        v = v_ref[...]
        rg = g_ref[...]
        dec, cd, gn = dec_ref[...], cd_ref[...], gn_ref[...]
        kdf, qdf = k * kd_ref[...], q * qd_ref[...]
        chunks = [slice(c * CHUNK, (c + 1) * CHUNK) for c in range(RET_BLOCK_CHUNKS)]
        contribs = [_dot(kdf[rows], v[rows], TN) for rows in chunks]
        state, states = state_ref[...], []
        for c in range(RET_BLOCK_CHUNKS):
            states.append(state)
            prev_ref[c] = state.astype(prev_ref.dtype)
            state = cd * state + contribs[c]
        state_ref[...] = state
        cross = jnp.concatenate([_dot(qdf[rows], st, NN) for rows, st in zip(chunks, states)], axis=0)
        intra = []
        for b in range(RET_ROWS // RET_SUB):
            rows = slice(b * RET_SUB, (b + 1) * RET_SUB)
            intra.append(_dot(_dot(q[rows], k[rows], NT) * dec, v[rows], NN))
        y = jnp.concatenate(intra, axis=0) + cross
        y_ref[...] = y
        mu = jnp.mean(y, axis=-1, keepdims=True)
        yc = y - mu
        var = jnp.mean(yc * yc, axis=-1, keepdims=True)
        yn = yc * lax.rsqrt(var + GN_EPS) * gn
        ret_ref[...] = (rg * jax.nn.sigmoid(rg) * yn).astype(ret_ref.dtype)

    def col(off):
        return pl.BlockSpec((RET_ROWS, HEAD_DIM), lambda h, i: (i, off + h))

    pos = pl.BlockSpec((RET_ROWS, HEAD_DIM), lambda h, i: (i, 0))
    per_head = lambda shape: pl.BlockSpec((None, *shape), lambda h, i: (h, 0, 0))
    rs = _Riders(riders, 11, 3)
    res = _pallas(
        _with_riders(body, 11, 3, 1, rs, (HEADS, nb)), name="retention_fwd", grid=(HEADS, nb),
        in_specs=[col(0), col(HEADS), col(2 * HEADS), col(3 * HEADS), pos, pos,
                  per_head((RET_SUB, RET_SUB)), per_head((RET_ROWS, HEAD_DIM)), per_head((RET_ROWS, HEAD_DIM)),
                  per_head((1, HEAD_DIM)), pl.BlockSpec((1, HEAD_DIM), lambda h, i: (0, h)), *rs.in_specs],
        out_specs=[col(0), col(0),
                   pl.BlockSpec((None, RET_BLOCK_CHUNKS, HEAD_DIM, HEAD_DIM), lambda h, i: (h, i, 0, 0)),
                   *rs.out_specs],
        out_shape=[_sds((s, 2 * HEADS * HEAD_DIM), BF16), _sds((s, HEADS * HEAD_DIM), F32),
                   _sds((HEADS, nc, HEAD_DIM, HEAD_DIM), MXU_DTYPE), *rs.out_shapes],
        input_output_aliases=rs.aliases,
        scratch_shapes=[pltpu.VMEM((HEAD_DIM, HEAD_DIM), F32), *rs.scratch],
        compiler_params=_params("arbitrary", "arbitrary"),
    )(proj, proj, proj, proj, cos_f, sin_f, decay, k_dec, q_dec, c_dec, gn_g, *rs.arrays)
    return res[:3], rs.split(res[3:])


def _retention_bwd(proj, gn_g, tables, y, prev, dmix, riders=()):
    s = proj.shape[0]
    nb = s // RET_ROWS
    cos_f, sin_f, decay, k_dec, q_dec, c_dec = tables

    def body(q_ref, k_ref, v_ref, g_ref, cos_ref, sin_ref, dec_ref, kd_ref, qd_ref, cd_ref, gn_ref,
             y_ref, prev_ref, dret_ref, dproj_ref, dgn_ref, gstate_ref, stage_ref, stage_sems):
        head, blk = pl.program_id(0), pl.program_id(1)
        step = head * nb + blk
        slot = step % 2

        def writes(sl):
            rows = pl.ds(pl.multiple_of((nb - 1 - blk) * RET_ROWS, RET_ROWS), RET_ROWS)
            return [pltpu.make_async_copy(
                stage_ref.at[sl, g], dproj_ref.at[rows, pl.ds(pl.multiple_of((g * HEADS + head) * HEAD_DIM, HEAD_DIM), HEAD_DIM)],
                stage_sems.at[sl, g]) for g in range(4)]

        @pl.when(step >= 2)
        def _():
            for cp in writes(slot):
                cp.wait()

        @pl.when(blk == 0)
        def _():
            gstate_ref[...] = jnp.zeros_like(gstate_ref)
            dgn_ref[...] = jnp.zeros_like(dgn_ref)

        cosv, sinv = cos_ref[...], sin_ref[...]
        q = _rot(q_ref[...], cosv, sinv)
        k = _rot(k_ref[...], cosv, sinv) * K_SCALE
        v = v_ref[...]
        dec, kd, qd, cd, gn = dec_ref[...], kd_ref[...], qd_ref[...], cd_ref[...], gn_ref[...]
        kdf, qdf = k * kd, q * qd
        rg = g_ref[...]
        yv = y_ref[...]
        dret = dret_ref[...]
        sig = jax.nn.sigmoid(rg)
        gate = rg * sig
        mu = jnp.mean(yv, axis=-1, keepdims=True)
        yc = yv - mu
        rstd = lax.rsqrt(jnp.mean(yc * yc, axis=-1, keepdims=True) + GN_EPS)
        z = yc * rstd
        dyn = dret * gate
        stage_ref[slot, 3] = (dret * (z * gn) * (sig * (1.0 + rg * (1.0 - sig)))).astype(stage_ref.dtype)
        dgn_ref[...] += jnp.sum(dyn * z, axis=0, keepdims=True)
        dz = dyn * gn
        dy = rstd * (dz - jnp.mean(dz, axis=-1, keepdims=True) - z * jnp.mean(dz * z, axis=-1, keepdims=True))
        chunks = [slice(c * CHUNK, (c + 1) * CHUNK) for c in range(RET_BLOCK_CHUNKS)]
        dprevs = [_dot(qdf[rows], dy[rows], TN) for rows in chunks]
        gst, gsts = gstate_ref[...], [None] * RET_BLOCK_CHUNKS
        for c in reversed(range(RET_BLOCK_CHUNKS)):
            gsts[c] = gst
            gst = dprevs[c] + cd * gst
        gstate_ref[...] = gst
        dq = jnp.concatenate([_dot(dy[rows], prev_ref[c], NT) for c, rows in enumerate(chunks)], axis=0) * qd
        dk = jnp.concatenate([_dot(v[rows], g, NT) for rows, g in zip(chunks, gsts)], axis=0) * kd
        dv = jnp.concatenate([_dot(kdf[rows], g, NN) for rows, g in zip(chunks, gsts)], axis=0)
        dqi, dki, dvi = [], [], []
        for b in range(RET_ROWS // RET_SUB):
            rows = slice(b * RET_SUB, (b + 1) * RET_SUB)
            qs, ks, vs, dys = q[rows], k[rows], v[rows], dy[rows]
            dvi.append(_dot(_dot(ks, qs, NT) * dec, dys, NN))
            dqi.append(_dot(_dot(dys, vs, NT) * dec, ks, NN))
            dki.append(_dot(_dot(vs, dys, NT) * dec, qs, NN))
        dq = dq + jnp.concatenate(dqi, axis=0)
        dk = dk + jnp.concatenate(dki, axis=0)
        dv = dv + jnp.concatenate(dvi, axis=0)
        stage_ref[slot, 0] = _rot_bwd(dq, cosv, sinv).astype(stage_ref.dtype)
        stage_ref[slot, 1] = _rot_bwd(dk * K_SCALE, cosv, sinv).astype(stage_ref.dtype)
        stage_ref[slot, 2] = dv.astype(stage_ref.dtype)
        for cp in writes(slot):
            cp.start()

        @pl.when(step == HEADS * nb - 1)
        def _():
            for cp in writes(1 - slot) + writes(slot):
                cp.wait()

    rev = lambda i: nb - 1 - i

    def col(off):
        return pl.BlockSpec((RET_ROWS, HEAD_DIM), lambda h, i: (rev(i), off + h))

    pos = pl.BlockSpec((RET_ROWS, HEAD_DIM), lambda h, i: (rev(i), 0))
    per_head = lambda shape: pl.BlockSpec((None, *shape), lambda h, i: (h, 0, 0))
    rs = _Riders(riders, 14, 2)
    res = _pallas(
        _with_riders(body, 14, 2, 3, rs, (HEADS, nb)), name="retention_bwd", grid=(HEADS, nb),
        in_specs=[col(0), col(HEADS), col(2 * HEADS), col(3 * HEADS), pos, pos,
                  per_head((RET_SUB, RET_SUB)), per_head((RET_ROWS, HEAD_DIM)), per_head((RET_ROWS, HEAD_DIM)),
                  per_head((1, HEAD_DIM)), pl.BlockSpec((1, HEAD_DIM), lambda h, i: (0, h)),
                  col(0), pl.BlockSpec((None, RET_BLOCK_CHUNKS, HEAD_DIM, HEAD_DIM), lambda h, i: (h, rev(i), 0, 0)),
                  col(0), *rs.in_specs],
        out_specs=[ANY, per_head((1, HEAD_DIM)), *rs.out_specs],
        out_shape=[_sds((s, proj.shape[1]), BF16), _sds((HEADS, 1, HEAD_DIM), F32), *rs.out_shapes],
        input_output_aliases=rs.aliases,
        scratch_shapes=[pltpu.VMEM((HEAD_DIM, HEAD_DIM), F32), pltpu.VMEM((2, 4, RET_ROWS, HEAD_DIM), BF16),
                        pltpu.SemaphoreType.DMA((2, 4)), *rs.scratch],
        compiler_params=_params("arbitrary", "arbitrary"),
    )(proj, proj, proj, proj, cos_f, sin_f, decay, k_dec, q_dec, c_dec, gn_g, y, prev, dmix, *rs.arrays)
    return res[:2], rs.split(res[2:])


ATT_COL0 = 4 * HEADS
PAD_ROWS = LEFT_CHUNKS * CHUNK
NORM_ROWS = 512
GROUP_CHUNKS = 4
GROUP = GROUP_CHUNKS * CHUNK
WIN = (LEFT_CHUNKS + GROUP_CHUNKS) * CHUNK
MASKED = -1e30


def _qk_norm(x, g):
    return x * lax.rsqrt(jnp.mean(x * x, axis=-1, keepdims=True) + EPS) * g


def _band_probs(qb, kb, bias, g):
    sc = _dot(qb, kb, NT) * K_SCALE + bias
    win_chunk = lax.broadcasted_iota(jnp.int32, (GROUP, WIN), 1) // CHUNK
    sc = jnp.where(g * GROUP_CHUNKS - LEFT_CHUNKS + win_chunk >= 0, sc, MASKED)
    e = jnp.exp(sc - jnp.max(sc, axis=-1, keepdims=True))
    return e / jnp.sum(e, axis=-1, keepdims=True)


def _band_probs_t(qb, kb, bias_t, g):
    sc = _dot(kb, qb, NT) * K_SCALE + bias_t
    win_chunk = lax.broadcasted_iota(jnp.int32, (WIN, GROUP), 0) // CHUNK
    sc = jnp.where(g * GROUP_CHUNKS - LEFT_CHUNKS + win_chunk >= 0, sc, MASKED)
    e = jnp.exp(sc - jnp.max(sc, axis=0, keepdims=True))
    return e / jnp.sum(e, axis=0, keepdims=True)


def _with_riders(core, n_in, n_out, n_scratch, rs, grid):
    n_rin, n_rout = len(rs.arrays), len(rs.out_shapes)
    if not rs.riders:
        return core
    steps = 1
    for n in grid:
        steps *= n

    def body(*refs):
        outs_at = n_in + n_rin
        scratch_at = outs_at + n_out + n_rout
        bound = rs.bind(refs[n_in:outs_at], refs[outs_at + n_out:scratch_at], refs[scratch_at + n_scratch:])
        step = 0
        for axis, n in enumerate(grid):
            step = step * n + pl.program_id(axis)
        rs.hooks(step, steps, bound)
        core(*refs[:n_in], *refs[outs_at:outs_at + n_out], *refs[scratch_at:scratch_at + n_scratch])
        pl.when(step == steps - 1)(lambda: rs.run("end", bound))

    return body


def _attention_fwd(proj, gq, gk, bias, mix, riders=()):
    s = proj.shape[0]
    rs = _Riders(riders, 7, 1)

    def body(q_ref, k_ref, v_ref, gq_ref, gk_ref, bias_ref, mix_ref, o_ref, kp_ref, vp_ref):
        kp_ref[0:PAD_ROWS, :] = jnp.zeros((PAD_ROWS, HEAD_DIM), kp_ref.dtype)
        vp_ref[0:PAD_ROWS, :] = jnp.zeros((PAD_ROWS, HEAD_DIM), vp_ref.dtype)
        gqv, gkv = gq_ref[...], gk_ref[...]

        def fill(b, carry):
            r0 = pl.multiple_of(b * NORM_ROWS, NORM_ROWS)
            kp_ref[pl.ds(PAD_ROWS + r0, NORM_ROWS), :] = _qk_norm(k_ref[pl.ds(r0, NORM_ROWS), :], gkv).astype(kp_ref.dtype)
            vp_ref[pl.ds(PAD_ROWS + r0, NORM_ROWS), :] = v_ref[pl.ds(r0, NORM_ROWS), :].astype(vp_ref.dtype)
            return carry

        lax.fori_loop(0, s // NORM_ROWS, fill, 0, unroll=2)

        def group(g, carry):
            r0 = pl.multiple_of(g * GROUP, GROUP)
            qn = _qk_norm(q_ref[pl.ds(r0, GROUP), :], gqv)
            p = _band_probs(qn, kp_ref[pl.ds(r0, WIN), :], bias_ref[...], g)
            o_ref[pl.ds(r0, GROUP), :] = _dot(p, vp_ref[pl.ds(r0, WIN), :], NN).astype(o_ref.dtype)
            return carry

        lax.fori_loop(0, s // GROUP, group, 0, unroll=8)

    def col(off):
        return pl.BlockSpec((s, HEAD_DIM), lambda h: (0, off + h))

    vec = pl.BlockSpec((1, HEAD_DIM), lambda h: (0, 0))
    res = _pallas(
        _with_riders(body, 7, 1, 2, rs, (HEADS,)), name="attention_fwd", grid=(HEADS,),
        in_specs=[col(ATT_COL0), col(ATT_COL0 + HEADS), col(ATT_COL0 + 2 * HEADS), vec, vec,
                  pl.BlockSpec((None, GROUP, WIN), lambda h: (h, 0, 0)), ANY, *rs.in_specs],
        out_specs=[col(HEADS), *rs.out_specs], out_shape=[_sds(mix.shape, mix.dtype), *rs.out_shapes],
        input_output_aliases={6: 0, **rs.aliases},
        scratch_shapes=[pltpu.VMEM((s + PAD_ROWS, HEAD_DIM), MXU_DTYPE), pltpu.VMEM((s + PAD_ROWS, HEAD_DIM), MXU_DTYPE),
                        *rs.scratch],
        compiler_params=_params("arbitrary"),
    )(proj, proj, proj, gq, gk, bias, mix, *rs.arrays)
    return res[0], rs.split(res[1:])


def _attention_bwd(proj, gq, gk, bias, dmix, dproj, riders=()):
    s = proj.shape[0]
    rs = _Riders(riders, 8, 4)

    def body(q_ref, k_ref, v_ref, gq_ref, gk_ref, bias_ref, do_ref, dproj_in_ref,
             dproj_ref, dgq_ref, dgk_ref, dbias_ref, kp_ref, vp_ref, dkp_ref, dvp_ref, dqn_ref, stage_ref, stage_sems,
             bias_t_ref, dbias_t_ref):
        head = pl.program_id(0)

        def writes():
            return [pltpu.make_async_copy(
                stage_ref.at[g],
                dproj_ref.at[:, pl.ds(pl.multiple_of((ATT_COL0 + g * HEADS + head) * HEAD_DIM, HEAD_DIM), HEAD_DIM)],
                stage_sems.at[g]) for g in range(3)]

        kp_ref[0:PAD_ROWS, :] = jnp.zeros((PAD_ROWS, HEAD_DIM), kp_ref.dtype)
        vp_ref[0:PAD_ROWS, :] = jnp.zeros((PAD_ROWS, HEAD_DIM), vp_ref.dtype)
        dkp_ref[...] = jnp.zeros_like(dkp_ref)
        dvp_ref[...] = jnp.zeros_like(dvp_ref)
        dbias_t_ref[...] = jnp.zeros_like(dbias_t_ref)
        bias_t_ref[...] = bias_ref[...].T
        gqv, gkv = gq_ref[...], gk_ref[...]

        def fill(b, carry):
            r0 = pl.multiple_of(b * NORM_ROWS, NORM_ROWS)
            kp_ref[pl.ds(PAD_ROWS + r0, NORM_ROWS), :] = _qk_norm(k_ref[pl.ds(r0, NORM_ROWS), :], gkv).astype(kp_ref.dtype)
            vp_ref[pl.ds(PAD_ROWS + r0, NORM_ROWS), :] = v_ref[pl.ds(r0, NORM_ROWS), :].astype(vp_ref.dtype)
            return carry

        lax.fori_loop(0, s // NORM_ROWS, fill, 0, unroll=2)

        def group(g, carry):
            r0 = pl.multiple_of(g * GROUP, GROUP)
            qn = _qk_norm(q_ref[pl.ds(r0, GROUP), :], gqv)
            kb = kp_ref[pl.ds(r0, WIN), :]
            vb = vp_ref[pl.ds(r0, WIN), :]
            p = _band_probs_t(qn, kb, bias_t_ref[...], g)
            do = do_ref[pl.ds(r0, GROUP), :]
            dvp_ref[pl.ds(r0, WIN), :] += _dot(p, do, NN)
            dp = _dot(vb, do, NT)
            ds = p * (dp - jnp.sum(dp * p, axis=0, keepdims=True))
            dbias_t_ref[...] += ds
            dss = ds * K_SCALE
            dqn_ref[pl.ds(r0, GROUP), :] = _dot(dss, kb, TN)
            dkp_ref[pl.ds(r0, WIN), :] += _dot(dss, qn, NN)
            return carry

        lax.fori_loop(0, s // GROUP, group, 0, unroll=8)
        dbias_ref[...] = dbias_t_ref[...].T

        @pl.when(head == 0)
        def _():
            dgq_ref[...] = jnp.zeros_like(dgq_ref)
            dgk_ref[...] = jnp.zeros_like(dgk_ref)

        @pl.when(head > 0)
        def _():
            for cp in writes():
                cp.wait()

        def norm_bwd(x, g, dn):
            rstd = lax.rsqrt(jnp.mean(x * x, axis=-1, keepdims=True) + EPS)
            xh = x * rstd
            dxh = dn * g
            return rstd * (dxh - xh * jnp.mean(dxh * xh, axis=-1, keepdims=True)), jnp.sum(dn * xh, axis=0, keepdims=True)

        def finish(b, carry):
            r0 = pl.multiple_of(b * NORM_ROWS, NORM_ROWS)
            rows = pl.ds(r0, NORM_ROWS)
            dq, dgq = norm_bwd(q_ref[rows, :], gqv, dqn_ref[rows, :])
            dk, dgk = norm_bwd(k_ref[rows, :], gkv, dkp_ref[pl.ds(PAD_ROWS + r0, NORM_ROWS), :])
            stage_ref[0, rows, :] = dq.astype(stage_ref.dtype)
            stage_ref[1, rows, :] = dk.astype(stage_ref.dtype)
            stage_ref[2, rows, :] = dvp_ref[pl.ds(PAD_ROWS + r0, NORM_ROWS), :].astype(stage_ref.dtype)
            dgq_ref[...] += dgq
            dgk_ref[...] += dgk
            return carry

        lax.fori_loop(0, s // NORM_ROWS, finish, 0, unroll=2)
        for cp in writes():
            cp.start()

        @pl.when(head == HEADS - 1)
        def _():
            for cp in writes():
                cp.wait()

    def col(off):
        return pl.BlockSpec((s, HEAD_DIM), lambda h: (0, off + h))

    vec = pl.BlockSpec((1, HEAD_DIM), lambda h: (0, 0))
    hbias = pl.BlockSpec((None, GROUP, WIN), lambda h: (h, 0, 0))
    res = _pallas(
        _with_riders(body, 8, 4, 9, rs, (HEADS,)), name="attention_bwd", grid=(HEADS,),
        in_specs=[col(ATT_COL0), col(ATT_COL0 + HEADS), col(ATT_COL0 + 2 * HEADS), vec, vec, hbias, col(HEADS), ANY,
                  *rs.in_specs],
        out_specs=[ANY, vec, vec, hbias, *rs.out_specs],
        out_shape=[_sds(dproj.shape, dproj.dtype), _sds((1, HEAD_DIM), F32), _sds((1, HEAD_DIM), F32),
                   _sds((HEADS, GROUP, WIN), F32), *rs.out_shapes],
        input_output_aliases={7: 0, **rs.aliases},
        scratch_shapes=[pltpu.VMEM((s + PAD_ROWS, HEAD_DIM), MXU_DTYPE), pltpu.VMEM((s + PAD_ROWS, HEAD_DIM), MXU_DTYPE),
                        pltpu.VMEM((s + PAD_ROWS, HEAD_DIM), F32), pltpu.VMEM((s + PAD_ROWS, HEAD_DIM), F32),
                        pltpu.VMEM((s, HEAD_DIM), F32), pltpu.VMEM((3, s, HEAD_DIM), BF16),
                        pltpu.SemaphoreType.DMA((3,)), pltpu.VMEM((WIN, GROUP), F32), pltpu.VMEM((WIN, GROUP), F32),
                        *rs.scratch],
        compiler_params=_params("arbitrary"),
    )(proj, proj, proj, gq, gk, bias, dmix, dproj, *rs.arrays)
    return res[:4], rs.split(res[4:])


DIAG_SPLIT = (BAND + WIN - CHUNK) // 2


def _diag_bin(m):
    t = jnp.where(m < DIAG_SPLIT, m, m - WIN)
    return jnp.clip(LEFT_CHUNKS * CHUNK - t, -(CHUNK - 1), REL_CLIP) + (CHUNK - 1)


def _skew_rows(a, left):
    row = lax.broadcasted_iota(jnp.int32, (GROUP, WIN), 0)
    for b in range(GROUP.bit_length() - 1):
        step = 1 << b
        a = jnp.where(jnp.bitwise_and(row, step) != 0, pltpu.roll(a, WIN - step if left else step, 1), a)
    return a


def _rel_bias_expand(rel_bias):
    def body(rb_ref, o_ref):
        onehot = (lax.broadcasted_iota(jnp.int32, (REL_SIZE, WIN), 0)
                  == _diag_bin(lax.broadcasted_iota(jnp.int32, (REL_SIZE, WIN), 1))).astype(MXU_DTYPE)
        rest = jnp.broadcast_to(rb_ref[...], (8, REL_SIZE))
        per_diag = jnp.zeros((8, WIN), F32)
        for _ in range(3):
            piece = rest.astype(BF16)
            per_diag = per_diag + _dot(piece, onehot, NN)
            rest = rest - piece.astype(F32)
        table = _skew_rows(jnp.broadcast_to(per_diag[0:1], (GROUP, WIN)), left=False)
        row_chunk = lax.broadcasted_iota(jnp.int32, (GROUP, WIN), 0) // CHUNK
        col_chunk = lax.broadcasted_iota(jnp.int32, (GROUP, WIN), 1) // CHUNK
        in_band = jnp.logical_and(col_chunk >= row_chunk, col_chunk <= row_chunk + LEFT_CHUNKS)
        o_ref[...] = jnp.where(in_band, table, MASKED)

    return _pallas(body, name="rel_bias_expand", grid=(HEADS,),
                   in_specs=[pl.BlockSpec((None, 1, REL_SIZE), lambda h: (h, 0, 0))],
                   out_specs=pl.BlockSpec((None, GROUP, WIN), lambda h: (h, 0, 0)),
                   out_shape=_sds((HEADS, GROUP, WIN), F32), compiler_params=_params("parallel"))(rel_bias)


def _rel_bias_fold(dbias):
    def body(a_ref, o_ref):
        diag = jnp.sum(_skew_rows(a_ref[...], left=True), axis=0, keepdims=True)
        onehot = (_diag_bin(lax.broadcasted_iota(jnp.int32, (WIN, REL_SIZE), 0))
                  == lax.broadcasted_iota(jnp.int32, (WIN, REL_SIZE), 1)).astype(MXU_DTYPE)
        rest = jnp.broadcast_to(diag, (8, WIN))
        out = jnp.zeros((8, REL_SIZE), F32)
        for _ in range(3):
            piece = rest.astype(BF16)
            out = out + _dot(piece, onehot, NN)
            rest = rest - piece.astype(F32)
        o_ref[...] = out[0:1]

    return _pallas(body, name="rel_bias_fold", grid=(HEADS,),
                   in_specs=[pl.BlockSpec((None, GROUP, WIN), lambda h: (h, 0, 0))],
                   out_specs=pl.BlockSpec((None, 1, REL_SIZE), lambda h: (h, 0, 0)),
                   out_shape=_sds((HEADS, 1, REL_SIZE), F32), compiler_params=_params("parallel"))(dbias)


def _place():
    return lax.axis_index("x"), lax.axis_index("y"), lax.axis_index("c")


def _other_chips(x, y):
    return [(1 - x, y), (x, 1 - y), (1 - x, 1 - y)]


class _Rider:
    reads, ins, new, n_sems = (), (), (), 1

    def schedule(self):
        return [(0.0, self.start), (RIDER_MID, self.mid), (RIDER_LATE, self.late)]

    def start(self, reads, ins, new, send, recv):
        pass

    def mid(self, reads, ins, new, send, recv):
        pass

    def late(self, reads, ins, new, send, recv):
        pass

    def end(self, reads, ins, new, send, recv):
        pass


class _Riders:
    def __init__(self, riders, n_host_in, n_host_out):
        self.riders = list(riders)
        self.arrays, self.out_shapes, self.aliases, self.scratch = [], [], {}, []
        for r in self.riders:
            for t, a in enumerate(r.ins):
                self.aliases[n_host_in + len(self.arrays) + len(r.reads) + t] = n_host_out + len(self.out_shapes) + t
            self.arrays += [*r.reads, *r.ins]
            self.out_shapes += [_sds(a.shape, a.dtype) for a in r.ins] + list(r.new)
            self.scratch += [pltpu.SemaphoreType.DMA((r.n_sems,)), pltpu.SemaphoreType.DMA((r.n_sems,))]
        self.in_specs = [ANY] * len(self.arrays)
        self.out_specs = [ANY] * len(self.out_shapes)

    def bind(self, in_refs, out_refs, scratch_refs):
        bound, i, o = [], 0, 0
        for t, r in enumerate(self.riders):
            reads = in_refs[i:i + len(r.reads)]
            i += len(r.reads) + len(r.ins)
            ins = out_refs[o:o + len(r.ins)]
            new = out_refs[o + len(r.ins):o + len(r.ins) + len(r.new)]
            o += len(r.ins) + len(r.new)
            bound.append((reads, ins, new, scratch_refs[2 * t], scratch_refs[2 * t + 1]))
        return bound

    def run(self, phase, bound):
        for r, b in zip(self.riders, bound):
            getattr(r, phase)(*b)

    def hooks(self, step, steps, bound):
        at = {}
        for r, b in zip(self.riders, bound):
            for frac, action in r.schedule():
                at.setdefault(min(int(steps * frac), steps - 1), []).append((action, b))
        for at_step, todo in sorted(at.items()):
            def go(todo=todo):
                for action, b in todo:
                    action(*b)

            pl.when(step == at_step)(go)

    def split(self, outs):
        res, o = [], 0
        for r in self.riders:
            n = len(r.ins) + len(r.new)
            res.append(list(outs[o:o + n]))
            o += n
        return res


def _run_riders(name, riders):
    rs = _Riders(riders, 0, 0)
    n_in, n_out = len(rs.arrays), len(rs.out_shapes)

    def body(*refs):
        bound = rs.bind(refs[:n_in], refs[n_in:n_in + n_out], refs[n_in + n_out:])
        rs.run("start", bound)
        rs.run("mid", bound)
        rs.run("late", bound)
        rs.run("end", bound)

    outs = _pallas(body, name=name, in_specs=rs.in_specs, out_specs=rs.out_specs, out_shape=rs.out_shapes,
                   input_output_aliases=rs.aliases, scratch_shapes=rs.scratch)(*rs.arrays)
    return rs.split(outs)


class _GatherRider(_Rider):
    X_LINK, Y_LINK, Y_PASS, X_PASS, D2D_X, D2D_Y, D2D_DIAG, N_SEMS = 0, 1, 2, 3, 4, 5, 6, 7

    def __init__(self, blocks, part=(0, 1, 1)):
        self.ins = tuple(blocks)
        self.part = part
        self.n_sems = self.N_SEMS * len(blocks)

    def _copy(self, out, send, recv, w, sem, chip_from, cc, to, sub=None):
        hr = self.ins[w].shape[1] // 2
        lo, hi, n = self.part
        first, size = cc * hr + lo * (hr // n), (hi - lo) * (hr // n)
        if sub is not None:
            size //= 2
            first += sub * size
        piece = out[w].at[2 * chip_from[0] + chip_from[1], pl.ds(first, size), :]
        return pltpu.make_async_remote_copy(src_ref=piece, dst_ref=piece, send_sem=send.at[self.N_SEMS * w + sem],
                                            recv_sem=recv.at[self.N_SEMS * w + sem], device_id=to, device_id_type=MESH)

    def _sent(self, out, send, recv, w):
        x, y, c = _place()
        me, sib = (x, y), (x, y, 1 - c)
        xn, yn, diag = _other_chips(x, y)
        cp = functools.partial(self._copy, out, send, recv, w)
        return [("start", cp(self.X_LINK, me, c, (*xn, c))), ("start", cp(self.Y_LINK, me, c, (*yn, c))),
                ("mid_x", cp(self.D2D_X, xn, c, sib)), ("mid_x", cp(self.Y_PASS, xn, c, (*yn, c), sub=0)),
                ("mid_y", cp(self.D2D_Y, yn, c, sib)), ("mid_y", cp(self.X_PASS, yn, c, (*xn, c), sub=1)),
                ("late", cp(self.D2D_DIAG, diag, c, sib))]

    def _go(self, out, send, recv, phase):
        for w in range(len(self.ins)):
            for ph, copy in self._sent(out, send, recv, w):
                if ph == phase:
                    copy.start()

    def start(self, reads, out, new, send, recv):
        self._go(out, send, recv, "start")

    def mid(self, reads, out, new, send, recv):
        x, y, c = _place()
        xn, yn, _ = _other_chips(x, y)
        for w in range(len(self.ins)):
            self._copy(out, send, recv, w, self.X_LINK, xn, c, (x, y, c)).wait_recv()
        self._go(out, send, recv, "mid_x")
        for w in range(len(self.ins)):
            self._copy(out, send, recv, w, self.Y_LINK, yn, c, (x, y, c)).wait_recv()
        self._go(out, send, recv, "mid_y")

    def late(self, reads, out, new, send, recv):
        x, y, c = _place()
        diag = _other_chips(x, y)[2]
        for w in range(len(self.ins)):
            self._copy(out, send, recv, w, self.Y_PASS, diag, c, (x, y, c), sub=0).wait_recv()
            self._copy(out, send, recv, w, self.X_PASS, diag, c, (x, y, c), sub=1).wait_recv()
        self._go(out, send, recv, "late")

    def end(self, reads, out, new, send, recv):
        x, y, c = _place()
        xn, yn, diag = _other_chips(x, y)
        for w in range(len(self.ins)):
            for sem, chip in ((self.D2D_X, xn), (self.D2D_Y, yn), (self.D2D_DIAG, diag)):
                self._copy(out, send, recv, w, sem, chip, 1 - c, (x, y, c)).wait_recv()
        for w in range(len(self.ins)):
            for _, copy in self._sent(out, send, recv, w):
                copy.wait_send()


class _SwapRider(_Rider):
    def __init__(self, grads):
        self.reads = tuple(grads)
        self.new = tuple(_sds((N_CHIPS, g.shape[1] // 2, g.shape[2]), g.dtype) for g in grads)
        self.n_sems = len(grads)

    def _copies(self, src, new, send, recv):
        x, y, c = _place()
        copies = []
        for w in range(len(self.reads)):
            hr = self.reads[w].shape[1] // 2
            copies.append(pltpu.make_async_remote_copy(
                src_ref=src[w].at[:, pl.ds((1 - c) * hr, hr), :], dst_ref=new[w],
                send_sem=send.at[w], recv_sem=recv.at[w], device_id=(x, y, 1 - c), device_id_type=MESH))
        return copies

    def start(self, src, ins, new, send, recv):
        for cp in self._copies(src, new, send, recv):
            cp.start()

    def end(self, src, ins, new, send, recv):
        for cp in self._copies(src, new, send, recv):
            cp.wait()


def _add_half(g, got, c_arr, name):
    nk, r, cols = g.shape
    hr = r // 2
    tr = min(hr, 1024)
    nb = hr // tr

    def body(c_ref, g_ref, got_ref, o_ref):
        o_ref[...] = (g_ref[...].astype(F32) + got_ref[...].astype(F32)).astype(o_ref.dtype)

    grid_spec = pltpu.PrefetchScalarGridSpec(
        num_scalar_prefetch=1, grid=(nk, nb),
        in_specs=[pl.BlockSpec((None, tr, cols), lambda k, i, c_ref: (k, c_ref[0] * nb + i, 0)),
                  pl.BlockSpec((None, tr, cols), lambda k, i, c_ref: (k, i, 0))],
        out_specs=pl.BlockSpec((None, tr, cols), lambda k, i, c_ref: (k, i, 0)))
    return _pallas(body, name=name, grid_spec=grid_spec, out_shape=_sds((nk, hr, cols), g.dtype),
                   compiler_params=_params("parallel", "parallel"))(c_arr, g, got)


class _SendPartialsRider(_Rider):
    def __init__(self, parts, got=None, part=(0, 1, 1)):
        self.reads = tuple(parts)
        if got is None:
            self.new = tuple(_sds((N_CHIPS - 1, *p.shape[1:]), p.dtype) for p in parts)
        else:
            self.ins = tuple(got)
        self.part = part
        self.n_sems = 3 * len(parts)

    def _copies(self, src, ins, new, send, recv):
        x, y, c = _place()
        land = ins if self.ins else new
        lo, hi, n = self.part
        copies = []
        for w in range(len(self.reads)):
            pr = self.reads[w].shape[1] // n
            rows = pl.ds(lo * pr, (hi - lo) * pr)
            for j, chip in enumerate(_other_chips(x, y)):
                copies.append(pltpu.make_async_remote_copy(
                    src_ref=src[w].at[2 * chip[0] + chip[1], rows, :], dst_ref=land[w].at[j, rows, :],
                    send_sem=send.at[3 * w + j], recv_sem=recv.at[3 * w + j], device_id=(*chip, c), device_id_type=MESH))
        return copies

    def start(self, src, ins, new, send, recv):
        for cp in self._copies(src, ins, new, send, recv):
            cp.start()

    def end(self, src, ins, new, send, recv):
        for cp in self._copies(src, ins, new, send, recv):
            cp.wait()


def _sum_partials(part, got, kc_arr, name):
    _, hr, cols = part.shape
    tr = min(hr, 512)
    nb = hr // tr

    def body(kc_ref, p_ref, g0_ref, g1_ref, g2_ref, o_ref):
        o_ref[...] = ((p_ref[...].astype(F32) + g0_ref[...].astype(F32)) + g1_ref[...].astype(F32)) + g2_ref[...].astype(F32)

    slot = lambda j: pl.BlockSpec((None, tr, cols), lambda i, kc_ref: (j, i, 0))
    grid_spec = pltpu.PrefetchScalarGridSpec(
        num_scalar_prefetch=1, grid=(nb,),
        in_specs=[pl.BlockSpec((None, tr, cols), lambda i, kc_ref: (kc_ref[0], i, 0)), slot(0), slot(1), slot(2)],
        out_specs=pl.BlockSpec((tr, cols), lambda i, kc_ref: (kc_ref[1] * nb + i, 0)))
    return _pallas(body, name=name, grid_spec=grid_spec, out_shape=_sds((2 * hr, cols), F32),
                   compiler_params=_params("parallel"))(kc_arr, part, got, got, got)


class _ShareRider(_Rider):
    def __init__(self, grads):
        self.ins = tuple(grads)
        self.n_sems = len(grads)

    def _copies(self, out, send, recv):
        x, y, c = _place()
        copies = []
        for w in range(len(self.ins)):
            hr = self.ins[w].shape[0] // 2
            mine = out[w].at[pl.ds(c * hr, hr), :]
            copies.append(pltpu.make_async_remote_copy(
                src_ref=mine, dst_ref=mine, send_sem=send.at[w], recv_sem=recv.at[w],
                device_id=(x, y, 1 - c), device_id_type=MESH))
        return copies

    def start(self, reads, out, new, send, recv):
        for cp in self._copies(out, send, recv):
            cp.start()

    def end(self, reads, out, new, send, recv):
        for cp in self._copies(out, send, recv):
            cp.wait()


def _small_allreduce_adamw(g_part, w, m, v, riders=()):
    rows = g_part.shape[0]
    rs = _Riders(riders, 4, 4)
    n_rin, n_rout = len(rs.arrays), len(rs.out_shapes)

    def body(*refs):
        g_ref, w_ref, m_ref, v_ref = refs[:4]
        go_ref, d_ref, nm_ref, nv_ref = refs[4 + n_rin:8 + n_rin]
        all_ref, send_sems, recv_sems = refs[8 + n_rin + n_rout:11 + n_rin + n_rout]
        bound = rs.bind(refs[4:4 + n_rin], refs[8 + n_rin:8 + n_rin + n_rout], refs[11 + n_rin + n_rout:])
        rs.run("start", bound)
        x, y, c = _place()
        me = 4 * x + 2 * y + c
        all_ref[me] = g_ref[...]
        copies = []
        for r in range(1, 8):
            dx, dy, dc = (r >> 2) & 1, (r >> 1) & 1, r & 1
            peer = (1 - x if dx else x, 1 - y if dy else y, 1 - c if dc else c)
            copies.append(pltpu.make_async_remote_copy(
                src_ref=g_ref, dst_ref=all_ref.at[me], send_sem=send_sems.at[r - 1], recv_sem=recv_sems.at[r - 1],
                device_id=peer, device_id_type=MESH))
        for cp in copies:
            cp.start()
        for cp in copies:
            cp.wait()
        tot = all_ref[0]
        for d in range(1, 8):
            tot = tot + all_ref[d]
        go_ref[...] = tot
        d_ref[...], nm_ref[...], nv_ref[...] = _adamw_math(w_ref[...], tot, m_ref[...], v_ref[...])
        for phase in ("mid", "late", "end"):
            rs.run(phase, bound)

    vm = pl.BlockSpec(memory_space=pltpu.VMEM)
    out = _pallas(
        body, name="small_allreduce_adamw", in_specs=[vm] * 4 + rs.in_specs, out_specs=[vm] * 4 + rs.out_specs,
        out_shape=[_sds((rows, 128), F32)] * 4 + rs.out_shapes, input_output_aliases=rs.aliases,
        scratch_shapes=[pltpu.VMEM((8, rows, 128), F32), pltpu.SemaphoreType.DMA((7,)), pltpu.SemaphoreType.DMA((7,)),
                        *rs.scratch],
    )(g_part, w, m, v, *rs.arrays)
    return out[:4], rs.split(out[4:])


SMALL_SIZES = (2048, 1024, 128, 128, HEADS * REL_SIZE, 2048)
SMALL_PART_ROWS = tuple(-(-size // 1024) * 8 for size in SMALL_SIZES)
SMALL_ROWS = sum(SMALL_PART_ROWS)


def _pack_small(parts):
    rows = []
    for p, size, nr in zip(parts, SMALL_SIZES, SMALL_PART_ROWS):
        rows.append(jnp.pad(p.reshape(-1), (0, nr * 128 - size)).reshape(nr, 128))
    return jnp.concatenate(rows, axis=0)


def _unpack_small(slab, shapes):
    out, off = [], 0
    for size, nr, shape in zip(SMALL_SIZES, SMALL_PART_ROWS, shapes):
        out.append(slab[off:off + nr].reshape(-1)[:size].reshape(shape))
        off += nr
    return out


def kernel(x, norm1_g, w_in, ret_norm_g, q_norm_g, k_norm_g, rel_bias, w_out, norm2_g, w_ff1, w_ff2, loss_target, m_norm1_g, m_w_in, m_ret_norm_g, m_q_norm_g, m_k_norm_g, m_rel_bias, m_w_out, m_norm2_g, m_w_ff1, m_w_ff2, v_norm1_g, v_w_in, v_ret_norm_g, v_q_norm_g, v_k_norm_g, v_rel_bias, v_w_out, v_norm2_g, v_w_ff1, v_w_ff2):
    xs = x[0]
    tgt = loss_target[0]
    s, d = xs.shape
    d_in = N_CHIPS * w_in.shape[2]
    d_ff = N_CHIPS * w_ff1.shape[2]
    in_sh, ff_sh = w_in.shape[2], w_ff1.shape[2]
    tm = min(s, 1024)
    gi = s // tm
    c_arr = lax.axis_index("c").astype(jnp.int32).reshape(1)
    k_arr = (2 * lax.axis_index("x") + lax.axis_index("y")).astype(jnp.int32).reshape(1)
    tables = _tables(s)
    bias = _rel_bias_expand(rel_bias[0][:, None, :])

    blk_in = [_cast_bf16(w_in[0], k_arr, "cast_w_in_%d" % half, cols=(half, 2)) for half in range(2)]
    blk_out, blk_ff1, blk_ff2 = (_cast_bf16(w_out[0], k_arr, "cast_w_out"), _cast_bf16(w_ff1[0], k_arr, "cast_w_ff1"),
                                 _cast_bf16(w_ff2[0], k_arr, "cast_w_ff2"))

    h1, ((wg_in0,),) = _rmsnorm_fwd(xs, norm1_g, "rmsnorm1", riders=[_GatherRider([blk_in[0]])])
    tn_in = in_sh // 2
    tk = d

    def proj_half(half, wg, through, riders):
        return _mm("proj_%d" % half, h1, wg, NN, (gi, N_CHIPS, 1),
                   pl.BlockSpec((tm, tk), lambda i, j, k: (i, 0)), pl.BlockSpec((None, tk, tn_in), lambda i, j, k: (j, 0, 0)),
                   [_sds((s, d_in), F32)], [pl.BlockSpec((tm, tn_in), lambda i, j, k: (i, 2 * j + half))], (tm, tn_in),
                   riders=riders, through=through)

    (proj,), ((wg_in1,),) = proj_half(0, wg_in0, None, [_GatherRider([blk_in[1]])])
    (proj,), ((wg_ff1,),) = proj_half(1, wg_in1, proj, [_GatherRider([blk_ff1], (0, 3, 8))])
    (mix, y_ret, prev), ((wg_ff1,),) = _retention_fwd(proj, ret_norm_g, tables, riders=[_GatherRider([wg_ff1], (3, 6, 8))])
    mix, ((wg_out,), (wg_ff2,)) = _attention_fwd(
        proj, q_norm_g, k_norm_g, bias, mix, riders=[_GatherRider([blk_out]), _GatherRider([blk_ff2], (0, 1, 4))])
    wg_out = wg_out.reshape(d, d)
    tn = 1024
    tile = pl.BlockSpec((tm, tn), lambda i, j, k: (i, j))
    def residual_norm(acc, res, g):
        x1v = res + acc
        yv = x1v * lax.rsqrt(jnp.mean(x1v * x1v, axis=-1, keepdims=True) + EPS)
        return x1v, yv * g

    tmo = min(s, 512)
    rows = pl.BlockSpec((tmo, d), lambda i, j, k: (i, 0))
    (x1, h2), ((wg_ff1,),) = _mm(
        "out_proj", mix, wg_out, NN, (s // tmo, 1, 1),
        rows, pl.BlockSpec((d, d), lambda i, j, k: (0, 0)),
        [_sds((s, d), F32), _sds((s, d), BF16)], [rows, rows], (tmo, d),
        extras=(xs, norm2_g), extra_specs=(rows, pl.BlockSpec((1, d), lambda i, j, k: (0, 0))),
        epi=residual_norm, riders=[_GatherRider([wg_ff1], (6, 8, 8))])
    tn_ff = min(ff_sh, 1024)
    per = ff_sh // tn_ff

    def relu2(acc):
        r = jnp.maximum(acc, 0.0)
        return acc, r * r

    (u, act), ((wg_ff2,),) = _mm(
        "ff1", h2, wg_ff1, NN, (gi, N_CHIPS * per, d // tk),
        pl.BlockSpec((tm, tk), lambda i, j, k: (i, k)),
        pl.BlockSpec((None, tk, tn_ff), lambda i, j, k: (j // per, k, j % per)),
        [_sds((s, d_ff), F32), _sds((s, d_ff), BF16)],
        [pl.BlockSpec((tm, tn_ff), lambda i, j, k: (i, j))] * 2, (tm, tn_ff), epi=relu2,
        riders=[_GatherRider([wg_ff2], (1, 4, 4))])
    wg_ff2 = wg_ff2.reshape(d_ff, d)

    def loss_epi(acc, res, t):
        diff = (res + acc) - t
        dy = diff / d
        return dy, dy, jnp.sum(diff * diff, axis=0, keepdims=True)

    tk2 = min(tk, 2048)
    dy, dyb, loss_cols = _mm(
        "ff2_loss", act, wg_ff2, NN, (gi, d // tn, d_ff // tk2),
        pl.BlockSpec((tm, tk2), lambda i, j, k: (i, k)), pl.BlockSpec((tk2, tn), lambda i, j, k: (k, j)),
        [_sds((s, d), F32), _sds((s, d), BF16), _sds((gi, 1, d), F32)],
        [tile, tile, pl.BlockSpec((None, 1, tn), lambda i, j, k: (i, 0, j))], (tm, tn),
        extras=(x1, tgt), extra_specs=(tile, tile), epi=loss_epi)
    loss = lax.psum(0.5 * jnp.sum(loss_cols) / d, ("x", "y", "c"))

    (du,) = _mm("d_act", dyb, wg_ff2, NT, (gi, d_ff // tn, d // tk),
                pl.BlockSpec((tm, tk), lambda i, j, k: (i, k)), pl.BlockSpec((tn, tk), lambda i, j, k: (j, k)),
                [_sds((s, d_ff), BF16)], [tile], (tm, tn), extras=(u,), extra_specs=(tile,),
                epi=lambda acc, uu: (acc * (2.0 * jnp.maximum(uu, 0.0)),))
    ts = min(s, 2048)
    wtile = pl.BlockSpec((tn, tn), lambda i, j, k: (i, j))
    (g_ff2,) = _mm("dw_ff2", act, dyb, TN, (d_ff // tn, d // tn, s // ts),
                   pl.BlockSpec((ts, tn), lambda i, j, k: (k, i)), pl.BlockSpec((ts, tn), lambda i, j, k: (k, j)),
                   [_sds((d_ff, d), BF16)], [wtile], (tn, tn))
    g_ff2 = g_ff2.reshape(N_CHIPS, d_ff // N_CHIPS, d)
    (g_ff1,), ((got_ff2,),) = _mm(
        "dw_ff1", h2, du, TN, (d // tn, N_CHIPS * per, s // ts),
        pl.BlockSpec((ts, tn), lambda i, j, k: (k, i)), pl.BlockSpec((ts, tn_ff), lambda i, j, k: (k, j)),
        [_sds((N_CHIPS, d, ff_sh), BF16)],
        [pl.BlockSpec((None, tn, tn_ff), lambda i, j, k: (j // per, i, j % per))], (tn, tn_ff),
        riders=[_SwapRider([g_ff2])])
    p_ff2 = _add_half(g_ff2, got_ff2, c_arr, "chip_partial_w_ff2")
    tkf = min(tk, ff_sh)
    kper = ff_sh // tkf
    (dh2,), ((got2_ff2,), (got_ff1,)) = _mm(
        "d_h2", du, wg_ff1, NT, (gi, d // tn, d_ff // tkf),
        pl.BlockSpec((tm, tkf), lambda i, j, k: (i, k)),
        pl.BlockSpec((None, tn, tkf), lambda i, j, k: (k // kper, j, k % kper)),
        [_sds((s, d), F32)], [tile], (tm, tn),
        riders=[_SendPartialsRider([p_ff2], part=(0, 3, 4)), _SwapRider([g_ff1])])
    p_ff1 = _add_half(g_ff1, got_ff1, c_arr, "chip_partial_w_ff1")
    dx1, dx1b, g_norm2 = _rmsnorm_bwd(x1, norm2_g, dh2, dy, "rmsnorm2_bwd")

    (dmix,) = _mm("d_mix", dx1b, wg_out, NT, (gi, d // tn, d // tk),
                  pl.BlockSpec((tm, tk), lambda i, j, k: (i, k)), pl.BlockSpec((tn, tk), lambda i, j, k: (j, k)),
                  [_sds((s, d), F32)], [tile], (tm, tn))
    (g_out,) = _mm("dw_out", mix, dx1b, TN, (d // tn, d // tn, s // ts),
                   pl.BlockSpec((ts, tn), lambda i, j, k: (k, i)), pl.BlockSpec((ts, tn), lambda i, j, k: (k, j)),
                   [_sds((d, d), BF16)], [wtile], (tn, tn))
    g_out = g_out.reshape(N_CHIPS, d // N_CHIPS, d)
    (dproj, g_gn), ((got2_ff2,), (got2_ff1,), (got_out,)) = _retention_bwd(
        proj, ret_norm_g, tables, y_ret, prev, dmix,
        riders=[_SendPartialsRider([p_ff2], got=[got2_ff2], part=(3, 4, 4)), _SendPartialsRider([p_ff1], part=(0, 2, 4)),
                _SwapRider([g_out])])
    p_out = _add_half(g_out, got_out, c_arr, "chip_partial_w_out")
    (dproj, g_gq, g_gk, dbias), ((got2_ff1,), (got2_out,)) = _attention_bwd(
        proj, q_norm_g, k_norm_g, bias, dmix, dproj,
        riders=[_SendPartialsRider([p_ff1], got=[got2_ff1], part=(2, 4, 4)), _SendPartialsRider([p_out])])
    g_rel = _rel_bias_fold(dbias)
    names = ["w_in", "w_out", "w_ff1", "w_ff2"]
    kc_arr = jnp.concatenate([k_arr, c_arr])
    early = [_sum_partials(p, r, kc_arr, "sum_partials_" + nm)
             for p, r, nm in zip((p_out, p_ff1, p_ff2), (got2_out, got2_ff1, got2_ff2), names[1:])]
    (g_in,), (early,) = _mm(
        "dw_in", h1, dproj, TN, (d // tn, 2 * N_CHIPS, s // ts),
        pl.BlockSpec((ts, tn), lambda i, j, k: (k, i)), pl.BlockSpec((ts, tn_in), lambda i, j, k: (k, j)),
        [_sds((N_CHIPS, d, in_sh), BF16)],
        [pl.BlockSpec((None, tn, tn_in), lambda i, j, k: (j // 2, i, j % 2))], (tn, tn_in), riders=[_ShareRider(early)])
    ((got_in,),) = _run_riders("grad_swap_w_in", [_SwapRider([g_in])])
    p_in = _add_half(g_in, got_in, c_arr, "chip_partial_w_in")
    half_spec = pl.BlockSpec((None, tn, tn_in), lambda i, j, k: (k // 2, j, 0))
    (dh1,), ((got2_in,),) = _mm(
        "d_h1", dproj, [wg_in0, wg_in1], NT, (gi, d // tn, 2 * N_CHIPS),
        pl.BlockSpec((tm, tn_in), lambda i, j, k: (i, k)), [half_spec, half_spec],
        [_sds((s, d), F32)], [tile], (tm, tn), riders=[_SendPartialsRider([p_in])])
    grad_x, _, g_norm1 = _rmsnorm_bwd(xs, norm1_g, dh1, dx1, "rmsnorm1_bwd")

    small_w = (norm1_g, ret_norm_g, q_norm_g, k_norm_g, rel_bias, norm2_g)
    small_m = (m_norm1_g, m_ret_norm_g, m_q_norm_g, m_k_norm_g, m_rel_bias, m_norm2_g)
    small_v = (v_norm1_g, v_ret_norm_g, v_q_norm_g, v_k_norm_g, v_rel_bias, v_norm2_g)
    shapes = [p.shape for p in small_w]
    g_small = _pack_small([g_norm1, g_gn, g_gq, g_gk, g_rel, g_norm2])
    small_out, ((g_w_in,),) = _small_allreduce_adamw(
        g_small, _pack_small(small_w), _pack_small(small_m), _pack_small(small_v),
        riders=[_ShareRider([_sum_partials(p_in, got2_in, kc_arr, "sum_partials_w_in")])])
    sg, sd, sm, sv = (_unpack_small(a, shapes) for a in small_out)

    g_big = [g_w_in, *early]
    big = []
    for g, w, m, v, nm in zip(g_big, (w_in, w_out, w_ff1, w_ff2), (m_w_in, m_w_out, m_w_ff1, m_w_ff2),
                              (v_w_in, v_w_out, v_w_ff1, v_w_ff2), names):
        g, delta, new_m, new_v = _adamw(w[0], g, m[0], v[0], "adamw_" + nm)
        big.append((g[None], delta[None], new_m[None], new_v[None]))

    def ordered(kind):
        sm_ = (sg, sd, sm, sv)[kind]
        return (sm_[0], big[0][kind], sm_[1], sm_[2], sm_[3], sm_[4], big[1][kind], sm_[5], big[2][kind], big[3][kind])

    return (loss, grad_x[None], *ordered(0), *ordered(1), *ordered(2), *ordered(3))
```

```python
import functools

import jax
import jax.numpy as jnp
from jax import lax
from jax.experimental import pallas as pl
from jax.experimental.pallas import tpu as pltpu

F32 = jnp.float32
BF16 = jnp.bfloat16
MXU_DTYPE = jnp.bfloat16

CHUNK = 64
HEADS = 8
HEAD_DIM = 128
LEFT_CHUNKS = 8
BAND = (LEFT_CHUNKS + 1) * CHUNK
REL_CLIP = 128
REL_SIZE = (CHUNK - 1) + REL_CLIP + 1
RET_BLOCK_CHUNKS = 16
RET_ROWS = RET_BLOCK_CHUNKS * CHUNK
RET_SUB = 256
ROPE_BASE = 10000.0
EPS = 1e-6
GN_EPS = 1e-5
ADAM_LR, ADAM_B1, ADAM_B2, ADAM_EPS, ADAM_WD, ADAM_STEP = 0.001, 0.9, 0.999, 1e-08, 0.01, 10
N_CHIPS = 4
VMEM_LIMIT = 56 * 1024 * 1024
MESH = pl.DeviceIdType.MESH
ANY = pl.BlockSpec(memory_space=pl.ANY)

NN = (((1,), (0,)), ((), ()))
NT = (((1,), (1,)), ((), ()))
TN = (((0,), (0,)), ((), ()))


def _pallas(body, **kw):
    return pl.pallas_call(body, **kw)


def _params(*sem):
    return pltpu.CompilerParams(dimension_semantics=sem, vmem_limit_bytes=VMEM_LIMIT)


def _dot(a, b, dims):
    return lax.dot_general(a.astype(MXU_DTYPE), b.astype(MXU_DTYPE), dims, preferred_element_type=F32)


RIDER_MID, RIDER_LATE = 0.5, 0.8


def _mm(name, a, b, dims, grid, a_spec, b_spec, outs, o_specs, acc_shape, extras=(), extra_specs=(), epi=None,
        riders=(), through=None):
    ni, nj, nk = grid
    n_ex, n_out = len(extras), len(outs)
    bs = list(b) if isinstance(b, (list, tuple)) else [b]
    b_specs = list(b_spec) if isinstance(b, (list, tuple)) else [b_spec]
    extras, extra_specs = (*bs[1:], *extras), (*b_specs[1:], *extra_specs)
    b, b_spec, n_b = bs[0], b_specs[0], len(bs)
    n_in = 1 + n_b + n_ex
    rs = _Riders(riders, n_in, n_out)
    n_rin, n_rout = len(rs.arrays), len(rs.out_shapes)
    steps = ni * nj * nk

    held = [] if through is None else [through]

    def body(*refs):
        a_ref, b_refs = refs[0], refs[1:1 + n_b]
        b_ref = b_refs[0]
        ex_refs = refs[1 + n_b:n_in]
        outs_at = n_in + n_rin + len(held)
        o_refs = refs[outs_at:outs_at + n_out]
        acc_ref = refs[outs_at + n_out + n_rout]
        k = pl.program_id(2)
        if riders:
            bound = rs.bind(refs[n_in:n_in + n_rin], refs[outs_at + n_out:outs_at + n_out + n_rout],
                            refs[outs_at + n_out + n_rout + 1:])
            step = (pl.program_id(0) * nj + pl.program_id(1)) * nk + k
            rs.hooks(step, steps, bound)

        def finish(acc):
            vals = epi(acc, *[r[...] for r in ex_refs]) if epi is not None else (acc,)
            for r, v in zip(o_refs, vals):
                r[...] = v.astype(r.dtype)

        if nk == 1:
            finish(_dot(a_ref[...], b_ref[...], dims))
        else:
            @pl.when(k == 0)
            def _():
                acc_ref[...] = jnp.zeros_like(acc_ref)

            for t, ref in enumerate(b_refs):
                def step_with(ref=ref):
                    acc_ref[...] += _dot(a_ref[...], ref[...], dims)

                if n_b == 1:
                    step_with()
                else:
                    pl.when(k % n_b == t)(step_with)
            pl.when(k == nk - 1)(lambda: finish(acc_ref[...]))

        if riders:
            pl.when(step == steps - 1)(lambda: rs.run("end", bound))

    res = _pallas(
        body, name=name, grid=grid, in_specs=[a_spec, b_spec, *extra_specs, *rs.in_specs, *[ANY for _ in held]],
        out_specs=[*o_specs, *rs.out_specs], out_shape=[*outs, *rs.out_shapes],
        input_output_aliases={**rs.aliases, **{n_in + n_rin: 0 for _ in held}},
        scratch_shapes=[pltpu.VMEM(acc_shape if nk > 1 else (8, 128), F32), *rs.scratch],
        compiler_params=_params(*(("arbitrary",) * 3 if riders else ("parallel", "parallel", "arbitrary"))),
    )(a, b, *extras, *rs.arrays, *held)
    return (res[:n_out], rs.split(res[n_out:])) if riders else res


def _sds(shape, dtype):
    return jax.ShapeDtypeStruct(shape, dtype)


def _cast_bf16(w, k_arr, name, cols=(0, 1)):
    r, c = w.shape[0], w.shape[1] // cols[1]
    tr = min(r, 512)

    def body(k_ref, w_ref, o_ref):
        o_ref[...] = w_ref[...].astype(BF16)

    grid_spec = pltpu.PrefetchScalarGridSpec(
        num_scalar_prefetch=1, grid=(r // tr,), in_specs=[pl.BlockSpec((tr, c), lambda i, k_ref: (i, cols[0]))],
        out_specs=pl.BlockSpec((None, tr, c), lambda i, k_ref: (k_ref[0], i, 0)))
    return _pallas(body, name=name, grid_spec=grid_spec, out_shape=_sds((N_CHIPS, r, c), BF16),
                   compiler_params=_params("parallel"))(k_arr, w)


def _rmsnorm_fwd(x, g, name, riders=()):
    s, d = x.shape
    tr = 512

    def body(x_ref, g_ref, o_ref):
        xv = x_ref[...]
        y = xv * lax.rsqrt(jnp.mean(xv * xv, axis=-1, keepdims=True) + EPS)
        o_ref[...] = (y * g_ref[...]).astype(o_ref.dtype)

    rs = _Riders(riders, 2, 1)
    out = _pallas(_with_riders(body, 2, 1, 0, rs, (s // tr,)), name=name, grid=(s // tr,),
                  in_specs=[pl.BlockSpec((tr, d), lambda i: (i, 0)), pl.BlockSpec((1, d), lambda i: (0, 0)), *rs.in_specs],
                  out_specs=[pl.BlockSpec((tr, d), lambda i: (i, 0)), *rs.out_specs],
                  out_shape=[_sds((s, d), BF16), *rs.out_shapes], input_output_aliases=rs.aliases,
                  scratch_shapes=rs.scratch, compiler_params=_params("arbitrary"))(x, g, *rs.arrays)
    return out[0], rs.split(out[1:])


def _rmsnorm_bwd(x, g, dh, res, name, riders=()):
    s, d = x.shape
    tr = 512

    def body(x_ref, g_ref, dh_ref, res_ref, dx_ref, dxb_ref, dg_ref):
        i = pl.program_id(0)
        xv = x_ref[...]
        rstd = lax.rsqrt(jnp.mean(xv * xv, axis=-1, keepdims=True) + EPS)
        xh = xv * rstd
        dhv = dh_ref[...]

        @pl.when(i == 0)
        def _():
            dg_ref[...] = jnp.zeros_like(dg_ref)

        dg_ref[...] += jnp.sum(dhv * xh, axis=0, keepdims=True)
        dxh = dhv * g_ref[...]
        dx = res_ref[...] + rstd * (dxh - xh * jnp.mean(dxh * xh, axis=-1, keepdims=True))
        dx_ref[...] = dx
        dxb_ref[...] = dx.astype(BF16)

    row = pl.BlockSpec((tr, d), lambda i: (i, 0))
    vec = pl.BlockSpec((1, d), lambda i: (0, 0))
    rs = _Riders(riders, 4, 3)
    out = _pallas(_with_riders(body, 4, 3, 0, rs, (s // tr,)), name=name, grid=(s // tr,),
                  in_specs=[row, vec, row, row, *rs.in_specs], out_specs=[row, row, vec, *rs.out_specs],
                  out_shape=[_sds((s, d), F32), _sds((s, d), BF16), _sds((1, d), F32), *rs.out_shapes],
                  input_output_aliases=rs.aliases, scratch_shapes=rs.scratch,
                  compiler_params=_params("arbitrary"))(x, g, dh, res, *rs.arrays)
    return (out[:3], rs.split(out[3:])) if riders else out


def _adamw_math(w, g, m, v):
    m = ADAM_B1 * m + (1.0 - ADAM_B1) * g
    v = ADAM_B2 * v + (1.0 - ADAM_B2) * (g * g)
    m_hat = m / (1.0 - ADAM_B1 ** ADAM_STEP)
    v_hat = v / (1.0 - ADAM_B2 ** ADAM_STEP)
    delta = -ADAM_LR * (m_hat / (jnp.sqrt(v_hat) + ADAM_EPS) + ADAM_WD * w)
    return delta, m, v


def _adamw(w, g, m, v, name):
    r, c = w.shape
    tr = 256

    def body(w_ref, g_ref, m_ref, v_ref, go_ref, d_ref, nm_ref, nv_ref):
        g = g_ref[...]
        go_ref[...] = g
        d_ref[...], nm_ref[...], nv_ref[...] = _adamw_math(w_ref[...], g, m_ref[...], v_ref[...])

    blk = pl.BlockSpec((tr, c), lambda i: (i, 0))
    return _pallas(body, name=name, grid=(r // tr,), in_specs=[blk] * 4, out_specs=[blk] * 4,
                   out_shape=[_sds((r, c), F32)] * 4, compiler_params=_params("parallel"))(w, g, m, v)


def _tables(s):
    half = HEAD_DIM // 2
    pos = jnp.arange(s, dtype=F32)
    inv_freq = ROPE_BASE ** (-jnp.arange(half, dtype=F32) / half)
    ang = pos[:, None] * inv_freq[None, :]
    cos, sin = jnp.cos(ang), jnp.sin(ang)
    cos_f = jnp.concatenate([cos, cos], axis=-1)
    sin_f = jnp.concatenate([-sin, sin], axis=-1)
    log_g = jnp.log1p(-jnp.exp2(-(5.0 + jnp.arange(HEADS, dtype=F32))))
    p = jnp.arange(CHUNK, dtype=F32)
    decay = jnp.exp(log_g[:, None, None] * jnp.abs(p[:, None] - p[None, :]))
    k_dec = jnp.exp(log_g[None, :] * (CHUNK - 1.0 - p)[:, None])
    q_dec = jnp.exp(log_g[None, :] * (p + 1.0)[:, None])
    c_dec = jnp.exp(log_g * CHUNK)
    k_dec = jnp.tile(jnp.broadcast_to(k_dec.T[:, :, None], (HEADS, CHUNK, HEAD_DIM)), (1, RET_BLOCK_CHUNKS, 1))
    q_dec = jnp.tile(jnp.broadcast_to(q_dec.T[:, :, None], (HEADS, CHUNK, HEAD_DIM)), (1, RET_BLOCK_CHUNKS, 1))
    c_dec = jnp.broadcast_to(c_dec[:, None, None], (HEADS, 1, HEAD_DIM))
    n = RET_SUB // CHUNK
    decay = (jnp.eye(n, dtype=F32)[None, :, None, :, None] * decay[:, None, :, None, :]).reshape(HEADS, RET_SUB, RET_SUB)
    return cos_f, sin_f, decay, k_dec, q_dec, c_dec


def _rot(x, cos_f, sin_f):
    return x * cos_f + pltpu.roll(x, HEAD_DIM // 2, 1) * sin_f


def _rot_bwd(d, cos_f, sin_f):
    return d * cos_f + pltpu.roll(d * sin_f, HEAD_DIM // 2, 1)


K_SCALE = HEAD_DIM ** -0.5


def _retention_fwd(proj, gn_g, tables, riders=()):
    s = proj.shape[0]
    nb = s // RET_ROWS
    nc = s // CHUNK
    cos_f, sin_f, decay, k_dec, q_dec, c_dec = tables

    def body(q_ref, k_ref, v_ref, g_ref, cos_ref, sin_ref, dec_ref, kd_ref, qd_ref, cd_ref, gn_ref,
             ret_ref, y_ref, prev_ref, state_ref):
        @pl.when(pl.program_id(1) == 0)
        def _():
            state_ref[...] = jnp.zeros_like(state_ref)

        cosv, sinv = cos_ref[...], sin_ref[...]
        q = _rot(q_ref[...], cosv, sinv)
        k = _rot(k_ref[...], cosv, sinv) * K_SCALE
        v = v_ref[...]
        rg = g_ref[...]
        dec, cd, gn = dec_ref[...], cd_ref[...], gn_ref[...]
        kdf, qdf = k * kd_ref[...], q * qd_ref[...]
        chunks = [slice(c * CHUNK, (c + 1) * CHUNK) for c in range(RET_BLOCK_CHUNKS)]
        contribs = [_dot(kdf[rows], v[rows], TN) for rows in chunks]
        state, states = state_ref[...], []
        for c in range(RET_BLOCK_CHUNKS):
            states.append(state)
            prev_ref[c] = state.astype(prev_ref.dtype)
            state = cd * state + contribs[c]
        state_ref[...] = state
        cross = jnp.concatenate([_dot(qdf[rows], st, NN) for rows, st in zip(chunks, states)], axis=0)
        intra = []
        for b in range(RET_ROWS // RET_SUB):
            rows = slice(b * RET_SUB, (b + 1) * RET_SUB)
            intra.append(_dot(_dot(q[rows], k[rows], NT) * dec, v[rows], NN))
        y = jnp.concatenate(intra, axis=0) + cross
        y_ref[...] = y
        mu = jnp.mean(y, axis=-1, keepdims=True)
        yc = y - mu
        var = jnp.mean(yc * yc, axis=-1, keepdims=True)
        yn = yc * lax.rsqrt(var + GN_EPS) * gn
        ret_ref[...] = (rg * jax.nn.sigmoid(rg) * yn).astype(ret_ref.dtype)

    def col(off):
        return pl.BlockSpec((RET_ROWS, HEAD_DIM), lambda h, i: (i, off + h))

    pos = pl.BlockSpec((RET_ROWS, HEAD_DIM), lambda h, i: (i, 0))
    per_head = lambda shape: pl.BlockSpec((None, *shape), lambda h, i: (h, 0, 0))
    rs = _Riders(riders, 11, 3)
    res = _pallas(
        _with_riders(body, 11, 3, 1, rs, (HEADS, nb)), name="retention_fwd", grid=(HEADS, nb),
        in_specs=[col(0), col(HEADS), col(2 * HEADS), col(3 * HEADS), pos, pos,
                  per_head((RET_SUB, RET_SUB)), per_head((RET_ROWS, HEAD_DIM)), per_head((RET_ROWS, HEAD_DIM)),
                  per_head((1, HEAD_DIM)), pl.BlockSpec((1, HEAD_DIM), lambda h, i: (0, h)), *rs.in_specs],
        out_specs=[col(0), col(0),
                   pl.BlockSpec((None, RET_BLOCK_CHUNKS, HEAD_DIM, HEAD_DIM), lambda h, i: (h, i, 0, 0)),
                   *rs.out_specs],
        out_shape=[_sds((s, 2 * HEADS * HEAD_DIM), BF16), _sds((s, HEADS * HEAD_DIM), F32),
                   _sds((HEADS, nc, HEAD_DIM, HEAD_DIM), MXU_DTYPE), *rs.out_shapes],
        input_output_aliases=rs.aliases,
        scratch_shapes=[pltpu.VMEM((HEAD_DIM, HEAD_DIM), F32), *rs.scratch],
        compiler_params=_params("arbitrary", "arbitrary"),
    )(proj, proj, proj, proj, cos_f, sin_f, decay, k_dec, q_dec, c_dec, gn_g, *rs.arrays)
    return res[:3], rs.split(res[3:])


def _retention_bwd(proj, gn_g, tables, y, prev, dmix, riders=()):
    s = proj.shape[0]
    nb = s // RET_ROWS
    cos_f, sin_f, decay, k_dec, q_dec, c_dec = tables

    def body(q_ref, k_ref, v_ref, g_ref, cos_ref, sin_ref, dec_ref, kd_ref, qd_ref, cd_ref, gn_ref,
             y_ref, prev_ref, dret_ref, dproj_ref, dgn_ref, gstate_ref, stage_ref, stage_sems):
        head, blk = pl.program_id(0), pl.program_id(1)
        step = head * nb + blk
        slot = step % 2

        def writes(sl):
            rows = pl.ds(pl.multiple_of((nb - 1 - blk) * RET_ROWS, RET_ROWS), RET_ROWS)
            return [pltpu.make_async_copy(
                stage_ref.at[sl, g], dproj_ref.at[rows, pl.ds(pl.multiple_of((g * HEADS + head) * HEAD_DIM, HEAD_DIM), HEAD_DIM)],
                stage_sems.at[sl, g]) for g in range(4)]

        @pl.when(step >= 2)
        def _():
            for cp in writes(slot):
                cp.wait()

        @pl.when(blk == 0)
        def _():
            gstate_ref[...] = jnp.zeros_like(gstate_ref)
            dgn_ref[...] = jnp.zeros_like(dgn_ref)

        cosv, sinv = cos_ref[...], sin_ref[...]
        q = _rot(q_ref[...], cosv, sinv)
        k = _rot(k_ref[...], cosv, sinv) * K_SCALE
        v = v_ref[...]
        dec, kd, qd, cd, gn = dec_ref[...], kd_ref[...], qd_ref[...], cd_ref[...], gn_ref[...]
        kdf, qdf = k * kd, q * qd
        rg = g_ref[...]
        yv = y_ref[...]
        dret = dret_ref[...]
        sig = jax.nn.sigmoid(rg)
        gate = rg * sig
        mu = jnp.mean(yv, axis=-1, keepdims=True)
        yc = yv - mu
        rstd = lax.rsqrt(jnp.mean(yc * yc, axis=-1, keepdims=True) + GN_EPS)
        z = yc * rstd
        dyn = dret * gate
        stage_ref[slot, 3] = (dret * (z * gn) * (sig * (1.0 + rg * (1.0 - sig)))).astype(stage_ref.dtype)
        dgn_ref[...] += jnp.sum(dyn * z, axis=0, keepdims=True)
        dz = dyn * gn
        dy = rstd * (dz - jnp.mean(dz, axis=-1, keepdims=True) - z * jnp.mean(dz * z, axis=-1, keepdims=True))
        chunks = [slice(c * CHUNK, (c + 1) * CHUNK) for c in range(RET_BLOCK_CHUNKS)]
        dprevs = [_dot(qdf[rows], dy[rows], TN) for rows in chunks]
        gst, gsts = gstate_ref[...], [None] * RET_BLOCK_CHUNKS
        for c in reversed(range(RET_BLOCK_CHUNKS)):
            gsts[c] = gst
            gst = dprevs[c] + cd * gst
        gstate_ref[...] = gst
        dq = jnp.concatenate([_dot(dy[rows], prev_ref[c], NT) for c, rows in enumerate(chunks)], axis=0) * qd
        dk = jnp.concatenate([_dot(v[rows], g, NT) for rows, g in zip(chunks, gsts)], axis=0) * kd
        dv = jnp.concatenate([_dot(kdf[rows], g, NN) for rows, g in zip(chunks, gsts)], axis=0)
        dqi, dki, dvi = [], [], []
        for b in range(RET_ROWS // RET_SUB):
            rows = slice(b * RET_SUB, (b + 1) * RET_SUB)
            qs, ks, vs, dys = q[rows], k[rows], v[rows], dy[rows]
            dvi.append(_dot(_dot(ks, qs, NT) * dec, dys, NN))
            dqi.append(_dot(_dot(dys, vs, NT) * dec, ks, NN))
            dki.append(_dot(_dot(vs, dys, NT) * dec, qs, NN))
        dq = dq + jnp.concatenate(dqi, axis=0)
        dk = dk + jnp.concatenate(dki, axis=0)
        dv = dv + jnp.concatenate(dvi, axis=0)
        stage_ref[slot, 0] = _rot_bwd(dq, cosv, sinv).astype(stage_ref.dtype)
        stage_ref[slot, 1] = _rot_bwd(dk * K_SCALE, cosv, sinv).astype(stage_ref.dtype)
        stage_ref[slot, 2] = dv.astype(stage_ref.dtype)
        for cp in writes(slot):
            cp.start()

        @pl.when(step == HEADS * nb - 1)
        def _():
            for cp in writes(1 - slot) + writes(slot):
                cp.wait()

    rev = lambda i: nb - 1 - i

    def col(off):
        return pl.BlockSpec((RET_ROWS, HEAD_DIM), lambda h, i: (rev(i), off + h))

    pos = pl.BlockSpec((RET_ROWS, HEAD_DIM), lambda h, i: (rev(i), 0))
    per_head = lambda shape: pl.BlockSpec((None, *shape), lambda h, i: (h, 0, 0))
    rs = _Riders(riders, 14, 2)
    res = _pallas(
        _with_riders(body, 14, 2, 3, rs, (HEADS, nb)), name="retention_bwd", grid=(HEADS, nb),
        in_specs=[col(0), col(HEADS), col(2 * HEADS), col(3 * HEADS), pos, pos,
                  per_head((RET_SUB, RET_SUB)), per_head((RET_ROWS, HEAD_DIM)), per_head((RET_ROWS, HEAD_DIM)),
                  per_head((1, HEAD_DIM)), pl.BlockSpec((1, HEAD_DIM), lambda h, i: (0, h)),
                  col(0), pl.BlockSpec((None, RET_BLOCK_CHUNKS, HEAD_DIM, HEAD_DIM), lambda h, i: (h, rev(i), 0, 0)),
                  col(0), *rs.in_specs],
        out_specs=[ANY, per_head((1, HEAD_DIM)), *rs.out_specs],
        out_shape=[_sds((s, proj.shape[1]), BF16), _sds((HEADS, 1, HEAD_DIM), F32), *rs.out_shapes],
        input_output_aliases=rs.aliases,
        scratch_shapes=[pltpu.VMEM((HEAD_DIM, HEAD_DIM), F32), pltpu.VMEM((2, 4, RET_ROWS, HEAD_DIM), BF16),
                        pltpu.SemaphoreType.DMA((2, 4)), *rs.scratch],
        compiler_params=_params("arbitrary", "arbitrary"),
    )(proj, proj, proj, proj, cos_f, sin_f, decay, k_dec, q_dec, c_dec, gn_g, y, prev, dmix, *rs.arrays)
    return res[:2], rs.split(res[2:])


ATT_COL0 = 4 * HEADS
PAD_ROWS = LEFT_CHUNKS * CHUNK
NORM_ROWS = 512
GROUP_CHUNKS = 4
GROUP = GROUP_CHUNKS * CHUNK
WIN = (LEFT_CHUNKS + GROUP_CHUNKS) * CHUNK
MASKED = -1e30


def _qk_norm(x, g):
    return x * lax.rsqrt(jnp.mean(x * x, axis=-1, keepdims=True) + EPS) * g


def _band_probs(qb, kb, bias, g):
    sc = _dot(qb, kb, NT) * K_SCALE + bias
    win_chunk = lax.broadcasted_iota(jnp.int32, (GROUP, WIN), 1) // CHUNK
    sc = jnp.where(g * GROUP_CHUNKS - LEFT_CHUNKS + win_chunk >= 0, sc, MASKED)
    e = jnp.exp(sc - jnp.max(sc, axis=-1, keepdims=True))
    return e / jnp.sum(e, axis=-1, keepdims=True)


def _with_riders(core, n_in, n_out, n_scratch, rs, grid):
    n_rin, n_rout = len(rs.arrays), len(rs.out_shapes)
    if not rs.riders:
        return core
    steps = 1
    for n in grid:
        steps *= n

    def body(*refs):
        outs_at = n_in + n_rin
        scratch_at = outs_at + n_out + n_rout
        bound = rs.bind(refs[n_in:outs_at], refs[outs_at + n_out:scratch_at], refs[scratch_at + n_scratch:])
        step = 0
        for axis, n in enumerate(grid):
            step = step * n + pl.program_id(axis)
        rs.hooks(step, steps, bound)
        core(*refs[:n_in], *refs[outs_at:outs_at + n_out], *refs[scratch_at:scratch_at + n_scratch])
        pl.when(step == steps - 1)(lambda: rs.run("end", bound))

    return body


def _attention_fwd(proj, gq, gk, bias, mix, riders=()):
    s = proj.shape[0]
    rs = _Riders(riders, 7, 3)

    def body(q_ref, k_ref, v_ref, gq_ref, gk_ref, bias_ref, mix_ref, o_ref, kp_ref, vp_ref):
        kp_ref[0:PAD_ROWS, :] = jnp.zeros((PAD_ROWS, HEAD_DIM), kp_ref.dtype)
        vp_ref[0:PAD_ROWS, :] = jnp.zeros((PAD_ROWS, HEAD_DIM), vp_ref.dtype)
        gqv, gkv = gq_ref[...], gk_ref[...]

        def fill(b, carry):
            r0 = pl.multiple_of(b * NORM_ROWS, NORM_ROWS)
            kp_ref[pl.ds(PAD_ROWS + r0, NORM_ROWS), :] = _qk_norm(k_ref[pl.ds(r0, NORM_ROWS), :], gkv).astype(kp_ref.dtype)
            vp_ref[pl.ds(PAD_ROWS + r0, NORM_ROWS), :] = v_ref[pl.ds(r0, NORM_ROWS), :].astype(vp_ref.dtype)
            return carry

        lax.fori_loop(0, s // NORM_ROWS, fill, 0, unroll=2)

        def group(g, carry):
            r0 = pl.multiple_of(g * GROUP, GROUP)
            qn = _qk_norm(q_ref[pl.ds(r0, GROUP), :], gqv)
            p = _band_probs(qn, kp_ref[pl.ds(r0, WIN), :], bias_ref[...], g)
            o_ref[pl.ds(r0, GROUP), :] = _dot(p, vp_ref[pl.ds(r0, WIN), :], NN).astype(o_ref.dtype)
            return carry

        lax.fori_loop(0, s // GROUP, group, 0, unroll=8)

    def col(off):
        return pl.BlockSpec((s, HEAD_DIM), lambda h: (0, off + h))

    vec = pl.BlockSpec((1, HEAD_DIM), lambda h: (0, 0))
    padded = pl.BlockSpec((None, s + PAD_ROWS, HEAD_DIM), lambda h: (h, 0, 0))
    padded_shape = _sds((HEADS, s + PAD_ROWS, HEAD_DIM), MXU_DTYPE)
    res = _pallas(
        _with_riders(body, 7, 3, 0, rs, (HEADS,)), name="attention_fwd", grid=(HEADS,),
        in_specs=[col(ATT_COL0), col(ATT_COL0 + HEADS), col(ATT_COL0 + 2 * HEADS), vec, vec,
                  pl.BlockSpec((None, GROUP, WIN), lambda h: (h, 0, 0)), ANY, *rs.in_specs],
        out_specs=[col(HEADS), padded, padded, *rs.out_specs],
        out_shape=[_sds(mix.shape, mix.dtype), padded_shape, padded_shape, *rs.out_shapes],
        input_output_aliases={6: 0, **rs.aliases}, scratch_shapes=rs.scratch,
        compiler_params=_params("arbitrary"),
    )(proj, proj, proj, gq, gk, bias, mix, *rs.arrays)
    return res[:3], rs.split(res[3:])


def _attention_bwd(proj, gq, gk, bias, dmix, dproj, kp, vp, riders=()):
    s = proj.shape[0]
    rs = _Riders(riders, 10, 4)

    def body(q_ref, k_ref, v_ref, gq_ref, gk_ref, bias_ref, do_ref, dproj_in_ref, kp_ref, vp_ref,
             dproj_ref, dgq_ref, dgk_ref, dbias_ref, dkp_ref, dvp_ref, dqn_ref, stage_ref, stage_sems):
        head = pl.program_id(0)

        def writes():
            return [pltpu.make_async_copy(
                stage_ref.at[g],
                dproj_ref.at[:, pl.ds(pl.multiple_of((ATT_COL0 + g * HEADS + head) * HEAD_DIM, HEAD_DIM), HEAD_DIM)],
                stage_sems.at[g]) for g in range(3)]

        dkp_ref[...] = jnp.zeros_like(dkp_ref)
        dvp_ref[...] = jnp.zeros_like(dvp_ref)
        dbias_ref[...] = jnp.zeros_like(dbias_ref)
        gqv, gkv = gq_ref[...], gk_ref[...]

        def group(g, carry):
            r0 = pl.multiple_of(g * GROUP, GROUP)
            qn = _qk_norm(q_ref[pl.ds(r0, GROUP), :], gqv)
            kb = kp_ref[pl.ds(r0, WIN), :]
            vb = vp_ref[pl.ds(r0, WIN), :]
            p = _band_probs(qn, kb, bias_ref[...], g)
            do = do_ref[pl.ds(r0, GROUP), :]
            dvp_ref[pl.ds(r0, WIN), :] += _dot(p, do, TN)
            dp = _dot(do, vb, NT)
            ds = p * (dp - jnp.sum(dp * p, axis=-1, keepdims=True))
            dbias_ref[...] += ds
            dss = ds * K_SCALE
            dqn_ref[pl.ds(r0, GROUP), :] = _dot(dss, kb, NN)
            dkp_ref[pl.ds(r0, WIN), :] += _dot(dss, qn, TN)
            return carry

        lax.fori_loop(0, s // GROUP, group, 0, unroll=8)

        @pl.when(head == 0)
        def _():
            dgq_ref[...] = jnp.zeros_like(dgq_ref)
            dgk_ref[...] = jnp.zeros_like(dgk_ref)

        @pl.when(head > 0)
        def _():
            for cp in writes():
                cp.wait()

        def norm_bwd(x, g, dn):
            rstd = lax.rsqrt(jnp.mean(x * x, axis=-1, keepdims=True) + EPS)
            xh = x * rstd
            dxh = dn * g
            return rstd * (dxh - xh * jnp.mean(dxh * xh, axis=-1, keepdims=True)), jnp.sum(dn * xh, axis=0, keepdims=True)

        def finish(b, carry):
            r0 = pl.multiple_of(b * NORM_ROWS, NORM_ROWS)
            rows = pl.ds(r0, NORM_ROWS)
            dq, dgq = norm_bwd(q_ref[rows, :], gqv, dqn_ref[rows, :])
            dk, dgk = norm_bwd(k_ref[rows, :], gkv, dkp_ref[pl.ds(PAD_ROWS + r0, NORM_ROWS), :])
            stage_ref[0, rows, :] = dq.astype(stage_ref.dtype)
            stage_ref[1, rows, :] = dk.astype(stage_ref.dtype)
            stage_ref[2, rows, :] = dvp_ref[pl.ds(PAD_ROWS + r0, NORM_ROWS), :].astype(stage_ref.dtype)
            dgq_ref[...] += dgq
            dgk_ref[...] += dgk
            return carry

        lax.fori_loop(0, s // NORM_ROWS, finish, 0, unroll=2)
        for cp in writes():
            cp.start()

        @pl.when(head == HEADS - 1)
        def _():
            for cp in writes():
                cp.wait()

    def col(off):
        return pl.BlockSpec((s, HEAD_DIM), lambda h: (0, off + h))

    vec = pl.BlockSpec((1, HEAD_DIM), lambda h: (0, 0))
    hbias = pl.BlockSpec((None, GROUP, WIN), lambda h: (h, 0, 0))
    res = _pallas(
        _with_riders(body, 10, 4, 5, rs, (HEADS,)), name="attention_bwd", grid=(HEADS,),
        in_specs=[col(ATT_COL0), col(ATT_COL0 + HEADS), col(ATT_COL0 + 2 * HEADS), vec, vec, hbias, col(HEADS), ANY,
                  pl.BlockSpec((None, s + PAD_ROWS, HEAD_DIM), lambda h: (h, 0, 0)),
                  pl.BlockSpec((None, s + PAD_ROWS, HEAD_DIM), lambda h: (h, 0, 0)), *rs.in_specs],
        out_specs=[ANY, vec, vec, hbias, *rs.out_specs],
        out_shape=[_sds(dproj.shape, dproj.dtype), _sds((1, HEAD_DIM), F32), _sds((1, HEAD_DIM), F32),
                   _sds((HEADS, GROUP, WIN), F32), *rs.out_shapes],
        input_output_aliases={7: 0, **rs.aliases},
        scratch_shapes=[pltpu.VMEM((s + PAD_ROWS, HEAD_DIM), F32), pltpu.VMEM((s + PAD_ROWS, HEAD_DIM), F32),
                        pltpu.VMEM((s, HEAD_DIM), F32), pltpu.VMEM((3, s, HEAD_DIM), BF16),
                        pltpu.SemaphoreType.DMA((3,)), *rs.scratch],
        compiler_params=_params("arbitrary"),
    )(proj, proj, proj, gq, gk, bias, dmix, dproj, kp, vp, *rs.arrays)
    return res[:4], rs.split(res[4:])


DIAG_SPLIT = (BAND + WIN - CHUNK) // 2


def _diag_bin(m):
    t = jnp.where(m < DIAG_SPLIT, m, m - WIN)
    return jnp.clip(LEFT_CHUNKS * CHUNK - t, -(CHUNK - 1), REL_CLIP) + (CHUNK - 1)


def _skew_rows(a, left):
    row = lax.broadcasted_iota(jnp.int32, (GROUP, WIN), 0)
    for b in range(GROUP.bit_length() - 1):
        step = 1 << b
        a = jnp.where(jnp.bitwise_and(row, step) != 0, pltpu.roll(a, WIN - step if left else step, 1), a)
    return a


def _rel_bias_expand(rel_bias):
    def body(rb_ref, o_ref):
        onehot = (lax.broadcasted_iota(jnp.int32, (REL_SIZE, WIN), 0)
                  == _diag_bin(lax.broadcasted_iota(jnp.int32, (REL_SIZE, WIN), 1))).astype(MXU_DTYPE)
        rest = jnp.broadcast_to(rb_ref[...], (8, REL_SIZE))
        per_diag = jnp.zeros((8, WIN), F32)
        for _ in range(3):
            piece = rest.astype(BF16)
            per_diag = per_diag + _dot(piece, onehot, NN)
            rest = rest - piece.astype(F32)
        table = _skew_rows(jnp.broadcast_to(per_diag[0:1], (GROUP, WIN)), left=False)
        row_chunk = lax.broadcasted_iota(jnp.int32, (GROUP, WIN), 0) // CHUNK
        col_chunk = lax.broadcasted_iota(jnp.int32, (GROUP, WIN), 1) // CHUNK
        in_band = jnp.logical_and(col_chunk >= row_chunk, col_chunk <= row_chunk + LEFT_CHUNKS)
        o_ref[...] = jnp.where(in_band, table, MASKED)

    return _pallas(body, name="rel_bias_expand", grid=(HEADS,),
                   in_specs=[pl.BlockSpec((None, 1, REL_SIZE), lambda h: (h, 0, 0))],
                   out_specs=pl.BlockSpec((None, GROUP, WIN), lambda h: (h, 0, 0)),
                   out_shape=_sds((HEADS, GROUP, WIN), F32), compiler_params=_params("parallel"))(rel_bias)


def _rel_bias_fold(dbias):
    def body(a_ref, o_ref):
        diag = jnp.sum(_skew_rows(a_ref[...], left=True), axis=0, keepdims=True)
        onehot = (_diag_bin(lax.broadcasted_iota(jnp.int32, (WIN, REL_SIZE), 0))
                  == lax.broadcasted_iota(jnp.int32, (WIN, REL_SIZE), 1)).astype(MXU_DTYPE)
        rest = jnp.broadcast_to(diag, (8, WIN))
        out = jnp.zeros((8, REL_SIZE), F32)
        for _ in range(3):
            piece = rest.astype(BF16)
            out = out + _dot(piece, onehot, NN)
            rest = rest - piece.astype(F32)
        o_ref[...] = out[0:1]

    return _pallas(body, name="rel_bias_fold", grid=(HEADS,),
                   in_specs=[pl.BlockSpec((None, GROUP, WIN), lambda h: (h, 0, 0))],
                   out_specs=pl.BlockSpec((None, 1, REL_SIZE), lambda h: (h, 0, 0)),
                   out_shape=_sds((HEADS, 1, REL_SIZE), F32), compiler_params=_params("parallel"))(dbias)


def _place():
    return lax.axis_index("x"), lax.axis_index("y"), lax.axis_index("c")


def _other_chips(x, y):
    return [(1 - x, y), (x, 1 - y), (1 - x, 1 - y)]


class _Rider:
    reads, ins, new, n_sems = (), (), (), 1

    def schedule(self):
        return [(0.0, self.start), (RIDER_MID, self.mid), (RIDER_LATE, self.late)]

    def start(self, reads, ins, new, send, recv):
        pass

    def mid(self, reads, ins, new, send, recv):
        pass

    def late(self, reads, ins, new, send, recv):
        pass

    def end(self, reads, ins, new, send, recv):
        pass


class _Riders:
    def __init__(self, riders, n_host_in, n_host_out):
        self.riders = list(riders)
        self.arrays, self.out_shapes, self.aliases, self.scratch = [], [], {}, []
        for r in self.riders:
            for t, a in enumerate(r.ins):
                self.aliases[n_host_in + len(self.arrays) + len(r.reads) + t] = n_host_out + len(self.out_shapes) + t
            self.arrays += [*r.reads, *r.ins]
            self.out_shapes += [_sds(a.shape, a.dtype) for a in r.ins] + list(r.new)
            self.scratch += [pltpu.SemaphoreType.DMA((r.n_sems,)), pltpu.SemaphoreType.DMA((r.n_sems,))]
        self.in_specs = [ANY] * len(self.arrays)
        self.out_specs = [ANY] * len(self.out_shapes)

    def bind(self, in_refs, out_refs, scratch_refs):
        bound, i, o = [], 0, 0
        for t, r in enumerate(self.riders):
            reads = in_refs[i:i + len(r.reads)]
            i += len(r.reads) + len(r.ins)
            ins = out_refs[o:o + len(r.ins)]
            new = out_refs[o + len(r.ins):o + len(r.ins) + len(r.new)]
            o += len(r.ins) + len(r.new)
            bound.append((reads, ins, new, scratch_refs[2 * t], scratch_refs[2 * t + 1]))
        return bound

    def run(self, phase, bound):
        for r, b in zip(self.riders, bound):
            getattr(r, phase)(*b)

    def hooks(self, step, steps, bound):
        at = {}
        for r, b in zip(self.riders, bound):
            for frac, action in r.schedule():
                at.setdefault(min(int(steps * frac), steps - 1), []).append((action, b))
        for at_step, todo in sorted(at.items()):
            def go(todo=todo):
                for action, b in todo:
                    action(*b)

            pl.when(step == at_step)(go)

    def split(self, outs):
        res, o = [], 0
        for r in self.riders:
            n = len(r.ins) + len(r.new)
            res.append(list(outs[o:o + n]))
            o += n
        return res


def _run_riders(name, riders):
    rs = _Riders(riders, 0, 0)
    n_in, n_out = len(rs.arrays), len(rs.out_shapes)

    def body(*refs):
        bound = rs.bind(refs[:n_in], refs[n_in:n_in + n_out], refs[n_in + n_out:])
        rs.run("start", bound)
        rs.run("mid", bound)
        rs.run("late", bound)
        rs.run("end", bound)

    outs = _pallas(body, name=name, in_specs=rs.in_specs, out_specs=rs.out_specs, out_shape=rs.out_shapes,
                   input_output_aliases=rs.aliases, scratch_shapes=rs.scratch)(*rs.arrays)
    return rs.split(outs)


class _GatherRider(_Rider):
    X_LINK, Y_LINK, Y_PASS, X_PASS, D2D_X, D2D_Y, D2D_DIAG, N_SEMS = 0, 1, 2, 3, 4, 5, 6, 7

    def __init__(self, blocks, part=(0, 1, 1)):
        self.ins = tuple(blocks)
        self.part = part
        self.n_sems = self.N_SEMS * len(blocks)

    def _copy(self, out, send, recv, w, sem, chip_from, cc, to, sub=None):
        hr = self.ins[w].shape[1] // 2
        lo, hi, n = self.part
        first, size = cc * hr + lo * (hr // n), (hi - lo) * (hr // n)
        if sub is not None:
            size //= 2
            first += sub * size
        piece = out[w].at[2 * chip_from[0] + chip_from[1], pl.ds(first, size), :]
        return pltpu.make_async_remote_copy(src_ref=piece, dst_ref=piece, send_sem=send.at[self.N_SEMS * w + sem],
                                            recv_sem=recv.at[self.N_SEMS * w + sem], device_id=to, device_id_type=MESH)

    def _sent(self, out, send, recv, w):
        x, y, c = _place()
        me, sib = (x, y), (x, y, 1 - c)
        xn, yn, diag = _other_chips(x, y)
        cp = functools.partial(self._copy, out, send, recv, w)
        return [("start", cp(self.X_LINK, me, c, (*xn, c))), ("start", cp(self.Y_LINK, me, c, (*yn, c))),
                ("mid_x", cp(self.D2D_X, xn, c, sib)), ("mid_x", cp(self.Y_PASS, xn, c, (*yn, c), sub=0)),
                ("mid_y", cp(self.D2D_Y, yn, c, sib)), ("mid_y", cp(self.X_PASS, yn, c, (*xn, c), sub=1)),
                ("late", cp(self.D2D_DIAG, diag, c, sib))]

    def _go(self, out, send, recv, phase):
        for w in range(len(self.ins)):
            for ph, copy in self._sent(out, send, recv, w):
                if ph == phase:
                    copy.start()

    def start(self, reads, out, new, send, recv):
        self._go(out, send, recv, "start")

    def mid(self, reads, out, new, send, recv):
        x, y, c = _place()
        xn, yn, _ = _other_chips(x, y)
        for w in range(len(self.ins)):
            self._copy(out, send, recv, w, self.X_LINK, xn, c, (x, y, c)).wait_recv()
        self._go(out, send, recv, "mid_x")
        for w in range(len(self.ins)):
            self._copy(out, send, recv, w, self.Y_LINK, yn, c, (x, y, c)).wait_recv()
        self._go(out, send, recv, "mid_y")

    def late(self, reads, out, new, send, recv):
        x, y, c = _place()
        diag = _other_chips(x, y)[2]
        for w in range(len(self.ins)):
            self._copy(out, send, recv, w, self.Y_PASS, diag, c, (x, y, c), sub=0).wait_recv()
            self._copy(out, send, recv, w, self.X_PASS, diag, c, (x, y, c), sub=1).wait_recv()
        self._go(out, send, recv, "late")

    def end(self, reads, out, new, send, recv):
        x, y, c = _place()
        xn, yn, diag = _other_chips(x, y)
        for w in range(len(self.ins)):
            for sem, chip in ((self.D2D_X, xn), (self.D2D_Y, yn), (self.D2D_DIAG, diag)):
                self._copy(out, send, recv, w, sem, chip, 1 - c, (x, y, c)).wait_recv()
        for w in range(len(self.ins)):
            for _, copy in self._sent(out, send, recv, w):
                copy.wait_send()


class _SwapRider(_Rider):
    def __init__(self, grads):
        self.reads = tuple(grads)
        self.new = tuple(_sds((N_CHIPS, g.shape[1] // 2, g.shape[2]), g.dtype) for g in grads)
        self.n_sems = len(grads)

    def _copies(self, src, new, send, recv):
        x, y, c = _place()
        copies = []
        for w in range(len(self.reads)):
            hr = self.reads[w].shape[1] // 2
            copies.append(pltpu.make_async_remote_copy(
                src_ref=src[w].at[:, pl.ds((1 - c) * hr, hr), :], dst_ref=new[w],
                send_sem=send.at[w], recv_sem=recv.at[w], device_id=(x, y, 1 - c), device_id_type=MESH))
        return copies

    def start(self, src, ins, new, send, recv):
        for cp in self._copies(src, new, send, recv):
            cp.start()

    def end(self, src, ins, new, send, recv):
        for cp in self._copies(src, new, send, recv):
            cp.wait()


def _add_half(g, got, c_arr, name):
    nk, r, cols = g.shape
    hr = r // 2
    tr = min(hr, 1024)
    nb = hr // tr

    def body(c_ref, g_ref, got_ref, o_ref):
        o_ref[...] = (g_ref[...].astype(F32) + got_ref[...].astype(F32)).astype(o_ref.dtype)

    grid_spec = pltpu.PrefetchScalarGridSpec(
        num_scalar_prefetch=1, grid=(nk, nb),
        in_specs=[pl.BlockSpec((None, tr, cols), lambda k, i, c_ref: (k, c_ref[0] * nb + i, 0)),
                  pl.BlockSpec((None, tr, cols), lambda k, i, c_ref: (k, i, 0))],
        out_specs=pl.BlockSpec((None, tr, cols), lambda k, i, c_ref: (k, i, 0)))
    return _pallas(body, name=name, grid_spec=grid_spec, out_shape=_sds((nk, hr, cols), g.dtype),
                   compiler_params=_params("parallel", "parallel"))(c_arr, g, got)


class _SendPartialsRider(_Rider):
    def __init__(self, parts, got=None, part=(0, 1, 1)):
        self.reads = tuple(parts)
        if got is None:
            self.new = tuple(_sds((N_CHIPS - 1, *p.shape[1:]), p.dtype) for p in parts)
        else:
            self.ins = tuple(got)
        self.part = part
        self.n_sems = 3 * len(parts)

    def _copies(self, src, ins, new, send, recv):
        x, y, c = _place()
        land = ins if self.ins else new
        lo, hi, n = self.part
        copies = []
        for w in range(len(self.reads)):
            pr = self.reads[w].shape[1] // n
            rows = pl.ds(lo * pr, (hi - lo) * pr)
            for j, chip in enumerate(_other_chips(x, y)):
                copies.append(pltpu.make_async_remote_copy(
                    src_ref=src[w].at[2 * chip[0] + chip[1], rows, :], dst_ref=land[w].at[j, rows, :],
                    send_sem=send.at[3 * w + j], recv_sem=recv.at[3 * w + j], device_id=(*chip, c), device_id_type=MESH))
        return copies

    def start(self, src, ins, new, send, recv):
        for cp in self._copies(src, ins, new, send, recv):
            cp.start()

    def end(self, src, ins, new, send, recv):
        for cp in self._copies(src, ins, new, send, recv):
            cp.wait()


def _sum_partials(part, got, kc_arr, name):
    _, hr, cols = part.shape
    tr = min(hr, 512)
    nb = hr // tr

    def body(kc_ref, p_ref, g0_ref, g1_ref, g2_ref, o_ref):
        o_ref[...] = ((p_ref[...].astype(F32) + g0_ref[...].astype(F32)) + g1_ref[...].astype(F32)) + g2_ref[...].astype(F32)

    slot = lambda j: pl.BlockSpec((None, tr, cols), lambda i, kc_ref: (j, i, 0))
    grid_spec = pltpu.PrefetchScalarGridSpec(
        num_scalar_prefetch=1, grid=(nb,),
        in_specs=[pl.BlockSpec((None, tr, cols), lambda i, kc_ref: (kc_ref[0], i, 0)), slot(0), slot(1), slot(2)],
        out_specs=pl.BlockSpec((tr, cols), lambda i, kc_ref: (kc_ref[1] * nb + i, 0)))
    return _pallas(body, name=name, grid_spec=grid_spec, out_shape=_sds((2 * hr, cols), F32),
                   compiler_params=_params("parallel"))(kc_arr, part, got, got, got)


class _ShareRider(_Rider):
    def __init__(self, grads):
        self.ins = tuple(grads)
        self.n_sems = len(grads)

    def _copies(self, out, send, recv):
        x, y, c = _place()
        copies = []
        for w in range(len(self.ins)):
            hr = self.ins[w].shape[0] // 2
            mine = out[w].at[pl.ds(c * hr, hr), :]
            copies.append(pltpu.make_async_remote_copy(
                src_ref=mine, dst_ref=mine, send_sem=send.at[w], recv_sem=recv.at[w],
                device_id=(x, y, 1 - c), device_id_type=MESH))
        return copies

    def start(self, reads, out, new, send, recv):
        for cp in self._copies(out, send, recv):
            cp.start()

    def end(self, reads, out, new, send, recv):
        for cp in self._copies(out, send, recv):
            cp.wait()


def _small_allreduce_adamw(g_part, w, m, v, riders=()):
    rows = g_part.shape[0]
    rs = _Riders(riders, 4, 4)
    n_rin, n_rout = len(rs.arrays), len(rs.out_shapes)

    def body(*refs):
        g_ref, w_ref, m_ref, v_ref = refs[:4]
        go_ref, d_ref, nm_ref, nv_ref = refs[4 + n_rin:8 + n_rin]
        all_ref, send_sems, recv_sems = refs[8 + n_rin + n_rout:11 + n_rin + n_rout]
        bound = rs.bind(refs[4:4 + n_rin], refs[8 + n_rin:8 + n_rin + n_rout], refs[11 + n_rin + n_rout:])
        rs.run("start", bound)
        x, y, c = _place()
        me = 4 * x + 2 * y + c
        all_ref[me] = g_ref[...]
        copies = []
        for r in range(1, 8):
            dx, dy, dc = (r >> 2) & 1, (r >> 1) & 1, r & 1
            peer = (1 - x if dx else x, 1 - y if dy else y, 1 - c if dc else c)
            copies.append(pltpu.make_async_remote_copy(
                src_ref=g_ref, dst_ref=all_ref.at[me], send_sem=send_sems.at[r - 1], recv_sem=recv_sems.at[r - 1],
                device_id=peer, device_id_type=MESH))
        for cp in copies:
            cp.start()
        for cp in copies:
            cp.wait()
        tot = all_ref[0]
        for d in range(1, 8):
            tot = tot + all_ref[d]
        go_ref[...] = tot
        d_ref[...], nm_ref[...], nv_ref[...] = _adamw_math(w_ref[...], tot, m_ref[...], v_ref[...])
        for phase in ("mid", "late", "end"):
            rs.run(phase, bound)

    vm = pl.BlockSpec(memory_space=pltpu.VMEM)
    out = _pallas(
        body, name="small_allreduce_adamw", in_specs=[vm] * 4 + rs.in_specs, out_specs=[vm] * 4 + rs.out_specs,
        out_shape=[_sds((rows, 128), F32)] * 4 + rs.out_shapes, input_output_aliases=rs.aliases,
        scratch_shapes=[pltpu.VMEM((8, rows, 128), F32), pltpu.SemaphoreType.DMA((7,)), pltpu.SemaphoreType.DMA((7,)),
                        *rs.scratch],
    )(g_part, w, m, v, *rs.arrays)
    return out[:4], rs.split(out[4:])


SMALL_SIZES = (2048, 1024, 128, 128, HEADS * REL_SIZE, 2048)
SMALL_PART_ROWS = tuple(-(-size // 1024) * 8 for size in SMALL_SIZES)
SMALL_ROWS = sum(SMALL_PART_ROWS)


def _pack_small(parts):
    rows = []
    for p, size, nr in zip(parts, SMALL_SIZES, SMALL_PART_ROWS):
        rows.append(jnp.pad(p.reshape(-1), (0, nr * 128 - size)).reshape(nr, 128))
    return jnp.concatenate(rows, axis=0)


def _unpack_small(slab, shapes):
    out, off = [], 0
    for size, nr, shape in zip(SMALL_SIZES, SMALL_PART_ROWS, shapes):
        out.append(slab[off:off + nr].reshape(-1)[:size].reshape(shape))
        off += nr
    return out


def kernel(x, norm1_g, w_in, ret_norm_g, q_norm_g, k_norm_g, rel_bias, w_out, norm2_g, w_ff1, w_ff2, loss_target, m_norm1_g, m_w_in, m_ret_norm_g, m_q_norm_g, m_k_norm_g, m_rel_bias, m_w_out, m_norm2_g, m_w_ff1, m_w_ff2, v_norm1_g, v_w_in, v_ret_norm_g, v_q_norm_g, v_k_norm_g, v_rel_bias, v_w_out, v_norm2_g, v_w_ff1, v_w_ff2):
    xs = x[0]
    tgt = loss_target[0]
    s, d = xs.shape
    d_in = N_CHIPS * w_in.shape[2]
    d_ff = N_CHIPS * w_ff1.shape[2]
    in_sh, ff_sh = w_in.shape[2], w_ff1.shape[2]
    tm = min(s, 1024)
    gi = s // tm
    c_arr = lax.axis_index("c").astype(jnp.int32).reshape(1)
    k_arr = (2 * lax.axis_index("x") + lax.axis_index("y")).astype(jnp.int32).reshape(1)
    tables = _tables(s)
    bias = _rel_bias_expand(rel_bias[0][:, None, :])

    blk_in = [_cast_bf16(w_in[0], k_arr, "cast_w_in_%d" % half, cols=(half, 2)) for half in range(2)]
    blk_out, blk_ff1, blk_ff2 = (_cast_bf16(w_out[0], k_arr, "cast_w_out"), _cast_bf16(w_ff1[0], k_arr, "cast_w_ff1"),
                                 _cast_bf16(w_ff2[0], k_arr, "cast_w_ff2"))

    h1, ((wg_in0,),) = _rmsnorm_fwd(xs, norm1_g, "rmsnorm1", riders=[_GatherRider([blk_in[0]])])
    tn_in = in_sh // 2
    tk = d

    def proj_half(half, wg, through, riders):
        return _mm("proj_%d" % half, h1, wg, NN, (gi, N_CHIPS, 1),
                   pl.BlockSpec((tm, tk), lambda i, j, k: (i, 0)), pl.BlockSpec((None, tk, tn_in), lambda i, j, k: (j, 0, 0)),
                   [_sds((s, d_in), F32)], [pl.BlockSpec((tm, tn_in), lambda i, j, k: (i, 2 * j + half))], (tm, tn_in),
                   riders=riders, through=through)

    (proj,), ((wg_in1,),) = proj_half(0, wg_in0, None, [_GatherRider([blk_in[1]])])
    (proj,), ((wg_ff1,),) = proj_half(1, wg_in1, proj, [_GatherRider([blk_ff1], (0, 3, 8))])
    (mix, y_ret, prev), ((wg_ff1,),) = _retention_fwd(proj, ret_norm_g, tables, riders=[_GatherRider([wg_ff1], (3, 6, 8))])
    (mix, att_kp, att_vp), ((wg_out,), (wg_ff2,)) = _attention_fwd(
        proj, q_norm_g, k_norm_g, bias, mix, riders=[_GatherRider([blk_out]), _GatherRider([blk_ff2], (0, 1, 4))])
    wg_out = wg_out.reshape(d, d)
    tn = 1024
    tile = pl.BlockSpec((tm, tn), lambda i, j, k: (i, j))
    def residual_norm(acc, res, g):
        x1v = res + acc
        yv = x1v * lax.rsqrt(jnp.mean(x1v * x1v, axis=-1, keepdims=True) + EPS)
        return x1v, yv * g

    tmo = min(s, 512)
    rows = pl.BlockSpec((tmo, d), lambda i, j, k: (i, 0))
    (x1, h2), ((wg_ff1,),) = _mm(
        "out_proj", mix, wg_out, NN, (s // tmo, 1, 1),
        rows, pl.BlockSpec((d, d), lambda i, j, k: (0, 0)),
        [_sds((s, d), F32), _sds((s, d), BF16)], [rows, rows], (tmo, d),
        extras=(xs, norm2_g), extra_specs=(rows, pl.BlockSpec((1, d), lambda i, j, k: (0, 0))),
        epi=residual_norm, riders=[_GatherRider([wg_ff1], (6, 8, 8))])
    tn_ff = min(ff_sh, 1024)
    per = ff_sh // tn_ff

    def relu2(acc):
        r = jnp.maximum(acc, 0.0)
        return acc, r * r

    (u, act), ((wg_ff2,),) = _mm(
        "ff1", h2, wg_ff1, NN, (gi, N_CHIPS * per, d // tk),
        pl.BlockSpec((tm, tk), lambda i, j, k: (i, k)),
        pl.BlockSpec((None, tk, tn_ff), lambda i, j, k: (j // per, k, j % per)),
        [_sds((s, d_ff), F32), _sds((s, d_ff), BF16)],
        [pl.BlockSpec((tm, tn_ff), lambda i, j, k: (i, j))] * 2, (tm, tn_ff), epi=relu2,
        riders=[_GatherRider([wg_ff2], (1, 4, 4))])
    wg_ff2 = wg_ff2.reshape(d_ff, d)

    def loss_epi(acc, res, t):
        diff = (res + acc) - t
        dy = diff / d
        return dy, dy, jnp.sum(diff * diff, axis=0, keepdims=True)

    tk2 = min(tk, 2048)
    dy, dyb, loss_cols = _mm(
        "ff2_loss", act, wg_ff2, NN, (gi, d // tn, d_ff // tk2),
        pl.BlockSpec((tm, tk2), lambda i, j, k: (i, k)), pl.BlockSpec((tk2, tn), lambda i, j, k: (k, j)),
        [_sds((s, d), F32), _sds((s, d), BF16), _sds((gi, 1, d), F32)],
        [tile, tile, pl.BlockSpec((None, 1, tn), lambda i, j, k: (i, 0, j))], (tm, tn),
        extras=(x1, tgt), extra_specs=(tile, tile), epi=loss_epi)
    loss = lax.psum(0.5 * jnp.sum(loss_cols) / d, ("x", "y", "c"))

    (du,) = _mm("d_act", dyb, wg_ff2, NT, (gi, d_ff // tn, d // tk),
                pl.BlockSpec((tm, tk), lambda i, j, k: (i, k)), pl.BlockSpec((tn, tk), lambda i, j, k: (j, k)),
                [_sds((s, d_ff), BF16)], [tile], (tm, tn), extras=(u,), extra_specs=(tile,),
                epi=lambda acc, uu: (acc * (2.0 * jnp.maximum(uu, 0.0)),))
    ts = min(s, 2048)
    wtile = pl.BlockSpec((tn, tn), lambda i, j, k: (i, j))
    (g_ff2,) = _mm("dw_ff2", act, dyb, TN, (d_ff // tn, d // tn, s // ts),
                   pl.BlockSpec((ts, tn), lambda i, j, k: (k, i)), pl.BlockSpec((ts, tn), lambda i, j, k: (k, j)),
                   [_sds((d_ff, d), BF16)], [wtile], (tn, tn))
    g_ff2 = g_ff2.reshape(N_CHIPS, d_ff // N_CHIPS, d)
    (g_ff1,), ((got_ff2,),) = _mm(
        "dw_ff1", h2, du, TN, (d // tn, N_CHIPS * per, s // ts),
        pl.BlockSpec((ts, tn), lambda i, j, k: (k, i)), pl.BlockSpec((ts, tn_ff), lambda i, j, k: (k, j)),
        [_sds((N_CHIPS, d, ff_sh), BF16)],
        [pl.BlockSpec((None, tn, tn_ff), lambda i, j, k: (j // per, i, j % per))], (tn, tn_ff),
        riders=[_SwapRider([g_ff2])])
    p_ff2 = _add_half(g_ff2, got_ff2, c_arr, "chip_partial_w_ff2")
    tkf = min(tk, ff_sh)
    kper = ff_sh // tkf
    (dh2,), ((got2_ff2,), (got_ff1,)) = _mm(
        "d_h2", du, wg_ff1, NT, (gi, d // tn, d_ff // tkf),
        pl.BlockSpec((tm, tkf), lambda i, j, k: (i, k)),
        pl.BlockSpec((None, tn, tkf), lambda i, j, k: (k // kper, j, k % kper)),
        [_sds((s, d), F32)], [tile], (tm, tn),
        riders=[_SendPartialsRider([p_ff2], part=(0, 3, 4)), _SwapRider([g_ff1])])
    p_ff1 = _add_half(g_ff1, got_ff1, c_arr, "chip_partial_w_ff1")
    dx1, dx1b, g_norm2 = _rmsnorm_bwd(x1, norm2_g, dh2, dy, "rmsnorm2_bwd")

    (dmix,) = _mm("d_mix", dx1b, wg_out, NT, (gi, d // tn, d // tk),
                  pl.BlockSpec((tm, tk), lambda i, j, k: (i, k)), pl.BlockSpec((tn, tk), lambda i, j, k: (j, k)),
                  [_sds((s, d), F32)], [tile], (tm, tn))
    (g_out,) = _mm("dw_out", mix, dx1b, TN, (d // tn, d // tn, s // ts),
                   pl.BlockSpec((ts, tn), lambda i, j, k: (k, i)), pl.BlockSpec((ts, tn), lambda i, j, k: (k, j)),
                   [_sds((d, d), BF16)], [wtile], (tn, tn))
    g_out = g_out.reshape(N_CHIPS, d // N_CHIPS, d)
    (dproj, g_gn), ((got2_ff2,), (got2_ff1,), (got_out,)) = _retention_bwd(
        proj, ret_norm_g, tables, y_ret, prev, dmix,
        riders=[_SendPartialsRider([p_ff2], got=[got2_ff2], part=(3, 4, 4)), _SendPartialsRider([p_ff1], part=(0, 2, 4)),
                _SwapRider([g_out])])
    p_out = _add_half(g_out, got_out, c_arr, "chip_partial_w_out")
    (dproj, g_gq, g_gk, dbias), ((got2_ff1,), (got2_out,)) = _attention_bwd(
        proj, q_norm_g, k_norm_g, bias, dmix, dproj, att_kp, att_vp,
        riders=[_SendPartialsRider([p_ff1], got=[got2_ff1], part=(2, 4, 4)), _SendPartialsRider([p_out])])
    g_rel = _rel_bias_fold(dbias)
    names = ["w_in", "w_out", "w_ff1", "w_ff2"]
    kc_arr = jnp.concatenate([k_arr, c_arr])
    early = [_sum_partials(p, r, kc_arr, "sum_partials_" + nm)
             for p, r, nm in zip((p_out, p_ff1, p_ff2), (got2_out, got2_ff1, got2_ff2), names[1:])]
    (g_in,), (early,) = _mm(
        "dw_in", h1, dproj, TN, (d // tn, 2 * N_CHIPS, s // ts),
        pl.BlockSpec((ts, tn), lambda i, j, k: (k, i)), pl.BlockSpec((ts, tn_in), lambda i, j, k: (k, j)),
        [_sds((N_CHIPS, d, in_sh), BF16)],
        [pl.BlockSpec((None, tn, tn_in), lambda i, j, k: (j // 2, i, j % 2))], (tn, tn_in), riders=[_ShareRider(early)])
    ((got_in,),) = _run_riders("grad_swap_w_in", [_SwapRider([g_in])])
    p_in = _add_half(g_in, got_in, c_arr, "chip_partial_w_in")
    half_spec = pl.BlockSpec((None, tn, tn_in), lambda i, j, k: (k // 2, j, 0))
    (dh1,), ((got2_in,),) = _mm(
        "d_h1", dproj, [wg_in0, wg_in1], NT, (gi, d // tn, 2 * N_CHIPS),
        pl.BlockSpec((tm, tn_in), lambda i, j, k: (i, k)), [half_spec, half_spec],
        [_sds((s, d), F32)], [tile], (tm, tn), riders=[_SendPartialsRider([p_in])])
    grad_x, _, g_norm1 = _rmsnorm_bwd(xs, norm1_g, dh1, dx1, "rmsnorm1_bwd")

    small_w = (norm1_g, ret_norm_g, q_norm_g, k_norm_g, rel_bias, norm2_g)
    small_m = (m_norm1_g, m_ret_norm_g, m_q_norm_g, m_k_norm_g, m_rel_bias, m_norm2_g)
    small_v = (v_norm1_g, v_ret_norm_g, v_q_norm_g, v_k_norm_g, v_rel_bias, v_norm2_g)
    shapes = [p.shape for p in small_w]
    g_small = _pack_small([g_norm1, g_gn, g_gq, g_gk, g_rel, g_norm2])
    small_out, ((g_w_in,),) = _small_allreduce_adamw(
        g_small, _pack_small(small_w), _pack_small(small_m), _pack_small(small_v),
        riders=[_ShareRider([_sum_partials(p_in, got2_in, kc_arr, "sum_partials_w_in")])])
    sg, sd, sm, sv = (_unpack_small(a, shapes) for a in small_out)

    g_big = [g_w_in, *early]
    big = []
    for g, w, m, v, nm in zip(g_big, (w_in, w_out, w_ff1, w_ff2), (m_w_in, m_w_out, m_w_ff1, m_w_ff2),
                              (v_w_in, v_w_out, v_w_ff1, v_w_ff2), names):
        g, delta, new_m, new_v = _adamw(w[0], g, m[0], v[0], "adamw_" + nm)
        big.append((g[None], delta[None], new_m[None], new_v[None]))

    def ordered(kind):
        sm_ = (sg, sd, sm, sv)[kind]
        return (sm_[0], big[0][kind], sm_[1], sm_[2], sm_[3], sm_[4], big[1][kind], sm_[5], big[2][kind], big[3][kind])

    return (loss, grad_x[None], *ordered(0), *ordered(1), *ordered(2), *ordered(3))
```
